```python
import jax, jax.numpy as jnp
from jax import lax
import numpy as np

D_MODEL = 2048
BATCH = 4
SEQ = 4096
DEPTH = 2

GRID_W = 64
CTX_LEN = 256
N_GROUPS = 4
GROUP_WIDTH = D_MODEL // N_GROUPS
HEAD_DIM = 128
N_HEADS = GROUP_WIDTH // HEAD_DIM
CHUNK = 64
GDN_CONV = 5
NA_ROWS = 8
NA_COLS = 16
MLA_Q_RANK = 384
MLA_KV_RANK = 256
MLA_NOPE = 128
MLA_ROPE = 64
MLA_V = 128
MLA_QK_DIM = MLA_NOPE + MLA_ROPE
ROPE_BASE = 10000.0
HGRN_FDIM = 128
Q_BLOCK = 128
MLP_HIDDEN = 4 * D_MODEL
ALPHA = (2 * DEPTH) ** 0.25
BETA_INIT = (8 * DEPTH) ** -0.25
LN_EPS = 1e-5
RMS_EPS = 1e-6
ADA_SCALE = 0.5

GDN_IN = 4 * GROUP_WIDTH + 4 * N_HEADS
NA_IN = 3 * GROUP_WIDTH
MLA_IN = MLA_Q_RANK + MLA_KV_RANK + MLA_ROPE
HGRN_IN = 5 * GROUP_WIDTH
IN_COLS = GDN_IN + NA_IN + MLA_IN + HGRN_IN

kernel_name = 'hybrid_parallel_heads_diffusion_block'

F32 = jnp.float32


def _split(z, sizes):
    return jnp.split(z, [int(s) for s in np.cumsum(sizes)[:-1]], axis=-1)


def _heads(t, d):
    B, L, _ = t.shape
    return t.reshape(B, L, -1, d).transpose(0, 2, 1, 3)


def _merge(t):
    B, H, L, d = t.shape
    return t.transpose(0, 2, 1, 3).reshape(B, L, H * d)


def _layernorm(x, w, b):
    xf = x.astype(F32)
    mu = jnp.mean(xf, -1, keepdims=True)
    var = jnp.mean(jnp.square(xf - mu), -1, keepdims=True)
    return ((xf - mu) * lax.rsqrt(var + LN_EPS) * w + b).astype(x.dtype)


def _rmsnorm(x, w):
    xf = x.astype(F32)
    return (xf * lax.rsqrt(jnp.mean(xf * xf, -1, keepdims=True) + RMS_EPS) * w).astype(x.dtype)


def _l2norm(t):
    t = t.astype(F32)
    return t * lax.rsqrt(jnp.sum(t * t, -1, keepdims=True) + RMS_EPS)


def _gated_head_norm(o, gate, w, dtype):
    o = o * lax.rsqrt(jnp.mean(o * o, -1, keepdims=True) + RMS_EPS) * w
    return (_merge(o) * jax.nn.silu(gate.astype(F32))).astype(dtype)


def _short_conv(x, w):
    K = w.shape[0]
    P = K // 2
    L = x.shape[1]
    xp = jnp.pad(x, ((0, 0), (P, P), (0, 0)))
    y = xp[:, 0:L] * w[0]
    for i in range(1, K):
        y = y + xp[:, i:i + L] * w[i]
    return jax.nn.silu(y)


def _to_chunks(t):
    B, H, L = t.shape[:3]
    return jnp.moveaxis(t.reshape(B, H, L // CHUNK, CHUNK, *t.shape[3:]), 2, 0)


def _from_chunks(o):
    n, B, H, C, d = o.shape
    return jnp.moveaxis(o, 0, 2).reshape(B, H, n * C, d)


def _gdn_chunk_scan(q, k, v, g, beta, state, emit):
    dv = v.shape[-1]
    incl = jnp.tril(jnp.ones((CHUNK, CHUNK), dtype=bool))
    strict = jnp.tril(jnp.ones((CHUNK, CHUNK), dtype=bool), -1)
    eye = jnp.eye(CHUNK, dtype=F32)

    def step(S, inp):
        qc, kc, vc, gc, bc = inp
        cum = jnp.cumsum(gc, axis=-1)
        decay = jnp.exp(jnp.where(incl, cum[..., :, None] - cum[..., None, :], -jnp.inf))
        kb = kc * bc[..., None]
        a = jnp.where(strict, jnp.einsum('bhtd,bhsd->bhts', kb, kc) * decay, 0.0)
        rhs = jnp.concatenate([vc * bc[..., None], kb * jnp.exp(cum)[..., None]], -1)
        sol = lax.linalg.triangular_solve(eye + a, rhs, left_side=True, lower=True, unit_diagonal=True)
        u, w = sol[..., :dv], sol[..., dv:]
        v_new = u - jnp.einsum('bhtd,bhdv->bhtv', w, S)
        last = cum[..., -1:]
        S_new = S * jnp.exp(last)[..., None] + jnp.einsum(
            'bhsd,bhsv->bhdv', kc * jnp.exp(last - cum)[..., None], v_new)
        if not emit:
            return S_new, None
        attn = jnp.where(incl, jnp.einsum('bhtd,bhsd->bhts', qc, kc) * decay, 0.0)
        o = jnp.einsum('bhtd,bhdv->bhtv', qc * jnp.exp(cum)[..., None], S) + jnp.einsum('bhts,bhsv->bhtv', attn, v_new)
        return S_new, o

    S, o = lax.scan(step, state, (_to_chunks(q), _to_chunks(k), _to_chunks(v), _to_chunks(g), _to_chunks(beta)))
    return S, (_from_chunks(o) if emit else None)


def _gla_chunk_scan(q, k, v, logf, state, emit):
    incl = jnp.tril(jnp.ones((CHUNK, CHUNK), dtype=bool))

    def step(S, inp):
        qc, kc, vc, gc = inp
        cum = jnp.cumsum(gc, axis=2)
        last = cum[:, :, -1]
        S_new = S * jnp.exp(last)[..., None] + jnp.einsum(
            'bhsd,bhsv->bhdv', kc * jnp.exp(last[:, :, None] - cum), vc)
        if not emit:
            return S_new, None
        diff = cum[:, :, :, None, :] - cum[:, :, None, :, :]
        decay = jnp.exp(jnp.where(incl[:, :, None], diff, -jnp.inf))
        scores = jnp.einsum('bhtd,bhsd,bhtsd->bhts', qc, kc, decay)
        o = jnp.einsum('bhts,bhsv->bhtv', scores, vc) + jnp.einsum('bhtd,bhdv->bhtv', qc * jnp.exp(cum), S)
        return S_new, o

    S, o = lax.scan(step, state, (_to_chunks(q), _to_chunks(k), _to_chunks(v), _to_chunks(logf)))
    return S, (_from_chunks(o) if emit else None)


def _bidirectional_prefix_scan(scan, ctx_f, ctx_b, lat_f, lat_b, state0, emit_ctx):
    def flip(ts):
        return tuple(jnp.flip(t, 2) for t in ts)
    s_f, oc_f = scan(*ctx_f, state0, emit_ctx)
    s_b, oc_b = scan(*flip(ctx_b), state0, emit_ctx)
    _, ol_f = scan(*lat_f, s_f, True)
    _, ol_b = scan(*flip(lat_b), s_b, True)
    o_lat = ol_f + jnp.flip(ol_b, 2)
    o_ctx = (oc_f + jnp.flip(oc_b, 2)) if emit_ctx else None
    return o_lat, o_ctx


def _gdn_mixer(zl, zc, conv_w, a_log, dt_bias, norm_w, emit_ctx):
    def prep(z):
        B, L, _ = z.shape
        qkv, gate, a, b = _split(z, [3 * GROUP_WIDTH, GROUP_WIDTH, 2 * N_HEADS, 2 * N_HEADS])
        q, k, v = jnp.split(_short_conv(qkv, conv_w), 3, axis=-1)
        q = _l2norm(_heads(q, HEAD_DIM)) * HEAD_DIM ** -0.5
        k = _l2norm(_heads(k, HEAD_DIM))
        v = _heads(v, HEAD_DIM).astype(F32)
        a = a.reshape(B, L, 2, N_HEADS).astype(F32)
        b = b.reshape(B, L, 2, N_HEADS).astype(F32)
        g = -jnp.exp(a_log.astype(F32)) * jax.nn.softplus(a + dt_bias.astype(F32))
        beta = jax.nn.sigmoid(b)
        g = jnp.transpose(g, (2, 0, 3, 1))
        beta = jnp.transpose(beta, (2, 0, 3, 1))
        return q, k, v, g, beta, gate

    qc, kc, vc, gc, bc, gate_c = prep(zc)
    ql, kl, vl, gl, bl, gate_l = prep(zl)
    B = zl.shape[0]
    s0 = jnp.zeros((B, N_HEADS, HEAD_DIM, HEAD_DIM), F32)
    o_lat, o_ctx = _bidirectional_prefix_scan(
        _gdn_chunk_scan,
        (qc, kc, vc, gc[0], bc[0]), (qc, kc, vc, gc[1], bc[1]),
        (ql, kl, vl, gl[0], bl[0]), (ql, kl, vl, gl[1], bl[1]), s0, emit_ctx)
    nw = norm_w.astype(F32)
    y_lat = _gated_head_norm(o_lat, gate_l, nw, zl.dtype)
    y_ctx = _gated_head_norm(o_ctx, gate_c, nw, zc.dtype) if emit_ctx else None
    return y_lat, y_ctx


def _blocked_attention(q, k, v, scale):
    B, H, S, dq = q.shape
    nb = S // Q_BLOCK
    qb = jnp.moveaxis(q.reshape(B, H, nb, Q_BLOCK, dq), 2, 0)

    def blk(qi):
        s = jnp.einsum('bhqd,bhkd->bhqk', qi, k).astype(F32) * scale
        p = jax.nn.softmax(s, axis=-1).astype(v.dtype)
        return jnp.einsum('bhqk,bhkd->bhqd', p, v)

    o = lax.map(blk, qb)
    return jnp.moveaxis(o, 0, 2).reshape(B, H, S, v.shape[-1])


def _neighbourhood_attention(q, k, v, kc, vc, rpb):
    B, H, S, dh = q.shape
    rows = S // GRID_W
    kr = min(NA_ROWS, rows)
    scale = dh ** -0.5
    qg = q.reshape(B, H, rows, GRID_W, dh)
    kg = k.reshape(B, H, rows, GRID_W, dh)
    vg = v.reshape(B, H, rows, GRID_W, dh)
    r = jnp.arange(rows)
    row_idx = jnp.clip(r - kr // 2, 0, rows - kr)[:, None] + jnp.arange(kr)[None, :]
    kw = kg[:, :, row_idx]
    vw = vg[:, :, row_idx]
    col = jnp.arange(GRID_W)
    c0 = jnp.clip(col - NA_COLS // 2, 0, GRID_W - NA_COLS)
    col_ok = (col[None, :] >= c0[:, None]) & (col[None, :] < c0[:, None] + NA_COLS)
    dr = row_idx - r[:, None] + NA_ROWS - 1
    dc = jnp.clip(col[None, :] - col[:, None], 1 - NA_COLS, NA_COLS - 1) + NA_COLS - 1
    bias = rpb[:, dr[:, None, :, None], dc[None, :, None, :]].astype(F32)
    s_win = jnp.einsum('bhrqd,bhrkwd->bhrqkw', qg, kw).astype(F32) * scale + bias
    s_win = jnp.where(col_ok[:, None, :], s_win, -jnp.inf)
    s_ctx = jnp.einsum('bhrqd,bhcd->bhrqc', qg, kc).astype(F32) * scale
    n_win = kr * GRID_W
    p = jax.nn.softmax(jnp.concatenate([s_win.reshape(B, H, rows, GRID_W, n_win), s_ctx], -1), axis=-1)
    p = p.astype(v.dtype)
    o = jnp.einsum('bhrqkw,bhrkwd->bhrqd', p[..., :n_win].reshape(s_win.shape), vw) + jnp.einsum(
        'bhrqc,bhcd->bhrqd', p[..., n_win:], vc)
    return o.reshape(B, H, S, dh)


def _na_mixer(zl, zc, rpb, emit_ctx):
    ql, kl, vl = [_heads(t, HEAD_DIM) for t in jnp.split(zl, 3, axis=-1)]
    qc, kc, vc = [_heads(t, HEAD_DIM) for t in jnp.split(zc, 3, axis=-1)]
    y_lat = _merge(_neighbourhood_attention(ql, kl, vl, kc, vc, rpb))
    y_ctx = _merge(_blocked_attention(qc, kc, vc, HEAD_DIM ** -0.5)) if emit_ctx else None
    return y_lat, y_ctx


def _axial_rope_tables(seq):
    n_freq = MLA_ROPE // 4
    freqs = ROPE_BASE ** (-jnp.arange(n_freq, dtype=F32) / n_freq)
    t = jnp.arange(seq)
    ang = jnp.concatenate([(t // GRID_W).astype(F32)[:, None] * freqs,
                           (t % GRID_W).astype(F32)[:, None] * freqs], -1)
    return jnp.cos(ang), jnp.sin(ang)


def _apply_rope(x, cos, sin):
    x1, x2 = x[..., 0::2], x[..., 1::2]
    out = jnp.stack([x1 * cos - x2 * sin, x1 * sin + x2 * cos], -1)
    return out.reshape(x.shape).astype(x.dtype)


def _mla_queries(z, q_norm_w, w_uq, rope):
    q = _heads(_rmsnorm(z[..., :MLA_Q_RANK], q_norm_w) @ w_uq, MLA_QK_DIM)
    if rope is not None:
        q = jnp.concatenate([q[..., :MLA_NOPE], _apply_rope(q[..., MLA_NOPE:], *rope)], -1)
    return q


def _mla_keys_values(z, kv_norm_w, w_uk, w_uv, rope):
    ckv = _rmsnorm(z[..., MLA_Q_RANK:MLA_Q_RANK + MLA_KV_RANK], kv_norm_w)
    k_rope = z[..., MLA_Q_RANK + MLA_KV_RANK:][:, None]
    if rope is not None:
        k_rope = _apply_rope(k_rope, *rope)
    k_nope = _heads(ckv @ w_uk, MLA_NOPE)
    v = _heads(ckv @ w_uv, MLA_V)
    k = jnp.concatenate([k_nope, jnp.broadcast_to(k_rope, k_nope.shape[:3] + (MLA_ROPE,))], -1)
    return k, v


def _mla_mixer(zl, zc, q_norm_w, kv_norm_w, w_uq, w_uk, w_uv, rope, emit_ctx):
    scale = MLA_QK_DIM ** -0.5
    kc, vc = _mla_keys_values(zc, kv_norm_w, w_uk, w_uv, None)
    kl, vl = _mla_keys_values(zl, kv_norm_w, w_uk, w_uv, rope)
    ql = _mla_queries(zl, q_norm_w, w_uq, rope)
    y_lat = _merge(_blocked_attention(ql, jnp.concatenate([kl, kc], 2), jnp.concatenate([vl, vc], 2), scale))
    y_ctx = None
    if emit_ctx:
        y_ctx = _merge(_blocked_attention(_mla_queries(zc, q_norm_w, w_uq, None), kc, vc, scale))
    return y_lat, y_ctx


def _hgrn_mixer(zl, zc, lb, norm_w, emit_ctx):
    lb = lb.astype(F32).reshape(N_HEADS, 1, HGRN_FDIM)

    def prep(z):
        q, f_f, f_b, i, gate = jnp.split(z, 5, axis=-1)
        q = _heads(jax.nn.silu(q), HGRN_FDIM).astype(F32) * HGRN_FDIM ** -0.5
        i = _heads(i, GROUP_WIDTH // N_HEADS).astype(F32)

        def gates(fz):
            f = lb + (1.0 - lb) * jax.nn.sigmoid(_heads(fz, HGRN_FDIM).astype(F32))
            return 1.0 - f, jnp.log(f)
        k_f, g_f = gates(f_f)
        k_b, g_b = gates(f_b)
        return q, i, k_f, g_f, k_b, g_b, gate

    qc, ic, kcf, gcf, kcb, gcb, gate_c = prep(zc)
    ql, il, klf, glf, klb, glb, gate_l = prep(zl)
    B = zl.shape[0]
    s0 = jnp.zeros((B, N_HEADS, HGRN_FDIM, GROUP_WIDTH // N_HEADS), F32)
    o_lat, o_ctx = _bidirectional_prefix_scan(
        _gla_chunk_scan,
        (qc, kcf, ic, gcf), (qc, kcb, ic, gcb),
        (ql, klf, il, glf), (ql, klb, il, glb), s0, emit_ctx)
    nw = norm_w.astype(F32)
    y_lat = _gated_head_norm(o_lat, gate_l, nw, zl.dtype)
    y_ctx = _gated_head_norm(o_ctx, gate_c, nw, zc.dtype) if emit_ctx else None
    return y_lat, y_ctx


def _sqrelu_mlp(h, w1, w2):
    return jnp.square(jax.nn.relu(h @ w1)) @ w2


def setup_inputs(seed: int = 0) -> dict:
    key = jax.random.key(seed)
    ks = jax.random.split(key, 26)
    nrm = jax.random.normal
    L = DEPTH
    dt = jnp.exp(jax.random.uniform(ks[9], (L, 2, N_HEADS)) * (np.log(0.1) - np.log(0.001)) + np.log(0.001))
    return {
        'x': nrm(ks[0], (BATCH, SEQ, D_MODEL), F32),
        'c': nrm(ks[1], (BATCH, D_MODEL), F32),
        'ctx': nrm(ks[2], (BATCH, CTX_LEN, D_MODEL), F32),
        'c_ctx': nrm(ks[3], (D_MODEL,), F32),
        'w_ada': nrm(ks[4], (L, D_MODEL, 6 * D_MODEL), F32) * (ADA_SCALE * D_MODEL ** -0.5),
        'b_ada': nrm(ks[5], (L, 6 * D_MODEL), F32) * 0.02,
        'w_in': nrm(ks[6], (L, D_MODEL, IN_COLS), F32) * D_MODEL ** -0.5,
        'gdn_conv_w': nrm(ks[7], (L, GDN_CONV, 3 * GROUP_WIDTH), F32) * GDN_CONV ** -0.5,
        'gdn_a_log': jnp.log(jax.random.uniform(ks[8], (L, 2, N_HEADS), F32, 1.0, 16.0)),
        'gdn_dt_bias': (dt + jnp.log(-jnp.expm1(-dt))).astype(F32),
        'gdn_norm_w': 1.0 + 0.1 * nrm(ks[10], (L, HEAD_DIM), F32),
        'na_rpb': 0.1 * nrm(ks[11], (L, N_HEADS, 2 * NA_ROWS - 1, 2 * NA_COLS - 1), F32),
        'mla_q_norm_w': 1.0 + 0.1 * nrm(ks[12], (L, MLA_Q_RANK), F32),
        'mla_kv_norm_w': 1.0 + 0.1 * nrm(ks[13], (L, MLA_KV_RANK), F32),
        'mla_w_uq': nrm(ks[14], (L, MLA_Q_RANK, N_HEADS * MLA_QK_DIM), F32) * MLA_Q_RANK ** -0.5,
        'mla_w_uk': nrm(ks[15], (L, MLA_KV_RANK, N_HEADS * MLA_NOPE), F32) * MLA_KV_RANK ** -0.5,
        'mla_w_uv': nrm(ks[16], (L, MLA_KV_RANK, N_HEADS * MLA_V), F32) * MLA_KV_RANK ** -0.5,
        'hgrn_lower_bounds': 0.5 * nrm(ks[17], (L, GROUP_WIDTH), F32),
        'hgrn_norm_w': 1.0 + 0.1 * nrm(ks[18], (L, GROUP_WIDTH // N_HEADS), F32),
        'w_out': nrm(ks[19], (L, D_MODEL, D_MODEL), F32) * (BETA_INIT * D_MODEL ** -0.5),
        'ln1_w': 1.0 + 0.1 * nrm(ks[20], (L, D_MODEL), F32),
        'ln1_b': 0.02 * nrm(ks[21], (L, D_MODEL), F32),
        'w_mlp1': nrm(ks[22], (L, D_MODEL, MLP_HIDDEN), F32) * D_MODEL ** -0.5,
        'w_mlp2': nrm(ks[23], (L, MLP_HIDDEN, D_MODEL), F32) * (BETA_INIT * MLP_HIDDEN ** -0.5),
        'ln2_w': 1.0 + 0.1 * nrm(ks[24], (L, D_MODEL), F32),
        'ln2_b': 0.02 * nrm(ks[25], (L, D_MODEL), F32),
    }


def reference(x, c, ctx, c_ctx, w_ada, b_ada, w_in, gdn_conv_w, gdn_a_log, gdn_dt_bias, gdn_norm_w,
              na_rpb, mla_q_norm_w, mla_kv_norm_w, mla_w_uq, mla_w_uk, mla_w_uv, hgrn_lower_bounds,
              hgrn_norm_w, w_out, ln1_w, ln1_b, w_mlp1, w_mlp2, ln2_w, ln2_b):
    rope = _axial_rope_tables(x.shape[1])
    p_lb = jax.nn.softmax(hgrn_lower_bounds.astype(F32), axis=0)
    lbs = jnp.cumsum(p_lb, axis=0) - p_lb[0]
    xc = ctx
    for l in range(DEPTH):
        emit_ctx = l < DEPTH - 1
        ada = (jax.nn.silu(c) @ w_ada[l] + b_ada[l])[:, None, :]
        ada_c = jax.nn.silu(c_ctx) @ w_ada[l] + b_ada[l]
        sh1, sc1, g1, sh2, sc2, g2 = jnp.split(ada, 6, axis=-1)
        sh1c, sc1c, g1c, sh2c, sc2c, g2c = jnp.split(ada_c, 6, axis=-1)
        zl = (x * (1 + sc1) + sh1) @ w_in[l]
        zc = (xc * (1 + sc1c) + sh1c) @ w_in[l]
        za, zb, zm, zh = _split(zl, [GDN_IN, NA_IN, MLA_IN, HGRN_IN])
        zac, zbc, zmc, zhc = _split(zc, [GDN_IN, NA_IN, MLA_IN, HGRN_IN])
        ya, yac = _gdn_mixer(za, zac, gdn_conv_w[l], gdn_a_log[l], gdn_dt_bias[l], gdn_norm_w[l], emit_ctx)
        yb, ybc = _na_mixer(zb, zbc, na_rpb[l], emit_ctx)
        ym, ymc = _mla_mixer(zm, zmc, mla_q_norm_w[l], mla_kv_norm_w[l], mla_w_uq[l], mla_w_uk[l],
                             mla_w_uv[l], rope, emit_ctx)
        yh, yhc = _hgrn_mixer(zh, zhc, lbs[l], hgrn_norm_w[l], emit_ctx)
        y = jnp.concatenate([ya, yb, ym, yh], axis=-1) @ w_out[l]
        x = _layernorm(ALPHA * x + g1 * y, ln1_w[l], ln1_b[l])
        m = _sqrelu_mlp(x * (1 + sc2) + sh2, w_mlp1[l], w_mlp2[l])
        x = _layernorm(ALPHA * x + g2 * m, ln2_w[l], ln2_b[l])
        if emit_ctx:
            yc = jnp.concatenate([yac, ybc, ymc, yhc], axis=-1) @ w_out[l]
            xc = _layernorm(ALPHA * xc + g1c * yc, ln1_w[l], ln1_b[l])
            mc = _sqrelu_mlp(xc * (1 + sc2c) + sh2c, w_mlp1[l], w_mlp2[l])
            xc = _layernorm(ALPHA * xc + g2c * mc, ln2_w[l], ln2_b[l])
    return x
```

```python
import functools

import numpy as np
import jax
import jax.numpy as jnp
from jax import lax
from jax.experimental import pallas as pl
from jax.experimental.pallas import tpu as pltpu

F32 = jnp.float32
BF16 = jnp.bfloat16
HIGHEST = lax.Precision.HIGHEST

GRID_W = 64
N_HEADS = 4
HEAD_DIM = 128
GROUP_WIDTH = 512
CHUNK = 64
SUB = 16
GDN_CONV = 5
NA_ROWS = 8
NA_COLS = 16
MLA_Q_RANK = 384
MLA_KV_RANK = 256
MLA_NOPE = 128
MLA_ROPE = 64
MLA_QK_DIM = MLA_NOPE + MLA_ROPE
ROPE_BASE = 10000.0
LN_EPS = 1e-5
RMS_EPS = 1e-6
NEG = -1e30

LANE = 128
CB_MQ, CB_MKRA, CB_MKV, CB_MKRB, CB_GAB = 0, 3, 4, 6, 7
CB_GQKV, CB_GGATE = 8, 20
CB_NAQ, CB_NAK, CB_NAV = 24, 28, 32
CB_HQ, CB_HFF, CB_HFB, CB_HI, CB_HG = 36, 40, 44, 48, 52
CHUNK_SHIFT = 6
NP_IN = 56 * LANE

VMEM_LIMIT = 48 << 20


def _cp(*sem):
    return pltpu.CompilerParams(dimension_semantics=sem, vmem_limit_bytes=VMEM_LIMIT)


def _silu(x):
    return x * jax.nn.sigmoid(x)


def _dot(a, b, **kw):
    return jnp.dot(a, b, preferred_element_type=F32, **kw)


def _dot_nt(a, b, **kw):
    return lax.dot_general(a, b, (((1,), (1,)), ((), ())), preferred_element_type=F32, **kw)


def _dot_tn(a, b, **kw):
    return lax.dot_general(a, b, (((0,), (0,)), ((), ())), preferred_element_type=F32, **kw)


def _ada_kernel(c_ref, w_ref, b_ref, o_ref):
    s = _silu(c_ref[...])
    o_ref[0] = _dot(s, w_ref[0], precision=HIGHEST) + b_ref[0]


def _ada(cin, w_ada, b_ada):
    depth, d, n = w_ada.shape
    tn = 512
    return pl.pallas_call(
        _ada_kernel,
        grid=(depth, n // tn),
        in_specs=[pl.BlockSpec((8, d), lambda l, j: (0, 0)),
                  pl.BlockSpec((1, d, tn), lambda l, j: (l, 0, j)),
                  pl.BlockSpec((1, 1, tn), lambda l, j: (l, 0, j))],
        out_specs=pl.BlockSpec((1, 8, tn), lambda l, j: (l, 0, j)),
        out_shape=jax.ShapeDtypeStruct((depth, 8, n), F32),
        compiler_params=_cp("parallel", "parallel"),
        name="ada",
    )(cin, w_ada, b_ada.reshape(depth, 1, n))


def _inproj_kernel(x_ref, sh_ref, sc_ref, w_ref, o_ref, xm_ref):
    @pl.when(pl.program_id(1) == 0)
    def _():
        xm_ref[...] = (x_ref[...] * (1.0 + sc_ref[0]) + sh_ref[0]).astype(BF16)
    o_ref[...] = _dot(xm_ref[...], w_ref[...])


def _inproj(x_all, ada_r, w, n_lat, seq, nb, tm=512, tn=1024):
    m, d = x_all.shape
    n = w.shape[1]
    row = lambda i: jnp.minimum((i * tm) // seq, nb)
    return pl.pallas_call(
        _inproj_kernel,
        grid=(m // tm, n // tn),
        in_specs=[pl.BlockSpec((tm, d), lambda i, j: (i, 0)),
                  pl.BlockSpec((1, 1, d), lambda i, j: (row(i) * 6 + 0, 0, 0)),
                  pl.BlockSpec((1, 1, d), lambda i, j: (row(i) * 6 + 1, 0, 0)),
                  pl.BlockSpec((d, tn), lambda i, j: (0, j))],
        out_specs=pl.BlockSpec((tm, tn), lambda i, j: (i, j)),
        out_shape=jax.ShapeDtypeStruct((m, n), F32),
        scratch_shapes=[pltpu.VMEM((tm, d), BF16)],
        compiler_params=_cp("parallel", "arbitrary"),
        name="inproj",
    )(x_all, ada_r, ada_r, w)


def _layernorm(r, w, b):
    mu = jnp.mean(r, axis=-1, keepdims=True)
    rc = r - mu
    var = jnp.mean(rc * rc, axis=-1, keepdims=True)
    return rc * lax.rsqrt(var + LN_EPS) * w + b


def _outproj_kernel(alpha, x_ref, ya_ref, yb_ref, ym_ref, yh_ref, w_ref, g_ref, lw_ref, lb_ref, o_ref):
    gw = GROUP_WIDTH
    acc = _dot(ya_ref[...], w_ref[0:gw, :])
    acc += _dot(yb_ref[...], w_ref[gw:2 * gw, :])
    acc += _dot(ym_ref[...], w_ref[2 * gw:3 * gw, :])
    acc += _dot(yh_ref[...], w_ref[3 * gw:4 * gw, :])
    r = alpha * x_ref[...] + g_ref[0] * acc
    o_ref[...] = _layernorm(r, lw_ref[...], lb_ref[...])


def _outproj(x_all, ys, w, ada_r, lw, lb, m_out, seq, nb, alpha, tm=512):
    d = x_all.shape[1]
    row = lambda i: jnp.minimum((i * tm) // seq, nb)
    yspec = pl.BlockSpec((tm, GROUP_WIDTH), lambda i: (i, 0))
    return pl.pallas_call(
        functools.partial(_outproj_kernel, alpha),
        grid=(m_out // tm,),
        in_specs=[pl.BlockSpec((tm, d), lambda i: (i, 0)), yspec, yspec, yspec, yspec,
                  pl.BlockSpec((d, d), lambda i: (0, 0)),
                  pl.BlockSpec((1, 1, d), lambda i: (row(i) * 6 + 2, 0, 0)),
                  pl.BlockSpec((1, d), lambda i: (0, 0)),
                  pl.BlockSpec((1, d), lambda i: (0, 0))],
        out_specs=pl.BlockSpec((tm, d), lambda i: (i, 0)),
        out_shape=jax.ShapeDtypeStruct((m_out, d), F32),
        compiler_params=_cp("parallel"),
        name="outproj_ln",
    )(x_all, *ys, w, ada_r, lw.reshape(1, d), lb.reshape(1, d))


def _mlp_kernel(alpha, x_ref, sh_ref, sc_ref, g_ref, w1_ref, w2_ref, lw_ref, lb_ref, o_ref, xm_ref, acc_ref):
    k = pl.program_id(1)

    @pl.when(k == 0)
    def _():
        xm_ref[...] = (x_ref[...] * (1.0 + sc_ref[0]) + sh_ref[0]).astype(BF16)
        acc_ref[...] = jnp.zeros_like(acc_ref)

    h = jnp.maximum(_dot(xm_ref[...], w1_ref[...]), 0.0)
    acc_ref[...] += _dot((h * h).astype(BF16), w2_ref[...])

    @pl.when(k == pl.num_programs(1) - 1)
    def _():
        r = alpha * x_ref[...] + g_ref[0] * acc_ref[...]
        o_ref[...] = _layernorm(r, lw_ref[...], lb_ref[...])


def _mlp(x_all, w1, w2, ada_r, lw, lb, seq, nb, alpha, tm=512, th=512):
    m, d = x_all.shape
    hid = w1.shape[1]
    row = lambda i: jnp.minimum((i * tm) // seq, nb)
    return pl.pallas_call(
        functools.partial(_mlp_kernel, alpha),
        grid=(m // tm, hid // th),
        in_specs=[pl.BlockSpec((tm, d), lambda i, k: (i, 0)),
                  pl.BlockSpec((1, 1, d), lambda i, k: (row(i) * 6 + 3, 0, 0)),
                  pl.BlockSpec((1, 1, d), lambda i, k: (row(i) * 6 + 4, 0, 0)),
                  pl.BlockSpec((1, 1, d), lambda i, k: (row(i) * 6 + 5, 0, 0)),
                  pl.BlockSpec((d, th), lambda i, k: (0, k)),
                  pl.BlockSpec((th, d), lambda i, k: (k, 0)),
                  pl.BlockSpec((1, d), lambda i, k: (0, 0)),
                  pl.BlockSpec((1, d), lambda i, k: (0, 0))],
        out_specs=pl.BlockSpec((tm, d), lambda i, k: (i, 0)),
        out_shape=jax.ShapeDtypeStruct((m, d), F32),
        scratch_shapes=[pltpu.VMEM((tm, d), BF16), pltpu.VMEM((tm, d), F32)],
        compiler_params=_cp("parallel", "arbitrary"),
        name="mlp_ln",
    )(x_all, ada_r, ada_r, ada_r, w1, w2, lw.reshape(1, d), lb.reshape(1, d))


def _gdn_conv_kernel(nrows, x_ref, w_ref, o_ref, pad_ref):
    j = pl.program_id(1)
    pad_ref[0:8, :] = jnp.zeros((8, LANE), F32)
    pad_ref[nrows + 8:nrows + 16, :] = jnp.zeros((8, LANE), F32)
    pad_ref[8:nrows + 8, :] = x_ref[...]
    w = w_ref[...]
    rb = min(nrows, 256)
    qscale = jnp.where(j < N_HEADS, HEAD_DIM ** -0.5, 1.0).astype(F32)

    p0 = 8 - GDN_CONV // 2
    for r0 in range(0, nrows, rb):
        y = pad_ref[r0 + p0:r0 + p0 + rb, :] * w[0:1, :]
        for i in range(1, GDN_CONV):
            y = y + pad_ref[r0 + p0 + i:r0 + p0 + i + rb, :] * w[i:i + 1, :]
        y = _silu(y)
        nrm = y * lax.rsqrt(jnp.sum(y * y, axis=-1, keepdims=True) + RMS_EPS) * qscale
        o_ref[r0:r0 + rb, :] = jnp.where(j < 2 * N_HEADS, nrm, y)


def _gdn_conv(z, conv_w, nrows, row_blk0, nb):
    nblk = 3 * N_HEADS
    return pl.pallas_call(
        functools.partial(_gdn_conv_kernel, nrows),
        grid=(nb, nblk),
        in_specs=[pl.BlockSpec((nrows, LANE), lambda b, j: (row_blk0 + b, CB_GQKV + j)),
                  pl.BlockSpec((GDN_CONV, LANE), lambda b, j: (0, j))],
        out_specs=pl.BlockSpec((nrows, LANE), lambda b, j: (b, j)),
        out_shape=jax.ShapeDtypeStruct((nb * nrows, nblk * LANE), F32),
        scratch_shapes=[pltpu.VMEM((nrows + 16, LANE), F32)],
        compiler_params=_cp("parallel", "parallel"),
        name="gdn_conv",
    )(z, conv_w)


def _gdn_gates_kernel(tm, s_ref, alog_ref, dtb_ref, o_ref):
    s = s_ref[...]
    g = -jnp.exp(alog_ref[...]) * (jnp.maximum(s + dtb_ref[...], 0.0)
                                    + jnp.log1p(jnp.exp(-jnp.abs(s + dtb_ref[...]))))
    r = lax.broadcasted_iota(jnp.int32, (tm, tm), 0)
    c = lax.broadcasted_iota(jnp.int32, (tm, tm), 1)
    same = (r >> CHUNK_SHIFT) == (c >> CHUNK_SHIFT)
    lo = jnp.where(same & (c <= r), 1.0, 0.0).astype(F32)
    up = jnp.where(same & (c >= r), 1.0, 0.0).astype(F32)
    cum_f = _dot(lo, g, precision=HIGHEST)
    cum_b = _dot(up, g, precision=HIGHEST)
    col = lax.broadcasted_iota(jnp.int32, s.shape, 1)
    o_ref[...] = jnp.where(col < N_HEADS, cum_f,
                           jnp.where(col < 2 * N_HEADS, cum_b,
                                     jnp.where(col < 4 * N_HEADS, jax.nn.sigmoid(s), 0.0)))


def _gdn_gates(z, a_log, dt_bias, tm=512):
    m = z.shape[0]
    pad = lambda v: jnp.zeros((1, LANE), F32).at[0, :2 * N_HEADS].set(v.reshape(-1).astype(F32))
    return pl.pallas_call(
        functools.partial(_gdn_gates_kernel, tm),
        grid=(m // tm,),
        in_specs=[pl.BlockSpec((tm, LANE), lambda i: (i, CB_GAB)),
                  pl.BlockSpec((1, LANE), lambda i: (0, 0)),
                  pl.BlockSpec((1, LANE), lambda i: (0, 0))],
        out_specs=pl.BlockSpec((tm, LANE), lambda i: (i, 0)),
        out_shape=jax.ShapeDtypeStruct((m, LANE), F32),
        compiler_params=_cp("parallel"),
        name="gdn_gates",
    )(z, pad(a_log), pad(dt_bias))


def _tri_masks(n):
    r = lax.broadcasted_iota(jnp.int32, (n, n), 0)
    c = lax.broadcasted_iota(jnp.int32, (n, n), 1)
    return r, c


def _gdn_chunk(rev, q, k, v, cum_c, cum_r, beta_c, S):
    C = CHUNK
    r, c = _tri_masks(C)
    incl = (c >= r) if rev else (c <= r)
    strict = (c > r) if rev else (c < r)
    last = cum_r[:, 0:1] if rev else cum_r[:, C - 1:C]
    decay = jnp.exp(jnp.where(incl, cum_c - cum_r, NEG))
    kb = k * beta_c
    kbf = k.astype(BF16)
    a = jnp.where(strict, _dot_nt(kb.astype(BF16), kbf) * decay, 0.0)
    t = jnp.where(r == c, 1.0, 0.0).astype(F32) - a
    p = a
    for _ in range(5):
        p = _dot(p, p, precision=HIGHEST)
        t = t + _dot(t, p, precision=HIGHEST)
    rhs = jnp.concatenate([v * beta_c, kb * jnp.exp(cum_c)], axis=-1)
    sol = _dot(t, rhs, precision=HIGHEST)
    u, w = sol[:, :HEAD_DIM], sol[:, HEAD_DIM:]
    sb = S.astype(BF16)
    v_new = u - _dot(w.astype(BF16), sb)
    vnb = v_new.astype(BF16)
    kt = (k * jnp.exp(last - cum_c)).astype(BF16)
    s_new = S * jnp.exp(last) + _dot_tn(kt, vnb)
    attn = jnp.where(incl, _dot_nt(q.astype(BF16), kbf) * decay, 0.0)
    o = _dot((q * jnp.exp(cum_c)).astype(BF16), sb) + _dot(attn.astype(BF16), vnb)
    return o, s_new


def _head_norm_gate(o, nw, gate):
    o = o * lax.rsqrt(jnp.mean(o * o, axis=-1, keepdims=True) + RMS_EPS) * nw
    return o * _silu(gate)


def _gdn_scan_kernel(emit_ctx, n_lat, n_ctx, *refs):
    (ql_ref, kl_ref, vl_ref, pl_ref, ctl_ref, gl_ref,
     qc_ref, kc_ref, vc_ref, pc_ref, ctc_ref, gc_ref, nw_ref) = refs[:13]
    if emit_ctx:
        yl_ref, yc_ref, sf_ref, sb_ref, of_ref, ob_ref, ocf_ref, ocb_ref = refs[13:]
    else:
        yl_ref, sf_ref, sb_ref, of_ref, ob_ref = refs[13:]
        yc_ref = ocf_ref = ocb_ref = None
    h = pl.program_id(1)
    lane = lax.broadcasted_iota(jnp.int32, (CHUNK, LANE), 1)
    sub8 = lax.broadcasted_iota(jnp.int32, (2 * N_HEADS, CHUNK), 0)

    def col(pblk, idx):
        return jnp.sum(jnp.where(lane == idx, pblk, 0.0), axis=-1, keepdims=True)

    def row(tblk, idx):
        return jnp.sum(jnp.where(sub8 == idx, tblk, 0.0), axis=0, keepdims=True)

    def step(rev, q_ref, k_ref, v_ref, p_ref, ct_ref, s_ref, o_ref, ci):
        r0 = pl.multiple_of(ci * CHUNK, CHUNK)
        sl = pl.ds(r0, CHUNK)
        d = 1 if rev else 0
        pblk = p_ref[sl, :]
        tblk = ct_ref[ci]
        o, s_new = _gdn_chunk(rev, q_ref[sl, :], k_ref[sl, :], v_ref[sl, :],
                              col(pblk, d * N_HEADS + h), row(tblk, d * N_HEADS + h),
                              col(pblk, (2 + d) * N_HEADS + h), s_ref[...])
        s_ref[...] = s_new
        if o_ref is not None:
            o_ref[sl, :] = o

    sf_ref[...] = jnp.zeros_like(sf_ref)
    sb_ref[...] = jnp.zeros_like(sb_ref)

    def ctx_body(i, carry):
        step(False, qc_ref, kc_ref, vc_ref, pc_ref, ctc_ref, sf_ref, ocf_ref, i)
        step(True, qc_ref, kc_ref, vc_ref, pc_ref, ctc_ref, sb_ref, ocb_ref, n_ctx - 1 - i)
        return carry

    lax.fori_loop(0, n_ctx, ctx_body, 0)

    def lat_body(i, carry):
        step(False, ql_ref, kl_ref, vl_ref, pl_ref, ctl_ref, sf_ref, of_ref, i)
        step(True, ql_ref, kl_ref, vl_ref, pl_ref, ctl_ref, sb_ref, ob_ref, n_lat - 1 - i)
        return carry

    lax.fori_loop(0, n_lat, lat_body, 0)

    nw = nw_ref[...]
    rb = 256

    def fin(o1, o2, g_ref, y_ref, n):
        def body(t, carry):
            sl = pl.ds(pl.multiple_of(t * rb, rb), rb)
            y_ref[sl, :] = _head_norm_gate(o1[sl, :] + o2[sl, :], nw, g_ref[sl, :]).astype(y_ref.dtype)
            return carry
        lax.fori_loop(0, n // rb, body, 0)

    fin(of_ref, ob_ref, gl_ref, yl_ref, n_lat * CHUNK)
    if emit_ctx:
        fin(ocf_ref, ocb_ref, gc_ref, yc_ref, n_ctx * CHUNK)


def _gdn_scan(qkv_l, qkv_c, p, ct, z, norm_w, nb, seq, ctx_len, emit_ctx):
    n_lat, n_ctx = seq // CHUNK, ctx_len // CHUNK
    H = N_HEADS
    cblk0 = nb * seq // ctx_len
    lat = lambda cb: pl.BlockSpec((seq, LANE), lambda b, h: (b, cb + h))
    ctx = lambda cb: pl.BlockSpec((ctx_len, LANE), lambda b, h: (b, cb + h))
    in_specs = [lat(0), lat(H), lat(2 * H),
                pl.BlockSpec((seq, LANE), lambda b, h: (b, 0)),
                pl.BlockSpec((n_lat, 2 * H, CHUNK), lambda b, h: (b, 0, 0)),
                pl.BlockSpec((seq, LANE), lambda b, h: (b, CB_GGATE + h)),
                ctx(0), ctx(H), ctx(2 * H),
                pl.BlockSpec((ctx_len, LANE), lambda b, h: (cblk0 + b, 0)),
                pl.BlockSpec((n_ctx, 2 * H, CHUNK), lambda b, h: (nb * n_lat // n_ctx + b, 0, 0)),
                pl.BlockSpec((ctx_len, LANE), lambda b, h: (cblk0 + b, CB_GGATE + h)),
                pl.BlockSpec((1, LANE), lambda b, h: (0, 0))]
    out_specs = [pl.BlockSpec((seq, LANE), lambda b, h: (b, h))]
    out_shape = [jax.ShapeDtypeStruct((nb * seq, GROUP_WIDTH), BF16)]
    scratch = [pltpu.VMEM((HEAD_DIM, HEAD_DIM), F32), pltpu.VMEM((HEAD_DIM, HEAD_DIM), F32),
               pltpu.VMEM((seq, LANE), F32), pltpu.VMEM((seq, LANE), F32)]
    if emit_ctx:
        out_specs.insert(1, pl.BlockSpec((ctx_len, LANE), lambda b, h: (b, h)))
        out_shape.insert(1, jax.ShapeDtypeStruct((nb * ctx_len, GROUP_WIDTH), BF16))
        scratch += [pltpu.VMEM((ctx_len, LANE), F32), pltpu.VMEM((ctx_len, LANE), F32)]
    return pl.pallas_call(
        functools.partial(_gdn_scan_kernel, emit_ctx, n_lat, n_ctx),
        grid=(nb, H),
        in_specs=in_specs, out_specs=out_specs, out_shape=out_shape, scratch_shapes=scratch,
        compiler_params=_cp("parallel", "parallel"),
        name="gdn_scan",
    )(qkv_l, qkv_l, qkv_l, p, ct, z, qkv_c, qkv_c, qkv_c, p, ct, z, norm_w.reshape(1, LANE))


def _gla_chunk(rev, zq, zf, zi, lb, tri, St):
    C, nsub = CHUNK, CHUNK // SUB
    q = _silu(zq) * HEAD_DIM ** -0.5
    f = lb + (1.0 - lb) * jax.nn.sigmoid(zf)
    k = 1.0 - f
    cum = _dot(tri, jnp.log(f), precision=HIGHEST)
    last = cum[0:1, :] if rev else cum[C - 1:C, :]
    vb = zi.astype(BF16)
    o = _dot_nt((q * jnp.exp(cum)).astype(BF16), St.astype(BF16))
    st_new = St * jnp.exp(last) + _dot_tn(vb, (k * jnp.exp(last - cum)).astype(BF16))
    tsub = lax.broadcasted_iota(jnp.int32, (SUB, HEAD_DIM), 0)
    trow = lax.broadcasted_iota(jnp.int32, (C, HEAD_DIM), 0)
    scol = lax.broadcasted_iota(jnp.int32, (SUB, C), 1)
    blocks = []
    for a in range(nsub):
        sa = slice(a * SUB, (a + 1) * SUB)
        qa, ka, ca = q[sa], k[sa], cum[sa]
        if rev and a < nsub - 1:
            cb = cum[(a + 1) * SUB:(a + 1) * SUB + 1, :]
            kt = k * jnp.exp(jnp.where(trow >= (a + 1) * SUB, cb - cum, NEG))
            sc = _dot_nt((qa * jnp.exp(ca - cb)).astype(BF16), kt.astype(BF16))
        elif (not rev) and a > 0:
            cb = cum[a * SUB - 1:a * SUB, :]
            kt = k * jnp.exp(jnp.where(trow < a * SUB, cb - cum, NEG))
            sc = _dot_nt((qa * jnp.exp(ca - cb)).astype(BF16), kt.astype(BF16))
        else:
            sc = jnp.zeros((SUB, C), F32)
        for j in range(SUB):
            ok = (tsub <= j) if rev else (tsub >= j)
            dec = jnp.exp(jnp.where(ok, ca - ca[j:j + 1, :], NEG))
            sj = jnp.sum(qa * ka[j:j + 1, :] * dec, axis=-1, keepdims=True)
            sc = jnp.where(scol == a * SUB + j, sj, sc)
        blocks.append(sc)
    scores = jnp.concatenate(blocks, axis=0)
    return o + _dot(scores.astype(BF16), vb), st_new


def _gla_scan_kernel(emit_ctx, n_lat, n_ctx, *refs):
    (ql_ref, ffl_ref, fbl_ref, il_ref, gl_ref,
     qc_ref, ffc_ref, fbc_ref, ic_ref, gc_ref, lb_ref, nw_ref) = refs[:12]
    if emit_ctx:
        yl_ref, yc_ref, sf_ref, sb_ref, of_ref, ob_ref, ocf_ref, ocb_ref = refs[12:]
    else:
        yl_ref, sf_ref, sb_ref, of_ref, ob_ref = refs[12:]
        yc_ref = ocf_ref = ocb_ref = None
    r, c = _tri_masks(CHUNK)
    lo = jnp.where(c <= r, 1.0, 0.0).astype(F32)
    up = jnp.where(c >= r, 1.0, 0.0).astype(F32)
    lb = lb_ref[0]

    def step(rev, q_ref, f_ref, i_ref, s_ref, o_ref, ci):
        sl = pl.ds(pl.multiple_of(ci * CHUNK, CHUNK), CHUNK)
        o, s_new = _gla_chunk(rev, q_ref[sl, :], f_ref[sl, :], i_ref[sl, :], lb, up if rev else lo, s_ref[...])
        s_ref[...] = s_new
        if o_ref is not None:
            o_ref[sl, :] = o

    sf_ref[...] = jnp.zeros_like(sf_ref)
    sb_ref[...] = jnp.zeros_like(sb_ref)

    def ctx_body(i, carry):
        step(False, qc_ref, ffc_ref, ic_ref, sf_ref, ocf_ref, i)
        step(True, qc_ref, fbc_ref, ic_ref, sb_ref, ocb_ref, n_ctx - 1 - i)
        return carry

    lax.fori_loop(0, n_ctx, ctx_body, 0)

    def lat_body(i, carry):
        step(False, ql_ref, ffl_ref, il_ref, sf_ref, of_ref, i)
        step(True, ql_ref, fbl_ref, il_ref, sb_ref, ob_ref, n_lat - 1 - i)
        return carry

    lax.fori_loop(0, n_lat, lat_body, 0)

    nw = nw_ref[...]
    rb = 256

    def fin(o1, o2, g_ref, y_ref, n):
        def body(t, carry):
            sl = pl.ds(pl.multiple_of(t * rb, rb), rb)
            y_ref[sl, :] = _head_norm_gate(o1[sl, :] + o2[sl, :], nw, g_ref[sl, :]).astype(y_ref.dtype)
            return carry
        lax.fori_loop(0, n // rb, body, 0)

    fin(of_ref, ob_ref, gl_ref, yl_ref, n_lat * CHUNK)
    if emit_ctx:
        fin(ocf_ref, ocb_ref, gc_ref, yc_ref, n_ctx * CHUNK)


def _gla_scan(z, lbs, norm_w, nb, seq, ctx_len, emit_ctx):
    n_lat, n_ctx = seq // CHUNK, ctx_len // CHUNK
    cblk0 = nb * seq // ctx_len
    lat = lambda cb: pl.BlockSpec((seq, LANE), lambda b, h: (b, cb + h))
    ctx = lambda cb: pl.BlockSpec((ctx_len, LANE), lambda b, h: (cblk0 + b, cb + h))
    cbs = (CB_HQ, CB_HFF, CB_HFB, CB_HI, CB_HG)
    in_specs = ([lat(cb) for cb in cbs] + [ctx(cb) for cb in cbs]
                + [pl.BlockSpec((1, 1, LANE), lambda b, h: (h, 0, 0)),
                   pl.BlockSpec((1, LANE), lambda b, h: (0, 0))])
    out_specs = [pl.BlockSpec((seq, LANE), lambda b, h: (b, h))]
    out_shape = [jax.ShapeDtypeStruct((nb * seq, GROUP_WIDTH), BF16)]
    scratch = [pltpu.VMEM((HEAD_DIM, HEAD_DIM), F32), pltpu.VMEM((HEAD_DIM, HEAD_DIM), F32),
               pltpu.VMEM((seq, LANE), F32), pltpu.VMEM((seq, LANE), F32)]
    if emit_ctx:
        out_specs.append(pl.BlockSpec((ctx_len, LANE), lambda b, h: (b, h)))
        out_shape.append(jax.ShapeDtypeStruct((nb * ctx_len, GROUP_WIDTH), BF16))
        scratch += [pltpu.VMEM((ctx_len, LANE), F32), pltpu.VMEM((ctx_len, LANE), F32)]
    return pl.pallas_call(
        functools.partial(_gla_scan_kernel, emit_ctx, n_lat, n_ctx),
        grid=(nb, N_HEADS),
        in_specs=in_specs, out_specs=out_specs, out_shape=out_shape, scratch_shapes=scratch,
        compiler_params=_cp("parallel", "parallel"),
        name="hgrn_scan",
    )(*([z] * 10), lbs.reshape(N_HEADS, 1, LANE), norm_w.reshape(1, LANE))


def _softmax_pv(parts):
    m = parts[0][0].max(axis=-1, keepdims=True)
    for s, _ in parts[1:]:
        m = jnp.maximum(m, s.max(axis=-1, keepdims=True))
    den, acc = None, None
    for s, v in parts:
        p = jnp.exp(s - m)
        d = jnp.sum(p, axis=-1, keepdims=True)
        a = _dot(p.astype(BF16), v)
        den = d if den is None else den + d
        acc = a if acc is None else acc + a
    return acc / den


def _na_kernel(emit_ctx, rows, *refs):
    q_ref, k_ref, v_ref, qc_ref, kc_ref, vc_ref, bias_ref = refs[:7]
    if emit_ctx:
        yl_ref, yc_ref, kb_ref, vb_ref = refs[7:]
    else:
        yl_ref, kb_ref, vb_ref = refs[7:]
    scale = HEAD_DIM ** -0.5
    win = NA_ROWS * GRID_W
    kb_ref[...] = k_ref[...].astype(BF16)
    vb_ref[...] = v_ref[...].astype(BF16)
    kc = kc_ref[...].astype(BF16)
    vc = vc_ref[...].astype(BF16)

    def body(r, carry):
        row0 = jnp.clip(r - NA_ROWS // 2, 0, rows - NA_ROWS)
        q = q_ref[pl.ds(pl.multiple_of(r * GRID_W, GRID_W), GRID_W), :].astype(BF16)
        ks = pl.ds(pl.multiple_of(row0 * GRID_W, GRID_W), win)
        s_win = _dot_nt(q, kb_ref[ks, :]) * scale + bias_ref[0, r - row0]
        s_ctx = _dot_nt(q, kc) * scale
        o = _softmax_pv([(s_win, vb_ref[ks, :]), (s_ctx, vc)])
        yl_ref[pl.ds(pl.multiple_of(r * GRID_W, GRID_W), GRID_W), :] = o.astype(yl_ref.dtype)
        return carry

    lax.fori_loop(0, rows, body, 0)
    if emit_ctx:
        s = _dot_nt(qc_ref[...].astype(BF16), kc) * scale
        yc_ref[...] = _softmax_pv([(s, vc)]).astype(yc_ref.dtype)


def _na_bias_table(rpb):
    col = np.arange(GRID_W)
    c0 = np.clip(col - NA_COLS // 2, 0, GRID_W - NA_COLS)
    col_ok = (col[None, :] >= c0[:, None]) & (col[None, :] < c0[:, None] + NA_COLS)
    dc = np.clip(col[None, :] - col[:, None], 1 - NA_COLS, NA_COLS - 1) + NA_COLS - 1
    s = np.arange(NA_ROWS)
    dr = np.arange(NA_ROWS)[None, :] - s[:, None] + NA_ROWS - 1
    b = rpb.astype(F32)[:, dr[:, None, :, None], dc[None, :, None, :]]
    b = jnp.where(col_ok[None, None, :, None, :], b, NEG)
    return b.reshape(rpb.shape[0], NA_ROWS, GRID_W, NA_ROWS * GRID_W)


def _na(z, rpb, nb, seq, ctx_len, emit_ctx):
    rows = seq // GRID_W
    cblk0 = nb * seq // ctx_len
    H = N_HEADS
    lat = lambda cb: pl.BlockSpec((seq, LANE), lambda b, h: (b, cb + h))
    ctx = lambda cb: pl.BlockSpec((ctx_len, LANE), lambda b, h: (cblk0 + b, cb + h))
    win = NA_ROWS * GRID_W
    in_specs = [lat(CB_NAQ), lat(CB_NAK), lat(CB_NAV), ctx(CB_NAQ), ctx(CB_NAK), ctx(CB_NAV),
                pl.BlockSpec((1, NA_ROWS, GRID_W, win), lambda b, h: (h, 0, 0, 0))]
    out_specs = [pl.BlockSpec((seq, LANE), lambda b, h: (b, h))]
    out_shape = [jax.ShapeDtypeStruct((nb * seq, GROUP_WIDTH), BF16)]
    if emit_ctx:
        out_specs.append(pl.BlockSpec((ctx_len, LANE), lambda b, h: (b, h)))
        out_shape.append(jax.ShapeDtypeStruct((nb * ctx_len, GROUP_WIDTH), BF16))
    return pl.pallas_call(
        functools.partial(_na_kernel, emit_ctx, rows),
        grid=(nb, H),
        in_specs=in_specs, out_specs=out_specs, out_shape=out_shape,
        scratch_shapes=[pltpu.VMEM((seq, LANE), BF16), pltpu.VMEM((seq, LANE), BF16)],
        compiler_params=_cp("parallel", "parallel"),
        name="na_attn",
    )(z, z, z, z, z, z, _na_bias_table(rpb))


def _rms(x, w):
    return x * lax.rsqrt(jnp.mean(x * x, axis=-1, keepdims=True) + RMS_EPS) * w


def _mla_prep_kernel(cq_ref, ckv_ref, kra_ref, krb_ref, cc_ref, ss_ref, qnw_ref, kvnw_ref, wuq_ref, wukv_ref,
                     q_ref, kn_ref, kr_ref, v_ref):
    cc, ss = cc_ref[...], ss_ref[...]
    qn = _rms(cq_ref[...], qnw_ref[...]).astype(BF16)
    qa = _dot(qn, wuq_ref[...])
    for h in range(N_HEADS):
        b = 3 * LANE * h
        q_ref[:, 2 * LANE * h:2 * LANE * h + LANE] = qa[:, b:b + LANE].astype(BF16)
        q_ref[:, 2 * LANE * h + LANE:2 * LANE * (h + 1)] = (
            qa[:, b + LANE:b + 2 * LANE] * cc + qa[:, b + 2 * LANE:b + 3 * LANE] * ss).astype(BF16)
    kvn = _rms(ckv_ref[...], kvnw_ref[...]).astype(BF16)
    kv = _dot(kvn, wukv_ref[...])
    kn_ref[...] = kv[:, :GROUP_WIDTH].astype(BF16)
    v_ref[...] = kv[:, GROUP_WIDTH:].astype(BF16)
    kr_ref[...] = (kra_ref[...] * cc + krb_ref[...] * ss).astype(BF16)


def _mla_prep(z, cc, ss, qnw, kvnw, wuq, wukv, n_lat, seq, tm=512):
    m = z.shape[0]
    nlt, spt = n_lat // tm, seq // tm
    tab = lambda i: jnp.where(i < nlt, i % spt, spt)
    H = N_HEADS
    return pl.pallas_call(
        _mla_prep_kernel,
        grid=(m // tm,),
        in_specs=[pl.BlockSpec((tm, MLA_Q_RANK), lambda i: (i, CB_MQ * LANE // MLA_Q_RANK)),
                  pl.BlockSpec((tm, MLA_KV_RANK), lambda i: (i, CB_MKV * LANE // MLA_KV_RANK)),
                  pl.BlockSpec((tm, LANE), lambda i: (i, CB_MKRA)),
                  pl.BlockSpec((tm, LANE), lambda i: (i, CB_MKRB)),
                  pl.BlockSpec((tm, LANE), lambda i: (tab(i), 0)),
                  pl.BlockSpec((tm, LANE), lambda i: (tab(i), 0)),
                  pl.BlockSpec((1, MLA_Q_RANK), lambda i: (0, 0)),
                  pl.BlockSpec((1, MLA_KV_RANK), lambda i: (0, 0)),
                  pl.BlockSpec(wuq.shape, lambda i: (0, 0)),
                  pl.BlockSpec(wukv.shape, lambda i: (0, 0))],
        out_specs=[pl.BlockSpec((tm, 2 * LANE * H), lambda i: (i, 0)),
                   pl.BlockSpec((tm, GROUP_WIDTH), lambda i: (i, 0)),
                   pl.BlockSpec((tm, LANE), lambda i: (i, 0)),
                   pl.BlockSpec((tm, GROUP_WIDTH), lambda i: (i, 0))],
        out_shape=[jax.ShapeDtypeStruct((m, 2 * LANE * H), BF16),
                   jax.ShapeDtypeStruct((m, GROUP_WIDTH), BF16),
                   jax.ShapeDtypeStruct((m, LANE), BF16),
                   jax.ShapeDtypeStruct((m, GROUP_WIDTH), BF16)],
        compiler_params=_cp("parallel"),
        name="mla_prep",
    )(z, z, z, z, cc, ss, qnw.reshape(1, -1), kvnw.reshape(1, -1), wuq, wukv)


def _mla_attn_kernel(with_lat, seq, *refs):
    if with_lat:
        q_ref, knl_ref, krl_ref, vl_ref, knc_ref, krc_ref, vc_ref, y_ref, k_scr = refs
    else:
        q_ref, knc_ref, krc_ref, vc_ref, y_ref, k_scr = refs
    scale = MLA_QK_DIM ** -0.5
    nk = k_scr.shape[0]

    @pl.when(pl.program_id(2) == 0)
    def _():
        if with_lat:
            k_scr[0:seq, 0:LANE] = knl_ref[...]
            k_scr[0:seq, LANE:2 * LANE] = krl_ref[...]
        k_scr[nk - knc_ref.shape[0]:nk, 0:LANE] = knc_ref[...]
        k_scr[nk - knc_ref.shape[0]:nk, LANE:2 * LANE] = krc_ref[...]

    q = q_ref[...]
    parts = []
    if with_lat:
        parts.append((_dot_nt(q, k_scr[0:seq, :]) * scale, vl_ref[...]))
    parts.append((_dot_nt(q, k_scr[nk - knc_ref.shape[0]:nk, :]) * scale, vc_ref[...]))
    y_ref[...] = _softmax_pv(parts).astype(y_ref.dtype)


def _mla_attn(q, kn, kr, v, nb, seq, ctx_len, with_lat, tq=256):
    H = N_HEADS
    cblk0 = nb * seq // ctx_len
    nq = seq if with_lat else ctx_len
    tq = min(tq, nq)
    qblk0 = 0 if with_lat else nb * seq // tq
    ctxs = [pl.BlockSpec((ctx_len, LANE), lambda b, h, i: (cblk0 + b, h)),
            pl.BlockSpec((ctx_len, LANE), lambda b, h, i: (cblk0 + b, 0)),
            pl.BlockSpec((ctx_len, LANE), lambda b, h, i: (cblk0 + b, h))]
    lats = [pl.BlockSpec((seq, LANE), lambda b, h, i: (b, h)),
            pl.BlockSpec((seq, LANE), lambda b, h, i: (b, 0)),
            pl.BlockSpec((seq, LANE), lambda b, h, i: (b, h))]
    in_specs = [pl.BlockSpec((tq, 2 * LANE), lambda b, h, i: (qblk0 + b * (nq // tq) + i, h))]
    args = [q]
    if with_lat:
        in_specs += lats
        args += [kn, kr, v]
    in_specs += ctxs
    args += [kn, kr, v]
    nk = (seq if with_lat else 0) + ctx_len
    return pl.pallas_call(
        functools.partial(_mla_attn_kernel, with_lat, seq),
        grid=(nb, H, nq // tq),
        in_specs=in_specs,
        out_specs=pl.BlockSpec((tq, LANE), lambda b, h, i: (b * (nq // tq) + i, h)),
        out_shape=jax.ShapeDtypeStruct((nb * nq, GROUP_WIDTH), BF16),
        scratch_shapes=[pltpu.VMEM((nk, 2 * LANE), BF16)],
        compiler_params=_cp("parallel", "parallel", "arbitrary"),
        name="mla_attn_lat" if with_lat else "mla_attn_ctx",
    )(*args)


def _prep_w_in(w_in):
    d = w_in.shape[0]
    gw = GROUP_WIDTH
    o_na = 4 * gw + 4 * N_HEADS
    o_mla = o_na + 3 * gw
    o_kr = o_mla + MLA_Q_RANK + MLA_KV_RANK
    o_hg = o_kr + MLA_ROPE
    kr = w_in[:, o_kr:o_hg]
    k1, k2 = kr[:, 0::2], kr[:, 1::2]
    z = lambda n: jnp.zeros((d, n), w_in.dtype)
    o_kv = o_mla + MLA_Q_RANK
    cols = [w_in[:, o_mla:o_kv], k1, k2, z(LANE - MLA_ROPE),
            w_in[:, o_kv:o_kr], k2, k1, z(LANE - MLA_ROPE),
            w_in[:, 4 * gw:o_na], z(LANE - 4 * N_HEADS),
            w_in[:, :4 * gw], w_in[:, o_na:o_mla], w_in[:, o_hg:]]
    w = jnp.concatenate(cols, axis=1).astype(BF16)
    assert w.shape[1] == NP_IN
    return w


def _prep_w_uq(w_uq):
    r = w_uq.shape[0]
    z = jnp.zeros((r, LANE - MLA_ROPE), w_uq.dtype)
    cols = []
    for h in range(N_HEADS):
        wh = w_uq[:, h * MLA_QK_DIM:(h + 1) * MLA_QK_DIM]
        rope = wh[:, MLA_NOPE:]
        r1, r2 = rope[:, 0::2], rope[:, 1::2]
        cols += [wh[:, :MLA_NOPE], r1, r2, z, r2, r1, z]
    return jnp.concatenate(cols, axis=1).astype(BF16)


def _rope_tables(seq, tm):
    n_freq = MLA_ROPE // 4
    freqs = ROPE_BASE ** (-jnp.arange(n_freq, dtype=F32) / n_freq)
    t = jnp.arange(seq)
    ang = jnp.concatenate([(t // GRID_W).astype(F32)[:, None] * freqs,
                           (t % GRID_W).astype(F32)[:, None] * freqs], -1)
    cos, sin = jnp.cos(ang), jnp.sin(ang)
    zp = jnp.zeros((seq, LANE - MLA_ROPE), F32)
    cc = jnp.concatenate([cos, cos, zp], axis=1)
    ss = jnp.concatenate([-sin, sin, zp], axis=1)
    ident = jnp.zeros((tm, LANE), F32).at[:, :MLA_ROPE].set(1.0)
    return jnp.concatenate([cc, ident], axis=0), jnp.concatenate([ss, jnp.zeros((tm, LANE), F32)], axis=0)


def kernel(x, c, ctx, c_ctx, w_ada, b_ada, w_in, gdn_conv_w, gdn_a_log, gdn_dt_bias, gdn_norm_w, na_rpb,
           mla_q_norm_w, mla_kv_norm_w, mla_w_uq, mla_w_uk, mla_w_uv, hgrn_lower_bounds, hgrn_norm_w, w_out,
           ln1_w, ln1_b, w_mlp1, w_mlp2, ln2_w, ln2_b):
    nb, seq, d = x.shape
    ctx_len = ctx.shape[1]
    depth = w_ada.shape[0]
    n_lat, n_ctx = nb * seq, nb * ctx_len
    alpha = (2 * depth) ** 0.25
    tm = 512
    assert nb < 8 and seq % tm == 0 and n_ctx % tm == 0 and seq % ctx_len == 0 and ctx_len % CHUNK == 0

    cin = jnp.zeros((8, d), F32).at[:nb].set(c).at[nb].set(c_ctx)
    ada = _ada(cin, w_ada, b_ada)
    p_lb = jax.nn.softmax(hgrn_lower_bounds.astype(F32), axis=0)
    lbs = jnp.cumsum(p_lb, axis=0) - p_lb[0]
    cc, ss = _rope_tables(seq, tm)

    x_all = jnp.concatenate([x.reshape(n_lat, d), ctx.reshape(n_ctx, d)], axis=0)
    for l in range(depth):
        emit_ctx = l < depth - 1
        ada_r = ada[l].reshape(8 * 6, 1, d)
        z = _inproj(x_all, ada_r, _prep_w_in(w_in[l]), n_lat, seq, nb, tm=tm)

        qkv_l = _gdn_conv(z, gdn_conv_w[l], seq, 0, nb)
        qkv_c = _gdn_conv(z, gdn_conv_w[l], ctx_len, n_lat // ctx_len, nb)
        p = _gdn_gates(z, gdn_a_log[l], gdn_dt_bias[l], tm=tm)
        ct = p[:, :2 * N_HEADS].reshape(-1, CHUNK, 2 * N_HEADS).transpose(0, 2, 1)
        ya = _gdn_scan(qkv_l, qkv_c, p, ct, z, gdn_norm_w[l], nb, seq, ctx_len, emit_ctx)
        yb = _na(z, na_rpb[l], nb, seq, ctx_len, emit_ctx)
        q, kn, kr, v = _mla_prep(z, cc, ss, mla_q_norm_w[l], mla_kv_norm_w[l], _prep_w_uq(mla_w_uq[l]),
                                 jnp.concatenate([mla_w_uk[l], mla_w_uv[l]], axis=1).astype(BF16),
                                 n_lat, seq, tm=tm)
        ym = [_mla_attn(q, kn, kr, v, nb, seq, ctx_len, True)]
        if emit_ctx:
            ym.append(_mla_attn(q, kn, kr, v, nb, seq, ctx_len, False))
        yh = _gla_scan(z, lbs[l], hgrn_norm_w[l], nb, seq, ctx_len, emit_ctx)

        if emit_ctx:
            ys = [jnp.concatenate(list(t), axis=0) for t in (ya, yb, ym, yh)]
            m_out = n_lat + n_ctx
        else:
            ys = [ya[0], yb[0], ym[0], yh[0]]
            m_out = n_lat
        x_all = _outproj(x_all, ys, w_out[l].astype(BF16), ada_r, ln1_w[l], ln1_b[l], m_out, seq, nb, alpha, tm=tm)
        x_all = _mlp(x_all, w_mlp1[l].astype(BF16), w_mlp2[l].astype(BF16), ada_r, ln2_w[l], ln2_b[l],
                     seq, nb, alpha, tm=tm)
    return x_all[:n_lat].reshape(nb, seq, d)
```

```python
import functools

import numpy as np
import jax
import jax.numpy as jnp
from jax import lax
from jax.experimental import pallas as pl
from jax.experimental.pallas import tpu as pltpu

F32 = jnp.float32
BF16 = jnp.bfloat16
HIGHEST = lax.Precision.HIGHEST

GRID_W = 64
N_HEADS = 4
HEAD_DIM = 128
GROUP_WIDTH = 512
CHUNK = 64
SUB = 16
GDN_CONV = 5
NA_ROWS = 8
NA_COLS = 16
MLA_Q_RANK = 384
MLA_KV_RANK = 256
MLA_NOPE = 128
MLA_ROPE = 64
MLA_QK_DIM = MLA_NOPE + MLA_ROPE
ROPE_BASE = 10000.0
LN_EPS = 1e-5
RMS_EPS = 1e-6
NEG = -1e30

LANE = 128
CB_MQ, CB_MKRA, CB_MKV, CB_MKRB, CB_GAB = 0, 3, 4, 6, 7
CB_GQKV, CB_GGATE = 8, 20
CB_NAQ, CB_NAK, CB_NAV = 24, 28, 32
CB_HQ, CB_HFF, CB_HFB, CB_HI, CB_HG = 36, 40, 44, 48, 52
CHUNK_SHIFT = 6
NP_IN = 56 * LANE

VMEM_LIMIT = 48 << 20


def _cp(*sem):
    return pltpu.CompilerParams(dimension_semantics=sem, vmem_limit_bytes=VMEM_LIMIT)


def _silu(x):
    return x * jax.nn.sigmoid(x)


def _dot(a, b, **kw):
    return jnp.dot(a, b, preferred_element_type=F32, **kw)


def _dot_nt(a, b, **kw):
    return lax.dot_general(a, b, (((1,), (1,)), ((), ())), preferred_element_type=F32, **kw)


def _dot_tn(a, b, **kw):
    return lax.dot_general(a, b, (((0,), (0,)), ((), ())), preferred_element_type=F32, **kw)


def _ada_kernel(c_ref, w_ref, b_ref, o_ref):
    s = _silu(c_ref[...])
    o_ref[0] = _dot(s, w_ref[0], precision=HIGHEST) + b_ref[0]


def _ada(cin, w_ada, b_ada):
    depth, d, n = w_ada.shape
    tn = 512
    return pl.pallas_call(
        _ada_kernel,
        grid=(depth, n // tn),
        in_specs=[pl.BlockSpec((8, d), lambda l, j: (0, 0)),
                  pl.BlockSpec((1, d, tn), lambda l, j: (l, 0, j)),
                  pl.BlockSpec((1, 1, tn), lambda l, j: (l, 0, j))],
        out_specs=pl.BlockSpec((1, 8, tn), lambda l, j: (l, 0, j)),
        out_shape=jax.ShapeDtypeStruct((depth, 8, n), F32),
        compiler_params=_cp("parallel", "parallel"),
        name="ada",
    )(cin, w_ada, b_ada.reshape(depth, 1, n))


def _inproj_kernel(x_ref, sh_ref, sc_ref, w_ref, o_ref, xm_ref):
    @pl.when(pl.program_id(1) == 0)
    def _():
        xm_ref[...] = (x_ref[...] * (1.0 + sc_ref[0]) + sh_ref[0]).astype(BF16)
    o_ref[...] = _dot(xm_ref[...], w_ref[...])


def _inproj(x_all, ada_r, w, n_lat, seq, nb, tm=512, tn=1024):
    m, d = x_all.shape
    n = w.shape[1]
    row = lambda i: jnp.minimum((i * tm) // seq, nb)
    return pl.pallas_call(
        _inproj_kernel,
        grid=(m // tm, n // tn),
        in_specs=[pl.BlockSpec((tm, d), lambda i, j: (i, 0)),
                  pl.BlockSpec((1, 1, d), lambda i, j: (row(i) * 6 + 0, 0, 0)),
                  pl.BlockSpec((1, 1, d), lambda i, j: (row(i) * 6 + 1, 0, 0)),
                  pl.BlockSpec((d, tn), lambda i, j: (0, j))],
        out_specs=pl.BlockSpec((tm, tn), lambda i, j: (i, j)),
        out_shape=jax.ShapeDtypeStruct((m, n), F32),
        scratch_shapes=[pltpu.VMEM((tm, d), BF16)],
        compiler_params=_cp("parallel", "arbitrary"),
        name="inproj",
    )(x_all, ada_r, ada_r, w)


def _layernorm(r, w, b):
    mu = jnp.mean(r, axis=-1, keepdims=True)
    rc = r - mu
    var = jnp.mean(rc * rc, axis=-1, keepdims=True)
    return rc * lax.rsqrt(var + LN_EPS) * w + b


def _outproj_kernel(alpha, x_ref, ya_ref, yb_ref, ym_ref, yh_ref, w_ref, g_ref, lw_ref, lb_ref, o_ref):
    gw = GROUP_WIDTH
    acc = _dot(ya_ref[...], w_ref[0:gw, :])
    acc += _dot(yb_ref[...], w_ref[gw:2 * gw, :])
    acc += _dot(ym_ref[...], w_ref[2 * gw:3 * gw, :])
    acc += _dot(yh_ref[...], w_ref[3 * gw:4 * gw, :])
    r = alpha * x_ref[...] + g_ref[0] * acc
    o_ref[...] = _layernorm(r, lw_ref[...], lb_ref[...])


def _outproj(x_all, ys, w, ada_r, lw, lb, m_out, seq, nb, alpha, tm=512):
    d = x_all.shape[1]
    row = lambda i: jnp.minimum((i * tm) // seq, nb)
    yspec = pl.BlockSpec((tm, GROUP_WIDTH), lambda i: (i, 0))
    return pl.pallas_call(
        functools.partial(_outproj_kernel, alpha),
        grid=(m_out // tm,),
        in_specs=[pl.BlockSpec((tm, d), lambda i: (i, 0)), yspec, yspec, yspec, yspec,
                  pl.BlockSpec((d, d), lambda i: (0, 0)),
                  pl.BlockSpec((1, 1, d), lambda i: (row(i) * 6 + 2, 0, 0)),
                  pl.BlockSpec((1, d), lambda i: (0, 0)),
                  pl.BlockSpec((1, d), lambda i: (0, 0))],
        out_specs=pl.BlockSpec((tm, d), lambda i: (i, 0)),
        out_shape=jax.ShapeDtypeStruct((m_out, d), F32),
        compiler_params=_cp("parallel"),
        name="outproj_ln",
    )(x_all, *ys, w, ada_r, lw.reshape(1, d), lb.reshape(1, d))


def _mlp_kernel(alpha, x_ref, sh_ref, sc_ref, g_ref, w1_ref, w2_ref, lw_ref, lb_ref, o_ref, xm_ref, acc_ref):
    k = pl.program_id(1)

    @pl.when(k == 0)
    def _():
        xm_ref[...] = (x_ref[...] * (1.0 + sc_ref[0]) + sh_ref[0]).astype(BF16)
        acc_ref[...] = jnp.zeros_like(acc_ref)

    h = jnp.maximum(_dot(xm_ref[...], w1_ref[...]), 0.0)
    acc_ref[...] += _dot((h * h).astype(BF16), w2_ref[...])

    @pl.when(k == pl.num_programs(1) - 1)
    def _():
        r = alpha * x_ref[...] + g_ref[0] * acc_ref[...]
        o_ref[...] = _layernorm(r, lw_ref[...], lb_ref[...])


def _mlp(x_all, w1, w2, ada_r, lw, lb, seq, nb, alpha, tm=512, th=512):
    m, d = x_all.shape
    hid = w1.shape[1]
    row = lambda i: jnp.minimum((i * tm) // seq, nb)
    return pl.pallas_call(
        functools.partial(_mlp_kernel, alpha),
        grid=(m // tm, hid // th),
        in_specs=[pl.BlockSpec((tm, d), lambda i, k: (i, 0)),
                  pl.BlockSpec((1, 1, d), lambda i, k: (row(i) * 6 + 3, 0, 0)),
                  pl.BlockSpec((1, 1, d), lambda i, k: (row(i) * 6 + 4, 0, 0)),
                  pl.BlockSpec((1, 1, d), lambda i, k: (row(i) * 6 + 5, 0, 0)),
                  pl.BlockSpec((d, th), lambda i, k: (0, k)),
                  pl.BlockSpec((th, d), lambda i, k: (k, 0)),
                  pl.BlockSpec((1, d), lambda i, k: (0, 0)),
                  pl.BlockSpec((1, d), lambda i, k: (0, 0))],
        out_specs=pl.BlockSpec((tm, d), lambda i, k: (i, 0)),
        out_shape=jax.ShapeDtypeStruct((m, d), F32),
        scratch_shapes=[pltpu.VMEM((tm, d), BF16), pltpu.VMEM((tm, d), F32)],
        compiler_params=_cp("parallel", "arbitrary"),
        name="mlp_ln",
    )(x_all, ada_r, ada_r, ada_r, w1, w2, lw.reshape(1, d), lb.reshape(1, d))


def _gdn_conv_kernel(nrows, x_ref, w_ref, *rest):
    o_ref, pad_ref = rest[-2:]
    j = pl.program_id(1)
    pad_ref[0:8, :] = jnp.zeros((8, LANE), F32)
    pad_ref[nrows + 8:nrows + 16, :] = jnp.zeros((8, LANE), F32)
    pad_ref[8:nrows + 8, :] = x_ref[...]
    w = w_ref[...]
    rb = min(nrows, 256)
    qscale = jnp.where(j < N_HEADS, HEAD_DIM ** -0.5, 1.0).astype(F32)

    p0 = 8 - GDN_CONV // 2
    for r0 in range(0, nrows, rb):
        y = pad_ref[r0 + p0:r0 + p0 + rb, :] * w[0:1, :]
        for i in range(1, GDN_CONV):
            y = y + pad_ref[r0 + p0 + i:r0 + p0 + i + rb, :] * w[i:i + 1, :]
        y = _silu(y)
        nrm = y * lax.rsqrt(jnp.sum(y * y, axis=-1, keepdims=True) + RMS_EPS) * qscale
        o_ref[r0:r0 + rb, :] = jnp.where(j < 2 * N_HEADS, nrm, y)


def _gdn_conv(z, conv_w, nrows, row_blk0, nb, prev=None):
    nblk = 3 * N_HEADS
    in_specs = [pl.BlockSpec((nrows, LANE), lambda b, j: (row_blk0 + b, CB_GQKV + j)),
                pl.BlockSpec((GDN_CONV, LANE), lambda b, j: (0, j))]
    args = [z, conv_w]
    if prev is not None:
        in_specs.append(pl.BlockSpec(memory_space=pl.ANY))
        args.append(prev)
    return pl.pallas_call(
        functools.partial(_gdn_conv_kernel, nrows),
        grid=(nb, nblk),
        in_specs=in_specs,
        out_specs=pl.BlockSpec((nrows, LANE), lambda b, j: (row_blk0 + b, j)),
        out_shape=jax.ShapeDtypeStruct((z.shape[0], nblk * LANE), F32),
        scratch_shapes=[pltpu.VMEM((nrows + 16, LANE), F32)],
        input_output_aliases={} if prev is None else {2: 0},
        compiler_params=_cp("parallel", "parallel"),
        name="gdn_conv",
    )(*args)


def _gdn_gates_kernel(tm, s_ref, alog_ref, dtb_ref, o_ref):
    s = s_ref[...]
    g = -jnp.exp(alog_ref[...]) * (jnp.maximum(s + dtb_ref[...], 0.0)
                                    + jnp.log1p(jnp.exp(-jnp.abs(s + dtb_ref[...]))))
    r = lax.broadcasted_iota(jnp.int32, (tm, tm), 0)
    c = lax.broadcasted_iota(jnp.int32, (tm, tm), 1)
    same = (r >> CHUNK_SHIFT) == (c >> CHUNK_SHIFT)
    lo = jnp.where(same & (c <= r), 1.0, 0.0).astype(F32)
    up = jnp.where(same & (c >= r), 1.0, 0.0).astype(F32)
    cum_f = _dot(lo, g, precision=HIGHEST)
    cum_b = _dot(up, g, precision=HIGHEST)
    col = lax.broadcasted_iota(jnp.int32, s.shape, 1)
    o_ref[...] = jnp.where(col < N_HEADS, cum_f,
                           jnp.where(col < 2 * N_HEADS, cum_b,
                                     jnp.where(col < 4 * N_HEADS, jax.nn.sigmoid(s), 0.0)))


def _gdn_gates(z, a_log, dt_bias, tm=512):
    m = z.shape[0]
    pad = lambda v: jnp.zeros((1, LANE), F32).at[0, :2 * N_HEADS].set(v.reshape(-1).astype(F32))
    return pl.pallas_call(
        functools.partial(_gdn_gates_kernel, tm),
        grid=(m // tm,),
        in_specs=[pl.BlockSpec((tm, LANE), lambda i: (i, CB_GAB)),
                  pl.BlockSpec((1, LANE), lambda i: (0, 0)),
                  pl.BlockSpec((1, LANE), lambda i: (0, 0))],
        out_specs=pl.BlockSpec((tm, LANE), lambda i: (i, 0)),
        out_shape=jax.ShapeDtypeStruct((m, LANE), F32),
        compiler_params=_cp("parallel"),
        name="gdn_gates",
    )(z, pad(a_log), pad(dt_bias))


def _tri_masks(n):
    r = lax.broadcasted_iota(jnp.int32, (n, n), 0)
    c = lax.broadcasted_iota(jnp.int32, (n, n), 1)
    return r, c


def _dot3(a, b):
    ah, bh = a.astype(BF16), b.astype(BF16)
    al, bl = (a - ah.astype(F32)).astype(BF16), (b - bh.astype(F32)).astype(BF16)
    return _dot(ah, bh) + (_dot(ah, bl) + _dot(al, bh))


def _gdn_intra(rev, q, k, v, cum_c, cum_r, beta_c):
    C = CHUNK
    r, c = _tri_masks(C)
    incl = (c >= r) if rev else (c <= r)
    strict = (c > r) if rev else (c < r)
    last = cum_r[:, 0:1] if rev else cum_r[:, C - 1:C]
    decay = jnp.exp(jnp.where(incl, cum_c - cum_r, NEG))
    kb = k * beta_c
    kbf = k.astype(BF16)
    a = jnp.where(strict, _dot_nt(kb.astype(BF16), kbf) * decay, 0.0)
    t = jnp.where(r == c, 1.0, 0.0).astype(F32) - a
    p = a
    for _ in range(5):
        p = _dot3(p, p)
        t = t + _dot3(t, p)
    ec = jnp.exp(cum_c)
    sol = _dot3(t, jnp.concatenate([v * beta_c, kb * ec], axis=-1))
    attn = jnp.where(incl, _dot_nt(q.astype(BF16), kbf) * decay, 0.0)
    return (sol[:, :HEAD_DIM], sol[:, HEAD_DIM:].astype(BF16), attn.astype(BF16),
            (q * ec).astype(BF16), (k * jnp.exp(last - cum_c)).astype(BF16), jnp.exp(last))


def _gdn_prep_kernel(qkv_ref, p_ref, ct_ref, u_ref, w_ref, qt_ref, kt_ref, att_ref, el_ref):
    lane = lax.broadcasted_iota(jnp.int32, (CHUNK, LANE), 1)
    sub8 = lax.broadcasted_iota(jnp.int32, (2 * N_HEADS, CHUNK), 0)
    pblk = p_ref[...]
    tblk = ct_ref[0]
    col = lambda idx: jnp.sum(jnp.where(lane == idx, pblk, 0.0), axis=-1, keepdims=True)
    row = lambda idx: jnp.sum(jnp.where(sub8 == idx, tblk, 0.0), axis=0, keepdims=True)
    H = N_HEADS
    for h in range(H):
        hs = slice(h * LANE, (h + 1) * LANE)
        q = qkv_ref[:, h * LANE:(h + 1) * LANE]
        k = qkv_ref[:, (H + h) * LANE:(H + h + 1) * LANE]
        v = qkv_ref[:, (2 * H + h) * LANE:(2 * H + h + 1) * LANE]
        for d in range(2):
            idx = d * H + h
            u, w, attn, qt, kt, el = _gdn_intra(d == 1, q, k, v, col(idx), row(idx), col(2 * H + idx))
            u_ref[d, :, hs] = u
            w_ref[d, :, hs] = w
            qt_ref[d, :, hs] = qt
            kt_ref[d, :, hs] = kt
            att_ref[d, h] = attn
            el_ref[0, idx:idx + 1, :] = jnp.broadcast_to(el, (1, LANE))


def _gdn_prep(qkv, p, ct):
    m = qkv.shape[0]
    H = N_HEADS
    gw = GROUP_WIDTH
    dspec = pl.BlockSpec((2, CHUNK, gw), lambda i: (0, i, 0))
    return pl.pallas_call(
        _gdn_prep_kernel,
        grid=(m // CHUNK,),
        in_specs=[pl.BlockSpec((CHUNK, 3 * gw), lambda i: (i, 0)),
                  pl.BlockSpec((CHUNK, LANE), lambda i: (i, 0)),
                  pl.BlockSpec((1, 2 * H, CHUNK), lambda i: (i, 0, 0))],
        out_specs=[dspec, dspec, dspec, dspec,
                   pl.BlockSpec((2, H, CHUNK, CHUNK), lambda i: (0, 0, i, 0)),
                   pl.BlockSpec((1, 2 * H, LANE), lambda i: (i, 0, 0))],
        out_shape=[jax.ShapeDtypeStruct((2, m, gw), F32),
                   jax.ShapeDtypeStruct((2, m, gw), BF16),
                   jax.ShapeDtypeStruct((2, m, gw), BF16),
                   jax.ShapeDtypeStruct((2, m, gw), BF16),
                   jax.ShapeDtypeStruct((2, H, m, CHUNK), BF16),
                   jax.ShapeDtypeStruct((m // CHUNK, 2 * H, LANE), F32)],
        compiler_params=_cp("parallel"),
        name="gdn_prep",
    )(qkv, p, ct)


def _gdn_scan_kernel(g, *refs):
    ins, (of_ref, ob_ref, s_ref) = refs[:12], refs[12:]
    H = N_HEADS

    @pl.when(pl.program_id(1) == 0)
    def _():
        s_ref[...] = jnp.zeros_like(s_ref)

    for j in range(g):
        for d in range(2):
            u_ref, w_ref, qt_ref, kt_ref, att_ref, el_ref = ins[d::2]
            o_ref = ob_ref if d else of_ref
            cj = g - 1 - j if d else j
            rs = slice(cj * CHUNK, (cj + 1) * CHUNK)
            for h in range(H):
                hs = slice(h * LANE, (h + 1) * LANE)
                idx = d * H + h
                S = s_ref[idx]
                sb = S.astype(BF16)
                v_new = u_ref[0, rs, hs] - _dot(w_ref[0, rs, hs], sb)
                vnb = v_new.astype(BF16)
                s_ref[idx] = S * el_ref[cj, idx:idx + 1, :] + _dot_tn(kt_ref[0, rs, hs], vnb)
                o_ref[rs, hs] = _dot(qt_ref[0, rs, hs], sb) + _dot(att_ref[0, h, rs, :], vnb)


def _gdn_scan(u, w, qt, kt, att, el, nb, seq, ctx_len):
    m = u.shape[1]
    H = N_HEADS
    gw = GROUP_WIDTH
    g = ctx_len // CHUNK
    nblk = seq // ctx_len
    cblk0 = nb * nblk
    blk_f = lambda b, t: jnp.where(t == 0, cblk0 + b, b * nblk + t - 1)
    blk_b = lambda b, t: jnp.where(t == 0, cblk0 + b, b * nblk + nblk - t)
    in_specs, args = [], []
    for arr in (u, w, qt, kt):
        in_specs += [pl.BlockSpec((1, ctx_len, gw), lambda b, t: (0, blk_f(b, t), 0)),
                     pl.BlockSpec((1, ctx_len, gw), lambda b, t: (1, blk_b(b, t), 0))]
        args += [arr, arr]
    in_specs += [pl.BlockSpec((1, H, ctx_len, CHUNK), lambda b, t: (0, 0, blk_f(b, t), 0)),
                 pl.BlockSpec((1, H, ctx_len, CHUNK), lambda b, t: (1, 0, blk_b(b, t), 0)),
                 pl.BlockSpec((g, 2 * H, LANE), lambda b, t: (blk_f(b, t), 0, 0)),
                 pl.BlockSpec((g, 2 * H, LANE), lambda b, t: (blk_b(b, t), 0, 0))]
    args += [att, att, el, el]
    return pl.pallas_call(
        functools.partial(_gdn_scan_kernel, g),
        grid=(nb, nblk + 1),
        in_specs=in_specs,
        out_specs=[pl.BlockSpec((ctx_len, gw), lambda b, t: (blk_f(b, t), 0)),
                   pl.BlockSpec((ctx_len, gw), lambda b, t: (blk_b(b, t), 0))],
        out_shape=[jax.ShapeDtypeStruct((m, gw), F32), jax.ShapeDtypeStruct((m, gw), F32)],
        scratch_shapes=[pltpu.VMEM((2 * H, HEAD_DIM, HEAD_DIM), F32)],
        compiler_params=_cp("parallel", "arbitrary"),
        name="gdn_scan",
    )(*args)


def _head_norm_gate(o, nw, gate):
    o = o * lax.rsqrt(jnp.mean(o * o, axis=-1, keepdims=True) + RMS_EPS) * nw
    return o * _silu(gate)


def _combine_kernel(of_ref, ob_ref, g_ref, nw_ref, y_ref):
    nw = nw_ref[...]
    for h in range(N_HEADS):
        hs = slice(h * LANE, (h + 1) * LANE)
        y_ref[:, hs] = _head_norm_gate(of_ref[:, hs] + ob_ref[:, hs], nw, g_ref[:, hs]).astype(y_ref.dtype)


def _combine(o_f, o_b, z, gate_cb, norm_w, m_out, tm=512):
    gw = GROUP_WIDTH
    spec = pl.BlockSpec((tm, gw), lambda i: (i, 0))
    return pl.pallas_call(
        _combine_kernel,
        grid=(m_out // tm,),
        in_specs=[spec, spec, pl.BlockSpec((tm, gw), lambda i: (i, gate_cb * LANE // gw)),
                  pl.BlockSpec((1, LANE), lambda i: (0, 0))],
        out_specs=spec,
        out_shape=jax.ShapeDtypeStruct((m_out, gw), BF16),
        compiler_params=_cp("parallel"),
        name="combine",
    )(o_f, o_b, z, norm_w.reshape(1, LANE))


def _gla_chunk(rev, zq, zf, zi, lb, tri, St):
    C, nsub = CHUNK, CHUNK // SUB
    q = _silu(zq) * HEAD_DIM ** -0.5
    f = lb + (1.0 - lb) * jax.nn.sigmoid(zf)
    k = 1.0 - f
    cum = _dot(tri, jnp.log(f), precision=HIGHEST)
    last = cum[0:1, :] if rev else cum[C - 1:C, :]
    vb = zi.astype(BF16)
    o = _dot_nt((q * jnp.exp(cum)).astype(BF16), St.astype(BF16))
    st_new = St * jnp.exp(last) + _dot_tn(vb, (k * jnp.exp(last - cum)).astype(BF16))
    tsub = lax.broadcasted_iota(jnp.int32, (SUB, HEAD_DIM), 0)
    trow = lax.broadcasted_iota(jnp.int32, (C, HEAD_DIM), 0)
    scol = lax.broadcasted_iota(jnp.int32, (SUB, C), 1)
    blocks = []
    for a in range(nsub):
        sa = slice(a * SUB, (a + 1) * SUB)
        qa, ka, ca = q[sa], k[sa], cum[sa]
        if rev and a < nsub - 1:
            cb = cum[(a + 1) * SUB:(a + 1) * SUB + 1, :]
            kt = k * jnp.exp(jnp.where(trow >= (a + 1) * SUB, cb - cum, NEG))
            sc = _dot_nt((qa * jnp.exp(ca - cb)).astype(BF16), kt.astype(BF16))
        elif (not rev) and a > 0:
            cb = cum[a * SUB - 1:a * SUB, :]
            kt = k * jnp.exp(jnp.where(trow < a * SUB, cb - cum, NEG))
            sc = _dot_nt((qa * jnp.exp(ca - cb)).astype(BF16), kt.astype(BF16))
        else:
            sc = jnp.zeros((SUB, C), F32)
        for j in range(SUB):
            ok = (tsub <= j) if rev else (tsub >= j)
            dec = jnp.exp(jnp.where(ok, ca - ca[j:j + 1, :], NEG))
            sj = jnp.sum(qa * ka[j:j + 1, :] * dec, axis=-1, keepdims=True)
            sc = jnp.where(scol == a * SUB + j, sj, sc)
        blocks.append(sc)
    scores = jnp.concatenate(blocks, axis=0)
    return o + _dot(scores.astype(BF16), vb), st_new


def _gla_scan_kernel(emit_ctx, n_lat, n_ctx, *refs):
    (ql_ref, ffl_ref, fbl_ref, il_ref, gl_ref,
     qc_ref, ffc_ref, fbc_ref, ic_ref, gc_ref, lb_ref, nw_ref) = refs[:12]
    if emit_ctx:
        yl_ref, yc_ref, sf_ref, sb_ref, of_ref, ob_ref, ocf_ref, ocb_ref = refs[12:]
    else:
        yl_ref, sf_ref, sb_ref, of_ref, ob_ref = refs[12:]
        yc_ref = ocf_ref = ocb_ref = None
    r, c = _tri_masks(CHUNK)
    lo = jnp.where(c <= r, 1.0, 0.0).astype(F32)
    up = jnp.where(c >= r, 1.0, 0.0).astype(F32)
    lb = lb_ref[0]

    def step(rev, q_ref, f_ref, i_ref, s_ref, o_ref, ci):
        sl = pl.ds(pl.multiple_of(ci * CHUNK, CHUNK), CHUNK)
        o, s_new = _gla_chunk(rev, q_ref[sl, :], f_ref[sl, :], i_ref[sl, :], lb, up if rev else lo, s_ref[...])
        s_ref[...] = s_new
        if o_ref is not None:
            o_ref[sl, :] = o

    sf_ref[...] = jnp.zeros_like(sf_ref)
    sb_ref[...] = jnp.zeros_like(sb_ref)

    def ctx_body(i, carry):
        step(False, qc_ref, ffc_ref, ic_ref, sf_ref, ocf_ref, i)
        step(True, qc_ref, fbc_ref, ic_ref, sb_ref, ocb_ref, n_ctx - 1 - i)
        return carry

    lax.fori_loop(0, n_ctx, ctx_body, 0)

    def lat_body(i, carry):
        step(False, ql_ref, ffl_ref, il_ref, sf_ref, of_ref, i)
        step(True, ql_ref, fbl_ref, il_ref, sb_ref, ob_ref, n_lat - 1 - i)
        return carry

    lax.fori_loop(0, n_lat, lat_body, 0)

    nw = nw_ref[...]
    rb = 256

    def fin(o1, o2, g_ref, y_ref, n):
        def body(t, carry):
            sl = pl.ds(pl.multiple_of(t * rb, rb), rb)
            y_ref[sl, :] = _head_norm_gate(o1[sl, :] + o2[sl, :], nw, g_ref[sl, :]).astype(y_ref.dtype)
            return carry
        lax.fori_loop(0, n // rb, body, 0)

    fin(of_ref, ob_ref, gl_ref, yl_ref, n_lat * CHUNK)
    if emit_ctx:
        fin(ocf_ref, ocb_ref, gc_ref, yc_ref, n_ctx * CHUNK)


def _gla_scan(z, lbs, norm_w, nb, seq, ctx_len, emit_ctx):
    n_lat, n_ctx = seq // CHUNK, ctx_len // CHUNK
    cblk0 = nb * seq // ctx_len
    lat = lambda cb: pl.BlockSpec((seq, LANE), lambda b, h: (b, cb + h))
    ctx = lambda cb: pl.BlockSpec((ctx_len, LANE), lambda b, h: (cblk0 + b, cb + h))
    cbs = (CB_HQ, CB_HFF, CB_HFB, CB_HI, CB_HG)
    in_specs = ([lat(cb) for cb in cbs] + [ctx(cb) for cb in cbs]
                + [pl.BlockSpec((1, 1, LANE), lambda b, h: (h, 0, 0)),
                   pl.BlockSpec((1, LANE), lambda b, h: (0, 0))])
    out_specs = [pl.BlockSpec((seq, LANE), lambda b, h: (b, h))]
    out_shape = [jax.ShapeDtypeStruct((nb * seq, GROUP_WIDTH), BF16)]
    scratch = [pltpu.VMEM((HEAD_DIM, HEAD_DIM), F32), pltpu.VMEM((HEAD_DIM, HEAD_DIM), F32),
               pltpu.VMEM((seq, LANE), F32), pltpu.VMEM((seq, LANE), F32)]
    if emit_ctx:
        out_specs.append(pl.BlockSpec((ctx_len, LANE), lambda b, h: (b, h)))
        out_shape.append(jax.ShapeDtypeStruct((nb * ctx_len, GROUP_WIDTH), BF16))
        scratch += [pltpu.VMEM((ctx_len, LANE), F32), pltpu.VMEM((ctx_len, LANE), F32)]
    return pl.pallas_call(
        functools.partial(_gla_scan_kernel, emit_ctx, n_lat, n_ctx),
        grid=(nb, N_HEADS),
        in_specs=in_specs, out_specs=out_specs, out_shape=out_shape, scratch_shapes=scratch,
        compiler_params=_cp("parallel", "parallel"),
        name="hgrn_scan",
    )(*([z] * 10), lbs.reshape(N_HEADS, 1, LANE), norm_w.reshape(1, LANE))


def _softmax_pv(parts):
    m = parts[0][0].max(axis=-1, keepdims=True)
    for s, _ in parts[1:]:
        m = jnp.maximum(m, s.max(axis=-1, keepdims=True))
    den, acc = None, None
    for s, v in parts:
        p = jnp.exp(s - m)
        d = jnp.sum(p, axis=-1, keepdims=True)
        a = _dot(p.astype(BF16), v)
        den = d if den is None else den + d
        acc = a if acc is None else acc + a
    return acc / den


def _na_kernel(emit_ctx, rows, *refs):
    q_ref, k_ref, v_ref, qc_ref, kc_ref, vc_ref, bias_ref = refs[:7]
    if emit_ctx:
        yl_ref, yc_ref, kb_ref, vb_ref = refs[7:]
    else:
        yl_ref, kb_ref, vb_ref = refs[7:]
    scale = HEAD_DIM ** -0.5
    win = NA_ROWS * GRID_W
    kb_ref[...] = k_ref[...].astype(BF16)
    vb_ref[...] = v_ref[...].astype(BF16)
    kc = kc_ref[...].astype(BF16)
    vc = vc_ref[...].astype(BF16)

    def body(r, carry):
        row0 = jnp.clip(r - NA_ROWS // 2, 0, rows - NA_ROWS)
        q = q_ref[pl.ds(pl.multiple_of(r * GRID_W, GRID_W), GRID_W), :].astype(BF16)
        ks = pl.ds(pl.multiple_of(row0 * GRID_W, GRID_W), win)
        s_win = _dot_nt(q, kb_ref[ks, :]) * scale + bias_ref[0, r - row0]
        s_ctx = _dot_nt(q, kc) * scale
        o = _softmax_pv([(s_win, vb_ref[ks, :]), (s_ctx, vc)])
        yl_ref[pl.ds(pl.multiple_of(r * GRID_W, GRID_W), GRID_W), :] = o.astype(yl_ref.dtype)
        return carry

    lax.fori_loop(0, rows, body, 0)
    if emit_ctx:
        s = _dot_nt(qc_ref[...].astype(BF16), kc) * scale
        yc_ref[...] = _softmax_pv([(s, vc)]).astype(yc_ref.dtype)


def _na_bias_kernel(rpb_ref, o_ref):
    n = lax.broadcasted_iota(jnp.int32, (LANE, GRID_W * GRID_W), 1)
    j = lax.broadcasted_iota(jnp.int32, (LANE, GRID_W * GRID_W), 0)
    q, w = n >> 6, n & (GRID_W - 1)
    dc = jnp.clip(w - q, 1 - NA_COLS, NA_COLS - 1) + NA_COLS - 1
    onehot = jnp.where(dc == j, 1.0, 0.0).astype(F32)
    m = _dot(rpb_ref[...], onehot, precision=HIGHEST)
    c0 = jnp.clip(q[0:1] - NA_COLS // 2, 0, GRID_W - NA_COLS)
    ok = (w[0:1] >= c0) & (w[0:1] < c0 + NA_COLS)
    o_ref[...] = jnp.where(ok, m, NEG)


def _na_bias_tables(rpb):
    depth, H, nr, nc = rpb.shape
    assert GRID_W == 64 and depth * H * nr <= LANE and nc <= LANE
    flat = jnp.zeros((LANE, LANE), F32).at[:depth * H * nr, :nc].set(rpb.reshape(-1, nc).astype(F32))
    m = pl.pallas_call(
        _na_bias_kernel,
        out_shape=jax.ShapeDtypeStruct((LANE, GRID_W * GRID_W), F32),
        compiler_params=pltpu.CompilerParams(vmem_limit_bytes=VMEM_LIMIT),
        name="na_bias",
    )(flat)
    m = m[:depth * H * nr].reshape(depth, H, nr, GRID_W, GRID_W)
    tab = jnp.stack([jnp.stack([m[:, :, k - s + NA_ROWS - 1] for k in range(NA_ROWS)], axis=3)
                     for s in range(NA_ROWS)], axis=2)
    return tab.reshape(depth, H, NA_ROWS, GRID_W, NA_ROWS * GRID_W)


def _na(z, bias, nb, seq, ctx_len, emit_ctx):
    rows = seq // GRID_W
    cblk0 = nb * seq // ctx_len
    H = N_HEADS
    lat = lambda cb: pl.BlockSpec((seq, LANE), lambda b, h: (b, cb + h))
    ctx = lambda cb: pl.BlockSpec((ctx_len, LANE), lambda b, h: (cblk0 + b, cb + h))
    win = NA_ROWS * GRID_W
    in_specs = [lat(CB_NAQ), lat(CB_NAK), lat(CB_NAV), ctx(CB_NAQ), ctx(CB_NAK), ctx(CB_NAV),
                pl.BlockSpec((1, NA_ROWS, GRID_W, win), lambda b, h: (h, 0, 0, 0))]
    out_specs = [pl.BlockSpec((seq, LANE), lambda b, h: (b, h))]
    out_shape = [jax.ShapeDtypeStruct((nb * seq, GROUP_WIDTH), BF16)]
    if emit_ctx:
        out_specs.append(pl.BlockSpec((ctx_len, LANE), lambda b, h: (b, h)))
        out_shape.append(jax.ShapeDtypeStruct((nb * ctx_len, GROUP_WIDTH), BF16))
    return pl.pallas_call(
        functools.partial(_na_kernel, emit_ctx, rows),
        grid=(nb, H),
        in_specs=in_specs, out_specs=out_specs, out_shape=out_shape,
        scratch_shapes=[pltpu.VMEM((seq, LANE), BF16), pltpu.VMEM((seq, LANE), BF16)],
        compiler_params=_cp("parallel", "parallel"),
        name="na_attn",
    )(z, z, z, z, z, z, bias)


def _rms(x, w):
    return x * lax.rsqrt(jnp.mean(x * x, axis=-1, keepdims=True) + RMS_EPS) * w


def _mla_prep_kernel(cq_ref, ckv_ref, kra_ref, krb_ref, cc_ref, ss_ref, qnw_ref, kvnw_ref, wuq_ref, wukv_ref,
                     q_ref, kn_ref, kr_ref, v_ref):
    cc, ss = cc_ref[...], ss_ref[...]
    qn = _rms(cq_ref[...], qnw_ref[...]).astype(BF16)
    qa = _dot(qn, wuq_ref[...])
    for h in range(N_HEADS):
        b = 3 * LANE * h
        q_ref[:, 2 * LANE * h:2 * LANE * h + LANE] = qa[:, b:b + LANE].astype(BF16)
        q_ref[:, 2 * LANE * h + LANE:2 * LANE * (h + 1)] = (
            qa[:, b + LANE:b + 2 * LANE] * cc + qa[:, b + 2 * LANE:b + 3 * LANE] * ss).astype(BF16)
    kvn = _rms(ckv_ref[...], kvnw_ref[...]).astype(BF16)
    kv = _dot(kvn, wukv_ref[...])
    kn_ref[...] = kv[:, :GROUP_WIDTH].astype(BF16)
    v_ref[...] = kv[:, GROUP_WIDTH:].astype(BF16)
    kr_ref[...] = (kra_ref[...] * cc + krb_ref[...] * ss).astype(BF16)


def _mla_prep(z, cc, ss, qnw, kvnw, wuq, wukv, n_lat, seq, tm=512):
    m = z.shape[0]
    nlt, spt = n_lat // tm, seq // tm
    tab = lambda i: jnp.where(i < nlt, i % spt, spt)
    H = N_HEADS
    return pl.pallas_call(
        _mla_prep_kernel,
        grid=(m // tm,),
        in_specs=[pl.BlockSpec((tm, MLA_Q_RANK), lambda i: (i, CB_MQ * LANE // MLA_Q_RANK)),
                  pl.BlockSpec((tm, MLA_KV_RANK), lambda i: (i, CB_MKV * LANE // MLA_KV_RANK)),
                  pl.BlockSpec((tm, LANE), lambda i: (i, CB_MKRA)),
                  pl.BlockSpec((tm, LANE), lambda i: (i, CB_MKRB)),
                  pl.BlockSpec((tm, LANE), lambda i: (tab(i), 0)),
                  pl.BlockSpec((tm, LANE), lambda i: (tab(i), 0)),
                  pl.BlockSpec((1, MLA_Q_RANK), lambda i: (0, 0)),
                  pl.BlockSpec((1, MLA_KV_RANK), lambda i: (0, 0)),
                  pl.BlockSpec(wuq.shape, lambda i: (0, 0)),
                  pl.BlockSpec(wukv.shape, lambda i: (0, 0))],
        out_specs=[pl.BlockSpec((tm, 2 * LANE * H), lambda i: (i, 0)),
                   pl.BlockSpec((tm, GROUP_WIDTH), lambda i: (i, 0)),
                   pl.BlockSpec((tm, LANE), lambda i: (i, 0)),
                   pl.BlockSpec((tm, GROUP_WIDTH), lambda i: (i, 0))],
        out_shape=[jax.ShapeDtypeStruct((m, 2 * LANE * H), BF16),
                   jax.ShapeDtypeStruct((m, GROUP_WIDTH), BF16),
                   jax.ShapeDtypeStruct((m, LANE), BF16),
                   jax.ShapeDtypeStruct((m, GROUP_WIDTH), BF16)],
        compiler_params=_cp("parallel"),
        name="mla_prep",
    )(z, z, z, z, cc, ss, qnw.reshape(1, -1), kvnw.reshape(1, -1), wuq, wukv)


def _mla_attn_kernel(with_lat, seq, *refs):
    if with_lat:
        q_ref, knl_ref, krl_ref, vl_ref, knc_ref, krc_ref, vc_ref, y_ref, k_scr = refs
    else:
        q_ref, knc_ref, krc_ref, vc_ref, y_ref, k_scr = refs
    scale = MLA_QK_DIM ** -0.5
    nk = k_scr.shape[0]

    @pl.when(pl.program_id(2) == 0)
    def _():
        if with_lat:
            k_scr[0:seq, 0:LANE] = knl_ref[...]
            k_scr[0:seq, LANE:2 * LANE] = krl_ref[...]
        k_scr[nk - knc_ref.shape[0]:nk, 0:LANE] = knc_ref[...]
        k_scr[nk - knc_ref.shape[0]:nk, LANE:2 * LANE] = krc_ref[...]

    q = q_ref[...]
    parts = []
    if with_lat:
        parts.append((_dot_nt(q, k_scr[0:seq, :]) * scale, vl_ref[...]))
    parts.append((_dot_nt(q, k_scr[nk - knc_ref.shape[0]:nk, :]) * scale, vc_ref[...]))
    y_ref[...] = _softmax_pv(parts).astype(y_ref.dtype)


def _mla_attn(q, kn, kr, v, nb, seq, ctx_len, with_lat, tq=256):
    H = N_HEADS
    cblk0 = nb * seq // ctx_len
    nq = seq if with_lat else ctx_len
    tq = min(tq, nq)
    qblk0 = 0 if with_lat else nb * seq // tq
    ctxs = [pl.BlockSpec((ctx_len, LANE), lambda b, h, i: (cblk0 + b, h)),
            pl.BlockSpec((ctx_len, LANE), lambda b, h, i: (cblk0 + b, 0)),
            pl.BlockSpec((ctx_len, LANE), lambda b, h, i: (cblk0 + b, h))]
    lats = [pl.BlockSpec((seq, LANE), lambda b, h, i: (b, h)),
            pl.BlockSpec((seq, LANE), lambda b, h, i: (b, 0)),
            pl.BlockSpec((seq, LANE), lambda b, h, i: (b, h))]
    in_specs = [pl.BlockSpec((tq, 2 * LANE), lambda b, h, i: (qblk0 + b * (nq // tq) + i, h))]
    args = [q]
    if with_lat:
        in_specs += lats
        args += [kn, kr, v]
    in_specs += ctxs
    args += [kn, kr, v]
    nk = (seq if with_lat else 0) + ctx_len
    return pl.pallas_call(
        functools.partial(_mla_attn_kernel, with_lat, seq),
        grid=(nb, H, nq // tq),
        in_specs=in_specs,
        out_specs=pl.BlockSpec((tq, LANE), lambda b, h, i: (b * (nq // tq) + i, h)),
        out_shape=jax.ShapeDtypeStruct((nb * nq, GROUP_WIDTH), BF16),
        scratch_shapes=[pltpu.VMEM((nk, 2 * LANE), BF16)],
        compiler_params=_cp("parallel", "parallel", "arbitrary"),
        name="mla_attn_lat" if with_lat else "mla_attn_ctx",
    )(*args)


def _prep_w_in(w_in):
    d = w_in.shape[0]
    gw = GROUP_WIDTH
    o_na = 4 * gw + 4 * N_HEADS
    o_mla = o_na + 3 * gw
    o_kr = o_mla + MLA_Q_RANK + MLA_KV_RANK
    o_hg = o_kr + MLA_ROPE
    kr = w_in[:, o_kr:o_hg]
    k1, k2 = kr[:, 0::2], kr[:, 1::2]
    z = lambda n: jnp.zeros((d, n), w_in.dtype)
    o_kv = o_mla + MLA_Q_RANK
    cols = [w_in[:, o_mla:o_kv], k1, k2, z(LANE - MLA_ROPE),
            w_in[:, o_kv:o_kr], k2, k1, z(LANE - MLA_ROPE),
            w_in[:, 4 * gw:o_na], z(LANE - 4 * N_HEADS),
            w_in[:, :4 * gw], w_in[:, o_na:o_mla], w_in[:, o_hg:]]
    w = jnp.concatenate(cols, axis=1).astype(BF16)
    assert w.shape[1] == NP_IN
    return w


def _prep_w_uq(w_uq):
    r = w_uq.shape[0]
    z = jnp.zeros((r, LANE - MLA_ROPE), w_uq.dtype)
    cols = []
    for h in range(N_HEADS):
        wh = w_uq[:, h * MLA_QK_DIM:(h + 1) * MLA_QK_DIM]
        rope = wh[:, MLA_NOPE:]
        r1, r2 = rope[:, 0::2], rope[:, 1::2]
        cols += [wh[:, :MLA_NOPE], r1, r2, z, r2, r1, z]
    return jnp.concatenate(cols, axis=1).astype(BF16)


def _rope_tables(seq, tm):
    n_freq = MLA_ROPE // 4
    freqs = ROPE_BASE ** (-jnp.arange(n_freq, dtype=F32) / n_freq)
    t = jnp.arange(seq)
    ang = jnp.concatenate([(t // GRID_W).astype(F32)[:, None] * freqs,
                           (t % GRID_W).astype(F32)[:, None] * freqs], -1)
    cos, sin = jnp.cos(ang), jnp.sin(ang)
    zp = jnp.zeros((seq, LANE - MLA_ROPE), F32)
    cc = jnp.concatenate([cos, cos, zp], axis=1)
    ss = jnp.concatenate([-sin, sin, zp], axis=1)
    ident = jnp.zeros((tm, LANE), F32).at[:, :MLA_ROPE].set(1.0)
    return jnp.concatenate([cc, ident], axis=0), jnp.concatenate([ss, jnp.zeros((tm, LANE), F32)], axis=0)


def kernel(x, c, ctx, c_ctx, w_ada, b_ada, w_in, gdn_conv_w, gdn_a_log, gdn_dt_bias, gdn_norm_w, na_rpb,
           mla_q_norm_w, mla_kv_norm_w, mla_w_uq, mla_w_uk, mla_w_uv, hgrn_lower_bounds, hgrn_norm_w, w_out,
           ln1_w, ln1_b, w_mlp1, w_mlp2, ln2_w, ln2_b):
    nb, seq, d = x.shape
    ctx_len = ctx.shape[1]
    depth = w_ada.shape[0]
    n_lat, n_ctx = nb * seq, nb * ctx_len
    alpha = (2 * depth) ** 0.25
    tm = 512
    assert nb < 8 and seq % tm == 0 and n_ctx % tm == 0 and seq % ctx_len == 0 and ctx_len % CHUNK == 0

    cin = jnp.zeros((8, d), F32).at[:nb].set(c).at[nb].set(c_ctx)
    ada = _ada(cin, w_ada, b_ada)
    p_lb = jax.nn.softmax(hgrn_lower_bounds.astype(F32), axis=0)
    lbs = jnp.cumsum(p_lb, axis=0) - p_lb[0]
    cc, ss = _rope_tables(seq, tm)
    na_bias = _na_bias_tables(na_rpb)

    x_all = jnp.concatenate([x.reshape(n_lat, d), ctx.reshape(n_ctx, d)], axis=0)
    for l in range(depth):
        emit_ctx = l < depth - 1
        ada_r = ada[l].reshape(8 * 6, 1, d)
        z = _inproj(x_all, ada_r, _prep_w_in(w_in[l]), n_lat, seq, nb, tm=tm)

        qkv = _gdn_conv(z, gdn_conv_w[l], seq, 0, nb)
        qkv = _gdn_conv(z, gdn_conv_w[l], ctx_len, n_lat // ctx_len, nb, prev=qkv)
        p = _gdn_gates(z, gdn_a_log[l], gdn_dt_bias[l], tm=tm)
        ct = p[:, :2 * N_HEADS].reshape(-1, CHUNK, 2 * N_HEADS).transpose(0, 2, 1)
        o_f, o_b = _gdn_scan(*_gdn_prep(qkv, p, ct), nb, seq, ctx_len)
        m_out = n_lat + n_ctx if emit_ctx else n_lat
        ya = _combine(o_f, o_b, z, CB_GGATE, gdn_norm_w[l], m_out, tm=tm)
        yb = _na(z, na_bias[l], nb, seq, ctx_len, emit_ctx)
        q, kn, kr, v = _mla_prep(z, cc, ss, mla_q_norm_w[l], mla_kv_norm_w[l], _prep_w_uq(mla_w_uq[l]),
                                 jnp.concatenate([mla_w_uk[l], mla_w_uv[l]], axis=1).astype(BF16),
                                 n_lat, seq, tm=tm)
        ym = [_mla_attn(q, kn, kr, v, nb, seq, ctx_len, True)]
        if emit_ctx:
            ym.append(_mla_attn(q, kn, kr, v, nb, seq, ctx_len, False))
        yh = _gla_scan(z, lbs[l], hgrn_norm_w[l], nb, seq, ctx_len, emit_ctx)

        if emit_ctx:
            ys = [ya] + [jnp.concatenate(list(t), axis=0) for t in (yb, ym, yh)]
        else:
            ys = [ya, yb[0], ym[0], yh[0]]
        x_all = _outproj(x_all, ys, w_out[l].astype(BF16), ada_r, ln1_w[l], ln1_b[l], m_out, seq, nb, alpha, tm=tm)
        x_all = _mlp(x_all, w_mlp1[l].astype(BF16), w_mlp2[l].astype(BF16), ada_r, ln2_w[l], ln2_b[l],
                     seq, nb, alpha, tm=tm)
    return x_all[:n_lat].reshape(nb, seq, d)
```

```python
import functools

import numpy as np
import jax
import jax.numpy as jnp
from jax import lax
from jax.experimental import pallas as pl
from jax.experimental.pallas import tpu as pltpu

F32 = jnp.float32
BF16 = jnp.bfloat16
HIGHEST = lax.Precision.HIGHEST

GRID_W = 64
N_HEADS = 4
HEAD_DIM = 128
GROUP_WIDTH = 512
CHUNK = 64
SUB = 16
GDN_CONV = 5
NA_ROWS = 8
NA_COLS = 16
MLA_Q_RANK = 384
MLA_KV_RANK = 256
MLA_NOPE = 128
MLA_ROPE = 64
MLA_QK_DIM = MLA_NOPE + MLA_ROPE
ROPE_BASE = 10000.0
LN_EPS = 1e-5
RMS_EPS = 1e-6
NEG = -1e30

LANE = 128
CB_MQ, CB_MKRA, CB_MKV, CB_MKRB, CB_GAB = 0, 3, 4, 6, 7
CB_GQKV, CB_GGATE = 8, 20
CB_NAQ, CB_NAK, CB_NAV = 24, 28, 32
CB_HQ, CB_HFF, CB_HFB, CB_HI, CB_HG = 36, 40, 44, 48, 52
CHUNK_SHIFT = 6
NP_IN = 56 * LANE

VMEM_LIMIT = 48 << 20


def _cp(*sem):
    return pltpu.CompilerParams(dimension_semantics=sem, vmem_limit_bytes=VMEM_LIMIT)


def _silu(x):
    return x * jax.nn.sigmoid(x)


def _dot(a, b, **kw):
    return jnp.dot(a, b, preferred_element_type=F32, **kw)


def _dot_nt(a, b, **kw):
    return lax.dot_general(a, b, (((1,), (1,)), ((), ())), preferred_element_type=F32, **kw)


def _dot_tn(a, b, **kw):
    return lax.dot_general(a, b, (((0,), (0,)), ((), ())), preferred_element_type=F32, **kw)


def _ada_kernel(c_ref, w_ref, b_ref, o_ref):
    s = _silu(c_ref[...])
    o_ref[0] = _dot(s, w_ref[0], precision=HIGHEST) + b_ref[0]


def _ada(cin, w_ada, b_ada):
    depth, d, n = w_ada.shape
    tn = 512
    return pl.pallas_call(
        _ada_kernel,
        grid=(depth, n // tn),
        in_specs=[pl.BlockSpec((8, d), lambda l, j: (0, 0)),
                  pl.BlockSpec((1, d, tn), lambda l, j: (l, 0, j)),
                  pl.BlockSpec((1, 1, tn), lambda l, j: (l, 0, j))],
        out_specs=pl.BlockSpec((1, 8, tn), lambda l, j: (l, 0, j)),
        out_shape=jax.ShapeDtypeStruct((depth, 8, n), F32),
        compiler_params=_cp("parallel", "parallel"),
        name="ada",
    )(cin, w_ada, b_ada.reshape(depth, 1, n))


def _inproj_kernel(x_ref, sh_ref, sc_ref, w_ref, o_ref, xm_ref):
    @pl.when(pl.program_id(1) == 0)
    def _():
        xm_ref[...] = (x_ref[...] * (1.0 + sc_ref[0]) + sh_ref[0]).astype(BF16)
    o_ref[...] = _dot(xm_ref[...], w_ref[...])


def _inproj(x_all, ada_r, w, n_lat, seq, nb, tm=512, tn=1024):
    m, d = x_all.shape
    n = w.shape[1]
    row = lambda i: jnp.minimum((i * tm) // seq, nb)
    return pl.pallas_call(
        _inproj_kernel,
        grid=(m // tm, n // tn),
        in_specs=[pl.BlockSpec((tm, d), lambda i, j: (i, 0)),
                  pl.BlockSpec((1, 1, d), lambda i, j: (row(i) * 6 + 0, 0, 0)),
                  pl.BlockSpec((1, 1, d), lambda i, j: (row(i) * 6 + 1, 0, 0)),
                  pl.BlockSpec((d, tn), lambda i, j: (0, j))],
        out_specs=pl.BlockSpec((tm, tn), lambda i, j: (i, j)),
        out_shape=jax.ShapeDtypeStruct((m, n), F32),
        scratch_shapes=[pltpu.VMEM((tm, d), BF16)],
        compiler_params=_cp("parallel", "arbitrary"),
        name="inproj",
    )(x_all, ada_r, ada_r, w)


def _layernorm(r, w, b):
    mu = jnp.mean(r, axis=-1, keepdims=True)
    rc = r - mu
    var = jnp.mean(rc * rc, axis=-1, keepdims=True)
    return rc * lax.rsqrt(var + LN_EPS) * w + b


def _outproj_kernel(alpha, x_ref, ya_ref, yb_ref, ym_ref, yh_ref, w_ref, g_ref, lw_ref, lb_ref, o_ref):
    gw = GROUP_WIDTH
    acc = _dot(ya_ref[...], w_ref[0:gw, :])
    acc += _dot(yb_ref[...], w_ref[gw:2 * gw, :])
    acc += _dot(ym_ref[...], w_ref[2 * gw:3 * gw, :])
    acc += _dot(yh_ref[...], w_ref[3 * gw:4 * gw, :])
    r = alpha * x_ref[...] + g_ref[0] * acc
    o_ref[...] = _layernorm(r, lw_ref[...], lb_ref[...])


def _outproj(x_all, ys, w, ada_r, lw, lb, m_out, seq, nb, alpha, tm=512):
    d = x_all.shape[1]
    row = lambda i: jnp.minimum((i * tm) // seq, nb)
    yspec = pl.BlockSpec((tm, GROUP_WIDTH), lambda i: (i, 0))
    return pl.pallas_call(
        functools.partial(_outproj_kernel, alpha),
        grid=(m_out // tm,),
        in_specs=[pl.BlockSpec((tm, d), lambda i: (i, 0)), yspec, yspec, yspec, yspec,
                  pl.BlockSpec((d, d), lambda i: (0, 0)),
                  pl.BlockSpec((1, 1, d), lambda i: (row(i) * 6 + 2, 0, 0)),
                  pl.BlockSpec((1, d), lambda i: (0, 0)),
                  pl.BlockSpec((1, d), lambda i: (0, 0))],
        out_specs=pl.BlockSpec((tm, d), lambda i: (i, 0)),
        out_shape=jax.ShapeDtypeStruct((m_out, d), F32),
        compiler_params=_cp("parallel"),
        name="outproj_ln",
    )(x_all, *ys, w, ada_r, lw.reshape(1, d), lb.reshape(1, d))


def _mlp_kernel(alpha, x_ref, sh_ref, sc_ref, g_ref, w1_ref, w2_ref, lw_ref, lb_ref, o_ref, xm_ref, acc_ref):
    k = pl.program_id(1)

    @pl.when(k == 0)
    def _():
        xm_ref[...] = (x_ref[...] * (1.0 + sc_ref[0]) + sh_ref[0]).astype(BF16)
        acc_ref[...] = jnp.zeros_like(acc_ref)

    h = jnp.maximum(_dot(xm_ref[...], w1_ref[...]), 0.0)
    acc_ref[...] += _dot((h * h).astype(BF16), w2_ref[...])

    @pl.when(k == pl.num_programs(1) - 1)
    def _():
        r = alpha * x_ref[...] + g_ref[0] * acc_ref[...]
        o_ref[...] = _layernorm(r, lw_ref[...], lb_ref[...])


def _mlp(x_all, w1, w2, ada_r, lw, lb, seq, nb, alpha, tm=512, th=512):
    m, d = x_all.shape
    hid = w1.shape[1]
    row = lambda i: jnp.minimum((i * tm) // seq, nb)
    return pl.pallas_call(
        functools.partial(_mlp_kernel, alpha),
        grid=(m // tm, hid // th),
        in_specs=[pl.BlockSpec((tm, d), lambda i, k: (i, 0)),
                  pl.BlockSpec((1, 1, d), lambda i, k: (row(i) * 6 + 3, 0, 0)),
                  pl.BlockSpec((1, 1, d), lambda i, k: (row(i) * 6 + 4, 0, 0)),
                  pl.BlockSpec((1, 1, d), lambda i, k: (row(i) * 6 + 5, 0, 0)),
                  pl.BlockSpec((d, th), lambda i, k: (0, k)),
                  pl.BlockSpec((th, d), lambda i, k: (k, 0)),
                  pl.BlockSpec((1, d), lambda i, k: (0, 0)),
                  pl.BlockSpec((1, d), lambda i, k: (0, 0))],
        out_specs=pl.BlockSpec((tm, d), lambda i, k: (i, 0)),
        out_shape=jax.ShapeDtypeStruct((m, d), F32),
        scratch_shapes=[pltpu.VMEM((tm, d), BF16), pltpu.VMEM((tm, d), F32)],
        compiler_params=_cp("parallel", "arbitrary"),
        name="mlp_ln",
    )(x_all, ada_r, ada_r, ada_r, w1, w2, lw.reshape(1, d), lb.reshape(1, d))


def _gdn_conv_kernel(nrows, x_ref, w_ref, *rest):
    o_ref, pad_ref = rest[-2:]
    j = pl.program_id(1)
    pad_ref[0:8, :] = jnp.zeros((8, LANE), F32)
    pad_ref[nrows + 8:nrows + 16, :] = jnp.zeros((8, LANE), F32)
    pad_ref[8:nrows + 8, :] = x_ref[...]
    w = w_ref[...]
    rb = min(nrows, 256)
    qscale = jnp.where(j < N_HEADS, HEAD_DIM ** -0.5, 1.0).astype(F32)

    p0 = 8 - GDN_CONV // 2
    for r0 in range(0, nrows, rb):
        y = pad_ref[r0 + p0:r0 + p0 + rb, :] * w[0:1, :]
        for i in range(1, GDN_CONV):
            y = y + pad_ref[r0 + p0 + i:r0 + p0 + i + rb, :] * w[i:i + 1, :]
        y = _silu(y)
        nrm = y * lax.rsqrt(jnp.sum(y * y, axis=-1, keepdims=True) + RMS_EPS) * qscale
        o_ref[r0:r0 + rb, :] = jnp.where(j < 2 * N_HEADS, nrm, y)


def _gdn_conv(z, conv_w, nrows, row_blk0, nb, prev=None):
    nblk = 3 * N_HEADS
    in_specs = [pl.BlockSpec((nrows, LANE), lambda b, j: (row_blk0 + b, CB_GQKV + j)),
                pl.BlockSpec((GDN_CONV, LANE), lambda b, j: (0, j))]
    args = [z, conv_w]
    if prev is not None:
        in_specs.append(pl.BlockSpec(memory_space=pl.ANY))
        args.append(prev)
    return pl.pallas_call(
        functools.partial(_gdn_conv_kernel, nrows),
        grid=(nb, nblk),
        in_specs=in_specs,
        out_specs=pl.BlockSpec((nrows, LANE), lambda b, j: (row_blk0 + b, j)),
        out_shape=jax.ShapeDtypeStruct((z.shape[0], nblk * LANE), F32),
        scratch_shapes=[pltpu.VMEM((nrows + 16, LANE), F32)],
        input_output_aliases={} if prev is None else {2: 0},
        compiler_params=_cp("parallel", "parallel"),
        name="gdn_conv",
    )(*args)


def _gdn_gates_kernel(tm, s_ref, alog_ref, dtb_ref, o_ref):
    s = s_ref[...]
    g = -jnp.exp(alog_ref[...]) * (jnp.maximum(s + dtb_ref[...], 0.0)
                                    + jnp.log1p(jnp.exp(-jnp.abs(s + dtb_ref[...]))))
    r = lax.broadcasted_iota(jnp.int32, (tm, tm), 0)
    c = lax.broadcasted_iota(jnp.int32, (tm, tm), 1)
    same = (r >> CHUNK_SHIFT) == (c >> CHUNK_SHIFT)
    lo = jnp.where(same & (c <= r), 1.0, 0.0).astype(F32)
    up = jnp.where(same & (c >= r), 1.0, 0.0).astype(F32)
    cum_f = _dot(lo, g, precision=HIGHEST)
    cum_b = _dot(up, g, precision=HIGHEST)
    col = lax.broadcasted_iota(jnp.int32, s.shape, 1)
    o_ref[...] = jnp.where(col < N_HEADS, cum_f,
                           jnp.where(col < 2 * N_HEADS, cum_b,
                                     jnp.where(col < 4 * N_HEADS, jax.nn.sigmoid(s), 0.0)))


def _gdn_gates(z, a_log, dt_bias, tm=512):
    m = z.shape[0]
    pad = lambda v: jnp.zeros((1, LANE), F32).at[0, :2 * N_HEADS].set(v.reshape(-1).astype(F32))
    return pl.pallas_call(
        functools.partial(_gdn_gates_kernel, tm),
        grid=(m // tm,),
        in_specs=[pl.BlockSpec((tm, LANE), lambda i: (i, CB_GAB)),
                  pl.BlockSpec((1, LANE), lambda i: (0, 0)),
                  pl.BlockSpec((1, LANE), lambda i: (0, 0))],
        out_specs=pl.BlockSpec((tm, LANE), lambda i: (i, 0)),
        out_shape=jax.ShapeDtypeStruct((m, LANE), F32),
        compiler_params=_cp("parallel"),
        name="gdn_gates",
    )(z, pad(a_log), pad(dt_bias))


def _tri_masks(n):
    r = lax.broadcasted_iota(jnp.int32, (n, n), 0)
    c = lax.broadcasted_iota(jnp.int32, (n, n), 1)
    return r, c


def _split(x):
    hi = x.astype(BF16)
    return hi, (x - hi.astype(F32)).astype(BF16)


def _dot3(a, b):
    return _dot(a[0], b[0]) + (_dot(a[0], b[1]) + _dot(a[1], b[0]))


def _gdn_prep_kernel(qkv_ref, p_ref, ct_ref, u_ref, w_ref, qt_ref, kt_ref, att_ref, el_ref):
    C, H = CHUNK, N_HEADS
    lane = lax.broadcasted_iota(jnp.int32, (C, LANE), 1)
    sub8 = lax.broadcasted_iota(jnp.int32, (2 * H, C), 0)
    pblk = p_ref[...]
    tblk = ct_ref[0]
    col = lambda idx: jnp.sum(jnp.where(lane == idx, pblk, 0.0), axis=-1, keepdims=True)
    row = lambda idx: jnp.sum(jnp.where(sub8 == idx, tblk, 0.0), axis=0, keepdims=True)
    r, c = _tri_masks(C)
    eye = jnp.where(r == c, 1.0, 0.0).astype(F32)
    a_list, rhs_list, where_list = [], [], []
    for h in range(H):
        hs = slice(h * LANE, (h + 1) * LANE)
        q = qkv_ref[:, h * LANE:(h + 1) * LANE]
        k = qkv_ref[:, (H + h) * LANE:(H + h + 1) * LANE]
        v = qkv_ref[:, (2 * H + h) * LANE:(2 * H + h + 1) * LANE]
        qbf, kbf = q.astype(BF16), k.astype(BF16)
        for d in range(2):
            idx = d * H + h
            cum_c, cum_r, beta_c = col(idx), row(idx), col(2 * H + idx)
            incl = (c >= r) if d else (c <= r)
            strict = (c > r) if d else (c < r)
            last = cum_r[:, 0:1] if d else cum_r[:, C - 1:C]
            decay = jnp.exp(jnp.where(incl, cum_c - cum_r, NEG))
            kb = k * beta_c
            ec = jnp.exp(cum_c)
            a_list.append(jnp.where(strict, _dot_nt(kb.astype(BF16), kbf) * decay, 0.0))
            rhs_list.append(_split(jnp.concatenate([v * beta_c, kb * ec], axis=-1)))
            where_list.append((d, hs))
            att_ref[d, h] = jnp.where(incl, _dot_nt(qbf, kbf) * decay, 0.0).astype(BF16)
            qt_ref[d, :, hs] = (q * ec).astype(BF16)
            kt_ref[d, :, hs] = (k * jnp.exp(last - cum_c)).astype(BF16)
            el_ref[0, idx:idx + 1, :] = jnp.broadcast_to(jnp.exp(last), (1, LANE))
    ts = [eye - a for a in a_list]
    ps = [_split(a) for a in a_list]
    for _ in range(5):
        ps = [_split(_dot3(p, p)) for p in ps]
        ts = [t + _dot3(_split(t), p) for t, p in zip(ts, ps)]
    sols = [_dot3(_split(t), rhs) for t, rhs in zip(ts, rhs_list)]
    for sol, (d, hs) in zip(sols, where_list):
        u_ref[d, :, hs] = sol[:, :HEAD_DIM]
        w_ref[d, :, hs] = sol[:, HEAD_DIM:].astype(BF16)


def _gdn_prep(qkv, p, ct):
    m = qkv.shape[0]
    H = N_HEADS
    gw = GROUP_WIDTH
    dspec = pl.BlockSpec((2, CHUNK, gw), lambda i: (0, i, 0))
    return pl.pallas_call(
        _gdn_prep_kernel,
        grid=(m // CHUNK,),
        in_specs=[pl.BlockSpec((CHUNK, 3 * gw), lambda i: (i, 0)),
                  pl.BlockSpec((CHUNK, LANE), lambda i: (i, 0)),
                  pl.BlockSpec((1, 2 * H, CHUNK), lambda i: (i, 0, 0))],
        out_specs=[dspec, dspec, dspec, dspec,
                   pl.BlockSpec((2, H, CHUNK, CHUNK), lambda i: (0, 0, i, 0)),
                   pl.BlockSpec((1, 2 * H, LANE), lambda i: (i, 0, 0))],
        out_shape=[jax.ShapeDtypeStruct((2, m, gw), F32),
                   jax.ShapeDtypeStruct((2, m, gw), BF16),
                   jax.ShapeDtypeStruct((2, m, gw), BF16),
                   jax.ShapeDtypeStruct((2, m, gw), BF16),
                   jax.ShapeDtypeStruct((2, H, m, CHUNK), BF16),
                   jax.ShapeDtypeStruct((m // CHUNK, 2 * H, LANE), F32)],
        compiler_params=_cp("parallel"),
        name="gdn_prep",
    )(qkv, p, ct)


def _gdn_scan_kernel(g, *refs):
    ins, (of_ref, ob_ref, s_ref) = refs[:12], refs[12:]
    H = N_HEADS

    @pl.when(pl.program_id(1) == 0)
    def _():
        s_ref[...] = jnp.zeros_like(s_ref)

    chains = [(d, h) for d in range(2) for h in range(H)]
    for j in range(g):
        mid = []
        for d, h in chains:
            u_ref, w_ref, qt_ref = ins[d:6:2]
            cj = g - 1 - j if d else j
            rs, hs = slice(cj * CHUNK, (cj + 1) * CHUNK), slice(h * LANE, (h + 1) * LANE)
            sb = s_ref[d * H + h].astype(BF16)
            vnb = (u_ref[0, rs, hs] - _dot(w_ref[0, rs, hs], sb)).astype(BF16)
            mid.append((vnb, _dot(qt_ref[0, rs, hs], sb)))
        for (d, h), (vnb, o_state) in zip(chains, mid):
            kt_ref, att_ref, el_ref = ins[6 + d::2]
            o_ref = ob_ref if d else of_ref
            cj = g - 1 - j if d else j
            rs, hs = slice(cj * CHUNK, (cj + 1) * CHUNK), slice(h * LANE, (h + 1) * LANE)
            idx = d * H + h
            s_ref[idx] = s_ref[idx] * el_ref[cj, idx:idx + 1, :] + _dot_tn(kt_ref[0, rs, hs], vnb)
            o_ref[rs, hs] = o_state + _dot(att_ref[0, h, rs, :], vnb)


def _gdn_scan(u, w, qt, kt, att, el, nb, seq, ctx_len):
    m = u.shape[1]
    H = N_HEADS
    gw = GROUP_WIDTH
    g = ctx_len // CHUNK
    nblk = seq // ctx_len
    cblk0 = nb * nblk
    blk_f = lambda b, t: jnp.where(t == 0, cblk0 + b, b * nblk + t - 1)
    blk_b = lambda b, t: jnp.where(t == 0, cblk0 + b, b * nblk + nblk - t)
    in_specs, args = [], []
    for arr in (u, w, qt, kt):
        in_specs += [pl.BlockSpec((1, ctx_len, gw), lambda b, t: (0, blk_f(b, t), 0)),
                     pl.BlockSpec((1, ctx_len, gw), lambda b, t: (1, blk_b(b, t), 0))]
        args += [arr, arr]
    in_specs += [pl.BlockSpec((1, H, ctx_len, CHUNK), lambda b, t: (0, 0, blk_f(b, t), 0)),
                 pl.BlockSpec((1, H, ctx_len, CHUNK), lambda b, t: (1, 0, blk_b(b, t), 0)),
                 pl.BlockSpec((g, 2 * H, LANE), lambda b, t: (blk_f(b, t), 0, 0)),
                 pl.BlockSpec((g, 2 * H, LANE), lambda b, t: (blk_b(b, t), 0, 0))]
    args += [att, att, el, el]
    return pl.pallas_call(
        functools.partial(_gdn_scan_kernel, g),
        grid=(nb, nblk + 1),
        in_specs=in_specs,
        out_specs=[pl.BlockSpec((ctx_len, gw), lambda b, t: (blk_f(b, t), 0)),
                   pl.BlockSpec((ctx_len, gw), lambda b, t: (blk_b(b, t), 0))],
        out_shape=[jax.ShapeDtypeStruct((m, gw), F32), jax.ShapeDtypeStruct((m, gw), F32)],
        scratch_shapes=[pltpu.VMEM((2 * H, HEAD_DIM, HEAD_DIM), F32)],
        compiler_params=_cp("parallel", "arbitrary"),
        name="gdn_scan",
    )(*args)


def _head_norm_gate(o, nw, gate):
    o = o * lax.rsqrt(jnp.mean(o * o, axis=-1, keepdims=True) + RMS_EPS) * nw
    return o * _silu(gate)


def _combine_kernel(of_ref, ob_ref, g_ref, nw_ref, y_ref):
    nw = nw_ref[...]
    for h in range(N_HEADS):
        hs = slice(h * LANE, (h + 1) * LANE)
        y_ref[:, hs] = _head_norm_gate(of_ref[:, hs] + ob_ref[:, hs], nw, g_ref[:, hs]).astype(y_ref.dtype)


def _combine(o_f, o_b, z, gate_cb, norm_w, m_out, tm=512):
    gw = GROUP_WIDTH
    spec = pl.BlockSpec((tm, gw), lambda i: (i, 0))
    return pl.pallas_call(
        _combine_kernel,
        grid=(m_out // tm,),
        in_specs=[spec, spec, pl.BlockSpec((tm, gw), lambda i: (i, gate_cb * LANE // gw)),
                  pl.BlockSpec((1, LANE), lambda i: (0, 0))],
        out_specs=spec,
        out_shape=jax.ShapeDtypeStruct((m_out, gw), BF16),
        compiler_params=_cp("parallel"),
        name="combine",
    )(o_f, o_b, z, norm_w.reshape(1, LANE))


def _gla_chunk(rev, zq, zf, zi, lb, tri, St):
    C, nsub = CHUNK, CHUNK // SUB
    q = _silu(zq) * HEAD_DIM ** -0.5
    f = lb + (1.0 - lb) * jax.nn.sigmoid(zf)
    k = 1.0 - f
    cum = _dot(tri, jnp.log(f), precision=HIGHEST)
    last = cum[0:1, :] if rev else cum[C - 1:C, :]
    vb = zi.astype(BF16)
    o = _dot_nt((q * jnp.exp(cum)).astype(BF16), St.astype(BF16))
    st_new = St * jnp.exp(last) + _dot_tn(vb, (k * jnp.exp(last - cum)).astype(BF16))
    tsub = lax.broadcasted_iota(jnp.int32, (SUB, HEAD_DIM), 0)
    trow = lax.broadcasted_iota(jnp.int32, (C, HEAD_DIM), 0)
    scol = lax.broadcasted_iota(jnp.int32, (SUB, C), 1)
    blocks = []
    for a in range(nsub):
        sa = slice(a * SUB, (a + 1) * SUB)
        qa, ka, ca = q[sa], k[sa], cum[sa]
        if rev and a < nsub - 1:
            cb = cum[(a + 1) * SUB:(a + 1) * SUB + 1, :]
            kt = k * jnp.exp(jnp.where(trow >= (a + 1) * SUB, cb - cum, NEG))
            sc = _dot_nt((qa * jnp.exp(ca - cb)).astype(BF16), kt.astype(BF16))
        elif (not rev) and a > 0:
            cb = cum[a * SUB - 1:a * SUB, :]
            kt = k * jnp.exp(jnp.where(trow < a * SUB, cb - cum, NEG))
            sc = _dot_nt((qa * jnp.exp(ca - cb)).astype(BF16), kt.astype(BF16))
        else:
            sc = jnp.zeros((SUB, C), F32)
        for j in range(SUB):
            ok = (tsub <= j) if rev else (tsub >= j)
            dec = jnp.exp(jnp.where(ok, ca - ca[j:j + 1, :], NEG))
            sj = jnp.sum(qa * ka[j:j + 1, :] * dec, axis=-1, keepdims=True)
            sc = jnp.where(scol == a * SUB + j, sj, sc)
        blocks.append(sc)
    scores = jnp.concatenate(blocks, axis=0)
    return o + _dot(scores.astype(BF16), vb), st_new


def _gla_scan_kernel(emit_ctx, n_lat, n_ctx, *refs):
    (ql_ref, ffl_ref, fbl_ref, il_ref, gl_ref,
     qc_ref, ffc_ref, fbc_ref, ic_ref, gc_ref, lb_ref, nw_ref) = refs[:12]
    if emit_ctx:
        yl_ref, yc_ref, sf_ref, sb_ref, of_ref, ob_ref, ocf_ref, ocb_ref = refs[12:]
    else:
        yl_ref, sf_ref, sb_ref, of_ref, ob_ref = refs[12:]
        yc_ref = ocf_ref = ocb_ref = None
    r, c = _tri_masks(CHUNK)
    lo = jnp.where(c <= r, 1.0, 0.0).astype(F32)
    up = jnp.where(c >= r, 1.0, 0.0).astype(F32)
    lb = lb_ref[0]

    def step(rev, q_ref, f_ref, i_ref, s_ref, o_ref, ci):
        sl = pl.ds(pl.multiple_of(ci * CHUNK, CHUNK), CHUNK)
        o, s_new = _gla_chunk(rev, q_ref[sl, :], f_ref[sl, :], i_ref[sl, :], lb, up if rev else lo, s_ref[...])
        s_ref[...] = s_new
        if o_ref is not None:
            o_ref[sl, :] = o

    sf_ref[...] = jnp.zeros_like(sf_ref)
    sb_ref[...] = jnp.zeros_like(sb_ref)

    def ctx_body(i, carry):
        step(False, qc_ref, ffc_ref, ic_ref, sf_ref, ocf_ref, i)
        step(True, qc_ref, fbc_ref, ic_ref, sb_ref, ocb_ref, n_ctx - 1 - i)
        return carry

    lax.fori_loop(0, n_ctx, ctx_body, 0)

    def lat_body(i, carry):
        step(False, ql_ref, ffl_ref, il_ref, sf_ref, of_ref, i)
        step(True, ql_ref, fbl_ref, il_ref, sb_ref, ob_ref, n_lat - 1 - i)
        return carry

    lax.fori_loop(0, n_lat, lat_body, 0)

    nw = nw_ref[...]
    rb = 256

    def fin(o1, o2, g_ref, y_ref, n):
        def body(t, carry):
            sl = pl.ds(pl.multiple_of(t * rb, rb), rb)
            y_ref[sl, :] = _head_norm_gate(o1[sl, :] + o2[sl, :], nw, g_ref[sl, :]).astype(y_ref.dtype)
            return carry
        lax.fori_loop(0, n // rb, body, 0)

    fin(of_ref, ob_ref, gl_ref, yl_ref, n_lat * CHUNK)
    if emit_ctx:
        fin(ocf_ref, ocb_ref, gc_ref, yc_ref, n_ctx * CHUNK)


def _gla_scan(z, lbs, norm_w, nb, seq, ctx_len, emit_ctx):
    n_lat, n_ctx = seq // CHUNK, ctx_len // CHUNK
    cblk0 = nb * seq // ctx_len
    lat = lambda cb: pl.BlockSpec((seq, LANE), lambda b, h: (b, cb + h))
    ctx = lambda cb: pl.BlockSpec((ctx_len, LANE), lambda b, h: (cblk0 + b, cb + h))
    cbs = (CB_HQ, CB_HFF, CB_HFB, CB_HI, CB_HG)
    in_specs = ([lat(cb) for cb in cbs] + [ctx(cb) for cb in cbs]
                + [pl.BlockSpec((1, 1, LANE), lambda b, h: (h, 0, 0)),
                   pl.BlockSpec((1, LANE), lambda b, h: (0, 0))])
    out_specs = [pl.BlockSpec((seq, LANE), lambda b, h: (b, h))]
    out_shape = [jax.ShapeDtypeStruct((nb * seq, GROUP_WIDTH), BF16)]
    scratch = [pltpu.VMEM((HEAD_DIM, HEAD_DIM), F32), pltpu.VMEM((HEAD_DIM, HEAD_DIM), F32),
               pltpu.VMEM((seq, LANE), F32), pltpu.VMEM((seq, LANE), F32)]
    if emit_ctx:
        out_specs.append(pl.BlockSpec((ctx_len, LANE), lambda b, h: (b, h)))
        out_shape.append(jax.ShapeDtypeStruct((nb * ctx_len, GROUP_WIDTH), BF16))
        scratch += [pltpu.VMEM((ctx_len, LANE), F32), pltpu.VMEM((ctx_len, LANE), F32)]
    return pl.pallas_call(
        functools.partial(_gla_scan_kernel, emit_ctx, n_lat, n_ctx),
        grid=(nb, N_HEADS),
        in_specs=in_specs, out_specs=out_specs, out_shape=out_shape, scratch_shapes=scratch,
        compiler_params=_cp("parallel", "parallel"),
        name="hgrn_scan",
    )(*([z] * 10), lbs.reshape(N_HEADS, 1, LANE), norm_w.reshape(1, LANE))


def _softmax_pv(parts):
    m = parts[0][0].max(axis=-1, keepdims=True)
    for s, _ in parts[1:]:
        m = jnp.maximum(m, s.max(axis=-1, keepdims=True))
    den, acc = None, None
    for s, v in parts:
        p = jnp.exp(s - m)
        d = jnp.sum(p, axis=-1, keepdims=True)
        a = _dot(p.astype(BF16), v)
        den = d if den is None else den + d
        acc = a if acc is None else acc + a
    return acc / den


def _na_kernel(emit_ctx, rows, *refs):
    q_ref, k_ref, v_ref, qc_ref, kc_ref, vc_ref, bias_ref = refs[:7]
    if emit_ctx:
        yl_ref, yc_ref, kb_ref, vb_ref = refs[7:]
    else:
        yl_ref, kb_ref, vb_ref = refs[7:]
    scale = HEAD_DIM ** -0.5
    win = NA_ROWS * GRID_W
    kb_ref[...] = k_ref[...].astype(BF16)
    vb_ref[...] = v_ref[...].astype(BF16)
    kc = kc_ref[...].astype(BF16)
    vc = vc_ref[...].astype(BF16)

    def body(r, carry):
        row0 = jnp.clip(r - NA_ROWS // 2, 0, rows - NA_ROWS)
        q = q_ref[pl.ds(pl.multiple_of(r * GRID_W, GRID_W), GRID_W), :].astype(BF16)
        ks = pl.ds(pl.multiple_of(row0 * GRID_W, GRID_W), win)
        s_win = _dot_nt(q, kb_ref[ks, :]) * scale + bias_ref[0, r - row0]
        s_ctx = _dot_nt(q, kc) * scale
        o = _softmax_pv([(s_win, vb_ref[ks, :]), (s_ctx, vc)])
        yl_ref[pl.ds(pl.multiple_of(r * GRID_W, GRID_W), GRID_W), :] = o.astype(yl_ref.dtype)
        return carry

    lax.fori_loop(0, rows, body, 0)
    if emit_ctx:
        s = _dot_nt(qc_ref[...].astype(BF16), kc) * scale
        yc_ref[...] = _softmax_pv([(s, vc)]).astype(yc_ref.dtype)


def _na_bias_kernel(rpb_ref, o_ref):
    n = lax.broadcasted_iota(jnp.int32, (LANE, GRID_W * GRID_W), 1)
    j = lax.broadcasted_iota(jnp.int32, (LANE, GRID_W * GRID_W), 0)
    q, w = n >> 6, n & (GRID_W - 1)
    dc = jnp.clip(w - q, 1 - NA_COLS, NA_COLS - 1) + NA_COLS - 1
    onehot = jnp.where(dc == j, 1.0, 0.0).astype(F32)
    m = _dot(rpb_ref[...], onehot, precision=HIGHEST)
    c0 = jnp.clip(q[0:1] - NA_COLS // 2, 0, GRID_W - NA_COLS)
    ok = (w[0:1] >= c0) & (w[0:1] < c0 + NA_COLS)
    o_ref[...] = jnp.where(ok, m, NEG)


def _na_bias_tables(rpb):
    depth, H, nr, nc = rpb.shape
    assert GRID_W == 64 and depth * H * nr <= LANE and nc <= LANE
    flat = jnp.zeros((LANE, LANE), F32).at[:depth * H * nr, :nc].set(rpb.reshape(-1, nc).astype(F32))
    m = pl.pallas_call(
        _na_bias_kernel,
        out_shape=jax.ShapeDtypeStruct((LANE, GRID_W * GRID_W), F32),
        compiler_params=pltpu.CompilerParams(vmem_limit_bytes=VMEM_LIMIT),
        name="na_bias",
    )(flat)
    m = m[:depth * H * nr].reshape(depth, H, nr, GRID_W, GRID_W)
    tab = jnp.stack([jnp.stack([m[:, :, k - s + NA_ROWS - 1] for k in range(NA_ROWS)], axis=3)
                     for s in range(NA_ROWS)], axis=2)
    return tab.reshape(depth, H, NA_ROWS, GRID_W, NA_ROWS * GRID_W)


def _na(z, bias, nb, seq, ctx_len, emit_ctx):
    rows = seq // GRID_W
    cblk0 = nb * seq // ctx_len
    H = N_HEADS
    lat = lambda cb: pl.BlockSpec((seq, LANE), lambda b, h: (b, cb + h))
    ctx = lambda cb: pl.BlockSpec((ctx_len, LANE), lambda b, h: (cblk0 + b, cb + h))
    win = NA_ROWS * GRID_W
    in_specs = [lat(CB_NAQ), lat(CB_NAK), lat(CB_NAV), ctx(CB_NAQ), ctx(CB_NAK), ctx(CB_NAV),
                pl.BlockSpec((1, NA_ROWS, GRID_W, win), lambda b, h: (h, 0, 0, 0))]
    out_specs = [pl.BlockSpec((seq, LANE), lambda b, h: (b, h))]
    out_shape = [jax.ShapeDtypeStruct((nb * seq, GROUP_WIDTH), BF16)]
    if emit_ctx:
        out_specs.append(pl.BlockSpec((ctx_len, LANE), lambda b, h: (b, h)))
        out_shape.append(jax.ShapeDtypeStruct((nb * ctx_len, GROUP_WIDTH), BF16))
    return pl.pallas_call(
        functools.partial(_na_kernel, emit_ctx, rows),
        grid=(nb, H),
        in_specs=in_specs, out_specs=out_specs, out_shape=out_shape,
        scratch_shapes=[pltpu.VMEM((seq, LANE), BF16), pltpu.VMEM((seq, LANE), BF16)],
        compiler_params=_cp("parallel", "parallel"),
        name="na_attn",
    )(z, z, z, z, z, z, bias)


def _rms(x, w):
    return x * lax.rsqrt(jnp.mean(x * x, axis=-1, keepdims=True) + RMS_EPS) * w


def _mla_prep_kernel(cq_ref, ckv_ref, kra_ref, krb_ref, cc_ref, ss_ref, qnw_ref, kvnw_ref, wuq_ref, wukv_ref,
                     q_ref, kn_ref, kr_ref, v_ref):
    cc, ss = cc_ref[...], ss_ref[...]
    qn = _rms(cq_ref[...], qnw_ref[...]).astype(BF16)
    qa = _dot(qn, wuq_ref[...])
    for h in range(N_HEADS):
        b = 3 * LANE * h
        q_ref[:, 2 * LANE * h:2 * LANE * h + LANE] = qa[:, b:b + LANE].astype(BF16)
        q_ref[:, 2 * LANE * h + LANE:2 * LANE * (h + 1)] = (
            qa[:, b + LANE:b + 2 * LANE] * cc + qa[:, b + 2 * LANE:b + 3 * LANE] * ss).astype(BF16)
    kvn = _rms(ckv_ref[...], kvnw_ref[...]).astype(BF16)
    kv = _dot(kvn, wukv_ref[...])
    kn_ref[...] = kv[:, :GROUP_WIDTH].astype(BF16)
    v_ref[...] = kv[:, GROUP_WIDTH:].astype(BF16)
    kr_ref[...] = (kra_ref[...] * cc + krb_ref[...] * ss).astype(BF16)


def _mla_prep(z, cc, ss, qnw, kvnw, wuq, wukv, n_lat, seq, tm=512):
    m = z.shape[0]
    nlt, spt = n_lat // tm, seq // tm
    tab = lambda i: jnp.where(i < nlt, i % spt, spt)
    H = N_HEADS
    return pl.pallas_call(
        _mla_prep_kernel,
        grid=(m // tm,),
        in_specs=[pl.BlockSpec((tm, MLA_Q_RANK), lambda i: (i, CB_MQ * LANE // MLA_Q_RANK)),
                  pl.BlockSpec((tm, MLA_KV_RANK), lambda i: (i, CB_MKV * LANE // MLA_KV_RANK)),
                  pl.BlockSpec((tm, LANE), lambda i: (i, CB_MKRA)),
                  pl.BlockSpec((tm, LANE), lambda i: (i, CB_MKRB)),
                  pl.BlockSpec((tm, LANE), lambda i: (tab(i), 0)),
                  pl.BlockSpec((tm, LANE), lambda i: (tab(i), 0)),
                  pl.BlockSpec((1, MLA_Q_RANK), lambda i: (0, 0)),
                  pl.BlockSpec((1, MLA_KV_RANK), lambda i: (0, 0)),
                  pl.BlockSpec(wuq.shape, lambda i: (0, 0)),
                  pl.BlockSpec(wukv.shape, lambda i: (0, 0))],
        out_specs=[pl.BlockSpec((tm, 2 * LANE * H), lambda i: (i, 0)),
                   pl.BlockSpec((tm, GROUP_WIDTH), lambda i: (i, 0)),
                   pl.BlockSpec((tm, LANE), lambda i: (i, 0)),
                   pl.BlockSpec((tm, GROUP_WIDTH), lambda i: (i, 0))],
        out_shape=[jax.ShapeDtypeStruct((m, 2 * LANE * H), BF16),
                   jax.ShapeDtypeStruct((m, GROUP_WIDTH), BF16),
                   jax.ShapeDtypeStruct((m, LANE), BF16),
                   jax.ShapeDtypeStruct((m, GROUP_WIDTH), BF16)],
        compiler_params=_cp("parallel"),
        name="mla_prep",
    )(z, z, z, z, cc, ss, qnw.reshape(1, -1), kvnw.reshape(1, -1), wuq, wukv)


def _mla_attn_kernel(with_lat, seq, *refs):
    if with_lat:
        q_ref, knl_ref, krl_ref, vl_ref, knc_ref, krc_ref, vc_ref, y_ref, k_scr = refs
    else:
        q_ref, knc_ref, krc_ref, vc_ref, y_ref, k_scr = refs
    scale = MLA_QK_DIM ** -0.5
    nk = k_scr.shape[0]

    @pl.when(pl.program_id(2) == 0)
    def _():
        if with_lat:
            k_scr[0:seq, 0:LANE] = knl_ref[...]
            k_scr[0:seq, LANE:2 * LANE] = krl_ref[...]
        k_scr[nk - knc_ref.shape[0]:nk, 0:LANE] = knc_ref[...]
        k_scr[nk - knc_ref.shape[0]:nk, LANE:2 * LANE] = krc_ref[...]

    q = q_ref[...]
    parts = []
    if with_lat:
        parts.append((_dot_nt(q, k_scr[0:seq, :]) * scale, vl_ref[...]))
    parts.append((_dot_nt(q, k_scr[nk - knc_ref.shape[0]:nk, :]) * scale, vc_ref[...]))
    y_ref[...] = _softmax_pv(parts).astype(y_ref.dtype)


def _mla_attn(q, kn, kr, v, nb, seq, ctx_len, with_lat, tq=256):
    H = N_HEADS
    cblk0 = nb * seq // ctx_len
    nq = seq if with_lat else ctx_len
    tq = min(tq, nq)
    qblk0 = 0 if with_lat else nb * seq // tq
    ctxs = [pl.BlockSpec((ctx_len, LANE), lambda b, h, i: (cblk0 + b, h)),
            pl.BlockSpec((ctx_len, LANE), lambda b, h, i: (cblk0 + b, 0)),
            pl.BlockSpec((ctx_len, LANE), lambda b, h, i: (cblk0 + b, h))]
    lats = [pl.BlockSpec((seq, LANE), lambda b, h, i: (b, h)),
            pl.BlockSpec((seq, LANE), lambda b, h, i: (b, 0)),
            pl.BlockSpec((seq, LANE), lambda b, h, i: (b, h))]
    in_specs = [pl.BlockSpec((tq, 2 * LANE), lambda b, h, i: (qblk0 + b * (nq // tq) + i, h))]
    args = [q]
    if with_lat:
        in_specs += lats
        args += [kn, kr, v]
    in_specs += ctxs
    args += [kn, kr, v]
    nk = (seq if with_lat else 0) + ctx_len
    return pl.pallas_call(
        functools.partial(_mla_attn_kernel, with_lat, seq),
        grid=(nb, H, nq // tq),
        in_specs=in_specs,
        out_specs=pl.BlockSpec((tq, LANE), lambda b, h, i: (b * (nq // tq) + i, h)),
        out_shape=jax.ShapeDtypeStruct((nb * nq, GROUP_WIDTH), BF16),
        scratch_shapes=[pltpu.VMEM((nk, 2 * LANE), BF16)],
        compiler_params=_cp("parallel", "parallel", "arbitrary"),
        name="mla_attn_lat" if with_lat else "mla_attn_ctx",
    )(*args)


def _prep_w_in(w_in):
    d = w_in.shape[0]
    gw = GROUP_WIDTH
    o_na = 4 * gw + 4 * N_HEADS
    o_mla = o_na + 3 * gw
    o_kr = o_mla + MLA_Q_RANK + MLA_KV_RANK
    o_hg = o_kr + MLA_ROPE
    kr = w_in[:, o_kr:o_hg]
    k1, k2 = kr[:, 0::2], kr[:, 1::2]
    z = lambda n: jnp.zeros((d, n), w_in.dtype)
    o_kv = o_mla + MLA_Q_RANK
    cols = [w_in[:, o_mla:o_kv], k1, k2, z(LANE - MLA_ROPE),
            w_in[:, o_kv:o_kr], k2, k1, z(LANE - MLA_ROPE),
            w_in[:, 4 * gw:o_na], z(LANE - 4 * N_HEADS),
            w_in[:, :4 * gw], w_in[:, o_na:o_mla], w_in[:, o_hg:]]
    w = jnp.concatenate(cols, axis=1).astype(BF16)
    assert w.shape[1] == NP_IN
    return w


def _prep_w_uq(w_uq):
    r = w_uq.shape[0]
    z = jnp.zeros((r, LANE - MLA_ROPE), w_uq.dtype)
    cols = []
    for h in range(N_HEADS):
        wh = w_uq[:, h * MLA_QK_DIM:(h + 1) * MLA_QK_DIM]
        rope = wh[:, MLA_NOPE:]
        r1, r2 = rope[:, 0::2], rope[:, 1::2]
        cols += [wh[:, :MLA_NOPE], r1, r2, z, r2, r1, z]
    return jnp.concatenate(cols, axis=1).astype(BF16)


def _rope_tables(seq, tm):
    n_freq = MLA_ROPE // 4
    freqs = ROPE_BASE ** (-jnp.arange(n_freq, dtype=F32) / n_freq)
    t = jnp.arange(seq)
    ang = jnp.concatenate([(t // GRID_W).astype(F32)[:, None] * freqs,
                           (t % GRID_W).astype(F32)[:, None] * freqs], -1)
    cos, sin = jnp.cos(ang), jnp.sin(ang)
    zp = jnp.zeros((seq, LANE - MLA_ROPE), F32)
    cc = jnp.concatenate([cos, cos, zp], axis=1)
    ss = jnp.concatenate([-sin, sin, zp], axis=1)
    ident = jnp.zeros((tm, LANE), F32).at[:, :MLA_ROPE].set(1.0)
    return jnp.concatenate([cc, ident], axis=0), jnp.concatenate([ss, jnp.zeros((tm, LANE), F32)], axis=0)


def kernel(x, c, ctx, c_ctx, w_ada, b_ada, w_in, gdn_conv_w, gdn_a_log, gdn_dt_bias, gdn_norm_w, na_rpb,
           mla_q_norm_w, mla_kv_norm_w, mla_w_uq, mla_w_uk, mla_w_uv, hgrn_lower_bounds, hgrn_norm_w, w_out,
           ln1_w, ln1_b, w_mlp1, w_mlp2, ln2_w, ln2_b):
    nb, seq, d = x.shape
    ctx_len = ctx.shape[1]
    depth = w_ada.shape[0]
    n_lat, n_ctx = nb * seq, nb * ctx_len
    alpha = (2 * depth) ** 0.25
    tm = 512
    assert nb < 8 and seq % tm == 0 and n_ctx % tm == 0 and seq % ctx_len == 0 and ctx_len % CHUNK == 0

    cin = jnp.zeros((8, d), F32).at[:nb].set(c).at[nb].set(c_ctx)
    ada = _ada(cin, w_ada, b_ada)
    p_lb = jax.nn.softmax(hgrn_lower_bounds.astype(F32), axis=0)
    lbs = jnp.cumsum(p_lb, axis=0) - p_lb[0]
    cc, ss = _rope_tables(seq, tm)
    na_bias = _na_bias_tables(na_rpb)

    x_all = jnp.concatenate([x.reshape(n_lat, d), ctx.reshape(n_ctx, d)], axis=0)
    for l in range(depth):
        emit_ctx = l < depth - 1
        ada_r = ada[l].reshape(8 * 6, 1, d)
        z = _inproj(x_all, ada_r, _prep_w_in(w_in[l]), n_lat, seq, nb, tm=tm)

        qkv = _gdn_conv(z, gdn_conv_w[l], seq, 0, nb)
        qkv = _gdn_conv(z, gdn_conv_w[l], ctx_len, n_lat // ctx_len, nb, prev=qkv)
        p = _gdn_gates(z, gdn_a_log[l], gdn_dt_bias[l], tm=tm)
        ct = p[:, :2 * N_HEADS].reshape(-1, CHUNK, 2 * N_HEADS).transpose(0, 2, 1)
        o_f, o_b = _gdn_scan(*_gdn_prep(qkv, p, ct), nb, seq, ctx_len)
        m_out = n_lat + n_ctx if emit_ctx else n_lat
        ya = _combine(o_f, o_b, z, CB_GGATE, gdn_norm_w[l], m_out, tm=tm)
        yb = _na(z, na_bias[l], nb, seq, ctx_len, emit_ctx)
        q, kn, kr, v = _mla_prep(z, cc, ss, mla_q_norm_w[l], mla_kv_norm_w[l], _prep_w_uq(mla_w_uq[l]),
                                 jnp.concatenate([mla_w_uk[l], mla_w_uv[l]], axis=1).astype(BF16),
                                 n_lat, seq, tm=tm)
        ym = [_mla_attn(q, kn, kr, v, nb, seq, ctx_len, True)]
        if emit_ctx:
            ym.append(_mla_attn(q, kn, kr, v, nb, seq, ctx_len, False))
        yh = _gla_scan(z, lbs[l], hgrn_norm_w[l], nb, seq, ctx_len, emit_ctx)

        if emit_ctx:
            ys = [ya] + [jnp.concatenate(list(t), axis=0) for t in (yb, ym, yh)]
        else:
            ys = [ya, yb[0], ym[0], yh[0]]
        x_all = _outproj(x_all, ys, w_out[l].astype(BF16), ada_r, ln1_w[l], ln1_b[l], m_out, seq, nb, alpha, tm=tm)
        x_all = _mlp(x_all, w_mlp1[l].astype(BF16), w_mlp2[l].astype(BF16), ada_r, ln2_w[l], ln2_b[l],
                     seq, nb, alpha, tm=tm)
    return x_all[:n_lat].reshape(nb, seq, d)
```

```python
import functools

import numpy as np
import jax
import jax.numpy as jnp
from jax import lax
from jax.experimental import pallas as pl
from jax.experimental.pallas import tpu as pltpu

F32 = jnp.float32
BF16 = jnp.bfloat16
HIGHEST = lax.Precision.HIGHEST

GRID_W = 64
N_HEADS = 4
HEAD_DIM = 128
GROUP_WIDTH = 512
CHUNK = 64
SUB = 16
GDN_CONV = 5
NA_ROWS = 8
NA_COLS = 16
NA_UNROLL = 4
MLA_Q_RANK = 384
MLA_KV_RANK = 256
MLA_NOPE = 128
MLA_ROPE = 64
MLA_QK_DIM = MLA_NOPE + MLA_ROPE
ROPE_BASE = 10000.0
LN_EPS = 1e-5
RMS_EPS = 1e-6
NEG = -1e30

LANE = 128
CB_MQ, CB_MKRA, CB_MKV, CB_MKRB, CB_GAB = 0, 3, 4, 6, 7
CB_GQKV, CB_GGATE = 8, 20
CB_NAQ, CB_NAK, CB_NAV = 24, 28, 32
CB_HQ, CB_HFF, CB_HFB, CB_HI, CB_HG = 36, 40, 44, 48, 52
CHUNK_SHIFT = 6
NP_IN = 56 * LANE

VMEM_LIMIT = 48 << 20


def _cp(*sem):
    return pltpu.CompilerParams(dimension_semantics=sem, vmem_limit_bytes=VMEM_LIMIT)


def _silu(x):
    return x * jax.nn.sigmoid(x)


def _dot(a, b, **kw):
    return jnp.dot(a, b, preferred_element_type=F32, **kw)


def _dot_nt(a, b, **kw):
    return lax.dot_general(a, b, (((1,), (1,)), ((), ())), preferred_element_type=F32, **kw)


def _dot_tn(a, b, **kw):
    return lax.dot_general(a, b, (((0,), (0,)), ((), ())), preferred_element_type=F32, **kw)


def _ada_kernel(c_ref, w_ref, b_ref, o_ref):
    s = _silu(c_ref[...])
    o_ref[0] = _dot(s, w_ref[0], precision=HIGHEST) + b_ref[0]


def _ada(cin, w_ada, b_ada):
    depth, d, n = w_ada.shape
    tn = 512
    return pl.pallas_call(
        _ada_kernel,
        grid=(depth, n // tn),
        in_specs=[pl.BlockSpec((8, d), lambda l, j: (0, 0)),
                  pl.BlockSpec((1, d, tn), lambda l, j: (l, 0, j)),
                  pl.BlockSpec((1, 1, tn), lambda l, j: (l, 0, j))],
        out_specs=pl.BlockSpec((1, 8, tn), lambda l, j: (l, 0, j)),
        out_shape=jax.ShapeDtypeStruct((depth, 8, n), F32),
        compiler_params=_cp("parallel", "parallel"),
        name="ada",
    )(cin, w_ada, b_ada.reshape(depth, 1, n))


ROW_STEP = 256


def _modulate(x_ref, sh_ref, sc_ref, xm_ref):
    sc1, sh = 1.0 + sc_ref[0], sh_ref[0]

    def body(t, carry):
        sl = pl.ds(pl.multiple_of(t * ROW_STEP, ROW_STEP), ROW_STEP)
        xm_ref[sl, :] = (x_ref[sl, :] * sc1 + sh).astype(BF16)
        return carry

    lax.fori_loop(0, x_ref.shape[0] // ROW_STEP, body, 0)


def _inproj_kernel(x_ref, sh_ref, sc_ref, w_ref, o_ref, xm_ref):
    @pl.when(pl.program_id(1) == 0)
    def _():
        _modulate(x_ref, sh_ref, sc_ref, xm_ref)
    o_ref[...] = _dot(xm_ref[...], w_ref[...])


def _inproj(x_all, ada_r, w, n_lat, seq, nb, tm=512, tn=1024):
    m, d = x_all.shape
    n = w.shape[1]
    row = lambda i: jnp.minimum((i * tm) // seq, nb)
    return pl.pallas_call(
        _inproj_kernel,
        grid=(m // tm, n // tn),
        in_specs=[pl.BlockSpec((tm, d), lambda i, j: (i, 0)),
                  pl.BlockSpec((1, 1, d), lambda i, j: (row(i) * 6 + 0, 0, 0)),
                  pl.BlockSpec((1, 1, d), lambda i, j: (row(i) * 6 + 1, 0, 0)),
                  pl.BlockSpec((d, tn), lambda i, j: (0, j))],
        out_specs=pl.BlockSpec((tm, tn), lambda i, j: (i, j)),
        out_shape=jax.ShapeDtypeStruct((m, n), F32),
        scratch_shapes=[pltpu.VMEM((tm, d), BF16)],
        compiler_params=_cp("parallel", "arbitrary"),
        name="inproj",
    )(x_all, ada_r, ada_r, w)


def _layernorm(r, w, b):
    mu = jnp.mean(r, axis=-1, keepdims=True)
    rc = r - mu
    var = jnp.mean(rc * rc, axis=-1, keepdims=True)
    return rc * lax.rsqrt(var + LN_EPS) * w + b


def _outproj_kernel(alpha, x_ref, ya_ref, yb_ref, ym_ref, yh_ref, w_ref, g_ref, lw_ref, lb_ref, o_ref):
    gw = GROUP_WIDTH
    acc = _dot(ya_ref[...], w_ref[0:gw, :])
    acc += _dot(yb_ref[...], w_ref[gw:2 * gw, :])
    acc += _dot(ym_ref[...], w_ref[2 * gw:3 * gw, :])
    acc += _dot(yh_ref[...], w_ref[3 * gw:4 * gw, :])
    r = alpha * x_ref[...] + g_ref[0] * acc
    o_ref[...] = _layernorm(r, lw_ref[...], lb_ref[...])


def _outproj(x_all, ys, w, ada_r, lw, lb, m_out, seq, nb, alpha, tm=512):
    d = x_all.shape[1]
    row = lambda i: jnp.minimum((i * tm) // seq, nb)
    yspec = pl.BlockSpec((tm, GROUP_WIDTH), lambda i: (i, 0))
    return pl.pallas_call(
        functools.partial(_outproj_kernel, alpha),
        grid=(m_out // tm,),
        in_specs=[pl.BlockSpec((tm, d), lambda i: (i, 0)), yspec, yspec, yspec, yspec,
                  pl.BlockSpec((d, d), lambda i: (0, 0)),
                  pl.BlockSpec((1, 1, d), lambda i: (row(i) * 6 + 2, 0, 0)),
                  pl.BlockSpec((1, d), lambda i: (0, 0)),
                  pl.BlockSpec((1, d), lambda i: (0, 0))],
        out_specs=pl.BlockSpec((tm, d), lambda i: (i, 0)),
        out_shape=jax.ShapeDtypeStruct((m_out, d), F32),
        compiler_params=_cp("parallel"),
        name="outproj_ln",
    )(x_all, *ys, w, ada_r, lw.reshape(1, d), lb.reshape(1, d))


def _mlp_kernel(alpha, x_ref, sh_ref, sc_ref, g_ref, w1_ref, w2_ref, lw_ref, lb_ref, o_ref, xm_ref, acc_ref):
    k = pl.program_id(1)

    @pl.when(k == 0)
    def _():
        _modulate(x_ref, sh_ref, sc_ref, xm_ref)
        acc_ref[...] = jnp.zeros_like(acc_ref)

    h = jnp.maximum(_dot(xm_ref[...], w1_ref[...]), 0.0)
    acc_ref[...] += _dot((h * h).astype(BF16), w2_ref[...])

    @pl.when(k == pl.num_programs(1) - 1)
    def _():
        g, lw, lb = g_ref[0], lw_ref[...], lb_ref[...]

        def body(t, carry):
            sl = pl.ds(pl.multiple_of(t * ROW_STEP, ROW_STEP), ROW_STEP)
            o_ref[sl, :] = _layernorm(alpha * x_ref[sl, :] + g * acc_ref[sl, :], lw, lb)
            return carry

        lax.fori_loop(0, x_ref.shape[0] // ROW_STEP, body, 0)


def _mlp(x_all, w1, w2, ada_r, lw, lb, seq, nb, alpha, tm=512, th=512):
    m, d = x_all.shape
    hid = w1.shape[1]
    row = lambda i: jnp.minimum((i * tm) // seq, nb)
    return pl.pallas_call(
        functools.partial(_mlp_kernel, alpha),
        grid=(m // tm, hid // th),
        in_specs=[pl.BlockSpec((tm, d), lambda i, k: (i, 0), pipeline_mode=pl.Buffered(1)),
                  pl.BlockSpec((1, 1, d), lambda i, k: (row(i) * 6 + 3, 0, 0)),
                  pl.BlockSpec((1, 1, d), lambda i, k: (row(i) * 6 + 4, 0, 0)),
                  pl.BlockSpec((1, 1, d), lambda i, k: (row(i) * 6 + 5, 0, 0)),
                  pl.BlockSpec((d, th), lambda i, k: (0, k)),
                  pl.BlockSpec((th, d), lambda i, k: (k, 0)),
                  pl.BlockSpec((1, d), lambda i, k: (0, 0)),
                  pl.BlockSpec((1, d), lambda i, k: (0, 0))],
        out_specs=pl.BlockSpec((tm, d), lambda i, k: (i, 0)),
        out_shape=jax.ShapeDtypeStruct((m, d), F32),
        scratch_shapes=[pltpu.VMEM((tm, d), BF16), pltpu.VMEM((tm, d), F32)],
        compiler_params=_cp("parallel", "arbitrary"),
        name="mlp_ln",
    )(x_all, ada_r, ada_r, ada_r, w1, w2, lw.reshape(1, d), lb.reshape(1, d))


def _gdn_conv_kernel(seq, ctx_len, xl_ref, xc_ref, w_ref, o_ref, pad_ref):
    j = pl.program_id(1)
    w = w_ref[...]
    qscale = jnp.where(j < N_HEADS, HEAD_DIM ** -0.5, 1.0).astype(F32)
    p0 = 8 - GDN_CONV // 2
    for x_ref, nrows, o0 in ((xl_ref, seq, 0), (xc_ref, ctx_len, seq)):
        pad_ref[0:8, :] = jnp.zeros((8, LANE), F32)
        pad_ref[nrows + 8:nrows + 16, :] = jnp.zeros((8, LANE), F32)
        pad_ref[8:nrows + 8, :] = x_ref[...]
        rb = min(nrows, 256)
        for r0 in range(0, nrows, rb):
            y = pad_ref[r0 + p0:r0 + p0 + rb, :] * w[0:1, :]
            for i in range(1, GDN_CONV):
                y = y + pad_ref[r0 + p0 + i:r0 + p0 + i + rb, :] * w[i:i + 1, :]
            y = _silu(y)
            nrm = y * lax.rsqrt(jnp.sum(y * y, axis=-1, keepdims=True) + RMS_EPS) * qscale
            o_ref[o0 + r0:o0 + r0 + rb, :] = jnp.where(j < 2 * N_HEADS, nrm, y)


def _gdn_conv(z, conv_w, nb, seq, ctx_len):
    nblk = 3 * N_HEADS
    cblk0 = nb * seq // ctx_len
    return pl.pallas_call(
        functools.partial(_gdn_conv_kernel, seq, ctx_len),
        grid=(nb, nblk),
        in_specs=[pl.BlockSpec((seq, LANE), lambda b, j: (b, CB_GQKV + j)),
                  pl.BlockSpec((ctx_len, LANE), lambda b, j: (cblk0 + b, CB_GQKV + j)),
                  pl.BlockSpec((GDN_CONV, LANE), lambda b, j: (0, j))],
        out_specs=pl.BlockSpec((seq + ctx_len, LANE), lambda b, j: (b, j)),
        out_shape=jax.ShapeDtypeStruct((nb * (seq + ctx_len), nblk * LANE), F32),
        scratch_shapes=[pltpu.VMEM((seq + 16, LANE), F32)],
        compiler_params=_cp("parallel", "parallel"),
        name="gdn_conv",
    )(z, z, conv_w)


def _gdn_gates_kernel(tm, s_ref, alog_ref, dtb_ref, o_ref):
    s = s_ref[...]
    g = -jnp.exp(alog_ref[...]) * (jnp.maximum(s + dtb_ref[...], 0.0)
                                    + jnp.log1p(jnp.exp(-jnp.abs(s + dtb_ref[...]))))
    r = lax.broadcasted_iota(jnp.int32, (tm, tm), 0)
    c = lax.broadcasted_iota(jnp.int32, (tm, tm), 1)
    same = (r >> CHUNK_SHIFT) == (c >> CHUNK_SHIFT)
    lo = jnp.where(same & (c <= r), 1.0, 0.0).astype(F32)
    up = jnp.where(same & (c >= r), 1.0, 0.0).astype(F32)
    cum_f = _dot(lo, g, precision=HIGHEST)
    cum_b = _dot(up, g, precision=HIGHEST)
    col = lax.broadcasted_iota(jnp.int32, s.shape, 1)
    o_ref[...] = jnp.where(col < N_HEADS, cum_f,
                           jnp.where(col < 2 * N_HEADS, cum_b,
                                     jnp.where(col < 4 * N_HEADS, jax.nn.sigmoid(s), 0.0)))


def _gdn_gates(z, a_log, dt_bias, tm=512):
    m = z.shape[0]
    pad = lambda v: jnp.zeros((1, LANE), F32).at[0, :2 * N_HEADS].set(v.reshape(-1).astype(F32))
    return pl.pallas_call(
        functools.partial(_gdn_gates_kernel, tm),
        grid=(m // tm,),
        in_specs=[pl.BlockSpec((tm, LANE), lambda i: (i, CB_GAB)),
                  pl.BlockSpec((1, LANE), lambda i: (0, 0)),
                  pl.BlockSpec((1, LANE), lambda i: (0, 0))],
        out_specs=pl.BlockSpec((tm, LANE), lambda i: (i, 0)),
        out_shape=jax.ShapeDtypeStruct((m, LANE), F32),
        compiler_params=_cp("parallel"),
        name="gdn_gates",
    )(z, pad(a_log), pad(dt_bias))


def _tri_masks(n):
    r = lax.broadcasted_iota(jnp.int32, (n, n), 0)
    c = lax.broadcasted_iota(jnp.int32, (n, n), 1)
    return r, c


def _split(x):
    hi = x.astype(BF16)
    return hi, (x - hi.astype(F32)).astype(BF16)


def _dot3(a, b):
    return _dot(a[0], b[0]) + (_dot(a[0], b[1]) + _dot(a[1], b[0]))


def _gdn_prep_kernel(qkv_ref, p_ref, ct_ref, u_ref, w_ref, qt_ref, kt_ref, att_ref, el_ref):
    C, H = CHUNK, N_HEADS
    lane = lax.broadcasted_iota(jnp.int32, (C, LANE), 1)
    sub8 = lax.broadcasted_iota(jnp.int32, (2 * H, C), 0)
    pblk = p_ref[...]
    tblk = ct_ref[0]
    col = lambda idx: jnp.sum(jnp.where(lane == idx, pblk, 0.0), axis=-1, keepdims=True)
    row = lambda idx: jnp.sum(jnp.where(sub8 == idx, tblk, 0.0), axis=0, keepdims=True)
    r, c = _tri_masks(C)
    eye = jnp.where(r == c, 1.0, 0.0).astype(F32)
    a_list, rhs_list, where_list = [], [], []
    for h in range(H):
        hs = slice(h * LANE, (h + 1) * LANE)
        q = qkv_ref[:, h * LANE:(h + 1) * LANE]
        k = qkv_ref[:, (H + h) * LANE:(H + h + 1) * LANE]
        v = qkv_ref[:, (2 * H + h) * LANE:(2 * H + h + 1) * LANE]
        qbf, kbf = q.astype(BF16), k.astype(BF16)
        for d in range(2):
            idx = d * H + h
            cum_c, cum_r, beta_c = col(idx), row(idx), col(2 * H + idx)
            incl = (c >= r) if d else (c <= r)
            strict = (c > r) if d else (c < r)
            last = cum_r[:, 0:1] if d else cum_r[:, C - 1:C]
            decay = jnp.exp(jnp.where(incl, cum_c - cum_r, NEG))
            kb = k * beta_c
            ec = jnp.exp(cum_c)
            a_list.append(jnp.where(strict, _dot_nt(kb.astype(BF16), kbf) * decay, 0.0))
            rhs_list.append(_split(jnp.concatenate([v * beta_c, kb * ec], axis=-1)))
            where_list.append((d, hs))
            att_ref[d, h] = jnp.where(incl, _dot_nt(qbf, kbf) * decay, 0.0).astype(BF16)
            qt_ref[d, :, hs] = (q * ec).astype(BF16)
            kt_ref[d, :, hs] = (k * jnp.exp(last - cum_c)).astype(BF16)
            el_ref[0, idx:idx + 1, :] = jnp.broadcast_to(jnp.exp(last), (1, LANE))
    ts = [eye - a for a in a_list]
    ps = [_split(a) for a in a_list]
    for _ in range(5):
        ps = [_split(_dot3(p, p)) for p in ps]
        ts = [t + _dot3(_split(t), p) for t, p in zip(ts, ps)]
    sols = [_dot3(_split(t), rhs) for t, rhs in zip(ts, rhs_list)]
    for sol, (d, hs) in zip(sols, where_list):
        u_ref[d, :, hs] = sol[:, :HEAD_DIM]
        w_ref[d, :, hs] = sol[:, HEAD_DIM:].astype(BF16)


def _gdn_prep(qkv, p, ct, nb, seq, ctx_len):
    m = qkv.shape[0]
    H = N_HEADS
    gw = GROUP_WIDTH
    dspec = pl.BlockSpec((2, CHUNK, gw), lambda i: (0, i, 0))
    nl, nc = seq // CHUNK, ctx_len // CHUNK

    def qkv_chunk(i):
        ic = i - nb * nl
        return jnp.where(i < nb * nl, (i // nl) * (nl + nc) + i % nl, (ic // nc) * (nl + nc) + nl + ic % nc)

    return pl.pallas_call(
        _gdn_prep_kernel,
        grid=(m // CHUNK,),
        in_specs=[pl.BlockSpec((CHUNK, 3 * gw), lambda i: (qkv_chunk(i), 0)),
                  pl.BlockSpec((CHUNK, LANE), lambda i: (i, 0)),
                  pl.BlockSpec((1, 2 * H, CHUNK), lambda i: (i, 0, 0))],
        out_specs=[dspec, dspec, dspec, dspec,
                   pl.BlockSpec((2, H, CHUNK, CHUNK), lambda i: (0, 0, i, 0)),
                   pl.BlockSpec((1, 2 * H, LANE), lambda i: (i, 0, 0))],
        out_shape=[jax.ShapeDtypeStruct((2, m, gw), F32),
                   jax.ShapeDtypeStruct((2, m, gw), BF16),
                   jax.ShapeDtypeStruct((2, m, gw), BF16),
                   jax.ShapeDtypeStruct((2, m, gw), BF16),
                   jax.ShapeDtypeStruct((2, H, m, CHUNK), BF16),
                   jax.ShapeDtypeStruct((m // CHUNK, 2 * H, LANE), F32)],
        compiler_params=_cp("parallel"),
        name="gdn_prep",
    )(qkv, p, ct)


def _gdn_scan_kernel(g, *refs):
    ins, (of_ref, ob_ref, s_ref) = refs[:12], refs[12:]
    H = N_HEADS

    @pl.when(pl.program_id(1) == 0)
    def _():
        s_ref[...] = jnp.zeros_like(s_ref)

    chains = [(d, h) for d in range(2) for h in range(H)]
    for j in range(g):
        mid = []
        for d, h in chains:
            u_ref, w_ref, qt_ref = ins[d:6:2]
            cj = g - 1 - j if d else j
            rs, hs = slice(cj * CHUNK, (cj + 1) * CHUNK), slice(h * LANE, (h + 1) * LANE)
            sb = s_ref[d * H + h].astype(BF16)
            vnb = (u_ref[0, rs, hs] - _dot(w_ref[0, rs, hs], sb)).astype(BF16)
            mid.append((vnb, _dot(qt_ref[0, rs, hs], sb)))
        for (d, h), (vnb, o_state) in zip(chains, mid):
            kt_ref, att_ref, el_ref = ins[6 + d::2]
            o_ref = ob_ref if d else of_ref
            cj = g - 1 - j if d else j
            rs, hs = slice(cj * CHUNK, (cj + 1) * CHUNK), slice(h * LANE, (h + 1) * LANE)
            idx = d * H + h
            s_ref[idx] = s_ref[idx] * el_ref[cj, idx:idx + 1, :] + _dot_tn(kt_ref[0, rs, hs], vnb)
            o_ref[rs, hs] = o_state + _dot(att_ref[0, h, rs, :], vnb)


def _gdn_scan(u, w, qt, kt, att, el, nb, seq, ctx_len):
    m = u.shape[1]
    H = N_HEADS
    gw = GROUP_WIDTH
    g = ctx_len // CHUNK
    nblk = seq // ctx_len
    cblk0 = nb * nblk
    blk_f = lambda b, t: jnp.where(t == 0, cblk0 + b, b * nblk + t - 1)
    blk_b = lambda b, t: jnp.where(t == 0, cblk0 + b, b * nblk + nblk - t)
    in_specs, args = [], []
    for arr in (u, w, qt, kt):
        in_specs += [pl.BlockSpec((1, ctx_len, gw), lambda b, t: (0, blk_f(b, t), 0)),
                     pl.BlockSpec((1, ctx_len, gw), lambda b, t: (1, blk_b(b, t), 0))]
        args += [arr, arr]
    in_specs += [pl.BlockSpec((1, H, ctx_len, CHUNK), lambda b, t: (0, 0, blk_f(b, t), 0)),
                 pl.BlockSpec((1, H, ctx_len, CHUNK), lambda b, t: (1, 0, blk_b(b, t), 0)),
                 pl.BlockSpec((g, 2 * H, LANE), lambda b, t: (blk_f(b, t), 0, 0)),
                 pl.BlockSpec((g, 2 * H, LANE), lambda b, t: (blk_b(b, t), 0, 0))]
    args += [att, att, el, el]
    return pl.pallas_call(
        functools.partial(_gdn_scan_kernel, g),
        grid=(nb, nblk + 1),
        in_specs=in_specs,
        out_specs=[pl.BlockSpec((ctx_len, gw), lambda b, t: (blk_f(b, t), 0)),
                   pl.BlockSpec((ctx_len, gw), lambda b, t: (blk_b(b, t), 0))],
        out_shape=[jax.ShapeDtypeStruct((m, gw), F32), jax.ShapeDtypeStruct((m, gw), F32)],
        scratch_shapes=[pltpu.VMEM((2 * H, HEAD_DIM, HEAD_DIM), F32)],
        compiler_params=_cp("parallel", "arbitrary"),
        name="gdn_scan",
    )(*args)


def _head_norm_gate(o, nw, gate):
    o = o * lax.rsqrt(jnp.mean(o * o, axis=-1, keepdims=True) + RMS_EPS) * nw
    return o * _silu(gate)


def _combine_kernel(of_ref, ob_ref, g_ref, nw_ref, y_ref):
    nw = nw_ref[...]
    for h in range(N_HEADS):
        hs = slice(h * LANE, (h + 1) * LANE)
        y_ref[:, hs] = _head_norm_gate(of_ref[:, hs] + ob_ref[:, hs], nw, g_ref[:, hs]).astype(y_ref.dtype)


def _combine(o_f, o_b, z, gate_cb, norm_w, m_out, tm=512):
    gw = GROUP_WIDTH
    spec = pl.BlockSpec((tm, gw), lambda i: (i, 0))
    return pl.pallas_call(
        _combine_kernel,
        grid=(m_out // tm,),
        in_specs=[spec, spec, pl.BlockSpec((tm, gw), lambda i: (i, gate_cb * LANE // gw)),
                  pl.BlockSpec((1, LANE), lambda i: (0, 0))],
        out_specs=spec,
        out_shape=jax.ShapeDtypeStruct((m_out, gw), BF16),
        compiler_params=_cp("parallel"),
        name="combine",
    )(o_f, o_b, z, norm_w.reshape(1, LANE))


def _gla_chunk(rev, zq, zf, zi, lb, tri, St):
    C, nsub = CHUNK, CHUNK // SUB
    q = _silu(zq) * HEAD_DIM ** -0.5
    f = lb + (1.0 - lb) * jax.nn.sigmoid(zf)
    k = 1.0 - f
    cum = _dot(tri, jnp.log(f), precision=HIGHEST)
    last = cum[0:1, :] if rev else cum[C - 1:C, :]
    vb = zi.astype(BF16)
    o = _dot_nt((q * jnp.exp(cum)).astype(BF16), St.astype(BF16))
    st_new = St * jnp.exp(last) + _dot_tn(vb, (k * jnp.exp(last - cum)).astype(BF16))
    tsub = lax.broadcasted_iota(jnp.int32, (SUB, HEAD_DIM), 0)
    trow = lax.broadcasted_iota(jnp.int32, (C, HEAD_DIM), 0)
    scol = lax.broadcasted_iota(jnp.int32, (SUB, C), 1)
    blocks = []
    for a in range(nsub):
        sa = slice(a * SUB, (a + 1) * SUB)
        qa, ka, ca = q[sa], k[sa], cum[sa]
        if rev and a < nsub - 1:
            cb = cum[(a + 1) * SUB:(a + 1) * SUB + 1, :]
            kt = k * jnp.exp(jnp.where(trow >= (a + 1) * SUB, cb - cum, NEG))
            sc = _dot_nt((qa * jnp.exp(ca - cb)).astype(BF16), kt.astype(BF16))
        elif (not rev) and a > 0:
            cb = cum[a * SUB - 1:a * SUB, :]
            kt = k * jnp.exp(jnp.where(trow < a * SUB, cb - cum, NEG))
            sc = _dot_nt((qa * jnp.exp(ca - cb)).astype(BF16), kt.astype(BF16))
        else:
            sc = jnp.zeros((SUB, C), F32)
        for j in range(SUB):
            ok = (tsub <= j) if rev else (tsub >= j)
            dec = jnp.exp(jnp.where(ok, ca - ca[j:j + 1, :], NEG))
            sj = jnp.sum(qa * ka[j:j + 1, :] * dec, axis=-1, keepdims=True)
            sc = jnp.where(scol == a * SUB + j, sj, sc)
        blocks.append(sc)
    scores = jnp.concatenate(blocks, axis=0)
    return o + _dot(scores.astype(BF16), vb), st_new


def _gla_scan_kernel(emit_ctx, n_lat, n_ctx, *refs):
    (ql_ref, ffl_ref, fbl_ref, il_ref, gl_ref,
     qc_ref, ffc_ref, fbc_ref, ic_ref, gc_ref, lb_ref, nw_ref) = refs[:12]
    if emit_ctx:
        yl_ref, yc_ref, sf_ref, sb_ref, of_ref, ob_ref, ocf_ref, ocb_ref = refs[12:]
    else:
        yl_ref, sf_ref, sb_ref, of_ref, ob_ref = refs[12:]
        yc_ref = ocf_ref = ocb_ref = None
    r, c = _tri_masks(CHUNK)
    lo = jnp.where(c <= r, 1.0, 0.0).astype(F32)
    up = jnp.where(c >= r, 1.0, 0.0).astype(F32)
    lb = lb_ref[0]

    def step(rev, q_ref, f_ref, i_ref, s_ref, o_ref, ci):
        sl = pl.ds(pl.multiple_of(ci * CHUNK, CHUNK), CHUNK)
        o, s_new = _gla_chunk(rev, q_ref[sl, :], f_ref[sl, :], i_ref[sl, :], lb, up if rev else lo, s_ref[...])
        s_ref[...] = s_new
        if o_ref is not None:
            o_ref[sl, :] = o

    sf_ref[...] = jnp.zeros_like(sf_ref)
    sb_ref[...] = jnp.zeros_like(sb_ref)

    def ctx_body(i, carry):
        step(False, qc_ref, ffc_ref, ic_ref, sf_ref, ocf_ref, i)
        step(True, qc_ref, fbc_ref, ic_ref, sb_ref, ocb_ref, n_ctx - 1 - i)
        return carry

    lax.fori_loop(0, n_ctx, ctx_body, 0)

    def lat_body(i, carry):
        step(False, ql_ref, ffl_ref, il_ref, sf_ref, of_ref, i)
        step(True, ql_ref, fbl_ref, il_ref, sb_ref, ob_ref, n_lat - 1 - i)
        return carry

    lax.fori_loop(0, n_lat, lat_body, 0)

    nw = nw_ref[...]
    rb = 256

    def fin(o1, o2, g_ref, y_ref, n):
        def body(t, carry):
            sl = pl.ds(pl.multiple_of(t * rb, rb), rb)
            y_ref[sl, :] = _head_norm_gate(o1[sl, :] + o2[sl, :], nw, g_ref[sl, :]).astype(y_ref.dtype)
            return carry
        lax.fori_loop(0, n // rb, body, 0)

    fin(of_ref, ob_ref, gl_ref, yl_ref, n_lat * CHUNK)
    if emit_ctx:
        fin(ocf_ref, ocb_ref, gc_ref, yc_ref, n_ctx * CHUNK)


def _gla_scan(z, lbs, norm_w, nb, seq, ctx_len, emit_ctx):
    n_lat, n_ctx = seq // CHUNK, ctx_len // CHUNK
    cblk0 = nb * seq // ctx_len
    lat = lambda cb: pl.BlockSpec((seq, LANE), lambda b, h: (b, cb + h))
    ctx = lambda cb: pl.BlockSpec((ctx_len, LANE), lambda b, h: (cblk0 + b, cb + h))
    cbs = (CB_HQ, CB_HFF, CB_HFB, CB_HI, CB_HG)
    in_specs = ([lat(cb) for cb in cbs] + [ctx(cb) for cb in cbs]
                + [pl.BlockSpec((1, 1, LANE), lambda b, h: (h, 0, 0)),
                   pl.BlockSpec((1, LANE), lambda b, h: (0, 0))])
    out_specs = [pl.BlockSpec((seq, LANE), lambda b, h: (b, h))]
    out_shape = [jax.ShapeDtypeStruct((nb * seq, GROUP_WIDTH), BF16)]
    scratch = [pltpu.VMEM((HEAD_DIM, HEAD_DIM), F32), pltpu.VMEM((HEAD_DIM, HEAD_DIM), F32),
               pltpu.VMEM((seq, LANE), F32), pltpu.VMEM((seq, LANE), F32)]
    if emit_ctx:
        out_specs.append(pl.BlockSpec((ctx_len, LANE), lambda b, h: (b, h)))
        out_shape.append(jax.ShapeDtypeStruct((nb * ctx_len, GROUP_WIDTH), BF16))
        scratch += [pltpu.VMEM((ctx_len, LANE), F32), pltpu.VMEM((ctx_len, LANE), F32)]
    return pl.pallas_call(
        functools.partial(_gla_scan_kernel, emit_ctx, n_lat, n_ctx),
        grid=(nb, N_HEADS),
        in_specs=in_specs, out_specs=out_specs, out_shape=out_shape, scratch_shapes=scratch,
        compiler_params=_cp("parallel", "parallel"),
        name="hgrn_scan",
    )(*([z] * 10), lbs.reshape(N_HEADS, 1, LANE), norm_w.reshape(1, LANE))


def _softmax_pv(parts):
    m = parts[0][0].max(axis=-1, keepdims=True)
    for s, _ in parts[1:]:
        m = jnp.maximum(m, s.max(axis=-1, keepdims=True))
    den, acc = None, None
    for s, v in parts:
        p = jnp.exp(s - m)
        d = jnp.sum(p, axis=-1, keepdims=True)
        a = _dot(p.astype(BF16), v)
        den = d if den is None else den + d
        acc = a if acc is None else acc + a
    return acc / den


def _na_kernel(emit_ctx, rows, *refs):
    q_ref, k_ref, v_ref, qc_ref, kc_ref, vc_ref, bias_ref = refs[:7]
    if emit_ctx:
        yl_ref, yc_ref, kb_ref, vb_ref = refs[7:]
    else:
        yl_ref, kb_ref, vb_ref = refs[7:]
    scale = HEAD_DIM ** -0.5
    win = NA_ROWS * GRID_W
    kb_ref[...] = k_ref[...].astype(BF16)
    vb_ref[...] = v_ref[...].astype(BF16)
    kc = kc_ref[...].astype(BF16)
    vc = vc_ref[...].astype(BF16)

    def body(i, carry):
        pre = []
        for t in range(NA_UNROLL):
            r = i * NA_UNROLL + t
            row0 = jnp.clip(r - NA_ROWS // 2, 0, rows - NA_ROWS)
            qs = pl.ds(pl.multiple_of(r * GRID_W, GRID_W), GRID_W)
            ks = pl.ds(pl.multiple_of(row0 * GRID_W, GRID_W), win)
            q = q_ref[qs, :].astype(BF16)
            pre.append((qs, ks, _dot_nt(q, kb_ref[ks, :]) * scale + bias_ref[0, r - row0], _dot_nt(q, kc) * scale))
        mid = []
        for qs, ks, s_win, s_ctx in pre:
            m = jnp.maximum(s_win.max(axis=-1, keepdims=True), s_ctx.max(axis=-1, keepdims=True))
            p_win, p_ctx = jnp.exp(s_win - m), jnp.exp(s_ctx - m)
            den = jnp.sum(p_win, axis=-1, keepdims=True) + jnp.sum(p_ctx, axis=-1, keepdims=True)
            mid.append((qs, ks, p_win.astype(BF16), p_ctx.astype(BF16), den))
        for qs, ks, p_win, p_ctx, den in mid:
            yl_ref[qs, :] = ((_dot(p_win, vb_ref[ks, :]) + _dot(p_ctx, vc)) / den).astype(yl_ref.dtype)
        return carry

    lax.fori_loop(0, rows // NA_UNROLL, body, 0)
    if emit_ctx:
        s = _dot_nt(qc_ref[...].astype(BF16), kc) * scale
        yc_ref[...] = _softmax_pv([(s, vc)]).astype(yc_ref.dtype)


def _na_bias_kernel(rpb_ref, o_ref):
    n = lax.broadcasted_iota(jnp.int32, (LANE, GRID_W * GRID_W), 1)
    j = lax.broadcasted_iota(jnp.int32, (LANE, GRID_W * GRID_W), 0)
    q, w = n >> 6, n & (GRID_W - 1)
    dc = jnp.clip(w - q, 1 - NA_COLS, NA_COLS - 1) + NA_COLS - 1
    onehot = jnp.where(dc == j, 1.0, 0.0).astype(F32)
    m = _dot(rpb_ref[...], onehot, precision=HIGHEST)
    c0 = jnp.clip(q[0:1] - NA_COLS // 2, 0, GRID_W - NA_COLS)
    ok = (w[0:1] >= c0) & (w[0:1] < c0 + NA_COLS)
    o_ref[...] = jnp.where(ok, m, NEG)


def _na_bias_tables(rpb):
    depth, H, nr, nc = rpb.shape
    assert GRID_W == 64 and depth * H * nr <= LANE and nc <= LANE
    flat = jnp.zeros((LANE, LANE), F32).at[:depth * H * nr, :nc].set(rpb.reshape(-1, nc).astype(F32))
    m = pl.pallas_call(
        _na_bias_kernel,
        out_shape=jax.ShapeDtypeStruct((LANE, GRID_W * GRID_W), F32),
        compiler_params=pltpu.CompilerParams(vmem_limit_bytes=VMEM_LIMIT),
        name="na_bias",
    )(flat)
    m = m[:depth * H * nr].reshape(depth, H, nr, GRID_W, GRID_W)
    tab = jnp.stack([jnp.stack([m[:, :, k - s + NA_ROWS - 1] for k in range(NA_ROWS)], axis=3)
                     for s in range(NA_ROWS)], axis=2)
    return tab.reshape(depth, H, NA_ROWS, GRID_W, NA_ROWS * GRID_W)


def _na(z, bias, nb, seq, ctx_len, emit_ctx):
    rows = seq // GRID_W
    cblk0 = nb * seq // ctx_len
    H = N_HEADS
    lat = lambda cb: pl.BlockSpec((seq, LANE), lambda b, h: (b, cb + h))
    ctx = lambda cb: pl.BlockSpec((ctx_len, LANE), lambda b, h: (cblk0 + b, cb + h))
    win = NA_ROWS * GRID_W
    in_specs = [lat(CB_NAQ), lat(CB_NAK), lat(CB_NAV), ctx(CB_NAQ), ctx(CB_NAK), ctx(CB_NAV),
                pl.BlockSpec((1, NA_ROWS, GRID_W, win), lambda b, h: (h, 0, 0, 0))]
    out_specs = [pl.BlockSpec((seq, LANE), lambda b, h: (b, h))]
    out_shape = [jax.ShapeDtypeStruct((nb * seq, GROUP_WIDTH), BF16)]
    if emit_ctx:
        out_specs.append(pl.BlockSpec((ctx_len, LANE), lambda b, h: (b, h)))
        out_shape.append(jax.ShapeDtypeStruct((nb * ctx_len, GROUP_WIDTH), BF16))
    return pl.pallas_call(
        functools.partial(_na_kernel, emit_ctx, rows),
        grid=(nb, H),
        in_specs=in_specs, out_specs=out_specs, out_shape=out_shape,
        scratch_shapes=[pltpu.VMEM((seq, LANE), BF16), pltpu.VMEM((seq, LANE), BF16)],
        compiler_params=_cp("parallel", "parallel"),
        name="na_attn",
    )(z, z, z, z, z, z, bias)


def _rms(x, w):
    return x * lax.rsqrt(jnp.mean(x * x, axis=-1, keepdims=True) + RMS_EPS) * w


def _mla_prep_kernel(cq_ref, ckv_ref, kra_ref, krb_ref, cc_ref, ss_ref, qnw_ref, kvnw_ref, wuq_ref, wukv_ref,
                     q_ref, kn_ref, kr_ref, v_ref):
    cc, ss = cc_ref[...], ss_ref[...]
    qn = _rms(cq_ref[...], qnw_ref[...]).astype(BF16)
    qa = _dot(qn, wuq_ref[...])
    for h in range(N_HEADS):
        b = 3 * LANE * h
        q_ref[:, 2 * LANE * h:2 * LANE * h + LANE] = qa[:, b:b + LANE].astype(BF16)
        q_ref[:, 2 * LANE * h + LANE:2 * LANE * (h + 1)] = (
            qa[:, b + LANE:b + 2 * LANE] * cc + qa[:, b + 2 * LANE:b + 3 * LANE] * ss).astype(BF16)
    kvn = _rms(ckv_ref[...], kvnw_ref[...]).astype(BF16)
    kv = _dot(kvn, wukv_ref[...])
    kn_ref[...] = kv[:, :GROUP_WIDTH].astype(BF16)
    v_ref[...] = kv[:, GROUP_WIDTH:].astype(BF16)
    kr_ref[...] = (kra_ref[...] * cc + krb_ref[...] * ss).astype(BF16)


def _mla_prep(z, cc, ss, qnw, kvnw, wuq, wukv, n_lat, seq, tm=512):
    m = z.shape[0]
    nlt, spt = n_lat // tm, seq // tm
    tab = lambda i: jnp.where(i < nlt, i % spt, spt)
    H = N_HEADS
    return pl.pallas_call(
        _mla_prep_kernel,
        grid=(m // tm,),
        in_specs=[pl.BlockSpec((tm, MLA_Q_RANK), lambda i: (i, CB_MQ * LANE // MLA_Q_RANK)),
                  pl.BlockSpec((tm, MLA_KV_RANK), lambda i: (i, CB_MKV * LANE // MLA_KV_RANK)),
                  pl.BlockSpec((tm, LANE), lambda i: (i, CB_MKRA)),
                  pl.BlockSpec((tm, LANE), lambda i: (i, CB_MKRB)),
                  pl.BlockSpec((tm, LANE), lambda i: (tab(i), 0)),
                  pl.BlockSpec((tm, LANE), lambda i: (tab(i), 0)),
                  pl.BlockSpec((1, MLA_Q_RANK), lambda i: (0, 0)),
                  pl.BlockSpec((1, MLA_KV_RANK), lambda i: (0, 0)),
                  pl.BlockSpec(wuq.shape, lambda i: (0, 0)),
                  pl.BlockSpec(wukv.shape, lambda i: (0, 0))],
        out_specs=[pl.BlockSpec((tm, 2 * LANE * H), lambda i: (i, 0)),
                   pl.BlockSpec((tm, GROUP_WIDTH), lambda i: (i, 0)),
                   pl.BlockSpec((tm, LANE), lambda i: (i, 0)),
                   pl.BlockSpec((tm, GROUP_WIDTH), lambda i: (i, 0))],
        out_shape=[jax.ShapeDtypeStruct((m, 2 * LANE * H), BF16),
                   jax.ShapeDtypeStruct((m, GROUP_WIDTH), BF16),
                   jax.ShapeDtypeStruct((m, LANE), BF16),
                   jax.ShapeDtypeStruct((m, GROUP_WIDTH), BF16)],
        compiler_params=_cp("parallel"),
        name="mla_prep",
    )(z, z, z, z, cc, ss, qnw.reshape(1, -1), kvnw.reshape(1, -1), wuq, wukv)


def _mla_attn_kernel(with_lat, seq, *refs):
    if with_lat:
        q_ref, knl_ref, krl_ref, vl_ref, knc_ref, krc_ref, vc_ref, y_ref, k_scr = refs
    else:
        q_ref, knc_ref, krc_ref, vc_ref, y_ref, k_scr = refs
    scale = MLA_QK_DIM ** -0.5
    nk = k_scr.shape[0]

    @pl.when(pl.program_id(2) == 0)
    def _():
        if with_lat:
            k_scr[0:seq, 0:LANE] = knl_ref[...]
            k_scr[0:seq, LANE:2 * LANE] = krl_ref[...]
        k_scr[nk - knc_ref.shape[0]:nk, 0:LANE] = knc_ref[...]
        k_scr[nk - knc_ref.shape[0]:nk, LANE:2 * LANE] = krc_ref[...]

    q = q_ref[...]
    parts = []
    if with_lat:
        parts.append((_dot_nt(q, k_scr[0:seq, :]) * scale, vl_ref[...]))
    parts.append((_dot_nt(q, k_scr[nk - knc_ref.shape[0]:nk, :]) * scale, vc_ref[...]))
    y_ref[...] = _softmax_pv(parts).astype(y_ref.dtype)


def _mla_attn(q, kn, kr, v, nb, seq, ctx_len, with_lat, tq=256):
    H = N_HEADS
    cblk0 = nb * seq // ctx_len
    nq = seq if with_lat else ctx_len
    tq = min(tq, nq)
    qblk0 = 0 if with_lat else nb * seq // tq
    ctxs = [pl.BlockSpec((ctx_len, LANE), lambda b, h, i: (cblk0 + b, h)),
            pl.BlockSpec((ctx_len, LANE), lambda b, h, i: (cblk0 + b, 0)),
            pl.BlockSpec((ctx_len, LANE), lambda b, h, i: (cblk0 + b, h))]
    lats = [pl.BlockSpec((seq, LANE), lambda b, h, i: (b, h)),
            pl.BlockSpec((seq, LANE), lambda b, h, i: (b, 0)),
            pl.BlockSpec((seq, LANE), lambda b, h, i: (b, h))]
    in_specs = [pl.BlockSpec((tq, 2 * LANE), lambda b, h, i: (qblk0 + b * (nq // tq) + i, h))]
    args = [q]
    if with_lat:
        in_specs += lats
        args += [kn, kr, v]
    in_specs += ctxs
    args += [kn, kr, v]
    nk = (seq if with_lat else 0) + ctx_len
    return pl.pallas_call(
        functools.partial(_mla_attn_kernel, with_lat, seq),
        grid=(nb, H, nq // tq),
        in_specs=in_specs,
        out_specs=pl.BlockSpec((tq, LANE), lambda b, h, i: (b * (nq // tq) + i, h)),
        out_shape=jax.ShapeDtypeStruct((nb * nq, GROUP_WIDTH), BF16),
        scratch_shapes=[pltpu.VMEM((nk, 2 * LANE), BF16)],
        compiler_params=_cp("parallel", "parallel", "arbitrary"),
        name="mla_attn_lat" if with_lat else "mla_attn_ctx",
    )(*args)


def _prep_w_in(w_in):
    d = w_in.shape[0]
    gw = GROUP_WIDTH
    o_na = 4 * gw + 4 * N_HEADS
    o_mla = o_na + 3 * gw
    o_kr = o_mla + MLA_Q_RANK + MLA_KV_RANK
    o_hg = o_kr + MLA_ROPE
    kr = w_in[:, o_kr:o_hg]
    k1, k2 = kr[:, 0::2], kr[:, 1::2]
    z = lambda n: jnp.zeros((d, n), w_in.dtype)
    o_kv = o_mla + MLA_Q_RANK
    cols = [w_in[:, o_mla:o_kv], k1, k2, z(LANE - MLA_ROPE),
            w_in[:, o_kv:o_kr], k2, k1, z(LANE - MLA_ROPE),
            w_in[:, 4 * gw:o_na], z(LANE - 4 * N_HEADS),
            w_in[:, :4 * gw], w_in[:, o_na:o_mla], w_in[:, o_hg:]]
    w = jnp.concatenate(cols, axis=1).astype(BF16)
    assert w.shape[1] == NP_IN
    return w


def _prep_w_uq(w_uq):
    r = w_uq.shape[0]
    z = jnp.zeros((r, LANE - MLA_ROPE), w_uq.dtype)
    cols = []
    for h in range(N_HEADS):
        wh = w_uq[:, h * MLA_QK_DIM:(h + 1) * MLA_QK_DIM]
        rope = wh[:, MLA_NOPE:]
        r1, r2 = rope[:, 0::2], rope[:, 1::2]
        cols += [wh[:, :MLA_NOPE], r1, r2, z, r2, r1, z]
    return jnp.concatenate(cols, axis=1).astype(BF16)


def _rope_tables(seq, tm):
    n_freq = MLA_ROPE // 4
    freqs = ROPE_BASE ** (-jnp.arange(n_freq, dtype=F32) / n_freq)
    t = jnp.arange(seq)
    ang = jnp.concatenate([(t // GRID_W).astype(F32)[:, None] * freqs,
                           (t % GRID_W).astype(F32)[:, None] * freqs], -1)
    cos, sin = jnp.cos(ang), jnp.sin(ang)
    zp = jnp.zeros((seq, LANE - MLA_ROPE), F32)
    cc = jnp.concatenate([cos, cos, zp], axis=1)
    ss = jnp.concatenate([-sin, sin, zp], axis=1)
    ident = jnp.zeros((tm, LANE), F32).at[:, :MLA_ROPE].set(1.0)
    return jnp.concatenate([cc, ident], axis=0), jnp.concatenate([ss, jnp.zeros((tm, LANE), F32)], axis=0)


def kernel(x, c, ctx, c_ctx, w_ada, b_ada, w_in, gdn_conv_w, gdn_a_log, gdn_dt_bias, gdn_norm_w, na_rpb,
           mla_q_norm_w, mla_kv_norm_w, mla_w_uq, mla_w_uk, mla_w_uv, hgrn_lower_bounds, hgrn_norm_w, w_out,
           ln1_w, ln1_b, w_mlp1, w_mlp2, ln2_w, ln2_b):
    nb, seq, d = x.shape
    ctx_len = ctx.shape[1]
    depth = w_ada.shape[0]
    n_lat, n_ctx = nb * seq, nb * ctx_len
    alpha = (2 * depth) ** 0.25
    tm = 512
    tmm = 1024 if (seq % 1024 == 0 and n_ctx % 1024 == 0) else tm
    assert nb < 8 and seq % tm == 0 and n_ctx % tm == 0 and seq % ctx_len == 0 and ctx_len % CHUNK == 0

    cin = jnp.zeros((8, d), F32).at[:nb].set(c).at[nb].set(c_ctx)
    ada = _ada(cin, w_ada, b_ada)
    p_lb = jax.nn.softmax(hgrn_lower_bounds.astype(F32), axis=0)
    lbs = jnp.cumsum(p_lb, axis=0) - p_lb[0]
    cc, ss = _rope_tables(seq, tm)
    na_bias = _na_bias_tables(na_rpb)

    x_all = jnp.concatenate([x.reshape(n_lat, d), ctx.reshape(n_ctx, d)], axis=0)
    for l in range(depth):
        emit_ctx = l < depth - 1
        ada_r = ada[l].reshape(8 * 6, 1, d)
        z = _inproj(x_all, ada_r, _prep_w_in(w_in[l]), n_lat, seq, nb, tm=tmm)

        qkv = _gdn_conv(z, gdn_conv_w[l], nb, seq, ctx_len)
        p = _gdn_gates(z, gdn_a_log[l], gdn_dt_bias[l], tm=tm)
        ct = p[:, :2 * N_HEADS].reshape(-1, CHUNK, 2 * N_HEADS).transpose(0, 2, 1)
        o_f, o_b = _gdn_scan(*_gdn_prep(qkv, p, ct, nb, seq, ctx_len), nb, seq, ctx_len)
        m_out = n_lat + n_ctx if emit_ctx else n_lat
        ya = _combine(o_f, o_b, z, CB_GGATE, gdn_norm_w[l], m_out, tm=tm)
        yb = _na(z, na_bias[l], nb, seq, ctx_len, emit_ctx)
        q, kn, kr, v = _mla_prep(z, cc, ss, mla_q_norm_w[l], mla_kv_norm_w[l], _prep_w_uq(mla_w_uq[l]),
                                 jnp.concatenate([mla_w_uk[l], mla_w_uv[l]], axis=1).astype(BF16),
                                 n_lat, seq, tm=tm)
        ym = [_mla_attn(q, kn, kr, v, nb, seq, ctx_len, True)]
        if emit_ctx:
            ym.append(_mla_attn(q, kn, kr, v, nb, seq, ctx_len, False))
        yh = _gla_scan(z, lbs[l], hgrn_norm_w[l], nb, seq, ctx_len, emit_ctx)

        if emit_ctx:
            ys = [ya] + [jnp.concatenate(list(t), axis=0) for t in (yb, ym, yh)]
        else:
            ys = [ya, yb[0], ym[0], yh[0]]
        x_all = _outproj(x_all, ys, w_out[l].astype(BF16), ada_r, ln1_w[l], ln1_b[l], m_out, seq, nb, alpha, tm=tm)
        x_all = _mlp(x_all, w_mlp1[l].astype(BF16), w_mlp2[l].astype(BF16), ada_r, ln2_w[l], ln2_b[l],
                     seq, nb, alpha, tm=tmm)
    return x_all[:n_lat].reshape(nb, seq, d)
```

```python
import functools

import numpy as np
import jax
import jax.numpy as jnp
from jax import lax
from jax.experimental import pallas as pl
from jax.experimental.pallas import tpu as pltpu

F32 = jnp.float32
BF16 = jnp.bfloat16
HIGHEST = lax.Precision.HIGHEST

GRID_W = 64
N_HEADS = 4
HEAD_DIM = 128
GROUP_WIDTH = 512
CHUNK = 64
SUB = 8
GDN_CONV = 5
NA_ROWS = 8
NA_COLS = 16
NA_UNROLL = 4
MLA_Q_RANK = 384
MLA_KV_RANK = 256
MLA_NOPE = 128
MLA_ROPE = 64
MLA_QK_DIM = MLA_NOPE + MLA_ROPE
MLA_QSUB = 128
ROPE_BASE = 10000.0
LN_EPS = 1e-5
RMS_EPS = 1e-6
NEG = -1e30

LANE = 128
CB_MQ, CB_MKRA, CB_MKV, CB_MKRB, CB_GAB = 0, 3, 4, 6, 7
CB_GQKV, CB_GGATE = 8, 20
CB_NAQ, CB_NAK, CB_NAV = 24, 28, 32
CB_HQ, CB_HFF, CB_HFB, CB_HI, CB_HG = 36, 40, 44, 48, 52
CHUNK_SHIFT = 6
NP_IN = 56 * LANE

VMEM_LIMIT = 48 << 20


def _cp(*sem):
    return pltpu.CompilerParams(dimension_semantics=sem, vmem_limit_bytes=VMEM_LIMIT)


def _silu(x):
    return x * jax.nn.sigmoid(x)


def _dot(a, b, **kw):
    return jnp.dot(a, b, preferred_element_type=F32, **kw)


def _dot_nt(a, b, **kw):
    return lax.dot_general(a, b, (((1,), (1,)), ((), ())), preferred_element_type=F32, **kw)


def _dot_tn(a, b, **kw):
    return lax.dot_general(a, b, (((0,), (0,)), ((), ())), preferred_element_type=F32, **kw)


def _ada_kernel(c_ref, w_ref, b_ref, o_ref):
    s = _silu(c_ref[...])
    o_ref[0] = _dot(s, w_ref[0], precision=HIGHEST) + b_ref[0]


def _ada(cin, w_ada, b_ada):
    depth, d, n = w_ada.shape
    tn = 512
    return pl.pallas_call(
        _ada_kernel,
        grid=(depth, n // tn),
        in_specs=[pl.BlockSpec((8, d), lambda l, j: (0, 0)),
                  pl.BlockSpec((1, d, tn), lambda l, j: (l, 0, j)),
                  pl.BlockSpec((1, 1, tn), lambda l, j: (l, 0, j))],
        out_specs=pl.BlockSpec((1, 8, tn), lambda l, j: (l, 0, j)),
        out_shape=jax.ShapeDtypeStruct((depth, 8, n), F32),
        compiler_params=_cp("parallel", "parallel"),
        name="ada",
    )(cin, w_ada, b_ada.reshape(depth, 1, n))


ROW_STEP = 256


def _modulate(x_ref, sh_ref, sc_ref, xm_ref):
    sc1, sh = 1.0 + sc_ref[0], sh_ref[0]

    def body(t, carry):
        sl = pl.ds(pl.multiple_of(t * ROW_STEP, ROW_STEP), ROW_STEP)
        xm_ref[sl, :] = (x_ref[sl, :] * sc1 + sh).astype(BF16)
        return carry

    lax.fori_loop(0, x_ref.shape[0] // ROW_STEP, body, 0)


def _inproj_kernel(x_ref, sh_ref, sc_ref, w_ref, o_ref, xm_ref):
    @pl.when(pl.program_id(1) == 0)
    def _():
        _modulate(x_ref, sh_ref, sc_ref, xm_ref)
    o_ref[...] = _dot(xm_ref[...], w_ref[...])


def _inproj(x_all, ada_r, w, n_lat, seq, nb, tm=512, tn=1024):
    m, d = x_all.shape
    n = w.shape[1]
    row = lambda i: jnp.minimum((i * tm) // seq, nb)
    return pl.pallas_call(
        _inproj_kernel,
        grid=(m // tm, n // tn),
        in_specs=[pl.BlockSpec((tm, d), lambda i, j: (i, 0)),
                  pl.BlockSpec((1, 1, d), lambda i, j: (row(i) * 6 + 0, 0, 0)),
                  pl.BlockSpec((1, 1, d), lambda i, j: (row(i) * 6 + 1, 0, 0)),
                  pl.BlockSpec((d, tn), lambda i, j: (0, j))],
        out_specs=pl.BlockSpec((tm, tn), lambda i, j: (i, j)),
        out_shape=jax.ShapeDtypeStruct((m, n), F32),
        scratch_shapes=[pltpu.VMEM((tm, d), BF16)],
        compiler_params=_cp("parallel", "arbitrary"),
        name="inproj",
    )(x_all, ada_r, ada_r, w)


def _layernorm(r, w, b):
    mu = jnp.mean(r, axis=-1, keepdims=True)
    rc = r - mu
    var = jnp.mean(rc * rc, axis=-1, keepdims=True)
    return rc * lax.rsqrt(var + LN_EPS) * w + b


def _outproj_kernel(alpha, x_ref, ya_ref, yb_ref, ym_ref, yh_ref, w_ref, g_ref, lw_ref, lb_ref, o_ref):
    gw = GROUP_WIDTH
    acc = _dot(ya_ref[...], w_ref[0:gw, :])
    acc += _dot(yb_ref[...], w_ref[gw:2 * gw, :])
    acc += _dot(ym_ref[...], w_ref[2 * gw:3 * gw, :])
    acc += _dot(yh_ref[...], w_ref[3 * gw:4 * gw, :])
    r = alpha * x_ref[...] + g_ref[0] * acc
    o_ref[...] = _layernorm(r, lw_ref[...], lb_ref[...])


def _outproj(x_all, ys, w, ada_r, lw, lb, m_out, seq, nb, alpha, tm=512):
    d = x_all.shape[1]
    row = lambda i: jnp.minimum((i * tm) // seq, nb)
    yspec = pl.BlockSpec((tm, GROUP_WIDTH), lambda i: (i, 0))
    return pl.pallas_call(
        functools.partial(_outproj_kernel, alpha),
        grid=(m_out // tm,),
        in_specs=[pl.BlockSpec((tm, d), lambda i: (i, 0)), yspec, yspec, yspec, yspec,
                  pl.BlockSpec((d, d), lambda i: (0, 0)),
                  pl.BlockSpec((1, 1, d), lambda i: (row(i) * 6 + 2, 0, 0)),
                  pl.BlockSpec((1, d), lambda i: (0, 0)),
                  pl.BlockSpec((1, d), lambda i: (0, 0))],
        out_specs=pl.BlockSpec((tm, d), lambda i: (i, 0)),
        out_shape=jax.ShapeDtypeStruct((m_out, d), F32),
        compiler_params=_cp("parallel"),
        name="outproj_ln",
    )(x_all, *ys, w, ada_r, lw.reshape(1, d), lb.reshape(1, d))


def _mlp_kernel(alpha, x_ref, sh_ref, sc_ref, g_ref, w1_ref, w2_ref, lw_ref, lb_ref, o_ref, xm_ref, acc_ref):
    k = pl.program_id(1)

    @pl.when(k == 0)
    def _():
        _modulate(x_ref, sh_ref, sc_ref, xm_ref)
        acc_ref[...] = jnp.zeros_like(acc_ref)

    h = jnp.maximum(_dot(xm_ref[...], w1_ref[...]), 0.0)
    acc_ref[...] += _dot((h * h).astype(BF16), w2_ref[...])

    @pl.when(k == pl.num_programs(1) - 1)
    def _():
        g, lw, lb = g_ref[0], lw_ref[...], lb_ref[...]

        def body(t, carry):
            sl = pl.ds(pl.multiple_of(t * ROW_STEP, ROW_STEP), ROW_STEP)
            o_ref[sl, :] = _layernorm(alpha * x_ref[sl, :] + g * acc_ref[sl, :], lw, lb)
            return carry

        lax.fori_loop(0, x_ref.shape[0] // ROW_STEP, body, 0)


def _mlp(x_all, w1, w2, ada_r, lw, lb, seq, nb, alpha, tm=512, th=512):
    m, d = x_all.shape
    hid = w1.shape[1]
    row = lambda i: jnp.minimum((i * tm) // seq, nb)
    return pl.pallas_call(
        functools.partial(_mlp_kernel, alpha),
        grid=(m // tm, hid // th),
        in_specs=[pl.BlockSpec((tm, d), lambda i, k: (i, 0)),
                  pl.BlockSpec((1, 1, d), lambda i, k: (row(i) * 6 + 3, 0, 0)),
                  pl.BlockSpec((1, 1, d), lambda i, k: (row(i) * 6 + 4, 0, 0)),
                  pl.BlockSpec((1, 1, d), lambda i, k: (row(i) * 6 + 5, 0, 0)),
                  pl.BlockSpec((d, th), lambda i, k: (0, k)),
                  pl.BlockSpec((th, d), lambda i, k: (k, 0)),
                  pl.BlockSpec((1, d), lambda i, k: (0, 0)),
                  pl.BlockSpec((1, d), lambda i, k: (0, 0))],
        out_specs=pl.BlockSpec((tm, d), lambda i, k: (i, 0)),
        out_shape=jax.ShapeDtypeStruct((m, d), F32),
        scratch_shapes=[pltpu.VMEM((tm, d), BF16), pltpu.VMEM((tm, d), F32)],
        compiler_params=_cp("parallel", "arbitrary"),
        name="mlp_ln",
    )(x_all, ada_r, ada_r, ada_r, w1, w2, lw.reshape(1, d), lb.reshape(1, d))


def _gdn_conv_kernel(seq, ctx_len, xl_ref, xc_ref, w_ref, o_ref, pad_ref):
    j = pl.program_id(1)
    w = w_ref[...]
    qscale = jnp.where(j < N_HEADS, HEAD_DIM ** -0.5, 1.0).astype(F32)
    p0 = 8 - GDN_CONV // 2
    for x_ref, nrows, o0 in ((xl_ref, seq, 0), (xc_ref, ctx_len, seq)):
        pad_ref[0:8, :] = jnp.zeros((8, LANE), F32)
        pad_ref[nrows + 8:nrows + 16, :] = jnp.zeros((8, LANE), F32)
        pad_ref[8:nrows + 8, :] = x_ref[...]
        rb = min(nrows, 256)
        for r0 in range(0, nrows, rb):
            y = pad_ref[r0 + p0:r0 + p0 + rb, :] * w[0:1, :]
            for i in range(1, GDN_CONV):
                y = y + pad_ref[r0 + p0 + i:r0 + p0 + i + rb, :] * w[i:i + 1, :]
            y = _silu(y)
            nrm = y * lax.rsqrt(jnp.sum(y * y, axis=-1, keepdims=True) + RMS_EPS) * qscale
            o_ref[o0 + r0:o0 + r0 + rb, :] = jnp.where(j < 2 * N_HEADS, nrm, y)


def _gdn_conv(z, conv_w, nb, seq, ctx_len):
    nblk = 3 * N_HEADS
    cblk0 = nb * seq // ctx_len
    return pl.pallas_call(
        functools.partial(_gdn_conv_kernel, seq, ctx_len),
        grid=(nb, nblk),
        in_specs=[pl.BlockSpec((seq, LANE), lambda b, j: (b, CB_GQKV + j)),
                  pl.BlockSpec((ctx_len, LANE), lambda b, j: (cblk0 + b, CB_GQKV + j)),
                  pl.BlockSpec((GDN_CONV, LANE), lambda b, j: (0, j))],
        out_specs=pl.BlockSpec((seq + ctx_len, LANE), lambda b, j: (b, j)),
        out_shape=jax.ShapeDtypeStruct((nb * (seq + ctx_len), nblk * LANE), F32),
        scratch_shapes=[pltpu.VMEM((seq + 16, LANE), F32)],
        compiler_params=_cp("parallel", "parallel"),
        name="gdn_conv",
    )(z, z, conv_w)


def _gdn_gates_kernel(tm, s_ref, alog_ref, dtb_ref, o_ref):
    s = s_ref[...]
    g = -jnp.exp(alog_ref[...]) * (jnp.maximum(s + dtb_ref[...], 0.0)
                                    + jnp.log1p(jnp.exp(-jnp.abs(s + dtb_ref[...]))))
    r = lax.broadcasted_iota(jnp.int32, (tm, tm), 0)
    c = lax.broadcasted_iota(jnp.int32, (tm, tm), 1)
    same = (r >> CHUNK_SHIFT) == (c >> CHUNK_SHIFT)
    lo = jnp.where(same & (c <= r), 1.0, 0.0).astype(F32)
    up = jnp.where(same & (c >= r), 1.0, 0.0).astype(F32)
    cum_f = _dot(lo, g, precision=HIGHEST)
    cum_b = _dot(up, g, precision=HIGHEST)
    col = lax.broadcasted_iota(jnp.int32, s.shape, 1)
    o_ref[...] = jnp.where(col < N_HEADS, cum_f,
                           jnp.where(col < 2 * N_HEADS, cum_b,
                                     jnp.where(col < 4 * N_HEADS, jax.nn.sigmoid(s), 0.0)))


def _gdn_gates(z, a_log, dt_bias, tm=512):
    m = z.shape[0]
    pad = lambda v: jnp.zeros((1, LANE), F32).at[0, :2 * N_HEADS].set(v.reshape(-1).astype(F32))
    return pl.pallas_call(
        functools.partial(_gdn_gates_kernel, tm),
        grid=(m // tm,),
        in_specs=[pl.BlockSpec((tm, LANE), lambda i: (i, CB_GAB)),
                  pl.BlockSpec((1, LANE), lambda i: (0, 0)),
                  pl.BlockSpec((1, LANE), lambda i: (0, 0))],
        out_specs=pl.BlockSpec((tm, LANE), lambda i: (i, 0)),
        out_shape=jax.ShapeDtypeStruct((m, LANE), F32),
        compiler_params=_cp("parallel"),
        name="gdn_gates",
    )(z, pad(a_log), pad(dt_bias))


def _tri_masks(n):
    r = lax.broadcasted_iota(jnp.int32, (n, n), 0)
    c = lax.broadcasted_iota(jnp.int32, (n, n), 1)
    return r, c


def _split(x):
    hi = x.astype(BF16)
    return hi, (x - hi.astype(F32)).astype(BF16)


def _dot3(a, b):
    return _dot(a[0], b[0]) + (_dot(a[0], b[1]) + _dot(a[1], b[0]))


def _gdn_prep_kernel(qkv_ref, p_ref, ct_ref, u_ref, w_ref, qt_ref, kt_ref, att_ref, el_ref):
    C, H = CHUNK, N_HEADS
    lane = lax.broadcasted_iota(jnp.int32, (C, LANE), 1)
    sub8 = lax.broadcasted_iota(jnp.int32, (2 * H, C), 0)
    pblk = p_ref[...]
    tblk = ct_ref[0]
    col = lambda idx: jnp.sum(jnp.where(lane == idx, pblk, 0.0), axis=-1, keepdims=True)
    row = lambda idx: jnp.sum(jnp.where(sub8 == idx, tblk, 0.0), axis=0, keepdims=True)
    r, c = _tri_masks(C)
    eye = jnp.where(r == c, 1.0, 0.0).astype(F32)
    a_list, rhs_list, where_list = [], [], []
    for h in range(H):
        hs = slice(h * LANE, (h + 1) * LANE)
        q = qkv_ref[:, h * LANE:(h + 1) * LANE]
        k = qkv_ref[:, (H + h) * LANE:(H + h + 1) * LANE]
        v = qkv_ref[:, (2 * H + h) * LANE:(2 * H + h + 1) * LANE]
        qbf, kbf = q.astype(BF16), k.astype(BF16)
        for d in range(2):
            idx = d * H + h
            cum_c, cum_r, beta_c = col(idx), row(idx), col(2 * H + idx)
            incl = (c >= r) if d else (c <= r)
            strict = (c > r) if d else (c < r)
            last = cum_r[:, 0:1] if d else cum_r[:, C - 1:C]
            decay = jnp.exp(jnp.where(incl, cum_c - cum_r, NEG))
            kb = k * beta_c
            ec = jnp.exp(cum_c)
            a_list.append(jnp.where(strict, _dot_nt(kb.astype(BF16), kbf) * decay, 0.0))
            rhs_list.append(_split(jnp.concatenate([v * beta_c, kb * ec], axis=-1)))
            where_list.append((d, hs))
            att_ref[d, h] = jnp.where(incl, _dot_nt(qbf, kbf) * decay, 0.0).astype(BF16)
            qt_ref[d, :, hs] = (q * ec).astype(BF16)
            kt_ref[d, :, hs] = (k * jnp.exp(last - cum_c)).astype(BF16)
            el_ref[0, idx:idx + 1, :] = jnp.broadcast_to(jnp.exp(last), (1, LANE))
    ts = [eye - a for a in a_list]
    ps = [_split(a) for a in a_list]
    for _ in range(5):
        ps = [_split(_dot3(p, p)) for p in ps]
        ts = [t + _dot3(_split(t), p) for t, p in zip(ts, ps)]
    sols = [_dot3(_split(t), rhs) for t, rhs in zip(ts, rhs_list)]
    for sol, (d, hs) in zip(sols, where_list):
        u_ref[d, :, hs] = sol[:, :HEAD_DIM]
        w_ref[d, :, hs] = sol[:, HEAD_DIM:].astype(BF16)


def _gdn_prep(qkv, p, ct, nb, seq, ctx_len):
    m = qkv.shape[0]
    H = N_HEADS
    gw = GROUP_WIDTH
    dspec = pl.BlockSpec((2, CHUNK, gw), lambda i: (0, i, 0))
    nl, nc = seq // CHUNK, ctx_len // CHUNK

    def qkv_chunk(i):
        ic = i - nb * nl
        return jnp.where(i < nb * nl, (i // nl) * (nl + nc) + i % nl, (ic // nc) * (nl + nc) + nl + ic % nc)

    return pl.pallas_call(
        _gdn_prep_kernel,
        grid=(m // CHUNK,),
        in_specs=[pl.BlockSpec((CHUNK, 3 * gw), lambda i: (qkv_chunk(i), 0)),
                  pl.BlockSpec((CHUNK, LANE), lambda i: (i, 0)),
                  pl.BlockSpec((1, 2 * H, CHUNK), lambda i: (i, 0, 0))],
        out_specs=[dspec, dspec, dspec, dspec,
                   pl.BlockSpec((2, H, CHUNK, CHUNK), lambda i: (0, 0, i, 0)),
                   pl.BlockSpec((1, 2 * H, LANE), lambda i: (i, 0, 0))],
        out_shape=[jax.ShapeDtypeStruct((2, m, gw), F32),
                   jax.ShapeDtypeStruct((2, m, gw), BF16),
                   jax.ShapeDtypeStruct((2, m, gw), BF16),
                   jax.ShapeDtypeStruct((2, m, gw), BF16),
                   jax.ShapeDtypeStruct((2, H, m, CHUNK), BF16),
                   jax.ShapeDtypeStruct((m // CHUNK, 2 * H, LANE), F32)],
        compiler_params=_cp("parallel"),
        name="gdn_prep",
    )(qkv, p, ct)


def _gdn_scan_kernel(g, *refs):
    ins, (of_ref, ob_ref, s_ref) = refs[:12], refs[12:]
    H = N_HEADS

    @pl.when(pl.program_id(1) == 0)
    def _():
        s_ref[...] = jnp.zeros_like(s_ref)

    chains = [(d, h) for d in range(2) for h in range(H)]
    for j in range(g):
        mid = []
        for d, h in chains:
            u_ref, w_ref, qt_ref = ins[d:6:2]
            cj = g - 1 - j if d else j
            rs, hs = slice(cj * CHUNK, (cj + 1) * CHUNK), slice(h * LANE, (h + 1) * LANE)
            sb = s_ref[d * H + h].astype(BF16)
            vnb = (u_ref[0, rs, hs] - _dot(w_ref[0, rs, hs], sb)).astype(BF16)
            mid.append((vnb, _dot(qt_ref[0, rs, hs], sb)))
        for (d, h), (vnb, o_state) in zip(chains, mid):
            kt_ref, att_ref, el_ref = ins[6 + d::2]
            o_ref = ob_ref if d else of_ref
            cj = g - 1 - j if d else j
            rs, hs = slice(cj * CHUNK, (cj + 1) * CHUNK), slice(h * LANE, (h + 1) * LANE)
            idx = d * H + h
            s_ref[idx] = s_ref[idx] * el_ref[cj, idx:idx + 1, :] + _dot_tn(kt_ref[0, rs, hs], vnb)
            o_ref[rs, hs] = o_state + _dot(att_ref[0, h, rs, :], vnb)


def _gdn_scan(u, w, qt, kt, att, el, nb, seq, ctx_len):
    m = u.shape[1]
    H = N_HEADS
    gw = GROUP_WIDTH
    g = ctx_len // CHUNK
    nblk = seq // ctx_len
    cblk0 = nb * nblk
    blk_f = lambda b, t: jnp.where(t == 0, cblk0 + b, b * nblk + t - 1)
    blk_b = lambda b, t: jnp.where(t == 0, cblk0 + b, b * nblk + nblk - t)
    in_specs, args = [], []
    for arr in (u, w, qt, kt):
        in_specs += [pl.BlockSpec((1, ctx_len, gw), lambda b, t: (0, blk_f(b, t), 0)),
                     pl.BlockSpec((1, ctx_len, gw), lambda b, t: (1, blk_b(b, t), 0))]
        args += [arr, arr]
    in_specs += [pl.BlockSpec((1, H, ctx_len, CHUNK), lambda b, t: (0, 0, blk_f(b, t), 0)),
                 pl.BlockSpec((1, H, ctx_len, CHUNK), lambda b, t: (1, 0, blk_b(b, t), 0)),
                 pl.BlockSpec((g, 2 * H, LANE), lambda b, t: (blk_f(b, t), 0, 0)),
                 pl.BlockSpec((g, 2 * H, LANE), lambda b, t: (blk_b(b, t), 0, 0))]
    args += [att, att, el, el]
    return pl.pallas_call(
        functools.partial(_gdn_scan_kernel, g),
        grid=(nb, nblk + 1),
        in_specs=in_specs,
        out_specs=[pl.BlockSpec((ctx_len, gw), lambda b, t: (blk_f(b, t), 0)),
                   pl.BlockSpec((ctx_len, gw), lambda b, t: (blk_b(b, t), 0))],
        out_shape=[jax.ShapeDtypeStruct((m, gw), F32), jax.ShapeDtypeStruct((m, gw), F32)],
        scratch_shapes=[pltpu.VMEM((2 * H, HEAD_DIM, HEAD_DIM), F32)],
        compiler_params=_cp("parallel", "arbitrary"),
        name="gdn_scan",
    )(*args)


def _head_norm_gate(o, nw, gate):
    o = o * lax.rsqrt(jnp.mean(o * o, axis=-1, keepdims=True) + RMS_EPS) * nw
    return o * _silu(gate)


def _combine_kernel(of_ref, ob_ref, g_ref, nw_ref, y_ref):
    nw = nw_ref[...]
    for h in range(N_HEADS):
        hs = slice(h * LANE, (h + 1) * LANE)
        y_ref[:, hs] = _head_norm_gate(of_ref[:, hs] + ob_ref[:, hs], nw, g_ref[:, hs]).astype(y_ref.dtype)


def _combine(o_f, o_b, z, gate_cb, norm_w, m_out, tm=512):
    gw = GROUP_WIDTH
    spec = pl.BlockSpec((tm, gw), lambda i: (i, 0))
    return pl.pallas_call(
        _combine_kernel,
        grid=(m_out // tm,),
        in_specs=[spec, spec, pl.BlockSpec((tm, gw), lambda i: (i, gate_cb * LANE // gw)),
                  pl.BlockSpec((1, LANE), lambda i: (0, 0))],
        out_specs=spec,
        out_shape=jax.ShapeDtypeStruct((m_out, gw), BF16),
        compiler_params=_cp("parallel"),
        name="combine",
    )(o_f, o_b, z, norm_w.reshape(1, LANE))


def _gla_kernel(g, qf_ref, ff_ref, if_ref, qb_ref, fb_ref, ib_ref, lb_ref, of_ref, ob_ref, s_ref):
    C, H, nsub = CHUNK, N_HEADS, CHUNK // SUB

    @pl.when(pl.program_id(1) == 0)
    def _():
        s_ref[...] = jnp.zeros_like(s_ref)

    r, c = _tri_masks(C)
    tri = (jnp.where(c <= r, 1.0, 0.0).astype(F32), jnp.where(c >= r, 1.0, 0.0).astype(F32))
    trow = lax.broadcasted_iota(jnp.int32, (C, HEAD_DIM), 0)
    srow = lax.broadcasted_iota(jnp.int32, (SUB, C), 0)
    scol = lax.broadcasted_iota(jnp.int32, (SUB, C), 1)
    chains = [(d, h) for d in range(2) for h in range(H)]
    srcs = ((qf_ref, ff_ref, if_ref, of_ref), (qb_ref, fb_ref, ib_ref, ob_ref))

    def body(j, carry):
        ph1 = []
        for d, h in chains:
            cj = g - 1 - j if d else j
            rs, hs = pl.ds(pl.multiple_of(cj * C, C), C), slice(h * LANE, (h + 1) * LANE)
            lb = lb_ref[:, hs]
            f = lb + (1.0 - lb) * jax.nn.sigmoid(srcs[d][1][rs, hs])
            ph1.append((rs, hs, f, _dot(tri[d], jnp.log(f), precision=HIGHEST)))
        ph2 = []
        for (d, h), (rs, hs, f, cum) in zip(chains, ph1):
            q_ref, _, i_ref, _ = srcs[d]
            q = _silu(q_ref[rs, hs]) * HEAD_DIM ** -0.5
            k = 1.0 - f
            vb = i_ref[rs, hs].astype(BF16)
            last = cum[0:1, :] if d else cum[C - 1:C, :]
            idx = d * H + h
            St = s_ref[idx]
            o_state = _dot_nt((q * jnp.exp(cum)).astype(BF16), St.astype(BF16))
            s_ref[idx] = St * jnp.exp(last) + _dot_tn(vb, (k * jnp.exp(last - cum)).astype(BF16))
            inter = []
            for a in range(nsub):
                sa = slice(a * SUB, (a + 1) * SUB)
                if d and a < nsub - 1:
                    cb = cum[(a + 1) * SUB:(a + 1) * SUB + 1, :]
                    kt = k * jnp.exp(jnp.where(trow >= (a + 1) * SUB, cb - cum, NEG))
                elif (not d) and a > 0:
                    cb = cum[a * SUB - 1:a * SUB, :]
                    kt = k * jnp.exp(jnp.where(trow < a * SUB, cb - cum, NEG))
                else:
                    inter.append(jnp.zeros((SUB, C), F32))
                    continue
                inter.append(_dot_nt((q[sa] * jnp.exp(cum[sa] - cb)).astype(BF16), kt.astype(BF16)))
            ph2.append((q, k, cum, vb, o_state, inter))
        for (d, h), (rs, hs, _, _), (q, k, cum, vb, o_state, inter) in zip(chains, ph1, ph2):
            blocks = []
            for a in range(nsub):
                sa = slice(a * SUB, (a + 1) * SUB)
                qa, ka, ca, sc = q[sa], k[sa], cum[sa], inter[a]
                for t in range(SUB):
                    dec = jnp.exp(ca - ca[t:t + 1, :])
                    st = jnp.sum(qa * ka[t:t + 1, :] * dec, axis=-1, keepdims=True)
                    ok = (srow <= t) if d else (srow >= t)
                    sc = jnp.where((scol == a * SUB + t) & ok, st, sc)
                blocks.append(sc)
            scores = jnp.concatenate(blocks, axis=0)
            srcs[d][3][rs, hs] = o_state + _dot(scores.astype(BF16), vb)
        return carry

    lax.fori_loop(0, g, body, 0)


def _gla_scan(z, lbs, nb, seq, ctx_len):
    m = z.shape[0]
    gw = GROUP_WIDTH
    g = ctx_len // CHUNK
    nblk = seq // ctx_len
    cblk0 = nb * nblk
    blk_f = lambda b, t: jnp.where(t == 0, cblk0 + b, b * nblk + t - 1)
    blk_b = lambda b, t: jnp.where(t == 0, cblk0 + b, b * nblk + nblk - t)
    col = lambda cb: cb * LANE // gw
    spec = lambda blk, cb: pl.BlockSpec((ctx_len, gw), lambda b, t: (blk(b, t), col(cb)))
    return pl.pallas_call(
        functools.partial(_gla_kernel, g),
        grid=(nb, nblk + 1),
        in_specs=[spec(blk_f, CB_HQ), spec(blk_f, CB_HFF), spec(blk_f, CB_HI),
                  spec(blk_b, CB_HQ), spec(blk_b, CB_HFB), spec(blk_b, CB_HI),
                  pl.BlockSpec((1, gw), lambda b, t: (0, 0))],
        out_specs=[pl.BlockSpec((ctx_len, gw), lambda b, t: (blk_f(b, t), 0)),
                   pl.BlockSpec((ctx_len, gw), lambda b, t: (blk_b(b, t), 0))],
        out_shape=[jax.ShapeDtypeStruct((m, gw), F32), jax.ShapeDtypeStruct((m, gw), F32)],
        scratch_shapes=[pltpu.VMEM((2 * N_HEADS, HEAD_DIM, HEAD_DIM), F32)],
        compiler_params=_cp("parallel", "arbitrary"),
        name="hgrn_scan",
    )(z, z, z, z, z, z, lbs.reshape(1, gw))


def _softmax_pv(parts):
    m = parts[0][0].max(axis=-1, keepdims=True)
    for s, _ in parts[1:]:
        m = jnp.maximum(m, s.max(axis=-1, keepdims=True))
    den, acc = None, None
    for s, v in parts:
        p = jnp.exp(s - m)
        d = jnp.sum(p, axis=-1, keepdims=True)
        a = _dot(p.astype(BF16), v)
        den = d if den is None else den + d
        acc = a if acc is None else acc + a
    return acc / den


def _na_kernel(emit_ctx, rows, *refs):
    q_ref, k_ref, v_ref, qc_ref, kc_ref, vc_ref, bias_ref = refs[:7]
    if emit_ctx:
        yl_ref, yc_ref, kb_ref, vb_ref = refs[7:]
    else:
        yl_ref, kb_ref, vb_ref = refs[7:]
    scale = HEAD_DIM ** -0.5
    win = NA_ROWS * GRID_W
    kb_ref[...] = k_ref[...].astype(BF16)
    vb_ref[...] = v_ref[...].astype(BF16)
    kc = kc_ref[...].astype(BF16)
    vc = vc_ref[...].astype(BF16)

    def body(i, carry):
        pre = []
        for t in range(NA_UNROLL):
            r = i * NA_UNROLL + t
            row0 = jnp.clip(r - NA_ROWS // 2, 0, rows - NA_ROWS)
            qs = pl.ds(pl.multiple_of(r * GRID_W, GRID_W), GRID_W)
            ks = pl.ds(pl.multiple_of(row0 * GRID_W, GRID_W), win)
            q = q_ref[qs, :].astype(BF16)
            pre.append((qs, ks, _dot_nt(q, kb_ref[ks, :]) * scale + bias_ref[0, r - row0], _dot_nt(q, kc) * scale))
        mid = []
        for qs, ks, s_win, s_ctx in pre:
            m = jnp.maximum(s_win.max(axis=-1, keepdims=True), s_ctx.max(axis=-1, keepdims=True))
            p_win, p_ctx = jnp.exp(s_win - m), jnp.exp(s_ctx - m)
            den = jnp.sum(p_win, axis=-1, keepdims=True) + jnp.sum(p_ctx, axis=-1, keepdims=True)
            mid.append((qs, ks, p_win.astype(BF16), p_ctx.astype(BF16), den))
        for qs, ks, p_win, p_ctx, den in mid:
            yl_ref[qs, :] = ((_dot(p_win, vb_ref[ks, :]) + _dot(p_ctx, vc)) / den).astype(yl_ref.dtype)
        return carry

    lax.fori_loop(0, rows // NA_UNROLL, body, 0)
    if emit_ctx:
        s = _dot_nt(qc_ref[...].astype(BF16), kc) * scale
        yc_ref[...] = _softmax_pv([(s, vc)]).astype(yc_ref.dtype)


def _na_bias_kernel(rpb_ref, o_ref):
    n = lax.broadcasted_iota(jnp.int32, (LANE, GRID_W * GRID_W), 1)
    j = lax.broadcasted_iota(jnp.int32, (LANE, GRID_W * GRID_W), 0)
    q, w = n >> 6, n & (GRID_W - 1)
    dc = jnp.clip(w - q, 1 - NA_COLS, NA_COLS - 1) + NA_COLS - 1
    onehot = jnp.where(dc == j, 1.0, 0.0).astype(F32)
    m = _dot(rpb_ref[...], onehot, precision=HIGHEST)
    c0 = jnp.clip(q[0:1] - NA_COLS // 2, 0, GRID_W - NA_COLS)
    ok = (w[0:1] >= c0) & (w[0:1] < c0 + NA_COLS)
    o_ref[...] = jnp.where(ok, m, NEG)


def _na_bias_tables(rpb):
    depth, H, nr, nc = rpb.shape
    assert GRID_W == 64 and depth * H * nr <= LANE and nc <= LANE
    flat = jnp.zeros((LANE, LANE), F32).at[:depth * H * nr, :nc].set(rpb.reshape(-1, nc).astype(F32))
    m = pl.pallas_call(
        _na_bias_kernel,
        out_shape=jax.ShapeDtypeStruct((LANE, GRID_W * GRID_W), F32),
        compiler_params=pltpu.CompilerParams(vmem_limit_bytes=VMEM_LIMIT),
        name="na_bias",
    )(flat)
    m = m[:depth * H * nr].reshape(depth, H, nr, GRID_W, GRID_W)
    tab = jnp.stack([jnp.stack([m[:, :, k - s + NA_ROWS - 1] for k in range(NA_ROWS)], axis=3)
                     for s in range(NA_ROWS)], axis=2)
    return tab.reshape(depth, H, NA_ROWS, GRID_W, NA_ROWS * GRID_W)


def _na(z, bias, nb, seq, ctx_len, emit_ctx):
    rows = seq // GRID_W
    cblk0 = nb * seq // ctx_len
    H = N_HEADS
    lat = lambda cb: pl.BlockSpec((seq, LANE), lambda b, h: (b, cb + h))
    ctx = lambda cb: pl.BlockSpec((ctx_len, LANE), lambda b, h: (cblk0 + b, cb + h))
    win = NA_ROWS * GRID_W
    in_specs = [lat(CB_NAQ), lat(CB_NAK), lat(CB_NAV), ctx(CB_NAQ), ctx(CB_NAK), ctx(CB_NAV),
                pl.BlockSpec((1, NA_ROWS, GRID_W, win), lambda b, h: (h, 0, 0, 0))]
    out_specs = [pl.BlockSpec((seq, LANE), lambda b, h: (b, h))]
    out_shape = [jax.ShapeDtypeStruct((nb * seq, GROUP_WIDTH), BF16)]
    if emit_ctx:
        out_specs.append(pl.BlockSpec((ctx_len, LANE), lambda b, h: (b, h)))
        out_shape.append(jax.ShapeDtypeStruct((nb * ctx_len, GROUP_WIDTH), BF16))
    return pl.pallas_call(
        functools.partial(_na_kernel, emit_ctx, rows),
        grid=(nb, H),
        in_specs=in_specs, out_specs=out_specs, out_shape=out_shape,
        scratch_shapes=[pltpu.VMEM((seq, LANE), BF16), pltpu.VMEM((seq, LANE), BF16)],
        compiler_params=_cp("parallel", "parallel"),
        name="na_attn",
    )(z, z, z, z, z, z, bias)


def _rms(x, w):
    return x * lax.rsqrt(jnp.mean(x * x, axis=-1, keepdims=True) + RMS_EPS) * w


def _mla_prep_kernel(cq_ref, ckv_ref, kra_ref, krb_ref, cc_ref, ss_ref, qnw_ref, kvnw_ref, wuq_ref, wukv_ref,
                     q_ref, kn_ref, kr_ref, v_ref):
    cc, ss = cc_ref[...], ss_ref[...]
    qn = _rms(cq_ref[...], qnw_ref[...]).astype(BF16)
    qa = _dot(qn, wuq_ref[...])
    for h in range(N_HEADS):
        b = 3 * LANE * h
        q_ref[:, 2 * LANE * h:2 * LANE * h + LANE] = qa[:, b:b + LANE].astype(BF16)
        q_ref[:, 2 * LANE * h + LANE:2 * LANE * (h + 1)] = (
            qa[:, b + LANE:b + 2 * LANE] * cc + qa[:, b + 2 * LANE:b + 3 * LANE] * ss).astype(BF16)
    kvn = _rms(ckv_ref[...], kvnw_ref[...]).astype(BF16)
    kv = _dot(kvn, wukv_ref[...])
    kn_ref[...] = kv[:, :GROUP_WIDTH].astype(BF16)
    v_ref[...] = kv[:, GROUP_WIDTH:].astype(BF16)
    kr_ref[...] = (kra_ref[...] * cc + krb_ref[...] * ss).astype(BF16)


def _mla_prep(z, cc, ss, qnw, kvnw, wuq, wukv, n_lat, seq, tm=512):
    m = z.shape[0]
    nlt, spt = n_lat // tm, seq // tm
    tab = lambda i: jnp.where(i < nlt, i % spt, spt)
    H = N_HEADS
    return pl.pallas_call(
        _mla_prep_kernel,
        grid=(m // tm,),
        in_specs=[pl.BlockSpec((tm, MLA_Q_RANK), lambda i: (i, CB_MQ * LANE // MLA_Q_RANK)),
                  pl.BlockSpec((tm, MLA_KV_RANK), lambda i: (i, CB_MKV * LANE // MLA_KV_RANK)),
                  pl.BlockSpec((tm, LANE), lambda i: (i, CB_MKRA)),
                  pl.BlockSpec((tm, LANE), lambda i: (i, CB_MKRB)),
                  pl.BlockSpec((tm, LANE), lambda i: (tab(i), 0)),
                  pl.BlockSpec((tm, LANE), lambda i: (tab(i), 0)),
                  pl.BlockSpec((1, MLA_Q_RANK), lambda i: (0, 0)),
                  pl.BlockSpec((1, MLA_KV_RANK), lambda i: (0, 0)),
                  pl.BlockSpec(wuq.shape, lambda i: (0, 0)),
                  pl.BlockSpec(wukv.shape, lambda i: (0, 0))],
        out_specs=[pl.BlockSpec((tm, 2 * LANE * H), lambda i: (i, 0)),
                   pl.BlockSpec((tm, GROUP_WIDTH), lambda i: (i, 0)),
                   pl.BlockSpec((tm, LANE), lambda i: (i, 0)),
                   pl.BlockSpec((tm, GROUP_WIDTH), lambda i: (i, 0))],
        out_shape=[jax.ShapeDtypeStruct((m, 2 * LANE * H), BF16),
                   jax.ShapeDtypeStruct((m, GROUP_WIDTH), BF16),
                   jax.ShapeDtypeStruct((m, LANE), BF16),
                   jax.ShapeDtypeStruct((m, GROUP_WIDTH), BF16)],
        compiler_params=_cp("parallel"),
        name="mla_prep",
    )(z, z, z, z, cc, ss, qnw.reshape(1, -1), kvnw.reshape(1, -1), wuq, wukv)


def _mla_attn_kernel(with_lat, seq, *refs):
    if with_lat:
        q_ref, knl_ref, krl_ref, vl_ref, knc_ref, krc_ref, vc_ref, y_ref, k_scr = refs
    else:
        q_ref, knc_ref, krc_ref, vc_ref, y_ref, k_scr = refs
    scale = MLA_QK_DIM ** -0.5
    nk = k_scr.shape[0]

    @pl.when(pl.program_id(2) == 0)
    def _():
        if with_lat:
            k_scr[0:seq, 0:LANE] = knl_ref[...]
            k_scr[0:seq, LANE:2 * LANE] = krl_ref[...]
        k_scr[nk - knc_ref.shape[0]:nk, 0:LANE] = knc_ref[...]
        k_scr[nk - knc_ref.shape[0]:nk, LANE:2 * LANE] = krc_ref[...]

    tq = q_ref.shape[0]
    qsub = min(tq, MLA_QSUB)
    nsub = tq // qsub
    vals = ([vl_ref] if with_lat else []) + [vc_ref]

    def qk(s):
        q = q_ref[s * qsub:(s + 1) * qsub, :]
        out = [_dot_nt(q, k_scr[0:seq, :])] if with_lat else []
        return out + [_dot_nt(q, k_scr[nk - knc_ref.shape[0]:nk, :])]

    def softmax(raw):
        m = raw[0].max(axis=-1, keepdims=True)
        for s in raw[1:]:
            m = jnp.maximum(m, s.max(axis=-1, keepdims=True))
        ps = [jnp.exp((s - m) * scale) for s in raw]
        den = sum(jnp.sum(p, axis=-1, keepdims=True) for p in ps)
        return [p.astype(BF16) for p in ps], den

    raw = qk(0)
    for s in range(nsub):
        nxt = qk(s + 1) if s + 1 < nsub else None
        ps, den = softmax(raw)
        acc = sum(_dot(p, v[...]) for p, v in zip(ps, vals))
        y_ref[s * qsub:(s + 1) * qsub, :] = (acc / den).astype(y_ref.dtype)
        raw = nxt


def _mla_attn(q, kn, kr, v, nb, seq, ctx_len, with_lat, tq=512):
    H = N_HEADS
    cblk0 = nb * seq // ctx_len
    nq = seq if with_lat else ctx_len
    tq = min(tq, nq)
    qblk0 = 0 if with_lat else nb * seq // tq
    ctxs = [pl.BlockSpec((ctx_len, LANE), lambda b, h, i: (cblk0 + b, h)),
            pl.BlockSpec((ctx_len, LANE), lambda b, h, i: (cblk0 + b, 0)),
            pl.BlockSpec((ctx_len, LANE), lambda b, h, i: (cblk0 + b, h))]
    lats = [pl.BlockSpec((seq, LANE), lambda b, h, i: (b, h)),
            pl.BlockSpec((seq, LANE), lambda b, h, i: (b, 0)),
            pl.BlockSpec((seq, LANE), lambda b, h, i: (b, h))]
    in_specs = [pl.BlockSpec((tq, 2 * LANE), lambda b, h, i: (qblk0 + b * (nq // tq) + i, h))]
    args = [q]
    if with_lat:
        in_specs += lats
        args += [kn, kr, v]
    in_specs += ctxs
    args += [kn, kr, v]
    nk = (seq if with_lat else 0) + ctx_len
    return pl.pallas_call(
        functools.partial(_mla_attn_kernel, with_lat, seq),
        grid=(nb, H, nq // tq),
        in_specs=in_specs,
        out_specs=pl.BlockSpec((tq, LANE), lambda b, h, i: (b * (nq // tq) + i, h)),
        out_shape=jax.ShapeDtypeStruct((nb * nq, GROUP_WIDTH), BF16),
        scratch_shapes=[pltpu.VMEM((nk, 2 * LANE), BF16)],
        compiler_params=_cp("parallel", "parallel", "arbitrary"),
        name="mla_attn_lat" if with_lat else "mla_attn_ctx",
    )(*args)


def _prep_w_in(w_in):
    d = w_in.shape[0]
    gw = GROUP_WIDTH
    o_na = 4 * gw + 4 * N_HEADS
    o_mla = o_na + 3 * gw
    o_kr = o_mla + MLA_Q_RANK + MLA_KV_RANK
    o_hg = o_kr + MLA_ROPE
    kr = w_in[:, o_kr:o_hg]
    k1, k2 = kr[:, 0::2], kr[:, 1::2]
    z = lambda n: jnp.zeros((d, n), w_in.dtype)
    o_kv = o_mla + MLA_Q_RANK
    cols = [w_in[:, o_mla:o_kv], k1, k2, z(LANE - MLA_ROPE),
            w_in[:, o_kv:o_kr], k2, k1, z(LANE - MLA_ROPE),
            w_in[:, 4 * gw:o_na], z(LANE - 4 * N_HEADS),
            w_in[:, :4 * gw], w_in[:, o_na:o_mla], w_in[:, o_hg:]]
    w = jnp.concatenate(cols, axis=1).astype(BF16)
    assert w.shape[1] == NP_IN
    return w


def _prep_w_uq(w_uq):
    r = w_uq.shape[0]
    z = jnp.zeros((r, LANE - MLA_ROPE), w_uq.dtype)
    cols = []
    for h in range(N_HEADS):
        wh = w_uq[:, h * MLA_QK_DIM:(h + 1) * MLA_QK_DIM]
        rope = wh[:, MLA_NOPE:]
        r1, r2 = rope[:, 0::2], rope[:, 1::2]
        cols += [wh[:, :MLA_NOPE], r1, r2, z, r2, r1, z]
    return jnp.concatenate(cols, axis=1).astype(BF16)


def _rope_tables(seq, tm):
    n_freq = MLA_ROPE // 4
    freqs = ROPE_BASE ** (-jnp.arange(n_freq, dtype=F32) / n_freq)
    t = jnp.arange(seq)
    ang = jnp.concatenate([(t // GRID_W).astype(F32)[:, None] * freqs,
                           (t % GRID_W).astype(F32)[:, None] * freqs], -1)
    cos, sin = jnp.cos(ang), jnp.sin(ang)
    zp = jnp.zeros((seq, LANE - MLA_ROPE), F32)
    cc = jnp.concatenate([cos, cos, zp], axis=1)
    ss = jnp.concatenate([-sin, sin, zp], axis=1)
    ident = jnp.zeros((tm, LANE), F32).at[:, :MLA_ROPE].set(1.0)
    return jnp.concatenate([cc, ident], axis=0), jnp.concatenate([ss, jnp.zeros((tm, LANE), F32)], axis=0)


def kernel(x, c, ctx, c_ctx, w_ada, b_ada, w_in, gdn_conv_w, gdn_a_log, gdn_dt_bias, gdn_norm_w, na_rpb,
           mla_q_norm_w, mla_kv_norm_w, mla_w_uq, mla_w_uk, mla_w_uv, hgrn_lower_bounds, hgrn_norm_w, w_out,
           ln1_w, ln1_b, w_mlp1, w_mlp2, ln2_w, ln2_b):
    nb, seq, d = x.shape
    ctx_len = ctx.shape[1]
    depth = w_ada.shape[0]
    n_lat, n_ctx = nb * seq, nb * ctx_len
    alpha = (2 * depth) ** 0.25
    tm = 512
    tmm = 1024 if (seq % 1024 == 0 and n_ctx % 1024 == 0) else tm
    assert nb < 8 and seq % tm == 0 and n_ctx % tm == 0 and seq % ctx_len == 0 and ctx_len % CHUNK == 0

    cin = jnp.zeros((8, d), F32).at[:nb].set(c).at[nb].set(c_ctx)
    ada = _ada(cin, w_ada, b_ada)
    p_lb = jax.nn.softmax(hgrn_lower_bounds.astype(F32), axis=0)
    lbs = jnp.cumsum(p_lb, axis=0) - p_lb[0]
    cc, ss = _rope_tables(seq, tm)
    na_bias = _na_bias_tables(na_rpb)

    x_all = jnp.concatenate([x.reshape(n_lat, d), ctx.reshape(n_ctx, d)], axis=0)
    for l in range(depth):
        emit_ctx = l < depth - 1
        ada_r = ada[l].reshape(8 * 6, 1, d)
        z = _inproj(x_all, ada_r, _prep_w_in(w_in[l]), n_lat, seq, nb, tm=tmm)

        qkv = _gdn_conv(z, gdn_conv_w[l], nb, seq, ctx_len)
        p = _gdn_gates(z, gdn_a_log[l], gdn_dt_bias[l], tm=tm)
        ct = p[:, :2 * N_HEADS].reshape(-1, CHUNK, 2 * N_HEADS).transpose(0, 2, 1)
        o_f, o_b = _gdn_scan(*_gdn_prep(qkv, p, ct, nb, seq, ctx_len), nb, seq, ctx_len)
        m_out = n_lat + n_ctx if emit_ctx else n_lat
        ya = _combine(o_f, o_b, z, CB_GGATE, gdn_norm_w[l], m_out, tm=tm)
        yb = _na(z, na_bias[l], nb, seq, ctx_len, emit_ctx)
        q, kn, kr, v = _mla_prep(z, cc, ss, mla_q_norm_w[l], mla_kv_norm_w[l], _prep_w_uq(mla_w_uq[l]),
                                 jnp.concatenate([mla_w_uk[l], mla_w_uv[l]], axis=1).astype(BF16),
                                 n_lat, seq, tm=tm)
        ym = [_mla_attn(q, kn, kr, v, nb, seq, ctx_len, True)]
        if emit_ctx:
            ym.append(_mla_attn(q, kn, kr, v, nb, seq, ctx_len, False))
        yh = _combine(*_gla_scan(z, lbs[l], nb, seq, ctx_len), z, CB_HG, hgrn_norm_w[l], m_out, tm=tm)

        if emit_ctx:
            ys = [ya] + [jnp.concatenate(list(t), axis=0) for t in (yb, ym)] + [yh]
        else:
            ys = [ya, yb[0], ym[0], yh]
        x_all = _outproj(x_all, ys, w_out[l].astype(BF16), ada_r, ln1_w[l], ln1_b[l], m_out, seq, nb, alpha, tm=tm)
        x_all = _mlp(x_all, w_mlp1[l].astype(BF16), w_mlp2[l].astype(BF16), ada_r, ln2_w[l], ln2_b[l],
                     seq, nb, alpha, tm=tm)
    return x_all[:n_lat].reshape(nb, seq, d)
```

```python
import functools

import numpy as np
import jax
import jax.numpy as jnp
from jax import lax
from jax.experimental import pallas as pl
from jax.experimental.pallas import tpu as pltpu

F32 = jnp.float32
BF16 = jnp.bfloat16
HIGHEST = lax.Precision.HIGHEST

GRID_W = 64
N_HEADS = 4
HEAD_DIM = 128
GROUP_WIDTH = 512
CHUNK = 64
SUB = 8
GDN_CONV = 5
NA_ROWS = 8
NA_COLS = 16
NA_UNROLL = 4
MLA_Q_RANK = 384
MLA_KV_RANK = 256
MLA_NOPE = 128
MLA_ROPE = 64
MLA_QK_DIM = MLA_NOPE + MLA_ROPE
MLA_QSUB = 512
ROPE_BASE = 10000.0
LN_EPS = 1e-5
RMS_EPS = 1e-6
NEG = -1e30

LANE = 128
CB_MQ, CB_MKRA, CB_MKV, CB_MKRB, CB_GAB = 0, 3, 4, 6, 7
CB_GQKV, CB_GGATE = 8, 20
CB_NAQ, CB_NAK, CB_NAV = 24, 28, 32
CB_HQ, CB_HFF, CB_HFB, CB_HI, CB_HG = 36, 40, 44, 48, 52
CHUNK_SHIFT = 6
PREP_CHUNKS = 2
NP_IN = 56 * LANE

VMEM_LIMIT = 48 << 20


def _cp(*sem):
    return pltpu.CompilerParams(dimension_semantics=sem, vmem_limit_bytes=VMEM_LIMIT)


def _silu(x):
    return x * jax.nn.sigmoid(x)


def _dot(a, b, **kw):
    return jnp.dot(a, b, preferred_element_type=F32, **kw)


def _dot_nt(a, b, **kw):
    return lax.dot_general(a, b, (((1,), (1,)), ((), ())), preferred_element_type=F32, **kw)


def _dot_tn(a, b, **kw):
    return lax.dot_general(a, b, (((0,), (0,)), ((), ())), preferred_element_type=F32, **kw)


def _ada_kernel(c_ref, w_ref, b_ref, o_ref):
    s = _silu(c_ref[...])
    o_ref[0] = _dot(s, w_ref[0], precision=HIGHEST) + b_ref[0]


def _ada(cin, w_ada, b_ada):
    depth, d, n = w_ada.shape
    tn = 512
    return pl.pallas_call(
        _ada_kernel,
        grid=(depth, n // tn),
        in_specs=[pl.BlockSpec((8, d), lambda l, j: (0, 0)),
                  pl.BlockSpec((1, d, tn), lambda l, j: (l, 0, j)),
                  pl.BlockSpec((1, 1, tn), lambda l, j: (l, 0, j))],
        out_specs=pl.BlockSpec((1, 8, tn), lambda l, j: (l, 0, j)),
        out_shape=jax.ShapeDtypeStruct((depth, 8, n), F32),
        compiler_params=_cp("parallel", "parallel"),
        name="ada",
    )(cin, w_ada, b_ada.reshape(depth, 1, n))


ROW_STEP = 256


def _modulate(x_ref, sh_ref, sc_ref, xm_ref):
    sc1, sh = 1.0 + sc_ref[0], sh_ref[0]

    def body(t, carry):
        sl = pl.ds(pl.multiple_of(t * ROW_STEP, ROW_STEP), ROW_STEP)
        xm_ref[sl, :] = (x_ref[sl, :] * sc1 + sh).astype(BF16)
        return carry

    lax.fori_loop(0, x_ref.shape[0] // ROW_STEP, body, 0)


def _inproj_kernel(x_ref, sh_ref, sc_ref, w_ref, o_ref, xm_ref):
    @pl.when(pl.program_id(1) == 0)
    def _():
        _modulate(x_ref, sh_ref, sc_ref, xm_ref)
    o_ref[...] = _dot(xm_ref[...], w_ref[...])


def _inproj(x_all, ada_r, w, n_lat, seq, nb, tm=512, tn=1024):
    m, d = x_all.shape
    n = w.shape[1]
    row = lambda i: jnp.minimum((i * tm) // seq, nb)
    return pl.pallas_call(
        _inproj_kernel,
        grid=(m // tm, n // tn),
        in_specs=[pl.BlockSpec((tm, d), lambda i, j: (i, 0)),
                  pl.BlockSpec((1, 1, d), lambda i, j: (row(i) * 6 + 0, 0, 0)),
                  pl.BlockSpec((1, 1, d), lambda i, j: (row(i) * 6 + 1, 0, 0)),
                  pl.BlockSpec((d, tn), lambda i, j: (0, j))],
        out_specs=pl.BlockSpec((tm, tn), lambda i, j: (i, j)),
        out_shape=jax.ShapeDtypeStruct((m, n), F32),
        scratch_shapes=[pltpu.VMEM((tm, d), BF16)],
        compiler_params=_cp("parallel", "arbitrary"),
        name="inproj",
    )(x_all, ada_r, ada_r, w)


def _layernorm(r, w, b):
    mu = jnp.mean(r, axis=-1, keepdims=True)
    rc = r - mu
    var = jnp.mean(rc * rc, axis=-1, keepdims=True)
    return rc * lax.rsqrt(var + LN_EPS) * w + b


def _outproj_kernel(alpha, x_ref, ya_ref, yb_ref, ym_ref, yh_ref, w_ref, g_ref, lw_ref, lb_ref, o_ref):
    gw = GROUP_WIDTH
    acc = _dot(ya_ref[...], w_ref[0:gw, :])
    acc += _dot(yb_ref[...], w_ref[gw:2 * gw, :])
    acc += _dot(ym_ref[...], w_ref[2 * gw:3 * gw, :])
    acc += _dot(yh_ref[...], w_ref[3 * gw:4 * gw, :])
    r = alpha * x_ref[...] + g_ref[0] * acc
    o_ref[...] = _layernorm(r, lw_ref[...], lb_ref[...])


def _outproj(x_all, ys, w, ada_r, lw, lb, m_out, seq, nb, alpha, tm=512):
    d = x_all.shape[1]
    row = lambda i: jnp.minimum((i * tm) // seq, nb)
    yspec = pl.BlockSpec((tm, GROUP_WIDTH), lambda i: (i, 0))
    return pl.pallas_call(
        functools.partial(_outproj_kernel, alpha),
        grid=(m_out // tm,),
        in_specs=[pl.BlockSpec((tm, d), lambda i: (i, 0)), yspec, yspec, yspec, yspec,
                  pl.BlockSpec((d, d), lambda i: (0, 0)),
                  pl.BlockSpec((1, 1, d), lambda i: (row(i) * 6 + 2, 0, 0)),
                  pl.BlockSpec((1, d), lambda i: (0, 0)),
                  pl.BlockSpec((1, d), lambda i: (0, 0))],
        out_specs=pl.BlockSpec((tm, d), lambda i: (i, 0)),
        out_shape=jax.ShapeDtypeStruct((m_out, d), F32),
        compiler_params=_cp("parallel"),
        name="outproj_ln",
    )(x_all, *ys, w, ada_r, lw.reshape(1, d), lb.reshape(1, d))


def _mlp_kernel(alpha, x_ref, sh_ref, sc_ref, g_ref, w1_ref, w2_ref, lw_ref, lb_ref, o_ref, xm_ref, acc_ref):
    k = pl.program_id(1)

    @pl.when(k == 0)
    def _():
        _modulate(x_ref, sh_ref, sc_ref, xm_ref)
        acc_ref[...] = jnp.zeros_like(acc_ref)

    h = jnp.maximum(_dot(xm_ref[...], w1_ref[...]), 0.0)
    acc_ref[...] += _dot((h * h).astype(BF16), w2_ref[...])

    @pl.when(k == pl.num_programs(1) - 1)
    def _():
        g, lw, lb = g_ref[0], lw_ref[...], lb_ref[...]

        def body(t, carry):
            sl = pl.ds(pl.multiple_of(t * ROW_STEP, ROW_STEP), ROW_STEP)
            o_ref[sl, :] = _layernorm(alpha * x_ref[sl, :] + g * acc_ref[sl, :], lw, lb)
            return carry

        lax.fori_loop(0, x_ref.shape[0] // ROW_STEP, body, 0)


def _mlp(x_all, w1, w2, ada_r, lw, lb, seq, nb, alpha, tm=512, th=512):
    m, d = x_all.shape
    hid = w1.shape[1]
    row = lambda i: jnp.minimum((i * tm) // seq, nb)
    return pl.pallas_call(
        functools.partial(_mlp_kernel, alpha),
        grid=(m // tm, hid // th),
        in_specs=[pl.BlockSpec((tm, d), lambda i, k: (i, 0)),
                  pl.BlockSpec((1, 1, d), lambda i, k: (row(i) * 6 + 3, 0, 0)),
                  pl.BlockSpec((1, 1, d), lambda i, k: (row(i) * 6 + 4, 0, 0)),
                  pl.BlockSpec((1, 1, d), lambda i, k: (row(i) * 6 + 5, 0, 0)),
                  pl.BlockSpec((d, th), lambda i, k: (0, k)),
                  pl.BlockSpec((th, d), lambda i, k: (k, 0)),
                  pl.BlockSpec((1, d), lambda i, k: (0, 0)),
                  pl.BlockSpec((1, d), lambda i, k: (0, 0))],
        out_specs=pl.BlockSpec((tm, d), lambda i, k: (i, 0)),
        out_shape=jax.ShapeDtypeStruct((m, d), F32),
        scratch_shapes=[pltpu.VMEM((tm, d), BF16), pltpu.VMEM((tm, d), F32)],
        compiler_params=_cp("parallel", "arbitrary"),
        name="mlp_ln",
    )(x_all, ada_r, ada_r, ada_r, w1, w2, lw.reshape(1, d), lb.reshape(1, d))


def _gdn_conv_kernel(seq, ctx_len, xl_ref, xc_ref, w_ref, o_ref, pad_ref):
    j = pl.program_id(1)
    w = w_ref[...]
    qscale = jnp.where(j < N_HEADS, HEAD_DIM ** -0.5, 1.0).astype(F32)
    p0 = 8 - GDN_CONV // 2
    for x_ref, nrows, o0 in ((xl_ref, seq, 0), (xc_ref, ctx_len, seq)):
        pad_ref[0:8, :] = jnp.zeros((8, LANE), F32)
        pad_ref[nrows + 8:nrows + 16, :] = jnp.zeros((8, LANE), F32)
        pad_ref[8:nrows + 8, :] = x_ref[...]
        rb = min(nrows, 256)
        for r0 in range(0, nrows, rb):
            y = pad_ref[r0 + p0:r0 + p0 + rb, :] * w[0:1, :]
            for i in range(1, GDN_CONV):
                y = y + pad_ref[r0 + p0 + i:r0 + p0 + i + rb, :] * w[i:i + 1, :]
            y = _silu(y)
            nrm = y * lax.rsqrt(jnp.sum(y * y, axis=-1, keepdims=True) + RMS_EPS) * qscale
            o_ref[o0 + r0:o0 + r0 + rb, :] = jnp.where(j < 2 * N_HEADS, nrm, y)


def _gdn_conv(z, conv_w, nb, seq, ctx_len):
    nblk = 3 * N_HEADS
    cblk0 = nb * seq // ctx_len
    return pl.pallas_call(
        functools.partial(_gdn_conv_kernel, seq, ctx_len),
        grid=(nb, nblk),
        in_specs=[pl.BlockSpec((seq, LANE), lambda b, j: (b, CB_GQKV + j)),
                  pl.BlockSpec((ctx_len, LANE), lambda b, j: (cblk0 + b, CB_GQKV + j)),
                  pl.BlockSpec((GDN_CONV, LANE), lambda b, j: (0, j))],
        out_specs=pl.BlockSpec((seq + ctx_len, LANE), lambda b, j: (b, j)),
        out_shape=jax.ShapeDtypeStruct((nb * (seq + ctx_len), nblk * LANE), F32),
        scratch_shapes=[pltpu.VMEM((seq + 16, LANE), F32)],
        compiler_params=_cp("parallel", "parallel"),
        name="gdn_conv",
    )(z, z, conv_w)


def _gdn_gates_kernel(tm, s_ref, alog_ref, dtb_ref, o_ref):
    s = s_ref[...]
    g = -jnp.exp(alog_ref[...]) * (jnp.maximum(s + dtb_ref[...], 0.0)
                                    + jnp.log1p(jnp.exp(-jnp.abs(s + dtb_ref[...]))))
    r = lax.broadcasted_iota(jnp.int32, (tm, tm), 0)
    c = lax.broadcasted_iota(jnp.int32, (tm, tm), 1)
    same = (r >> CHUNK_SHIFT) == (c >> CHUNK_SHIFT)
    lo = jnp.where(same & (c <= r), 1.0, 0.0).astype(F32)
    up = jnp.where(same & (c >= r), 1.0, 0.0).astype(F32)
    cum_f = _dot(lo, g, precision=HIGHEST)
    cum_b = _dot(up, g, precision=HIGHEST)
    col = lax.broadcasted_iota(jnp.int32, s.shape, 1)
    o_ref[...] = jnp.where(col < N_HEADS, cum_f,
                           jnp.where(col < 2 * N_HEADS, cum_b,
                                     jnp.where(col < 4 * N_HEADS, jax.nn.sigmoid(s), 0.0)))


def _gdn_gates(z, a_log, dt_bias, tm=512):
    m = z.shape[0]
    pad = lambda v: jnp.zeros((1, LANE), F32).at[0, :2 * N_HEADS].set(v.reshape(-1).astype(F32))
    return pl.pallas_call(
        functools.partial(_gdn_gates_kernel, tm),
        grid=(m // tm,),
        in_specs=[pl.BlockSpec((tm, LANE), lambda i: (i, CB_GAB)),
                  pl.BlockSpec((1, LANE), lambda i: (0, 0)),
                  pl.BlockSpec((1, LANE), lambda i: (0, 0))],
        out_specs=pl.BlockSpec((tm, LANE), lambda i: (i, 0)),
        out_shape=jax.ShapeDtypeStruct((m, LANE), F32),
        compiler_params=_cp("parallel"),
        name="gdn_gates",
    )(z, pad(a_log), pad(dt_bias))


def _tri_masks(n):
    r = lax.broadcasted_iota(jnp.int32, (n, n), 0)
    c = lax.broadcasted_iota(jnp.int32, (n, n), 1)
    return r, c


def _split(x):
    hi = x.astype(BF16)
    return hi, (x - hi.astype(F32)).astype(BF16)


def _dot3(a, b):
    return _dot(a[0], b[0]) + (_dot(a[0], b[1]) + _dot(a[1], b[0]))


def _gdn_prep_kernel(qkv_ref, p_ref, ct_ref, u_ref, w_ref, qt_ref, kt_ref, att_ref, el_ref):
    C, H = CHUNK, N_HEADS
    lane = lax.broadcasted_iota(jnp.int32, (C, LANE), 1)
    sub8 = lax.broadcasted_iota(jnp.int32, (2 * H, C), 0)
    r, c = _tri_masks(C)
    eye = jnp.where(r == c, 1.0, 0.0).astype(F32)
    a_list, rhs_list, where_list = [], [], []
    for n in range(PREP_CHUNKS):
        rs = slice(n * C, (n + 1) * C)
        pblk = p_ref[rs, :]
        tblk = ct_ref[n]
        col = lambda idx, pblk=pblk: jnp.sum(jnp.where(lane == idx, pblk, 0.0), axis=-1, keepdims=True)
        row = lambda idx, tblk=tblk: jnp.sum(jnp.where(sub8 == idx, tblk, 0.0), axis=0, keepdims=True)
        for h in range(H):
            hs = slice(h * LANE, (h + 1) * LANE)
            q = qkv_ref[rs, h * LANE:(h + 1) * LANE]
            k = qkv_ref[rs, (H + h) * LANE:(H + h + 1) * LANE]
            v = qkv_ref[rs, (2 * H + h) * LANE:(2 * H + h + 1) * LANE]
            qbf, kbf = q.astype(BF16), k.astype(BF16)
            for d in range(2):
                idx = d * H + h
                cum_c, cum_r, beta_c = col(idx), row(idx), col(2 * H + idx)
                incl = (c >= r) if d else (c <= r)
                strict = (c > r) if d else (c < r)
                last = cum_r[:, 0:1] if d else cum_r[:, C - 1:C]
                decay = jnp.exp(jnp.where(incl, cum_c - cum_r, NEG))
                kb = k * beta_c
                ec = jnp.exp(cum_c)
                a_list.append(jnp.where(strict, _dot_nt(kb.astype(BF16), kbf) * decay, 0.0))
                rhs_list.append(_split(jnp.concatenate([v * beta_c, kb * ec], axis=-1)))
                where_list.append((d, rs, hs))
                att_ref[d, h, rs, :] = jnp.where(incl, _dot_nt(qbf, kbf) * decay, 0.0).astype(BF16)
                qt_ref[d, rs, hs] = (q * ec).astype(BF16)
                kt_ref[d, rs, hs] = (k * jnp.exp(last - cum_c)).astype(BF16)
                el_ref[n, idx:idx + 1, :] = jnp.broadcast_to(jnp.exp(last), (1, LANE))
    ts = [eye - a for a in a_list]
    ps = [a.astype(BF16) for a in a_list]
    for _ in range(5):
        ps = [_dot(p, p).astype(BF16) for p in ps]
        ts = [t + _dot(t.astype(BF16), p) for t, p in zip(ts, ps)]
    res = [eye - t - _dot3(_split(a), _split(t)) for a, t in zip(a_list, ts)]
    ts = [t + _dot(t.astype(BF16), e.astype(BF16)) for t, e in zip(ts, res)]
    sols = [_dot3(_split(t), rhs) for t, rhs in zip(ts, rhs_list)]
    for sol, (d, rs, hs) in zip(sols, where_list):
        u_ref[d, rs, hs] = sol[:, :HEAD_DIM]
        w_ref[d, rs, hs] = sol[:, HEAD_DIM:].astype(BF16)


def _gdn_prep(qkv, p, ct, nb, seq, ctx_len):
    m = qkv.shape[0]
    H = N_HEADS
    gw = GROUP_WIDTH
    rows = PREP_CHUNKS * CHUNK
    assert seq % rows == 0 and ctx_len % rows == 0
    dspec = pl.BlockSpec((2, rows, gw), lambda i: (0, i, 0))
    nl, nc = seq // rows, ctx_len // rows

    def qkv_blk(i):
        ic = i - nb * nl
        return jnp.where(i < nb * nl, (i // nl) * (nl + nc) + i % nl, (ic // nc) * (nl + nc) + nl + ic % nc)

    return pl.pallas_call(
        _gdn_prep_kernel,
        grid=(m // rows,),
        in_specs=[pl.BlockSpec((rows, 3 * gw), lambda i: (qkv_blk(i), 0)),
                  pl.BlockSpec((rows, LANE), lambda i: (i, 0)),
                  pl.BlockSpec((PREP_CHUNKS, 2 * H, CHUNK), lambda i: (i, 0, 0))],
        out_specs=[dspec, dspec, dspec, dspec,
                   pl.BlockSpec((2, H, rows, CHUNK), lambda i: (0, 0, i, 0)),
                   pl.BlockSpec((PREP_CHUNKS, 2 * H, LANE), lambda i: (i, 0, 0))],
        out_shape=[jax.ShapeDtypeStruct((2, m, gw), F32),
                   jax.ShapeDtypeStruct((2, m, gw), BF16),
                   jax.ShapeDtypeStruct((2, m, gw), BF16),
                   jax.ShapeDtypeStruct((2, m, gw), BF16),
                   jax.ShapeDtypeStruct((2, H, m, CHUNK), BF16),
                   jax.ShapeDtypeStruct((m // CHUNK, 2 * H, LANE), F32)],
        compiler_params=_cp("parallel"),
        name="gdn_prep",
    )(qkv, p, ct)


def _gdn_scan_kernel(g, *refs):
    ins, (of_ref, ob_ref, s_ref) = refs[:12], refs[12:]
    H = N_HEADS

    @pl.when(pl.program_id(1) == 0)
    def _():
        s_ref[...] = jnp.zeros_like(s_ref)

    chains = [(d, h) for d in range(2) for h in range(H)]
    for j in range(g):
        mid = []
        for d, h in chains:
            u_ref, w_ref, qt_ref = ins[d:6:2]
            cj = g - 1 - j if d else j
            rs, hs = slice(cj * CHUNK, (cj + 1) * CHUNK), slice(h * LANE, (h + 1) * LANE)
            sb = s_ref[d * H + h].astype(BF16)
            vnb = (u_ref[0, rs, hs] - _dot(w_ref[0, rs, hs], sb)).astype(BF16)
            mid.append((vnb, _dot(qt_ref[0, rs, hs], sb)))
        for (d, h), (vnb, o_state) in zip(chains, mid):
            kt_ref, att_ref, el_ref = ins[6 + d::2]
            o_ref = ob_ref if d else of_ref
            cj = g - 1 - j if d else j
            rs, hs = slice(cj * CHUNK, (cj + 1) * CHUNK), slice(h * LANE, (h + 1) * LANE)
            idx = d * H + h
            s_ref[idx] = s_ref[idx] * el_ref[cj, idx:idx + 1, :] + _dot_tn(kt_ref[0, rs, hs], vnb)
            o_ref[rs, hs] = o_state + _dot(att_ref[0, h, rs, :], vnb)


def _gdn_scan(u, w, qt, kt, att, el, nb, seq, ctx_len):
    m = u.shape[1]
    H = N_HEADS
    gw = GROUP_WIDTH
    g = ctx_len // CHUNK
    nblk = seq // ctx_len
    cblk0 = nb * nblk
    blk_f = lambda b, t: jnp.where(t == 0, cblk0 + b, b * nblk + t - 1)
    blk_b = lambda b, t: jnp.where(t == 0, cblk0 + b, b * nblk + nblk - t)
    in_specs, args = [], []
    for arr in (u, w, qt, kt):
        in_specs += [pl.BlockSpec((1, ctx_len, gw), lambda b, t: (0, blk_f(b, t), 0)),
                     pl.BlockSpec((1, ctx_len, gw), lambda b, t: (1, blk_b(b, t), 0))]
        args += [arr, arr]
    in_specs += [pl.BlockSpec((1, H, ctx_len, CHUNK), lambda b, t: (0, 0, blk_f(b, t), 0)),
                 pl.BlockSpec((1, H, ctx_len, CHUNK), lambda b, t: (1, 0, blk_b(b, t), 0)),
                 pl.BlockSpec((g, 2 * H, LANE), lambda b, t: (blk_f(b, t), 0, 0)),
                 pl.BlockSpec((g, 2 * H, LANE), lambda b, t: (blk_b(b, t), 0, 0))]
    args += [att, att, el, el]
    return pl.pallas_call(
        functools.partial(_gdn_scan_kernel, g),
        grid=(nb, nblk + 1),
        in_specs=in_specs,
        out_specs=[pl.BlockSpec((ctx_len, gw), lambda b, t: (blk_f(b, t), 0)),
                   pl.BlockSpec((ctx_len, gw), lambda b, t: (blk_b(b, t), 0))],
        out_shape=[jax.ShapeDtypeStruct((m, gw), F32), jax.ShapeDtypeStruct((m, gw), F32)],
        scratch_shapes=[pltpu.VMEM((2 * H, HEAD_DIM, HEAD_DIM), F32)],
        compiler_params=_cp("parallel", "arbitrary"),
        name="gdn_scan",
    )(*args)


def _head_norm_gate(o, nw, gate):
    o = o * lax.rsqrt(jnp.mean(o * o, axis=-1, keepdims=True) + RMS_EPS) * nw
    return o * _silu(gate)


def _combine_kernel(of_ref, ob_ref, g_ref, nw_ref, y_ref):
    nw = nw_ref[...]
    for h in range(N_HEADS):
        hs = slice(h * LANE, (h + 1) * LANE)
        y_ref[:, hs] = _head_norm_gate(of_ref[:, hs] + ob_ref[:, hs], nw, g_ref[:, hs]).astype(y_ref.dtype)


def _combine(o_f, o_b, z, gate_cb, norm_w, m_out, tm=512):
    gw = GROUP_WIDTH
    spec = pl.BlockSpec((tm, gw), lambda i: (i, 0))
    return pl.pallas_call(
        _combine_kernel,
        grid=(m_out // tm,),
        in_specs=[spec, spec, pl.BlockSpec((tm, gw), lambda i: (i, gate_cb * LANE // gw)),
                  pl.BlockSpec((1, LANE), lambda i: (0, 0))],
        out_specs=spec,
        out_shape=jax.ShapeDtypeStruct((m_out, gw), BF16),
        compiler_params=_cp("parallel"),
        name="combine",
    )(o_f, o_b, z, norm_w.reshape(1, LANE))


def _gla_kernel(g, qf_ref, ff_ref, if_ref, qb_ref, fb_ref, ib_ref, lb_ref, of_ref, ob_ref, s_ref):
    C, H, nsub = CHUNK, N_HEADS, CHUNK // SUB

    @pl.when(pl.program_id(1) == 0)
    def _():
        s_ref[...] = jnp.zeros_like(s_ref)

    r, c = _tri_masks(C)
    tri = (jnp.where(c <= r, 1.0, 0.0).astype(F32), jnp.where(c >= r, 1.0, 0.0).astype(F32))
    trow = lax.broadcasted_iota(jnp.int32, (C, HEAD_DIM), 0)
    srow = lax.broadcasted_iota(jnp.int32, (SUB, C), 0)
    scol = lax.broadcasted_iota(jnp.int32, (SUB, C), 1)
    chains = [(d, h) for d in range(2) for h in range(H)]
    srcs = ((qf_ref, ff_ref, if_ref, of_ref), (qb_ref, fb_ref, ib_ref, ob_ref))

    def body(j, carry):
        ph1 = []
        for d, h in chains:
            cj = g - 1 - j if d else j
            rs, hs = pl.ds(pl.multiple_of(cj * C, C), C), slice(h * LANE, (h + 1) * LANE)
            lb = lb_ref[:, hs]
            f = lb + (1.0 - lb) * jax.nn.sigmoid(srcs[d][1][rs, hs])
            ph1.append((rs, hs, f, _dot(tri[d], jnp.log(f), precision=HIGHEST)))
        ph2 = []
        for (d, h), (rs, hs, f, cum) in zip(chains, ph1):
            q_ref, _, i_ref, _ = srcs[d]
            q = _silu(q_ref[rs, hs]) * HEAD_DIM ** -0.5
            k = 1.0 - f
            vb = i_ref[rs, hs].astype(BF16)
            last = cum[0:1, :] if d else cum[C - 1:C, :]
            idx = d * H + h
            St = s_ref[idx]
            o_state = _dot_nt((q * jnp.exp(cum)).astype(BF16), St.astype(BF16))
            s_ref[idx] = St * jnp.exp(last) + _dot_tn(vb, (k * jnp.exp(last - cum)).astype(BF16))
            inter = []
            for a in range(nsub):
                sa = slice(a * SUB, (a + 1) * SUB)
                if d and a < nsub - 1:
                    cb = cum[(a + 1) * SUB:(a + 1) * SUB + 1, :]
                    kt = k * jnp.exp(jnp.where(trow >= (a + 1) * SUB, cb - cum, NEG))
                elif (not d) and a > 0:
                    cb = cum[a * SUB - 1:a * SUB, :]
                    kt = k * jnp.exp(jnp.where(trow < a * SUB, cb - cum, NEG))
                else:
                    inter.append(jnp.zeros((SUB, C), F32))
                    continue
                inter.append(_dot_nt((q[sa] * jnp.exp(cum[sa] - cb)).astype(BF16), kt.astype(BF16)))
            ph2.append((q, k, cum, vb, o_state, inter))
        for (d, h), (rs, hs, _, _), (q, k, cum, vb, o_state, inter) in zip(chains, ph1, ph2):
            blocks = []
            for a in range(nsub):
                sa = slice(a * SUB, (a + 1) * SUB)
                qa, ka, ca, sc = q[sa], k[sa], cum[sa], inter[a]
                for t in range(SUB):
                    dec = jnp.exp(ca - ca[t:t + 1, :])
                    st = jnp.sum(qa * ka[t:t + 1, :] * dec, axis=-1, keepdims=True)
                    ok = (srow <= t) if d else (srow >= t)
                    sc = jnp.where((scol == a * SUB + t) & ok, st, sc)
                blocks.append(sc)
            scores = jnp.concatenate(blocks, axis=0)
            srcs[d][3][rs, hs] = o_state + _dot(scores.astype(BF16), vb)
        return carry

    lax.fori_loop(0, g, body, 0)


def _gla_scan(z, lbs, nb, seq, ctx_len):
    m = z.shape[0]
    gw = GROUP_WIDTH
    g = ctx_len // CHUNK
    nblk = seq // ctx_len
    cblk0 = nb * nblk
    blk_f = lambda b, t: jnp.where(t == 0, cblk0 + b, b * nblk + t - 1)
    blk_b = lambda b, t: jnp.where(t == 0, cblk0 + b, b * nblk + nblk - t)
    col = lambda cb: cb * LANE // gw
    spec = lambda blk, cb: pl.BlockSpec((ctx_len, gw), lambda b, t: (blk(b, t), col(cb)))
    return pl.pallas_call(
        functools.partial(_gla_kernel, g),
        grid=(nb, nblk + 1),
        in_specs=[spec(blk_f, CB_HQ), spec(blk_f, CB_HFF), spec(blk_f, CB_HI),
                  spec(blk_b, CB_HQ), spec(blk_b, CB_HFB), spec(blk_b, CB_HI),
                  pl.BlockSpec((1, gw), lambda b, t: (0, 0))],
        out_specs=[pl.BlockSpec((ctx_len, gw), lambda b, t: (blk_f(b, t), 0)),
                   pl.BlockSpec((ctx_len, gw), lambda b, t: (blk_b(b, t), 0))],
        out_shape=[jax.ShapeDtypeStruct((m, gw), F32), jax.ShapeDtypeStruct((m, gw), F32)],
        scratch_shapes=[pltpu.VMEM((2 * N_HEADS, HEAD_DIM, HEAD_DIM), F32)],
        compiler_params=_cp("parallel", "arbitrary"),
        name="hgrn_scan",
    )(z, z, z, z, z, z, lbs.reshape(1, gw))


def _softmax_pv(parts):
    m = parts[0][0].max(axis=-1, keepdims=True)
    for s, _ in parts[1:]:
        m = jnp.maximum(m, s.max(axis=-1, keepdims=True))
    den, acc = None, None
    for s, v in parts:
        p = jnp.exp(s - m)
        d = jnp.sum(p, axis=-1, keepdims=True)
        a = _dot(p.astype(BF16), v)
        den = d if den is None else den + d
        acc = a if acc is None else acc + a
    return acc / den


def _na_kernel(emit_ctx, rows, *refs):
    q_ref, k_ref, v_ref, qc_ref, kc_ref, vc_ref, bias_ref = refs[:7]
    if emit_ctx:
        yl_ref, yc_ref, kb_ref, vb_ref = refs[7:]
    else:
        yl_ref, kb_ref, vb_ref = refs[7:]
    scale = HEAD_DIM ** -0.5
    win = NA_ROWS * GRID_W
    kb_ref[...] = k_ref[...].astype(BF16)
    vb_ref[...] = v_ref[...].astype(BF16)
    kc = kc_ref[...].astype(BF16)
    vc = vc_ref[...].astype(BF16)

    def body(i, carry):
        pre = []
        for t in range(NA_UNROLL):
            r = i * NA_UNROLL + t
            row0 = jnp.clip(r - NA_ROWS // 2, 0, rows - NA_ROWS)
            qs = pl.ds(pl.multiple_of(r * GRID_W, GRID_W), GRID_W)
            ks = pl.ds(pl.multiple_of(row0 * GRID_W, GRID_W), win)
            q = q_ref[qs, :].astype(BF16)
            pre.append((qs, ks, _dot_nt(q, kb_ref[ks, :]) * scale + bias_ref[0, r - row0], _dot_nt(q, kc) * scale))
        mid = []
        for qs, ks, s_win, s_ctx in pre:
            m = jnp.maximum(s_win.max(axis=-1, keepdims=True), s_ctx.max(axis=-1, keepdims=True))
            p_win, p_ctx = jnp.exp(s_win - m), jnp.exp(s_ctx - m)
            den = jnp.sum(p_win, axis=-1, keepdims=True) + jnp.sum(p_ctx, axis=-1, keepdims=True)
            mid.append((qs, ks, p_win.astype(BF16), p_ctx.astype(BF16), den))
        for qs, ks, p_win, p_ctx, den in mid:
            yl_ref[qs, :] = ((_dot(p_win, vb_ref[ks, :]) + _dot(p_ctx, vc)) / den).astype(yl_ref.dtype)
        return carry

    lax.fori_loop(0, rows // NA_UNROLL, body, 0)
    if emit_ctx:
        s = _dot_nt(qc_ref[...].astype(BF16), kc) * scale
        yc_ref[...] = _softmax_pv([(s, vc)]).astype(yc_ref.dtype)


def _na_bias_kernel(rpb_ref, o_ref):
    n = lax.broadcasted_iota(jnp.int32, (LANE, GRID_W * GRID_W), 1)
    j = lax.broadcasted_iota(jnp.int32, (LANE, GRID_W * GRID_W), 0)
    q, w = n >> 6, n & (GRID_W - 1)
    dc = jnp.clip(w - q, 1 - NA_COLS, NA_COLS - 1) + NA_COLS - 1
    onehot = jnp.where(dc == j, 1.0, 0.0).astype(F32)
    m = _dot(rpb_ref[...], onehot, precision=HIGHEST)
    c0 = jnp.clip(q[0:1] - NA_COLS // 2, 0, GRID_W - NA_COLS)
    ok = (w[0:1] >= c0) & (w[0:1] < c0 + NA_COLS)
    o_ref[...] = jnp.where(ok, m, NEG)


def _na_bias_tables(rpb):
    depth, H, nr, nc = rpb.shape
    assert GRID_W == 64 and depth * H * nr <= LANE and nc <= LANE
    flat = jnp.zeros((LANE, LANE), F32).at[:depth * H * nr, :nc].set(rpb.reshape(-1, nc).astype(F32))
    m = pl.pallas_call(
        _na_bias_kernel,
        out_shape=jax.ShapeDtypeStruct((LANE, GRID_W * GRID_W), F32),
        compiler_params=pltpu.CompilerParams(vmem_limit_bytes=VMEM_LIMIT),
        name="na_bias",
    )(flat)
    m = m[:depth * H * nr].reshape(depth, H, nr, GRID_W, GRID_W)
    tab = jnp.stack([jnp.stack([m[:, :, k - s + NA_ROWS - 1] for k in range(NA_ROWS)], axis=3)
                     for s in range(NA_ROWS)], axis=2)
    return tab.reshape(depth, H, NA_ROWS, GRID_W, NA_ROWS * GRID_W)


def _na(z, bias, nb, seq, ctx_len, emit_ctx):
    rows = seq // GRID_W
    cblk0 = nb * seq // ctx_len
    H = N_HEADS
    lat = lambda cb: pl.BlockSpec((seq, LANE), lambda b, h: (b, cb + h))
    ctx = lambda cb: pl.BlockSpec((ctx_len, LANE), lambda b, h: (cblk0 + b, cb + h))
    win = NA_ROWS * GRID_W
    in_specs = [lat(CB_NAQ), lat(CB_NAK), lat(CB_NAV), ctx(CB_NAQ), ctx(CB_NAK), ctx(CB_NAV),
                pl.BlockSpec((1, NA_ROWS, GRID_W, win), lambda b, h: (h, 0, 0, 0))]
    out_specs = [pl.BlockSpec((seq, LANE), lambda b, h: (b, h))]
    out_shape = [jax.ShapeDtypeStruct((nb * seq, GROUP_WIDTH), BF16)]
    if emit_ctx:
        out_specs.append(pl.BlockSpec((ctx_len, LANE), lambda b, h: (b, h)))
        out_shape.append(jax.ShapeDtypeStruct((nb * ctx_len, GROUP_WIDTH), BF16))
    return pl.pallas_call(
        functools.partial(_na_kernel, emit_ctx, rows),
        grid=(nb, H),
        in_specs=in_specs, out_specs=out_specs, out_shape=out_shape,
        scratch_shapes=[pltpu.VMEM((seq, LANE), BF16), pltpu.VMEM((seq, LANE), BF16)],
        compiler_params=_cp("parallel", "parallel"),
        name="na_attn",
    )(z, z, z, z, z, z, bias)


def _rms(x, w):
    return x * lax.rsqrt(jnp.mean(x * x, axis=-1, keepdims=True) + RMS_EPS) * w


def _mla_prep_kernel(cq_ref, ckv_ref, kra_ref, krb_ref, cc_ref, ss_ref, qnw_ref, kvnw_ref, wuq_ref, wukv_ref,
                     q_ref, kn_ref, kr_ref, v_ref):
    cc, ss = cc_ref[...], ss_ref[...]
    qn = _rms(cq_ref[...], qnw_ref[...]).astype(BF16)
    qa = _dot(qn, wuq_ref[...])
    for h in range(N_HEADS):
        b = 3 * LANE * h
        q_ref[:, 2 * LANE * h:2 * LANE * h + LANE] = qa[:, b:b + LANE].astype(BF16)
        q_ref[:, 2 * LANE * h + LANE:2 * LANE * (h + 1)] = (
            qa[:, b + LANE:b + 2 * LANE] * cc + qa[:, b + 2 * LANE:b + 3 * LANE] * ss).astype(BF16)
    kvn = _rms(ckv_ref[...], kvnw_ref[...]).astype(BF16)
    kv = _dot(kvn, wukv_ref[...])
    kn_ref[...] = kv[:, :GROUP_WIDTH].astype(BF16)
    v_ref[...] = kv[:, GROUP_WIDTH:].astype(BF16)
    kr_ref[...] = (kra_ref[...] * cc + krb_ref[...] * ss).astype(BF16)


def _mla_prep(z, cc, ss, qnw, kvnw, wuq, wukv, n_lat, seq, tm=512):
    m = z.shape[0]
    nlt, spt = n_lat // tm, seq // tm
    tab = lambda i: jnp.where(i < nlt, i % spt, spt)
    H = N_HEADS
    return pl.pallas_call(
        _mla_prep_kernel,
        grid=(m // tm,),
        in_specs=[pl.BlockSpec((tm, MLA_Q_RANK), lambda i: (i, CB_MQ * LANE // MLA_Q_RANK)),
                  pl.BlockSpec((tm, MLA_KV_RANK), lambda i: (i, CB_MKV * LANE // MLA_KV_RANK)),
                  pl.BlockSpec((tm, LANE), lambda i: (i, CB_MKRA)),
                  pl.BlockSpec((tm, LANE), lambda i: (i, CB_MKRB)),
                  pl.BlockSpec((tm, LANE), lambda i: (tab(i), 0)),
                  pl.BlockSpec((tm, LANE), lambda i: (tab(i), 0)),
                  pl.BlockSpec((1, MLA_Q_RANK), lambda i: (0, 0)),
                  pl.BlockSpec((1, MLA_KV_RANK), lambda i: (0, 0)),
                  pl.BlockSpec(wuq.shape, lambda i: (0, 0)),
                  pl.BlockSpec(wukv.shape, lambda i: (0, 0))],
        out_specs=[pl.BlockSpec((tm, 2 * LANE * H), lambda i: (i, 0)),
                   pl.BlockSpec((tm, GROUP_WIDTH), lambda i: (i, 0)),
                   pl.BlockSpec((tm, LANE), lambda i: (i, 0)),
                   pl.BlockSpec((tm, GROUP_WIDTH), lambda i: (i, 0))],
        out_shape=[jax.ShapeDtypeStruct((m, 2 * LANE * H), BF16),
                   jax.ShapeDtypeStruct((m, GROUP_WIDTH), BF16),
                   jax.ShapeDtypeStruct((m, LANE), BF16),
                   jax.ShapeDtypeStruct((m, GROUP_WIDTH), BF16)],
        compiler_params=_cp("parallel"),
        name="mla_prep",
    )(z, z, z, z, cc, ss, qnw.reshape(1, -1), kvnw.reshape(1, -1), wuq, wukv)


def _mla_attn_kernel(with_lat, seq, *refs):
    if with_lat:
        q_ref, knl_ref, krl_ref, vl_ref, knc_ref, krc_ref, vc_ref, y_ref, k_scr = refs
    else:
        q_ref, knc_ref, krc_ref, vc_ref, y_ref, k_scr = refs
    scale = MLA_QK_DIM ** -0.5
    nk = k_scr.shape[0]

    @pl.when(pl.program_id(2) == 0)
    def _():
        if with_lat:
            k_scr[0:seq, 0:LANE] = knl_ref[...]
            k_scr[0:seq, LANE:2 * LANE] = krl_ref[...]
        k_scr[nk - knc_ref.shape[0]:nk, 0:LANE] = knc_ref[...]
        k_scr[nk - knc_ref.shape[0]:nk, LANE:2 * LANE] = krc_ref[...]

    tq = q_ref.shape[0]
    qsub = min(tq, MLA_QSUB)
    nsub = tq // qsub
    vals = ([vl_ref] if with_lat else []) + [vc_ref]

    def qk(s):
        q = q_ref[s * qsub:(s + 1) * qsub, :]
        out = [_dot_nt(q, k_scr[0:seq, :])] if with_lat else []
        return out + [_dot_nt(q, k_scr[nk - knc_ref.shape[0]:nk, :])]

    def softmax(raw):
        m = raw[0].max(axis=-1, keepdims=True)
        for s in raw[1:]:
            m = jnp.maximum(m, s.max(axis=-1, keepdims=True))
        ps = [jnp.exp((s - m) * scale) for s in raw]
        den = sum(jnp.sum(p, axis=-1, keepdims=True) for p in ps)
        return [p.astype(BF16) for p in ps], den

    raw = qk(0)
    for s in range(nsub):
        nxt = qk(s + 1) if s + 1 < nsub else None
        ps, den = softmax(raw)
        acc = sum(_dot(p, v[...]) for p, v in zip(ps, vals))
        y_ref[s * qsub:(s + 1) * qsub, :] = (acc / den).astype(y_ref.dtype)
        raw = nxt


def _mla_attn(q, kn, kr, v, nb, seq, ctx_len, with_lat, tq=1024):
    H = N_HEADS
    cblk0 = nb * seq // ctx_len
    nq = seq if with_lat else ctx_len
    tq = min(tq, nq)
    qblk0 = 0 if with_lat else nb * seq // tq
    ctxs = [pl.BlockSpec((ctx_len, LANE), lambda b, h, i: (cblk0 + b, h)),
            pl.BlockSpec((ctx_len, LANE), lambda b, h, i: (cblk0 + b, 0)),
            pl.BlockSpec((ctx_len, LANE), lambda b, h, i: (cblk0 + b, h))]
    lats = [pl.BlockSpec((seq, LANE), lambda b, h, i: (b, h)),
            pl.BlockSpec((seq, LANE), lambda b, h, i: (b, 0)),
            pl.BlockSpec((seq, LANE), lambda b, h, i: (b, h))]
    in_specs = [pl.BlockSpec((tq, 2 * LANE), lambda b, h, i: (qblk0 + b * (nq // tq) + i, h))]
    args = [q]
    if with_lat:
        in_specs += lats
        args += [kn, kr, v]
    in_specs += ctxs
    args += [kn, kr, v]
    nk = (seq if with_lat else 0) + ctx_len
    return pl.pallas_call(
        functools.partial(_mla_attn_kernel, with_lat, seq),
        grid=(nb, H, nq // tq),
        in_specs=in_specs,
        out_specs=pl.BlockSpec((tq, LANE), lambda b, h, i: (b * (nq // tq) + i, h)),
        out_shape=jax.ShapeDtypeStruct((nb * nq, GROUP_WIDTH), BF16),
        scratch_shapes=[pltpu.VMEM((nk, 2 * LANE), BF16)],
        compiler_params=_cp("parallel", "parallel", "arbitrary"),
        name="mla_attn_lat" if with_lat else "mla_attn_ctx",
    )(*args)


def _prep_w_in(w_in):
    d = w_in.shape[0]
    gw = GROUP_WIDTH
    o_na = 4 * gw + 4 * N_HEADS
    o_mla = o_na + 3 * gw
    o_kr = o_mla + MLA_Q_RANK + MLA_KV_RANK
    o_hg = o_kr + MLA_ROPE
    kr = w_in[:, o_kr:o_hg]
    k1, k2 = kr[:, 0::2], kr[:, 1::2]
    z = lambda n: jnp.zeros((d, n), w_in.dtype)
    o_kv = o_mla + MLA_Q_RANK
    cols = [w_in[:, o_mla:o_kv], k1, k2, z(LANE - MLA_ROPE),
            w_in[:, o_kv:o_kr], k2, k1, z(LANE - MLA_ROPE),
            w_in[:, 4 * gw:o_na], z(LANE - 4 * N_HEADS),
            w_in[:, :4 * gw], w_in[:, o_na:o_mla], w_in[:, o_hg:]]
    w = jnp.concatenate(cols, axis=1).astype(BF16)
    assert w.shape[1] == NP_IN
    return w


def _prep_w_uq(w_uq):
    r = w_uq.shape[0]
    z = jnp.zeros((r, LANE - MLA_ROPE), w_uq.dtype)
    cols = []
    for h in range(N_HEADS):
        wh = w_uq[:, h * MLA_QK_DIM:(h + 1) * MLA_QK_DIM]
        rope = wh[:, MLA_NOPE:]
        r1, r2 = rope[:, 0::2], rope[:, 1::2]
        cols += [wh[:, :MLA_NOPE], r1, r2, z, r2, r1, z]
    return jnp.concatenate(cols, axis=1).astype(BF16)


def _rope_tables(seq, tm):
    n_freq = MLA_ROPE // 4
    freqs = ROPE_BASE ** (-jnp.arange(n_freq, dtype=F32) / n_freq)
    t = jnp.arange(seq)
    ang = jnp.concatenate([(t // GRID_W).astype(F32)[:, None] * freqs,
                           (t % GRID_W).astype(F32)[:, None] * freqs], -1)
    cos, sin = jnp.cos(ang), jnp.sin(ang)
    zp = jnp.zeros((seq, LANE - MLA_ROPE), F32)
    cc = jnp.concatenate([cos, cos, zp], axis=1)
    ss = jnp.concatenate([-sin, sin, zp], axis=1)
    ident = jnp.zeros((tm, LANE), F32).at[:, :MLA_ROPE].set(1.0)
    return jnp.concatenate([cc, ident], axis=0), jnp.concatenate([ss, jnp.zeros((tm, LANE), F32)], axis=0)


def kernel(x, c, ctx, c_ctx, w_ada, b_ada, w_in, gdn_conv_w, gdn_a_log, gdn_dt_bias, gdn_norm_w, na_rpb,
           mla_q_norm_w, mla_kv_norm_w, mla_w_uq, mla_w_uk, mla_w_uv, hgrn_lower_bounds, hgrn_norm_w, w_out,
           ln1_w, ln1_b, w_mlp1, w_mlp2, ln2_w, ln2_b):
    nb, seq, d = x.shape
    ctx_len = ctx.shape[1]
    depth = w_ada.shape[0]
    n_lat, n_ctx = nb * seq, nb * ctx_len
    alpha = (2 * depth) ** 0.25
    tm = 512
    tmm = 1024 if (seq % 1024 == 0 and n_ctx % 1024 == 0) else tm
    assert nb < 8 and seq % tm == 0 and n_ctx % tm == 0 and seq % ctx_len == 0 and ctx_len % CHUNK == 0

    cin = jnp.zeros((8, d), F32).at[:nb].set(c).at[nb].set(c_ctx)
    ada = _ada(cin, w_ada, b_ada)
    p_lb = jax.nn.softmax(hgrn_lower_bounds.astype(F32), axis=0)
    lbs = jnp.cumsum(p_lb, axis=0) - p_lb[0]
    cc, ss = _rope_tables(seq, tm)
    na_bias = _na_bias_tables(na_rpb)

    x_all = jnp.concatenate([x.reshape(n_lat, d), ctx.reshape(n_ctx, d)], axis=0)
    for l in range(depth):
        emit_ctx = l < depth - 1
        ada_r = ada[l].reshape(8 * 6, 1, d)
        z = _inproj(x_all, ada_r, _prep_w_in(w_in[l]), n_lat, seq, nb, tm=tmm)

        qkv = _gdn_conv(z, gdn_conv_w[l], nb, seq, ctx_len)
        p = _gdn_gates(z, gdn_a_log[l], gdn_dt_bias[l], tm=tm)
        ct = p[:, :2 * N_HEADS].reshape(-1, CHUNK, 2 * N_HEADS).transpose(0, 2, 1)
        o_f, o_b = _gdn_scan(*_gdn_prep(qkv, p, ct, nb, seq, ctx_len), nb, seq, ctx_len)
        m_out = n_lat + n_ctx if emit_ctx else n_lat
        ya = _combine(o_f, o_b, z, CB_GGATE, gdn_norm_w[l], m_out, tm=tm)
        yb = _na(z, na_bias[l], nb, seq, ctx_len, emit_ctx)
        q, kn, kr, v = _mla_prep(z, cc, ss, mla_q_norm_w[l], mla_kv_norm_w[l], _prep_w_uq(mla_w_uq[l]),
                                 jnp.concatenate([mla_w_uk[l], mla_w_uv[l]], axis=1).astype(BF16),
                                 n_lat, seq, tm=tm)
        ym = [_mla_attn(q, kn, kr, v, nb, seq, ctx_len, True)]
        if emit_ctx:
            ym.append(_mla_attn(q, kn, kr, v, nb, seq, ctx_len, False))
        yh = _combine(*_gla_scan(z, lbs[l], nb, seq, ctx_len), z, CB_HG, hgrn_norm_w[l], m_out, tm=tm)

        if emit_ctx:
            ys = [ya] + [jnp.concatenate(list(t), axis=0) for t in (yb, ym)] + [yh]
        else:
            ys = [ya, yb[0], ym[0], yh]
        x_all = _outproj(x_all, ys, w_out[l].astype(BF16), ada_r, ln1_w[l], ln1_b[l], m_out, seq, nb, alpha, tm=tm)
        x_all = _mlp(x_all, w_mlp1[l].astype(BF16), w_mlp2[l].astype(BF16), ada_r, ln2_w[l], ln2_b[l],
                     seq, nb, alpha, tm=tm)
    return x_all[:n_lat].reshape(nb, seq, d)
```

```python
import functools

import numpy as np
import jax
import jax.numpy as jnp
from jax import lax
from jax.experimental import pallas as pl
from jax.experimental.pallas import tpu as pltpu

F32 = jnp.float32
BF16 = jnp.bfloat16
HIGHEST = lax.Precision.HIGHEST

GRID_W = 64
N_HEADS = 4
HEAD_DIM = 128
GROUP_WIDTH = 512
CHUNK = 64
SUB = 8
GDN_CONV = 5
NA_ROWS = 8
NA_COLS = 16
NA_UNROLL = 4
MLA_Q_RANK = 384
MLA_KV_RANK = 256
MLA_NOPE = 128
MLA_ROPE = 64
MLA_QK_DIM = MLA_NOPE + MLA_ROPE
MLA_QSUB = 512
ROPE_BASE = 10000.0
LN_EPS = 1e-5
RMS_EPS = 1e-6
NEG = -1e30

LANE = 128
CB_MQ, CB_MKRA, CB_MKV, CB_MKRB, CB_GAB = 0, 3, 4, 6, 7
CB_GQKV, CB_GGATE = 8, 20
CB_NAQ, CB_NAK, CB_NAV = 24, 28, 32
CB_HQ, CB_HFF, CB_HFB, CB_HI, CB_HG = 36, 40, 44, 48, 52
CHUNK_SHIFT = 6
PREP_CHUNKS = 4
NP_IN = 56 * LANE

VMEM_LIMIT = 48 << 20


def _cp(*sem):
    return pltpu.CompilerParams(dimension_semantics=sem, vmem_limit_bytes=VMEM_LIMIT)


def _silu(x):
    return x * jax.nn.sigmoid(x)


def _dot(a, b, **kw):
    return jnp.dot(a, b, preferred_element_type=F32, **kw)


def _dot_nt(a, b, **kw):
    return lax.dot_general(a, b, (((1,), (1,)), ((), ())), preferred_element_type=F32, **kw)


def _dot_tn(a, b, **kw):
    return lax.dot_general(a, b, (((0,), (0,)), ((), ())), preferred_element_type=F32, **kw)


def _ada_kernel(c_ref, w_ref, b_ref, o_ref):
    s = _silu(c_ref[...])
    o_ref[0] = _dot(s, w_ref[0], precision=HIGHEST) + b_ref[0]


def _ada(cin, w_ada, b_ada):
    depth, d, n = w_ada.shape
    tn = 512
    return pl.pallas_call(
        _ada_kernel,
        grid=(depth, n // tn),
        in_specs=[pl.BlockSpec((8, d), lambda l, j: (0, 0)),
                  pl.BlockSpec((1, d, tn), lambda l, j: (l, 0, j)),
                  pl.BlockSpec((1, 1, tn), lambda l, j: (l, 0, j))],
        out_specs=pl.BlockSpec((1, 8, tn), lambda l, j: (l, 0, j)),
        out_shape=jax.ShapeDtypeStruct((depth, 8, n), F32),
        compiler_params=_cp("parallel", "parallel"),
        name="ada",
    )(cin, w_ada, b_ada.reshape(depth, 1, n))


ROW_STEP = 256


def _modulate(x_ref, sh_ref, sc_ref, xm_ref):
    sc1, sh = 1.0 + sc_ref[0], sh_ref[0]

    def body(t, carry):
        sl = pl.ds(pl.multiple_of(t * ROW_STEP, ROW_STEP), ROW_STEP)
        xm_ref[sl, :] = (x_ref[sl, :] * sc1 + sh).astype(BF16)
        return carry

    lax.fori_loop(0, x_ref.shape[0] // ROW_STEP, body, 0)


def _inproj_kernel(x_ref, sh_ref, sc_ref, w_ref, o_ref, xm_ref):
    @pl.when(pl.program_id(1) == 0)
    def _():
        _modulate(x_ref, sh_ref, sc_ref, xm_ref)
    o_ref[...] = _dot(xm_ref[...], w_ref[...])


def _inproj(x_all, ada_r, w, n_lat, seq, nb, tm=512, tn=1024):
    m, d = x_all.shape
    n = w.shape[1]
    row = lambda i: jnp.minimum((i * tm) // seq, nb)
    return pl.pallas_call(
        _inproj_kernel,
        grid=(m // tm, n // tn),
        in_specs=[pl.BlockSpec((tm, d), lambda i, j: (i, 0)),
                  pl.BlockSpec((1, 1, d), lambda i, j: (row(i) * 6 + 0, 0, 0)),
                  pl.BlockSpec((1, 1, d), lambda i, j: (row(i) * 6 + 1, 0, 0)),
                  pl.BlockSpec((d, tn), lambda i, j: (0, j))],
        out_specs=pl.BlockSpec((tm, tn), lambda i, j: (i, j)),
        out_shape=jax.ShapeDtypeStruct((m, n), F32),
        scratch_shapes=[pltpu.VMEM((tm, d), BF16)],
        compiler_params=_cp("parallel", "arbitrary"),
        name="inproj",
    )(x_all, ada_r, ada_r, w)


def _layernorm(r, w, b):
    mu = jnp.mean(r, axis=-1, keepdims=True)
    rc = r - mu
    var = jnp.mean(rc * rc, axis=-1, keepdims=True)
    return rc * lax.rsqrt(var + LN_EPS) * w + b


def _outproj_kernel(alpha, x_ref, ya_ref, yb_ref, ym_ref, yh_ref, w_ref, g_ref, lw_ref, lb_ref, o_ref):
    gw = GROUP_WIDTH
    acc = _dot(ya_ref[...], w_ref[0:gw, :])
    acc += _dot(yb_ref[...], w_ref[gw:2 * gw, :])
    acc += _dot(ym_ref[...], w_ref[2 * gw:3 * gw, :])
    acc += _dot(yh_ref[...], w_ref[3 * gw:4 * gw, :])
    r = alpha * x_ref[...] + g_ref[0] * acc
    o_ref[...] = _layernorm(r, lw_ref[...], lb_ref[...])


def _outproj(x_all, ys, w, ada_r, lw, lb, m_out, seq, nb, alpha, tm=512):
    d = x_all.shape[1]
    row = lambda i: jnp.minimum((i * tm) // seq, nb)
    yspec = pl.BlockSpec((tm, GROUP_WIDTH), lambda i: (i, 0))
    return pl.pallas_call(
        functools.partial(_outproj_kernel, alpha),
        grid=(m_out // tm,),
        in_specs=[pl.BlockSpec((tm, d), lambda i: (i, 0)), yspec, yspec, yspec, yspec,
                  pl.BlockSpec((d, d), lambda i: (0, 0)),
                  pl.BlockSpec((1, 1, d), lambda i: (row(i) * 6 + 2, 0, 0)),
                  pl.BlockSpec((1, d), lambda i: (0, 0)),
                  pl.BlockSpec((1, d), lambda i: (0, 0))],
        out_specs=pl.BlockSpec((tm, d), lambda i: (i, 0)),
        out_shape=jax.ShapeDtypeStruct((m_out, d), F32),
        compiler_params=_cp("parallel"),
        name="outproj_ln",
    )(x_all, *ys, w, ada_r, lw.reshape(1, d), lb.reshape(1, d))


def _mlp_kernel(alpha, x_ref, sh_ref, sc_ref, g_ref, w1_ref, w2_ref, lw_ref, lb_ref, o_ref, xm_ref, acc_ref):
    k = pl.program_id(1)

    @pl.when(k == 0)
    def _():
        _modulate(x_ref, sh_ref, sc_ref, xm_ref)
        acc_ref[...] = jnp.zeros_like(acc_ref)

    h = jnp.maximum(_dot(xm_ref[...], w1_ref[...]), 0.0)
    acc_ref[...] += _dot((h * h).astype(BF16), w2_ref[...])

    @pl.when(k == pl.num_programs(1) - 1)
    def _():
        g, lw, lb = g_ref[0], lw_ref[...], lb_ref[...]

        def body(t, carry):
            sl = pl.ds(pl.multiple_of(t * ROW_STEP, ROW_STEP), ROW_STEP)
            o_ref[sl, :] = _layernorm(alpha * x_ref[sl, :] + g * acc_ref[sl, :], lw, lb)
            return carry

        lax.fori_loop(0, x_ref.shape[0] // ROW_STEP, body, 0)


def _mlp(x_all, w1, w2, ada_r, lw, lb, seq, nb, alpha, tm=512, th=1024):
    m, d = x_all.shape
    hid = w1.shape[1]
    row = lambda i: jnp.minimum((i * tm) // seq, nb)
    return pl.pallas_call(
        functools.partial(_mlp_kernel, alpha),
        grid=(m // tm, hid // th),
        in_specs=[pl.BlockSpec((tm, d), lambda i, k: (i, 0)),
                  pl.BlockSpec((1, 1, d), lambda i, k: (row(i) * 6 + 3, 0, 0)),
                  pl.BlockSpec((1, 1, d), lambda i, k: (row(i) * 6 + 4, 0, 0)),
                  pl.BlockSpec((1, 1, d), lambda i, k: (row(i) * 6 + 5, 0, 0)),
                  pl.BlockSpec((d, th), lambda i, k: (0, k)),
                  pl.BlockSpec((th, d), lambda i, k: (k, 0)),
                  pl.BlockSpec((1, d), lambda i, k: (0, 0)),
                  pl.BlockSpec((1, d), lambda i, k: (0, 0))],
        out_specs=pl.BlockSpec((tm, d), lambda i, k: (i, 0)),
        out_shape=jax.ShapeDtypeStruct((m, d), F32),
        scratch_shapes=[pltpu.VMEM((tm, d), BF16), pltpu.VMEM((tm, d), F32)],
        compiler_params=_cp("parallel", "arbitrary"),
        name="mlp_ln",
    )(x_all, ada_r, ada_r, ada_r, w1, w2, lw.reshape(1, d), lb.reshape(1, d))


def _gdn_conv_kernel(seq, ctx_len, xl_ref, xc_ref, w_ref, o_ref, pad_ref):
    j = pl.program_id(1)
    w = w_ref[...]
    qscale = jnp.where(j < N_HEADS, HEAD_DIM ** -0.5, 1.0).astype(F32)
    p0 = 8 - GDN_CONV // 2
    for x_ref, nrows, o0 in ((xl_ref, seq, 0), (xc_ref, ctx_len, seq)):
        pad_ref[0:8, :] = jnp.zeros((8, LANE), F32)
        pad_ref[nrows + 8:nrows + 16, :] = jnp.zeros((8, LANE), F32)
        pad_ref[8:nrows + 8, :] = x_ref[...]
        rb = min(nrows, 256)
        for r0 in range(0, nrows, rb):
            y = pad_ref[r0 + p0:r0 + p0 + rb, :] * w[0:1, :]
            for i in range(1, GDN_CONV):
                y = y + pad_ref[r0 + p0 + i:r0 + p0 + i + rb, :] * w[i:i + 1, :]
            y = _silu(y)
            nrm = y * lax.rsqrt(jnp.sum(y * y, axis=-1, keepdims=True) + RMS_EPS) * qscale
            o_ref[o0 + r0:o0 + r0 + rb, :] = jnp.where(j < 2 * N_HEADS, nrm, y)


def _gdn_conv(z, conv_w, nb, seq, ctx_len):
    nblk = 3 * N_HEADS
    cblk0 = nb * seq // ctx_len
    return pl.pallas_call(
        functools.partial(_gdn_conv_kernel, seq, ctx_len),
        grid=(nb, nblk),
        in_specs=[pl.BlockSpec((seq, LANE), lambda b, j: (b, CB_GQKV + j)),
                  pl.BlockSpec((ctx_len, LANE), lambda b, j: (cblk0 + b, CB_GQKV + j)),
                  pl.BlockSpec((GDN_CONV, LANE), lambda b, j: (0, j))],
        out_specs=pl.BlockSpec((seq + ctx_len, LANE), lambda b, j: (b, j)),
        out_shape=jax.ShapeDtypeStruct((nb * (seq + ctx_len), nblk * LANE), F32),
        scratch_shapes=[pltpu.VMEM((seq + 16, LANE), F32)],
        compiler_params=_cp("parallel", "parallel"),
        name="gdn_conv",
    )(z, z, conv_w)


def _gdn_gates_kernel(tm, s_ref, alog_ref, dtb_ref, o_ref):
    s = s_ref[...]
    g = -jnp.exp(alog_ref[...]) * (jnp.maximum(s + dtb_ref[...], 0.0)
                                    + jnp.log1p(jnp.exp(-jnp.abs(s + dtb_ref[...]))))
    r = lax.broadcasted_iota(jnp.int32, (tm, tm), 0)
    c = lax.broadcasted_iota(jnp.int32, (tm, tm), 1)
    same = (r >> CHUNK_SHIFT) == (c >> CHUNK_SHIFT)
    lo = jnp.where(same & (c <= r), 1.0, 0.0).astype(F32)
    up = jnp.where(same & (c >= r), 1.0, 0.0).astype(F32)
    cum_f = _dot(lo, g, precision=HIGHEST)
    cum_b = _dot(up, g, precision=HIGHEST)
    col = lax.broadcasted_iota(jnp.int32, s.shape, 1)
    o_ref[...] = jnp.where(col < N_HEADS, cum_f,
                           jnp.where(col < 2 * N_HEADS, cum_b,
                                     jnp.where(col < 4 * N_HEADS, jax.nn.sigmoid(s), 0.0)))


def _gdn_gates(z, a_log, dt_bias, tm=512):
    m = z.shape[0]
    pad = lambda v: jnp.zeros((1, LANE), F32).at[0, :2 * N_HEADS].set(v.reshape(-1).astype(F32))
    return pl.pallas_call(
        functools.partial(_gdn_gates_kernel, tm),
        grid=(m // tm,),
        in_specs=[pl.BlockSpec((tm, LANE), lambda i: (i, CB_GAB)),
                  pl.BlockSpec((1, LANE), lambda i: (0, 0)),
                  pl.BlockSpec((1, LANE), lambda i: (0, 0))],
        out_specs=pl.BlockSpec((tm, LANE), lambda i: (i, 0)),
        out_shape=jax.ShapeDtypeStruct((m, LANE), F32),
        compiler_params=_cp("parallel"),
        name="gdn_gates",
    )(z, pad(a_log), pad(dt_bias))


def _tri_masks(n):
    r = lax.broadcasted_iota(jnp.int32, (n, n), 0)
    c = lax.broadcasted_iota(jnp.int32, (n, n), 1)
    return r, c


def _split(x):
    hi = x.astype(BF16)
    return hi, (x - hi.astype(F32)).astype(BF16)


def _dot3(a, b):
    return _dot(a[0], b[0]) + (_dot(a[0], b[1]) + _dot(a[1], b[0]))


def _gdn_prep_kernel(qkv_ref, p_ref, ct_ref, u_ref, w_ref, qt_ref, kt_ref, att_ref, el_ref):
    C, H = CHUNK, N_HEADS
    lane = lax.broadcasted_iota(jnp.int32, (C, LANE), 1)
    sub8 = lax.broadcasted_iota(jnp.int32, (2 * H, C), 0)
    r, c = _tri_masks(C)
    eye = jnp.where(r == c, 1.0, 0.0).astype(F32)
    a_list, rhs_list, where_list = [], [], []
    for n in range(PREP_CHUNKS):
        rs = slice(n * C, (n + 1) * C)
        pblk = p_ref[rs, :]
        tblk = ct_ref[n]
        col = lambda idx, pblk=pblk: jnp.sum(jnp.where(lane == idx, pblk, 0.0), axis=-1, keepdims=True)
        row = lambda idx, tblk=tblk: jnp.sum(jnp.where(sub8 == idx, tblk, 0.0), axis=0, keepdims=True)
        for h in range(H):
            hs = slice(h * LANE, (h + 1) * LANE)
            q = qkv_ref[rs, h * LANE:(h + 1) * LANE]
            k = qkv_ref[rs, (H + h) * LANE:(H + h + 1) * LANE]
            v = qkv_ref[rs, (2 * H + h) * LANE:(2 * H + h + 1) * LANE]
            qbf, kbf = q.astype(BF16), k.astype(BF16)
            for d in range(2):
                idx = d * H + h
                cum_c, cum_r, beta_c = col(idx), row(idx), col(2 * H + idx)
                incl = (c >= r) if d else (c <= r)
                strict = (c > r) if d else (c < r)
                last = cum_r[:, 0:1] if d else cum_r[:, C - 1:C]
                decay = jnp.exp(jnp.where(incl, cum_c - cum_r, NEG))
                kb = k * beta_c
                ec = jnp.exp(cum_c)
                a_list.append(jnp.where(strict, _dot_nt(kb.astype(BF16), kbf) * decay, 0.0))
                rhs_list.append(_split(jnp.concatenate([v * beta_c, kb * ec], axis=-1)))
                where_list.append((d, rs, hs))
                att_ref[d, h, rs, :] = jnp.where(incl, _dot_nt(qbf, kbf) * decay, 0.0).astype(BF16)
                qt_ref[d, rs, hs] = (q * ec).astype(BF16)
                kt_ref[d, rs, hs] = (k * jnp.exp(last - cum_c)).astype(BF16)
                el_ref[n, idx:idx + 1, :] = jnp.broadcast_to(jnp.exp(last), (1, LANE))
    ts = [eye - a for a in a_list]
    ps = [a.astype(BF16) for a in a_list]
    for _ in range(5):
        ps = [_dot(p, p).astype(BF16) for p in ps]
        ts = [t + _dot(t.astype(BF16), p) for t, p in zip(ts, ps)]
    res = [eye - t - _dot3(_split(a), _split(t)) for a, t in zip(a_list, ts)]
    ts = [t + _dot(t.astype(BF16), e.astype(BF16)) for t, e in zip(ts, res)]
    sols = [_dot3(_split(t), rhs) for t, rhs in zip(ts, rhs_list)]
    for sol, (d, rs, hs) in zip(sols, where_list):
        u_ref[d, rs, hs] = sol[:, :HEAD_DIM]
        w_ref[d, rs, hs] = sol[:, HEAD_DIM:].astype(BF16)


def _gdn_prep(qkv, p, ct, nb, seq, ctx_len):
    m = qkv.shape[0]
    H = N_HEADS
    gw = GROUP_WIDTH
    rows = PREP_CHUNKS * CHUNK
    assert seq % rows == 0 and ctx_len % rows == 0
    dspec = pl.BlockSpec((2, rows, gw), lambda i: (0, i, 0))
    nl, nc = seq // rows, ctx_len // rows

    def qkv_blk(i):
        ic = i - nb * nl
        return jnp.where(i < nb * nl, (i // nl) * (nl + nc) + i % nl, (ic // nc) * (nl + nc) + nl + ic % nc)

    return pl.pallas_call(
        _gdn_prep_kernel,
        grid=(m // rows,),
        in_specs=[pl.BlockSpec((rows, 3 * gw), lambda i: (qkv_blk(i), 0)),
                  pl.BlockSpec((rows, LANE), lambda i: (i, 0)),
                  pl.BlockSpec((PREP_CHUNKS, 2 * H, CHUNK), lambda i: (i, 0, 0))],
        out_specs=[dspec, dspec, dspec, dspec,
                   pl.BlockSpec((2, H, rows, CHUNK), lambda i: (0, 0, i, 0)),
                   pl.BlockSpec((PREP_CHUNKS, 2 * H, LANE), lambda i: (i, 0, 0))],
        out_shape=[jax.ShapeDtypeStruct((2, m, gw), F32),
                   jax.ShapeDtypeStruct((2, m, gw), BF16),
                   jax.ShapeDtypeStruct((2, m, gw), BF16),
                   jax.ShapeDtypeStruct((2, m, gw), BF16),
                   jax.ShapeDtypeStruct((2, H, m, CHUNK), BF16),
                   jax.ShapeDtypeStruct((m // CHUNK, 2 * H, LANE), F32)],
        compiler_params=_cp("parallel"),
        name="gdn_prep",
    )(qkv, p, ct)


def _gdn_scan_kernel(g, *refs):
    ins, (of_ref, ob_ref, s_ref) = refs[:12], refs[12:]
    H = N_HEADS

    @pl.when(pl.program_id(1) == 0)
    def _():
        s_ref[...] = jnp.zeros_like(s_ref)

    chains = [(d, h) for d in range(2) for h in range(H)]
    for j in range(g):
        mid = []
        for d, h in chains:
            u_ref, w_ref, qt_ref = ins[d:6:2]
            cj = g - 1 - j if d else j
            rs, hs = slice(cj * CHUNK, (cj + 1) * CHUNK), slice(h * LANE, (h + 1) * LANE)
            sb = s_ref[d * H + h].astype(BF16)
            vnb = (u_ref[0, rs, hs] - _dot(w_ref[0, rs, hs], sb)).astype(BF16)
            mid.append((vnb, _dot(qt_ref[0, rs, hs], sb)))
        for (d, h), (vnb, o_state) in zip(chains, mid):
            kt_ref, att_ref, el_ref = ins[6 + d::2]
            o_ref = ob_ref if d else of_ref
            cj = g - 1 - j if d else j
            rs, hs = slice(cj * CHUNK, (cj + 1) * CHUNK), slice(h * LANE, (h + 1) * LANE)
            idx = d * H + h
            s_ref[idx] = s_ref[idx] * el_ref[cj, idx:idx + 1, :] + _dot_tn(kt_ref[0, rs, hs], vnb)
            o_ref[rs, hs] = o_state + _dot(att_ref[0, h, rs, :], vnb)


def _gdn_scan(u, w, qt, kt, att, el, nb, seq, ctx_len):
    m = u.shape[1]
    H = N_HEADS
    gw = GROUP_WIDTH
    g = ctx_len // CHUNK
    nblk = seq // ctx_len
    cblk0 = nb * nblk
    blk_f = lambda b, t: jnp.where(t == 0, cblk0 + b, b * nblk + t - 1)
    blk_b = lambda b, t: jnp.where(t == 0, cblk0 + b, b * nblk + nblk - t)
    in_specs, args = [], []
    for arr in (u, w, qt, kt):
        in_specs += [pl.BlockSpec((1, ctx_len, gw), lambda b, t: (0, blk_f(b, t), 0)),
                     pl.BlockSpec((1, ctx_len, gw), lambda b, t: (1, blk_b(b, t), 0))]
        args += [arr, arr]
    in_specs += [pl.BlockSpec((1, H, ctx_len, CHUNK), lambda b, t: (0, 0, blk_f(b, t), 0)),
                 pl.BlockSpec((1, H, ctx_len, CHUNK), lambda b, t: (1, 0, blk_b(b, t), 0)),
                 pl.BlockSpec((g, 2 * H, LANE), lambda b, t: (blk_f(b, t), 0, 0)),
                 pl.BlockSpec((g, 2 * H, LANE), lambda b, t: (blk_b(b, t), 0, 0))]
    args += [att, att, el, el]
    return pl.pallas_call(
        functools.partial(_gdn_scan_kernel, g),
        grid=(nb, nblk + 1),
        in_specs=in_specs,
        out_specs=[pl.BlockSpec((ctx_len, gw), lambda b, t: (blk_f(b, t), 0)),
                   pl.BlockSpec((ctx_len, gw), lambda b, t: (blk_b(b, t), 0))],
        out_shape=[jax.ShapeDtypeStruct((m, gw), F32), jax.ShapeDtypeStruct((m, gw), F32)],
        scratch_shapes=[pltpu.VMEM((2 * H, HEAD_DIM, HEAD_DIM), F32)],
        compiler_params=_cp("parallel", "arbitrary"),
        name="gdn_scan",
    )(*args)


def _head_norm_gate(o, nw, gate):
    o = o * lax.rsqrt(jnp.mean(o * o, axis=-1, keepdims=True) + RMS_EPS) * nw
    return o * _silu(gate)


def _combine_kernel(of_ref, ob_ref, g_ref, nw_ref, y_ref):
    nw = nw_ref[...]
    for h in range(N_HEADS):
        hs = slice(h * LANE, (h + 1) * LANE)
        y_ref[:, hs] = _head_norm_gate(of_ref[:, hs] + ob_ref[:, hs], nw, g_ref[:, hs]).astype(y_ref.dtype)


def _combine(o_f, o_b, z, gate_cb, norm_w, m_out, tm=512):
    gw = GROUP_WIDTH
    spec = pl.BlockSpec((tm, gw), lambda i: (i, 0))
    return pl.pallas_call(
        _combine_kernel,
        grid=(m_out // tm,),
        in_specs=[spec, spec, pl.BlockSpec((tm, gw), lambda i: (i, gate_cb * LANE // gw)),
                  pl.BlockSpec((1, LANE), lambda i: (0, 0))],
        out_specs=spec,
        out_shape=jax.ShapeDtypeStruct((m_out, gw), BF16),
        compiler_params=_cp("parallel"),
        name="combine",
    )(o_f, o_b, z, norm_w.reshape(1, LANE))


def _gla_kernel(g, qf_ref, ff_ref, if_ref, qb_ref, fb_ref, ib_ref, lb_ref, of_ref, ob_ref, s_ref):
    C, H, nsub = CHUNK, N_HEADS, CHUNK // SUB

    @pl.when(pl.program_id(1) == 0)
    def _():
        s_ref[...] = jnp.zeros_like(s_ref)

    r, c = _tri_masks(C)
    tri = (jnp.where(c <= r, 1.0, 0.0).astype(F32), jnp.where(c >= r, 1.0, 0.0).astype(F32))
    trow = lax.broadcasted_iota(jnp.int32, (C, HEAD_DIM), 0)
    srow = lax.broadcasted_iota(jnp.int32, (SUB, C), 0)
    scol = lax.broadcasted_iota(jnp.int32, (SUB, C), 1)
    chains = [(d, h) for d in range(2) for h in range(H)]
    srcs = ((qf_ref, ff_ref, if_ref, of_ref), (qb_ref, fb_ref, ib_ref, ob_ref))

    def body(j, carry):
        ph1 = []
        for d, h in chains:
            cj = g - 1 - j if d else j
            rs, hs = pl.ds(pl.multiple_of(cj * C, C), C), slice(h * LANE, (h + 1) * LANE)
            lb = lb_ref[:, hs]
            f = lb + (1.0 - lb) * jax.nn.sigmoid(srcs[d][1][rs, hs])
            ph1.append((rs, hs, f, _dot(tri[d], jnp.log(f), precision=HIGHEST)))
        ph2 = []
        for (d, h), (rs, hs, f, cum) in zip(chains, ph1):
            q_ref, _, i_ref, _ = srcs[d]
            q = _silu(q_ref[rs, hs]) * HEAD_DIM ** -0.5
            k = 1.0 - f
            vb = i_ref[rs, hs].astype(BF16)
            last = cum[0:1, :] if d else cum[C - 1:C, :]
            idx = d * H + h
            St = s_ref[idx]
            o_state = _dot_nt((q * jnp.exp(cum)).astype(BF16), St.astype(BF16))
            s_ref[idx] = St * jnp.exp(last) + _dot_tn(vb, (k * jnp.exp(last - cum)).astype(BF16))
            inter = []
            for a in range(nsub):
                sa = slice(a * SUB, (a + 1) * SUB)
                if d and a < nsub - 1:
                    cb = cum[(a + 1) * SUB:(a + 1) * SUB + 1, :]
                    kt = k * jnp.exp(jnp.where(trow >= (a + 1) * SUB, cb - cum, NEG))
                elif (not d) and a > 0:
                    cb = cum[a * SUB - 1:a * SUB, :]
                    kt = k * jnp.exp(jnp.where(trow < a * SUB, cb - cum, NEG))
                else:
                    inter.append(jnp.zeros((SUB, C), F32))
                    continue
                inter.append(_dot_nt((q[sa] * jnp.exp(cum[sa] - cb)).astype(BF16), kt.astype(BF16)))
            ph2.append((q, k, cum, vb, o_state, inter))
        for (d, h), (rs, hs, _, _), (q, k, cum, vb, o_state, inter) in zip(chains, ph1, ph2):
            blocks = []
            for a in range(nsub):
                sa = slice(a * SUB, (a + 1) * SUB)
                qa, ka, ca, sc = q[sa], k[sa], cum[sa], inter[a]
                for t in range(SUB):
                    dec = jnp.exp(ca - ca[t:t + 1, :])
                    st = jnp.sum(qa * ka[t:t + 1, :] * dec, axis=-1, keepdims=True)
                    ok = (srow <= t) if d else (srow >= t)
                    sc = jnp.where((scol == a * SUB + t) & ok, st, sc)
                blocks.append(sc)
            scores = jnp.concatenate(blocks, axis=0)
            srcs[d][3][rs, hs] = o_state + _dot(scores.astype(BF16), vb)
        return carry

    lax.fori_loop(0, g, body, 0)


def _gla_scan(z, lbs, nb, seq, ctx_len):
    m = z.shape[0]
    gw = GROUP_WIDTH
    g = ctx_len // CHUNK
    nblk = seq // ctx_len
    cblk0 = nb * nblk
    blk_f = lambda b, t: jnp.where(t == 0, cblk0 + b, b * nblk + t - 1)
    blk_b = lambda b, t: jnp.where(t == 0, cblk0 + b, b * nblk + nblk - t)
    col = lambda cb: cb * LANE // gw
    spec = lambda blk, cb: pl.BlockSpec((ctx_len, gw), lambda b, t: (blk(b, t), col(cb)))
    return pl.pallas_call(
        functools.partial(_gla_kernel, g),
        grid=(nb, nblk + 1),
        in_specs=[spec(blk_f, CB_HQ), spec(blk_f, CB_HFF), spec(blk_f, CB_HI),
                  spec(blk_b, CB_HQ), spec(blk_b, CB_HFB), spec(blk_b, CB_HI),
                  pl.BlockSpec((1, gw), lambda b, t: (0, 0))],
        out_specs=[pl.BlockSpec((ctx_len, gw), lambda b, t: (blk_f(b, t), 0)),
                   pl.BlockSpec((ctx_len, gw), lambda b, t: (blk_b(b, t), 0))],
        out_shape=[jax.ShapeDtypeStruct((m, gw), F32), jax.ShapeDtypeStruct((m, gw), F32)],
        scratch_shapes=[pltpu.VMEM((2 * N_HEADS, HEAD_DIM, HEAD_DIM), F32)],
        compiler_params=_cp("parallel", "arbitrary"),
        name="hgrn_scan",
    )(z, z, z, z, z, z, lbs.reshape(1, gw))


def _softmax_pv(parts):
    m = parts[0][0].max(axis=-1, keepdims=True)
    for s, _ in parts[1:]:
        m = jnp.maximum(m, s.max(axis=-1, keepdims=True))
    den, acc = None, None
    for s, v in parts:
        p = jnp.exp(s - m)
        d = jnp.sum(p, axis=-1, keepdims=True)
        a = _dot(p.astype(BF16), v)
        den = d if den is None else den + d
        acc = a if acc is None else acc + a
    return acc / den


def _na_kernel(emit_ctx, rows, *refs):
    q_ref, k_ref, v_ref, qc_ref, kc_ref, vc_ref, bias_ref = refs[:7]
    if emit_ctx:
        yl_ref, yc_ref, kb_ref, vb_ref = refs[7:]
    else:
        yl_ref, kb_ref, vb_ref = refs[7:]
    scale = HEAD_DIM ** -0.5
    win = NA_ROWS * GRID_W
    kb_ref[...] = k_ref[...].astype(BF16)
    vb_ref[...] = v_ref[...].astype(BF16)
    kc = kc_ref[...].astype(BF16)
    vc = vc_ref[...].astype(BF16)

    def body(i, carry):
        pre = []
        for t in range(NA_UNROLL):
            r = i * NA_UNROLL + t
            row0 = jnp.clip(r - NA_ROWS // 2, 0, rows - NA_ROWS)
            qs = pl.ds(pl.multiple_of(r * GRID_W, GRID_W), GRID_W)
            ks = pl.ds(pl.multiple_of(row0 * GRID_W, GRID_W), win)
            q = q_ref[qs, :].astype(BF16)
            pre.append((qs, ks, _dot_nt(q, kb_ref[ks, :]) * scale + bias_ref[0, r - row0], _dot_nt(q, kc) * scale))
        mid = []
        for qs, ks, s_win, s_ctx in pre:
            m = jnp.maximum(s_win.max(axis=-1, keepdims=True), s_ctx.max(axis=-1, keepdims=True))
            p_win, p_ctx = jnp.exp(s_win - m), jnp.exp(s_ctx - m)
            den = jnp.sum(p_win, axis=-1, keepdims=True) + jnp.sum(p_ctx, axis=-1, keepdims=True)
            mid.append((qs, ks, p_win.astype(BF16), p_ctx.astype(BF16), den))
        for qs, ks, p_win, p_ctx, den in mid:
            yl_ref[qs, :] = ((_dot(p_win, vb_ref[ks, :]) + _dot(p_ctx, vc)) / den).astype(yl_ref.dtype)
        return carry

    lax.fori_loop(0, rows // NA_UNROLL, body, 0)
    if emit_ctx:
        s = _dot_nt(qc_ref[...].astype(BF16), kc) * scale
        yc_ref[...] = _softmax_pv([(s, vc)]).astype(yc_ref.dtype)


def _na_bias_kernel(rpb_ref, o_ref):
    n = lax.broadcasted_iota(jnp.int32, (LANE, GRID_W * GRID_W), 1)
    j = lax.broadcasted_iota(jnp.int32, (LANE, GRID_W * GRID_W), 0)
    q, w = n >> 6, n & (GRID_W - 1)
    dc = jnp.clip(w - q, 1 - NA_COLS, NA_COLS - 1) + NA_COLS - 1
    onehot = jnp.where(dc == j, 1.0, 0.0).astype(F32)
    m = _dot(rpb_ref[...], onehot, precision=HIGHEST)
    c0 = jnp.clip(q[0:1] - NA_COLS // 2, 0, GRID_W - NA_COLS)
    ok = (w[0:1] >= c0) & (w[0:1] < c0 + NA_COLS)
    o_ref[...] = jnp.where(ok, m, NEG)


def _na_bias_tables(rpb):
    depth, H, nr, nc = rpb.shape
    assert GRID_W == 64 and depth * H * nr <= LANE and nc <= LANE
    flat = jnp.zeros((LANE, LANE), F32).at[:depth * H * nr, :nc].set(rpb.reshape(-1, nc).astype(F32))
    m = pl.pallas_call(
        _na_bias_kernel,
        out_shape=jax.ShapeDtypeStruct((LANE, GRID_W * GRID_W), F32),
        compiler_params=pltpu.CompilerParams(vmem_limit_bytes=VMEM_LIMIT),
        name="na_bias",
    )(flat)
    m = m[:depth * H * nr].reshape(depth, H, nr, GRID_W, GRID_W)
    tab = jnp.stack([jnp.stack([m[:, :, k - s + NA_ROWS - 1] for k in range(NA_ROWS)], axis=3)
                     for s in range(NA_ROWS)], axis=2)
    return tab.reshape(depth, H, NA_ROWS, GRID_W, NA_ROWS * GRID_W)


def _na(z, bias, nb, seq, ctx_len, emit_ctx):
    rows = seq // GRID_W
    cblk0 = nb * seq // ctx_len
    H = N_HEADS
    lat = lambda cb: pl.BlockSpec((seq, LANE), lambda b, h: (b, cb + h))
    ctx = lambda cb: pl.BlockSpec((ctx_len, LANE), lambda b, h: (cblk0 + b, cb + h))
    win = NA_ROWS * GRID_W
    in_specs = [lat(CB_NAQ), lat(CB_NAK), lat(CB_NAV), ctx(CB_NAQ), ctx(CB_NAK), ctx(CB_NAV),
                pl.BlockSpec((1, NA_ROWS, GRID_W, win), lambda b, h: (h, 0, 0, 0))]
    out_specs = [pl.BlockSpec((seq, LANE), lambda b, h: (b, h))]
    out_shape = [jax.ShapeDtypeStruct((nb * seq, GROUP_WIDTH), BF16)]
    if emit_ctx:
        out_specs.append(pl.BlockSpec((ctx_len, LANE), lambda b, h: (b, h)))
        out_shape.append(jax.ShapeDtypeStruct((nb * ctx_len, GROUP_WIDTH), BF16))
    return pl.pallas_call(
        functools.partial(_na_kernel, emit_ctx, rows),
        grid=(nb, H),
        in_specs=in_specs, out_specs=out_specs, out_shape=out_shape,
        scratch_shapes=[pltpu.VMEM((seq, LANE), BF16), pltpu.VMEM((seq, LANE), BF16)],
        compiler_params=_cp("parallel", "parallel"),
        name="na_attn",
    )(z, z, z, z, z, z, bias)


def _rms(x, w):
    return x * lax.rsqrt(jnp.mean(x * x, axis=-1, keepdims=True) + RMS_EPS) * w


def _mla_prep_kernel(cq_ref, ckv_ref, kra_ref, krb_ref, cc_ref, ss_ref, qnw_ref, kvnw_ref, wuq_ref, wukv_ref,
                     q_ref, kn_ref, kr_ref, v_ref):
    cc, ss = cc_ref[...], ss_ref[...]
    qn = _rms(cq_ref[...], qnw_ref[...]).astype(BF16)
    qa = _dot(qn, wuq_ref[...])
    for h in range(N_HEADS):
        b = 3 * LANE * h
        q_ref[:, 2 * LANE * h:2 * LANE * h + LANE] = qa[:, b:b + LANE].astype(BF16)
        q_ref[:, 2 * LANE * h + LANE:2 * LANE * (h + 1)] = (
            qa[:, b + LANE:b + 2 * LANE] * cc + qa[:, b + 2 * LANE:b + 3 * LANE] * ss).astype(BF16)
    kvn = _rms(ckv_ref[...], kvnw_ref[...]).astype(BF16)
    kv = _dot(kvn, wukv_ref[...])
    kn_ref[...] = kv[:, :GROUP_WIDTH].astype(BF16)
    v_ref[...] = kv[:, GROUP_WIDTH:].astype(BF16)
    kr_ref[...] = (kra_ref[...] * cc + krb_ref[...] * ss).astype(BF16)


def _mla_prep(z, cc, ss, qnw, kvnw, wuq, wukv, n_lat, seq, tm=512):
    m = z.shape[0]
    nlt, spt = n_lat // tm, seq // tm
    tab = lambda i: jnp.where(i < nlt, i % spt, spt)
    H = N_HEADS
    return pl.pallas_call(
        _mla_prep_kernel,
        grid=(m // tm,),
        in_specs=[pl.BlockSpec((tm, MLA_Q_RANK), lambda i: (i, CB_MQ * LANE // MLA_Q_RANK)),
                  pl.BlockSpec((tm, MLA_KV_RANK), lambda i: (i, CB_MKV * LANE // MLA_KV_RANK)),
                  pl.BlockSpec((tm, LANE), lambda i: (i, CB_MKRA)),
                  pl.BlockSpec((tm, LANE), lambda i: (i, CB_MKRB)),
                  pl.BlockSpec((tm, LANE), lambda i: (tab(i), 0)),
                  pl.BlockSpec((tm, LANE), lambda i: (tab(i), 0)),
                  pl.BlockSpec((1, MLA_Q_RANK), lambda i: (0, 0)),
                  pl.BlockSpec((1, MLA_KV_RANK), lambda i: (0, 0)),
                  pl.BlockSpec(wuq.shape, lambda i: (0, 0)),
                  pl.BlockSpec(wukv.shape, lambda i: (0, 0))],
        out_specs=[pl.BlockSpec((tm, 2 * LANE * H), lambda i: (i, 0)),
                   pl.BlockSpec((tm, GROUP_WIDTH), lambda i: (i, 0)),
                   pl.BlockSpec((tm, LANE), lambda i: (i, 0)),
                   pl.BlockSpec((tm, GROUP_WIDTH), lambda i: (i, 0))],
        out_shape=[jax.ShapeDtypeStruct((m, 2 * LANE * H), BF16),
                   jax.ShapeDtypeStruct((m, GROUP_WIDTH), BF16),
                   jax.ShapeDtypeStruct((m, LANE), BF16),
                   jax.ShapeDtypeStruct((m, GROUP_WIDTH), BF16)],
        compiler_params=_cp("parallel"),
        name="mla_prep",
    )(z, z, z, z, cc, ss, qnw.reshape(1, -1), kvnw.reshape(1, -1), wuq, wukv)


def _mla_attn_kernel(with_lat, seq, *refs):
    if with_lat:
        q_ref, knl_ref, krl_ref, vl_ref, knc_ref, krc_ref, vc_ref, y_ref, k_scr = refs
    else:
        q_ref, knc_ref, krc_ref, vc_ref, y_ref, k_scr = refs
    scale = MLA_QK_DIM ** -0.5
    nk = k_scr.shape[0]

    @pl.when(pl.program_id(2) == 0)
    def _():
        if with_lat:
            k_scr[0:seq, 0:LANE] = knl_ref[...]
            k_scr[0:seq, LANE:2 * LANE] = krl_ref[...]
        k_scr[nk - knc_ref.shape[0]:nk, 0:LANE] = knc_ref[...]
        k_scr[nk - knc_ref.shape[0]:nk, LANE:2 * LANE] = krc_ref[...]

    tq = q_ref.shape[0]
    qsub = min(tq, MLA_QSUB)
    nsub = tq // qsub
    vals = ([vl_ref] if with_lat else []) + [vc_ref]

    def qk(s):
        q = q_ref[s * qsub:(s + 1) * qsub, :]
        out = [_dot_nt(q, k_scr[0:seq, :])] if with_lat else []
        return out + [_dot_nt(q, k_scr[nk - knc_ref.shape[0]:nk, :])]

    def softmax(raw):
        m = raw[0].max(axis=-1, keepdims=True)
        for s in raw[1:]:
            m = jnp.maximum(m, s.max(axis=-1, keepdims=True))
        ps = [jnp.exp((s - m) * scale) for s in raw]
        den = sum(jnp.sum(p, axis=-1, keepdims=True) for p in ps)
        return [p.astype(BF16) for p in ps], den

    raw = qk(0)
    for s in range(nsub):
        nxt = qk(s + 1) if s + 1 < nsub else None
        ps, den = softmax(raw)
        acc = sum(_dot(p, v[...]) for p, v in zip(ps, vals))
        y_ref[s * qsub:(s + 1) * qsub, :] = (acc / den).astype(y_ref.dtype)
        raw = nxt


def _mla_attn(q, kn, kr, v, nb, seq, ctx_len, with_lat, tq=2048):
    H = N_HEADS
    cblk0 = nb * seq // ctx_len
    nq = seq if with_lat else ctx_len
    tq = min(tq, nq)
    qblk0 = 0 if with_lat else nb * seq // tq
    ctxs = [pl.BlockSpec((ctx_len, LANE), lambda b, h, i: (cblk0 + b, h)),
            pl.BlockSpec((ctx_len, LANE), lambda b, h, i: (cblk0 + b, 0)),
            pl.BlockSpec((ctx_len, LANE), lambda b, h, i: (cblk0 + b, h))]
    lats = [pl.BlockSpec((seq, LANE), lambda b, h, i: (b, h)),
            pl.BlockSpec((seq, LANE), lambda b, h, i: (b, 0)),
            pl.BlockSpec((seq, LANE), lambda b, h, i: (b, h))]
    in_specs = [pl.BlockSpec((tq, 2 * LANE), lambda b, h, i: (qblk0 + b * (nq // tq) + i, h))]
    args = [q]
    if with_lat:
        in_specs += lats
        args += [kn, kr, v]
    in_specs += ctxs
    args += [kn, kr, v]
    nk = (seq if with_lat else 0) + ctx_len
    return pl.pallas_call(
        functools.partial(_mla_attn_kernel, with_lat, seq),
        grid=(nb, H, nq // tq),
        in_specs=in_specs,
        out_specs=pl.BlockSpec((tq, LANE), lambda b, h, i: (b * (nq // tq) + i, h)),
        out_shape=jax.ShapeDtypeStruct((nb * nq, GROUP_WIDTH), BF16),
        scratch_shapes=[pltpu.VMEM((nk, 2 * LANE), BF16)],
        compiler_params=_cp("parallel", "parallel", "arbitrary"),
        name="mla_attn_lat" if with_lat else "mla_attn_ctx",
    )(*args)


def _prep_w_in(w_in):
    d = w_in.shape[0]
    gw = GROUP_WIDTH
    o_na = 4 * gw + 4 * N_HEADS
    o_mla = o_na + 3 * gw
    o_kr = o_mla + MLA_Q_RANK + MLA_KV_RANK
    o_hg = o_kr + MLA_ROPE
    kr = w_in[:, o_kr:o_hg]
    k1, k2 = kr[:, 0::2], kr[:, 1::2]
    z = lambda n: jnp.zeros((d, n), w_in.dtype)
    o_kv = o_mla + MLA_Q_RANK
    cols = [w_in[:, o_mla:o_kv], k1, k2, z(LANE - MLA_ROPE),
            w_in[:, o_kv:o_kr], k2, k1, z(LANE - MLA_ROPE),
            w_in[:, 4 * gw:o_na], z(LANE - 4 * N_HEADS),
            w_in[:, :4 * gw], w_in[:, o_na:o_mla], w_in[:, o_hg:]]
    w = jnp.concatenate(cols, axis=1).astype(BF16)
    assert w.shape[1] == NP_IN
    return w


def _prep_w_uq(w_uq):
    r = w_uq.shape[0]
    z = jnp.zeros((r, LANE - MLA_ROPE), w_uq.dtype)
    cols = []
    for h in range(N_HEADS):
        wh = w_uq[:, h * MLA_QK_DIM:(h + 1) * MLA_QK_DIM]
        rope = wh[:, MLA_NOPE:]
        r1, r2 = rope[:, 0::2], rope[:, 1::2]
        cols += [wh[:, :MLA_NOPE], r1, r2, z, r2, r1, z]
    return jnp.concatenate(cols, axis=1).astype(BF16)


def _rope_tables(seq, tm):
    n_freq = MLA_ROPE // 4
    freqs = ROPE_BASE ** (-jnp.arange(n_freq, dtype=F32) / n_freq)
    t = jnp.arange(seq)
    ang = jnp.concatenate([(t // GRID_W).astype(F32)[:, None] * freqs,
                           (t % GRID_W).astype(F32)[:, None] * freqs], -1)
    cos, sin = jnp.cos(ang), jnp.sin(ang)
    zp = jnp.zeros((seq, LANE - MLA_ROPE), F32)
    cc = jnp.concatenate([cos, cos, zp], axis=1)
    ss = jnp.concatenate([-sin, sin, zp], axis=1)
    ident = jnp.zeros((tm, LANE), F32).at[:, :MLA_ROPE].set(1.0)
    return jnp.concatenate([cc, ident], axis=0), jnp.concatenate([ss, jnp.zeros((tm, LANE), F32)], axis=0)


def kernel(x, c, ctx, c_ctx, w_ada, b_ada, w_in, gdn_conv_w, gdn_a_log, gdn_dt_bias, gdn_norm_w, na_rpb,
           mla_q_norm_w, mla_kv_norm_w, mla_w_uq, mla_w_uk, mla_w_uv, hgrn_lower_bounds, hgrn_norm_w, w_out,
           ln1_w, ln1_b, w_mlp1, w_mlp2, ln2_w, ln2_b):
    nb, seq, d = x.shape
    ctx_len = ctx.shape[1]
    depth = w_ada.shape[0]
    n_lat, n_ctx = nb * seq, nb * ctx_len
    alpha = (2 * depth) ** 0.25
    tm = 512
    tmm = 1024 if (seq % 1024 == 0 and n_ctx % 1024 == 0) else tm
    assert nb < 8 and seq % tm == 0 and n_ctx % tm == 0 and seq % ctx_len == 0 and ctx_len % CHUNK == 0

    cin = jnp.zeros((8, d), F32).at[:nb].set(c).at[nb].set(c_ctx)
    ada = _ada(cin, w_ada, b_ada)
    p_lb = jax.nn.softmax(hgrn_lower_bounds.astype(F32), axis=0)
    lbs = jnp.cumsum(p_lb, axis=0) - p_lb[0]
    cc, ss = _rope_tables(seq, tm)
    na_bias = _na_bias_tables(na_rpb)

    x_all = jnp.concatenate([x.reshape(n_lat, d), ctx.reshape(n_ctx, d)], axis=0)
    for l in range(depth):
        emit_ctx = l < depth - 1
        ada_r = ada[l].reshape(8 * 6, 1, d)
        z = _inproj(x_all, ada_r, _prep_w_in(w_in[l]), n_lat, seq, nb, tm=tmm)

        qkv = _gdn_conv(z, gdn_conv_w[l], nb, seq, ctx_len)
        p = _gdn_gates(z, gdn_a_log[l], gdn_dt_bias[l], tm=tm)
        ct = p[:, :2 * N_HEADS].reshape(-1, CHUNK, 2 * N_HEADS).transpose(0, 2, 1)
        o_f, o_b = _gdn_scan(*_gdn_prep(qkv, p, ct, nb, seq, ctx_len), nb, seq, ctx_len)
        m_out = n_lat + n_ctx if emit_ctx else n_lat
        ya = _combine(o_f, o_b, z, CB_GGATE, gdn_norm_w[l], m_out, tm=tm)
        yb = _na(z, na_bias[l], nb, seq, ctx_len, emit_ctx)
        q, kn, kr, v = _mla_prep(z, cc, ss, mla_q_norm_w[l], mla_kv_norm_w[l], _prep_w_uq(mla_w_uq[l]),
                                 jnp.concatenate([mla_w_uk[l], mla_w_uv[l]], axis=1).astype(BF16),
                                 n_lat, seq, tm=tm)
        ym = [_mla_attn(q, kn, kr, v, nb, seq, ctx_len, True)]
        if emit_ctx:
            ym.append(_mla_attn(q, kn, kr, v, nb, seq, ctx_len, False))
        yh = _combine(*_gla_scan(z, lbs[l], nb, seq, ctx_len), z, CB_HG, hgrn_norm_w[l], m_out, tm=tm)

        if emit_ctx:
            ys = [ya] + [jnp.concatenate(list(t), axis=0) for t in (yb, ym)] + [yh]
        else:
            ys = [ya, yb[0], ym[0], yh]
        x_all = _outproj(x_all, ys, w_out[l].astype(BF16), ada_r, ln1_w[l], ln1_b[l], m_out, seq, nb, alpha, tm=tm)
        x_all = _mlp(x_all, w_mlp1[l].astype(BF16), w_mlp2[l].astype(BF16), ada_r, ln2_w[l], ln2_b[l],
                     seq, nb, alpha, tm=tm)
    return x_all[:n_lat].reshape(nb, seq, d)
```

```python
import functools

import numpy as np
import jax
import jax.numpy as jnp
from jax import lax
from jax.experimental import pallas as pl
from jax.experimental.pallas import tpu as pltpu

F32 = jnp.float32
BF16 = jnp.bfloat16
HIGHEST = lax.Precision.HIGHEST

GRID_W = 64
N_HEADS = 4
HEAD_DIM = 128
GROUP_WIDTH = 512
CHUNK = 64
SUB = 8
GDN_CONV = 5
NA_ROWS = 8
NA_COLS = 16
NA_UNROLL = 4
MLA_Q_RANK = 384
MLA_KV_RANK = 256
MLA_NOPE = 128
MLA_ROPE = 64
MLA_QK_DIM = MLA_NOPE + MLA_ROPE
MLA_QSUB = 512
ROPE_BASE = 10000.0
LN_EPS = 1e-5
RMS_EPS = 1e-6
NEG = -1e30

LANE = 128
CB_MQ, CB_MKRA, CB_MKV, CB_MKRB, CB_GAB = 0, 3, 4, 6, 7
CB_GQKV, CB_GGATE = 8, 20
CB_NAQ, CB_NAK, CB_NAV = 24, 28, 32
CB_HQ, CB_HFF, CB_HFB, CB_HI, CB_HG = 36, 40, 44, 48, 52
CHUNK_SHIFT = 6
PREP_CHUNKS = 4
NP_IN = 56 * LANE

VMEM_LIMIT = 48 << 20


def _cp(*sem):
    return pltpu.CompilerParams(dimension_semantics=sem, vmem_limit_bytes=VMEM_LIMIT)


def _silu(x):
    return x * jax.nn.sigmoid(x)


def _dot(a, b, **kw):
    return jnp.dot(a, b, preferred_element_type=F32, **kw)


def _dot_nt(a, b, **kw):
    return lax.dot_general(a, b, (((1,), (1,)), ((), ())), preferred_element_type=F32, **kw)


def _dot_tn(a, b, **kw):
    return lax.dot_general(a, b, (((0,), (0,)), ((), ())), preferred_element_type=F32, **kw)


def _ada_kernel(c_ref, w_ref, b_ref, o_ref):
    s = _silu(c_ref[...])
    o_ref[0] = _dot(s, w_ref[0], precision=HIGHEST) + b_ref[0]


def _ada(cin, w_ada, b_ada):
    depth, d, n = w_ada.shape
    tn = 512
    return pl.pallas_call(
        _ada_kernel,
        grid=(depth, n // tn),
        in_specs=[pl.BlockSpec((8, d), lambda l, j: (0, 0)),
                  pl.BlockSpec((1, d, tn), lambda l, j: (l, 0, j)),
                  pl.BlockSpec((1, 1, tn), lambda l, j: (l, 0, j))],
        out_specs=pl.BlockSpec((1, 8, tn), lambda l, j: (l, 0, j)),
        out_shape=jax.ShapeDtypeStruct((depth, 8, n), F32),
        compiler_params=_cp("parallel", "parallel"),
        name="ada",
    )(cin, w_ada, b_ada.reshape(depth, 1, n))


ROW_STEP = 256


def _modulate(x_ref, sh_ref, sc_ref, xm_ref):
    sc1, sh = 1.0 + sc_ref[0], sh_ref[0]

    def body(t, carry):
        sl = pl.ds(pl.multiple_of(t * ROW_STEP, ROW_STEP), ROW_STEP)
        xm_ref[sl, :] = (x_ref[sl, :] * sc1 + sh).astype(BF16)
        return carry

    lax.fori_loop(0, x_ref.shape[0] // ROW_STEP, body, 0)


def _inproj_kernel(n_lat_tiles, *refs):
    xs, (sh_ref, sc_ref, w_ref, o_ref, xm_ref) = refs[:-5], refs[-5:]
    first = pl.program_id(1) == 0
    if len(xs) == 1:
        pl.when(first)(lambda: _modulate(xs[0], sh_ref, sc_ref, xm_ref))
    else:
        is_lat = pl.program_id(0) < n_lat_tiles
        pl.when(first & is_lat)(lambda: _modulate(xs[0], sh_ref, sc_ref, xm_ref))
        pl.when(first & jnp.logical_not(is_lat))(lambda: _modulate(xs[1], sh_ref, sc_ref, xm_ref))
    o_ref[...] = _dot(xm_ref[...], w_ref[...])


def _inproj(xs, ada_r, w, m, n_lat, seq, nb, tm=512, tn=1024):
    d, n = w.shape
    nlt = n_lat // tm
    row = lambda i: jnp.minimum((i * tm) // seq, nb)
    if len(xs) == 1:
        x_specs = [pl.BlockSpec((tm, d), lambda i, j: (i, 0))]
    else:
        x_specs = [pl.BlockSpec((tm, d), lambda i, j: (jnp.minimum(i, nlt - 1), 0)),
                   pl.BlockSpec((tm, d), lambda i, j: (jnp.maximum(i - nlt, 0), 0),
                                pipeline_mode=pl.Buffered(1))]
    return pl.pallas_call(
        functools.partial(_inproj_kernel, nlt),
        grid=(m // tm, n // tn),
        in_specs=x_specs + [
                  pl.BlockSpec((1, 1, d), lambda i, j: (row(i) * 6 + 0, 0, 0)),
                  pl.BlockSpec((1, 1, d), lambda i, j: (row(i) * 6 + 1, 0, 0)),
                  pl.BlockSpec((d, tn), lambda i, j: (0, j))],
        out_specs=pl.BlockSpec((tm, tn), lambda i, j: (i, j)),
        out_shape=jax.ShapeDtypeStruct((m, n), F32),
        scratch_shapes=[pltpu.VMEM((tm, d), BF16)],
        compiler_params=_cp("parallel", "arbitrary"),
        name="inproj",
    )(*xs, ada_r, ada_r, w)


def _layernorm(r, w, b):
    mu = jnp.mean(r, axis=-1, keepdims=True)
    rc = r - mu
    var = jnp.mean(rc * rc, axis=-1, keepdims=True)
    return rc * lax.rsqrt(var + LN_EPS) * w + b


def _head_norm_gate(o, nw, gate):
    o = o * lax.rsqrt(jnp.mean(o * o, axis=-1, keepdims=True) + RMS_EPS) * nw
    return o * _silu(gate)


def _scan_mixer_out(of_ref, ob_ref, gate_ref, nw_ref):
    nw = nw_ref[...]
    heads = []
    for h in range(N_HEADS):
        hs = slice(h * LANE, (h + 1) * LANE)
        heads.append(_head_norm_gate(of_ref[:, hs] + ob_ref[:, hs], nw, gate_ref[:, hs]).astype(BF16))
    return jnp.concatenate(heads, axis=-1)


def _outproj_kernel(alpha, n_lat_tiles, n_split, *refs):
    pairs, rest = refs[:n_split], refs[n_split:]
    (gaf_ref, gab_ref, gag_ref, hgf_ref, hgb_ref, hgg_ref, nwa_ref, nwh_ref,
     w_ref, g_ref, lw_ref, lb_ref, o_ref) = rest
    gw = GROUP_WIDTH

    def run(x_ref, yb_ref, ym_ref):
        acc = _dot(_scan_mixer_out(gaf_ref, gab_ref, gag_ref, nwa_ref), w_ref[0:gw, :])
        acc += _dot(yb_ref[...], w_ref[gw:2 * gw, :])
        acc += _dot(ym_ref[...], w_ref[2 * gw:3 * gw, :])
        acc += _dot(_scan_mixer_out(hgf_ref, hgb_ref, hgg_ref, nwh_ref), w_ref[3 * gw:4 * gw, :])
        r = alpha * x_ref[...] + g_ref[0] * acc
        o_ref[...] = _layernorm(r, lw_ref[...], lb_ref[...])

    if n_split == 3:
        run(*pairs)
    else:
        is_lat = pl.program_id(0) < n_lat_tiles
        pl.when(is_lat)(lambda: run(*pairs[0::2]))
        pl.when(jnp.logical_not(is_lat))(lambda: run(*pairs[1::2]))


def _outproj(xs, gdn_o, ybs, yms, hgrn_o, z, nwa, nwh, w, ada_r, lw, lb, m_out, n_lat, seq, nb, alpha, tm=256):
    d = w.shape[0]
    gw = GROUP_WIDTH
    nlt = n_lat // tm
    row = lambda i: jnp.minimum((i * tm) // seq, nb)
    in_specs, args = [], []
    dual = len(xs) == 2
    for arrs, width in ((xs, d), (ybs, gw), (yms, gw)):
        if dual:
            in_specs += [pl.BlockSpec((tm, width), lambda i: (jnp.minimum(i, nlt - 1), 0)),
                         pl.BlockSpec((tm, width), lambda i: (jnp.maximum(i - nlt, 0), 0))]
        else:
            in_specs += [pl.BlockSpec((tm, width), lambda i: (i, 0))]
        args += list(arrs)
    n_split = len(args)
    rowspec = pl.BlockSpec((tm, gw), lambda i: (i, 0))
    gate = lambda cb: pl.BlockSpec((tm, gw), lambda i: (i, cb * LANE // gw))
    vec = lambda n: pl.BlockSpec((1, n), lambda i: (0, 0))
    in_specs += [rowspec, rowspec, gate(CB_GGATE), rowspec, rowspec, gate(CB_HG), vec(LANE), vec(LANE),
                 pl.BlockSpec((d, d), lambda i: (0, 0), pipeline_mode=pl.Buffered(1)),
                 pl.BlockSpec((1, 1, d), lambda i: (row(i) * 6 + 2, 0, 0)), vec(d), vec(d)]
    args += [*gdn_o, z, *hgrn_o, z, nwa.reshape(1, LANE), nwh.reshape(1, LANE), w, ada_r,
             lw.reshape(1, d), lb.reshape(1, d)]
    return pl.pallas_call(
        functools.partial(_outproj_kernel, alpha, nlt, n_split),
        grid=(m_out // tm,),
        in_specs=in_specs,
        out_specs=pl.BlockSpec((tm, d), lambda i: (i, 0)),
        out_shape=jax.ShapeDtypeStruct((m_out, d), F32),
        compiler_params=_cp("parallel"),
        name="outproj_ln",
    )(*args)


def _mlp_kernel(alpha, x_ref, sh_ref, sc_ref, g_ref, w1_ref, w2_ref, lw_ref, lb_ref, o_ref, xm_ref, acc_ref):
    k = pl.program_id(1)

    @pl.when(k == 0)
    def _():
        _modulate(x_ref, sh_ref, sc_ref, xm_ref)
        acc_ref[...] = jnp.zeros_like(acc_ref)

    h = jnp.maximum(_dot(xm_ref[...], w1_ref[...]), 0.0)
    acc_ref[...] += _dot((h * h).astype(BF16), w2_ref[...])

    @pl.when(k == pl.num_programs(1) - 1)
    def _():
        g, lw, lb = g_ref[0], lw_ref[...], lb_ref[...]

        def body(t, carry):
            sl = pl.ds(pl.multiple_of(t * ROW_STEP, ROW_STEP), ROW_STEP)
            o_ref[sl, :] = _layernorm(alpha * x_ref[sl, :] + g * acc_ref[sl, :], lw, lb)
            return carry

        lax.fori_loop(0, x_ref.shape[0] // ROW_STEP, body, 0)


def _mlp(x_all, w1, w2, ada_r, lw, lb, seq, nb, alpha, tm=512, th=1024):
    m, d = x_all.shape
    hid = w1.shape[1]
    row = lambda i: jnp.minimum((i * tm) // seq, nb)
    return pl.pallas_call(
        functools.partial(_mlp_kernel, alpha),
        grid=(m // tm, hid // th),
        in_specs=[pl.BlockSpec((tm, d), lambda i, k: (i, 0)),
                  pl.BlockSpec((1, 1, d), lambda i, k: (row(i) * 6 + 3, 0, 0)),
                  pl.BlockSpec((1, 1, d), lambda i, k: (row(i) * 6 + 4, 0, 0)),
                  pl.BlockSpec((1, 1, d), lambda i, k: (row(i) * 6 + 5, 0, 0)),
                  pl.BlockSpec((d, th), lambda i, k: (0, k)),
                  pl.BlockSpec((th, d), lambda i, k: (k, 0)),
                  pl.BlockSpec((1, d), lambda i, k: (0, 0)),
                  pl.BlockSpec((1, d), lambda i, k: (0, 0))],
        out_specs=pl.BlockSpec((tm, d), lambda i, k: (i, 0)),
        out_shape=jax.ShapeDtypeStruct((m, d), F32),
        scratch_shapes=[pltpu.VMEM((tm, d), BF16), pltpu.VMEM((tm, d), F32)],
        compiler_params=_cp("parallel", "arbitrary"),
        name="mlp_ln",
    )(x_all, ada_r, ada_r, ada_r, w1, w2, lw.reshape(1, d), lb.reshape(1, d))


def _gdn_conv_kernel(seq, ctx_len, xl_ref, xc_ref, w_ref, o_ref, pad_ref):
    j = pl.program_id(1)
    w = w_ref[...]
    qscale = jnp.where(j < N_HEADS, HEAD_DIM ** -0.5, 1.0).astype(F32)
    p0 = 8 - GDN_CONV // 2
    for x_ref, nrows, o0 in ((xl_ref, seq, 0), (xc_ref, ctx_len, seq)):
        pad_ref[0:8, :] = jnp.zeros((8, LANE), F32)
        pad_ref[nrows + 8:nrows + 16, :] = jnp.zeros((8, LANE), F32)
        pad_ref[8:nrows + 8, :] = x_ref[...]
        rb = min(nrows, 256)
        for r0 in range(0, nrows, rb):
            y = pad_ref[r0 + p0:r0 + p0 + rb, :] * w[0:1, :]
            for i in range(1, GDN_CONV):
                y = y + pad_ref[r0 + p0 + i:r0 + p0 + i + rb, :] * w[i:i + 1, :]
            y = _silu(y)
            nrm = y * lax.rsqrt(jnp.sum(y * y, axis=-1, keepdims=True) + RMS_EPS) * qscale
            o_ref[o0 + r0:o0 + r0 + rb, :] = jnp.where(j < 2 * N_HEADS, nrm, y)


def _gdn_conv(z, conv_w, nb, seq, ctx_len):
    nblk = 3 * N_HEADS
    cblk0 = nb * seq // ctx_len
    return pl.pallas_call(
        functools.partial(_gdn_conv_kernel, seq, ctx_len),
        grid=(nb, nblk),
        in_specs=[pl.BlockSpec((seq, LANE), lambda b, j: (b, CB_GQKV + j)),
                  pl.BlockSpec((ctx_len, LANE), lambda b, j: (cblk0 + b, CB_GQKV + j)),
                  pl.BlockSpec((GDN_CONV, LANE), lambda b, j: (0, j))],
        out_specs=pl.BlockSpec((seq + ctx_len, LANE), lambda b, j: (b, j)),
        out_shape=jax.ShapeDtypeStruct((nb * (seq + ctx_len), nblk * LANE), F32),
        scratch_shapes=[pltpu.VMEM((seq + 16, LANE), F32)],
        compiler_params=_cp("parallel", "parallel"),
        name="gdn_conv",
    )(z, z, conv_w)


def _gdn_gates_kernel(tm, s_ref, alog_ref, dtb_ref, o_ref):
    s = s_ref[...]
    g = -jnp.exp(alog_ref[...]) * (jnp.maximum(s + dtb_ref[...], 0.0)
                                    + jnp.log1p(jnp.exp(-jnp.abs(s + dtb_ref[...]))))
    r = lax.broadcasted_iota(jnp.int32, (tm, tm), 0)
    c = lax.broadcasted_iota(jnp.int32, (tm, tm), 1)
    same = (r >> CHUNK_SHIFT) == (c >> CHUNK_SHIFT)
    lo = jnp.where(same & (c <= r), 1.0, 0.0).astype(F32)
    up = jnp.where(same & (c >= r), 1.0, 0.0).astype(F32)
    cum_f = _dot(lo, g, precision=HIGHEST)
    cum_b = _dot(up, g, precision=HIGHEST)
    col = lax.broadcasted_iota(jnp.int32, s.shape, 1)
    o_ref[...] = jnp.where(col < N_HEADS, cum_f,
                           jnp.where(col < 2 * N_HEADS, cum_b,
                                     jnp.where(col < 4 * N_HEADS, jax.nn.sigmoid(s), 0.0)))


def _gdn_gates(z, a_log, dt_bias, tm=512):
    m = z.shape[0]
    pad = lambda v: jnp.zeros((1, LANE), F32).at[0, :2 * N_HEADS].set(v.reshape(-1).astype(F32))
    return pl.pallas_call(
        functools.partial(_gdn_gates_kernel, tm),
        grid=(m // tm,),
        in_specs=[pl.BlockSpec((tm, LANE), lambda i: (i, CB_GAB)),
                  pl.BlockSpec((1, LANE), lambda i: (0, 0)),
                  pl.BlockSpec((1, LANE), lambda i: (0, 0))],
        out_specs=pl.BlockSpec((tm, LANE), lambda i: (i, 0)),
        out_shape=jax.ShapeDtypeStruct((m, LANE), F32),
        compiler_params=_cp("parallel"),
        name="gdn_gates",
    )(z, pad(a_log), pad(dt_bias))


def _tri_masks(n):
    r = lax.broadcasted_iota(jnp.int32, (n, n), 0)
    c = lax.broadcasted_iota(jnp.int32, (n, n), 1)
    return r, c


def _split(x):
    hi = x.astype(BF16)
    return hi, (x - hi.astype(F32)).astype(BF16)


def _dot3(a, b):
    return _dot(a[0], b[0]) + (_dot(a[0], b[1]) + _dot(a[1], b[0]))


def _gdn_prep_kernel(qkv_ref, p_ref, ct_ref, u_ref, w_ref, qt_ref, kt_ref, att_ref, el_ref):
    C, H = CHUNK, N_HEADS
    lane = lax.broadcasted_iota(jnp.int32, (C, LANE), 1)
    sub8 = lax.broadcasted_iota(jnp.int32, (2 * H, C), 0)
    r, c = _tri_masks(C)
    eye = jnp.where(r == c, 1.0, 0.0).astype(F32)
    a_list, rhs_list, where_list = [], [], []
    for n in range(PREP_CHUNKS):
        rs = slice(n * C, (n + 1) * C)
        pblk = p_ref[rs, :]
        tblk = ct_ref[n]
        col = lambda idx, pblk=pblk: jnp.sum(jnp.where(lane == idx, pblk, 0.0), axis=-1, keepdims=True)
        row = lambda idx, tblk=tblk: jnp.sum(jnp.where(sub8 == idx, tblk, 0.0), axis=0, keepdims=True)
        for h in range(H):
            hs = slice(h * LANE, (h + 1) * LANE)
            q = qkv_ref[rs, h * LANE:(h + 1) * LANE]
            k = qkv_ref[rs, (H + h) * LANE:(H + h + 1) * LANE]
            v = qkv_ref[rs, (2 * H + h) * LANE:(2 * H + h + 1) * LANE]
            qbf, kbf = q.astype(BF16), k.astype(BF16)
            for d in range(2):
                idx = d * H + h
                cum_c, cum_r, beta_c = col(idx), row(idx), col(2 * H + idx)
                incl = (c >= r) if d else (c <= r)
                strict = (c > r) if d else (c < r)
                last = cum_r[:, 0:1] if d else cum_r[:, C - 1:C]
                decay = jnp.exp(jnp.where(incl, cum_c - cum_r, NEG))
                kb = k * beta_c
                ec = jnp.exp(cum_c)
                a_list.append(jnp.where(strict, _dot_nt(kb.astype(BF16), kbf) * decay, 0.0))
                rhs_list.append(_split(jnp.concatenate([v * beta_c, kb * ec], axis=-1)))
                where_list.append((d, rs, hs))
                att_ref[d, h, rs, :] = jnp.where(incl, _dot_nt(qbf, kbf) * decay, 0.0).astype(BF16)
                qt_ref[d, rs, hs] = (q * ec).astype(BF16)
                kt_ref[d, rs, hs] = (k * jnp.exp(last - cum_c)).astype(BF16)
                el_ref[n, idx:idx + 1, :] = jnp.broadcast_to(jnp.exp(last), (1, LANE))
    ts = [eye - a for a in a_list]
    ps = [a.astype(BF16) for a in a_list]
    for _ in range(5):
        ps = [_dot(p, p).astype(BF16) for p in ps]
        ts = [t + _dot(t.astype(BF16), p) for t, p in zip(ts, ps)]
    res = [eye - t - _dot3(_split(a), _split(t)) for a, t in zip(a_list, ts)]
    ts = [t + _dot(t.astype(BF16), e.astype(BF16)) for t, e in zip(ts, res)]
    sols = [_dot3(_split(t), rhs) for t, rhs in zip(ts, rhs_list)]
    for sol, (d, rs, hs) in zip(sols, where_list):
        u_ref[d, rs, hs] = sol[:, :HEAD_DIM]
        w_ref[d, rs, hs] = sol[:, HEAD_DIM:].astype(BF16)


def _gdn_prep(qkv, p, ct, nb, seq, ctx_len):
    m = qkv.shape[0]
    H = N_HEADS
    gw = GROUP_WIDTH
    rows = PREP_CHUNKS * CHUNK
    assert seq % rows == 0 and ctx_len % rows == 0
    dspec = pl.BlockSpec((2, rows, gw), lambda i: (0, i, 0))
    nl, nc = seq // rows, ctx_len // rows

    def qkv_blk(i):
        ic = i - nb * nl
        return jnp.where(i < nb * nl, (i // nl) * (nl + nc) + i % nl, (ic // nc) * (nl + nc) + nl + ic % nc)

    return pl.pallas_call(
        _gdn_prep_kernel,
        grid=(m // rows,),
        in_specs=[pl.BlockSpec((rows, 3 * gw), lambda i: (qkv_blk(i), 0)),
                  pl.BlockSpec((rows, LANE), lambda i: (i, 0)),
                  pl.BlockSpec((PREP_CHUNKS, 2 * H, CHUNK), lambda i: (i, 0, 0))],
        out_specs=[dspec, dspec, dspec, dspec,
                   pl.BlockSpec((2, H, rows, CHUNK), lambda i: (0, 0, i, 0)),
                   pl.BlockSpec((PREP_CHUNKS, 2 * H, LANE), lambda i: (i, 0, 0))],
        out_shape=[jax.ShapeDtypeStruct((2, m, gw), F32),
                   jax.ShapeDtypeStruct((2, m, gw), BF16),
                   jax.ShapeDtypeStruct((2, m, gw), BF16),
                   jax.ShapeDtypeStruct((2, m, gw), BF16),
                   jax.ShapeDtypeStruct((2, H, m, CHUNK), BF16),
                   jax.ShapeDtypeStruct((m // CHUNK, 2 * H, LANE), F32)],
        compiler_params=_cp("parallel"),
        name="gdn_prep",
    )(qkv, p, ct)


def _gdn_scan_kernel(g, *refs):
    ins, (of_ref, ob_ref, s_ref) = refs[:12], refs[12:]
    H = N_HEADS

    @pl.when(pl.program_id(1) == 0)
    def _():
        s_ref[...] = jnp.zeros_like(s_ref)

    chains = [(d, h) for d in range(2) for h in range(H)]
    for j in range(g):
        mid = []
        for d, h in chains:
            u_ref, w_ref, qt_ref = ins[d:6:2]
            cj = g - 1 - j if d else j
            rs, hs = slice(cj * CHUNK, (cj + 1) * CHUNK), slice(h * LANE, (h + 1) * LANE)
            sb = s_ref[d * H + h].astype(BF16)
            vnb = (u_ref[0, rs, hs] - _dot(w_ref[0, rs, hs], sb)).astype(BF16)
            mid.append((vnb, _dot(qt_ref[0, rs, hs], sb)))
        for (d, h), (vnb, o_state) in zip(chains, mid):
            kt_ref, att_ref, el_ref = ins[6 + d::2]
            o_ref = ob_ref if d else of_ref
            cj = g - 1 - j if d else j
            rs, hs = slice(cj * CHUNK, (cj + 1) * CHUNK), slice(h * LANE, (h + 1) * LANE)
            idx = d * H + h
            s_ref[idx] = s_ref[idx] * el_ref[cj, idx:idx + 1, :] + _dot_tn(kt_ref[0, rs, hs], vnb)
            o_ref[rs, hs] = o_state + _dot(att_ref[0, h, rs, :], vnb)


def _gdn_scan(u, w, qt, kt, att, el, nb, seq, ctx_len):
    m = u.shape[1]
    H = N_HEADS
    gw = GROUP_WIDTH
    g = ctx_len // CHUNK
    nblk = seq // ctx_len
    cblk0 = nb * nblk
    blk_f = lambda b, t: jnp.where(t == 0, cblk0 + b, b * nblk + t - 1)
    blk_b = lambda b, t: jnp.where(t == 0, cblk0 + b, b * nblk + nblk - t)
    in_specs, args = [], []
    for arr in (u, w, qt, kt):
        in_specs += [pl.BlockSpec((1, ctx_len, gw), lambda b, t: (0, blk_f(b, t), 0)),
                     pl.BlockSpec((1, ctx_len, gw), lambda b, t: (1, blk_b(b, t), 0))]
        args += [arr, arr]
    in_specs += [pl.BlockSpec((1, H, ctx_len, CHUNK), lambda b, t: (0, 0, blk_f(b, t), 0)),
                 pl.BlockSpec((1, H, ctx_len, CHUNK), lambda b, t: (1, 0, blk_b(b, t), 0)),
                 pl.BlockSpec((g, 2 * H, LANE), lambda b, t: (blk_f(b, t), 0, 0)),
                 pl.BlockSpec((g, 2 * H, LANE), lambda b, t: (blk_b(b, t), 0, 0))]
    args += [att, att, el, el]
    return pl.pallas_call(
        functools.partial(_gdn_scan_kernel, g),
        grid=(nb, nblk + 1),
        in_specs=in_specs,
        out_specs=[pl.BlockSpec((ctx_len, gw), lambda b, t: (blk_f(b, t), 0)),
                   pl.BlockSpec((ctx_len, gw), lambda b, t: (blk_b(b, t), 0))],
        out_shape=[jax.ShapeDtypeStruct((m, gw), F32), jax.ShapeDtypeStruct((m, gw), F32)],
        scratch_shapes=[pltpu.VMEM((2 * H, HEAD_DIM, HEAD_DIM), F32)],
        compiler_params=_cp("parallel", "arbitrary"),
        name="gdn_scan",
    )(*args)


def _gla_kernel(g, qf_ref, ff_ref, if_ref, qb_ref, fb_ref, ib_ref, lb_ref, of_ref, ob_ref, s_ref):
    C, H, nsub = CHUNK, N_HEADS, CHUNK // SUB

    @pl.when(pl.program_id(1) == 0)
    def _():
        s_ref[...] = jnp.zeros_like(s_ref)

    r, c = _tri_masks(C)
    tri = (jnp.where(c <= r, 1.0, 0.0).astype(F32), jnp.where(c >= r, 1.0, 0.0).astype(F32))
    trow = lax.broadcasted_iota(jnp.int32, (C, HEAD_DIM), 0)
    srow = lax.broadcasted_iota(jnp.int32, (SUB, C), 0)
    scol = lax.broadcasted_iota(jnp.int32, (SUB, C), 1)
    chains = [(d, h) for d in range(2) for h in range(H)]
    srcs = ((qf_ref, ff_ref, if_ref, of_ref), (qb_ref, fb_ref, ib_ref, ob_ref))

    def body(j, carry):
        ph1 = []
        for d, h in chains:
            cj = g - 1 - j if d else j
            rs, hs = pl.ds(pl.multiple_of(cj * C, C), C), slice(h * LANE, (h + 1) * LANE)
            lb = lb_ref[:, hs]
            f = lb + (1.0 - lb) * jax.nn.sigmoid(srcs[d][1][rs, hs])
            ph1.append((rs, hs, f, _dot(tri[d], jnp.log(f), precision=HIGHEST)))
        ph2 = []
        for (d, h), (rs, hs, f, cum) in zip(chains, ph1):
            q_ref, _, i_ref, _ = srcs[d]
            q = _silu(q_ref[rs, hs]) * HEAD_DIM ** -0.5
            k = 1.0 - f
            vb = i_ref[rs, hs].astype(BF16)
            last = cum[0:1, :] if d else cum[C - 1:C, :]
            idx = d * H + h
            St = s_ref[idx]
            o_state = _dot_nt((q * jnp.exp(cum)).astype(BF16), St.astype(BF16))
            s_ref[idx] = St * jnp.exp(last) + _dot_tn(vb, (k * jnp.exp(last - cum)).astype(BF16))
            inter = []
            for a in range(nsub):
                sa = slice(a * SUB, (a + 1) * SUB)
                if d and a < nsub - 1:
                    cb = cum[(a + 1) * SUB:(a + 1) * SUB + 1, :]
                    kt = k * jnp.exp(jnp.where(trow >= (a + 1) * SUB, cb - cum, NEG))
                elif (not d) and a > 0:
                    cb = cum[a * SUB - 1:a * SUB, :]
                    kt = k * jnp.exp(jnp.where(trow < a * SUB, cb - cum, NEG))
                else:
                    inter.append(jnp.zeros((SUB, C), F32))
                    continue
                inter.append(_dot_nt((q[sa] * jnp.exp(cum[sa] - cb)).astype(BF16), kt.astype(BF16)))
            ph2.append((q, k, cum, vb, o_state, inter))
        for (d, h), (rs, hs, _, _), (q, k, cum, vb, o_state, inter) in zip(chains, ph1, ph2):
            blocks = []
            for a in range(nsub):
                sa = slice(a * SUB, (a + 1) * SUB)
                qa, ka, ca, sc = q[sa], k[sa], cum[sa], inter[a]
                for t in range(SUB):
                    dec = jnp.exp(ca - ca[t:t + 1, :])
                    st = jnp.sum(qa * ka[t:t + 1, :] * dec, axis=-1, keepdims=True)
                    ok = (srow <= t) if d else (srow >= t)
                    sc = jnp.where((scol == a * SUB + t) & ok, st, sc)
                blocks.append(sc)
            scores = jnp.concatenate(blocks, axis=0)
            srcs[d][3][rs, hs] = o_state + _dot(scores.astype(BF16), vb)
        return carry

    lax.fori_loop(0, g, body, 0)


def _gla_scan(z, lbs, nb, seq, ctx_len):
    m = z.shape[0]
    gw = GROUP_WIDTH
    g = ctx_len // CHUNK
    nblk = seq // ctx_len
    cblk0 = nb * nblk
    blk_f = lambda b, t: jnp.where(t == 0, cblk0 + b, b * nblk + t - 1)
    blk_b = lambda b, t: jnp.where(t == 0, cblk0 + b, b * nblk + nblk - t)
    col = lambda cb: cb * LANE // gw
    spec = lambda blk, cb: pl.BlockSpec((ctx_len, gw), lambda b, t: (blk(b, t), col(cb)))
    return pl.pallas_call(
        functools.partial(_gla_kernel, g),
        grid=(nb, nblk + 1),
        in_specs=[spec(blk_f, CB_HQ), spec(blk_f, CB_HFF), spec(blk_f, CB_HI),
                  spec(blk_b, CB_HQ), spec(blk_b, CB_HFB), spec(blk_b, CB_HI),
                  pl.BlockSpec((1, gw), lambda b, t: (0, 0))],
        out_specs=[pl.BlockSpec((ctx_len, gw), lambda b, t: (blk_f(b, t), 0)),
                   pl.BlockSpec((ctx_len, gw), lambda b, t: (blk_b(b, t), 0))],
        out_shape=[jax.ShapeDtypeStruct((m, gw), F32), jax.ShapeDtypeStruct((m, gw), F32)],
        scratch_shapes=[pltpu.VMEM((2 * N_HEADS, HEAD_DIM, HEAD_DIM), F32)],
        compiler_params=_cp("parallel", "arbitrary"),
        name="hgrn_scan",
    )(z, z, z, z, z, z, lbs.reshape(1, gw))


def _softmax_pv(parts):
    m = parts[0][0].max(axis=-1, keepdims=True)
    for s, _ in parts[1:]:
        m = jnp.maximum(m, s.max(axis=-1, keepdims=True))
    den, acc = None, None
    for s, v in parts:
        p = jnp.exp(s - m)
        d = jnp.sum(p, axis=-1, keepdims=True)
        a = _dot(p.astype(BF16), v)
        den = d if den is None else den + d
        acc = a if acc is None else acc + a
    return acc / den


def _na_kernel(emit_ctx, rows, *refs):
    q_ref, k_ref, v_ref, qc_ref, kc_ref, vc_ref, bias_ref = refs[:7]
    if emit_ctx:
        yl_ref, yc_ref, kb_ref, vb_ref = refs[7:]
    else:
        yl_ref, kb_ref, vb_ref = refs[7:]
    scale = HEAD_DIM ** -0.5
    win = NA_ROWS * GRID_W
    kb_ref[...] = k_ref[...].astype(BF16)
    vb_ref[...] = v_ref[...].astype(BF16)
    kc = kc_ref[...].astype(BF16)
    vc = vc_ref[...].astype(BF16)

    def body(i, carry):
        pre = []
        for t in range(NA_UNROLL):
            r = i * NA_UNROLL + t
            row0 = jnp.clip(r - NA_ROWS // 2, 0, rows - NA_ROWS)
            qs = pl.ds(pl.multiple_of(r * GRID_W, GRID_W), GRID_W)
            ks = pl.ds(pl.multiple_of(row0 * GRID_W, GRID_W), win)
            q = q_ref[qs, :].astype(BF16)
            pre.append((qs, ks, _dot_nt(q, kb_ref[ks, :]) * scale + bias_ref[0, r - row0], _dot_nt(q, kc) * scale))
        mid = []
        for qs, ks, s_win, s_ctx in pre:
            m = jnp.maximum(s_win.max(axis=-1, keepdims=True), s_ctx.max(axis=-1, keepdims=True))
            p_win, p_ctx = jnp.exp(s_win - m), jnp.exp(s_ctx - m)
            den = jnp.sum(p_win, axis=-1, keepdims=True) + jnp.sum(p_ctx, axis=-1, keepdims=True)
            mid.append((qs, ks, p_win.astype(BF16), p_ctx.astype(BF16), den))
        for qs, ks, p_win, p_ctx, den in mid:
            yl_ref[qs, :] = ((_dot(p_win, vb_ref[ks, :]) + _dot(p_ctx, vc)) / den).astype(yl_ref.dtype)
        return carry

    lax.fori_loop(0, rows // NA_UNROLL, body, 0)
    if emit_ctx:
        s = _dot_nt(qc_ref[...].astype(BF16), kc) * scale
        yc_ref[...] = _softmax_pv([(s, vc)]).astype(yc_ref.dtype)


def _na_bias_kernel(rpb_ref, o_ref):
    n = lax.broadcasted_iota(jnp.int32, (LANE, GRID_W * GRID_W), 1)
    j = lax.broadcasted_iota(jnp.int32, (LANE, GRID_W * GRID_W), 0)
    q, w = n >> 6, n & (GRID_W - 1)
    dc = jnp.clip(w - q, 1 - NA_COLS, NA_COLS - 1) + NA_COLS - 1
    onehot = jnp.where(dc == j, 1.0, 0.0).astype(F32)
    m = _dot(rpb_ref[...], onehot, precision=HIGHEST)
    c0 = jnp.clip(q[0:1] - NA_COLS // 2, 0, GRID_W - NA_COLS)
    ok = (w[0:1] >= c0) & (w[0:1] < c0 + NA_COLS)
    o_ref[...] = jnp.where(ok, m, NEG)


def _na_bias_tables(rpb):
    depth, H, nr, nc = rpb.shape
    assert GRID_W == 64 and depth * H * nr <= LANE and nc <= LANE
    flat = jnp.zeros((LANE, LANE), F32).at[:depth * H * nr, :nc].set(rpb.reshape(-1, nc).astype(F32))
    m = pl.pallas_call(
        _na_bias_kernel,
        out_shape=jax.ShapeDtypeStruct((LANE, GRID_W * GRID_W), F32),
        compiler_params=pltpu.CompilerParams(vmem_limit_bytes=VMEM_LIMIT),
        name="na_bias",
    )(flat)
    m = m[:depth * H * nr].reshape(depth, H, nr, GRID_W, GRID_W)
    tab = jnp.stack([jnp.stack([m[:, :, k - s + NA_ROWS - 1] for k in range(NA_ROWS)], axis=3)
                     for s in range(NA_ROWS)], axis=2)
    return tab.reshape(depth, H, NA_ROWS, GRID_W, NA_ROWS * GRID_W)


def _na(z, bias, nb, seq, ctx_len, emit_ctx):
    rows = seq // GRID_W
    cblk0 = nb * seq // ctx_len
    H = N_HEADS
    lat = lambda cb: pl.BlockSpec((seq, LANE), lambda b, h: (b, cb + h))
    ctx = lambda cb: pl.BlockSpec((ctx_len, LANE), lambda b, h: (cblk0 + b, cb + h))
    win = NA_ROWS * GRID_W
    in_specs = [lat(CB_NAQ), lat(CB_NAK), lat(CB_NAV), ctx(CB_NAQ), ctx(CB_NAK), ctx(CB_NAV),
                pl.BlockSpec((1, NA_ROWS, GRID_W, win), lambda b, h: (h, 0, 0, 0))]
    out_specs = [pl.BlockSpec((seq, LANE), lambda b, h: (b, h))]
    out_shape = [jax.ShapeDtypeStruct((nb * seq, GROUP_WIDTH), BF16)]
    if emit_ctx:
        out_specs.append(pl.BlockSpec((ctx_len, LANE), lambda b, h: (b, h)))
        out_shape.append(jax.ShapeDtypeStruct((nb * ctx_len, GROUP_WIDTH), BF16))
    return pl.pallas_call(
        functools.partial(_na_kernel, emit_ctx, rows),
        grid=(nb, H),
        in_specs=in_specs, out_specs=out_specs, out_shape=out_shape,
        scratch_shapes=[pltpu.VMEM((seq, LANE), BF16), pltpu.VMEM((seq, LANE), BF16)],
        compiler_params=_cp("parallel", "parallel"),
        name="na_attn",
    )(z, z, z, z, z, z, bias)


def _rms(x, w):
    return x * lax.rsqrt(jnp.mean(x * x, axis=-1, keepdims=True) + RMS_EPS) * w


def _mla_prep_kernel(cq_ref, ckv_ref, kra_ref, krb_ref, cc_ref, ss_ref, qnw_ref, kvnw_ref, wuq_ref, wukv_ref,
                     q_ref, kn_ref, kr_ref, v_ref):
    cc, ss = cc_ref[...], ss_ref[...]
    qn = _rms(cq_ref[...], qnw_ref[...]).astype(BF16)
    qa = _dot(qn, wuq_ref[...])
    for h in range(N_HEADS):
        b = 3 * LANE * h
        q_ref[:, 2 * LANE * h:2 * LANE * h + LANE] = qa[:, b:b + LANE].astype(BF16)
        q_ref[:, 2 * LANE * h + LANE:2 * LANE * (h + 1)] = (
            qa[:, b + LANE:b + 2 * LANE] * cc + qa[:, b + 2 * LANE:b + 3 * LANE] * ss).astype(BF16)
    kvn = _rms(ckv_ref[...], kvnw_ref[...]).astype(BF16)
    kv = _dot(kvn, wukv_ref[...])
    kn_ref[...] = kv[:, :GROUP_WIDTH].astype(BF16)
    v_ref[...] = kv[:, GROUP_WIDTH:].astype(BF16)
    kr_ref[...] = (kra_ref[...] * cc + krb_ref[...] * ss).astype(BF16)


def _mla_prep(z, cc, ss, qnw, kvnw, wuq, wukv, n_lat, seq, tm=512):
    m = z.shape[0]
    nlt, spt = n_lat // tm, seq // tm
    tab = lambda i: jnp.where(i < nlt, i % spt, spt)
    H = N_HEADS
    return pl.pallas_call(
        _mla_prep_kernel,
        grid=(m // tm,),
        in_specs=[pl.BlockSpec((tm, MLA_Q_RANK), lambda i: (i, CB_MQ * LANE // MLA_Q_RANK)),
                  pl.BlockSpec((tm, MLA_KV_RANK), lambda i: (i, CB_MKV * LANE // MLA_KV_RANK)),
                  pl.BlockSpec((tm, LANE), lambda i: (i, CB_MKRA)),
                  pl.BlockSpec((tm, LANE), lambda i: (i, CB_MKRB)),
                  pl.BlockSpec((tm, LANE), lambda i: (tab(i), 0)),
                  pl.BlockSpec((tm, LANE), lambda i: (tab(i), 0)),
                  pl.BlockSpec((1, MLA_Q_RANK), lambda i: (0, 0)),
                  pl.BlockSpec((1, MLA_KV_RANK), lambda i: (0, 0)),
                  pl.BlockSpec(wuq.shape, lambda i: (0, 0)),
                  pl.BlockSpec(wukv.shape, lambda i: (0, 0))],
        out_specs=[pl.BlockSpec((tm, 2 * LANE * H), lambda i: (i, 0)),
                   pl.BlockSpec((tm, GROUP_WIDTH), lambda i: (i, 0)),
                   pl.BlockSpec((tm, LANE), lambda i: (i, 0)),
                   pl.BlockSpec((tm, GROUP_WIDTH), lambda i: (i, 0))],
        out_shape=[jax.ShapeDtypeStruct((m, 2 * LANE * H), BF16),
                   jax.ShapeDtypeStruct((m, GROUP_WIDTH), BF16),
                   jax.ShapeDtypeStruct((m, LANE), BF16),
                   jax.ShapeDtypeStruct((m, GROUP_WIDTH), BF16)],
        compiler_params=_cp("parallel"),
        name="mla_prep",
    )(z, z, z, z, cc, ss, qnw.reshape(1, -1), kvnw.reshape(1, -1), wuq, wukv)


def _mla_attn_kernel(with_lat, seq, *refs):
    if with_lat:
        q_ref, knl_ref, krl_ref, vl_ref, knc_ref, krc_ref, vc_ref, y_ref, k_scr = refs
    else:
        q_ref, knc_ref, krc_ref, vc_ref, y_ref, k_scr = refs
    scale = MLA_QK_DIM ** -0.5
    nk = k_scr.shape[0]

    @pl.when(pl.program_id(2) == 0)
    def _():
        if with_lat:
            k_scr[0:seq, 0:LANE] = knl_ref[...]
            k_scr[0:seq, LANE:2 * LANE] = krl_ref[...]
        k_scr[nk - knc_ref.shape[0]:nk, 0:LANE] = knc_ref[...]
        k_scr[nk - knc_ref.shape[0]:nk, LANE:2 * LANE] = krc_ref[...]

    tq = q_ref.shape[0]
    qsub = min(tq, MLA_QSUB)
    nsub = tq // qsub
    vals = ([vl_ref] if with_lat else []) + [vc_ref]

    def qk(s):
        q = q_ref[s * qsub:(s + 1) * qsub, :]
        out = [_dot_nt(q, k_scr[0:seq, :])] if with_lat else []
        return out + [_dot_nt(q, k_scr[nk - knc_ref.shape[0]:nk, :])]

    def softmax(raw):
        m = raw[0].max(axis=-1, keepdims=True)
        for s in raw[1:]:
            m = jnp.maximum(m, s.max(axis=-1, keepdims=True))
        ps = [jnp.exp((s - m) * scale) for s in raw]
        den = sum(jnp.sum(p, axis=-1, keepdims=True) for p in ps)
        return [p.astype(BF16) for p in ps], den

    raw = qk(0)
    for s in range(nsub):
        nxt = qk(s + 1) if s + 1 < nsub else None
        ps, den = softmax(raw)
        acc = sum(_dot(p, v[...]) for p, v in zip(ps, vals))
        y_ref[s * qsub:(s + 1) * qsub, :] = (acc / den).astype(y_ref.dtype)
        raw = nxt


def _mla_attn(q, kn, kr, v, nb, seq, ctx_len, with_lat, tq=2048):
    H = N_HEADS
    cblk0 = nb * seq // ctx_len
    nq = seq if with_lat else ctx_len
    tq = min(tq, nq)
    qblk0 = 0 if with_lat else nb * seq // tq
    ctxs = [pl.BlockSpec((ctx_len, LANE), lambda b, h, i: (cblk0 + b, h)),
            pl.BlockSpec((ctx_len, LANE), lambda b, h, i: (cblk0 + b, 0)),
            pl.BlockSpec((ctx_len, LANE), lambda b, h, i: (cblk0 + b, h))]
    lats = [pl.BlockSpec((seq, LANE), lambda b, h, i: (b, h)),
            pl.BlockSpec((seq, LANE), lambda b, h, i: (b, 0)),
            pl.BlockSpec((seq, LANE), lambda b, h, i: (b, h))]
    in_specs = [pl.BlockSpec((tq, 2 * LANE), lambda b, h, i: (qblk0 + b * (nq // tq) + i, h))]
    args = [q]
    if with_lat:
        in_specs += lats
        args += [kn, kr, v]
    in_specs += ctxs
    args += [kn, kr, v]
    nk = (seq if with_lat else 0) + ctx_len
    return pl.pallas_call(
        functools.partial(_mla_attn_kernel, with_lat, seq),
        grid=(nb, H, nq // tq),
        in_specs=in_specs,
        out_specs=pl.BlockSpec((tq, LANE), lambda b, h, i: (b * (nq // tq) + i, h)),
        out_shape=jax.ShapeDtypeStruct((nb * nq, GROUP_WIDTH), BF16),
        scratch_shapes=[pltpu.VMEM((nk, 2 * LANE), BF16)],
        compiler_params=_cp("parallel", "parallel", "arbitrary"),
        name="mla_attn_lat" if with_lat else "mla_attn_ctx",
    )(*args)


def _prep_w_in(w_in):
    d = w_in.shape[0]
    gw = GROUP_WIDTH
    o_na = 4 * gw + 4 * N_HEADS
    o_mla = o_na + 3 * gw
    o_kr = o_mla + MLA_Q_RANK + MLA_KV_RANK
    o_hg = o_kr + MLA_ROPE
    kr = w_in[:, o_kr:o_hg]
    k1, k2 = kr[:, 0::2], kr[:, 1::2]
    z = lambda n: jnp.zeros((d, n), w_in.dtype)
    o_kv = o_mla + MLA_Q_RANK
    cols = [w_in[:, o_mla:o_kv], k1, k2, z(LANE - MLA_ROPE),
            w_in[:, o_kv:o_kr], k2, k1, z(LANE - MLA_ROPE),
            w_in[:, 4 * gw:o_na], z(LANE - 4 * N_HEADS),
            w_in[:, :4 * gw], w_in[:, o_na:o_mla], w_in[:, o_hg:]]
    w = jnp.concatenate(cols, axis=1).astype(BF16)
    assert w.shape[1] == NP_IN
    return w


def _prep_w_uq(w_uq):
    r = w_uq.shape[0]
    z = jnp.zeros((r, LANE - MLA_ROPE), w_uq.dtype)
    cols = []
    for h in range(N_HEADS):
        wh = w_uq[:, h * MLA_QK_DIM:(h + 1) * MLA_QK_DIM]
        rope = wh[:, MLA_NOPE:]
        r1, r2 = rope[:, 0::2], rope[:, 1::2]
        cols += [wh[:, :MLA_NOPE], r1, r2, z, r2, r1, z]
    return jnp.concatenate(cols, axis=1).astype(BF16)


def _rope_tables(seq, tm):
    n_freq = MLA_ROPE // 4
    freqs = ROPE_BASE ** (-jnp.arange(n_freq, dtype=F32) / n_freq)
    t = jnp.arange(seq)
    ang = jnp.concatenate([(t // GRID_W).astype(F32)[:, None] * freqs,
                           (t % GRID_W).astype(F32)[:, None] * freqs], -1)
    cos, sin = jnp.cos(ang), jnp.sin(ang)
    zp = jnp.zeros((seq, LANE - MLA_ROPE), F32)
    cc = jnp.concatenate([cos, cos, zp], axis=1)
    ss = jnp.concatenate([-sin, sin, zp], axis=1)
    ident = jnp.zeros((tm, LANE), F32).at[:, :MLA_ROPE].set(1.0)
    return jnp.concatenate([cc, ident], axis=0), jnp.concatenate([ss, jnp.zeros((tm, LANE), F32)], axis=0)


def kernel(x, c, ctx, c_ctx, w_ada, b_ada, w_in, gdn_conv_w, gdn_a_log, gdn_dt_bias, gdn_norm_w, na_rpb,
           mla_q_norm_w, mla_kv_norm_w, mla_w_uq, mla_w_uk, mla_w_uv, hgrn_lower_bounds, hgrn_norm_w, w_out,
           ln1_w, ln1_b, w_mlp1, w_mlp2, ln2_w, ln2_b):
    nb, seq, d = x.shape
    ctx_len = ctx.shape[1]
    depth = w_ada.shape[0]
    n_lat, n_ctx = nb * seq, nb * ctx_len
    alpha = (2 * depth) ** 0.25
    tm = 512
    tmm = 1024 if (seq % 1024 == 0 and n_ctx % 1024 == 0) else tm
    assert nb < 8 and seq % tm == 0 and n_ctx % tm == 0 and seq % ctx_len == 0 and ctx_len % CHUNK == 0

    cin = jnp.zeros((8, d), F32).at[:nb].set(c).at[nb].set(c_ctx)
    ada = _ada(cin, w_ada, b_ada)
    p_lb = jax.nn.softmax(hgrn_lower_bounds.astype(F32), axis=0)
    lbs = jnp.cumsum(p_lb, axis=0) - p_lb[0]
    cc, ss = _rope_tables(seq, tm)
    na_bias = _na_bias_tables(na_rpb)

    xs = (x.reshape(n_lat, d), ctx.reshape(n_ctx, d))
    for l in range(depth):
        emit_ctx = l < depth - 1
        ada_r = ada[l].reshape(8 * 6, 1, d)
        z = _inproj(xs, ada_r, _prep_w_in(w_in[l]), n_lat + n_ctx, n_lat, seq, nb, tm=tmm)

        qkv = _gdn_conv(z, gdn_conv_w[l], nb, seq, ctx_len)
        p = _gdn_gates(z, gdn_a_log[l], gdn_dt_bias[l], tm=tm)
        ct = p[:, :2 * N_HEADS].reshape(-1, CHUNK, 2 * N_HEADS).transpose(0, 2, 1)
        gdn_o = _gdn_scan(*_gdn_prep(qkv, p, ct, nb, seq, ctx_len), nb, seq, ctx_len)
        m_out = n_lat + n_ctx if emit_ctx else n_lat
        yb = _na(z, na_bias[l], nb, seq, ctx_len, emit_ctx)
        q, kn, kr, v = _mla_prep(z, cc, ss, mla_q_norm_w[l], mla_kv_norm_w[l], _prep_w_uq(mla_w_uq[l]),
                                 jnp.concatenate([mla_w_uk[l], mla_w_uv[l]], axis=1).astype(BF16),
                                 n_lat, seq, tm=tm)
        ym = [_mla_attn(q, kn, kr, v, nb, seq, ctx_len, True)]
        if emit_ctx:
            ym.append(_mla_attn(q, kn, kr, v, nb, seq, ctx_len, False))
        hgrn_o = _gla_scan(z, lbs[l], nb, seq, ctx_len)

        if len(xs) == 1 and emit_ctx:
            yb, ym = [jnp.concatenate(yb, axis=0)], [jnp.concatenate(ym, axis=0)]
        x_all = _outproj(xs if emit_ctx else xs[:1], gdn_o, yb, ym, hgrn_o, z, gdn_norm_w[l], hgrn_norm_w[l], w_out[l].astype(BF16),
                         ada_r, ln1_w[l], ln1_b[l], m_out, n_lat, seq, nb, alpha)
        x_all = _mlp(x_all, w_mlp1[l].astype(BF16), w_mlp2[l].astype(BF16), ada_r, ln2_w[l], ln2_b[l],
                     seq, nb, alpha, tm=tm)
        xs = (x_all,)
    return x_all[:n_lat].reshape(nb, seq, d)
```

```python
import functools

import numpy as np
import jax
import jax.numpy as jnp
from jax import lax
from jax.experimental import pallas as pl
from jax.experimental.pallas import tpu as pltpu

F32 = jnp.float32
BF16 = jnp.bfloat16
HIGHEST = lax.Precision.HIGHEST

GRID_W = 64
N_HEADS = 4
HEAD_DIM = 128
GROUP_WIDTH = 512
CHUNK = 64
SUB = 8
GDN_CONV = 5
NA_ROWS = 8
NA_COLS = 16
NA_UNROLL = 4
MLA_Q_RANK = 384
MLA_KV_RANK = 256
MLA_NOPE = 128
MLA_ROPE = 64
MLA_QK_DIM = MLA_NOPE + MLA_ROPE
MLA_QSUB = 512
ROPE_BASE = 10000.0
LN_EPS = 1e-5
RMS_EPS = 1e-6
LOG2E = 1.4426950408889634
NEG = -1e30

LANE = 128
CB_MQ, CB_MKRA, CB_MKV, CB_MKRB, CB_GAB = 0, 3, 4, 6, 7
CB_GQKV, CB_GGATE = 8, 20
CB_NAQ, CB_NAK, CB_NAV = 24, 28, 32
CB_HQ, CB_HFF, CB_HFB, CB_HI, CB_HG = 36, 40, 44, 48, 52
CHUNK_SHIFT = 6
PREP_CHUNKS = 4
NP_IN = 56 * LANE

VMEM_LIMIT = 48 << 20


def _cp(*sem):
    return pltpu.CompilerParams(dimension_semantics=sem, vmem_limit_bytes=VMEM_LIMIT)


def _silu(x):
    return x * jax.nn.sigmoid(x)


def _dot(a, b, **kw):
    return jnp.dot(a, b, preferred_element_type=F32, **kw)


def _dot_nt(a, b, **kw):
    return lax.dot_general(a, b, (((1,), (1,)), ((), ())), preferred_element_type=F32, **kw)


def _dot_tn(a, b, **kw):
    return lax.dot_general(a, b, (((0,), (0,)), ((), ())), preferred_element_type=F32, **kw)


def _ada_kernel(c_ref, w_ref, b_ref, o_ref):
    s = _silu(c_ref[...])
    o_ref[0] = _dot(s, w_ref[0], precision=HIGHEST) + b_ref[0]


def _ada(cin, w_ada, b_ada):
    depth, d, n = w_ada.shape
    tn = 512
    return pl.pallas_call(
        _ada_kernel,
        grid=(depth, n // tn),
        in_specs=[pl.BlockSpec((8, d), lambda l, j: (0, 0)),
                  pl.BlockSpec((1, d, tn), lambda l, j: (l, 0, j)),
                  pl.BlockSpec((1, 1, tn), lambda l, j: (l, 0, j))],
        out_specs=pl.BlockSpec((1, 8, tn), lambda l, j: (l, 0, j)),
        out_shape=jax.ShapeDtypeStruct((depth, 8, n), F32),
        compiler_params=_cp("parallel", "parallel"),
        name="ada",
    )(cin, w_ada, b_ada.reshape(depth, 1, n))


ROW_STEP = 256


def _modulate(x_ref, sh_ref, sc_ref, xm_ref):
    sc1, sh = 1.0 + sc_ref[0], sh_ref[0]

    def body(t, carry):
        sl = pl.ds(pl.multiple_of(t * ROW_STEP, ROW_STEP), ROW_STEP)
        xm_ref[sl, :] = (x_ref[sl, :] * sc1 + sh).astype(BF16)
        return carry

    lax.fori_loop(0, x_ref.shape[0] // ROW_STEP, body, 0)


def _inproj_kernel(n_lat_tiles, *refs):
    xs, (sh_ref, sc_ref, w_ref, o_ref, xm_ref) = refs[:-5], refs[-5:]
    first = pl.program_id(1) == 0
    if len(xs) == 1:
        pl.when(first)(lambda: _modulate(xs[0], sh_ref, sc_ref, xm_ref))
    else:
        is_lat = pl.program_id(0) < n_lat_tiles
        pl.when(first & is_lat)(lambda: _modulate(xs[0], sh_ref, sc_ref, xm_ref))
        pl.when(first & jnp.logical_not(is_lat))(lambda: _modulate(xs[1], sh_ref, sc_ref, xm_ref))
    o_ref[...] = _dot(xm_ref[...], w_ref[...])


def _inproj(xs, ada_r, w, m, n_lat, seq, nb, tm=512, tn=1024):
    d, n = w.shape
    nlt = n_lat // tm
    row = lambda i: jnp.minimum((i * tm) // seq, nb)
    if len(xs) == 1:
        x_specs = [pl.BlockSpec((tm, d), lambda i, j: (i, 0))]
    else:
        x_specs = [pl.BlockSpec((tm, d), lambda i, j: (jnp.minimum(i, nlt - 1), 0)),
                   pl.BlockSpec((tm, d), lambda i, j: (jnp.maximum(i - nlt, 0), 0),
                                pipeline_mode=pl.Buffered(1))]
    return pl.pallas_call(
        functools.partial(_inproj_kernel, nlt),
        grid=(m // tm, n // tn),
        in_specs=x_specs + [
                  pl.BlockSpec((1, 1, d), lambda i, j: (row(i) * 6 + 0, 0, 0)),
                  pl.BlockSpec((1, 1, d), lambda i, j: (row(i) * 6 + 1, 0, 0)),
                  pl.BlockSpec((d, tn), lambda i, j: (0, j))],
        out_specs=pl.BlockSpec((tm, tn), lambda i, j: (i, j)),
        out_shape=jax.ShapeDtypeStruct((m, n), F32),
        scratch_shapes=[pltpu.VMEM((tm, d), BF16)],
        compiler_params=_cp("parallel", "arbitrary"),
        name="inproj",
    )(*xs, ada_r, ada_r, w)


def _layernorm(r, w, b):
    mu = jnp.mean(r, axis=-1, keepdims=True)
    rc = r - mu
    var = jnp.mean(rc * rc, axis=-1, keepdims=True)
    return rc * lax.rsqrt(var + LN_EPS) * w + b


def _head_norm_gate(o, nw, gate):
    o = o * lax.rsqrt(jnp.mean(o * o, axis=-1, keepdims=True) + RMS_EPS) * nw
    return o * _silu(gate)


def _scan_mixer_out(of_ref, ob_ref, gate_ref, nw_ref):
    nw = nw_ref[...]
    heads = []
    for h in range(N_HEADS):
        hs = slice(h * LANE, (h + 1) * LANE)
        heads.append(_head_norm_gate(of_ref[:, hs] + ob_ref[:, hs], nw, gate_ref[:, hs]).astype(BF16))
    return jnp.concatenate(heads, axis=-1)


def _outproj_kernel(alpha, n_lat_tiles, n_split, *refs):
    pairs, rest = refs[:n_split], refs[n_split:]
    (gaf_ref, gab_ref, gag_ref, hgf_ref, hgb_ref, hgg_ref, nwa_ref, nwh_ref,
     w_ref, g_ref, lw_ref, lb_ref, o_ref) = rest
    gw = GROUP_WIDTH

    def run(x_ref, yb_ref, ym_ref):
        acc = _dot(_scan_mixer_out(gaf_ref, gab_ref, gag_ref, nwa_ref), w_ref[0:gw, :])
        acc += _dot(yb_ref[...], w_ref[gw:2 * gw, :])
        acc += _dot(ym_ref[...], w_ref[2 * gw:3 * gw, :])
        acc += _dot(_scan_mixer_out(hgf_ref, hgb_ref, hgg_ref, nwh_ref), w_ref[3 * gw:4 * gw, :])
        r = alpha * x_ref[...] + g_ref[0] * acc
        o_ref[...] = _layernorm(r, lw_ref[...], lb_ref[...])

    if n_split == 3:
        run(*pairs)
    else:
        is_lat = pl.program_id(0) < n_lat_tiles
        pl.when(is_lat)(lambda: run(*pairs[0::2]))
        pl.when(jnp.logical_not(is_lat))(lambda: run(*pairs[1::2]))


def _outproj(xs, gdn_o, ybs, yms, hgrn_o, z, nwa, nwh, w, ada_r, lw, lb, m_out, n_lat, seq, nb, alpha, tm=256):
    d = w.shape[0]
    gw = GROUP_WIDTH
    nlt = n_lat // tm
    row = lambda i: jnp.minimum((i * tm) // seq, nb)
    in_specs, args = [], []
    dual = len(xs) == 2
    for arrs, width in ((xs, d), (ybs, gw), (yms, gw)):
        if dual:
            in_specs += [pl.BlockSpec((tm, width), lambda i: (jnp.minimum(i, nlt - 1), 0)),
                         pl.BlockSpec((tm, width), lambda i: (jnp.maximum(i - nlt, 0), 0))]
        else:
            in_specs += [pl.BlockSpec((tm, width), lambda i: (i, 0))]
        args += list(arrs)
    n_split = len(args)
    rowspec = pl.BlockSpec((tm, gw), lambda i: (i, 0))
    gate = lambda cb: pl.BlockSpec((tm, gw), lambda i: (i, cb * LANE // gw))
    vec = lambda n: pl.BlockSpec((1, n), lambda i: (0, 0))
    in_specs += [rowspec, rowspec, gate(CB_GGATE), rowspec, rowspec, gate(CB_HG), vec(LANE), vec(LANE),
                 pl.BlockSpec((d, d), lambda i: (0, 0), pipeline_mode=pl.Buffered(1)),
                 pl.BlockSpec((1, 1, d), lambda i: (row(i) * 6 + 2, 0, 0)), vec(d), vec(d)]
    args += [*gdn_o, z, *hgrn_o, z, nwa.reshape(1, LANE), nwh.reshape(1, LANE), w, ada_r,
             lw.reshape(1, d), lb.reshape(1, d)]
    return pl.pallas_call(
        functools.partial(_outproj_kernel, alpha, nlt, n_split),
        grid=(m_out // tm,),
        in_specs=in_specs,
        out_specs=pl.BlockSpec((tm, d), lambda i: (i, 0)),
        out_shape=jax.ShapeDtypeStruct((m_out, d), F32),
        compiler_params=_cp("parallel"),
        name="outproj_ln",
    )(*args)


def _mlp_kernel(alpha, x_ref, sh_ref, sc_ref, g_ref, w1_ref, w2_ref, lw_ref, lb_ref, o_ref, xm_ref, acc_ref):
    k = pl.program_id(1)

    @pl.when(k == 0)
    def _():
        _modulate(x_ref, sh_ref, sc_ref, xm_ref)
        acc_ref[...] = jnp.zeros_like(acc_ref)

    h = jnp.maximum(_dot(xm_ref[...], w1_ref[...]), 0.0)
    acc_ref[...] += _dot((h * h).astype(BF16), w2_ref[...])

    @pl.when(k == pl.num_programs(1) - 1)
    def _():
        g, lw, lb = g_ref[0], lw_ref[...], lb_ref[...]

        def body(t, carry):
            sl = pl.ds(pl.multiple_of(t * ROW_STEP, ROW_STEP), ROW_STEP)
            o_ref[sl, :] = _layernorm(alpha * x_ref[sl, :] + g * acc_ref[sl, :], lw, lb)
            return carry

        lax.fori_loop(0, x_ref.shape[0] // ROW_STEP, body, 0)


def _mlp(x_all, w1, w2, ada_r, lw, lb, seq, nb, alpha, tm=512, th=1024):
    m, d = x_all.shape
    hid = w1.shape[1]
    row = lambda i: jnp.minimum((i * tm) // seq, nb)
    return pl.pallas_call(
        functools.partial(_mlp_kernel, alpha),
        grid=(m // tm, hid // th),
        in_specs=[pl.BlockSpec((tm, d), lambda i, k: (i, 0)),
                  pl.BlockSpec((1, 1, d), lambda i, k: (row(i) * 6 + 3, 0, 0)),
                  pl.BlockSpec((1, 1, d), lambda i, k: (row(i) * 6 + 4, 0, 0)),
                  pl.BlockSpec((1, 1, d), lambda i, k: (row(i) * 6 + 5, 0, 0)),
                  pl.BlockSpec((d, th), lambda i, k: (0, k)),
                  pl.BlockSpec((th, d), lambda i, k: (k, 0)),
                  pl.BlockSpec((1, d), lambda i, k: (0, 0)),
                  pl.BlockSpec((1, d), lambda i, k: (0, 0))],
        out_specs=pl.BlockSpec((tm, d), lambda i, k: (i, 0)),
        out_shape=jax.ShapeDtypeStruct((m, d), F32),
        scratch_shapes=[pltpu.VMEM((tm, d), BF16), pltpu.VMEM((tm, d), F32)],
        compiler_params=_cp("parallel", "arbitrary"),
        name="mlp_ln",
    )(x_all, ada_r, ada_r, ada_r, w1, w2, lw.reshape(1, d), lb.reshape(1, d))


def _gdn_conv_kernel(seq, ctx_len, xl_ref, xc_ref, w_ref, o_ref, pad_ref):
    j = pl.program_id(1)
    w = w_ref[...]
    qscale = jnp.where(j < N_HEADS, HEAD_DIM ** -0.5, 1.0).astype(F32)
    p0 = 8 - GDN_CONV // 2
    for x_ref, nrows, o0 in ((xl_ref, seq, 0), (xc_ref, ctx_len, seq)):
        pad_ref[0:8, :] = jnp.zeros((8, LANE), F32)
        pad_ref[nrows + 8:nrows + 16, :] = jnp.zeros((8, LANE), F32)
        pad_ref[8:nrows + 8, :] = x_ref[...]
        rb = min(nrows, 256)
        for r0 in range(0, nrows, rb):
            y = pad_ref[r0 + p0:r0 + p0 + rb, :] * w[0:1, :]
            for i in range(1, GDN_CONV):
                y = y + pad_ref[r0 + p0 + i:r0 + p0 + i + rb, :] * w[i:i + 1, :]
            y = _silu(y)
            nrm = y * lax.rsqrt(jnp.sum(y * y, axis=-1, keepdims=True) + RMS_EPS) * qscale
            o_ref[o0 + r0:o0 + r0 + rb, :] = jnp.where(j < 2 * N_HEADS, nrm, y)


def _gdn_conv(z, conv_w, nb, seq, ctx_len):
    nblk = 3 * N_HEADS
    cblk0 = nb * seq // ctx_len
    return pl.pallas_call(
        functools.partial(_gdn_conv_kernel, seq, ctx_len),
        grid=(nb, nblk),
        in_specs=[pl.BlockSpec((seq, LANE), lambda b, j: (b, CB_GQKV + j)),
                  pl.BlockSpec((ctx_len, LANE), lambda b, j: (cblk0 + b, CB_GQKV + j)),
                  pl.BlockSpec((GDN_CONV, LANE), lambda b, j: (0, j))],
        out_specs=pl.BlockSpec((seq + ctx_len, LANE), lambda b, j: (b, j)),
        out_shape=jax.ShapeDtypeStruct((nb * (seq + ctx_len), nblk * LANE), F32),
        scratch_shapes=[pltpu.VMEM((seq + 16, LANE), F32)],
        compiler_params=_cp("parallel", "parallel"),
        name="gdn_conv",
    )(z, z, conv_w)


def _gdn_gates_kernel(tm, s_ref, alog_ref, dtb_ref, o_ref):
    s = s_ref[...]
    g = -jnp.exp(alog_ref[...]) * (jnp.maximum(s + dtb_ref[...], 0.0)
                                    + jnp.log1p(jnp.exp(-jnp.abs(s + dtb_ref[...]))))
    r = lax.broadcasted_iota(jnp.int32, (tm, tm), 0)
    c = lax.broadcasted_iota(jnp.int32, (tm, tm), 1)
    same = (r >> CHUNK_SHIFT) == (c >> CHUNK_SHIFT)
    lo = jnp.where(same & (c <= r), 1.0, 0.0).astype(F32)
    up = jnp.where(same & (c >= r), 1.0, 0.0).astype(F32)
    cum_f = _dot(lo, g, precision=HIGHEST)
    cum_b = _dot(up, g, precision=HIGHEST)
    col = lax.broadcasted_iota(jnp.int32, s.shape, 1)
    o_ref[...] = jnp.where(col < N_HEADS, cum_f,
                           jnp.where(col < 2 * N_HEADS, cum_b,
                                     jnp.where(col < 4 * N_HEADS, jax.nn.sigmoid(s), 0.0)))


def _gdn_gates(z, a_log, dt_bias, tm=512):
    m = z.shape[0]
    pad = lambda v: jnp.zeros((1, LANE), F32).at[0, :2 * N_HEADS].set(v.reshape(-1).astype(F32))
    return pl.pallas_call(
        functools.partial(_gdn_gates_kernel, tm),
        grid=(m // tm,),
        in_specs=[pl.BlockSpec((tm, LANE), lambda i: (i, CB_GAB)),
                  pl.BlockSpec((1, LANE), lambda i: (0, 0)),
                  pl.BlockSpec((1, LANE), lambda i: (0, 0))],
        out_specs=pl.BlockSpec((tm, LANE), lambda i: (i, 0)),
        out_shape=jax.ShapeDtypeStruct((m, LANE), F32),
        compiler_params=_cp("parallel"),
        name="gdn_gates",
    )(z, pad(a_log), pad(dt_bias))


def _tri_masks(n):
    r = lax.broadcasted_iota(jnp.int32, (n, n), 0)
    c = lax.broadcasted_iota(jnp.int32, (n, n), 1)
    return r, c


def _split(x):
    hi = x.astype(BF16)
    return hi, (x - hi.astype(F32)).astype(BF16)


def _dot3(a, b):
    return _dot(a[0], b[0]) + (_dot(a[0], b[1]) + _dot(a[1], b[0]))


def _gdn_prep_kernel(qkv_ref, p_ref, u_ref, w_ref, qt_ref, kt_ref, att_ref, el_ref):
    C, H = CHUNK, N_HEADS
    lane = lax.broadcasted_iota(jnp.int32, (C, LANE), 1)
    r, c = _tri_masks(C)
    eye = jnp.where(r == c, 1.0, 0.0).astype(F32)
    a_list, rhs_list, where_list = [], [], []
    for n in range(PREP_CHUNKS):
        rs = slice(n * C, (n + 1) * C)
        pblk = p_ref[rs, :]
        tblk = pblk.T
        col = lambda idx, pblk=pblk: jnp.sum(jnp.where(lane == idx, pblk, 0.0), axis=-1, keepdims=True)
        row = lambda idx, tblk=tblk: tblk[idx:idx + 1, :]
        for h in range(H):
            hs = slice(h * LANE, (h + 1) * LANE)
            q = qkv_ref[rs, h * LANE:(h + 1) * LANE]
            k = qkv_ref[rs, (H + h) * LANE:(H + h + 1) * LANE]
            v = qkv_ref[rs, (2 * H + h) * LANE:(2 * H + h + 1) * LANE]
            qbf, kbf = q.astype(BF16), k.astype(BF16)
            for d in range(2):
                idx = d * H + h
                cum_c, cum_r, beta_c = col(idx), row(idx), col(2 * H + idx)
                incl = (c >= r) if d else (c <= r)
                strict = (c > r) if d else (c < r)
                last = cum_r[:, 0:1] if d else cum_r[:, C - 1:C]
                decay = jnp.exp(jnp.where(incl, cum_c - cum_r, NEG))
                kb = k * beta_c
                ec = jnp.exp(cum_c)
                a_list.append(jnp.where(strict, _dot_nt(kb.astype(BF16), kbf) * decay, 0.0))
                rhs_list.append(jnp.concatenate([v * beta_c, kb * ec], axis=-1).astype(BF16))
                where_list.append((d, rs, hs))
                att_ref[d, h, rs, :] = jnp.where(incl, _dot_nt(qbf, kbf) * decay, 0.0).astype(BF16)
                qt_ref[d, rs, hs] = (q * ec).astype(BF16)
                kt_ref[d, rs, hs] = (k * jnp.exp(last - cum_c)).astype(BF16)
                el_ref[n, idx:idx + 1, :] = jnp.broadcast_to(jnp.exp(last), (1, LANE))
    ts = [eye - a for a in a_list]
    ps = [a.astype(BF16) for a in a_list]
    for _ in range(5):
        ps = [_dot(p, p).astype(BF16) for p in ps]
        ts = [t + _dot(t.astype(BF16), p) for t, p in zip(ts, ps)]
    res = [eye - t - _dot3(_split(a), _split(t)) for a, t in zip(a_list, ts)]
    ts = [t + _dot(t.astype(BF16), e.astype(BF16)) for t, e in zip(ts, res)]
    sols = [_dot(t.astype(BF16), rhs) for t, rhs in zip(ts, rhs_list)]
    for sol, (d, rs, hs) in zip(sols, where_list):
        u_ref[d, rs, hs] = sol[:, :HEAD_DIM]
        w_ref[d, rs, hs] = sol[:, HEAD_DIM:].astype(BF16)


def _gdn_prep(qkv, p, nb, seq, ctx_len):
    m = qkv.shape[0]
    H = N_HEADS
    gw = GROUP_WIDTH
    rows = PREP_CHUNKS * CHUNK
    assert seq % rows == 0 and ctx_len % rows == 0
    dspec = pl.BlockSpec((2, rows, gw), lambda i: (0, i, 0))
    nl, nc = seq // rows, ctx_len // rows

    def qkv_blk(i):
        ic = i - nb * nl
        return jnp.where(i < nb * nl, (i // nl) * (nl + nc) + i % nl, (ic // nc) * (nl + nc) + nl + ic % nc)

    return pl.pallas_call(
        _gdn_prep_kernel,
        grid=(m // rows,),
        in_specs=[pl.BlockSpec((rows, 3 * gw), lambda i: (qkv_blk(i), 0)),
                  pl.BlockSpec((rows, LANE), lambda i: (i, 0))],
        out_specs=[dspec, dspec, dspec, dspec,
                   pl.BlockSpec((2, H, rows, CHUNK), lambda i: (0, 0, i, 0)),
                   pl.BlockSpec((PREP_CHUNKS, 2 * H, LANE), lambda i: (i, 0, 0))],
        out_shape=[jax.ShapeDtypeStruct((2, m, gw), F32),
                   jax.ShapeDtypeStruct((2, m, gw), BF16),
                   jax.ShapeDtypeStruct((2, m, gw), BF16),
                   jax.ShapeDtypeStruct((2, m, gw), BF16),
                   jax.ShapeDtypeStruct((2, H, m, CHUNK), BF16),
                   jax.ShapeDtypeStruct((m // CHUNK, 2 * H, LANE), F32)],
        compiler_params=_cp("parallel"),
        name="gdn_prep",
    )(qkv, p)


def _gdn_scan_kernel(g, *refs):
    ins, (of_ref, ob_ref, s_ref) = refs[:12], refs[12:]
    H = N_HEADS

    @pl.when(pl.program_id(1) == 0)
    def _():
        s_ref[...] = jnp.zeros_like(s_ref)

    chains = [(d, h) for d in range(2) for h in range(H)]
    for j in range(g):
        mid = []
        for d, h in chains:
            u_ref, w_ref, qt_ref = ins[d:6:2]
            cj = g - 1 - j if d else j
            rs, hs = slice(cj * CHUNK, (cj + 1) * CHUNK), slice(h * LANE, (h + 1) * LANE)
            sb = s_ref[d * H + h].astype(BF16)
            vnb = (u_ref[0, rs, hs] - _dot(w_ref[0, rs, hs], sb)).astype(BF16)
            mid.append((vnb, _dot(qt_ref[0, rs, hs], sb)))
        for (d, h), (vnb, o_state) in zip(chains, mid):
            kt_ref, att_ref, el_ref = ins[6 + d::2]
            o_ref = ob_ref if d else of_ref
            cj = g - 1 - j if d else j
            rs, hs = slice(cj * CHUNK, (cj + 1) * CHUNK), slice(h * LANE, (h + 1) * LANE)
            idx = d * H + h
            s_ref[idx] = s_ref[idx] * el_ref[cj, idx:idx + 1, :] + _dot_tn(kt_ref[0, rs, hs], vnb)
            o_ref[rs, hs] = o_state + _dot(att_ref[0, h, rs, :], vnb)


def _gdn_scan(u, w, qt, kt, att, el, nb, seq, ctx_len):
    m = u.shape[1]
    H = N_HEADS
    gw = GROUP_WIDTH
    g = ctx_len // CHUNK
    nblk = seq // ctx_len
    cblk0 = nb * nblk
    blk_f = lambda b, t: jnp.where(t == 0, cblk0 + b, b * nblk + t - 1)
    blk_b = lambda b, t: jnp.where(t == 0, cblk0 + b, b * nblk + nblk - t)
    in_specs, args = [], []
    for arr in (u, w, qt, kt):
        in_specs += [pl.BlockSpec((1, ctx_len, gw), lambda b, t: (0, blk_f(b, t), 0)),
                     pl.BlockSpec((1, ctx_len, gw), lambda b, t: (1, blk_b(b, t), 0))]
        args += [arr, arr]
    in_specs += [pl.BlockSpec((1, H, ctx_len, CHUNK), lambda b, t: (0, 0, blk_f(b, t), 0)),
                 pl.BlockSpec((1, H, ctx_len, CHUNK), lambda b, t: (1, 0, blk_b(b, t), 0)),
                 pl.BlockSpec((g, 2 * H, LANE), lambda b, t: (blk_f(b, t), 0, 0)),
                 pl.BlockSpec((g, 2 * H, LANE), lambda b, t: (blk_b(b, t), 0, 0))]
    args += [att, att, el, el]
    return pl.pallas_call(
        functools.partial(_gdn_scan_kernel, g),
        grid=(nb, nblk + 1),
        in_specs=in_specs,
        out_specs=[pl.BlockSpec((ctx_len, gw), lambda b, t: (blk_f(b, t), 0)),
                   pl.BlockSpec((ctx_len, gw), lambda b, t: (blk_b(b, t), 0))],
        out_shape=[jax.ShapeDtypeStruct((m, gw), F32), jax.ShapeDtypeStruct((m, gw), F32)],
        scratch_shapes=[pltpu.VMEM((2 * H, HEAD_DIM, HEAD_DIM), F32)],
        compiler_params=_cp("parallel", "arbitrary"),
        name="gdn_scan",
    )(*args)


def _gla_kernel(g, qf_ref, ff_ref, if_ref, qb_ref, fb_ref, ib_ref, lb_ref, of_ref, ob_ref, s_ref):
    C, H, nsub = CHUNK, N_HEADS, CHUNK // SUB

    @pl.when(pl.program_id(1) == 0)
    def _():
        s_ref[...] = jnp.zeros_like(s_ref)

    r, c = _tri_masks(C)
    tri = (jnp.where(c <= r, 1.0, 0.0).astype(F32), jnp.where(c >= r, 1.0, 0.0).astype(F32))
    trow = lax.broadcasted_iota(jnp.int32, (C, HEAD_DIM), 0)
    srow = lax.broadcasted_iota(jnp.int32, (SUB, C), 0)
    scol = lax.broadcasted_iota(jnp.int32, (SUB, C), 1)
    chains = [(d, h) for d in range(2) for h in range(H)]
    srcs = ((qf_ref, ff_ref, if_ref, of_ref), (qb_ref, fb_ref, ib_ref, ob_ref))

    def body(j, carry):
        ph1 = []
        for d, h in chains:
            cj = g - 1 - j if d else j
            rs, hs = pl.ds(pl.multiple_of(cj * C, C), C), slice(h * LANE, (h + 1) * LANE)
            lb = lb_ref[:, hs]
            f = lb + (1.0 - lb) * jax.nn.sigmoid(srcs[d][1][rs, hs])
            ph1.append((rs, hs, f, _dot(tri[d], jnp.log(f), precision=HIGHEST)))
        ph2 = []
        for (d, h), (rs, hs, f, cum) in zip(chains, ph1):
            q_ref, _, i_ref, _ = srcs[d]
            q = _silu(q_ref[rs, hs]) * HEAD_DIM ** -0.5
            k = 1.0 - f
            vb = i_ref[rs, hs].astype(BF16)
            last = cum[0:1, :] if d else cum[C - 1:C, :]
            idx = d * H + h
            St = s_ref[idx]
            o_state = _dot_nt((q * jnp.exp(cum)).astype(BF16), St.astype(BF16))
            s_ref[idx] = St * jnp.exp(last) + _dot_tn(vb, (k * jnp.exp(last - cum)).astype(BF16))
            inter = []
            for a in range(nsub):
                sa = slice(a * SUB, (a + 1) * SUB)
                if d and a < nsub - 1:
                    cb = cum[(a + 1) * SUB:(a + 1) * SUB + 1, :]
                    kt = k * jnp.exp(jnp.where(trow >= (a + 1) * SUB, cb - cum, NEG))
                elif (not d) and a > 0:
                    cb = cum[a * SUB - 1:a * SUB, :]
                    kt = k * jnp.exp(jnp.where(trow < a * SUB, cb - cum, NEG))
                else:
                    inter.append(jnp.zeros((SUB, C), F32))
                    continue
                inter.append(_dot_nt((q[sa] * jnp.exp(cum[sa] - cb)).astype(BF16), kt.astype(BF16)))
            ph2.append((q, k, cum, vb, o_state, inter))
        for (d, h), (rs, hs, _, _), (q, k, cum, vb, o_state, inter) in zip(chains, ph1, ph2):
            blocks = []
            for a in range(nsub):
                sa = slice(a * SUB, (a + 1) * SUB)
                qa, ka, ca, sc = q[sa], k[sa], cum[sa], inter[a]
                for t in range(SUB):
                    dec = jnp.exp(ca - ca[t:t + 1, :])
                    st = jnp.sum(qa * ka[t:t + 1, :] * dec, axis=-1, keepdims=True)
                    ok = (srow <= t) if d else (srow >= t)
                    sc = jnp.where((scol == a * SUB + t) & ok, st, sc)
                blocks.append(sc)
            scores = jnp.concatenate(blocks, axis=0)
            srcs[d][3][rs, hs] = o_state + _dot(scores.astype(BF16), vb)
        return carry

    lax.fori_loop(0, g, body, 0)


def _gla_scan(z, lbs, nb, seq, ctx_len):
    m = z.shape[0]
    gw = GROUP_WIDTH
    g = ctx_len // CHUNK
    nblk = seq // ctx_len
    cblk0 = nb * nblk
    blk_f = lambda b, t: jnp.where(t == 0, cblk0 + b, b * nblk + t - 1)
    blk_b = lambda b, t: jnp.where(t == 0, cblk0 + b, b * nblk + nblk - t)
    col = lambda cb: cb * LANE // gw
    spec = lambda blk, cb: pl.BlockSpec((ctx_len, gw), lambda b, t: (blk(b, t), col(cb)))
    return pl.pallas_call(
        functools.partial(_gla_kernel, g),
        grid=(nb, nblk + 1),
        in_specs=[spec(blk_f, CB_HQ), spec(blk_f, CB_HFF), spec(blk_f, CB_HI),
                  spec(blk_b, CB_HQ), spec(blk_b, CB_HFB), spec(blk_b, CB_HI),
                  pl.BlockSpec((1, gw), lambda b, t: (0, 0))],
        out_specs=[pl.BlockSpec((ctx_len, gw), lambda b, t: (blk_f(b, t), 0)),
                   pl.BlockSpec((ctx_len, gw), lambda b, t: (blk_b(b, t), 0))],
        out_shape=[jax.ShapeDtypeStruct((m, gw), F32), jax.ShapeDtypeStruct((m, gw), F32)],
        scratch_shapes=[pltpu.VMEM((2 * N_HEADS, HEAD_DIM, HEAD_DIM), F32)],
        compiler_params=_cp("parallel", "arbitrary"),
        name="hgrn_scan",
    )(z, z, z, z, z, z, lbs.reshape(1, gw))


def _softmax_pv(parts):
    m = parts[0][0].max(axis=-1, keepdims=True)
    for s, _ in parts[1:]:
        m = jnp.maximum(m, s.max(axis=-1, keepdims=True))
    den, acc = None, None
    for s, v in parts:
        p = jnp.exp(s - m)
        d = jnp.sum(p, axis=-1, keepdims=True)
        a = _dot(p.astype(BF16), v)
        den = d if den is None else den + d
        acc = a if acc is None else acc + a
    return acc / den


def _na_kernel(emit_ctx, rows, *refs):
    q_ref, k_ref, v_ref, qc_ref, kc_ref, vc_ref, bias_ref = refs[:7]
    if emit_ctx:
        yl_ref, yc_ref, kb_ref, vb_ref = refs[7:]
    else:
        yl_ref, kb_ref, vb_ref = refs[7:]
    scale = HEAD_DIM ** -0.5
    win = NA_ROWS * GRID_W
    kb_ref[...] = k_ref[...].astype(BF16)
    vb_ref[...] = v_ref[...].astype(BF16)
    kc = kc_ref[...].astype(BF16)
    vc = vc_ref[...].astype(BF16)

    def body(i, carry):
        pre = []
        for t in range(NA_UNROLL):
            r = i * NA_UNROLL + t
            row0 = jnp.clip(r - NA_ROWS // 2, 0, rows - NA_ROWS)
            qs = pl.ds(pl.multiple_of(r * GRID_W, GRID_W), GRID_W)
            ks = pl.ds(pl.multiple_of(row0 * GRID_W, GRID_W), win)
            q = q_ref[qs, :].astype(BF16)
            pre.append((qs, ks, _dot_nt(q, kb_ref[ks, :]) * scale + bias_ref[0, r - row0], _dot_nt(q, kc) * scale))
        mid = []
        for qs, ks, s_win, s_ctx in pre:
            m = jnp.maximum(s_win.max(axis=-1, keepdims=True), s_ctx.max(axis=-1, keepdims=True))
            p_win, p_ctx = jnp.exp(s_win - m), jnp.exp(s_ctx - m)
            den = jnp.sum(p_win, axis=-1, keepdims=True) + jnp.sum(p_ctx, axis=-1, keepdims=True)
            mid.append((qs, ks, p_win.astype(BF16), p_ctx.astype(BF16), den))
        for qs, ks, p_win, p_ctx, den in mid:
            yl_ref[qs, :] = ((_dot(p_win, vb_ref[ks, :]) + _dot(p_ctx, vc)) / den).astype(yl_ref.dtype)
        return carry

    lax.fori_loop(0, rows // NA_UNROLL, body, 0)
    if emit_ctx:
        s = _dot_nt(qc_ref[...].astype(BF16), kc) * scale
        yc_ref[...] = _softmax_pv([(s, vc)]).astype(yc_ref.dtype)


def _na_bias_kernel(rpb_ref, o_ref):
    n = lax.broadcasted_iota(jnp.int32, (LANE, GRID_W * GRID_W), 1)
    j = lax.broadcasted_iota(jnp.int32, (LANE, GRID_W * GRID_W), 0)
    q, w = n >> 6, n & (GRID_W - 1)
    dc = jnp.clip(w - q, 1 - NA_COLS, NA_COLS - 1) + NA_COLS - 1
    onehot = jnp.where(dc == j, 1.0, 0.0).astype(F32)
    m = _dot(rpb_ref[...], onehot, precision=HIGHEST)
    c0 = jnp.clip(q[0:1] - NA_COLS // 2, 0, GRID_W - NA_COLS)
    ok = (w[0:1] >= c0) & (w[0:1] < c0 + NA_COLS)
    o_ref[...] = jnp.where(ok, m, NEG)


def _na_bias_tables(rpb):
    depth, H, nr, nc = rpb.shape
    assert GRID_W == 64 and depth * H * nr <= LANE and nc <= LANE
    flat = jnp.zeros((LANE, LANE), F32).at[:depth * H * nr, :nc].set(rpb.reshape(-1, nc).astype(F32))
    m = pl.pallas_call(
        _na_bias_kernel,
        out_shape=jax.ShapeDtypeStruct((LANE, GRID_W * GRID_W), F32),
        compiler_params=pltpu.CompilerParams(vmem_limit_bytes=VMEM_LIMIT),
        name="na_bias",
    )(flat)
    m = m[:depth * H * nr].reshape(depth, H, nr, GRID_W, GRID_W)
    tab = jnp.stack([jnp.stack([m[:, :, k - s + NA_ROWS - 1] for k in range(NA_ROWS)], axis=3)
                     for s in range(NA_ROWS)], axis=2)
    return tab.reshape(depth, H, NA_ROWS, GRID_W, NA_ROWS * GRID_W)


def _na(z, bias, nb, seq, ctx_len, emit_ctx):
    rows = seq // GRID_W
    cblk0 = nb * seq // ctx_len
    H = N_HEADS
    lat = lambda cb: pl.BlockSpec((seq, LANE), lambda b, h: (b, cb + h))
    ctx = lambda cb: pl.BlockSpec((ctx_len, LANE), lambda b, h: (cblk0 + b, cb + h))
    win = NA_ROWS * GRID_W
    in_specs = [lat(CB_NAQ), lat(CB_NAK), lat(CB_NAV), ctx(CB_NAQ), ctx(CB_NAK), ctx(CB_NAV),
                pl.BlockSpec((1, NA_ROWS, GRID_W, win), lambda b, h: (h, 0, 0, 0))]
    out_specs = [pl.BlockSpec((seq, LANE), lambda b, h: (b, h))]
    out_shape = [jax.ShapeDtypeStruct((nb * seq, GROUP_WIDTH), BF16)]
    if emit_ctx:
        out_specs.append(pl.BlockSpec((ctx_len, LANE), lambda b, h: (b, h)))
        out_shape.append(jax.ShapeDtypeStruct((nb * ctx_len, GROUP_WIDTH), BF16))
    return pl.pallas_call(
        functools.partial(_na_kernel, emit_ctx, rows),
        grid=(nb, H),
        in_specs=in_specs, out_specs=out_specs, out_shape=out_shape,
        scratch_shapes=[pltpu.VMEM((seq, LANE), BF16), pltpu.VMEM((seq, LANE), BF16)],
        compiler_params=_cp("parallel", "parallel"),
        name="na_attn",
    )(z, z, z, z, z, z, bias)


def _rms(x, w):
    return x * lax.rsqrt(jnp.mean(x * x, axis=-1, keepdims=True) + RMS_EPS) * w


def _mla_prep_kernel(cq_ref, ckv_ref, kra_ref, krb_ref, cc_ref, ss_ref, qnw_ref, kvnw_ref, wuq_ref, wukv_ref,
                     q_ref, kn_ref, kr_ref, v_ref):
    cc, ss = cc_ref[...], ss_ref[...]
    qn = _rms(cq_ref[...], qnw_ref[...]).astype(BF16)
    qa = _dot(qn, wuq_ref[...])
    for h in range(N_HEADS):
        b = 3 * LANE * h
        q_ref[:, 2 * LANE * h:2 * LANE * h + LANE] = qa[:, b:b + LANE].astype(BF16)
        q_ref[:, 2 * LANE * h + LANE:2 * LANE * (h + 1)] = (
            qa[:, b + LANE:b + 2 * LANE] * cc + qa[:, b + 2 * LANE:b + 3 * LANE] * ss).astype(BF16)
    kvn = _rms(ckv_ref[...], kvnw_ref[...]).astype(BF16)
    kv = _dot(kvn, wukv_ref[...])
    kn_ref[...] = kv[:, :GROUP_WIDTH].astype(BF16)
    v_ref[...] = kv[:, GROUP_WIDTH:].astype(BF16)
    kr_ref[...] = (kra_ref[...] * cc + krb_ref[...] * ss).astype(BF16)


def _mla_prep(z, cc, ss, qnw, kvnw, wuq, wukv, n_lat, seq, tm=512):
    m = z.shape[0]
    nlt, spt = n_lat // tm, seq // tm
    tab = lambda i: jnp.where(i < nlt, i % spt, spt)
    H = N_HEADS
    return pl.pallas_call(
        _mla_prep_kernel,
        grid=(m // tm,),
        in_specs=[pl.BlockSpec((tm, MLA_Q_RANK), lambda i: (i, CB_MQ * LANE // MLA_Q_RANK)),
                  pl.BlockSpec((tm, MLA_KV_RANK), lambda i: (i, CB_MKV * LANE // MLA_KV_RANK)),
                  pl.BlockSpec((tm, LANE), lambda i: (i, CB_MKRA)),
                  pl.BlockSpec((tm, LANE), lambda i: (i, CB_MKRB)),
                  pl.BlockSpec((tm, LANE), lambda i: (tab(i), 0)),
                  pl.BlockSpec((tm, LANE), lambda i: (tab(i), 0)),
                  pl.BlockSpec((1, MLA_Q_RANK), lambda i: (0, 0)),
                  pl.BlockSpec((1, MLA_KV_RANK), lambda i: (0, 0)),
                  pl.BlockSpec(wuq.shape, lambda i: (0, 0)),
                  pl.BlockSpec(wukv.shape, lambda i: (0, 0))],
        out_specs=[pl.BlockSpec((tm, 2 * LANE * H), lambda i: (i, 0)),
                   pl.BlockSpec((tm, GROUP_WIDTH), lambda i: (i, 0)),
                   pl.BlockSpec((tm, LANE), lambda i: (i, 0)),
                   pl.BlockSpec((tm, GROUP_WIDTH), lambda i: (i, 0))],
        out_shape=[jax.ShapeDtypeStruct((m, 2 * LANE * H), BF16),
                   jax.ShapeDtypeStruct((m, GROUP_WIDTH), BF16),
                   jax.ShapeDtypeStruct((m, LANE), BF16),
                   jax.ShapeDtypeStruct((m, GROUP_WIDTH), BF16)],
        compiler_params=_cp("parallel"),
        name="mla_prep",
    )(z, z, z, z, cc, ss, qnw.reshape(1, -1), kvnw.reshape(1, -1), wuq, wukv)


def _mla_attn_kernel(with_lat, seq, *refs):
    if with_lat:
        q_ref, knl_ref, krl_ref, vl_ref, knc_ref, krc_ref, vc_ref, y_ref, k_scr = refs
    else:
        q_ref, knc_ref, krc_ref, vc_ref, y_ref, k_scr = refs
    scale = MLA_QK_DIM ** -0.5
    nk = k_scr.shape[0]

    @pl.when(pl.program_id(2) == 0)
    def _():
        if with_lat:
            k_scr[0:seq, 0:LANE] = knl_ref[...]
            k_scr[0:seq, LANE:2 * LANE] = krl_ref[...]
        k_scr[nk - knc_ref.shape[0]:nk, 0:LANE] = knc_ref[...]
        k_scr[nk - knc_ref.shape[0]:nk, LANE:2 * LANE] = krc_ref[...]

    tq = q_ref.shape[0]
    qsub = min(tq, MLA_QSUB)
    nsub = tq // qsub
    vals = ([vl_ref] if with_lat else []) + [vc_ref]

    def qk(s):
        q = q_ref[s * qsub:(s + 1) * qsub, :]
        out = [_dot_nt(q, k_scr[0:seq, :])] if with_lat else []
        return out + [_dot_nt(q, k_scr[nk - knc_ref.shape[0]:nk, :])]

    def softmax(raw):
        m = raw[0].max(axis=-1, keepdims=True)
        for s in raw[1:]:
            m = jnp.maximum(m, s.max(axis=-1, keepdims=True))
        ps = [jnp.exp2((s - m) * (scale * LOG2E)) for s in raw]
        den = sum(jnp.sum(p, axis=-1, keepdims=True) for p in ps)
        return [p.astype(BF16) for p in ps], den

    raw = qk(0)
    for s in range(nsub):
        nxt = qk(s + 1) if s + 1 < nsub else None
        ps, den = softmax(raw)
        acc = sum(_dot(p, v[...]) for p, v in zip(ps, vals))
        y_ref[s * qsub:(s + 1) * qsub, :] = (acc / den).astype(y_ref.dtype)
        raw = nxt


def _mla_attn(q, kn, kr, v, nb, seq, ctx_len, with_lat, tq=2048):
    H = N_HEADS
    cblk0 = nb * seq // ctx_len
    nq = seq if with_lat else ctx_len
    tq = min(tq, nq)
    qblk0 = 0 if with_lat else nb * seq // tq
    ctxs = [pl.BlockSpec((ctx_len, LANE), lambda b, h, i: (cblk0 + b, h)),
            pl.BlockSpec((ctx_len, LANE), lambda b, h, i: (cblk0 + b, 0)),
            pl.BlockSpec((ctx_len, LANE), lambda b, h, i: (cblk0 + b, h))]
    lats = [pl.BlockSpec((seq, LANE), lambda b, h, i: (b, h)),
            pl.BlockSpec((seq, LANE), lambda b, h, i: (b, 0)),
            pl.BlockSpec((seq, LANE), lambda b, h, i: (b, h))]
    in_specs = [pl.BlockSpec((tq, 2 * LANE), lambda b, h, i: (qblk0 + b * (nq // tq) + i, h))]
    args = [q]
    if with_lat:
        in_specs += lats
        args += [kn, kr, v]
    in_specs += ctxs
    args += [kn, kr, v]
    nk = (seq if with_lat else 0) + ctx_len
    return pl.pallas_call(
        functools.partial(_mla_attn_kernel, with_lat, seq),
        grid=(nb, H, nq // tq),
        in_specs=in_specs,
        out_specs=pl.BlockSpec((tq, LANE), lambda b, h, i: (b * (nq // tq) + i, h)),
        out_shape=jax.ShapeDtypeStruct((nb * nq, GROUP_WIDTH), BF16),
        scratch_shapes=[pltpu.VMEM((nk, 2 * LANE), BF16)],
        compiler_params=_cp("parallel", "parallel", "arbitrary"),
        name="mla_attn_lat" if with_lat else "mla_attn_ctx",
    )(*args)


def _prep_w_in(w_in):
    d = w_in.shape[0]
    gw = GROUP_WIDTH
    o_na = 4 * gw + 4 * N_HEADS
    o_mla = o_na + 3 * gw
    o_kr = o_mla + MLA_Q_RANK + MLA_KV_RANK
    o_hg = o_kr + MLA_ROPE
    kr = w_in[:, o_kr:o_hg]
    k1, k2 = kr[:, 0::2], kr[:, 1::2]
    z = lambda n: jnp.zeros((d, n), w_in.dtype)
    o_kv = o_mla + MLA_Q_RANK
    cols = [w_in[:, o_mla:o_kv], k1, k2, z(LANE - MLA_ROPE),
            w_in[:, o_kv:o_kr], k2, k1, z(LANE - MLA_ROPE),
            w_in[:, 4 * gw:o_na], z(LANE - 4 * N_HEADS),
            w_in[:, :4 * gw], w_in[:, o_na:o_mla], w_in[:, o_hg:]]
    w = jnp.concatenate([c.astype(BF16) for c in cols], axis=1)
    assert w.shape[1] == NP_IN
    return w


def _prep_w_uq(w_uq):
    r = w_uq.shape[0]
    z = jnp.zeros((r, LANE - MLA_ROPE), w_uq.dtype)
    cols = []
    for h in range(N_HEADS):
        wh = w_uq[:, h * MLA_QK_DIM:(h + 1) * MLA_QK_DIM]
        rope = wh[:, MLA_NOPE:]
        r1, r2 = rope[:, 0::2], rope[:, 1::2]
        cols += [wh[:, :MLA_NOPE], r1, r2, z, r2, r1, z]
    return jnp.concatenate(cols, axis=1).astype(BF16)


def _rope_tables(seq, tm):
    n_freq = MLA_ROPE // 4
    freqs = ROPE_BASE ** (-jnp.arange(n_freq, dtype=F32) / n_freq)
    t = jnp.arange(seq)
    ang = jnp.concatenate([(t // GRID_W).astype(F32)[:, None] * freqs,
                           (t % GRID_W).astype(F32)[:, None] * freqs], -1)
    cos, sin = jnp.cos(ang), jnp.sin(ang)
    zp = jnp.zeros((seq, LANE - MLA_ROPE), F32)
    cc = jnp.concatenate([cos, cos, zp], axis=1)
    ss = jnp.concatenate([-sin, sin, zp], axis=1)
    ident = jnp.zeros((tm, LANE), F32).at[:, :MLA_ROPE].set(1.0)
    return jnp.concatenate([cc, ident], axis=0), jnp.concatenate([ss, jnp.zeros((tm, LANE), F32)], axis=0)


def kernel(x, c, ctx, c_ctx, w_ada, b_ada, w_in, gdn_conv_w, gdn_a_log, gdn_dt_bias, gdn_norm_w, na_rpb,
           mla_q_norm_w, mla_kv_norm_w, mla_w_uq, mla_w_uk, mla_w_uv, hgrn_lower_bounds, hgrn_norm_w, w_out,
           ln1_w, ln1_b, w_mlp1, w_mlp2, ln2_w, ln2_b):
    nb, seq, d = x.shape
    ctx_len = ctx.shape[1]
    depth = w_ada.shape[0]
    n_lat, n_ctx = nb * seq, nb * ctx_len
    alpha = (2 * depth) ** 0.25
    tm = 512
    tmm = 1024 if (seq % 1024 == 0 and n_ctx % 1024 == 0) else tm
    assert nb < 8 and seq % tm == 0 and n_ctx % tm == 0 and seq % ctx_len == 0 and ctx_len % CHUNK == 0

    cin = jnp.zeros((8, d), F32).at[:nb].set(c).at[nb].set(c_ctx)
    ada = _ada(cin, w_ada, b_ada)
    p_lb = jax.nn.softmax(hgrn_lower_bounds.astype(F32), axis=0)
    lbs = jnp.cumsum(p_lb, axis=0) - p_lb[0]
    cc, ss = _rope_tables(seq, tm)
    na_bias = _na_bias_tables(na_rpb)

    xs = (x.reshape(n_lat, d), ctx.reshape(n_ctx, d))
    for l in range(depth):
        emit_ctx = l < depth - 1
        ada_r = ada[l].reshape(8 * 6, 1, d)
        z = _inproj(xs, ada_r, _prep_w_in(w_in[l]), n_lat + n_ctx, n_lat, seq, nb, tm=tmm)

        qkv = _gdn_conv(z, gdn_conv_w[l], nb, seq, ctx_len)
        p = _gdn_gates(z, gdn_a_log[l], gdn_dt_bias[l], tm=tm)
        gdn_o = _gdn_scan(*_gdn_prep(qkv, p, nb, seq, ctx_len), nb, seq, ctx_len)
        m_out = n_lat + n_ctx if emit_ctx else n_lat
        yb = _na(z, na_bias[l], nb, seq, ctx_len, emit_ctx)
        q, kn, kr, v = _mla_prep(z, cc, ss, mla_q_norm_w[l], mla_kv_norm_w[l], _prep_w_uq(mla_w_uq[l]),
                                 jnp.concatenate([mla_w_uk[l], mla_w_uv[l]], axis=1).astype(BF16),
                                 n_lat, seq, tm=tm)
        ym = [_mla_attn(q, kn, kr, v, nb, seq, ctx_len, True)]
        if emit_ctx:
            ym.append(_mla_attn(q, kn, kr, v, nb, seq, ctx_len, False))
        hgrn_o = _gla_scan(z, lbs[l], nb, seq, ctx_len)

        if len(xs) == 1 and emit_ctx:
            yb, ym = [jnp.concatenate(yb, axis=0)], [jnp.concatenate(ym, axis=0)]
        x_all = _outproj(xs if emit_ctx else xs[:1], gdn_o, yb, ym, hgrn_o, z, gdn_norm_w[l], hgrn_norm_w[l], w_out[l].astype(BF16),
                         ada_r, ln1_w[l], ln1_b[l], m_out, n_lat, seq, nb, alpha)
        x_all = _mlp(x_all, w_mlp1[l].astype(BF16), w_mlp2[l].astype(BF16), ada_r, ln2_w[l], ln2_b[l],
                     seq, nb, alpha, tm=tm)
        xs = (x_all,)
    return x_all[:n_lat].reshape(nb, seq, d)
```

```python
import functools

import numpy as np
import jax
import jax.numpy as jnp
from jax import lax
from jax.experimental import pallas as pl
from jax.experimental.pallas import tpu as pltpu

F32 = jnp.float32
BF16 = jnp.bfloat16
HIGHEST = lax.Precision.HIGHEST

GRID_W = 64
N_HEADS = 4
HEAD_DIM = 128
GROUP_WIDTH = 512
CHUNK = 64
SUB = 8
GDN_CONV = 5
NA_ROWS = 8
NA_COLS = 16
NA_UNROLL = 4
MLA_Q_RANK = 384
MLA_KV_RANK = 256
MLA_NOPE = 128
MLA_ROPE = 64
MLA_QK_DIM = MLA_NOPE + MLA_ROPE
MLA_QSUB = 512
ROPE_BASE = 10000.0
LN_EPS = 1e-5
RMS_EPS = 1e-6
LOG2E = 1.4426950408889634
NEG = -1e30

LANE = 128
CB_MQ, CB_MKRA, CB_MKV, CB_MKRB, CB_GAB = 0, 3, 4, 6, 7
CB_GQKV, CB_GGATE = 8, 20
CB_NAQ, CB_NAK, CB_NAV = 24, 28, 32
CB_HQ, CB_HFF, CB_HFB, CB_HI, CB_HG = 36, 40, 44, 48, 52
CHUNK_SHIFT = 6
PREP_CHUNKS = 4
NP_IN = 56 * LANE

VMEM_LIMIT = 48 << 20


def _cp(*sem):
    return pltpu.CompilerParams(dimension_semantics=sem, vmem_limit_bytes=VMEM_LIMIT)


def _silu(x):
    return x * jax.nn.sigmoid(x)


def _dot(a, b, **kw):
    return jnp.dot(a, b, preferred_element_type=F32, **kw)


def _dot_nt(a, b, **kw):
    return lax.dot_general(a, b, (((1,), (1,)), ((), ())), preferred_element_type=F32, **kw)


def _dot_tn(a, b, **kw):
    return lax.dot_general(a, b, (((0,), (0,)), ((), ())), preferred_element_type=F32, **kw)


def _ada_kernel(c_ref, w_ref, b_ref, o_ref):
    s = _silu(c_ref[...])
    o_ref[0] = _dot(s, w_ref[0], precision=HIGHEST) + b_ref[0]


def _ada(cin, w_ada, b_ada):
    depth, d, n = w_ada.shape
    tn = 512
    return pl.pallas_call(
        _ada_kernel,
        grid=(depth, n // tn),
        in_specs=[pl.BlockSpec((8, d), lambda l, j: (0, 0)),
                  pl.BlockSpec((1, d, tn), lambda l, j: (l, 0, j)),
                  pl.BlockSpec((1, 1, tn), lambda l, j: (l, 0, j))],
        out_specs=pl.BlockSpec((1, 8, tn), lambda l, j: (l, 0, j)),
        out_shape=jax.ShapeDtypeStruct((depth, 8, n), F32),
        compiler_params=_cp("parallel", "parallel"),
        name="ada",
    )(cin, w_ada, b_ada.reshape(depth, 1, n))


ROW_STEP = 256


def _modulate(x_ref, sh_ref, sc_ref, xm_ref):
    sc1, sh = 1.0 + sc_ref[0], sh_ref[0]

    def body(t, carry):
        sl = pl.ds(pl.multiple_of(t * ROW_STEP, ROW_STEP), ROW_STEP)
        xm_ref[sl, :] = (x_ref[sl, :] * sc1 + sh).astype(BF16)
        return carry

    lax.fori_loop(0, x_ref.shape[0] // ROW_STEP, body, 0)


def _inproj_kernel(n_lat_tiles, *refs):
    xs, (sh_ref, sc_ref, w_ref, o_ref, xm_ref) = refs[:-5], refs[-5:]
    first = pl.program_id(1) == 0
    if len(xs) == 1:
        pl.when(first)(lambda: _modulate(xs[0], sh_ref, sc_ref, xm_ref))
    else:
        is_lat = pl.program_id(0) < n_lat_tiles
        pl.when(first & is_lat)(lambda: _modulate(xs[0], sh_ref, sc_ref, xm_ref))
        pl.when(first & jnp.logical_not(is_lat))(lambda: _modulate(xs[1], sh_ref, sc_ref, xm_ref))
    o_ref[...] = _dot(xm_ref[...], w_ref[...])


def _inproj(xs, ada_r, w, layer, m, n_lat, seq, nb, tm=512, tn=1024):
    _, d, n = w.shape
    nlt = n_lat // tm
    row = lambda i: jnp.minimum((i * tm) // seq, nb)
    if len(xs) == 1:
        x_specs = [pl.BlockSpec((tm, d), lambda i, j: (i, 0))]
    else:
        x_specs = [pl.BlockSpec((tm, d), lambda i, j: (jnp.minimum(i, nlt - 1), 0)),
                   pl.BlockSpec((tm, d), lambda i, j: (jnp.maximum(i - nlt, 0), 0),
                                pipeline_mode=pl.Buffered(1))]
    return pl.pallas_call(
        functools.partial(_inproj_kernel, nlt),
        grid=(m // tm, n // tn),
        in_specs=x_specs + [
                  pl.BlockSpec((1, 1, d), lambda i, j: (row(i) * 6 + 0, 0, 0)),
                  pl.BlockSpec((1, 1, d), lambda i, j: (row(i) * 6 + 1, 0, 0)),
                  pl.BlockSpec((None, d, tn), lambda i, j: (layer, 0, j))],
        out_specs=pl.BlockSpec((tm, tn), lambda i, j: (i, j)),
        out_shape=jax.ShapeDtypeStruct((m, n), F32),
        scratch_shapes=[pltpu.VMEM((tm, d), BF16)],
        compiler_params=_cp("parallel", "arbitrary"),
        name="inproj",
    )(*xs, ada_r, ada_r, w)


def _layernorm(r, w, b):
    mu = jnp.mean(r, axis=-1, keepdims=True)
    rc = r - mu
    var = jnp.mean(rc * rc, axis=-1, keepdims=True)
    return rc * lax.rsqrt(var + LN_EPS) * w + b


def _head_norm_gate(o, nw, gate):
    o = o * lax.rsqrt(jnp.mean(o * o, axis=-1, keepdims=True) + RMS_EPS) * nw
    return o * _silu(gate)


def _scan_mixer_out(of_ref, ob_ref, gate_ref, nw_ref):
    nw = nw_ref[...]
    heads = []
    for h in range(N_HEADS):
        hs = slice(h * LANE, (h + 1) * LANE)
        heads.append(_head_norm_gate(of_ref[:, hs] + ob_ref[:, hs], nw, gate_ref[:, hs]).astype(BF16))
    return jnp.concatenate(heads, axis=-1)


def _outproj_kernel(alpha, n_lat_tiles, n_split, *refs):
    pairs, rest = refs[:n_split], refs[n_split:]
    (gaf_ref, gab_ref, gag_ref, hgf_ref, hgb_ref, hgg_ref, nwa_ref, nwh_ref,
     w_ref, g_ref, lw_ref, lb_ref, o_ref) = rest
    gw = GROUP_WIDTH

    def run(x_ref, yb_ref, ym_ref):
        acc = _dot(_scan_mixer_out(gaf_ref, gab_ref, gag_ref, nwa_ref), w_ref[0:gw, :])
        acc += _dot(yb_ref[...], w_ref[gw:2 * gw, :])
        acc += _dot(ym_ref[...], w_ref[2 * gw:3 * gw, :])
        acc += _dot(_scan_mixer_out(hgf_ref, hgb_ref, hgg_ref, nwh_ref), w_ref[3 * gw:4 * gw, :])
        r = alpha * x_ref[...] + g_ref[0] * acc
        o_ref[...] = _layernorm(r, lw_ref[...], lb_ref[...])

    if n_split == 3:
        run(*pairs)
    else:
        is_lat = pl.program_id(0) < n_lat_tiles
        pl.when(is_lat)(lambda: run(*pairs[0::2]))
        pl.when(jnp.logical_not(is_lat))(lambda: run(*pairs[1::2]))


def _outproj(xs, gdn_o, ybs, yms, hgrn_o, z, nwa, nwh, w, layer, ada_r, lw, lb, m_out, n_lat, seq, nb, alpha,
             tm=256):
    d = w.shape[1]
    gw = GROUP_WIDTH
    nlt = n_lat // tm
    row = lambda i: jnp.minimum((i * tm) // seq, nb)
    in_specs, args = [], []
    dual = len(xs) == 2
    for arrs, width in ((xs, d), (ybs, gw), (yms, gw)):
        if dual:
            in_specs += [pl.BlockSpec((tm, width), lambda i: (jnp.minimum(i, nlt - 1), 0)),
                         pl.BlockSpec((tm, width), lambda i: (jnp.maximum(i - nlt, 0), 0))]
        else:
            in_specs += [pl.BlockSpec((tm, width), lambda i: (i, 0))]
        args += list(arrs)
    n_split = len(args)
    rowspec = pl.BlockSpec((tm, gw), lambda i: (i, 0))
    gate = lambda cb: pl.BlockSpec((tm, gw), lambda i: (i, cb * LANE // gw))
    vec = lambda n: pl.BlockSpec((1, n), lambda i: (0, 0))
    in_specs += [rowspec, rowspec, gate(CB_GGATE), rowspec, rowspec, gate(CB_HG), vec(LANE), vec(LANE),
                 pl.BlockSpec((None, d, d), lambda i: (layer, 0, 0), pipeline_mode=pl.Buffered(1)),
                 pl.BlockSpec((1, 1, d), lambda i: (row(i) * 6 + 2, 0, 0)), vec(d), vec(d)]
    args += [*gdn_o, z, *hgrn_o, z, nwa.reshape(1, LANE), nwh.reshape(1, LANE), w, ada_r,
             lw.reshape(1, d), lb.reshape(1, d)]
    return pl.pallas_call(
        functools.partial(_outproj_kernel, alpha, nlt, n_split),
        grid=(m_out // tm,),
        in_specs=in_specs,
        out_specs=pl.BlockSpec((tm, d), lambda i: (i, 0)),
        out_shape=jax.ShapeDtypeStruct((m_out, d), F32),
        compiler_params=_cp("parallel"),
        name="outproj_ln",
    )(*args)


def _mlp_kernel(alpha, x_ref, sh_ref, sc_ref, g_ref, w1_ref, w2_ref, lw_ref, lb_ref, o_ref, xm_ref, acc_ref):
    k = pl.program_id(1)

    @pl.when(k == 0)
    def _():
        _modulate(x_ref, sh_ref, sc_ref, xm_ref)
        acc_ref[...] = jnp.zeros_like(acc_ref)

    h = jnp.maximum(_dot(xm_ref[...], w1_ref[...]), 0.0)
    acc_ref[...] += _dot((h * h).astype(BF16), w2_ref[...])

    @pl.when(k == pl.num_programs(1) - 1)
    def _():
        g, lw, lb = g_ref[0], lw_ref[...], lb_ref[...]

        def body(t, carry):
            sl = pl.ds(pl.multiple_of(t * ROW_STEP, ROW_STEP), ROW_STEP)
            o_ref[sl, :] = _layernorm(alpha * x_ref[sl, :] + g * acc_ref[sl, :], lw, lb)
            return carry

        lax.fori_loop(0, x_ref.shape[0] // ROW_STEP, body, 0)


def _mlp(x_all, w1, w2, layer, ada_r, lw, lb, seq, nb, alpha, tm=512, th=1024):
    m, d = x_all.shape
    hid = w1.shape[2]
    row = lambda i: jnp.minimum((i * tm) // seq, nb)
    return pl.pallas_call(
        functools.partial(_mlp_kernel, alpha),
        grid=(m // tm, hid // th),
        in_specs=[pl.BlockSpec((tm, d), lambda i, k: (i, 0)),
                  pl.BlockSpec((1, 1, d), lambda i, k: (row(i) * 6 + 3, 0, 0)),
                  pl.BlockSpec((1, 1, d), lambda i, k: (row(i) * 6 + 4, 0, 0)),
                  pl.BlockSpec((1, 1, d), lambda i, k: (row(i) * 6 + 5, 0, 0)),
                  pl.BlockSpec((None, d, th), lambda i, k: (layer, 0, k)),
                  pl.BlockSpec((None, th, d), lambda i, k: (layer, k, 0)),
                  pl.BlockSpec((1, d), lambda i, k: (0, 0)),
                  pl.BlockSpec((1, d), lambda i, k: (0, 0))],
        out_specs=pl.BlockSpec((tm, d), lambda i, k: (i, 0)),
        out_shape=jax.ShapeDtypeStruct((m, d), F32),
        scratch_shapes=[pltpu.VMEM((tm, d), BF16), pltpu.VMEM((tm, d), F32)],
        compiler_params=_cp("parallel", "arbitrary"),
        name="mlp_ln",
    )(x_all, ada_r, ada_r, ada_r, w1, w2, lw.reshape(1, d), lb.reshape(1, d))


def _gdn_conv_kernel(seq, ctx_len, xl_ref, xc_ref, w_ref, o_ref, pad_ref):
    j = pl.program_id(1)
    w = w_ref[...]
    qscale = jnp.where(j < N_HEADS, HEAD_DIM ** -0.5, 1.0).astype(F32)
    p0 = 8 - GDN_CONV // 2
    for x_ref, nrows, o0 in ((xl_ref, seq, 0), (xc_ref, ctx_len, seq)):
        pad_ref[0:8, :] = jnp.zeros((8, LANE), F32)
        pad_ref[nrows + 8:nrows + 16, :] = jnp.zeros((8, LANE), F32)
        pad_ref[8:nrows + 8, :] = x_ref[...]
        rb = min(nrows, 256)
        for r0 in range(0, nrows, rb):
            y = pad_ref[r0 + p0:r0 + p0 + rb, :] * w[0:1, :]
            for i in range(1, GDN_CONV):
                y = y + pad_ref[r0 + p0 + i:r0 + p0 + i + rb, :] * w[i:i + 1, :]
            y = _silu(y)
            nrm = y * lax.rsqrt(jnp.sum(y * y, axis=-1, keepdims=True) + RMS_EPS) * qscale
            o_ref[o0 + r0:o0 + r0 + rb, :] = jnp.where(j < 2 * N_HEADS, nrm, y)


def _gdn_conv(z, conv_w, nb, seq, ctx_len):
    nblk = 3 * N_HEADS
    cblk0 = nb * seq // ctx_len
    return pl.pallas_call(
        functools.partial(_gdn_conv_kernel, seq, ctx_len),
        grid=(nb, nblk),
        in_specs=[pl.BlockSpec((seq, LANE), lambda b, j: (b, CB_GQKV + j)),
                  pl.BlockSpec((ctx_len, LANE), lambda b, j: (cblk0 + b, CB_GQKV + j)),
                  pl.BlockSpec((GDN_CONV, LANE), lambda b, j: (0, j))],
        out_specs=pl.BlockSpec((seq + ctx_len, LANE), lambda b, j: (b, j)),
        out_shape=jax.ShapeDtypeStruct((nb * (seq + ctx_len), nblk * LANE), F32),
        scratch_shapes=[pltpu.VMEM((seq + 16, LANE), F32)],
        compiler_params=_cp("parallel", "parallel"),
        name="gdn_conv",
    )(z, z, conv_w)


def _gdn_gates_kernel(tm, s_ref, alog_ref, dtb_ref, o_ref):
    s = s_ref[...]
    g = -jnp.exp(alog_ref[...]) * (jnp.maximum(s + dtb_ref[...], 0.0)
                                    + jnp.log1p(jnp.exp(-jnp.abs(s + dtb_ref[...]))))
    r = lax.broadcasted_iota(jnp.int32, (tm, tm), 0)
    c = lax.broadcasted_iota(jnp.int32, (tm, tm), 1)
    same = (r >> CHUNK_SHIFT) == (c >> CHUNK_SHIFT)
    lo = jnp.where(same & (c <= r), 1.0, 0.0).astype(F32)
    up = jnp.where(same & (c >= r), 1.0, 0.0).astype(F32)
    cum_f = _dot(lo, g, precision=HIGHEST)
    cum_b = _dot(up, g, precision=HIGHEST)
    col = lax.broadcasted_iota(jnp.int32, s.shape, 1)
    o_ref[...] = jnp.where(col < N_HEADS, cum_f,
                           jnp.where(col < 2 * N_HEADS, cum_b,
                                     jnp.where(col < 4 * N_HEADS, jax.nn.sigmoid(s), 0.0)))


def _gdn_gates(z, a_log, dt_bias, tm=512):
    m = z.shape[0]
    pad = lambda v: jnp.zeros((1, LANE), F32).at[0, :2 * N_HEADS].set(v.reshape(-1).astype(F32))
    return pl.pallas_call(
        functools.partial(_gdn_gates_kernel, tm),
        grid=(m // tm,),
        in_specs=[pl.BlockSpec((tm, LANE), lambda i: (i, CB_GAB)),
                  pl.BlockSpec((1, LANE), lambda i: (0, 0)),
                  pl.BlockSpec((1, LANE), lambda i: (0, 0))],
        out_specs=pl.BlockSpec((tm, LANE), lambda i: (i, 0)),
        out_shape=jax.ShapeDtypeStruct((m, LANE), F32),
        compiler_params=_cp("parallel"),
        name="gdn_gates",
    )(z, pad(a_log), pad(dt_bias))


def _tri_masks(n):
    r = lax.broadcasted_iota(jnp.int32, (n, n), 0)
    c = lax.broadcasted_iota(jnp.int32, (n, n), 1)
    return r, c


def _split(x):
    hi = x.astype(BF16)
    return hi, (x - hi.astype(F32)).astype(BF16)


def _dot3(a, b):
    return _dot(a[0], b[0]) + (_dot(a[0], b[1]) + _dot(a[1], b[0]))


def _gdn_prep_kernel(qkv_ref, p_ref, u_ref, w_ref, qt_ref, kt_ref, att_ref, el_ref):
    C, H = CHUNK, N_HEADS
    lane = lax.broadcasted_iota(jnp.int32, (C, LANE), 1)
    r, c = _tri_masks(C)
    eye = jnp.where(r == c, 1.0, 0.0).astype(F32)
    a_list, rhs_list, where_list = [], [], []
    for n in range(PREP_CHUNKS):
        rs = slice(n * C, (n + 1) * C)
        pblk = p_ref[rs, :]
        tblk = pblk.T
        col = lambda idx, pblk=pblk: jnp.sum(jnp.where(lane == idx, pblk, 0.0), axis=-1, keepdims=True)
        row = lambda idx, tblk=tblk: tblk[idx:idx + 1, :]
        for h in range(H):
            hs = slice(h * LANE, (h + 1) * LANE)
            q = qkv_ref[rs, h * LANE:(h + 1) * LANE]
            k = qkv_ref[rs, (H + h) * LANE:(H + h + 1) * LANE]
            v = qkv_ref[rs, (2 * H + h) * LANE:(2 * H + h + 1) * LANE]
            qbf, kbf = q.astype(BF16), k.astype(BF16)
            for d in range(2):
                idx = d * H + h
                cum_c, cum_r, beta_c = col(idx), row(idx), col(2 * H + idx)
                incl = (c >= r) if d else (c <= r)
                strict = (c > r) if d else (c < r)
                last = cum_r[:, 0:1] if d else cum_r[:, C - 1:C]
                decay = jnp.exp(jnp.where(incl, cum_c - cum_r, NEG))
                kb = k * beta_c
                ec = jnp.exp(cum_c)
                a_list.append(jnp.where(strict, _dot_nt(kb.astype(BF16), kbf) * decay, 0.0))
                rhs_list.append(jnp.concatenate([v * beta_c, kb * ec], axis=-1).astype(BF16))
                where_list.append((d, rs, hs))
                att_ref[d, h, rs, :] = jnp.where(incl, _dot_nt(qbf, kbf) * decay, 0.0).astype(BF16)
                qt_ref[d, rs, hs] = (q * ec).astype(BF16)
                kt_ref[d, rs, hs] = (k * jnp.exp(last - cum_c)).astype(BF16)
                el_ref[n, idx:idx + 1, :] = jnp.broadcast_to(jnp.exp(last), (1, LANE))
    ts = [eye - a for a in a_list]
    ps = [a.astype(BF16) for a in a_list]
    for _ in range(5):
        ps = [_dot(p, p).astype(BF16) for p in ps]
        ts = [t + _dot(t.astype(BF16), p) for t, p in zip(ts, ps)]
    res = [eye - t - _dot3(_split(a), _split(t)) for a, t in zip(a_list, ts)]
    ts = [t + _dot(t.astype(BF16), e.astype(BF16)) for t, e in zip(ts, res)]
    sols = [_dot(t.astype(BF16), rhs) for t, rhs in zip(ts, rhs_list)]
    for sol, (d, rs, hs) in zip(sols, where_list):
        u_ref[d, rs, hs] = sol[:, :HEAD_DIM]
        w_ref[d, rs, hs] = sol[:, HEAD_DIM:].astype(BF16)


def _gdn_prep(qkv, p, nb, seq, ctx_len):
    m = qkv.shape[0]
    H = N_HEADS
    gw = GROUP_WIDTH
    rows = PREP_CHUNKS * CHUNK
    assert seq % rows == 0 and ctx_len % rows == 0
    dspec = pl.BlockSpec((2, rows, gw), lambda i: (0, i, 0))
    nl, nc = seq // rows, ctx_len // rows

    def qkv_blk(i):
        ic = i - nb * nl
        return jnp.where(i < nb * nl, (i // nl) * (nl + nc) + i % nl, (ic // nc) * (nl + nc) + nl + ic % nc)

    return pl.pallas_call(
        _gdn_prep_kernel,
        grid=(m // rows,),
        in_specs=[pl.BlockSpec((rows, 3 * gw), lambda i: (qkv_blk(i), 0)),
                  pl.BlockSpec((rows, LANE), lambda i: (i, 0))],
        out_specs=[dspec, dspec, dspec, dspec,
                   pl.BlockSpec((2, H, rows, CHUNK), lambda i: (0, 0, i, 0)),
                   pl.BlockSpec((PREP_CHUNKS, 2 * H, LANE), lambda i: (i, 0, 0))],
        out_shape=[jax.ShapeDtypeStruct((2, m, gw), F32),
                   jax.ShapeDtypeStruct((2, m, gw), BF16),
                   jax.ShapeDtypeStruct((2, m, gw), BF16),
                   jax.ShapeDtypeStruct((2, m, gw), BF16),
                   jax.ShapeDtypeStruct((2, H, m, CHUNK), BF16),
                   jax.ShapeDtypeStruct((m // CHUNK, 2 * H, LANE), F32)],
        compiler_params=_cp("parallel"),
        name="gdn_prep",
    )(qkv, p)


def _gdn_scan_kernel(g, *refs):
    ins, (of_ref, ob_ref, s_ref) = refs[:12], refs[12:]
    H = N_HEADS

    @pl.when(pl.program_id(1) == 0)
    def _():
        s_ref[...] = jnp.zeros_like(s_ref)

    chains = [(d, h) for d in range(2) for h in range(H)]
    for j in range(g):
        mid = []
        for d, h in chains:
            u_ref, w_ref, qt_ref = ins[d:6:2]
            cj = g - 1 - j if d else j
            rs, hs = slice(cj * CHUNK, (cj + 1) * CHUNK), slice(h * LANE, (h + 1) * LANE)
            sb = s_ref[d * H + h].astype(BF16)
            vnb = (u_ref[0, rs, hs] - _dot(w_ref[0, rs, hs], sb)).astype(BF16)
            mid.append((vnb, _dot(qt_ref[0, rs, hs], sb)))
        for (d, h), (vnb, o_state) in zip(chains, mid):
            kt_ref, att_ref, el_ref = ins[6 + d::2]
            o_ref = ob_ref if d else of_ref
            cj = g - 1 - j if d else j
            rs, hs = slice(cj * CHUNK, (cj + 1) * CHUNK), slice(h * LANE, (h + 1) * LANE)
            idx = d * H + h
            s_ref[idx] = s_ref[idx] * el_ref[cj, idx:idx + 1, :] + _dot_tn(kt_ref[0, rs, hs], vnb)
            o_ref[rs, hs] = o_state + _dot(att_ref[0, h, rs, :], vnb)


def _gdn_scan(u, w, qt, kt, att, el, nb, seq, ctx_len):
    m = u.shape[1]
    H = N_HEADS
    gw = GROUP_WIDTH
    g = ctx_len // CHUNK
    nblk = seq // ctx_len
    cblk0 = nb * nblk
    blk_f = lambda b, t: jnp.where(t == 0, cblk0 + b, b * nblk + t - 1)
    blk_b = lambda b, t: jnp.where(t == 0, cblk0 + b, b * nblk + nblk - t)
    in_specs, args = [], []
    for arr in (u, w, qt, kt):
        in_specs += [pl.BlockSpec((1, ctx_len, gw), lambda b, t: (0, blk_f(b, t), 0)),
                     pl.BlockSpec((1, ctx_len, gw), lambda b, t: (1, blk_b(b, t), 0))]
        args += [arr, arr]
    in_specs += [pl.BlockSpec((1, H, ctx_len, CHUNK), lambda b, t: (0, 0, blk_f(b, t), 0)),
                 pl.BlockSpec((1, H, ctx_len, CHUNK), lambda b, t: (1, 0, blk_b(b, t), 0)),
                 pl.BlockSpec((g, 2 * H, LANE), lambda b, t: (blk_f(b, t), 0, 0)),
                 pl.BlockSpec((g, 2 * H, LANE), lambda b, t: (blk_b(b, t), 0, 0))]
    args += [att, att, el, el]
    return pl.pallas_call(
        functools.partial(_gdn_scan_kernel, g),
        grid=(nb, nblk + 1),
        in_specs=in_specs,
        out_specs=[pl.BlockSpec((ctx_len, gw), lambda b, t: (blk_f(b, t), 0)),
                   pl.BlockSpec((ctx_len, gw), lambda b, t: (blk_b(b, t), 0))],
        out_shape=[jax.ShapeDtypeStruct((m, gw), F32), jax.ShapeDtypeStruct((m, gw), F32)],
        scratch_shapes=[pltpu.VMEM((2 * H, HEAD_DIM, HEAD_DIM), F32)],
        compiler_params=_cp("parallel", "arbitrary"),
        name="gdn_scan",
    )(*args)


def _gla_kernel(g, qf_ref, ff_ref, if_ref, qb_ref, fb_ref, ib_ref, lb_ref, of_ref, ob_ref, s_ref):
    C, H, nsub = CHUNK, N_HEADS, CHUNK // SUB

    @pl.when(pl.program_id(1) == 0)
    def _():
        s_ref[...] = jnp.zeros_like(s_ref)

    r, c = _tri_masks(C)
    tri = (jnp.where(c <= r, 1.0, 0.0).astype(F32), jnp.where(c >= r, 1.0, 0.0).astype(F32))
    trow = lax.broadcasted_iota(jnp.int32, (C, HEAD_DIM), 0)
    srow = lax.broadcasted_iota(jnp.int32, (SUB, C), 0)
    scol = lax.broadcasted_iota(jnp.int32, (SUB, C), 1)
    chains = [(d, h) for d in range(2) for h in range(H)]
    srcs = ((qf_ref, ff_ref, if_ref, of_ref), (qb_ref, fb_ref, ib_ref, ob_ref))

    def body(j, carry):
        ph1 = []
        for d, h in chains:
            cj = g - 1 - j if d else j
            rs, hs = pl.ds(pl.multiple_of(cj * C, C), C), slice(h * LANE, (h + 1) * LANE)
            lb = lb_ref[:, hs]
            f = lb + (1.0 - lb) * jax.nn.sigmoid(srcs[d][1][rs, hs])
            ph1.append((rs, hs, f, _dot(tri[d], jnp.log(f), precision=HIGHEST)))
        ph2 = []
        for (d, h), (rs, hs, f, cum) in zip(chains, ph1):
            q_ref, _, i_ref, _ = srcs[d]
            q = _silu(q_ref[rs, hs]) * HEAD_DIM ** -0.5
            k = 1.0 - f
            vb = i_ref[rs, hs].astype(BF16)
            last = cum[0:1, :] if d else cum[C - 1:C, :]
            idx = d * H + h
            St = s_ref[idx]
            o_state = _dot_nt((q * jnp.exp(cum)).astype(BF16), St.astype(BF16))
            s_ref[idx] = St * jnp.exp(last) + _dot_tn(vb, (k * jnp.exp(last - cum)).astype(BF16))
            inter = []
            for a in range(nsub):
                sa = slice(a * SUB, (a + 1) * SUB)
                if d and a < nsub - 1:
                    cb = cum[(a + 1) * SUB:(a + 1) * SUB + 1, :]
                    kt = k * jnp.exp(jnp.where(trow >= (a + 1) * SUB, cb - cum, NEG))
                elif (not d) and a > 0:
                    cb = cum[a * SUB - 1:a * SUB, :]
                    kt = k * jnp.exp(jnp.where(trow < a * SUB, cb - cum, NEG))
                else:
                    inter.append(jnp.zeros((SUB, C), F32))
                    continue
                inter.append(_dot_nt((q[sa] * jnp.exp(cum[sa] - cb)).astype(BF16), kt.astype(BF16)))
            ph2.append((q, k, cum, vb, o_state, inter))
        for (d, h), (rs, hs, _, _), (q, k, cum, vb, o_state, inter) in zip(chains, ph1, ph2):
            blocks = []
            for a in range(nsub):
                sa = slice(a * SUB, (a + 1) * SUB)
                qa, ka, ca, sc = q[sa], k[sa], cum[sa], inter[a]
                for t in range(SUB):
                    dec = jnp.exp(ca - ca[t:t + 1, :])
                    st = jnp.sum(qa * ka[t:t + 1, :] * dec, axis=-1, keepdims=True)
                    ok = (srow <= t) if d else (srow >= t)
                    sc = jnp.where((scol == a * SUB + t) & ok, st, sc)
                blocks.append(sc)
            scores = jnp.concatenate(blocks, axis=0)
            srcs[d][3][rs, hs] = o_state + _dot(scores.astype(BF16), vb)
        return carry

    lax.fori_loop(0, g, body, 0)


def _gla_scan(z, lbs, nb, seq, ctx_len):
    m = z.shape[0]
    gw = GROUP_WIDTH
    g = ctx_len // CHUNK
    nblk = seq // ctx_len
    cblk0 = nb * nblk
    blk_f = lambda b, t: jnp.where(t == 0, cblk0 + b, b * nblk + t - 1)
    blk_b = lambda b, t: jnp.where(t == 0, cblk0 + b, b * nblk + nblk - t)
    col = lambda cb: cb * LANE // gw
    spec = lambda blk, cb: pl.BlockSpec((ctx_len, gw), lambda b, t: (blk(b, t), col(cb)))
    return pl.pallas_call(
        functools.partial(_gla_kernel, g),
        grid=(nb, nblk + 1),
        in_specs=[spec(blk_f, CB_HQ), spec(blk_f, CB_HFF), spec(blk_f, CB_HI),
                  spec(blk_b, CB_HQ), spec(blk_b, CB_HFB), spec(blk_b, CB_HI),
                  pl.BlockSpec((1, gw), lambda b, t: (0, 0))],
        out_specs=[pl.BlockSpec((ctx_len, gw), lambda b, t: (blk_f(b, t), 0)),
                   pl.BlockSpec((ctx_len, gw), lambda b, t: (blk_b(b, t), 0))],
        out_shape=[jax.ShapeDtypeStruct((m, gw), F32), jax.ShapeDtypeStruct((m, gw), F32)],
        scratch_shapes=[pltpu.VMEM((2 * N_HEADS, HEAD_DIM, HEAD_DIM), F32)],
        compiler_params=_cp("parallel", "arbitrary"),
        name="hgrn_scan",
    )(z, z, z, z, z, z, lbs.reshape(1, gw))


def _softmax_pv(parts):
    m = parts[0][0].max(axis=-1, keepdims=True)
    for s, _ in parts[1:]:
        m = jnp.maximum(m, s.max(axis=-1, keepdims=True))
    den, acc = None, None
    for s, v in parts:
        p = jnp.exp(s - m)
        d = jnp.sum(p, axis=-1, keepdims=True)
        a = _dot(p.astype(BF16), v)
        den = d if den is None else den + d
        acc = a if acc is None else acc + a
    return acc / den


def _na_kernel(emit_ctx, rows, *refs):
    q_ref, k_ref, v_ref, qc_ref, kc_ref, vc_ref, bias_ref = refs[:7]
    if emit_ctx:
        yl_ref, yc_ref, kb_ref, vb_ref = refs[7:]
    else:
        yl_ref, kb_ref, vb_ref = refs[7:]
    scale = HEAD_DIM ** -0.5
    win = NA_ROWS * GRID_W
    kb_ref[...] = k_ref[...].astype(BF16)
    vb_ref[...] = v_ref[...].astype(BF16)
    kc = kc_ref[...].astype(BF16)
    vc = vc_ref[...].astype(BF16)

    def body(i, carry):
        pre = []
        for t in range(NA_UNROLL):
            r = i * NA_UNROLL + t
            row0 = jnp.clip(r - NA_ROWS // 2, 0, rows - NA_ROWS)
            qs = pl.ds(pl.multiple_of(r * GRID_W, GRID_W), GRID_W)
            ks = pl.ds(pl.multiple_of(row0 * GRID_W, GRID_W), win)
            q = q_ref[qs, :].astype(BF16)
            pre.append((qs, ks, _dot_nt(q, kb_ref[ks, :]) * scale + bias_ref[0, r - row0], _dot_nt(q, kc) * scale))
        mid = []
        for qs, ks, s_win, s_ctx in pre:
            m = jnp.maximum(s_win.max(axis=-1, keepdims=True), s_ctx.max(axis=-1, keepdims=True))
            p_win, p_ctx = jnp.exp(s_win - m), jnp.exp(s_ctx - m)
            den = jnp.sum(p_win, axis=-1, keepdims=True) + jnp.sum(p_ctx, axis=-1, keepdims=True)
            mid.append((qs, ks, p_win.astype(BF16), p_ctx.astype(BF16), den))
        for qs, ks, p_win, p_ctx, den in mid:
            yl_ref[qs, :] = ((_dot(p_win, vb_ref[ks, :]) + _dot(p_ctx, vc)) / den).astype(yl_ref.dtype)
        return carry

    lax.fori_loop(0, rows // NA_UNROLL, body, 0)
    if emit_ctx:
        s = _dot_nt(qc_ref[...].astype(BF16), kc) * scale
        yc_ref[...] = _softmax_pv([(s, vc)]).astype(yc_ref.dtype)


def _na_bias_kernel(rpb_ref, o_ref):
    n = lax.broadcasted_iota(jnp.int32, (LANE, GRID_W * GRID_W), 1)
    j = lax.broadcasted_iota(jnp.int32, (LANE, GRID_W * GRID_W), 0)
    q, w = n >> 6, n & (GRID_W - 1)
    dc = jnp.clip(w - q, 1 - NA_COLS, NA_COLS - 1) + NA_COLS - 1
    onehot = jnp.where(dc == j, 1.0, 0.0).astype(F32)
    m = _dot(rpb_ref[...], onehot, precision=HIGHEST)
    c0 = jnp.clip(q[0:1] - NA_COLS // 2, 0, GRID_W - NA_COLS)
    ok = (w[0:1] >= c0) & (w[0:1] < c0 + NA_COLS)
    o_ref[...] = jnp.where(ok, m, NEG)


def _na_bias_tables(rpb):
    depth, H, nr, nc = rpb.shape
    assert GRID_W == 64 and depth * H * nr <= LANE and nc <= LANE
    flat = jnp.zeros((LANE, LANE), F32).at[:depth * H * nr, :nc].set(rpb.reshape(-1, nc).astype(F32))
    m = pl.pallas_call(
        _na_bias_kernel,
        out_shape=jax.ShapeDtypeStruct((LANE, GRID_W * GRID_W), F32),
        compiler_params=pltpu.CompilerParams(vmem_limit_bytes=VMEM_LIMIT),
        name="na_bias",
    )(flat)
    m = m[:depth * H * nr].reshape(depth, H, nr, GRID_W, GRID_W)
    tab = jnp.stack([jnp.stack([m[:, :, k - s + NA_ROWS - 1] for k in range(NA_ROWS)], axis=3)
                     for s in range(NA_ROWS)], axis=2)
    return tab.reshape(depth, H, NA_ROWS, GRID_W, NA_ROWS * GRID_W)


def _na(z, bias, nb, seq, ctx_len, emit_ctx):
    rows = seq // GRID_W
    cblk0 = nb * seq // ctx_len
    H = N_HEADS
    lat = lambda cb: pl.BlockSpec((seq, LANE), lambda b, h: (b, cb + h))
    ctx = lambda cb: pl.BlockSpec((ctx_len, LANE), lambda b, h: (cblk0 + b, cb + h))
    win = NA_ROWS * GRID_W
    in_specs = [lat(CB_NAQ), lat(CB_NAK), lat(CB_NAV), ctx(CB_NAQ), ctx(CB_NAK), ctx(CB_NAV),
                pl.BlockSpec((1, NA_ROWS, GRID_W, win), lambda b, h: (h, 0, 0, 0))]
    out_specs = [pl.BlockSpec((seq, LANE), lambda b, h: (b, h))]
    out_shape = [jax.ShapeDtypeStruct((nb * seq, GROUP_WIDTH), BF16)]
    if emit_ctx:
        out_specs.append(pl.BlockSpec((ctx_len, LANE), lambda b, h: (b, h)))
        out_shape.append(jax.ShapeDtypeStruct((nb * ctx_len, GROUP_WIDTH), BF16))
    return pl.pallas_call(
        functools.partial(_na_kernel, emit_ctx, rows),
        grid=(nb, H),
        in_specs=in_specs, out_specs=out_specs, out_shape=out_shape,
        scratch_shapes=[pltpu.VMEM((seq, LANE), BF16), pltpu.VMEM((seq, LANE), BF16)],
        compiler_params=_cp("parallel", "parallel"),
        name="na_attn",
    )(z, z, z, z, z, z, bias)


def _rms(x, w):
    return x * lax.rsqrt(jnp.mean(x * x, axis=-1, keepdims=True) + RMS_EPS) * w


def _mla_prep_kernel(cq_ref, ckv_ref, kra_ref, krb_ref, cc_ref, ss_ref, qnw_ref, kvnw_ref, wuq_ref, wukv_ref,
                     q_ref, kn_ref, kr_ref, v_ref):
    cc, ss = cc_ref[...], ss_ref[...]
    qn = _rms(cq_ref[...], qnw_ref[...]).astype(BF16)
    qa = _dot(qn, wuq_ref[...])
    for h in range(N_HEADS):
        b = 3 * LANE * h
        q_ref[:, 2 * LANE * h:2 * LANE * h + LANE] = qa[:, b:b + LANE].astype(BF16)
        q_ref[:, 2 * LANE * h + LANE:2 * LANE * (h + 1)] = (
            qa[:, b + LANE:b + 2 * LANE] * cc + qa[:, b + 2 * LANE:b + 3 * LANE] * ss).astype(BF16)
    kvn = _rms(ckv_ref[...], kvnw_ref[...]).astype(BF16)
    kv = _dot(kvn, wukv_ref[...])
    kn_ref[...] = kv[:, :GROUP_WIDTH].astype(BF16)
    v_ref[...] = kv[:, GROUP_WIDTH:].astype(BF16)
    kr_ref[...] = (kra_ref[...] * cc + krb_ref[...] * ss).astype(BF16)


def _mla_prep(z, cc, ss, qnw, kvnw, wuq, wukv, n_lat, seq, tm=512):
    m = z.shape[0]
    nlt, spt = n_lat // tm, seq // tm
    tab = lambda i: jnp.where(i < nlt, i % spt, spt)
    H = N_HEADS
    return pl.pallas_call(
        _mla_prep_kernel,
        grid=(m // tm,),
        in_specs=[pl.BlockSpec((tm, MLA_Q_RANK), lambda i: (i, CB_MQ * LANE // MLA_Q_RANK)),
                  pl.BlockSpec((tm, MLA_KV_RANK), lambda i: (i, CB_MKV * LANE // MLA_KV_RANK)),
                  pl.BlockSpec((tm, LANE), lambda i: (i, CB_MKRA)),
                  pl.BlockSpec((tm, LANE), lambda i: (i, CB_MKRB)),
                  pl.BlockSpec((tm, LANE), lambda i: (tab(i), 0)),
                  pl.BlockSpec((tm, LANE), lambda i: (tab(i), 0)),
                  pl.BlockSpec((1, MLA_Q_RANK), lambda i: (0, 0)),
                  pl.BlockSpec((1, MLA_KV_RANK), lambda i: (0, 0)),
                  pl.BlockSpec(wuq.shape, lambda i: (0, 0)),
                  pl.BlockSpec(wukv.shape, lambda i: (0, 0))],
        out_specs=[pl.BlockSpec((tm, 2 * LANE * H), lambda i: (i, 0)),
                   pl.BlockSpec((tm, GROUP_WIDTH), lambda i: (i, 0)),
                   pl.BlockSpec((tm, LANE), lambda i: (i, 0)),
                   pl.BlockSpec((tm, GROUP_WIDTH), lambda i: (i, 0))],
        out_shape=[jax.ShapeDtypeStruct((m, 2 * LANE * H), BF16),
                   jax.ShapeDtypeStruct((m, GROUP_WIDTH), BF16),
                   jax.ShapeDtypeStruct((m, LANE), BF16),
                   jax.ShapeDtypeStruct((m, GROUP_WIDTH), BF16)],
        compiler_params=_cp("parallel"),
        name="mla_prep",
    )(z, z, z, z, cc, ss, qnw.reshape(1, -1), kvnw.reshape(1, -1), wuq, wukv)


def _mla_attn_kernel(with_lat, seq, *refs):
    if with_lat:
        q_ref, knl_ref, krl_ref, vl_ref, knc_ref, krc_ref, vc_ref, y_ref, k_scr = refs
    else:
        q_ref, knc_ref, krc_ref, vc_ref, y_ref, k_scr = refs
    scale = MLA_QK_DIM ** -0.5
    nk = k_scr.shape[0]

    @pl.when(pl.program_id(2) == 0)
    def _():
        if with_lat:
            k_scr[0:seq, 0:LANE] = knl_ref[...]
            k_scr[0:seq, LANE:2 * LANE] = krl_ref[...]
        k_scr[nk - knc_ref.shape[0]:nk, 0:LANE] = knc_ref[...]
        k_scr[nk - knc_ref.shape[0]:nk, LANE:2 * LANE] = krc_ref[...]

    tq = q_ref.shape[0]
    qsub = min(tq, MLA_QSUB)
    nsub = tq // qsub
    vals = ([vl_ref] if with_lat else []) + [vc_ref]

    def qk(s):
        q = q_ref[s * qsub:(s + 1) * qsub, :]
        out = [_dot_nt(q, k_scr[0:seq, :])] if with_lat else []
        return out + [_dot_nt(q, k_scr[nk - knc_ref.shape[0]:nk, :])]

    def softmax(raw):
        m = raw[0].max(axis=-1, keepdims=True)
        for s in raw[1:]:
            m = jnp.maximum(m, s.max(axis=-1, keepdims=True))
        ps = [jnp.exp2((s - m) * (scale * LOG2E)) for s in raw]
        den = sum(jnp.sum(p, axis=-1, keepdims=True) for p in ps)
        return [p.astype(BF16) for p in ps], den

    raw = qk(0)
    for s in range(nsub):
        nxt = qk(s + 1) if s + 1 < nsub else None
        ps, den = softmax(raw)
        acc = sum(_dot(p, v[...]) for p, v in zip(ps, vals))
        y_ref[s * qsub:(s + 1) * qsub, :] = (acc / den).astype(y_ref.dtype)
        raw = nxt


def _mla_attn(q, kn, kr, v, nb, seq, ctx_len, with_lat, tq=2048):
    H = N_HEADS
    cblk0 = nb * seq // ctx_len
    nq = seq if with_lat else ctx_len
    tq = min(tq, nq)
    qblk0 = 0 if with_lat else nb * seq // tq
    ctxs = [pl.BlockSpec((ctx_len, LANE), lambda b, h, i: (cblk0 + b, h)),
            pl.BlockSpec((ctx_len, LANE), lambda b, h, i: (cblk0 + b, 0)),
            pl.BlockSpec((ctx_len, LANE), lambda b, h, i: (cblk0 + b, h))]
    lats = [pl.BlockSpec((seq, LANE), lambda b, h, i: (b, h)),
            pl.BlockSpec((seq, LANE), lambda b, h, i: (b, 0)),
            pl.BlockSpec((seq, LANE), lambda b, h, i: (b, h))]
    in_specs = [pl.BlockSpec((tq, 2 * LANE), lambda b, h, i: (qblk0 + b * (nq // tq) + i, h))]
    args = [q]
    if with_lat:
        in_specs += lats
        args += [kn, kr, v]
    in_specs += ctxs
    args += [kn, kr, v]
    nk = (seq if with_lat else 0) + ctx_len
    return pl.pallas_call(
        functools.partial(_mla_attn_kernel, with_lat, seq),
        grid=(nb, H, nq // tq),
        in_specs=in_specs,
        out_specs=pl.BlockSpec((tq, LANE), lambda b, h, i: (b * (nq // tq) + i, h)),
        out_shape=jax.ShapeDtypeStruct((nb * nq, GROUP_WIDTH), BF16),
        scratch_shapes=[pltpu.VMEM((nk, 2 * LANE), BF16)],
        compiler_params=_cp("parallel", "parallel", "arbitrary"),
        name="mla_attn_lat" if with_lat else "mla_attn_ctx",
    )(*args)


def _prep_w_in(w_in):
    gw = GROUP_WIDTH
    o_na = 4 * gw + 4 * N_HEADS
    o_mla = o_na + 3 * gw
    o_kv = o_mla + MLA_Q_RANK
    o_kr = o_kv + MLA_KV_RANK
    o_hg = o_kr + MLA_ROPE
    kr = w_in[..., o_kr:o_hg]
    k1, k2 = kr[..., 0::2], kr[..., 1::2]
    half = MLA_ROPE // 2
    pieces = [(CB_MQ * LANE, w_in[..., o_mla:o_kv]),
              (CB_MKRA * LANE, k1), (CB_MKRA * LANE + half, k2),
              (CB_MKV * LANE, w_in[..., o_kv:o_kr]),
              (CB_MKRB * LANE, k2), (CB_MKRB * LANE + half, k1),
              (CB_GAB * LANE, w_in[..., 4 * gw:o_na]),
              (CB_GQKV * LANE, w_in[..., :4 * gw]),
              (CB_NAQ * LANE, w_in[..., o_na:o_mla]),
              (CB_HQ * LANE, w_in[..., o_hg:])]
    w = jnp.zeros(w_in.shape[:-1] + (NP_IN,), BF16)
    for off, piece in pieces:
        w = lax.dynamic_update_slice_in_dim(w, piece.astype(BF16), off, axis=-1)
    return w


def _prep_w_uq(w_uq):
    r = w_uq.shape[0]
    z = jnp.zeros((r, LANE - MLA_ROPE), w_uq.dtype)
    cols = []
    for h in range(N_HEADS):
        wh = w_uq[:, h * MLA_QK_DIM:(h + 1) * MLA_QK_DIM]
        rope = wh[:, MLA_NOPE:]
        r1, r2 = rope[:, 0::2], rope[:, 1::2]
        cols += [wh[:, :MLA_NOPE], r1, r2, z, r2, r1, z]
    return jnp.concatenate(cols, axis=1).astype(BF16)


def _rope_tables(seq, tm):
    n_freq = MLA_ROPE // 4
    freqs = ROPE_BASE ** (-jnp.arange(n_freq, dtype=F32) / n_freq)
    t = jnp.arange(seq)
    ang = jnp.concatenate([(t // GRID_W).astype(F32)[:, None] * freqs,
                           (t % GRID_W).astype(F32)[:, None] * freqs], -1)
    cos, sin = jnp.cos(ang), jnp.sin(ang)
    zp = jnp.zeros((seq, LANE - MLA_ROPE), F32)
    cc = jnp.concatenate([cos, cos, zp], axis=1)
    ss = jnp.concatenate([-sin, sin, zp], axis=1)
    ident = jnp.zeros((tm, LANE), F32).at[:, :MLA_ROPE].set(1.0)
    return jnp.concatenate([cc, ident], axis=0), jnp.concatenate([ss, jnp.zeros((tm, LANE), F32)], axis=0)


def kernel(x, c, ctx, c_ctx, w_ada, b_ada, w_in, gdn_conv_w, gdn_a_log, gdn_dt_bias, gdn_norm_w, na_rpb,
           mla_q_norm_w, mla_kv_norm_w, mla_w_uq, mla_w_uk, mla_w_uv, hgrn_lower_bounds, hgrn_norm_w, w_out,
           ln1_w, ln1_b, w_mlp1, w_mlp2, ln2_w, ln2_b):
    nb, seq, d = x.shape
    ctx_len = ctx.shape[1]
    depth = w_ada.shape[0]
    n_lat, n_ctx = nb * seq, nb * ctx_len
    alpha = (2 * depth) ** 0.25
    tm = 512
    tmm = 1024 if (seq % 1024 == 0 and n_ctx % 1024 == 0) else tm
    assert nb < 8 and seq % tm == 0 and n_ctx % tm == 0 and seq % ctx_len == 0 and ctx_len % CHUNK == 0

    cin = jnp.zeros((8, d), F32).at[:nb].set(c).at[nb].set(c_ctx)
    ada = _ada(cin, w_ada, b_ada)
    p_lb = jax.nn.softmax(hgrn_lower_bounds.astype(F32), axis=0)
    lbs = jnp.cumsum(p_lb, axis=0) - p_lb[0]
    cc, ss = _rope_tables(seq, tm)
    na_bias = _na_bias_tables(na_rpb)
    w_in_b, w_out_b = _prep_w_in(w_in), w_out.astype(BF16)
    w_mlp1_b, w_mlp2_b = w_mlp1.astype(BF16), w_mlp2.astype(BF16)

    xs = (x.reshape(n_lat, d), ctx.reshape(n_ctx, d))
    for l in range(depth):
        emit_ctx = l < depth - 1
        ada_r = ada[l].reshape(8 * 6, 1, d)
        z = _inproj(xs, ada_r, w_in_b, l, n_lat + n_ctx, n_lat, seq, nb, tm=tmm)

        qkv = _gdn_conv(z, gdn_conv_w[l], nb, seq, ctx_len)
        p = _gdn_gates(z, gdn_a_log[l], gdn_dt_bias[l], tm=tm)
        gdn_o = _gdn_scan(*_gdn_prep(qkv, p, nb, seq, ctx_len), nb, seq, ctx_len)
        m_out = n_lat + n_ctx if emit_ctx else n_lat
        yb = _na(z, na_bias[l], nb, seq, ctx_len, emit_ctx)
        q, kn, kr, v = _mla_prep(z, cc, ss, mla_q_norm_w[l], mla_kv_norm_w[l], _prep_w_uq(mla_w_uq[l]),
                                 jnp.concatenate([mla_w_uk[l], mla_w_uv[l]], axis=1).astype(BF16),
                                 n_lat, seq, tm=tm)
        ym = [_mla_attn(q, kn, kr, v, nb, seq, ctx_len, True)]
        if emit_ctx:
            ym.append(_mla_attn(q, kn, kr, v, nb, seq, ctx_len, False))
        hgrn_o = _gla_scan(z, lbs[l], nb, seq, ctx_len)

        if len(xs) == 1 and emit_ctx:
            yb, ym = [jnp.concatenate(yb, axis=0)], [jnp.concatenate(ym, axis=0)]
        x_all = _outproj(xs if emit_ctx else xs[:1], gdn_o, yb, ym, hgrn_o, z, gdn_norm_w[l], hgrn_norm_w[l],
                         w_out_b, l, ada_r, ln1_w[l], ln1_b[l], m_out, n_lat, seq, nb, alpha)
        x_all = _mlp(x_all, w_mlp1_b, w_mlp2_b, l, ada_r, ln2_w[l], ln2_b[l], seq, nb, alpha, tm=tm)
        xs = (x_all,)
    return x_all[:n_lat].reshape(nb, seq, d)
```

```python
import functools

import numpy as np
import jax
import jax.numpy as jnp
from jax import lax
from jax.experimental import pallas as pl
from jax.experimental.pallas import tpu as pltpu

F32 = jnp.float32
BF16 = jnp.bfloat16
HIGHEST = lax.Precision.HIGHEST

GRID_W = 64
N_HEADS = 4
HEAD_DIM = 128
GROUP_WIDTH = 512
CHUNK = 64
SUB = 8
GDN_CONV = 5
NA_ROWS = 8
NA_COLS = 16
NA_UNROLL = 4
MLA_Q_RANK = 384
MLA_KV_RANK = 256
MLA_NOPE = 128
MLA_ROPE = 64
MLA_QK_DIM = MLA_NOPE + MLA_ROPE
MLA_QSUB = 512
ROPE_BASE = 10000.0
LN_EPS = 1e-5
RMS_EPS = 1e-6
LOG2E = 1.4426950408889634
NEG = -1e30

LANE = 128
CB_MQ, CB_MKRA, CB_MKV, CB_MKRB, CB_GAB = 0, 3, 4, 6, 7
CB_GQKV, CB_GGATE = 8, 20
CB_NAQ, CB_NAK, CB_NAV = 24, 28, 32
CB_HQ, CB_HFF, CB_HFB, CB_HI, CB_HG = 36, 40, 44, 48, 52
CHUNK_SHIFT = 6
PREP_CHUNKS = 4
NP_IN = 56 * LANE

VMEM_LIMIT = 48 << 20


def _cp(*sem):
    return pltpu.CompilerParams(dimension_semantics=sem, vmem_limit_bytes=VMEM_LIMIT)


def _silu(x):
    return x * jax.nn.sigmoid(x)


def _dot(a, b, **kw):
    return jnp.dot(a, b, preferred_element_type=F32, **kw)


def _dot_nt(a, b, **kw):
    return lax.dot_general(a, b, (((1,), (1,)), ((), ())), preferred_element_type=F32, **kw)


def _dot_tn(a, b, **kw):
    return lax.dot_general(a, b, (((0,), (0,)), ((), ())), preferred_element_type=F32, **kw)


def _ada_kernel(c_ref, w_ref, b_ref, o_ref):
    s = _silu(c_ref[...])
    o_ref[0] = _dot(s, w_ref[0], precision=HIGHEST) + b_ref[0]


def _ada(cin, w_ada, b_ada):
    depth, d, n = w_ada.shape
    tn = 1024
    return pl.pallas_call(
        _ada_kernel,
        grid=(depth, n // tn),
        in_specs=[pl.BlockSpec((8, d), lambda l, j: (0, 0)),
                  pl.BlockSpec((1, d, tn), lambda l, j: (l, 0, j)),
                  pl.BlockSpec((1, 1, tn), lambda l, j: (l, 0, j))],
        out_specs=pl.BlockSpec((1, 8, tn), lambda l, j: (l, 0, j)),
        out_shape=jax.ShapeDtypeStruct((depth, 8, n), F32),
        compiler_params=_cp("parallel", "parallel"),
        name="ada",
    )(cin, w_ada, b_ada.reshape(depth, 1, n))


ROW_STEP = 256


def _modulate(x_ref, sh_ref, sc_ref, xm_ref):
    sc1, sh = 1.0 + sc_ref[0], sh_ref[0]

    def body(t, carry):
        sl = pl.ds(pl.multiple_of(t * ROW_STEP, ROW_STEP), ROW_STEP)
        xm_ref[sl, :] = (x_ref[sl, :] * sc1 + sh).astype(BF16)
        return carry

    lax.fori_loop(0, x_ref.shape[0] // ROW_STEP, body, 0)


def _inproj_kernel(n_lat_tiles, *refs):
    xs, (sh_ref, sc_ref, w_ref, o_ref, xm_ref) = refs[:-5], refs[-5:]
    first = pl.program_id(1) == 0
    if len(xs) == 1:
        pl.when(first)(lambda: _modulate(xs[0], sh_ref, sc_ref, xm_ref))
    else:
        is_lat = pl.program_id(0) < n_lat_tiles
        pl.when(first & is_lat)(lambda: _modulate(xs[0], sh_ref, sc_ref, xm_ref))
        pl.when(first & jnp.logical_not(is_lat))(lambda: _modulate(xs[1], sh_ref, sc_ref, xm_ref))
    o_ref[...] = _dot(xm_ref[...], w_ref[...])


def _inproj(xs, ada_r, w, layer, m, n_lat, seq, nb, tm=512, tn=1024):
    _, d, n = w.shape
    nlt = n_lat // tm
    row = lambda i: jnp.minimum((i * tm) // seq, nb)
    if len(xs) == 1:
        x_specs = [pl.BlockSpec((tm, d), lambda i, j: (i, 0))]
    else:
        x_specs = [pl.BlockSpec((tm, d), lambda i, j: (jnp.minimum(i, nlt - 1), 0)),
                   pl.BlockSpec((tm, d), lambda i, j: (jnp.maximum(i - nlt, 0), 0),
                                pipeline_mode=pl.Buffered(1))]
    return pl.pallas_call(
        functools.partial(_inproj_kernel, nlt),
        grid=(m // tm, n // tn),
        in_specs=x_specs + [
                  pl.BlockSpec((1, 1, d), lambda i, j: (row(i) * 6 + 0, 0, 0)),
                  pl.BlockSpec((1, 1, d), lambda i, j: (row(i) * 6 + 1, 0, 0)),
                  pl.BlockSpec((None, d, tn), lambda i, j: (layer, 0, j))],
        out_specs=pl.BlockSpec((tm, tn), lambda i, j: (i, j)),
        out_shape=jax.ShapeDtypeStruct((m, n), F32),
        scratch_shapes=[pltpu.VMEM((tm, d), BF16)],
        compiler_params=_cp("parallel", "arbitrary"),
        name="inproj",
    )(*xs, ada_r, ada_r, w)


def _layernorm(r, w, b):
    mu = jnp.mean(r, axis=-1, keepdims=True)
    rc = r - mu
    var = jnp.mean(rc * rc, axis=-1, keepdims=True)
    return rc * lax.rsqrt(var + LN_EPS) * w + b


def _head_norm_gate(o, nw, gate):
    o = o * lax.rsqrt(jnp.mean(o * o, axis=-1, keepdims=True) + RMS_EPS) * nw
    return o * _silu(gate)


def _scan_mixer_out(of_ref, ob_ref, gate_ref, nw_ref):
    nw = nw_ref[...]
    heads = []
    for h in range(N_HEADS):
        hs = slice(h * LANE, (h + 1) * LANE)
        heads.append(_head_norm_gate(of_ref[:, hs] + ob_ref[:, hs], nw, gate_ref[:, hs]).astype(BF16))
    return jnp.concatenate(heads, axis=-1)


def _outproj_kernel(alpha, n_lat_tiles, n_split, *refs):
    pairs, rest = refs[:n_split], refs[n_split:]
    (gaf_ref, gab_ref, gag_ref, hgf_ref, hgb_ref, hgg_ref, nwa_ref, nwh_ref,
     w_ref, g_ref, lw_ref, lb_ref, o_ref) = rest
    gw = GROUP_WIDTH

    def run(x_ref, yb_ref, ym_ref):
        acc = _dot(_scan_mixer_out(gaf_ref, gab_ref, gag_ref, nwa_ref), w_ref[0:gw, :])
        acc += _dot(yb_ref[...], w_ref[gw:2 * gw, :])
        acc += _dot(ym_ref[...], w_ref[2 * gw:3 * gw, :])
        acc += _dot(_scan_mixer_out(hgf_ref, hgb_ref, hgg_ref, nwh_ref), w_ref[3 * gw:4 * gw, :])
        r = alpha * x_ref[...] + g_ref[0] * acc
        o_ref[...] = _layernorm(r, lw_ref[...], lb_ref[...])

    if n_split == 3:
        run(*pairs)
    else:
        is_lat = pl.program_id(0) < n_lat_tiles
        pl.when(is_lat)(lambda: run(*pairs[0::2]))
        pl.when(jnp.logical_not(is_lat))(lambda: run(*pairs[1::2]))


def _outproj(xs, gdn_o, ybs, yms, hgrn_o, z, nwa, nwh, w, layer, ada_r, lw, lb, m_out, n_lat, seq, nb, alpha,
             tm=256):
    d = w.shape[1]
    gw = GROUP_WIDTH
    nlt = n_lat // tm
    row = lambda i: jnp.minimum((i * tm) // seq, nb)
    in_specs, args = [], []
    dual = len(xs) == 2
    for arrs, width in ((xs, d), (ybs, gw), (yms, gw)):
        if dual:
            in_specs += [pl.BlockSpec((tm, width), lambda i: (jnp.minimum(i, nlt - 1), 0)),
                         pl.BlockSpec((tm, width), lambda i: (jnp.maximum(i - nlt, 0), 0))]
        else:
            in_specs += [pl.BlockSpec((tm, width), lambda i: (i, 0))]
        args += list(arrs)
    n_split = len(args)
    rowspec = pl.BlockSpec((tm, gw), lambda i: (i, 0))
    gate = lambda cb: pl.BlockSpec((tm, gw), lambda i: (i, cb * LANE // gw))
    vec = lambda n: pl.BlockSpec((1, n), lambda i: (0, 0))
    in_specs += [rowspec, rowspec, gate(CB_GGATE), rowspec, rowspec, gate(CB_HG), vec(LANE), vec(LANE),
                 pl.BlockSpec((None, d, d), lambda i: (layer, 0, 0), pipeline_mode=pl.Buffered(1)),
                 pl.BlockSpec((1, 1, d), lambda i: (row(i) * 6 + 2, 0, 0)), vec(d), vec(d)]
    args += [*gdn_o, z, *hgrn_o, z, nwa.reshape(1, LANE), nwh.reshape(1, LANE), w, ada_r,
             lw.reshape(1, d), lb.reshape(1, d)]
    return pl.pallas_call(
        functools.partial(_outproj_kernel, alpha, nlt, n_split),
        grid=(m_out // tm,),
        in_specs=in_specs,
        out_specs=pl.BlockSpec((tm, d), lambda i: (i, 0)),
        out_shape=jax.ShapeDtypeStruct((m_out, d), F32),
        compiler_params=_cp("parallel"),
        name="outproj_ln",
    )(*args)


def _mlp_kernel(alpha, x_ref, sh_ref, sc_ref, g_ref, w1_ref, w2_ref, lw_ref, lb_ref, o_ref, xm_ref, acc_ref):
    k = pl.program_id(1)

    @pl.when(k == 0)
    def _():
        _modulate(x_ref, sh_ref, sc_ref, xm_ref)
        acc_ref[...] = jnp.zeros_like(acc_ref)

    h = jnp.maximum(_dot(xm_ref[...], w1_ref[...]), 0.0)
    acc_ref[...] += _dot((h * h).astype(BF16), w2_ref[...])

    @pl.when(k == pl.num_programs(1) - 1)
    def _():
        g, lw, lb = g_ref[0], lw_ref[...], lb_ref[...]

        def body(t, carry):
            sl = pl.ds(pl.multiple_of(t * ROW_STEP, ROW_STEP), ROW_STEP)
            o_ref[sl, :] = _layernorm(alpha * x_ref[sl, :] + g * acc_ref[sl, :], lw, lb)
            return carry

        lax.fori_loop(0, x_ref.shape[0] // ROW_STEP, body, 0)


def _mlp(x_all, w1, w2, layer, ada_r, lw, lb, seq, nb, alpha, tm=512, th=1024):
    m, d = x_all.shape
    hid = w1.shape[2]
    row = lambda i: jnp.minimum((i * tm) // seq, nb)
    return pl.pallas_call(
        functools.partial(_mlp_kernel, alpha),
        grid=(m // tm, hid // th),
        in_specs=[pl.BlockSpec((tm, d), lambda i, k: (i, 0)),
                  pl.BlockSpec((1, 1, d), lambda i, k: (row(i) * 6 + 3, 0, 0)),
                  pl.BlockSpec((1, 1, d), lambda i, k: (row(i) * 6 + 4, 0, 0)),
                  pl.BlockSpec((1, 1, d), lambda i, k: (row(i) * 6 + 5, 0, 0)),
                  pl.BlockSpec((None, d, th), lambda i, k: (layer, 0, k)),
                  pl.BlockSpec((None, th, d), lambda i, k: (layer, k, 0)),
                  pl.BlockSpec((1, d), lambda i, k: (0, 0)),
                  pl.BlockSpec((1, d), lambda i, k: (0, 0))],
        out_specs=pl.BlockSpec((tm, d), lambda i, k: (i, 0)),
        out_shape=jax.ShapeDtypeStruct((m, d), F32),
        scratch_shapes=[pltpu.VMEM((tm, d), BF16), pltpu.VMEM((tm, d), F32)],
        compiler_params=_cp("parallel", "arbitrary"),
        name="mlp_ln",
    )(x_all, ada_r, ada_r, ada_r, w1, w2, lw.reshape(1, d), lb.reshape(1, d))


def _gdn_conv_kernel(seq, ctx_len, xl_ref, xc_ref, w_ref, o_ref, pad_ref):
    j = pl.program_id(1)
    w = w_ref[...]
    qscale = jnp.where(j < N_HEADS, HEAD_DIM ** -0.5, 1.0).astype(F32)
    p0 = 8 - GDN_CONV // 2
    for x_ref, nrows, o0 in ((xl_ref, seq, 0), (xc_ref, ctx_len, seq)):
        pad_ref[0:8, :] = jnp.zeros((8, LANE), F32)
        pad_ref[nrows + 8:nrows + 16, :] = jnp.zeros((8, LANE), F32)
        pad_ref[8:nrows + 8, :] = x_ref[...]
        rb = min(nrows, 256)
        for r0 in range(0, nrows, rb):
            y = pad_ref[r0 + p0:r0 + p0 + rb, :] * w[0:1, :]
            for i in range(1, GDN_CONV):
                y = y + pad_ref[r0 + p0 + i:r0 + p0 + i + rb, :] * w[i:i + 1, :]
            y = _silu(y)
            nrm = y * lax.rsqrt(jnp.sum(y * y, axis=-1, keepdims=True) + RMS_EPS) * qscale
            o_ref[o0 + r0:o0 + r0 + rb, :] = jnp.where(j < 2 * N_HEADS, nrm, y)


def _gdn_conv(z, conv_w, nb, seq, ctx_len):
    nblk = 3 * N_HEADS
    cblk0 = nb * seq // ctx_len
    return pl.pallas_call(
        functools.partial(_gdn_conv_kernel, seq, ctx_len),
        grid=(nb, nblk),
        in_specs=[pl.BlockSpec((seq, LANE), lambda b, j: (b, CB_GQKV + j)),
                  pl.BlockSpec((ctx_len, LANE), lambda b, j: (cblk0 + b, CB_GQKV + j)),
                  pl.BlockSpec((GDN_CONV, LANE), lambda b, j: (0, j))],
        out_specs=pl.BlockSpec((seq + ctx_len, LANE), lambda b, j: (b, j)),
        out_shape=jax.ShapeDtypeStruct((nb * (seq + ctx_len), nblk * LANE), F32),
        scratch_shapes=[pltpu.VMEM((seq + 16, LANE), F32)],
        compiler_params=_cp("parallel", "parallel"),
        name="gdn_conv",
    )(z, z, conv_w)


def _gdn_gates_kernel(tm, s_ref, alog_ref, dtb_ref, o_ref):
    s = s_ref[...]
    g = -jnp.exp(alog_ref[...]) * (jnp.maximum(s + dtb_ref[...], 0.0)
                                    + jnp.log1p(jnp.exp(-jnp.abs(s + dtb_ref[...]))))
    r = lax.broadcasted_iota(jnp.int32, (tm, tm), 0)
    c = lax.broadcasted_iota(jnp.int32, (tm, tm), 1)
    same = (r >> CHUNK_SHIFT) == (c >> CHUNK_SHIFT)
    lo = jnp.where(same & (c <= r), 1.0, 0.0).astype(F32)
    up = jnp.where(same & (c >= r), 1.0, 0.0).astype(F32)
    cum_f = _dot(lo, g, precision=HIGHEST)
    cum_b = _dot(up, g, precision=HIGHEST)
    col = lax.broadcasted_iota(jnp.int32, s.shape, 1)
    o_ref[...] = jnp.where(col < N_HEADS, cum_f,
                           jnp.where(col < 2 * N_HEADS, cum_b,
                                     jnp.where(col < 4 * N_HEADS, jax.nn.sigmoid(s), 0.0)))


def _gdn_gates(z, a_log, dt_bias, tm=512):
    m = z.shape[0]
    pad = lambda v: jnp.zeros((1, LANE), F32).at[0, :2 * N_HEADS].set(v.reshape(-1).astype(F32))
    return pl.pallas_call(
        functools.partial(_gdn_gates_kernel, tm),
        grid=(m // tm,),
        in_specs=[pl.BlockSpec((tm, LANE), lambda i: (i, CB_GAB)),
                  pl.BlockSpec((1, LANE), lambda i: (0, 0)),
                  pl.BlockSpec((1, LANE), lambda i: (0, 0))],
        out_specs=pl.BlockSpec((tm, LANE), lambda i: (i, 0)),
        out_shape=jax.ShapeDtypeStruct((m, LANE), F32),
        compiler_params=_cp("parallel"),
        name="gdn_gates",
    )(z, pad(a_log), pad(dt_bias))


def _tri_masks(n):
    r = lax.broadcasted_iota(jnp.int32, (n, n), 0)
    c = lax.broadcasted_iota(jnp.int32, (n, n), 1)
    return r, c


def _split(x):
    hi = x.astype(BF16)
    return hi, (x - hi.astype(F32)).astype(BF16)


def _dot3(a, b):
    return _dot(a[0], b[0]) + (_dot(a[0], b[1]) + _dot(a[1], b[0]))


def _gdn_prep_kernel(qkv_ref, p_ref, u_ref, w_ref, qt_ref, kt_ref, att_ref, el_ref):
    C, H = CHUNK, N_HEADS
    lane = lax.broadcasted_iota(jnp.int32, (C, LANE), 1)
    r, c = _tri_masks(C)
    eye = jnp.where(r == c, 1.0, 0.0).astype(F32)
    a_list, rhs_list, where_list = [], [], []
    for n in range(PREP_CHUNKS):
        rs = slice(n * C, (n + 1) * C)
        pblk = p_ref[rs, :]
        tblk = pblk.T
        col = lambda idx, pblk=pblk: jnp.sum(jnp.where(lane == idx, pblk, 0.0), axis=-1, keepdims=True)
        row = lambda idx, tblk=tblk: tblk[idx:idx + 1, :]
        for h in range(H):
            hs = slice(h * LANE, (h + 1) * LANE)
            q = qkv_ref[rs, h * LANE:(h + 1) * LANE]
            k = qkv_ref[rs, (H + h) * LANE:(H + h + 1) * LANE]
            v = qkv_ref[rs, (2 * H + h) * LANE:(2 * H + h + 1) * LANE]
            qbf, kbf = q.astype(BF16), k.astype(BF16)
            for d in range(2):
                idx = d * H + h
                cum_c, cum_r, beta_c = col(idx), row(idx), col(2 * H + idx)
                incl = (c >= r) if d else (c <= r)
                strict = (c > r) if d else (c < r)
                last = cum_r[:, 0:1] if d else cum_r[:, C - 1:C]
                decay = jnp.exp(jnp.where(incl, cum_c - cum_r, NEG))
                kb = k * beta_c
                ec = jnp.exp(cum_c)
                a_list.append(jnp.where(strict, _dot_nt(kb.astype(BF16), kbf) * decay, 0.0))
                rhs_list.append(jnp.concatenate([v * beta_c, kb * ec], axis=-1).astype(BF16))
                where_list.append((d, rs, hs))
                att_ref[d, h, rs, :] = jnp.where(incl, _dot_nt(qbf, kbf) * decay, 0.0).astype(BF16)
                qt_ref[d, rs, hs] = (q * ec).astype(BF16)
                kt_ref[d, rs, hs] = (k * jnp.exp(last - cum_c)).astype(BF16)
                el_ref[n, idx:idx + 1, :] = jnp.broadcast_to(jnp.exp(last), (1, LANE))
    ts = [eye - a for a in a_list]
    ps = [a.astype(BF16) for a in a_list]
    for _ in range(5):
        ps = [_dot(p, p).astype(BF16) for p in ps]
        ts = [t + _dot(t.astype(BF16), p) for t, p in zip(ts, ps)]
    res = [eye - t - _dot3(_split(a), _split(t)) for a, t in zip(a_list, ts)]
    ts = [t + _dot(t.astype(BF16), e.astype(BF16)) for t, e in zip(ts, res)]
    sols = [_dot(t.astype(BF16), rhs) for t, rhs in zip(ts, rhs_list)]
    for sol, (d, rs, hs) in zip(sols, where_list):
        u_ref[d, rs, hs] = sol[:, :HEAD_DIM]
        w_ref[d, rs, hs] = sol[:, HEAD_DIM:].astype(BF16)


def _gdn_prep(qkv, p, nb, seq, ctx_len):
    m = qkv.shape[0]
    H = N_HEADS
    gw = GROUP_WIDTH
    rows = PREP_CHUNKS * CHUNK
    assert seq % rows == 0 and ctx_len % rows == 0
    dspec = pl.BlockSpec((2, rows, gw), lambda i: (0, i, 0))
    nl, nc = seq // rows, ctx_len // rows

    def qkv_blk(i):
        ic = i - nb * nl
        return jnp.where(i < nb * nl, (i // nl) * (nl + nc) + i % nl, (ic // nc) * (nl + nc) + nl + ic % nc)

    return pl.pallas_call(
        _gdn_prep_kernel,
        grid=(m // rows,),
        in_specs=[pl.BlockSpec((rows, 3 * gw), lambda i: (qkv_blk(i), 0)),
                  pl.BlockSpec((rows, LANE), lambda i: (i, 0))],
        out_specs=[dspec, dspec, dspec, dspec,
                   pl.BlockSpec((2, H, rows, CHUNK), lambda i: (0, 0, i, 0)),
                   pl.BlockSpec((PREP_CHUNKS, 2 * H, LANE), lambda i: (i, 0, 0))],
        out_shape=[jax.ShapeDtypeStruct((2, m, gw), F32),
                   jax.ShapeDtypeStruct((2, m, gw), BF16),
                   jax.ShapeDtypeStruct((2, m, gw), BF16),
                   jax.ShapeDtypeStruct((2, m, gw), BF16),
                   jax.ShapeDtypeStruct((2, H, m, CHUNK), BF16),
                   jax.ShapeDtypeStruct((m // CHUNK, 2 * H, LANE), F32)],
        compiler_params=_cp("parallel"),
        name="gdn_prep",
    )(qkv, p)


def _gdn_scan_kernel(g, *refs):
    ins, (of_ref, ob_ref, s_ref) = refs[:12], refs[12:]
    H = N_HEADS

    @pl.when(pl.program_id(1) == 0)
    def _():
        s_ref[...] = jnp.zeros_like(s_ref)

    chains = [(d, h) for d in range(2) for h in range(H)]
    for j in range(g):
        mid = []
        for d, h in chains:
            u_ref, w_ref, qt_ref = ins[d:6:2]
            cj = g - 1 - j if d else j
            rs, hs = slice(cj * CHUNK, (cj + 1) * CHUNK), slice(h * LANE, (h + 1) * LANE)
            sb = s_ref[d * H + h].astype(BF16)
            vnb = (u_ref[0, rs, hs] - _dot(w_ref[0, rs, hs], sb)).astype(BF16)
            mid.append((vnb, _dot(qt_ref[0, rs, hs], sb)))
        for (d, h), (vnb, o_state) in zip(chains, mid):
            kt_ref, att_ref, el_ref = ins[6 + d::2]
            o_ref = ob_ref if d else of_ref
            cj = g - 1 - j if d else j
            rs, hs = slice(cj * CHUNK, (cj + 1) * CHUNK), slice(h * LANE, (h + 1) * LANE)
            idx = d * H + h
            s_ref[idx] = s_ref[idx] * el_ref[cj, idx:idx + 1, :] + _dot_tn(kt_ref[0, rs, hs], vnb)
            o_ref[rs, hs] = o_state + _dot(att_ref[0, h, rs, :], vnb)


def _gdn_scan(u, w, qt, kt, att, el, nb, seq, ctx_len):
    m = u.shape[1]
    H = N_HEADS
    gw = GROUP_WIDTH
    g = ctx_len // CHUNK
    nblk = seq // ctx_len
    cblk0 = nb * nblk
    blk_f = lambda b, t: jnp.where(t == 0, cblk0 + b, b * nblk + t - 1)
    blk_b = lambda b, t: jnp.where(t == 0, cblk0 + b, b * nblk + nblk - t)
    in_specs, args = [], []
    for arr in (u, w, qt, kt):
        in_specs += [pl.BlockSpec((1, ctx_len, gw), lambda b, t: (0, blk_f(b, t), 0)),
                     pl.BlockSpec((1, ctx_len, gw), lambda b, t: (1, blk_b(b, t), 0))]
        args += [arr, arr]
    in_specs += [pl.BlockSpec((1, H, ctx_len, CHUNK), lambda b, t: (0, 0, blk_f(b, t), 0)),
                 pl.BlockSpec((1, H, ctx_len, CHUNK), lambda b, t: (1, 0, blk_b(b, t), 0)),
                 pl.BlockSpec((g, 2 * H, LANE), lambda b, t: (blk_f(b, t), 0, 0)),
                 pl.BlockSpec((g, 2 * H, LANE), lambda b, t: (blk_b(b, t), 0, 0))]
    args += [att, att, el, el]
    return pl.pallas_call(
        functools.partial(_gdn_scan_kernel, g),
        grid=(nb, nblk + 1),
        in_specs=in_specs,
        out_specs=[pl.BlockSpec((ctx_len, gw), lambda b, t: (blk_f(b, t), 0)),
                   pl.BlockSpec((ctx_len, gw), lambda b, t: (blk_b(b, t), 0))],
        out_shape=[jax.ShapeDtypeStruct((m, gw), F32), jax.ShapeDtypeStruct((m, gw), F32)],
        scratch_shapes=[pltpu.VMEM((2 * H, HEAD_DIM, HEAD_DIM), F32)],
        compiler_params=_cp("parallel", "arbitrary"),
        name="gdn_scan",
    )(*args)


def _gla_kernel(g, qf_ref, ff_ref, if_ref, qb_ref, fb_ref, ib_ref, lb_ref, of_ref, ob_ref, s_ref):
    C, H, nsub = CHUNK, N_HEADS, CHUNK // SUB

    @pl.when(pl.program_id(1) == 0)
    def _():
        s_ref[...] = jnp.zeros_like(s_ref)

    r, c = _tri_masks(C)
    tri = (jnp.where(c <= r, 1.0, 0.0).astype(F32), jnp.where(c >= r, 1.0, 0.0).astype(F32))
    trow = lax.broadcasted_iota(jnp.int32, (C, HEAD_DIM), 0)
    srow = lax.broadcasted_iota(jnp.int32, (SUB, C), 0)
    scol = lax.broadcasted_iota(jnp.int32, (SUB, C), 1)
    chains = [(d, h) for d in range(2) for h in range(H)]
    srcs = ((qf_ref, ff_ref, if_ref, of_ref), (qb_ref, fb_ref, ib_ref, ob_ref))

    def body(j, carry):
        ph1 = []
        for d, h in chains:
            cj = g - 1 - j if d else j
            rs, hs = pl.ds(pl.multiple_of(cj * C, C), C), slice(h * LANE, (h + 1) * LANE)
            lb = lb_ref[:, hs]
            f = lb + (1.0 - lb) * jax.nn.sigmoid(srcs[d][1][rs, hs])
            ph1.append((rs, hs, f, _dot(tri[d], jnp.log(f), precision=HIGHEST)))
        ph2 = []
        for (d, h), (rs, hs, f, cum) in zip(chains, ph1):
            q_ref, _, i_ref, _ = srcs[d]
            q = _silu(q_ref[rs, hs]) * HEAD_DIM ** -0.5
            k = 1.0 - f
            vb = i_ref[rs, hs].astype(BF16)
            last = cum[0:1, :] if d else cum[C - 1:C, :]
            idx = d * H + h
            St = s_ref[idx]
            o_state = _dot_nt((q * jnp.exp(cum)).astype(BF16), St.astype(BF16))
            s_ref[idx] = St * jnp.exp(last) + _dot_tn(vb, (k * jnp.exp(last - cum)).astype(BF16))
            inter = []
            for a in range(nsub):
                sa = slice(a * SUB, (a + 1) * SUB)
                if d and a < nsub - 1:
                    cb = cum[(a + 1) * SUB:(a + 1) * SUB + 1, :]
                    kt = k * jnp.exp(jnp.where(trow >= (a + 1) * SUB, cb - cum, NEG))
                elif (not d) and a > 0:
                    cb = cum[a * SUB - 1:a * SUB, :]
                    kt = k * jnp.exp(jnp.where(trow < a * SUB, cb - cum, NEG))
                else:
                    inter.append(jnp.zeros((SUB, C), F32))
                    continue
                inter.append(_dot_nt((q[sa] * jnp.exp(cum[sa] - cb)).astype(BF16), kt.astype(BF16)))
            ph2.append((q, k, cum, vb, o_state, inter))
        for (d, h), (rs, hs, _, _), (q, k, cum, vb, o_state, inter) in zip(chains, ph1, ph2):
            blocks = []
            for a in range(nsub):
                sa = slice(a * SUB, (a + 1) * SUB)
                qa, ka, ca, sc = q[sa], k[sa], cum[sa], inter[a]
                for t in range(SUB):
                    dec = jnp.exp(ca - ca[t:t + 1, :])
                    st = jnp.sum(qa * ka[t:t + 1, :] * dec, axis=-1, keepdims=True)
                    ok = (srow <= t) if d else (srow >= t)
                    sc = jnp.where((scol == a * SUB + t) & ok, st, sc)
                blocks.append(sc)
            scores = jnp.concatenate(blocks, axis=0)
            srcs[d][3][rs, hs] = o_state + _dot(scores.astype(BF16), vb)
        return carry

    lax.fori_loop(0, g, body, 0)


def _gla_scan(z, lbs, nb, seq, ctx_len):
    m = z.shape[0]
    gw = GROUP_WIDTH
    g = ctx_len // CHUNK
    nblk = seq // ctx_len
    cblk0 = nb * nblk
    blk_f = lambda b, t: jnp.where(t == 0, cblk0 + b, b * nblk + t - 1)
    blk_b = lambda b, t: jnp.where(t == 0, cblk0 + b, b * nblk + nblk - t)
    col = lambda cb: cb * LANE // gw
    spec = lambda blk, cb: pl.BlockSpec((ctx_len, gw), lambda b, t: (blk(b, t), col(cb)))
    return pl.pallas_call(
        functools.partial(_gla_kernel, g),
        grid=(nb, nblk + 1),
        in_specs=[spec(blk_f, CB_HQ), spec(blk_f, CB_HFF), spec(blk_f, CB_HI),
                  spec(blk_b, CB_HQ), spec(blk_b, CB_HFB), spec(blk_b, CB_HI),
                  pl.BlockSpec((1, gw), lambda b, t: (0, 0))],
        out_specs=[pl.BlockSpec((ctx_len, gw), lambda b, t: (blk_f(b, t), 0)),
                   pl.BlockSpec((ctx_len, gw), lambda b, t: (blk_b(b, t), 0))],
        out_shape=[jax.ShapeDtypeStruct((m, gw), F32), jax.ShapeDtypeStruct((m, gw), F32)],
        scratch_shapes=[pltpu.VMEM((2 * N_HEADS, HEAD_DIM, HEAD_DIM), F32)],
        compiler_params=_cp("parallel", "arbitrary"),
        name="hgrn_scan",
    )(z, z, z, z, z, z, lbs.reshape(1, gw))


def _softmax_pv(parts):
    m = parts[0][0].max(axis=-1, keepdims=True)
    for s, _ in parts[1:]:
        m = jnp.maximum(m, s.max(axis=-1, keepdims=True))
    den, acc = None, None
    for s, v in parts:
        p = jnp.exp(s - m)
        d = jnp.sum(p, axis=-1, keepdims=True)
        a = _dot(p.astype(BF16), v)
        den = d if den is None else den + d
        acc = a if acc is None else acc + a
    return acc / den


def _na_kernel(emit_ctx, rows, *refs):
    q_ref, k_ref, v_ref, qc_ref, kc_ref, vc_ref, bias_ref = refs[:7]
    if emit_ctx:
        yl_ref, yc_ref, kb_ref, vb_ref = refs[7:]
    else:
        yl_ref, kb_ref, vb_ref = refs[7:]
    scale = HEAD_DIM ** -0.5
    win = NA_ROWS * GRID_W
    kb_ref[...] = k_ref[...].astype(BF16)
    vb_ref[...] = v_ref[...].astype(BF16)
    kc = kc_ref[...].astype(BF16)
    vc = vc_ref[...].astype(BF16)

    def body(i, carry):
        pre = []
        for t in range(NA_UNROLL):
            r = i * NA_UNROLL + t
            row0 = jnp.clip(r - NA_ROWS // 2, 0, rows - NA_ROWS)
            qs = pl.ds(pl.multiple_of(r * GRID_W, GRID_W), GRID_W)
            ks = pl.ds(pl.multiple_of(row0 * GRID_W, GRID_W), win)
            q = q_ref[qs, :].astype(BF16)
            pre.append((qs, ks, _dot_nt(q, kb_ref[ks, :]) * scale + bias_ref[0, r - row0], _dot_nt(q, kc) * scale))
        mid = []
        for qs, ks, s_win, s_ctx in pre:
            m = jnp.maximum(s_win.max(axis=-1, keepdims=True), s_ctx.max(axis=-1, keepdims=True))
            p_win, p_ctx = jnp.exp(s_win - m), jnp.exp(s_ctx - m)
            den = jnp.sum(p_win, axis=-1, keepdims=True) + jnp.sum(p_ctx, axis=-1, keepdims=True)
            mid.append((qs, ks, p_win.astype(BF16), p_ctx.astype(BF16), den))
        for qs, ks, p_win, p_ctx, den in mid:
            yl_ref[qs, :] = ((_dot(p_win, vb_ref[ks, :]) + _dot(p_ctx, vc)) / den).astype(yl_ref.dtype)
        return carry

    lax.fori_loop(0, rows // NA_UNROLL, body, 0)
    if emit_ctx:
        s = _dot_nt(qc_ref[...].astype(BF16), kc) * scale
        yc_ref[...] = _softmax_pv([(s, vc)]).astype(yc_ref.dtype)


def _na_bias_kernel(rpb_ref, o_ref):
    n = lax.broadcasted_iota(jnp.int32, (LANE, GRID_W * GRID_W), 1)
    j = lax.broadcasted_iota(jnp.int32, (LANE, GRID_W * GRID_W), 0)
    q, w = n >> 6, n & (GRID_W - 1)
    dc = jnp.clip(w - q, 1 - NA_COLS, NA_COLS - 1) + NA_COLS - 1
    onehot = jnp.where(dc == j, 1.0, 0.0).astype(F32)
    m = _dot(rpb_ref[...], onehot, precision=HIGHEST)
    c0 = jnp.clip(q[0:1] - NA_COLS // 2, 0, GRID_W - NA_COLS)
    ok = (w[0:1] >= c0) & (w[0:1] < c0 + NA_COLS)
    o_ref[...] = jnp.where(ok, m, NEG)


def _na_bias_tables(rpb):
    depth, H, nr, nc = rpb.shape
    assert GRID_W == 64 and depth * H * nr <= LANE and nc <= LANE
    flat = jnp.zeros((LANE, LANE), F32).at[:depth * H * nr, :nc].set(rpb.reshape(-1, nc).astype(F32))
    m = pl.pallas_call(
        _na_bias_kernel,
        out_shape=jax.ShapeDtypeStruct((LANE, GRID_W * GRID_W), F32),
        compiler_params=pltpu.CompilerParams(vmem_limit_bytes=VMEM_LIMIT),
        name="na_bias",
    )(flat)
    m = m[:depth * H * nr].reshape(depth, H, nr, GRID_W, GRID_W)
    tab = jnp.stack([jnp.stack([m[:, :, k - s + NA_ROWS - 1] for k in range(NA_ROWS)], axis=3)
                     for s in range(NA_ROWS)], axis=2)
    return tab.reshape(depth, H, NA_ROWS, GRID_W, NA_ROWS * GRID_W)


def _na(z, bias, nb, seq, ctx_len, emit_ctx):
    rows = seq // GRID_W
    cblk0 = nb * seq // ctx_len
    H = N_HEADS
    lat = lambda cb: pl.BlockSpec((seq, LANE), lambda b, h: (b, cb + h))
    ctx = lambda cb: pl.BlockSpec((ctx_len, LANE), lambda b, h: (cblk0 + b, cb + h))
    win = NA_ROWS * GRID_W
    in_specs = [lat(CB_NAQ), lat(CB_NAK), lat(CB_NAV), ctx(CB_NAQ), ctx(CB_NAK), ctx(CB_NAV),
                pl.BlockSpec((1, NA_ROWS, GRID_W, win), lambda b, h: (h, 0, 0, 0))]
    out_specs = [pl.BlockSpec((seq, LANE), lambda b, h: (b, h))]
    out_shape = [jax.ShapeDtypeStruct((nb * seq, GROUP_WIDTH), BF16)]
    if emit_ctx:
        out_specs.append(pl.BlockSpec((ctx_len, LANE), lambda b, h: (b, h)))
        out_shape.append(jax.ShapeDtypeStruct((nb * ctx_len, GROUP_WIDTH), BF16))
    return pl.pallas_call(
        functools.partial(_na_kernel, emit_ctx, rows),
        grid=(nb, H),
        in_specs=in_specs, out_specs=out_specs, out_shape=out_shape,
        scratch_shapes=[pltpu.VMEM((seq, LANE), BF16), pltpu.VMEM((seq, LANE), BF16)],
        compiler_params=_cp("parallel", "parallel"),
        name="na_attn",
    )(z, z, z, z, z, z, bias)


def _rms(x, w):
    return x * lax.rsqrt(jnp.mean(x * x, axis=-1, keepdims=True) + RMS_EPS) * w


def _mla_prep_kernel(cq_ref, ckv_ref, kra_ref, krb_ref, cc_ref, ss_ref, qnw_ref, kvnw_ref, wuq_ref, wukv_ref,
                     q_ref, kn_ref, kr_ref, v_ref):
    cc, ss = cc_ref[...], ss_ref[...]
    qn = _rms(cq_ref[...], qnw_ref[...]).astype(BF16)
    qa = _dot(qn, wuq_ref[...])
    for h in range(N_HEADS):
        b = 3 * LANE * h
        q_ref[:, 2 * LANE * h:2 * LANE * h + LANE] = qa[:, b:b + LANE].astype(BF16)
        q_ref[:, 2 * LANE * h + LANE:2 * LANE * (h + 1)] = (
            qa[:, b + LANE:b + 2 * LANE] * cc + qa[:, b + 2 * LANE:b + 3 * LANE] * ss).astype(BF16)
    kvn = _rms(ckv_ref[...], kvnw_ref[...]).astype(BF16)
    kv = _dot(kvn, wukv_ref[...])
    kn_ref[...] = kv[:, :GROUP_WIDTH].astype(BF16)
    v_ref[...] = kv[:, GROUP_WIDTH:].astype(BF16)
    kr_ref[...] = (kra_ref[...] * cc + krb_ref[...] * ss).astype(BF16)


def _mla_prep(z, cc, ss, qnw, kvnw, wuq, wukv, n_lat, seq, tm=512):
    m = z.shape[0]
    nlt, spt = n_lat // tm, seq // tm
    tab = lambda i: jnp.where(i < nlt, i % spt, spt)
    H = N_HEADS
    return pl.pallas_call(
        _mla_prep_kernel,
        grid=(m // tm,),
        in_specs=[pl.BlockSpec((tm, MLA_Q_RANK), lambda i: (i, CB_MQ * LANE // MLA_Q_RANK)),
                  pl.BlockSpec((tm, MLA_KV_RANK), lambda i: (i, CB_MKV * LANE // MLA_KV_RANK)),
                  pl.BlockSpec((tm, LANE), lambda i: (i, CB_MKRA)),
                  pl.BlockSpec((tm, LANE), lambda i: (i, CB_MKRB)),
                  pl.BlockSpec((tm, LANE), lambda i: (tab(i), 0)),
                  pl.BlockSpec((tm, LANE), lambda i: (tab(i), 0)),
                  pl.BlockSpec((1, MLA_Q_RANK), lambda i: (0, 0)),
                  pl.BlockSpec((1, MLA_KV_RANK), lambda i: (0, 0)),
                  pl.BlockSpec(wuq.shape, lambda i: (0, 0)),
                  pl.BlockSpec(wukv.shape, lambda i: (0, 0))],
        out_specs=[pl.BlockSpec((tm, 2 * LANE * H), lambda i: (i, 0)),
                   pl.BlockSpec((tm, GROUP_WIDTH), lambda i: (i, 0)),
                   pl.BlockSpec((tm, LANE), lambda i: (i, 0)),
                   pl.BlockSpec((tm, GROUP_WIDTH), lambda i: (i, 0))],
        out_shape=[jax.ShapeDtypeStruct((m, 2 * LANE * H), BF16),
                   jax.ShapeDtypeStruct((m, GROUP_WIDTH), BF16),
                   jax.ShapeDtypeStruct((m, LANE), BF16),
                   jax.ShapeDtypeStruct((m, GROUP_WIDTH), BF16)],
        compiler_params=_cp("parallel"),
        name="mla_prep",
    )(z, z, z, z, cc, ss, qnw.reshape(1, -1), kvnw.reshape(1, -1), wuq, wukv)


def _mla_attn_kernel(with_lat, seq, *refs):
    if with_lat:
        q_ref, knl_ref, krl_ref, vl_ref, knc_ref, krc_ref, vc_ref, y_ref, k_scr = refs
    else:
        q_ref, knc_ref, krc_ref, vc_ref, y_ref, k_scr = refs
    scale = MLA_QK_DIM ** -0.5
    nk = k_scr.shape[0]

    @pl.when(pl.program_id(2) == 0)
    def _():
        if with_lat:
            k_scr[0:seq, 0:LANE] = knl_ref[...]
            k_scr[0:seq, LANE:2 * LANE] = krl_ref[...]
        k_scr[nk - knc_ref.shape[0]:nk, 0:LANE] = knc_ref[...]
        k_scr[nk - knc_ref.shape[0]:nk, LANE:2 * LANE] = krc_ref[...]

    tq = q_ref.shape[0]
    qsub = min(tq, MLA_QSUB)
    nsub = tq // qsub
    vals = ([vl_ref] if with_lat else []) + [vc_ref]

    def qk(s):
        q = q_ref[s * qsub:(s + 1) * qsub, :]
        out = [_dot_nt(q, k_scr[0:seq, :])] if with_lat else []
        return out + [_dot_nt(q, k_scr[nk - knc_ref.shape[0]:nk, :])]

    def softmax(raw):
        m = raw[0].max(axis=-1, keepdims=True)
        for s in raw[1:]:
            m = jnp.maximum(m, s.max(axis=-1, keepdims=True))
        ps = [jnp.exp2((s - m) * (scale * LOG2E)) for s in raw]
        den = sum(jnp.sum(p, axis=-1, keepdims=True) for p in ps)
        return [p.astype(BF16) for p in ps], den

    raw = qk(0)
    for s in range(nsub):
        nxt = qk(s + 1) if s + 1 < nsub else None
        ps, den = softmax(raw)
        acc = sum(_dot(p, v[...]) for p, v in zip(ps, vals))
        y_ref[s * qsub:(s + 1) * qsub, :] = (acc / den).astype(y_ref.dtype)
        raw = nxt


def _mla_attn(q, kn, kr, v, nb, seq, ctx_len, with_lat, tq=2048):
    H = N_HEADS
    cblk0 = nb * seq // ctx_len
    nq = seq if with_lat else ctx_len
    tq = min(tq, nq)
    qblk0 = 0 if with_lat else nb * seq // tq
    ctxs = [pl.BlockSpec((ctx_len, LANE), lambda b, h, i: (cblk0 + b, h)),
            pl.BlockSpec((ctx_len, LANE), lambda b, h, i: (cblk0 + b, 0)),
            pl.BlockSpec((ctx_len, LANE), lambda b, h, i: (cblk0 + b, h))]
    lats = [pl.BlockSpec((seq, LANE), lambda b, h, i: (b, h)),
            pl.BlockSpec((seq, LANE), lambda b, h, i: (b, 0)),
            pl.BlockSpec((seq, LANE), lambda b, h, i: (b, h))]
    in_specs = [pl.BlockSpec((tq, 2 * LANE), lambda b, h, i: (qblk0 + b * (nq // tq) + i, h))]
    args = [q]
    if with_lat:
        in_specs += lats
        args += [kn, kr, v]
    in_specs += ctxs
    args += [kn, kr, v]
    nk = (seq if with_lat else 0) + ctx_len
    return pl.pallas_call(
        functools.partial(_mla_attn_kernel, with_lat, seq),
        grid=(nb, H, nq // tq),
        in_specs=in_specs,
        out_specs=pl.BlockSpec((tq, LANE), lambda b, h, i: (b * (nq // tq) + i, h)),
        out_shape=jax.ShapeDtypeStruct((nb * nq, GROUP_WIDTH), BF16),
        scratch_shapes=[pltpu.VMEM((nk, 2 * LANE), BF16)],
        compiler_params=_cp("parallel", "parallel", "arbitrary"),
        name="mla_attn_lat" if with_lat else "mla_attn_ctx",
    )(*args)


def _prep_w_in(w_in):
    gw = GROUP_WIDTH
    o_na = 4 * gw + 4 * N_HEADS
    o_mla = o_na + 3 * gw
    o_kv = o_mla + MLA_Q_RANK
    o_kr = o_kv + MLA_KV_RANK
    o_hg = o_kr + MLA_ROPE
    depth, d, n_in = w_in.shape
    kr = w_in[..., o_kr:o_hg].astype(BF16)
    k1, k2 = kr[..., 0::2], kr[..., 1::2]
    zpad = jnp.zeros((depth, d, LANE - MLA_ROPE), BF16)
    kra, krb = jnp.concatenate([k1, k2, zpad], axis=-1), jnp.concatenate([k2, k1, zpad], axis=-1)
    moves = ((CB_MQ, o_mla, MLA_Q_RANK), (CB_MKV, o_kv, MLA_KV_RANK), (CB_GAB, 4 * gw, 4 * N_HEADS),
             (CB_GQKV, 0, 4 * gw), (CB_NAQ, o_na, 3 * gw), (CB_HQ, o_hg, 5 * gw))

    def body(w_ref, kra_ref, krb_ref, o_ref):
        for cb, src, width in moves:
            lo = src // LANE * LANE
            hi = min(-(-(src + width) // LANE) * LANE, n_in)
            piece = w_ref[:, lo:hi][:, src - lo:src - lo + width].astype(BF16)
            pad = -width % LANE
            if pad:
                piece = jnp.concatenate([piece, jnp.zeros((piece.shape[0], pad), BF16)], axis=-1)
            o_ref[:, cb * LANE:cb * LANE + width + pad] = piece
        o_ref[:, CB_MKRA * LANE:(CB_MKRA + 1) * LANE] = kra_ref[...]
        o_ref[:, CB_MKRB * LANE:(CB_MKRB + 1) * LANE] = krb_ref[...]

    tr = 256
    return pl.pallas_call(
        body,
        grid=(depth, d // tr),
        in_specs=[pl.BlockSpec((None, tr, n_in), lambda l, i: (l, i, 0)),
                  pl.BlockSpec((None, tr, LANE), lambda l, i: (l, i, 0)),
                  pl.BlockSpec((None, tr, LANE), lambda l, i: (l, i, 0))],
        out_specs=pl.BlockSpec((None, tr, NP_IN), lambda l, i: (l, i, 0)),
        out_shape=jax.ShapeDtypeStruct((depth, d, NP_IN), BF16),
        compiler_params=_cp("parallel", "parallel"),
        name="w_in_layout",
    )(w_in, kra, krb)


def _prep_w_uq(w_uq):
    r = w_uq.shape[0]
    z = jnp.zeros((r, LANE - MLA_ROPE), w_uq.dtype)
    cols = []
    for h in range(N_HEADS):
        wh = w_uq[:, h * MLA_QK_DIM:(h + 1) * MLA_QK_DIM]
        rope = wh[:, MLA_NOPE:]
        r1, r2 = rope[:, 0::2], rope[:, 1::2]
        cols += [wh[:, :MLA_NOPE], r1, r2, z, r2, r1, z]
    return jnp.concatenate(cols, axis=1).astype(BF16)


def _rope_tables(seq, tm):
    n_freq = MLA_ROPE // 4
    freqs = ROPE_BASE ** (-jnp.arange(n_freq, dtype=F32) / n_freq)
    t = jnp.arange(seq)
    ang = jnp.concatenate([(t // GRID_W).astype(F32)[:, None] * freqs,
                           (t % GRID_W).astype(F32)[:, None] * freqs], -1)
    cos, sin = jnp.cos(ang), jnp.sin(ang)
    zp = jnp.zeros((seq, LANE - MLA_ROPE), F32)
    cc = jnp.concatenate([cos, cos, zp], axis=1)
    ss = jnp.concatenate([-sin, sin, zp], axis=1)
    ident = jnp.zeros((tm, LANE), F32).at[:, :MLA_ROPE].set(1.0)
    return jnp.concatenate([cc, ident], axis=0), jnp.concatenate([ss, jnp.zeros((tm, LANE), F32)], axis=0)


def kernel(x, c, ctx, c_ctx, w_ada, b_ada, w_in, gdn_conv_w, gdn_a_log, gdn_dt_bias, gdn_norm_w, na_rpb,
           mla_q_norm_w, mla_kv_norm_w, mla_w_uq, mla_w_uk, mla_w_uv, hgrn_lower_bounds, hgrn_norm_w, w_out,
           ln1_w, ln1_b, w_mlp1, w_mlp2, ln2_w, ln2_b):
    nb, seq, d = x.shape
    ctx_len = ctx.shape[1]
    depth = w_ada.shape[0]
    n_lat, n_ctx = nb * seq, nb * ctx_len
    alpha = (2 * depth) ** 0.25
    tm = 512
    tmm = 1024 if (seq % 1024 == 0 and n_ctx % 1024 == 0) else tm
    assert nb < 8 and seq % tm == 0 and n_ctx % tm == 0 and seq % ctx_len == 0 and ctx_len % CHUNK == 0

    cin = jnp.zeros((8, d), F32).at[:nb].set(c).at[nb].set(c_ctx)
    ada = _ada(cin, w_ada, b_ada)
    p_lb = jax.nn.softmax(hgrn_lower_bounds.astype(F32), axis=0)
    lbs = jnp.cumsum(p_lb, axis=0) - p_lb[0]
    cc, ss = _rope_tables(seq, tm)
    na_bias = _na_bias_tables(na_rpb)
    w_in_b, w_out_b = _prep_w_in(w_in), w_out.astype(BF16)
    w_mlp1_b, w_mlp2_b = w_mlp1.astype(BF16), w_mlp2.astype(BF16)

    xs = (x.reshape(n_lat, d), ctx.reshape(n_ctx, d))
    for l in range(depth):
        emit_ctx = l < depth - 1
        ada_r = ada[l].reshape(8 * 6, 1, d)
        z = _inproj(xs, ada_r, w_in_b, l, n_lat + n_ctx, n_lat, seq, nb, tm=tmm)

        qkv = _gdn_conv(z, gdn_conv_w[l], nb, seq, ctx_len)
        p = _gdn_gates(z, gdn_a_log[l], gdn_dt_bias[l], tm=tm)
        gdn_o = _gdn_scan(*_gdn_prep(qkv, p, nb, seq, ctx_len), nb, seq, ctx_len)
        m_out = n_lat + n_ctx if emit_ctx else n_lat
        yb = _na(z, na_bias[l], nb, seq, ctx_len, emit_ctx)
        q, kn, kr, v = _mla_prep(z, cc, ss, mla_q_norm_w[l], mla_kv_norm_w[l], _prep_w_uq(mla_w_uq[l]),
                                 jnp.concatenate([mla_w_uk[l], mla_w_uv[l]], axis=1).astype(BF16),
                                 n_lat, seq, tm=tm)
        ym = [_mla_attn(q, kn, kr, v, nb, seq, ctx_len, True)]
        if emit_ctx:
            ym.append(_mla_attn(q, kn, kr, v, nb, seq, ctx_len, False))
        hgrn_o = _gla_scan(z, lbs[l], nb, seq, ctx_len)

        if len(xs) == 1 and emit_ctx:
            yb, ym = [jnp.concatenate(yb, axis=0)], [jnp.concatenate(ym, axis=0)]
        x_all = _outproj(xs if emit_ctx else xs[:1], gdn_o, yb, ym, hgrn_o, z, gdn_norm_w[l], hgrn_norm_w[l],
                         w_out_b, l, ada_r, ln1_w[l], ln1_b[l], m_out, n_lat, seq, nb, alpha)
        x_all = _mlp(x_all, w_mlp1_b, w_mlp2_b, l, ada_r, ln2_w[l], ln2_b[l], seq, nb, alpha, tm=tm)
        xs = (x_all,)
    return x_all[:n_lat].reshape(nb, seq, d)
```

```python
import functools

import numpy as np
import jax
import jax.numpy as jnp
from jax import lax
from jax.experimental import pallas as pl
from jax.experimental.pallas import tpu as pltpu

F32 = jnp.float32
BF16 = jnp.bfloat16
HIGHEST = lax.Precision.HIGHEST

GRID_W = 64
N_HEADS = 4
HEAD_DIM = 128
GROUP_WIDTH = 512
CHUNK = 64
SUB = 8
GDN_CONV = 5
NA_ROWS = 8
NA_COLS = 16
NA_UNROLL = 4
MLA_Q_RANK = 384
MLA_KV_RANK = 256
MLA_NOPE = 128
MLA_ROPE = 64
MLA_QK_DIM = MLA_NOPE + MLA_ROPE
MLA_QSUB = 512
ROPE_BASE = 10000.0
LN_EPS = 1e-5
RMS_EPS = 1e-6
LOG2E = 1.4426950408889634
NEG = -1e30

LANE = 128
CB_MQ, CB_MKRA, CB_MKV, CB_MKRB, CB_GAB = 0, 3, 4, 6, 7
CB_GQKV, CB_GGATE = 8, 20
CB_NAQ, CB_NAK, CB_NAV = 24, 28, 32
CB_HQ, CB_HFF, CB_HFB, CB_HI, CB_HG = 36, 40, 44, 48, 52
CHUNK_SHIFT = 6
PREP_CHUNKS = 4
NP_IN = 56 * LANE

VMEM_LIMIT = 48 << 20


def _cp(*sem):
    return pltpu.CompilerParams(dimension_semantics=sem, vmem_limit_bytes=VMEM_LIMIT)


def _silu(x):
    return x * jax.nn.sigmoid(x)


def _dot(a, b, **kw):
    return jnp.dot(a, b, preferred_element_type=F32, **kw)


def _dot_nt(a, b, **kw):
    return lax.dot_general(a, b, (((1,), (1,)), ((), ())), preferred_element_type=F32, **kw)


def _dot_tn(a, b, **kw):
    return lax.dot_general(a, b, (((0,), (0,)), ((), ())), preferred_element_type=F32, **kw)


def _ada_kernel(c_ref, w_ref, b_ref, o_ref):
    s = _silu(c_ref[...])
    o_ref[0] = _dot(s, w_ref[0], precision=HIGHEST) + b_ref[0]


def _ada(cin, w_ada, b_ada):
    depth, d, n = w_ada.shape
    tn = 1024
    return pl.pallas_call(
        _ada_kernel,
        grid=(depth, n // tn),
        in_specs=[pl.BlockSpec((8, d), lambda l, j: (0, 0)),
                  pl.BlockSpec((1, d, tn), lambda l, j: (l, 0, j)),
                  pl.BlockSpec((1, 1, tn), lambda l, j: (l, 0, j))],
        out_specs=pl.BlockSpec((1, 8, tn), lambda l, j: (l, 0, j)),
        out_shape=jax.ShapeDtypeStruct((depth, 8, n), F32),
        compiler_params=_cp("parallel", "parallel"),
        name="ada",
    )(cin, w_ada, b_ada.reshape(depth, 1, n))


ROW_STEP = 256


def _modulate(x_ref, sh_ref, sc_ref, xm_ref):
    sc1, sh = 1.0 + sc_ref[0], sh_ref[0]

    def body(t, carry):
        sl = pl.ds(pl.multiple_of(t * ROW_STEP, ROW_STEP), ROW_STEP)
        xm_ref[sl, :] = (x_ref[sl, :] * sc1 + sh).astype(BF16)
        return carry

    lax.fori_loop(0, x_ref.shape[0] // ROW_STEP, body, 0)


def _inproj_kernel(n_lat_tiles, *refs):
    xs, (sh_ref, sc_ref, w_ref, o_ref, xm_ref) = refs[:-5], refs[-5:]
    first = pl.program_id(1) == 0
    if len(xs) == 1:
        pl.when(first)(lambda: _modulate(xs[0], sh_ref, sc_ref, xm_ref))
    else:
        is_lat = pl.program_id(0) < n_lat_tiles
        pl.when(first & is_lat)(lambda: _modulate(xs[0], sh_ref, sc_ref, xm_ref))
        pl.when(first & jnp.logical_not(is_lat))(lambda: _modulate(xs[1], sh_ref, sc_ref, xm_ref))
    o_ref[...] = _dot(xm_ref[...], w_ref[...])


def _inproj(xs, ada_r, w, layer, m, n_lat, seq, nb, tm=512, tn=1024):
    _, d, n = w.shape
    nlt = n_lat // tm
    row = lambda i: jnp.minimum((i * tm) // seq, nb)
    if len(xs) == 1:
        x_specs = [pl.BlockSpec((tm, d), lambda i, j: (i, 0))]
    else:
        x_specs = [pl.BlockSpec((tm, d), lambda i, j: (jnp.minimum(i, nlt - 1), 0)),
                   pl.BlockSpec((tm, d), lambda i, j: (jnp.maximum(i - nlt, 0), 0),
                                pipeline_mode=pl.Buffered(1))]
    return pl.pallas_call(
        functools.partial(_inproj_kernel, nlt),
        grid=(m // tm, n // tn),
        in_specs=x_specs + [
                  pl.BlockSpec((1, 1, d), lambda i, j: (row(i) * 6 + 0, 0, 0)),
                  pl.BlockSpec((1, 1, d), lambda i, j: (row(i) * 6 + 1, 0, 0)),
                  pl.BlockSpec((None, d, tn), lambda i, j: (layer, 0, j))],
        out_specs=pl.BlockSpec((tm, tn), lambda i, j: (i, j)),
        out_shape=jax.ShapeDtypeStruct((m, n), F32),
        scratch_shapes=[pltpu.VMEM((tm, d), BF16)],
        compiler_params=_cp("parallel", "arbitrary"),
        name="inproj",
    )(*xs, ada_r, ada_r, w)


def _layernorm(r, w, b):
    mu = jnp.mean(r, axis=-1, keepdims=True)
    rc = r - mu
    var = jnp.mean(rc * rc, axis=-1, keepdims=True)
    return rc * lax.rsqrt(var + LN_EPS) * w + b


def _head_norm_gate(o, nw, gate):
    o = o * lax.rsqrt(jnp.mean(o * o, axis=-1, keepdims=True) + RMS_EPS) * nw
    return o * _silu(gate)


def _scan_mixer_out(of_ref, ob_ref, gate_ref, nw_ref):
    nw = nw_ref[...]
    heads = []
    for h in range(N_HEADS):
        hs = slice(h * LANE, (h + 1) * LANE)
        heads.append(_head_norm_gate(of_ref[:, hs] + ob_ref[:, hs], nw, gate_ref[:, hs]).astype(BF16))
    return jnp.concatenate(heads, axis=-1)


def _outproj_kernel(alpha, n_lat_tiles, n_split, *refs):
    pairs, rest = refs[:n_split], refs[n_split:]
    (gaf_ref, gab_ref, gag_ref, hgf_ref, hgb_ref, hgg_ref, nwa_ref, nwh_ref,
     w_ref, g_ref, lw_ref, lb_ref, o_ref) = rest
    gw = GROUP_WIDTH

    def run(x_ref, yb_ref, ym_ref):
        acc = _dot(_scan_mixer_out(gaf_ref, gab_ref, gag_ref, nwa_ref), w_ref[0:gw, :])
        acc += _dot(yb_ref[...], w_ref[gw:2 * gw, :])
        acc += _dot(ym_ref[...], w_ref[2 * gw:3 * gw, :])
        acc += _dot(_scan_mixer_out(hgf_ref, hgb_ref, hgg_ref, nwh_ref), w_ref[3 * gw:4 * gw, :])
        r = alpha * x_ref[...] + g_ref[0] * acc
        o_ref[...] = _layernorm(r, lw_ref[...], lb_ref[...])

    if n_split == 3:
        run(*pairs)
    else:
        is_lat = pl.program_id(0) < n_lat_tiles
        pl.when(is_lat)(lambda: run(*pairs[0::2]))
        pl.when(jnp.logical_not(is_lat))(lambda: run(*pairs[1::2]))


def _outproj(xs, gdn_o, ybs, yms, hgrn_o, z, nwa, nwh, w, layer, ada_r, lw, lb, m_out, n_lat, seq, nb, alpha,
             tm=256):
    d = w.shape[1]
    gw = GROUP_WIDTH
    nlt = n_lat // tm
    row = lambda i: jnp.minimum((i * tm) // seq, nb)
    in_specs, args = [], []
    dual = len(xs) == 2
    for arrs, width in ((xs, d), (ybs, gw), (yms, gw)):
        if dual:
            in_specs += [pl.BlockSpec((tm, width), lambda i: (jnp.minimum(i, nlt - 1), 0)),
                         pl.BlockSpec((tm, width), lambda i: (jnp.maximum(i - nlt, 0), 0))]
        else:
            in_specs += [pl.BlockSpec((tm, width), lambda i: (i, 0))]
        args += list(arrs)
    n_split = len(args)
    rowspec = pl.BlockSpec((tm, gw), lambda i: (i, 0))
    gate = lambda cb: pl.BlockSpec((tm, gw), lambda i: (i, cb * LANE // gw))
    vec = lambda n: pl.BlockSpec((1, n), lambda i: (0, 0))
    in_specs += [rowspec, rowspec, gate(CB_GGATE), rowspec, rowspec, gate(CB_HG), vec(LANE), vec(LANE),
                 pl.BlockSpec((None, d, d), lambda i: (layer, 0, 0), pipeline_mode=pl.Buffered(1)),
                 pl.BlockSpec((1, 1, d), lambda i: (row(i) * 6 + 2, 0, 0)), vec(d), vec(d)]
    args += [*gdn_o, z, *hgrn_o, z, nwa.reshape(1, LANE), nwh.reshape(1, LANE), w, ada_r,
             lw.reshape(1, d), lb.reshape(1, d)]
    return pl.pallas_call(
        functools.partial(_outproj_kernel, alpha, nlt, n_split),
        grid=(m_out // tm,),
        in_specs=in_specs,
        out_specs=pl.BlockSpec((tm, d), lambda i: (i, 0)),
        out_shape=jax.ShapeDtypeStruct((m_out, d), F32),
        compiler_params=_cp("parallel"),
        name="outproj_ln",
    )(*args)


def _mlp_kernel(alpha, x_ref, sh_ref, sc_ref, g_ref, w1_ref, w2_ref, lw_ref, lb_ref, o_ref, xm_ref, acc_ref):
    k = pl.program_id(1)

    @pl.when(k == 0)
    def _():
        _modulate(x_ref, sh_ref, sc_ref, xm_ref)
        acc_ref[...] = jnp.zeros_like(acc_ref)

    h = jnp.maximum(_dot(xm_ref[...], w1_ref[...]), 0.0)
    acc_ref[...] += _dot((h * h).astype(BF16), w2_ref[...])

    @pl.when(k == pl.num_programs(1) - 1)
    def _():
        g, lw, lb = g_ref[0], lw_ref[...], lb_ref[...]

        def body(t, carry):
            sl = pl.ds(pl.multiple_of(t * ROW_STEP, ROW_STEP), ROW_STEP)
            o_ref[sl, :] = _layernorm(alpha * x_ref[sl, :] + g * acc_ref[sl, :], lw, lb)
            return carry

        lax.fori_loop(0, x_ref.shape[0] // ROW_STEP, body, 0)


def _mlp(x_all, w1, w2, layer, ada_r, lw, lb, seq, nb, alpha, tm=512, th=1024):
    m, d = x_all.shape
    hid = w1.shape[2]
    row = lambda i: jnp.minimum((i * tm) // seq, nb)
    return pl.pallas_call(
        functools.partial(_mlp_kernel, alpha),
        grid=(m // tm, hid // th),
        in_specs=[pl.BlockSpec((tm, d), lambda i, k: (i, 0)),
                  pl.BlockSpec((1, 1, d), lambda i, k: (row(i) * 6 + 3, 0, 0)),
                  pl.BlockSpec((1, 1, d), lambda i, k: (row(i) * 6 + 4, 0, 0)),
                  pl.BlockSpec((1, 1, d), lambda i, k: (row(i) * 6 + 5, 0, 0)),
                  pl.BlockSpec((None, d, th), lambda i, k: (layer, 0, k)),
                  pl.BlockSpec((None, th, d), lambda i, k: (layer, k, 0)),
                  pl.BlockSpec((1, d), lambda i, k: (0, 0)),
                  pl.BlockSpec((1, d), lambda i, k: (0, 0))],
        out_specs=pl.BlockSpec((tm, d), lambda i, k: (i, 0)),
        out_shape=jax.ShapeDtypeStruct((m, d), F32),
        scratch_shapes=[pltpu.VMEM((tm, d), BF16), pltpu.VMEM((tm, d), F32)],
        compiler_params=_cp("parallel", "arbitrary"),
        name="mlp_ln",
    )(x_all, ada_r, ada_r, ada_r, w1, w2, lw.reshape(1, d), lb.reshape(1, d))


def _gdn_conv_kernel(seq, ctx_len, xl_ref, xc_ref, w_ref, o_ref, pad_ref):
    j = pl.program_id(1)
    w = w_ref[...]
    qscale = jnp.where(j < N_HEADS, HEAD_DIM ** -0.5, 1.0).astype(F32)
    p0 = 8 - GDN_CONV // 2
    for x_ref, nrows, o0 in ((xl_ref, seq, 0), (xc_ref, ctx_len, seq)):
        pad_ref[0:8, :] = jnp.zeros((8, LANE), F32)
        pad_ref[nrows + 8:nrows + 16, :] = jnp.zeros((8, LANE), F32)
        pad_ref[8:nrows + 8, :] = x_ref[...]
        rb = min(nrows, 256)
        for r0 in range(0, nrows, rb):
            y = pad_ref[r0 + p0:r0 + p0 + rb, :] * w[0:1, :]
            for i in range(1, GDN_CONV):
                y = y + pad_ref[r0 + p0 + i:r0 + p0 + i + rb, :] * w[i:i + 1, :]
            y = _silu(y)
            nrm = y * lax.rsqrt(jnp.sum(y * y, axis=-1, keepdims=True) + RMS_EPS) * qscale
            o_ref[o0 + r0:o0 + r0 + rb, :] = jnp.where(j < 2 * N_HEADS, nrm, y)


def _gdn_conv(z, conv_w, nb, seq, ctx_len):
    nblk = 3 * N_HEADS
    cblk0 = nb * seq // ctx_len
    return pl.pallas_call(
        functools.partial(_gdn_conv_kernel, seq, ctx_len),
        grid=(nb, nblk),
        in_specs=[pl.BlockSpec((seq, LANE), lambda b, j: (b, CB_GQKV + j)),
                  pl.BlockSpec((ctx_len, LANE), lambda b, j: (cblk0 + b, CB_GQKV + j)),
                  pl.BlockSpec((GDN_CONV, LANE), lambda b, j: (0, j))],
        out_specs=pl.BlockSpec((seq + ctx_len, LANE), lambda b, j: (b, j)),
        out_shape=jax.ShapeDtypeStruct((nb * (seq + ctx_len), nblk * LANE), F32),
        scratch_shapes=[pltpu.VMEM((seq + 16, LANE), F32)],
        compiler_params=_cp("parallel", "parallel"),
        name="gdn_conv",
    )(z, z, conv_w)


def _gdn_gates_kernel(tm, s_ref, alog_ref, dtb_ref, o_ref):
    s = s_ref[...]
    g = -jnp.exp(alog_ref[...]) * (jnp.maximum(s + dtb_ref[...], 0.0)
                                    + jnp.log1p(jnp.exp(-jnp.abs(s + dtb_ref[...]))))
    r = lax.broadcasted_iota(jnp.int32, (tm, tm), 0)
    c = lax.broadcasted_iota(jnp.int32, (tm, tm), 1)
    same = (r >> CHUNK_SHIFT) == (c >> CHUNK_SHIFT)
    lo = jnp.where(same & (c <= r), 1.0, 0.0).astype(F32)
    up = jnp.where(same & (c >= r), 1.0, 0.0).astype(F32)
    cum_f = _dot(lo, g, precision=HIGHEST)
    cum_b = _dot(up, g, precision=HIGHEST)
    col = lax.broadcasted_iota(jnp.int32, s.shape, 1)
    o_ref[...] = jnp.where(col < N_HEADS, cum_f,
                           jnp.where(col < 2 * N_HEADS, cum_b,
                                     jnp.where(col < 4 * N_HEADS, jax.nn.sigmoid(s), 0.0)))


def _gdn_gates(z, a_log, dt_bias, tm=512):
    m = z.shape[0]
    pad = lambda v: jnp.zeros((1, LANE), F32).at[0, :2 * N_HEADS].set(v.reshape(-1).astype(F32))
    return pl.pallas_call(
        functools.partial(_gdn_gates_kernel, tm),
        grid=(m // tm,),
        in_specs=[pl.BlockSpec((tm, LANE), lambda i: (i, CB_GAB)),
                  pl.BlockSpec((1, LANE), lambda i: (0, 0)),
                  pl.BlockSpec((1, LANE), lambda i: (0, 0))],
        out_specs=pl.BlockSpec((tm, LANE), lambda i: (i, 0)),
        out_shape=jax.ShapeDtypeStruct((m, LANE), F32),
        compiler_params=_cp("parallel"),
        name="gdn_gates",
    )(z, pad(a_log), pad(dt_bias))


def _tri_masks(n):
    r = lax.broadcasted_iota(jnp.int32, (n, n), 0)
    c = lax.broadcasted_iota(jnp.int32, (n, n), 1)
    return r, c


def _split(x):
    hi = x.astype(BF16)
    return hi, (x - hi.astype(F32)).astype(BF16)


def _dot3(a, b):
    return _dot(a[0], b[0]) + (_dot(a[0], b[1]) + _dot(a[1], b[0]))


def _gdn_prep_kernel(qkv_ref, p_ref, u_ref, w_ref, qt_ref, kt_ref, att_ref, el_ref):
    C, H = CHUNK, N_HEADS
    lane = lax.broadcasted_iota(jnp.int32, (C, LANE), 1)
    r, c = _tri_masks(C)
    eye = jnp.where(r == c, 1.0, 0.0).astype(F32)
    a_list, rhs_list, where_list = [], [], []
    for n in range(PREP_CHUNKS):
        rs = slice(n * C, (n + 1) * C)
        pblk = p_ref[rs, :]
        tblk = pblk.T
        col = lambda idx, pblk=pblk: jnp.sum(jnp.where(lane == idx, pblk, 0.0), axis=-1, keepdims=True)
        row = lambda idx, tblk=tblk: tblk[idx:idx + 1, :]
        for h in range(H):
            hs = slice(h * LANE, (h + 1) * LANE)
            q = qkv_ref[rs, h * LANE:(h + 1) * LANE]
            k = qkv_ref[rs, (H + h) * LANE:(H + h + 1) * LANE]
            v = qkv_ref[rs, (2 * H + h) * LANE:(2 * H + h + 1) * LANE]
            qbf, kbf = q.astype(BF16), k.astype(BF16)
            for d in range(2):
                idx = d * H + h
                cum_c, cum_r, beta_c = col(idx), row(idx), col(2 * H + idx)
                incl = (c >= r) if d else (c <= r)
                strict = (c > r) if d else (c < r)
                last = cum_r[:, 0:1] if d else cum_r[:, C - 1:C]
                decay = jnp.exp(jnp.where(incl, cum_c - cum_r, NEG))
                kb = k * beta_c
                ec = jnp.exp(cum_c)
                a_list.append(jnp.where(strict, _dot_nt(kb.astype(BF16), kbf) * decay, 0.0))
                rhs_list.append(jnp.concatenate([v * beta_c, kb * ec], axis=-1).astype(BF16))
                where_list.append((d, rs, hs))
                att_ref[d, h, rs, :] = jnp.where(incl, _dot_nt(qbf, kbf) * decay, 0.0).astype(BF16)
                qt_ref[d, rs, hs] = (q * ec).astype(BF16)
                kt_ref[d, rs, hs] = (k * jnp.exp(last - cum_c)).astype(BF16)
                el_ref[n, idx:idx + 1, :] = jnp.broadcast_to(jnp.exp(last), (1, LANE))
    ts = [eye - a for a in a_list]
    ps = [a.astype(BF16) for a in a_list]
    for _ in range(5):
        ps = [_dot(p, p).astype(BF16) for p in ps]
        ts = [t + _dot(t.astype(BF16), p) for t, p in zip(ts, ps)]
    res = [eye - t - _dot3(_split(a), _split(t)) for a, t in zip(a_list, ts)]
    ts = [t + _dot(t.astype(BF16), e.astype(BF16)) for t, e in zip(ts, res)]
    sols = [_dot(t.astype(BF16), rhs) for t, rhs in zip(ts, rhs_list)]
    for sol, (d, rs, hs) in zip(sols, where_list):
        u_ref[d, rs, hs] = sol[:, :HEAD_DIM]
        w_ref[d, rs, hs] = sol[:, HEAD_DIM:].astype(BF16)


def _gdn_prep(qkv, p, nb, seq, ctx_len):
    m = qkv.shape[0]
    H = N_HEADS
    gw = GROUP_WIDTH
    rows = PREP_CHUNKS * CHUNK
    assert seq % rows == 0 and ctx_len % rows == 0
    dspec = pl.BlockSpec((2, rows, gw), lambda i: (0, i, 0))
    nl, nc = seq // rows, ctx_len // rows

    def qkv_blk(i):
        ic = i - nb * nl
        return jnp.where(i < nb * nl, (i // nl) * (nl + nc) + i % nl, (ic // nc) * (nl + nc) + nl + ic % nc)

    return pl.pallas_call(
        _gdn_prep_kernel,
        grid=(m // rows,),
        in_specs=[pl.BlockSpec((rows, 3 * gw), lambda i: (qkv_blk(i), 0)),
                  pl.BlockSpec((rows, LANE), lambda i: (i, 0))],
        out_specs=[dspec, dspec, dspec, dspec,
                   pl.BlockSpec((2, H, rows, CHUNK), lambda i: (0, 0, i, 0)),
                   pl.BlockSpec((PREP_CHUNKS, 2 * H, LANE), lambda i: (i, 0, 0))],
        out_shape=[jax.ShapeDtypeStruct((2, m, gw), F32),
                   jax.ShapeDtypeStruct((2, m, gw), BF16),
                   jax.ShapeDtypeStruct((2, m, gw), BF16),
                   jax.ShapeDtypeStruct((2, m, gw), BF16),
                   jax.ShapeDtypeStruct((2, H, m, CHUNK), BF16),
                   jax.ShapeDtypeStruct((m // CHUNK, 2 * H, LANE), F32)],
        compiler_params=_cp("parallel"),
        name="gdn_prep",
    )(qkv, p)


def _gdn_scan_a(ins, s_ref, g, j):
    H = N_HEADS
    mid = []
    for d in range(2):
        u_ref, w_ref, qt_ref = ins[d:6:2]
        cj = g - 1 - j if d else j
        rs = pl.ds(pl.multiple_of(cj * CHUNK, CHUNK), CHUNK)
        for h in range(H):
            hs = slice(h * LANE, (h + 1) * LANE)
            sb = s_ref[d * H + h].astype(BF16)
            vnb = (u_ref[0, rs, hs] - _dot(w_ref[0, rs, hs], sb)).astype(BF16)
            mid.append((vnb, _dot(qt_ref[0, rs, hs], sb)))
    return mid


def _gdn_scan_b(ins, s_ref, o_refs, g, j, mid):
    H = N_HEADS
    for d in range(2):
        kt_ref, att_ref, el_ref = ins[6 + d::2]
        cj = g - 1 - j if d else j
        rs = pl.ds(pl.multiple_of(cj * CHUNK, CHUNK), CHUNK)
        for h in range(H):
            hs = slice(h * LANE, (h + 1) * LANE)
            idx = d * H + h
            vnb, o_state = mid[idx]
            s_ref[idx] = s_ref[idx] * el_ref[cj, idx:idx + 1, :] + _dot_tn(kt_ref[0, rs, hs], vnb)
            o_refs[d][rs, hs] = o_state + _dot(att_ref[0, h, rs, :], vnb)


def _scans_kernel(g, *refs):
    gdn_ins = refs[:12]
    qf_ref, ff_ref, if_ref, qb_ref, fb_ref, ib_ref, lb_ref = refs[12:19]
    gof_ref, gob_ref, of_ref, ob_ref, gs_ref, s_ref = refs[19:]
    C, H, nsub = CHUNK, N_HEADS, CHUNK // SUB

    @pl.when(pl.program_id(1) == 0)
    def _():
        s_ref[...] = jnp.zeros_like(s_ref)
        gs_ref[...] = jnp.zeros_like(gs_ref)

    r, c = _tri_masks(C)
    tri = (jnp.where(c <= r, 1.0, 0.0).astype(F32), jnp.where(c >= r, 1.0, 0.0).astype(F32))
    trow = lax.broadcasted_iota(jnp.int32, (C, HEAD_DIM), 0)
    srow = lax.broadcasted_iota(jnp.int32, (SUB, C), 0)
    scol = lax.broadcasted_iota(jnp.int32, (SUB, C), 1)
    chains = [(d, h) for d in range(2) for h in range(H)]
    srcs = ((qf_ref, ff_ref, if_ref, of_ref), (qb_ref, fb_ref, ib_ref, ob_ref))

    def body(j, carry):
        ph1 = []
        for d, h in chains:
            cj = g - 1 - j if d else j
            rs, hs = pl.ds(pl.multiple_of(cj * C, C), C), slice(h * LANE, (h + 1) * LANE)
            lb = lb_ref[:, hs]
            f = lb + (1.0 - lb) * jax.nn.sigmoid(srcs[d][1][rs, hs])
            ph1.append((rs, hs, f, _dot(tri[d], jnp.log(f), precision=HIGHEST)))
        gdn_mid = _gdn_scan_a(gdn_ins, gs_ref, g, j)
        ph2 = []
        for (d, h), (rs, hs, f, cum) in zip(chains, ph1):
            q_ref, _, i_ref, _ = srcs[d]
            q = _silu(q_ref[rs, hs]) * HEAD_DIM ** -0.5
            k = 1.0 - f
            vb = i_ref[rs, hs].astype(BF16)
            last = cum[0:1, :] if d else cum[C - 1:C, :]
            idx = d * H + h
            St = s_ref[idx]
            o_state = _dot_nt((q * jnp.exp(cum)).astype(BF16), St.astype(BF16))
            s_ref[idx] = St * jnp.exp(last) + _dot_tn(vb, (k * jnp.exp(last - cum)).astype(BF16))
            inter = []
            for a in range(nsub):
                sa = slice(a * SUB, (a + 1) * SUB)
                if d and a < nsub - 1:
                    cb = cum[(a + 1) * SUB:(a + 1) * SUB + 1, :]
                    kt = k * jnp.exp(jnp.where(trow >= (a + 1) * SUB, cb - cum, NEG))
                elif (not d) and a > 0:
                    cb = cum[a * SUB - 1:a * SUB, :]
                    kt = k * jnp.exp(jnp.where(trow < a * SUB, cb - cum, NEG))
                else:
                    inter.append(jnp.zeros((SUB, C), F32))
                    continue
                inter.append(_dot_nt((q[sa] * jnp.exp(cum[sa] - cb)).astype(BF16), kt.astype(BF16)))
            ph2.append((q, k, cum, vb, o_state, inter))
        _gdn_scan_b(gdn_ins, gs_ref, (gof_ref, gob_ref), g, j, gdn_mid)
        for (d, h), (rs, hs, _, _), (q, k, cum, vb, o_state, inter) in zip(chains, ph1, ph2):
            blocks = []
            for a in range(nsub):
                sa = slice(a * SUB, (a + 1) * SUB)
                qa, ka, ca, sc = q[sa], k[sa], cum[sa], inter[a]
                for t in range(SUB):
                    dec = jnp.exp(ca - ca[t:t + 1, :])
                    st = jnp.sum(qa * ka[t:t + 1, :] * dec, axis=-1, keepdims=True)
                    ok = (srow <= t) if d else (srow >= t)
                    sc = jnp.where((scol == a * SUB + t) & ok, st, sc)
                blocks.append(sc)
            scores = jnp.concatenate(blocks, axis=0)
            srcs[d][3][rs, hs] = o_state + _dot(scores.astype(BF16), vb)
        return carry

    lax.fori_loop(0, g, body, 0)


def _scans(u, w, qt, kt, att, el, z, lbs, nb, seq, ctx_len):
    m = z.shape[0]
    H = N_HEADS
    gw = GROUP_WIDTH
    g = ctx_len // CHUNK
    nblk = seq // ctx_len
    cblk0 = nb * nblk
    blk_f = lambda b, t: jnp.where(t == 0, cblk0 + b, b * nblk + t - 1)
    blk_b = lambda b, t: jnp.where(t == 0, cblk0 + b, b * nblk + nblk - t)
    in_specs, args = [], []
    for arr in (u, w, qt, kt):
        in_specs += [pl.BlockSpec((1, ctx_len, gw), lambda b, t: (0, blk_f(b, t), 0)),
                     pl.BlockSpec((1, ctx_len, gw), lambda b, t: (1, blk_b(b, t), 0))]
        args += [arr, arr]
    in_specs += [pl.BlockSpec((1, H, ctx_len, CHUNK), lambda b, t: (0, 0, blk_f(b, t), 0)),
                 pl.BlockSpec((1, H, ctx_len, CHUNK), lambda b, t: (1, 0, blk_b(b, t), 0)),
                 pl.BlockSpec((g, 2 * H, LANE), lambda b, t: (blk_f(b, t), 0, 0)),
                 pl.BlockSpec((g, 2 * H, LANE), lambda b, t: (blk_b(b, t), 0, 0))]
    args += [att, att, el, el]
    col = lambda cb: cb * LANE // gw
    zspec = lambda blk, cb: pl.BlockSpec((ctx_len, gw), lambda b, t: (blk(b, t), col(cb)))
    in_specs += [zspec(blk_f, CB_HQ), zspec(blk_f, CB_HFF), zspec(blk_f, CB_HI),
                 zspec(blk_b, CB_HQ), zspec(blk_b, CB_HFB), zspec(blk_b, CB_HI),
                 pl.BlockSpec((1, gw), lambda b, t: (0, 0))]
    args += [z, z, z, z, z, z, lbs.reshape(1, gw)]
    ospec_f = pl.BlockSpec((ctx_len, gw), lambda b, t: (blk_f(b, t), 0))
    ospec_b = pl.BlockSpec((ctx_len, gw), lambda b, t: (blk_b(b, t), 0))
    state = pltpu.VMEM((2 * H, HEAD_DIM, HEAD_DIM), F32)
    outs = pl.pallas_call(
        functools.partial(_scans_kernel, g),
        grid=(nb, nblk + 1),
        in_specs=in_specs,
        out_specs=[ospec_f, ospec_b, ospec_f, ospec_b],
        out_shape=[jax.ShapeDtypeStruct((m, gw), F32)] * 4,
        scratch_shapes=[state, state],
        compiler_params=_cp("parallel", "arbitrary"),
        name="scans",
    )(*args)
    return outs[:2], outs[2:]


def _softmax_pv(parts):
    m = parts[0][0].max(axis=-1, keepdims=True)
    for s, _ in parts[1:]:
        m = jnp.maximum(m, s.max(axis=-1, keepdims=True))
    den, acc = None, None
    for s, v in parts:
        p = jnp.exp(s - m)
        d = jnp.sum(p, axis=-1, keepdims=True)
        a = _dot(p.astype(BF16), v)
        den = d if den is None else den + d
        acc = a if acc is None else acc + a
    return acc / den


def _na_kernel(emit_ctx, rows, *refs):
    q_ref, k_ref, v_ref, qc_ref, kc_ref, vc_ref, bias_ref = refs[:7]
    if emit_ctx:
        yl_ref, yc_ref, kb_ref, vb_ref = refs[7:]
    else:
        yl_ref, kb_ref, vb_ref = refs[7:]
    scale = HEAD_DIM ** -0.5
    win = NA_ROWS * GRID_W
    kb_ref[...] = k_ref[...].astype(BF16)
    vb_ref[...] = v_ref[...].astype(BF16)
    kc = kc_ref[...].astype(BF16)
    vc = vc_ref[...].astype(BF16)

    def body(i, carry):
        pre = []
        for t in range(NA_UNROLL):
            r = i * NA_UNROLL + t
            row0 = jnp.clip(r - NA_ROWS // 2, 0, rows - NA_ROWS)
            qs = pl.ds(pl.multiple_of(r * GRID_W, GRID_W), GRID_W)
            ks = pl.ds(pl.multiple_of(row0 * GRID_W, GRID_W), win)
            q = q_ref[qs, :].astype(BF16)
            pre.append((qs, ks, _dot_nt(q, kb_ref[ks, :]) * scale + bias_ref[0, r - row0], _dot_nt(q, kc) * scale))
        mid = []
        for qs, ks, s_win, s_ctx in pre:
            m = jnp.maximum(s_win.max(axis=-1, keepdims=True), s_ctx.max(axis=-1, keepdims=True))
            p_win, p_ctx = jnp.exp(s_win - m), jnp.exp(s_ctx - m)
            den = jnp.sum(p_win, axis=-1, keepdims=True) + jnp.sum(p_ctx, axis=-1, keepdims=True)
            mid.append((qs, ks, p_win.astype(BF16), p_ctx.astype(BF16), den))
        for qs, ks, p_win, p_ctx, den in mid:
            yl_ref[qs, :] = ((_dot(p_win, vb_ref[ks, :]) + _dot(p_ctx, vc)) / den).astype(yl_ref.dtype)
        return carry

    lax.fori_loop(0, rows // NA_UNROLL, body, 0)
    if emit_ctx:
        s = _dot_nt(qc_ref[...].astype(BF16), kc) * scale
        yc_ref[...] = _softmax_pv([(s, vc)]).astype(yc_ref.dtype)


def _na_bias_kernel(rpb_ref, o_ref):
    n = lax.broadcasted_iota(jnp.int32, (LANE, GRID_W * GRID_W), 1)
    j = lax.broadcasted_iota(jnp.int32, (LANE, GRID_W * GRID_W), 0)
    q, w = n >> 6, n & (GRID_W - 1)
    dc = jnp.clip(w - q, 1 - NA_COLS, NA_COLS - 1) + NA_COLS - 1
    onehot = jnp.where(dc == j, 1.0, 0.0).astype(F32)
    m = _dot(rpb_ref[...], onehot, precision=HIGHEST)
    c0 = jnp.clip(q[0:1] - NA_COLS // 2, 0, GRID_W - NA_COLS)
    ok = (w[0:1] >= c0) & (w[0:1] < c0 + NA_COLS)
    o_ref[...] = jnp.where(ok, m, NEG)


def _na_bias_tables(rpb):
    depth, H, nr, nc = rpb.shape
    assert GRID_W == 64 and depth * H * nr <= LANE and nc <= LANE
    flat = jnp.zeros((LANE, LANE), F32).at[:depth * H * nr, :nc].set(rpb.reshape(-1, nc).astype(F32))
    m = pl.pallas_call(
        _na_bias_kernel,
        out_shape=jax.ShapeDtypeStruct((LANE, GRID_W * GRID_W), F32),
        compiler_params=pltpu.CompilerParams(vmem_limit_bytes=VMEM_LIMIT),
        name="na_bias",
    )(flat)
    m = m[:depth * H * nr].reshape(depth, H, nr, GRID_W, GRID_W)
    tab = jnp.stack([jnp.stack([m[:, :, k - s + NA_ROWS - 1] for k in range(NA_ROWS)], axis=3)
                     for s in range(NA_ROWS)], axis=2)
    return tab.reshape(depth, H, NA_ROWS, GRID_W, NA_ROWS * GRID_W)


def _na(z, bias, nb, seq, ctx_len, emit_ctx):
    rows = seq // GRID_W
    cblk0 = nb * seq // ctx_len
    H = N_HEADS
    lat = lambda cb: pl.BlockSpec((seq, LANE), lambda b, h: (b, cb + h))
    ctx = lambda cb: pl.BlockSpec((ctx_len, LANE), lambda b, h: (cblk0 + b, cb + h))
    win = NA_ROWS * GRID_W
    in_specs = [lat(CB_NAQ), lat(CB_NAK), lat(CB_NAV), ctx(CB_NAQ), ctx(CB_NAK), ctx(CB_NAV),
                pl.BlockSpec((1, NA_ROWS, GRID_W, win), lambda b, h: (h, 0, 0, 0))]
    out_specs = [pl.BlockSpec((seq, LANE), lambda b, h: (b, h))]
    out_shape = [jax.ShapeDtypeStruct((nb * seq, GROUP_WIDTH), BF16)]
    if emit_ctx:
        out_specs.append(pl.BlockSpec((ctx_len, LANE), lambda b, h: (b, h)))
        out_shape.append(jax.ShapeDtypeStruct((nb * ctx_len, GROUP_WIDTH), BF16))
    return pl.pallas_call(
        functools.partial(_na_kernel, emit_ctx, rows),
        grid=(nb, H),
        in_specs=in_specs, out_specs=out_specs, out_shape=out_shape,
        scratch_shapes=[pltpu.VMEM((seq, LANE), BF16), pltpu.VMEM((seq, LANE), BF16)],
        compiler_params=_cp("parallel", "parallel"),
        name="na_attn",
    )(z, z, z, z, z, z, bias)


def _rms(x, w):
    return x * lax.rsqrt(jnp.mean(x * x, axis=-1, keepdims=True) + RMS_EPS) * w


def _mla_prep_kernel(cq_ref, ckv_ref, kra_ref, krb_ref, cc_ref, ss_ref, qnw_ref, kvnw_ref, wuq_ref, wukv_ref,
                     q_ref, kn_ref, kr_ref, v_ref):
    cc, ss = cc_ref[...], ss_ref[...]
    qn = _rms(cq_ref[...], qnw_ref[...]).astype(BF16)
    qa = _dot(qn, wuq_ref[...])
    for h in range(N_HEADS):
        b = 3 * LANE * h
        q_ref[:, 2 * LANE * h:2 * LANE * h + LANE] = qa[:, b:b + LANE].astype(BF16)
        q_ref[:, 2 * LANE * h + LANE:2 * LANE * (h + 1)] = (
            qa[:, b + LANE:b + 2 * LANE] * cc + qa[:, b + 2 * LANE:b + 3 * LANE] * ss).astype(BF16)
    kvn = _rms(ckv_ref[...], kvnw_ref[...]).astype(BF16)
    kv = _dot(kvn, wukv_ref[...])
    kn_ref[...] = kv[:, :GROUP_WIDTH].astype(BF16)
    v_ref[...] = kv[:, GROUP_WIDTH:].astype(BF16)
    kr_ref[...] = (kra_ref[...] * cc + krb_ref[...] * ss).astype(BF16)


def _mla_prep(z, cc, ss, qnw, kvnw, wuq, wukv, n_lat, seq, tm=512):
    m = z.shape[0]
    nlt, spt = n_lat // tm, seq // tm
    tab = lambda i: jnp.where(i < nlt, i % spt, spt)
    H = N_HEADS
    return pl.pallas_call(
        _mla_prep_kernel,
        grid=(m // tm,),
        in_specs=[pl.BlockSpec((tm, MLA_Q_RANK), lambda i: (i, CB_MQ * LANE // MLA_Q_RANK)),
                  pl.BlockSpec((tm, MLA_KV_RANK), lambda i: (i, CB_MKV * LANE // MLA_KV_RANK)),
                  pl.BlockSpec((tm, LANE), lambda i: (i, CB_MKRA)),
                  pl.BlockSpec((tm, LANE), lambda i: (i, CB_MKRB)),
                  pl.BlockSpec((tm, LANE), lambda i: (tab(i), 0)),
                  pl.BlockSpec((tm, LANE), lambda i: (tab(i), 0)),
                  pl.BlockSpec((1, MLA_Q_RANK), lambda i: (0, 0)),
                  pl.BlockSpec((1, MLA_KV_RANK), lambda i: (0, 0)),
                  pl.BlockSpec(wuq.shape, lambda i: (0, 0)),
                  pl.BlockSpec(wukv.shape, lambda i: (0, 0))],
        out_specs=[pl.BlockSpec((tm, 2 * LANE * H), lambda i: (i, 0)),
                   pl.BlockSpec((tm, GROUP_WIDTH), lambda i: (i, 0)),
                   pl.BlockSpec((tm, LANE), lambda i: (i, 0)),
                   pl.BlockSpec((tm, GROUP_WIDTH), lambda i: (i, 0))],
        out_shape=[jax.ShapeDtypeStruct((m, 2 * LANE * H), BF16),
                   jax.ShapeDtypeStruct((m, GROUP_WIDTH), BF16),
                   jax.ShapeDtypeStruct((m, LANE), BF16),
                   jax.ShapeDtypeStruct((m, GROUP_WIDTH), BF16)],
        compiler_params=_cp("parallel"),
        name="mla_prep",
    )(z, z, z, z, cc, ss, qnw.reshape(1, -1), kvnw.reshape(1, -1), wuq, wukv)


def _mla_attn_kernel(with_lat, seq, *refs):
    if with_lat:
        q_ref, knl_ref, krl_ref, vl_ref, knc_ref, krc_ref, vc_ref, y_ref, k_scr = refs
    else:
        q_ref, knc_ref, krc_ref, vc_ref, y_ref, k_scr = refs
    scale = MLA_QK_DIM ** -0.5
    nk = k_scr.shape[0]

    @pl.when(pl.program_id(2) == 0)
    def _():
        if with_lat:
            k_scr[0:seq, 0:LANE] = knl_ref[...]
            k_scr[0:seq, LANE:2 * LANE] = krl_ref[...]
        k_scr[nk - knc_ref.shape[0]:nk, 0:LANE] = knc_ref[...]
        k_scr[nk - knc_ref.shape[0]:nk, LANE:2 * LANE] = krc_ref[...]

    tq = q_ref.shape[0]
    qsub = min(tq, MLA_QSUB)
    nsub = tq // qsub
    vals = ([vl_ref] if with_lat else []) + [vc_ref]

    def qk(s):
        q = q_ref[s * qsub:(s + 1) * qsub, :]
        out = [_dot_nt(q, k_scr[0:seq, :])] if with_lat else []
        return out + [_dot_nt(q, k_scr[nk - knc_ref.shape[0]:nk, :])]

    def softmax(raw):
        m = raw[0].max(axis=-1, keepdims=True)
        for s in raw[1:]:
            m = jnp.maximum(m, s.max(axis=-1, keepdims=True))
        ps = [jnp.exp2((s - m) * (scale * LOG2E)) for s in raw]
        den = sum(jnp.sum(p, axis=-1, keepdims=True) for p in ps)
        return [p.astype(BF16) for p in ps], den

    raw = qk(0)
    for s in range(nsub):
        nxt = qk(s + 1) if s + 1 < nsub else None
        ps, den = softmax(raw)
        acc = sum(_dot(p, v[...]) for p, v in zip(ps, vals))
        y_ref[s * qsub:(s + 1) * qsub, :] = (acc / den).astype(y_ref.dtype)
        raw = nxt


def _mla_attn(q, kn, kr, v, nb, seq, ctx_len, with_lat, tq=2048):
    H = N_HEADS
    cblk0 = nb * seq // ctx_len
    nq = seq if with_lat else ctx_len
    tq = min(tq, nq)
    qblk0 = 0 if with_lat else nb * seq // tq
    ctxs = [pl.BlockSpec((ctx_len, LANE), lambda b, h, i: (cblk0 + b, h)),
            pl.BlockSpec((ctx_len, LANE), lambda b, h, i: (cblk0 + b, 0)),
            pl.BlockSpec((ctx_len, LANE), lambda b, h, i: (cblk0 + b, h))]
    lats = [pl.BlockSpec((seq, LANE), lambda b, h, i: (b, h)),
            pl.BlockSpec((seq, LANE), lambda b, h, i: (b, 0)),
            pl.BlockSpec((seq, LANE), lambda b, h, i: (b, h))]
    in_specs = [pl.BlockSpec((tq, 2 * LANE), lambda b, h, i: (qblk0 + b * (nq // tq) + i, h))]
    args = [q]
    if with_lat:
        in_specs += lats
        args += [kn, kr, v]
    in_specs += ctxs
    args += [kn, kr, v]
    nk = (seq if with_lat else 0) + ctx_len
    return pl.pallas_call(
        functools.partial(_mla_attn_kernel, with_lat, seq),
        grid=(nb, H, nq // tq),
        in_specs=in_specs,
        out_specs=pl.BlockSpec((tq, LANE), lambda b, h, i: (b * (nq // tq) + i, h)),
        out_shape=jax.ShapeDtypeStruct((nb * nq, GROUP_WIDTH), BF16),
        scratch_shapes=[pltpu.VMEM((nk, 2 * LANE), BF16)],
        compiler_params=_cp("parallel", "parallel", "arbitrary"),
        name="mla_attn_lat" if with_lat else "mla_attn_ctx",
    )(*args)


def _prep_w_in(w_in):
    gw = GROUP_WIDTH
    o_na = 4 * gw + 4 * N_HEADS
    o_mla = o_na + 3 * gw
    o_kv = o_mla + MLA_Q_RANK
    o_kr = o_kv + MLA_KV_RANK
    o_hg = o_kr + MLA_ROPE
    depth, d, n_in = w_in.shape
    kr = w_in[..., o_kr:o_hg].astype(BF16)
    k1, k2 = kr[..., 0::2], kr[..., 1::2]
    zpad = jnp.zeros((depth, d, LANE - MLA_ROPE), BF16)
    kra, krb = jnp.concatenate([k1, k2, zpad], axis=-1), jnp.concatenate([k2, k1, zpad], axis=-1)
    moves = ((CB_MQ, o_mla, MLA_Q_RANK), (CB_MKV, o_kv, MLA_KV_RANK), (CB_GAB, 4 * gw, 4 * N_HEADS),
             (CB_GQKV, 0, 4 * gw), (CB_NAQ, o_na, 3 * gw), (CB_HQ, o_hg, 5 * gw))

    def body(w_ref, kra_ref, krb_ref, o_ref):
        for cb, src, width in moves:
            lo = src // LANE * LANE
            hi = min(-(-(src + width) // LANE) * LANE, n_in)
            piece = w_ref[:, lo:hi][:, src - lo:src - lo + width].astype(BF16)
            pad = -width % LANE
            if pad:
                piece = jnp.concatenate([piece, jnp.zeros((piece.shape[0], pad), BF16)], axis=-1)
            o_ref[:, cb * LANE:cb * LANE + width + pad] = piece
        o_ref[:, CB_MKRA * LANE:(CB_MKRA + 1) * LANE] = kra_ref[...]
        o_ref[:, CB_MKRB * LANE:(CB_MKRB + 1) * LANE] = krb_ref[...]

    tr = 256
    return pl.pallas_call(
        body,
        grid=(depth, d // tr),
        in_specs=[pl.BlockSpec((None, tr, n_in), lambda l, i: (l, i, 0)),
                  pl.BlockSpec((None, tr, LANE), lambda l, i: (l, i, 0)),
                  pl.BlockSpec((None, tr, LANE), lambda l, i: (l, i, 0))],
        out_specs=pl.BlockSpec((None, tr, NP_IN), lambda l, i: (l, i, 0)),
        out_shape=jax.ShapeDtypeStruct((depth, d, NP_IN), BF16),
        compiler_params=_cp("parallel", "parallel"),
        name="w_in_layout",
    )(w_in, kra, krb)


def _prep_w_uq(w_uq):
    r = w_uq.shape[0]
    z = jnp.zeros((r, LANE - MLA_ROPE), w_uq.dtype)
    cols = []
    for h in range(N_HEADS):
        wh = w_uq[:, h * MLA_QK_DIM:(h + 1) * MLA_QK_DIM]
        rope = wh[:, MLA_NOPE:]
        r1, r2 = rope[:, 0::2], rope[:, 1::2]
        cols += [wh[:, :MLA_NOPE], r1, r2, z, r2, r1, z]
    return jnp.concatenate(cols, axis=1).astype(BF16)


def _rope_tables(seq, tm):
    n_freq = MLA_ROPE // 4
    freqs = ROPE_BASE ** (-jnp.arange(n_freq, dtype=F32) / n_freq)
    t = jnp.arange(seq)
    ang = jnp.concatenate([(t // GRID_W).astype(F32)[:, None] * freqs,
                           (t % GRID_W).astype(F32)[:, None] * freqs], -1)
    cos, sin = jnp.cos(ang), jnp.sin(ang)
    zp = jnp.zeros((seq, LANE - MLA_ROPE), F32)
    cc = jnp.concatenate([cos, cos, zp], axis=1)
    ss = jnp.concatenate([-sin, sin, zp], axis=1)
    ident = jnp.zeros((tm, LANE), F32).at[:, :MLA_ROPE].set(1.0)
    return jnp.concatenate([cc, ident], axis=0), jnp.concatenate([ss, jnp.zeros((tm, LANE), F32)], axis=0)


def kernel(x, c, ctx, c_ctx, w_ada, b_ada, w_in, gdn_conv_w, gdn_a_log, gdn_dt_bias, gdn_norm_w, na_rpb,
           mla_q_norm_w, mla_kv_norm_w, mla_w_uq, mla_w_uk, mla_w_uv, hgrn_lower_bounds, hgrn_norm_w, w_out,
           ln1_w, ln1_b, w_mlp1, w_mlp2, ln2_w, ln2_b):
    nb, seq, d = x.shape
    ctx_len = ctx.shape[1]
    depth = w_ada.shape[0]
    n_lat, n_ctx = nb * seq, nb * ctx_len
    alpha = (2 * depth) ** 0.25
    tm = 512
    tmm = 1024 if (seq % 1024 == 0 and n_ctx % 1024 == 0) else tm
    assert nb < 8 and seq % tm == 0 and n_ctx % tm == 0 and seq % ctx_len == 0 and ctx_len % CHUNK == 0

    cin = jnp.zeros((8, d), F32).at[:nb].set(c).at[nb].set(c_ctx)
    ada = _ada(cin, w_ada, b_ada)
    p_lb = jax.nn.softmax(hgrn_lower_bounds.astype(F32), axis=0)
    lbs = jnp.cumsum(p_lb, axis=0) - p_lb[0]
    cc, ss = _rope_tables(seq, tm)
    na_bias = _na_bias_tables(na_rpb)
    w_in_b, w_out_b = _prep_w_in(w_in), w_out.astype(BF16)
    w_mlp1_b, w_mlp2_b = w_mlp1.astype(BF16), w_mlp2.astype(BF16)

    xs = (x.reshape(n_lat, d), ctx.reshape(n_ctx, d))
    for l in range(depth):
        emit_ctx = l < depth - 1
        ada_r = ada[l].reshape(8 * 6, 1, d)
        z = _inproj(xs, ada_r, w_in_b, l, n_lat + n_ctx, n_lat, seq, nb, tm=tmm)

        qkv = _gdn_conv(z, gdn_conv_w[l], nb, seq, ctx_len)
        p = _gdn_gates(z, gdn_a_log[l], gdn_dt_bias[l], tm=tm)
        gdn_o, hgrn_o = _scans(*_gdn_prep(qkv, p, nb, seq, ctx_len), z, lbs[l], nb, seq, ctx_len)
        m_out = n_lat + n_ctx if emit_ctx else n_lat
        yb = _na(z, na_bias[l], nb, seq, ctx_len, emit_ctx)
        q, kn, kr, v = _mla_prep(z, cc, ss, mla_q_norm_w[l], mla_kv_norm_w[l], _prep_w_uq(mla_w_uq[l]),
                                 jnp.concatenate([mla_w_uk[l], mla_w_uv[l]], axis=1).astype(BF16),
                                 n_lat, seq, tm=tm)
        ym = [_mla_attn(q, kn, kr, v, nb, seq, ctx_len, True)]
        if emit_ctx:
            ym.append(_mla_attn(q, kn, kr, v, nb, seq, ctx_len, False))
        if len(xs) == 1 and emit_ctx:
            yb, ym = [jnp.concatenate(yb, axis=0)], [jnp.concatenate(ym, axis=0)]
        x_all = _outproj(xs if emit_ctx else xs[:1], gdn_o, yb, ym, hgrn_o, z, gdn_norm_w[l], hgrn_norm_w[l],
                         w_out_b, l, ada_r, ln1_w[l], ln1_b[l], m_out, n_lat, seq, nb, alpha)
        x_all = _mlp(x_all, w_mlp1_b, w_mlp2_b, l, ada_r, ln2_w[l], ln2_b[l], seq, nb, alpha, tm=tm)
        xs = (x_all,)
    return x_all[:n_lat].reshape(nb, seq, d)
```

```python
import functools

import numpy as np
import jax
import jax.numpy as jnp
from jax import lax
from jax.experimental import pallas as pl
from jax.experimental.pallas import tpu as pltpu

F32 = jnp.float32
BF16 = jnp.bfloat16
HIGHEST = lax.Precision.HIGHEST

GRID_W = 64
N_HEADS = 4
HEAD_DIM = 128
GROUP_WIDTH = 512
CHUNK = 64
SUB = 8
GDN_CONV = 5
NA_ROWS = 8
NA_COLS = 16
NA_UNROLL = 8
MLA_Q_RANK = 384
MLA_KV_RANK = 256
MLA_NOPE = 128
MLA_ROPE = 64
MLA_QK_DIM = MLA_NOPE + MLA_ROPE
MLA_QSUB = 512
ROPE_BASE = 10000.0
LN_EPS = 1e-5
RMS_EPS = 1e-6
LOG2E = 1.4426950408889634
NEG = -1e30

LANE = 128
CB_MQ, CB_MKRA, CB_MKV, CB_MKRB, CB_GAB = 0, 3, 4, 6, 7
CB_GQKV, CB_GGATE = 8, 20
CB_NAQ, CB_NAK, CB_NAV = 24, 28, 32
CB_HQ, CB_HFF, CB_HFB, CB_HI, CB_HG = 36, 40, 44, 48, 52
CHUNK_SHIFT = 6
PREP_CHUNKS = 4
NP_IN = 56 * LANE

VMEM_LIMIT = 48 << 20


def _cp(*sem):
    return pltpu.CompilerParams(dimension_semantics=sem, vmem_limit_bytes=VMEM_LIMIT)


def _silu(x):
    return x * jax.nn.sigmoid(x)


def _dot(a, b, **kw):
    return jnp.dot(a, b, preferred_element_type=F32, **kw)


def _dot_nt(a, b, **kw):
    return lax.dot_general(a, b, (((1,), (1,)), ((), ())), preferred_element_type=F32, **kw)


def _dot_tn(a, b, **kw):
    return lax.dot_general(a, b, (((0,), (0,)), ((), ())), preferred_element_type=F32, **kw)


def _ada_kernel(c_ref, w_ref, b_ref, o_ref):
    s = _silu(c_ref[...])
    o_ref[0] = _dot(s, w_ref[0], precision=HIGHEST) + b_ref[0]


def _ada(cin, w_ada, b_ada):
    depth, d, n = w_ada.shape
    tn = 1024
    return pl.pallas_call(
        _ada_kernel,
        grid=(depth, n // tn),
        in_specs=[pl.BlockSpec((8, d), lambda l, j: (0, 0)),
                  pl.BlockSpec((1, d, tn), lambda l, j: (l, 0, j)),
                  pl.BlockSpec((1, 1, tn), lambda l, j: (l, 0, j))],
        out_specs=pl.BlockSpec((1, 8, tn), lambda l, j: (l, 0, j)),
        out_shape=jax.ShapeDtypeStruct((depth, 8, n), F32),
        compiler_params=_cp("parallel", "parallel"),
        name="ada",
    )(cin, w_ada, b_ada.reshape(depth, 1, n))


ROW_STEP = 256


def _modulate(x_ref, sh_ref, sc_ref, xm_ref):
    sc1, sh = 1.0 + sc_ref[0], sh_ref[0]

    def body(t, carry):
        sl = pl.ds(pl.multiple_of(t * ROW_STEP, ROW_STEP), ROW_STEP)
        xm_ref[sl, :] = (x_ref[sl, :] * sc1 + sh).astype(BF16)
        return carry

    lax.fori_loop(0, x_ref.shape[0] // ROW_STEP, body, 0)


def _inproj_kernel(n_lat_tiles, *refs):
    xs, (sh_ref, sc_ref, w_ref, o_ref, xm_ref) = refs[:-5], refs[-5:]
    first = pl.program_id(1) == 0
    if len(xs) == 1:
        pl.when(first)(lambda: _modulate(xs[0], sh_ref, sc_ref, xm_ref))
    else:
        is_lat = pl.program_id(0) < n_lat_tiles
        pl.when(first & is_lat)(lambda: _modulate(xs[0], sh_ref, sc_ref, xm_ref))
        pl.when(first & jnp.logical_not(is_lat))(lambda: _modulate(xs[1], sh_ref, sc_ref, xm_ref))
    o_ref[...] = _dot(xm_ref[...], w_ref[...])


def _inproj(xs, ada_r, w, layer, m, n_lat, seq, nb, tm=512, tn=1024):
    _, d, n = w.shape
    nlt = n_lat // tm
    row = lambda i: jnp.minimum((i * tm) // seq, nb)
    if len(xs) == 1:
        x_specs = [pl.BlockSpec((tm, d), lambda i, j: (i, 0))]
    else:
        x_specs = [pl.BlockSpec((tm, d), lambda i, j: (jnp.minimum(i, nlt - 1), 0)),
                   pl.BlockSpec((tm, d), lambda i, j: (jnp.maximum(i - nlt, 0), 0),
                                pipeline_mode=pl.Buffered(1))]
    return pl.pallas_call(
        functools.partial(_inproj_kernel, nlt),
        grid=(m // tm, n // tn),
        in_specs=x_specs + [
                  pl.BlockSpec((1, 1, d), lambda i, j: (row(i) * 6 + 0, 0, 0)),
                  pl.BlockSpec((1, 1, d), lambda i, j: (row(i) * 6 + 1, 0, 0)),
                  pl.BlockSpec((None, d, tn), lambda i, j: (layer, 0, j))],
        out_specs=pl.BlockSpec((tm, tn), lambda i, j: (i, j)),
        out_shape=jax.ShapeDtypeStruct((m, n), F32),
        scratch_shapes=[pltpu.VMEM((tm, d), BF16)],
        compiler_params=_cp("parallel", "arbitrary"),
        name="inproj",
    )(*xs, ada_r, ada_r, w)


def _layernorm(r, w, b):
    mu = jnp.mean(r, axis=-1, keepdims=True)
    rc = r - mu
    var = jnp.mean(rc * rc, axis=-1, keepdims=True)
    return rc * lax.rsqrt(var + LN_EPS) * w + b


def _head_norm_gate(o, nw, gate):
    o = o * lax.rsqrt(jnp.mean(o * o, axis=-1, keepdims=True) + RMS_EPS) * nw
    return o * _silu(gate)


def _scan_mixer_out(of_ref, ob_ref, gate_ref, nw_ref):
    nw = nw_ref[...]
    heads = []
    for h in range(N_HEADS):
        hs = slice(h * LANE, (h + 1) * LANE)
        heads.append(_head_norm_gate(of_ref[:, hs] + ob_ref[:, hs], nw, gate_ref[:, hs]).astype(BF16))
    return jnp.concatenate(heads, axis=-1)


def _outproj_kernel(alpha, n_lat_tiles, n_split, *refs):
    pairs, rest = refs[:n_split], refs[n_split:]
    (gaf_ref, gab_ref, gag_ref, hgf_ref, hgb_ref, hgg_ref, nwa_ref, nwh_ref,
     w_ref, g_ref, lw_ref, lb_ref, o_ref) = rest
    gw = GROUP_WIDTH

    def run(x_ref, yb_ref, ym_ref):
        acc = _dot(_scan_mixer_out(gaf_ref, gab_ref, gag_ref, nwa_ref), w_ref[0:gw, :])
        acc += _dot(yb_ref[...], w_ref[gw:2 * gw, :])
        acc += _dot(ym_ref[...], w_ref[2 * gw:3 * gw, :])
        acc += _dot(_scan_mixer_out(hgf_ref, hgb_ref, hgg_ref, nwh_ref), w_ref[3 * gw:4 * gw, :])
        r = alpha * x_ref[...] + g_ref[0] * acc
        o_ref[...] = _layernorm(r, lw_ref[...], lb_ref[...])

    if n_split == 3:
        run(*pairs)
    else:
        is_lat = pl.program_id(0) < n_lat_tiles
        pl.when(is_lat)(lambda: run(*pairs[0::2]))
        pl.when(jnp.logical_not(is_lat))(lambda: run(*pairs[1::2]))


def _outproj(xs, gdn_o, ybs, yms, hgrn_o, z, nwa, nwh, w, layer, ada_r, lw, lb, m_out, n_lat, seq, nb, alpha,
             tm=256):
    d = w.shape[1]
    gw = GROUP_WIDTH
    nlt = n_lat // tm
    row = lambda i: jnp.minimum((i * tm) // seq, nb)
    in_specs, args = [], []
    dual = len(xs) == 2
    for arrs, width in ((xs, d), (ybs, gw), (yms, gw)):
        if dual:
            in_specs += [pl.BlockSpec((tm, width), lambda i: (jnp.minimum(i, nlt - 1), 0)),
                         pl.BlockSpec((tm, width), lambda i: (jnp.maximum(i - nlt, 0), 0))]
        else:
            in_specs += [pl.BlockSpec((tm, width), lambda i: (i, 0))]
        args += list(arrs)
    n_split = len(args)
    rowspec = pl.BlockSpec((tm, gw), lambda i: (i, 0))
    gate = lambda cb: pl.BlockSpec((tm, gw), lambda i: (i, cb * LANE // gw))
    vec = lambda n: pl.BlockSpec((1, n), lambda i: (0, 0))
    in_specs += [rowspec, rowspec, gate(CB_GGATE), rowspec, rowspec, gate(CB_HG), vec(LANE), vec(LANE),
                 pl.BlockSpec((None, d, d), lambda i: (layer, 0, 0), pipeline_mode=pl.Buffered(1)),
                 pl.BlockSpec((1, 1, d), lambda i: (row(i) * 6 + 2, 0, 0)), vec(d), vec(d)]
    args += [*gdn_o, z, *hgrn_o, z, nwa.reshape(1, LANE), nwh.reshape(1, LANE), w, ada_r,
             lw.reshape(1, d), lb.reshape(1, d)]
    return pl.pallas_call(
        functools.partial(_outproj_kernel, alpha, nlt, n_split),
        grid=(m_out // tm,),
        in_specs=in_specs,
        out_specs=pl.BlockSpec((tm, d), lambda i: (i, 0)),
        out_shape=jax.ShapeDtypeStruct((m_out, d), F32),
        compiler_params=_cp("parallel"),
        name="outproj_ln",
    )(*args)


def _mlp_kernel(alpha, x_ref, sh_ref, sc_ref, g_ref, w1_ref, w2_ref, lw_ref, lb_ref, o_ref, xm_ref, acc_ref):
    k = pl.program_id(1)

    @pl.when(k == 0)
    def _():
        _modulate(x_ref, sh_ref, sc_ref, xm_ref)
        acc_ref[...] = jnp.zeros_like(acc_ref)

    h = jnp.maximum(_dot(xm_ref[...], w1_ref[...]), 0.0)
    acc_ref[...] += _dot((h * h).astype(BF16), w2_ref[...])

    @pl.when(k == pl.num_programs(1) - 1)
    def _():
        g, lw, lb = g_ref[0], lw_ref[...], lb_ref[...]

        def body(t, carry):
            sl = pl.ds(pl.multiple_of(t * ROW_STEP, ROW_STEP), ROW_STEP)
            o_ref[sl, :] = _layernorm(alpha * x_ref[sl, :] + g * acc_ref[sl, :], lw, lb)
            return carry

        lax.fori_loop(0, x_ref.shape[0] // ROW_STEP, body, 0)


def _mlp(x_all, w1, w2, layer, ada_r, lw, lb, seq, nb, alpha, tm=512, th=1024):
    m, d = x_all.shape
    hid = w1.shape[2]
    row = lambda i: jnp.minimum((i * tm) // seq, nb)
    return pl.pallas_call(
        functools.partial(_mlp_kernel, alpha),
        grid=(m // tm, hid // th),
        in_specs=[pl.BlockSpec((tm, d), lambda i, k: (i, 0)),
                  pl.BlockSpec((1, 1, d), lambda i, k: (row(i) * 6 + 3, 0, 0)),
                  pl.BlockSpec((1, 1, d), lambda i, k: (row(i) * 6 + 4, 0, 0)),
                  pl.BlockSpec((1, 1, d), lambda i, k: (row(i) * 6 + 5, 0, 0)),
                  pl.BlockSpec((None, d, th), lambda i, k: (layer, 0, k)),
                  pl.BlockSpec((None, th, d), lambda i, k: (layer, k, 0)),
                  pl.BlockSpec((1, d), lambda i, k: (0, 0)),
                  pl.BlockSpec((1, d), lambda i, k: (0, 0))],
        out_specs=pl.BlockSpec((tm, d), lambda i, k: (i, 0)),
        out_shape=jax.ShapeDtypeStruct((m, d), F32),
        scratch_shapes=[pltpu.VMEM((tm, d), BF16), pltpu.VMEM((tm, d), F32)],
        compiler_params=_cp("parallel", "arbitrary"),
        name="mlp_ln",
    )(x_all, ada_r, ada_r, ada_r, w1, w2, lw.reshape(1, d), lb.reshape(1, d))


def _gdn_conv_kernel(seq, ctx_len, xl_ref, xc_ref, w_ref, o_ref, pad_ref):
    j = pl.program_id(1)
    w = w_ref[...]
    qscale = jnp.where(j < N_HEADS, HEAD_DIM ** -0.5, 1.0).astype(F32)
    p0 = 8 - GDN_CONV // 2
    for x_ref, nrows, o0 in ((xl_ref, seq, 0), (xc_ref, ctx_len, seq)):
        pad_ref[0:8, :] = jnp.zeros((8, LANE), F32)
        pad_ref[nrows + 8:nrows + 16, :] = jnp.zeros((8, LANE), F32)
        pad_ref[8:nrows + 8, :] = x_ref[...]
        rb = min(nrows, 256)
        for r0 in range(0, nrows, rb):
            y = pad_ref[r0 + p0:r0 + p0 + rb, :] * w[0:1, :]
            for i in range(1, GDN_CONV):
                y = y + pad_ref[r0 + p0 + i:r0 + p0 + i + rb, :] * w[i:i + 1, :]
            y = _silu(y)
            nrm = y * lax.rsqrt(jnp.sum(y * y, axis=-1, keepdims=True) + RMS_EPS) * qscale
            o_ref[o0 + r0:o0 + r0 + rb, :] = jnp.where(j < 2 * N_HEADS, nrm, y)


def _gdn_conv(z, conv_w, nb, seq, ctx_len):
    nblk = 3 * N_HEADS
    cblk0 = nb * seq // ctx_len
    return pl.pallas_call(
        functools.partial(_gdn_conv_kernel, seq, ctx_len),
        grid=(nb, nblk),
        in_specs=[pl.BlockSpec((seq, LANE), lambda b, j: (b, CB_GQKV + j)),
                  pl.BlockSpec((ctx_len, LANE), lambda b, j: (cblk0 + b, CB_GQKV + j)),
                  pl.BlockSpec((GDN_CONV, LANE), lambda b, j: (0, j))],
        out_specs=pl.BlockSpec((seq + ctx_len, LANE), lambda b, j: (b, j)),
        out_shape=jax.ShapeDtypeStruct((nb * (seq + ctx_len), nblk * LANE), F32),
        scratch_shapes=[pltpu.VMEM((seq + 16, LANE), F32)],
        compiler_params=_cp("parallel", "parallel"),
        name="gdn_conv",
    )(z, z, conv_w)


def _gdn_gates_kernel(tm, s_ref, alog_ref, dtb_ref, o_ref):
    s = s_ref[...]
    g = -jnp.exp(alog_ref[...]) * (jnp.maximum(s + dtb_ref[...], 0.0)
                                    + jnp.log1p(jnp.exp(-jnp.abs(s + dtb_ref[...]))))
    r = lax.broadcasted_iota(jnp.int32, (tm, tm), 0)
    c = lax.broadcasted_iota(jnp.int32, (tm, tm), 1)
    same = (r >> CHUNK_SHIFT) == (c >> CHUNK_SHIFT)
    lo = jnp.where(same & (c <= r), 1.0, 0.0).astype(F32)
    up = jnp.where(same & (c >= r), 1.0, 0.0).astype(F32)
    cum_f = _dot(lo, g, precision=HIGHEST)
    cum_b = _dot(up, g, precision=HIGHEST)
    col = lax.broadcasted_iota(jnp.int32, s.shape, 1)
    o_ref[...] = jnp.where(col < N_HEADS, cum_f,
                           jnp.where(col < 2 * N_HEADS, cum_b,
                                     jnp.where(col < 4 * N_HEADS, jax.nn.sigmoid(s), 0.0)))


def _gdn_gates(z, a_log, dt_bias, tm=512):
    m = z.shape[0]
    pad = lambda v: jnp.zeros((1, LANE), F32).at[0, :2 * N_HEADS].set(v.reshape(-1).astype(F32))
    return pl.pallas_call(
        functools.partial(_gdn_gates_kernel, tm),
        grid=(m // tm,),
        in_specs=[pl.BlockSpec((tm, LANE), lambda i: (i, CB_GAB)),
                  pl.BlockSpec((1, LANE), lambda i: (0, 0)),
                  pl.BlockSpec((1, LANE), lambda i: (0, 0))],
        out_specs=pl.BlockSpec((tm, LANE), lambda i: (i, 0)),
        out_shape=jax.ShapeDtypeStruct((m, LANE), F32),
        compiler_params=_cp("parallel"),
        name="gdn_gates",
    )(z, pad(a_log), pad(dt_bias))


def _tri_masks(n):
    r = lax.broadcasted_iota(jnp.int32, (n, n), 0)
    c = lax.broadcasted_iota(jnp.int32, (n, n), 1)
    return r, c


def _split(x):
    hi = x.astype(BF16)
    return hi, (x - hi.astype(F32)).astype(BF16)


def _dot3(a, b):
    return _dot(a[0], b[0]) + (_dot(a[0], b[1]) + _dot(a[1], b[0]))


def _gdn_prep_kernel(qkv_ref, p_ref, u_ref, w_ref, qt_ref, kt_ref, att_ref, el_ref):
    C, H = CHUNK, N_HEADS
    lane = lax.broadcasted_iota(jnp.int32, (C, LANE), 1)
    r, c = _tri_masks(C)
    eye = jnp.where(r == c, 1.0, 0.0).astype(F32)
    a_list, rhs_list, where_list = [], [], []
    for n in range(PREP_CHUNKS):
        rs = slice(n * C, (n + 1) * C)
        pblk = p_ref[rs, :]
        tblk = pblk.T
        col = lambda idx, pblk=pblk: jnp.sum(jnp.where(lane == idx, pblk, 0.0), axis=-1, keepdims=True)
        row = lambda idx, tblk=tblk: tblk[idx:idx + 1, :]
        for h in range(H):
            hs = slice(h * LANE, (h + 1) * LANE)
            q = qkv_ref[rs, h * LANE:(h + 1) * LANE]
            k = qkv_ref[rs, (H + h) * LANE:(H + h + 1) * LANE]
            v = qkv_ref[rs, (2 * H + h) * LANE:(2 * H + h + 1) * LANE]
            qbf, kbf = q.astype(BF16), k.astype(BF16)
            for d in range(2):
                idx = d * H + h
                cum_c, cum_r, beta_c = col(idx), row(idx), col(2 * H + idx)
                incl = (c >= r) if d else (c <= r)
                strict = (c > r) if d else (c < r)
                last = cum_r[:, 0:1] if d else cum_r[:, C - 1:C]
                decay = jnp.exp(jnp.where(incl, cum_c - cum_r, NEG))
                kb = k * beta_c
                ec = jnp.exp(cum_c)
                a_list.append(jnp.where(strict, _dot_nt(kb.astype(BF16), kbf) * decay, 0.0))
                rhs_list.append(jnp.concatenate([v * beta_c, kb * ec], axis=-1).astype(BF16))
                where_list.append((d, rs, hs))
                att_ref[d, h, rs, :] = jnp.where(incl, _dot_nt(qbf, kbf) * decay, 0.0).astype(BF16)
                qt_ref[d, rs, hs] = (q * ec).astype(BF16)
                kt_ref[d, rs, hs] = (k * jnp.exp(last - cum_c)).astype(BF16)
                el_ref[n, idx:idx + 1, :] = jnp.broadcast_to(jnp.exp(last), (1, LANE))
    ts = [eye - a for a in a_list]
    ps = [a.astype(BF16) for a in a_list]
    for _ in range(5):
        ps = [_dot(p, p).astype(BF16) for p in ps]
        ts = [t + _dot(t.astype(BF16), p) for t, p in zip(ts, ps)]
    res = [eye - t - _dot3(_split(a), _split(t)) for a, t in zip(a_list, ts)]
    ts = [t + _dot(t.astype(BF16), e.astype(BF16)) for t, e in zip(ts, res)]
    sols = [_dot(t.astype(BF16), rhs) for t, rhs in zip(ts, rhs_list)]
    for sol, (d, rs, hs) in zip(sols, where_list):
        u_ref[d, rs, hs] = sol[:, :HEAD_DIM]
        w_ref[d, rs, hs] = sol[:, HEAD_DIM:].astype(BF16)


def _gdn_prep(qkv, p, nb, seq, ctx_len):
    m = qkv.shape[0]
    H = N_HEADS
    gw = GROUP_WIDTH
    rows = PREP_CHUNKS * CHUNK
    assert seq % rows == 0 and ctx_len % rows == 0
    dspec = pl.BlockSpec((2, rows, gw), lambda i: (0, i, 0))
    nl, nc = seq // rows, ctx_len // rows

    def qkv_blk(i):
        ic = i - nb * nl
        return jnp.where(i < nb * nl, (i // nl) * (nl + nc) + i % nl, (ic // nc) * (nl + nc) + nl + ic % nc)

    return pl.pallas_call(
        _gdn_prep_kernel,
        grid=(m // rows,),
        in_specs=[pl.BlockSpec((rows, 3 * gw), lambda i: (qkv_blk(i), 0)),
                  pl.BlockSpec((rows, LANE), lambda i: (i, 0))],
        out_specs=[dspec, dspec, dspec, dspec,
                   pl.BlockSpec((2, H, rows, CHUNK), lambda i: (0, 0, i, 0)),
                   pl.BlockSpec((PREP_CHUNKS, 2 * H, LANE), lambda i: (i, 0, 0))],
        out_shape=[jax.ShapeDtypeStruct((2, m, gw), F32),
                   jax.ShapeDtypeStruct((2, m, gw), BF16),
                   jax.ShapeDtypeStruct((2, m, gw), BF16),
                   jax.ShapeDtypeStruct((2, m, gw), BF16),
                   jax.ShapeDtypeStruct((2, H, m, CHUNK), BF16),
                   jax.ShapeDtypeStruct((m // CHUNK, 2 * H, LANE), F32)],
        compiler_params=_cp("parallel"),
        name="gdn_prep",
    )(qkv, p)


def _gdn_scan_a(ins, s_ref, g, j):
    H = N_HEADS
    mid = []
    for d in range(2):
        u_ref, w_ref, qt_ref = ins[d:6:2]
        cj = g - 1 - j if d else j
        rs = pl.ds(pl.multiple_of(cj * CHUNK, CHUNK), CHUNK)
        for h in range(H):
            hs = slice(h * LANE, (h + 1) * LANE)
            sb = s_ref[d * H + h].astype(BF16)
            vnb = (u_ref[0, rs, hs] - _dot(w_ref[0, rs, hs], sb)).astype(BF16)
            mid.append((vnb, _dot(qt_ref[0, rs, hs], sb)))
    return mid


def _gdn_scan_b(ins, s_ref, o_refs, g, j, mid):
    H = N_HEADS
    for d in range(2):
        kt_ref, att_ref, el_ref = ins[6 + d::2]
        cj = g - 1 - j if d else j
        rs = pl.ds(pl.multiple_of(cj * CHUNK, CHUNK), CHUNK)
        for h in range(H):
            hs = slice(h * LANE, (h + 1) * LANE)
            idx = d * H + h
            vnb, o_state = mid[idx]
            s_ref[idx] = s_ref[idx] * el_ref[cj, idx:idx + 1, :] + _dot_tn(kt_ref[0, rs, hs], vnb)
            o_refs[d][rs, hs] = o_state + _dot(att_ref[0, h, rs, :], vnb)


def _scans_kernel(g, *refs):
    gdn_ins = refs[:12]
    qf_ref, ff_ref, if_ref, qb_ref, fb_ref, ib_ref, lb_ref = refs[12:19]
    gof_ref, gob_ref, of_ref, ob_ref, gs_ref, s_ref = refs[19:]
    C, H, nsub = CHUNK, N_HEADS, CHUNK // SUB

    @pl.when(pl.program_id(1) == 0)
    def _():
        s_ref[...] = jnp.zeros_like(s_ref)
        gs_ref[...] = jnp.zeros_like(gs_ref)

    r, c = _tri_masks(C)
    tri = (jnp.where(c <= r, 1.0, 0.0).astype(F32), jnp.where(c >= r, 1.0, 0.0).astype(F32))
    trow = lax.broadcasted_iota(jnp.int32, (C, HEAD_DIM), 0)
    srow = lax.broadcasted_iota(jnp.int32, (SUB, C), 0)
    scol = lax.broadcasted_iota(jnp.int32, (SUB, C), 1)
    chains = [(d, h) for d in range(2) for h in range(H)]
    srcs = ((qf_ref, ff_ref, if_ref, of_ref), (qb_ref, fb_ref, ib_ref, ob_ref))

    def body(j, carry):
        ph1 = []
        for d, h in chains:
            cj = g - 1 - j if d else j
            rs, hs = pl.ds(pl.multiple_of(cj * C, C), C), slice(h * LANE, (h + 1) * LANE)
            lb = lb_ref[:, hs]
            f = lb + (1.0 - lb) * jax.nn.sigmoid(srcs[d][1][rs, hs])
            ph1.append((rs, hs, f, _dot(tri[d], jnp.log(f), precision=HIGHEST)))
        gdn_mid = _gdn_scan_a(gdn_ins, gs_ref, g, j)
        ph2 = []
        for (d, h), (rs, hs, f, cum) in zip(chains, ph1):
            q_ref, _, i_ref, _ = srcs[d]
            q = _silu(q_ref[rs, hs]) * HEAD_DIM ** -0.5
            k = 1.0 - f
            vb = i_ref[rs, hs].astype(BF16)
            last = cum[0:1, :] if d else cum[C - 1:C, :]
            idx = d * H + h
            St = s_ref[idx]
            o_state = _dot_nt((q * jnp.exp(cum)).astype(BF16), St.astype(BF16))
            s_ref[idx] = St * jnp.exp(last) + _dot_tn(vb, (k * jnp.exp(last - cum)).astype(BF16))
            inter = []
            for a in range(nsub):
                sa = slice(a * SUB, (a + 1) * SUB)
                if d and a < nsub - 1:
                    cb = cum[(a + 1) * SUB:(a + 1) * SUB + 1, :]
                    kt = k * jnp.exp(jnp.where(trow >= (a + 1) * SUB, cb - cum, NEG))
                elif (not d) and a > 0:
                    cb = cum[a * SUB - 1:a * SUB, :]
                    kt = k * jnp.exp(jnp.where(trow < a * SUB, cb - cum, NEG))
                else:
                    inter.append(jnp.zeros((SUB, C), F32))
                    continue
                inter.append(_dot_nt((q[sa] * jnp.exp(cum[sa] - cb)).astype(BF16), kt.astype(BF16)))
            ph2.append((q, k, cum, vb, o_state, inter))
        _gdn_scan_b(gdn_ins, gs_ref, (gof_ref, gob_ref), g, j, gdn_mid)
        for (d, h), (rs, hs, _, _), (q, k, cum, vb, o_state, inter) in zip(chains, ph1, ph2):
            blocks = []
            for a in range(nsub):
                sa = slice(a * SUB, (a + 1) * SUB)
                qa, ka, ca, sc = q[sa], k[sa], cum[sa], inter[a]
                for t in range(SUB):
                    dec = jnp.exp(ca - ca[t:t + 1, :])
                    st = jnp.sum(qa * ka[t:t + 1, :] * dec, axis=-1, keepdims=True)
                    ok = (srow <= t) if d else (srow >= t)
                    sc = jnp.where((scol == a * SUB + t) & ok, st, sc)
                blocks.append(sc)
            scores = jnp.concatenate(blocks, axis=0)
            srcs[d][3][rs, hs] = o_state + _dot(scores.astype(BF16), vb)
        return carry

    lax.fori_loop(0, g, body, 0)


def _scans(u, w, qt, kt, att, el, z, lbs, nb, seq, ctx_len):
    m = z.shape[0]
    H = N_HEADS
    gw = GROUP_WIDTH
    g = ctx_len // CHUNK
    nblk = seq // ctx_len
    cblk0 = nb * nblk
    blk_f = lambda b, t: jnp.where(t == 0, cblk0 + b, b * nblk + t - 1)
    blk_b = lambda b, t: jnp.where(t == 0, cblk0 + b, b * nblk + nblk - t)
    in_specs, args = [], []
    for arr in (u, w, qt, kt):
        in_specs += [pl.BlockSpec((1, ctx_len, gw), lambda b, t: (0, blk_f(b, t), 0)),
                     pl.BlockSpec((1, ctx_len, gw), lambda b, t: (1, blk_b(b, t), 0))]
        args += [arr, arr]
    in_specs += [pl.BlockSpec((1, H, ctx_len, CHUNK), lambda b, t: (0, 0, blk_f(b, t), 0)),
                 pl.BlockSpec((1, H, ctx_len, CHUNK), lambda b, t: (1, 0, blk_b(b, t), 0)),
                 pl.BlockSpec((g, 2 * H, LANE), lambda b, t: (blk_f(b, t), 0, 0)),
                 pl.BlockSpec((g, 2 * H, LANE), lambda b, t: (blk_b(b, t), 0, 0))]
    args += [att, att, el, el]
    col = lambda cb: cb * LANE // gw
    zspec = lambda blk, cb: pl.BlockSpec((ctx_len, gw), lambda b, t: (blk(b, t), col(cb)))
    in_specs += [zspec(blk_f, CB_HQ), zspec(blk_f, CB_HFF), zspec(blk_f, CB_HI),
                 zspec(blk_b, CB_HQ), zspec(blk_b, CB_HFB), zspec(blk_b, CB_HI),
                 pl.BlockSpec((1, gw), lambda b, t: (0, 0))]
    args += [z, z, z, z, z, z, lbs.reshape(1, gw)]
    ospec_f = pl.BlockSpec((ctx_len, gw), lambda b, t: (blk_f(b, t), 0))
    ospec_b = pl.BlockSpec((ctx_len, gw), lambda b, t: (blk_b(b, t), 0))
    state = pltpu.VMEM((2 * H, HEAD_DIM, HEAD_DIM), F32)
    outs = pl.pallas_call(
        functools.partial(_scans_kernel, g),
        grid=(nb, nblk + 1),
        in_specs=in_specs,
        out_specs=[ospec_f, ospec_b, ospec_f, ospec_b],
        out_shape=[jax.ShapeDtypeStruct((m, gw), F32)] * 4,
        scratch_shapes=[state, state],
        compiler_params=_cp("parallel", "arbitrary"),
        name="scans",
    )(*args)
    return outs[:2], outs[2:]


def _softmax_pv(parts):
    m = parts[0][0].max(axis=-1, keepdims=True)
    for s, _ in parts[1:]:
        m = jnp.maximum(m, s.max(axis=-1, keepdims=True))
    den, acc = None, None
    for s, v in parts:
        p = jnp.exp(s - m)
        d = jnp.sum(p, axis=-1, keepdims=True)
        a = _dot(p.astype(BF16), v)
        den = d if den is None else den + d
        acc = a if acc is None else acc + a
    return acc / den


def _na_kernel(emit_ctx, rows, *refs):
    q_ref, k_ref, v_ref, qc_ref, kc_ref, vc_ref, bias_ref = refs[:7]
    if emit_ctx:
        yl_ref, yc_ref, kb_ref, vb_ref = refs[7:]
    else:
        yl_ref, kb_ref, vb_ref = refs[7:]
    scale = HEAD_DIM ** -0.5
    win = NA_ROWS * GRID_W
    kb_ref[...] = k_ref[...].astype(BF16)
    vb_ref[...] = v_ref[...].astype(BF16)
    kc = kc_ref[...].astype(BF16)
    vc = vc_ref[...].astype(BF16)

    def body(i, carry):
        pre = []
        for t in range(NA_UNROLL):
            r = i * NA_UNROLL + t
            row0 = jnp.clip(r - NA_ROWS // 2, 0, rows - NA_ROWS)
            qs = pl.ds(pl.multiple_of(r * GRID_W, GRID_W), GRID_W)
            ks = pl.ds(pl.multiple_of(row0 * GRID_W, GRID_W), win)
            q = q_ref[qs, :].astype(BF16)
            pre.append((qs, ks, _dot_nt(q, kb_ref[ks, :]) * scale + bias_ref[0, r - row0], _dot_nt(q, kc) * scale))
        mid = []
        for qs, ks, s_win, s_ctx in pre:
            m = jnp.maximum(s_win.max(axis=-1, keepdims=True), s_ctx.max(axis=-1, keepdims=True))
            p_win, p_ctx = jnp.exp(s_win - m), jnp.exp(s_ctx - m)
            den = jnp.sum(p_win, axis=-1, keepdims=True) + jnp.sum(p_ctx, axis=-1, keepdims=True)
            mid.append((qs, ks, p_win.astype(BF16), p_ctx.astype(BF16), den))
        for qs, ks, p_win, p_ctx, den in mid:
            yl_ref[qs, :] = ((_dot(p_win, vb_ref[ks, :]) + _dot(p_ctx, vc)) / den).astype(yl_ref.dtype)
        return carry

    lax.fori_loop(0, rows // NA_UNROLL, body, 0)
    if emit_ctx:
        s = _dot_nt(qc_ref[...].astype(BF16), kc) * scale
        yc_ref[...] = _softmax_pv([(s, vc)]).astype(yc_ref.dtype)


def _na_bias_kernel(rpb_ref, o_ref):
    n = lax.broadcasted_iota(jnp.int32, (LANE, GRID_W * GRID_W), 1)
    j = lax.broadcasted_iota(jnp.int32, (LANE, GRID_W * GRID_W), 0)
    q, w = n >> 6, n & (GRID_W - 1)
    dc = jnp.clip(w - q, 1 - NA_COLS, NA_COLS - 1) + NA_COLS - 1
    onehot = jnp.where(dc == j, 1.0, 0.0).astype(F32)
    m = _dot(rpb_ref[...], onehot, precision=HIGHEST)
    c0 = jnp.clip(q[0:1] - NA_COLS // 2, 0, GRID_W - NA_COLS)
    ok = (w[0:1] >= c0) & (w[0:1] < c0 + NA_COLS)
    o_ref[...] = jnp.where(ok, m, NEG)


def _na_bias_tables(rpb):
    depth, H, nr, nc = rpb.shape
    assert GRID_W == 64 and depth * H * nr <= LANE and nc <= LANE
    flat = jnp.zeros((LANE, LANE), F32).at[:depth * H * nr, :nc].set(rpb.reshape(-1, nc).astype(F32))
    m = pl.pallas_call(
        _na_bias_kernel,
        out_shape=jax.ShapeDtypeStruct((LANE, GRID_W * GRID_W), F32),
        compiler_params=pltpu.CompilerParams(vmem_limit_bytes=VMEM_LIMIT),
        name="na_bias",
    )(flat)
    m = m[:depth * H * nr].reshape(depth, H, nr, GRID_W, GRID_W)
    tab = jnp.stack([jnp.stack([m[:, :, k - s + NA_ROWS - 1] for k in range(NA_ROWS)], axis=3)
                     for s in range(NA_ROWS)], axis=2)
    return tab.reshape(depth, H, NA_ROWS, GRID_W, NA_ROWS * GRID_W)


def _na(z, bias, nb, seq, ctx_len, emit_ctx):
    rows = seq // GRID_W
    cblk0 = nb * seq // ctx_len
    H = N_HEADS
    lat = lambda cb: pl.BlockSpec((seq, LANE), lambda b, h: (b, cb + h))
    ctx = lambda cb: pl.BlockSpec((ctx_len, LANE), lambda b, h: (cblk0 + b, cb + h))
    win = NA_ROWS * GRID_W
    in_specs = [lat(CB_NAQ), lat(CB_NAK), lat(CB_NAV), ctx(CB_NAQ), ctx(CB_NAK), ctx(CB_NAV),
                pl.BlockSpec((1, NA_ROWS, GRID_W, win), lambda b, h: (h, 0, 0, 0))]
    out_specs = [pl.BlockSpec((seq, LANE), lambda b, h: (b, h))]
    out_shape = [jax.ShapeDtypeStruct((nb * seq, GROUP_WIDTH), BF16)]
    if emit_ctx:
        out_specs.append(pl.BlockSpec((ctx_len, LANE), lambda b, h: (b, h)))
        out_shape.append(jax.ShapeDtypeStruct((nb * ctx_len, GROUP_WIDTH), BF16))
    return pl.pallas_call(
        functools.partial(_na_kernel, emit_ctx, rows),
        grid=(nb, H),
        in_specs=in_specs, out_specs=out_specs, out_shape=out_shape,
        scratch_shapes=[pltpu.VMEM((seq, LANE), BF16), pltpu.VMEM((seq, LANE), BF16)],
        compiler_params=_cp("parallel", "parallel"),
        name="na_attn",
    )(z, z, z, z, z, z, bias)


def _rms(x, w):
    return x * lax.rsqrt(jnp.mean(x * x, axis=-1, keepdims=True) + RMS_EPS) * w


def _mla_prep_kernel(cq_ref, ckv_ref, kra_ref, krb_ref, cc_ref, ss_ref, qnw_ref, kvnw_ref, wuq_ref, wukv_ref,
                     q_ref, kn_ref, kr_ref, v_ref):
    cc, ss = cc_ref[...], ss_ref[...]
    qn = _rms(cq_ref[...], qnw_ref[...]).astype(BF16)
    qa = _dot(qn, wuq_ref[...])
    for h in range(N_HEADS):
        b = 3 * LANE * h
        q_ref[:, 2 * LANE * h:2 * LANE * h + LANE] = qa[:, b:b + LANE].astype(BF16)
        q_ref[:, 2 * LANE * h + LANE:2 * LANE * (h + 1)] = (
            qa[:, b + LANE:b + 2 * LANE] * cc + qa[:, b + 2 * LANE:b + 3 * LANE] * ss).astype(BF16)
    kvn = _rms(ckv_ref[...], kvnw_ref[...]).astype(BF16)
    kv = _dot(kvn, wukv_ref[...])
    kn_ref[...] = kv[:, :GROUP_WIDTH].astype(BF16)
    v_ref[...] = kv[:, GROUP_WIDTH:].astype(BF16)
    kr_ref[...] = (kra_ref[...] * cc + krb_ref[...] * ss).astype(BF16)


def _mla_prep(z, cc, ss, qnw, kvnw, wuq, wukv, n_lat, seq, tm=512):
    m = z.shape[0]
    nlt, spt = n_lat // tm, seq // tm
    tab = lambda i: jnp.where(i < nlt, i % spt, spt)
    H = N_HEADS
    return pl.pallas_call(
        _mla_prep_kernel,
        grid=(m // tm,),
        in_specs=[pl.BlockSpec((tm, MLA_Q_RANK), lambda i: (i, CB_MQ * LANE // MLA_Q_RANK)),
                  pl.BlockSpec((tm, MLA_KV_RANK), lambda i: (i, CB_MKV * LANE // MLA_KV_RANK)),
                  pl.BlockSpec((tm, LANE), lambda i: (i, CB_MKRA)),
                  pl.BlockSpec((tm, LANE), lambda i: (i, CB_MKRB)),
                  pl.BlockSpec((tm, LANE), lambda i: (tab(i), 0)),
                  pl.BlockSpec((tm, LANE), lambda i: (tab(i), 0)),
                  pl.BlockSpec((1, MLA_Q_RANK), lambda i: (0, 0)),
                  pl.BlockSpec((1, MLA_KV_RANK), lambda i: (0, 0)),
                  pl.BlockSpec(wuq.shape, lambda i: (0, 0)),
                  pl.BlockSpec(wukv.shape, lambda i: (0, 0))],
        out_specs=[pl.BlockSpec((tm, 2 * LANE * H), lambda i: (i, 0)),
                   pl.BlockSpec((tm, GROUP_WIDTH), lambda i: (i, 0)),
                   pl.BlockSpec((tm, LANE), lambda i: (i, 0)),
                   pl.BlockSpec((tm, GROUP_WIDTH), lambda i: (i, 0))],
        out_shape=[jax.ShapeDtypeStruct((m, 2 * LANE * H), BF16),
                   jax.ShapeDtypeStruct((m, GROUP_WIDTH), BF16),
                   jax.ShapeDtypeStruct((m, LANE), BF16),
                   jax.ShapeDtypeStruct((m, GROUP_WIDTH), BF16)],
        compiler_params=_cp("parallel"),
        name="mla_prep",
    )(z, z, z, z, cc, ss, qnw.reshape(1, -1), kvnw.reshape(1, -1), wuq, wukv)


def _mla_attn_kernel(with_lat, seq, *refs):
    if with_lat:
        q_ref, knl_ref, krl_ref, vl_ref, knc_ref, krc_ref, vc_ref, y_ref, k_scr = refs
    else:
        q_ref, knc_ref, krc_ref, vc_ref, y_ref, k_scr = refs
    scale = MLA_QK_DIM ** -0.5
    nk = k_scr.shape[0]

    @pl.when(pl.program_id(2) == 0)
    def _():
        if with_lat:
            k_scr[0:seq, 0:LANE] = knl_ref[...]
            k_scr[0:seq, LANE:2 * LANE] = krl_ref[...]
        k_scr[nk - knc_ref.shape[0]:nk, 0:LANE] = knc_ref[...]
        k_scr[nk - knc_ref.shape[0]:nk, LANE:2 * LANE] = krc_ref[...]

    tq = q_ref.shape[0]
    qsub = min(tq, MLA_QSUB)
    nsub = tq // qsub
    vals = ([vl_ref] if with_lat else []) + [vc_ref]

    def qk(s):
        q = q_ref[s * qsub:(s + 1) * qsub, :]
        out = [_dot_nt(q, k_scr[0:seq, :])] if with_lat else []
        return out + [_dot_nt(q, k_scr[nk - knc_ref.shape[0]:nk, :])]

    def softmax(raw):
        m = raw[0].max(axis=-1, keepdims=True)
        for s in raw[1:]:
            m = jnp.maximum(m, s.max(axis=-1, keepdims=True))
        ps = [jnp.exp2((s - m) * (scale * LOG2E)) for s in raw]
        den = sum(jnp.sum(p, axis=-1, keepdims=True) for p in ps)
        return [p.astype(BF16) for p in ps], den

    raw = qk(0)
    for s in range(nsub):
        nxt = qk(s + 1) if s + 1 < nsub else None
        ps, den = softmax(raw)
        acc = sum(_dot(p, v[...]) for p, v in zip(ps, vals))
        y_ref[s * qsub:(s + 1) * qsub, :] = (acc / den).astype(y_ref.dtype)
        raw = nxt


def _mla_attn(q, kn, kr, v, nb, seq, ctx_len, with_lat, tq=2048):
    H = N_HEADS
    cblk0 = nb * seq // ctx_len
    nq = seq if with_lat else ctx_len
    tq = min(tq, nq)
    qblk0 = 0 if with_lat else nb * seq // tq
    ctxs = [pl.BlockSpec((ctx_len, LANE), lambda b, h, i: (cblk0 + b, h)),
            pl.BlockSpec((ctx_len, LANE), lambda b, h, i: (cblk0 + b, 0)),
            pl.BlockSpec((ctx_len, LANE), lambda b, h, i: (cblk0 + b, h))]
    lats = [pl.BlockSpec((seq, LANE), lambda b, h, i: (b, h)),
            pl.BlockSpec((seq, LANE), lambda b, h, i: (b, 0)),
            pl.BlockSpec((seq, LANE), lambda b, h, i: (b, h))]
    in_specs = [pl.BlockSpec((tq, 2 * LANE), lambda b, h, i: (qblk0 + b * (nq // tq) + i, h))]
    args = [q]
    if with_lat:
        in_specs += lats
        args += [kn, kr, v]
    in_specs += ctxs
    args += [kn, kr, v]
    nk = (seq if with_lat else 0) + ctx_len
    return pl.pallas_call(
        functools.partial(_mla_attn_kernel, with_lat, seq),
        grid=(nb, H, nq // tq),
        in_specs=in_specs,
        out_specs=pl.BlockSpec((tq, LANE), lambda b, h, i: (b * (nq // tq) + i, h)),
        out_shape=jax.ShapeDtypeStruct((nb * nq, GROUP_WIDTH), BF16),
        scratch_shapes=[pltpu.VMEM((nk, 2 * LANE), BF16)],
        compiler_params=_cp("parallel", "parallel", "arbitrary"),
        name="mla_attn_lat" if with_lat else "mla_attn_ctx",
    )(*args)


def _prep_w_in(w_in):
    gw = GROUP_WIDTH
    o_na = 4 * gw + 4 * N_HEADS
    o_mla = o_na + 3 * gw
    o_kv = o_mla + MLA_Q_RANK
    o_kr = o_kv + MLA_KV_RANK
    o_hg = o_kr + MLA_ROPE
    depth, d, n_in = w_in.shape
    kr = w_in[..., o_kr:o_hg].astype(BF16)
    k1, k2 = kr[..., 0::2], kr[..., 1::2]
    zpad = jnp.zeros((depth, d, LANE - MLA_ROPE), BF16)
    kra, krb = jnp.concatenate([k1, k2, zpad], axis=-1), jnp.concatenate([k2, k1, zpad], axis=-1)
    moves = ((CB_MQ, o_mla, MLA_Q_RANK), (CB_MKV, o_kv, MLA_KV_RANK), (CB_GAB, 4 * gw, 4 * N_HEADS),
             (CB_GQKV, 0, 4 * gw), (CB_NAQ, o_na, 3 * gw), (CB_HQ, o_hg, 5 * gw))

    def body(w_ref, kra_ref, krb_ref, o_ref):
        for cb, src, width in moves:
            lo = src // LANE * LANE
            hi = min(-(-(src + width) // LANE) * LANE, n_in)
            piece = w_ref[:, lo:hi][:, src - lo:src - lo + width].astype(BF16)
            pad = -width % LANE
            if pad:
                piece = jnp.concatenate([piece, jnp.zeros((piece.shape[0], pad), BF16)], axis=-1)
            o_ref[:, cb * LANE:cb * LANE + width + pad] = piece
        o_ref[:, CB_MKRA * LANE:(CB_MKRA + 1) * LANE] = kra_ref[...]
        o_ref[:, CB_MKRB * LANE:(CB_MKRB + 1) * LANE] = krb_ref[...]

    tr = 256
    return pl.pallas_call(
        body,
        grid=(depth, d // tr),
        in_specs=[pl.BlockSpec((None, tr, n_in), lambda l, i: (l, i, 0)),
                  pl.BlockSpec((None, tr, LANE), lambda l, i: (l, i, 0)),
                  pl.BlockSpec((None, tr, LANE), lambda l, i: (l, i, 0))],
        out_specs=pl.BlockSpec((None, tr, NP_IN), lambda l, i: (l, i, 0)),
        out_shape=jax.ShapeDtypeStruct((depth, d, NP_IN), BF16),
        compiler_params=_cp("parallel", "parallel"),
        name="w_in_layout",
    )(w_in, kra, krb)


def _prep_w_uq(w_uq):
    r = w_uq.shape[0]
    z = jnp.zeros((r, LANE - MLA_ROPE), w_uq.dtype)
    cols = []
    for h in range(N_HEADS):
        wh = w_uq[:, h * MLA_QK_DIM:(h + 1) * MLA_QK_DIM]
        rope = wh[:, MLA_NOPE:]
        r1, r2 = rope[:, 0::2], rope[:, 1::2]
        cols += [wh[:, :MLA_NOPE], r1, r2, z, r2, r1, z]
    return jnp.concatenate(cols, axis=1).astype(BF16)


def _rope_tables(seq, tm):
    n_freq = MLA_ROPE // 4
    freqs = ROPE_BASE ** (-jnp.arange(n_freq, dtype=F32) / n_freq)
    t = jnp.arange(seq)
    ang = jnp.concatenate([(t // GRID_W).astype(F32)[:, None] * freqs,
                           (t % GRID_W).astype(F32)[:, None] * freqs], -1)
    cos, sin = jnp.cos(ang), jnp.sin(ang)
    zp = jnp.zeros((seq, LANE - MLA_ROPE), F32)
    cc = jnp.concatenate([cos, cos, zp], axis=1)
    ss = jnp.concatenate([-sin, sin, zp], axis=1)
    ident = jnp.zeros((tm, LANE), F32).at[:, :MLA_ROPE].set(1.0)
    return jnp.concatenate([cc, ident], axis=0), jnp.concatenate([ss, jnp.zeros((tm, LANE), F32)], axis=0)


def kernel(x, c, ctx, c_ctx, w_ada, b_ada, w_in, gdn_conv_w, gdn_a_log, gdn_dt_bias, gdn_norm_w, na_rpb,
           mla_q_norm_w, mla_kv_norm_w, mla_w_uq, mla_w_uk, mla_w_uv, hgrn_lower_bounds, hgrn_norm_w, w_out,
           ln1_w, ln1_b, w_mlp1, w_mlp2, ln2_w, ln2_b):
    nb, seq, d = x.shape
    ctx_len = ctx.shape[1]
    depth = w_ada.shape[0]
    n_lat, n_ctx = nb * seq, nb * ctx_len
    alpha = (2 * depth) ** 0.25
    tm = 512
    tmm = 1024 if (seq % 1024 == 0 and n_ctx % 1024 == 0) else tm
    assert nb < 8 and seq % tm == 0 and n_ctx % tm == 0 and seq % ctx_len == 0 and ctx_len % CHUNK == 0

    cin = jnp.zeros((8, d), F32).at[:nb].set(c).at[nb].set(c_ctx)
    ada = _ada(cin, w_ada, b_ada)
    p_lb = jax.nn.softmax(hgrn_lower_bounds.astype(F32), axis=0)
    lbs = jnp.cumsum(p_lb, axis=0) - p_lb[0]
    cc, ss = _rope_tables(seq, tm)
    na_bias = _na_bias_tables(na_rpb)
    w_in_b, w_out_b = _prep_w_in(w_in), w_out.astype(BF16)
    w_mlp1_b, w_mlp2_b = w_mlp1.astype(BF16), w_mlp2.astype(BF16)

    xs = (x.reshape(n_lat, d), ctx.reshape(n_ctx, d))
    for l in range(depth):
        emit_ctx = l < depth - 1
        ada_r = ada[l].reshape(8 * 6, 1, d)
        z = _inproj(xs, ada_r, w_in_b, l, n_lat + n_ctx, n_lat, seq, nb, tm=tmm)

        qkv = _gdn_conv(z, gdn_conv_w[l], nb, seq, ctx_len)
        p = _gdn_gates(z, gdn_a_log[l], gdn_dt_bias[l], tm=tm // 2)
        gdn_o, hgrn_o = _scans(*_gdn_prep(qkv, p, nb, seq, ctx_len), z, lbs[l], nb, seq, ctx_len)
        m_out = n_lat + n_ctx if emit_ctx else n_lat
        yb = _na(z, na_bias[l], nb, seq, ctx_len, emit_ctx)
        q, kn, kr, v = _mla_prep(z, cc, ss, mla_q_norm_w[l], mla_kv_norm_w[l], _prep_w_uq(mla_w_uq[l]),
                                 jnp.concatenate([mla_w_uk[l], mla_w_uv[l]], axis=1).astype(BF16),
                                 n_lat, seq, tm=tm)
        ym = [_mla_attn(q, kn, kr, v, nb, seq, ctx_len, True)]
        if emit_ctx:
            ym.append(_mla_attn(q, kn, kr, v, nb, seq, ctx_len, False))
        if len(xs) == 1 and emit_ctx:
            yb, ym = [jnp.concatenate(yb, axis=0)], [jnp.concatenate(ym, axis=0)]
        x_all = _outproj(xs if emit_ctx else xs[:1], gdn_o, yb, ym, hgrn_o, z, gdn_norm_w[l], hgrn_norm_w[l],
                         w_out_b, l, ada_r, ln1_w[l], ln1_b[l], m_out, n_lat, seq, nb, alpha)
        x_all = _mlp(x_all, w_mlp1_b, w_mlp2_b, l, ada_r, ln2_w[l], ln2_b[l], seq, nb, alpha, tm=tm)
        xs = (x_all,)
    return x_all[:n_lat].reshape(nb, seq, d)
```

```python
import functools

import jax
import jax.numpy as jnp
from jax import lax
from jax.experimental import pallas as pl
from jax.experimental.pallas import tpu as pltpu

F32 = jnp.float32
BF16 = jnp.bfloat16
HIGHEST = lax.Precision.HIGHEST

GRID_W = 64
N_HEADS = 4
HEAD_DIM = 128
GROUP_WIDTH = 512
CHUNK = 64
SUB = 8
GDN_CONV = 5
NA_ROWS = 8
NA_COLS = 16
NA_UNROLL = 16
MLA_Q_RANK = 384
MLA_KV_RANK = 256
MLA_NOPE = 128
MLA_ROPE = 64
MLA_QK_DIM = MLA_NOPE + MLA_ROPE
MLA_QSUB = 512
ROPE_BASE = 10000.0
LN_EPS = 1e-5
RMS_EPS = 1e-6
LOG2E = 1.4426950408889634
NEG = -1e30

LANE = 128
CB_MQ, CB_MKRA, CB_MKV, CB_MKRB, CB_GAB = 0, 3, 4, 6, 7
CB_GQKV, CB_GGATE = 8, 20
CB_NAQ, CB_NAK, CB_NAV = 24, 28, 32
CB_HQ, CB_HFF, CB_HFB, CB_HI, CB_HG = 36, 40, 44, 48, 52
CHUNK_SHIFT = 6
PREP_CHUNKS = 4
NP_IN = 56 * LANE

VMEM_LIMIT = 48 << 20


def _cp(*sem):
    return pltpu.CompilerParams(dimension_semantics=sem, vmem_limit_bytes=VMEM_LIMIT)


def _silu(x):
    return x * jax.nn.sigmoid(x)


def _dot(a, b, **kw):
    return jnp.dot(a, b, preferred_element_type=F32, **kw)


def _dot_nt(a, b, **kw):
    return lax.dot_general(a, b, (((1,), (1,)), ((), ())), preferred_element_type=F32, **kw)


def _dot_tn(a, b, **kw):
    return lax.dot_general(a, b, (((0,), (0,)), ((), ())), preferred_element_type=F32, **kw)


def _ada_kernel(c_ref, w_ref, b_ref, o_ref):
    s = _silu(c_ref[...])
    o_ref[0] = _dot(s, w_ref[0], precision=HIGHEST) + b_ref[0]


def _ada(cin, w_ada, b_ada):
    depth, d, n = w_ada.shape
    tn = 1024
    return pl.pallas_call(
        _ada_kernel,
        grid=(depth, n // tn),
        in_specs=[pl.BlockSpec((8, d), lambda l, j: (0, 0)),
                  pl.BlockSpec((1, d, tn), lambda l, j: (l, 0, j)),
                  pl.BlockSpec((1, 1, tn), lambda l, j: (l, 0, j))],
        out_specs=pl.BlockSpec((1, 8, tn), lambda l, j: (l, 0, j)),
        out_shape=jax.ShapeDtypeStruct((depth, 8, n), F32),
        compiler_params=_cp("parallel", "parallel"),
        name="ada",
    )(cin, w_ada, b_ada.reshape(depth, 1, n))


ROW_STEP = 256


def _modulate(x_ref, sh_ref, sc_ref, xm_ref):
    sc1, sh = 1.0 + sc_ref[0], sh_ref[0]

    def body(t, carry):
        sl = pl.ds(pl.multiple_of(t * ROW_STEP, ROW_STEP), ROW_STEP)
        xm_ref[sl, :] = (x_ref[sl, :] * sc1 + sh).astype(BF16)
        return carry

    lax.fori_loop(0, x_ref.shape[0] // ROW_STEP, body, 0)


def _inproj_kernel(n_lat_tiles, *refs):
    xs, (sh_ref, sc_ref, w_ref, o_ref, xm_ref) = refs[:-5], refs[-5:]
    first = pl.program_id(1) == 0
    if len(xs) == 1:
        pl.when(first)(lambda: _modulate(xs[0], sh_ref, sc_ref, xm_ref))
    else:
        is_lat = pl.program_id(0) < n_lat_tiles
        pl.when(first & is_lat)(lambda: _modulate(xs[0], sh_ref, sc_ref, xm_ref))
        pl.when(first & jnp.logical_not(is_lat))(lambda: _modulate(xs[1], sh_ref, sc_ref, xm_ref))
    o_ref[...] = _dot(xm_ref[...], w_ref[...])


def _inproj(xs, ada_r, w, layer, m, n_lat, seq, nb, tm=512, tn=1024):
    _, d, n = w.shape
    nlt = n_lat // tm
    row = lambda i: jnp.minimum((i * tm) // seq, nb)
    if len(xs) == 1:
        x_specs = [pl.BlockSpec((tm, d), lambda i, j: (i, 0))]
    else:
        x_specs = [pl.BlockSpec((tm, d), lambda i, j: (jnp.minimum(i, nlt - 1), 0)),
                   pl.BlockSpec((tm, d), lambda i, j: (jnp.maximum(i - nlt, 0), 0),
                                pipeline_mode=pl.Buffered(1))]
    return pl.pallas_call(
        functools.partial(_inproj_kernel, nlt),
        grid=(m // tm, n // tn),
        in_specs=x_specs + [
                  pl.BlockSpec((1, 1, d), lambda i, j: (row(i) * 6 + 0, 0, 0)),
                  pl.BlockSpec((1, 1, d), lambda i, j: (row(i) * 6 + 1, 0, 0)),
                  pl.BlockSpec((None, d, tn), lambda i, j: (layer, 0, j))],
        out_specs=pl.BlockSpec((tm, tn), lambda i, j: (i, j)),
        out_shape=jax.ShapeDtypeStruct((m, n), F32),
        scratch_shapes=[pltpu.VMEM((tm, d), BF16)],
        compiler_params=_cp("parallel", "arbitrary"),
        name="inproj",
    )(*xs, ada_r, ada_r, w)


def _layernorm(r, w, b):
    mu = jnp.mean(r, axis=-1, keepdims=True)
    rc = r - mu
    var = jnp.mean(rc * rc, axis=-1, keepdims=True)
    return rc * lax.rsqrt(var + LN_EPS) * w + b


def _head_norm_gate(o, nw, gate):
    o = o * lax.rsqrt(jnp.mean(o * o, axis=-1, keepdims=True) + RMS_EPS) * nw
    return o * _silu(gate)


def _scan_mixer_out(of_ref, ob_ref, gate_ref, nw_ref):
    nw = nw_ref[...]
    heads = []
    for h in range(N_HEADS):
        hs = slice(h * LANE, (h + 1) * LANE)
        heads.append(_head_norm_gate(of_ref[:, hs] + ob_ref[:, hs], nw, gate_ref[:, hs]).astype(BF16))
    return jnp.concatenate(heads, axis=-1)


def _outproj_kernel(alpha, n_lat_tiles, n_split, *refs):
    pairs, rest = refs[:n_split], refs[n_split:]
    (gaf_ref, gab_ref, gag_ref, hgf_ref, hgb_ref, hgg_ref, nwa_ref, nwh_ref,
     w_ref, g_ref, lw_ref, lb_ref, o_ref) = rest
    gw = GROUP_WIDTH

    def run(x_ref, yb_ref, ym_ref):
        acc = _dot(_scan_mixer_out(gaf_ref, gab_ref, gag_ref, nwa_ref), w_ref[0:gw, :])
        acc += _dot(yb_ref[...], w_ref[gw:2 * gw, :])
        acc += _dot(ym_ref[...], w_ref[2 * gw:3 * gw, :])
        acc += _dot(_scan_mixer_out(hgf_ref, hgb_ref, hgg_ref, nwh_ref), w_ref[3 * gw:4 * gw, :])
        r = alpha * x_ref[...] + g_ref[0] * acc
        o_ref[...] = _layernorm(r, lw_ref[...], lb_ref[...])

    if n_split == 3:
        run(*pairs)
    else:
        is_lat = pl.program_id(0) < n_lat_tiles
        pl.when(is_lat)(lambda: run(*pairs[0::2]))
        pl.when(jnp.logical_not(is_lat))(lambda: run(*pairs[1::2]))


def _outproj(xs, gdn_o, ybs, yms, hgrn_o, z, nwa, nwh, w, layer, ada_r, lw, lb, m_out, n_lat, seq, nb, alpha,
             tm=256):
    d = w.shape[1]
    gw = GROUP_WIDTH
    nlt = n_lat // tm
    row = lambda i: jnp.minimum((i * tm) // seq, nb)
    in_specs, args = [], []
    dual = len(xs) == 2
    for arrs, width in ((xs, d), (ybs, gw), (yms, gw)):
        if dual:
            in_specs += [pl.BlockSpec((tm, width), lambda i: (jnp.minimum(i, nlt - 1), 0)),
                         pl.BlockSpec((tm, width), lambda i: (jnp.maximum(i - nlt, 0), 0))]
        else:
            in_specs += [pl.BlockSpec((tm, width), lambda i: (i, 0))]
        args += list(arrs)
    n_split = len(args)
    rowspec = pl.BlockSpec((tm, gw), lambda i: (i, 0))
    gate = lambda cb: pl.BlockSpec((tm, gw), lambda i: (i, cb * LANE // gw))
    vec = lambda n: pl.BlockSpec((1, n), lambda i: (0, 0))
    in_specs += [rowspec, rowspec, gate(CB_GGATE), rowspec, rowspec, gate(CB_HG), vec(LANE), vec(LANE),
                 pl.BlockSpec((None, d, d), lambda i: (layer, 0, 0), pipeline_mode=pl.Buffered(1)),
                 pl.BlockSpec((1, 1, d), lambda i: (row(i) * 6 + 2, 0, 0)), vec(d), vec(d)]
    args += [*gdn_o, z, *hgrn_o, z, nwa.reshape(1, LANE), nwh.reshape(1, LANE), w, ada_r,
             lw.reshape(1, d), lb.reshape(1, d)]
    return pl.pallas_call(
        functools.partial(_outproj_kernel, alpha, nlt, n_split),
        grid=(m_out // tm,),
        in_specs=in_specs,
        out_specs=pl.BlockSpec((tm, d), lambda i: (i, 0)),
        out_shape=jax.ShapeDtypeStruct((m_out, d), F32),
        compiler_params=_cp("parallel"),
        name="outproj_ln",
    )(*args)


def _mlp_kernel(alpha, x_ref, sh_ref, sc_ref, g_ref, w1_ref, w2_ref, lw_ref, lb_ref, o_ref, xm_ref, acc_ref):
    k = pl.program_id(1)

    @pl.when(k == 0)
    def _():
        _modulate(x_ref, sh_ref, sc_ref, xm_ref)
        acc_ref[...] = jnp.zeros_like(acc_ref)

    h = jnp.maximum(_dot(xm_ref[...], w1_ref[...]), 0.0)
    acc_ref[...] += _dot((h * h).astype(BF16), w2_ref[...])

    @pl.when(k == pl.num_programs(1) - 1)
    def _():
        g, lw, lb = g_ref[0], lw_ref[...], lb_ref[...]

        def body(t, carry):
            sl = pl.ds(pl.multiple_of(t * ROW_STEP, ROW_STEP), ROW_STEP)
            o_ref[sl, :] = _layernorm(alpha * x_ref[sl, :] + g * acc_ref[sl, :], lw, lb)
            return carry

        lax.fori_loop(0, x_ref.shape[0] // ROW_STEP, body, 0)


def _mlp(x_all, w1, w2, layer, ada_r, lw, lb, seq, nb, alpha, tm=512, th=1024):
    m, d = x_all.shape
    hid = w1.shape[2]
    row = lambda i: jnp.minimum((i * tm) // seq, nb)
    return pl.pallas_call(
        functools.partial(_mlp_kernel, alpha),
        grid=(m // tm, hid // th),
        in_specs=[pl.BlockSpec((tm, d), lambda i, k: (i, 0)),
                  pl.BlockSpec((1, 1, d), lambda i, k: (row(i) * 6 + 3, 0, 0)),
                  pl.BlockSpec((1, 1, d), lambda i, k: (row(i) * 6 + 4, 0, 0)),
                  pl.BlockSpec((1, 1, d), lambda i, k: (row(i) * 6 + 5, 0, 0)),
                  pl.BlockSpec((None, d, th), lambda i, k: (layer, 0, k)),
                  pl.BlockSpec((None, th, d), lambda i, k: (layer, k, 0)),
                  pl.BlockSpec((1, d), lambda i, k: (0, 0)),
                  pl.BlockSpec((1, d), lambda i, k: (0, 0))],
        out_specs=pl.BlockSpec((tm, d), lambda i, k: (i, 0)),
        out_shape=jax.ShapeDtypeStruct((m, d), F32),
        scratch_shapes=[pltpu.VMEM((tm, d), BF16), pltpu.VMEM((tm, d), F32)],
        compiler_params=_cp("parallel", "arbitrary"),
        name="mlp_ln",
    )(x_all, ada_r, ada_r, ada_r, w1, w2, lw.reshape(1, d), lb.reshape(1, d))


def _gdn_conv_kernel(seq, ctx_len, xl_ref, xc_ref, w_ref, o_ref, pad_ref):
    j = pl.program_id(1)
    w = w_ref[...]
    qscale = jnp.where(j < N_HEADS, HEAD_DIM ** -0.5, 1.0).astype(F32)
    p0 = 8 - GDN_CONV // 2
    for x_ref, nrows, o0 in ((xl_ref, seq, 0), (xc_ref, ctx_len, seq)):
        pad_ref[0:8, :] = jnp.zeros((8, LANE), F32)
        pad_ref[nrows + 8:nrows + 16, :] = jnp.zeros((8, LANE), F32)
        pad_ref[8:nrows + 8, :] = x_ref[...]
        rb = min(nrows, 256)
        for r0 in range(0, nrows, rb):
            y = pad_ref[r0 + p0:r0 + p0 + rb, :] * w[0:1, :]
            for i in range(1, GDN_CONV):
                y = y + pad_ref[r0 + p0 + i:r0 + p0 + i + rb, :] * w[i:i + 1, :]
            y = _silu(y)
            nrm = y * lax.rsqrt(jnp.sum(y * y, axis=-1, keepdims=True) + RMS_EPS) * qscale
            o_ref[o0 + r0:o0 + r0 + rb, :] = jnp.where(j < 2 * N_HEADS, nrm, y)


def _gdn_conv(z, conv_w, nb, seq, ctx_len):
    nblk = 3 * N_HEADS
    cblk0 = nb * seq // ctx_len
    return pl.pallas_call(
        functools.partial(_gdn_conv_kernel, seq, ctx_len),
        grid=(nb, nblk),
        in_specs=[pl.BlockSpec((seq, LANE), lambda b, j: (b, CB_GQKV + j)),
                  pl.BlockSpec((ctx_len, LANE), lambda b, j: (cblk0 + b, CB_GQKV + j)),
                  pl.BlockSpec((GDN_CONV, LANE), lambda b, j: (0, j))],
        out_specs=pl.BlockSpec((seq + ctx_len, LANE), lambda b, j: (b, j)),
        out_shape=jax.ShapeDtypeStruct((nb * (seq + ctx_len), nblk * LANE), F32),
        scratch_shapes=[pltpu.VMEM((seq + 16, LANE), F32)],
        compiler_params=_cp("parallel", "parallel"),
        name="gdn_conv",
    )(z, z, conv_w)


def _gdn_gates_kernel(tm, s_ref, alog_ref, dtb_ref, o_ref):
    s = s_ref[...]
    g = -jnp.exp(alog_ref[...]) * (jnp.maximum(s + dtb_ref[...], 0.0)
                                    + jnp.log1p(jnp.exp(-jnp.abs(s + dtb_ref[...]))))
    r = lax.broadcasted_iota(jnp.int32, (tm, tm), 0)
    c = lax.broadcasted_iota(jnp.int32, (tm, tm), 1)
    same = (r >> CHUNK_SHIFT) == (c >> CHUNK_SHIFT)
    lo = jnp.where(same & (c <= r), 1.0, 0.0).astype(F32)
    up = jnp.where(same & (c >= r), 1.0, 0.0).astype(F32)
    cum_f = _dot(lo, g, precision=HIGHEST)
    cum_b = _dot(up, g, precision=HIGHEST)
    col = lax.broadcasted_iota(jnp.int32, s.shape, 1)
    o_ref[...] = jnp.where(col < N_HEADS, cum_f,
                           jnp.where(col < 2 * N_HEADS, cum_b,
                                     jnp.where(col < 4 * N_HEADS, jax.nn.sigmoid(s), 0.0)))


def _gdn_gates(z, a_log, dt_bias, tm=512):
    m = z.shape[0]
    pad = lambda v: jnp.zeros((1, LANE), F32).at[0, :2 * N_HEADS].set(v.reshape(-1).astype(F32))
    return pl.pallas_call(
        functools.partial(_gdn_gates_kernel, tm),
        grid=(m // tm,),
        in_specs=[pl.BlockSpec((tm, LANE), lambda i: (i, CB_GAB)),
                  pl.BlockSpec((1, LANE), lambda i: (0, 0)),
                  pl.BlockSpec((1, LANE), lambda i: (0, 0))],
        out_specs=pl.BlockSpec((tm, LANE), lambda i: (i, 0)),
        out_shape=jax.ShapeDtypeStruct((m, LANE), F32),
        compiler_params=_cp("parallel"),
        name="gdn_gates",
    )(z, pad(a_log), pad(dt_bias))


def _tri_masks(n):
    r = lax.broadcasted_iota(jnp.int32, (n, n), 0)
    c = lax.broadcasted_iota(jnp.int32, (n, n), 1)
    return r, c


def _split(x):
    hi = x.astype(BF16)
    return hi, (x - hi.astype(F32)).astype(BF16)


def _dot3(a, b):
    return _dot(a[0], b[0]) + (_dot(a[0], b[1]) + _dot(a[1], b[0]))


def _gdn_prep_kernel(qkv_ref, p_ref, u_ref, w_ref, qt_ref, kt_ref, att_ref, el_ref):
    C, H = CHUNK, N_HEADS
    lane = lax.broadcasted_iota(jnp.int32, (C, LANE), 1)
    r, c = _tri_masks(C)
    eye = jnp.where(r == c, 1.0, 0.0).astype(F32)
    a_list, rhs_list, where_list = [], [], []
    for n in range(PREP_CHUNKS):
        rs = slice(n * C, (n + 1) * C)
        pblk = p_ref[rs, :]
        tblk = pblk.T
        col = lambda idx, pblk=pblk: jnp.sum(jnp.where(lane == idx, pblk, 0.0), axis=-1, keepdims=True)
        row = lambda idx, tblk=tblk: tblk[idx:idx + 1, :]
        for h in range(H):
            hs = slice(h * LANE, (h + 1) * LANE)
            q = qkv_ref[rs, h * LANE:(h + 1) * LANE]
            k = qkv_ref[rs, (H + h) * LANE:(H + h + 1) * LANE]
            v = qkv_ref[rs, (2 * H + h) * LANE:(2 * H + h + 1) * LANE]
            qbf, kbf = q.astype(BF16), k.astype(BF16)
            for d in range(2):
                idx = d * H + h
                cum_c, cum_r, beta_c = col(idx), row(idx), col(2 * H + idx)
                incl = (c >= r) if d else (c <= r)
                strict = (c > r) if d else (c < r)
                last = cum_r[:, 0:1] if d else cum_r[:, C - 1:C]
                decay = jnp.exp(jnp.where(incl, cum_c - cum_r, NEG))
                kb = k * beta_c
                ec = jnp.exp(cum_c)
                a_list.append(jnp.where(strict, _dot_nt(kb.astype(BF16), kbf) * decay, 0.0))
                rhs_list.append(jnp.concatenate([v * beta_c, kb * ec], axis=-1).astype(BF16))
                where_list.append((d, rs, hs))
                att_ref[d, h, rs, :] = jnp.where(incl, _dot_nt(qbf, kbf) * decay, 0.0).astype(BF16)
                qt_ref[d, rs, hs] = (q * ec).astype(BF16)
                kt_ref[d, rs, hs] = (k * jnp.exp(last - cum_c)).astype(BF16)
                el_ref[n, idx:idx + 1, :] = jnp.broadcast_to(jnp.exp(last), (1, LANE))
    ts = [eye - a for a in a_list]
    ps = [a.astype(BF16) for a in a_list]
    for _ in range(5):
        ps = [_dot(p, p).astype(BF16) for p in ps]
        ts = [t + _dot(t.astype(BF16), p) for t, p in zip(ts, ps)]
    res = [eye - t - _dot3(_split(a), _split(t)) for a, t in zip(a_list, ts)]
    ts = [t + _dot(t.astype(BF16), e.astype(BF16)) for t, e in zip(ts, res)]
    sols = [_dot(t.astype(BF16), rhs) for t, rhs in zip(ts, rhs_list)]
    for sol, (d, rs, hs) in zip(sols, where_list):
        u_ref[d, rs, hs] = sol[:, :HEAD_DIM]
        w_ref[d, rs, hs] = sol[:, HEAD_DIM:].astype(BF16)


def _gdn_prep(qkv, p, nb, seq, ctx_len):
    m = qkv.shape[0]
    H = N_HEADS
    gw = GROUP_WIDTH
    rows = PREP_CHUNKS * CHUNK
    assert seq % rows == 0 and ctx_len % rows == 0
    dspec = pl.BlockSpec((2, rows, gw), lambda i: (0, i, 0))
    nl, nc = seq // rows, ctx_len // rows

    def qkv_blk(i):
        ic = i - nb * nl
        return jnp.where(i < nb * nl, (i // nl) * (nl + nc) + i % nl, (ic // nc) * (nl + nc) + nl + ic % nc)

    return pl.pallas_call(
        _gdn_prep_kernel,
        grid=(m // rows,),
        in_specs=[pl.BlockSpec((rows, 3 * gw), lambda i: (qkv_blk(i), 0)),
                  pl.BlockSpec((rows, LANE), lambda i: (i, 0))],
        out_specs=[dspec, dspec, dspec, dspec,
                   pl.BlockSpec((2, H, rows, CHUNK), lambda i: (0, 0, i, 0)),
                   pl.BlockSpec((PREP_CHUNKS, 2 * H, LANE), lambda i: (i, 0, 0))],
        out_shape=[jax.ShapeDtypeStruct((2, m, gw), F32),
                   jax.ShapeDtypeStruct((2, m, gw), BF16),
                   jax.ShapeDtypeStruct((2, m, gw), BF16),
                   jax.ShapeDtypeStruct((2, m, gw), BF16),
                   jax.ShapeDtypeStruct((2, H, m, CHUNK), BF16),
                   jax.ShapeDtypeStruct((m // CHUNK, 2 * H, LANE), F32)],
        compiler_params=_cp("parallel"),
        name="gdn_prep",
    )(qkv, p)


def _gdn_scan_a(ins, s_ref, g, j):
    H = N_HEADS
    mid = []
    for d in range(2):
        u_ref, w_ref, qt_ref = ins[d:6:2]
        cj = g - 1 - j if d else j
        rs = pl.ds(pl.multiple_of(cj * CHUNK, CHUNK), CHUNK)
        for h in range(H):
            hs = slice(h * LANE, (h + 1) * LANE)
            sb = s_ref[d * H + h].astype(BF16)
            vnb = (u_ref[0, rs, hs] - _dot(w_ref[0, rs, hs], sb)).astype(BF16)
            mid.append((vnb, _dot(qt_ref[0, rs, hs], sb)))
    return mid


def _gdn_scan_b(ins, s_ref, o_refs, g, j, mid):
    H = N_HEADS
    for d in range(2):
        kt_ref, att_ref, el_ref = ins[6 + d::2]
        cj = g - 1 - j if d else j
        rs = pl.ds(pl.multiple_of(cj * CHUNK, CHUNK), CHUNK)
        for h in range(H):
            hs = slice(h * LANE, (h + 1) * LANE)
            idx = d * H + h
            vnb, o_state = mid[idx]
            s_ref[idx] = s_ref[idx] * el_ref[cj, idx:idx + 1, :] + _dot_tn(kt_ref[0, rs, hs], vnb)
            o_refs[d][rs, hs] = o_state + _dot(att_ref[0, h, rs, :], vnb)


def _scans_kernel(g, *refs):
    gdn_ins = refs[:12]
    qf_ref, ff_ref, if_ref, qb_ref, fb_ref, ib_ref, lb_ref = refs[12:19]
    gof_ref, gob_ref, of_ref, ob_ref, gs_ref, s_ref = refs[19:]
    C, H, nsub = CHUNK, N_HEADS, CHUNK // SUB

    @pl.when(pl.program_id(1) == 0)
    def _():
        s_ref[...] = jnp.zeros_like(s_ref)
        gs_ref[...] = jnp.zeros_like(gs_ref)

    r, c = _tri_masks(C)
    tri = (jnp.where(c <= r, 1.0, 0.0).astype(F32), jnp.where(c >= r, 1.0, 0.0).astype(F32))
    trow = lax.broadcasted_iota(jnp.int32, (C, HEAD_DIM), 0)
    srow = lax.broadcasted_iota(jnp.int32, (SUB, C), 0)
    scol = lax.broadcasted_iota(jnp.int32, (SUB, C), 1)
    chains = [(d, h) for d in range(2) for h in range(H)]
    srcs = ((qf_ref, ff_ref, if_ref, of_ref), (qb_ref, fb_ref, ib_ref, ob_ref))

    def body(j, carry):
        ph1 = []
        for d, h in chains:
            cj = g - 1 - j if d else j
            rs, hs = pl.ds(pl.multiple_of(cj * C, C), C), slice(h * LANE, (h + 1) * LANE)
            lb = lb_ref[:, hs]
            f = lb + (1.0 - lb) * jax.nn.sigmoid(srcs[d][1][rs, hs])
            ph1.append((rs, hs, f, _dot(tri[d], jnp.log(f), precision=HIGHEST)))
        gdn_mid = _gdn_scan_a(gdn_ins, gs_ref, g, j)
        ph2 = []
        for (d, h), (rs, hs, f, cum) in zip(chains, ph1):
            q_ref, _, i_ref, _ = srcs[d]
            q = _silu(q_ref[rs, hs]) * HEAD_DIM ** -0.5
            k = 1.0 - f
            vb = i_ref[rs, hs].astype(BF16)
            last = cum[0:1, :] if d else cum[C - 1:C, :]
            idx = d * H + h
            St = s_ref[idx]
            o_state = _dot_nt((q * jnp.exp(cum)).astype(BF16), St.astype(BF16))
            s_ref[idx] = St * jnp.exp(last) + _dot_tn(vb, (k * jnp.exp(last - cum)).astype(BF16))
            inter = []
            for a in range(nsub):
                sa = slice(a * SUB, (a + 1) * SUB)
                if d and a < nsub - 1:
                    cb = cum[(a + 1) * SUB:(a + 1) * SUB + 1, :]
                    kt = k * jnp.exp(jnp.where(trow >= (a + 1) * SUB, cb - cum, NEG))
                elif (not d) and a > 0:
                    cb = cum[a * SUB - 1:a * SUB, :]
                    kt = k * jnp.exp(jnp.where(trow < a * SUB, cb - cum, NEG))
                else:
                    inter.append(jnp.zeros((SUB, C), F32))
                    continue
                inter.append(_dot_nt((q[sa] * jnp.exp(cum[sa] - cb)).astype(BF16), kt.astype(BF16)))
            ph2.append((q, k, cum, vb, o_state, inter))
        _gdn_scan_b(gdn_ins, gs_ref, (gof_ref, gob_ref), g, j, gdn_mid)
        for (d, h), (rs, hs, _, _), (q, k, cum, vb, o_state, inter) in zip(chains, ph1, ph2):
            blocks = []
            for a in range(nsub):
                sa = slice(a * SUB, (a + 1) * SUB)
                qa, ka, ca, sc = q[sa], k[sa], cum[sa], inter[a]
                for t in range(SUB):
                    dec = jnp.exp(ca - ca[t:t + 1, :])
                    st = jnp.sum(qa * ka[t:t + 1, :] * dec, axis=-1, keepdims=True)
                    ok = (srow <= t) if d else (srow >= t)
                    sc = jnp.where((scol == a * SUB + t) & ok, st, sc)
                blocks.append(sc)
            scores = jnp.concatenate(blocks, axis=0)
            srcs[d][3][rs, hs] = o_state + _dot(scores.astype(BF16), vb)
        return carry

    lax.fori_loop(0, g, body, 0)


def _scans(u, w, qt, kt, att, el, z, lbs, nb, seq, ctx_len):
    m = z.shape[0]
    H = N_HEADS
    gw = GROUP_WIDTH
    g = ctx_len // CHUNK
    nblk = seq // ctx_len
    cblk0 = nb * nblk
    blk_f = lambda b, t: jnp.where(t == 0, cblk0 + b, b * nblk + t - 1)
    blk_b = lambda b, t: jnp.where(t == 0, cblk0 + b, b * nblk + nblk - t)
    in_specs, args = [], []
    for arr in (u, w, qt, kt):
        in_specs += [pl.BlockSpec((1, ctx_len, gw), lambda b, t: (0, blk_f(b, t), 0)),
                     pl.BlockSpec((1, ctx_len, gw), lambda b, t: (1, blk_b(b, t), 0))]
        args += [arr, arr]
    in_specs += [pl.BlockSpec((1, H, ctx_len, CHUNK), lambda b, t: (0, 0, blk_f(b, t), 0)),
                 pl.BlockSpec((1, H, ctx_len, CHUNK), lambda b, t: (1, 0, blk_b(b, t), 0)),
                 pl.BlockSpec((g, 2 * H, LANE), lambda b, t: (blk_f(b, t), 0, 0)),
                 pl.BlockSpec((g, 2 * H, LANE), lambda b, t: (blk_b(b, t), 0, 0))]
    args += [att, att, el, el]
    col = lambda cb: cb * LANE // gw
    zspec = lambda blk, cb: pl.BlockSpec((ctx_len, gw), lambda b, t: (blk(b, t), col(cb)))
    in_specs += [zspec(blk_f, CB_HQ), zspec(blk_f, CB_HFF), zspec(blk_f, CB_HI),
                 zspec(blk_b, CB_HQ), zspec(blk_b, CB_HFB), zspec(blk_b, CB_HI),
                 pl.BlockSpec((1, gw), lambda b, t: (0, 0))]
    args += [z, z, z, z, z, z, lbs.reshape(1, gw)]
    ospec_f = pl.BlockSpec((ctx_len, gw), lambda b, t: (blk_f(b, t), 0))
    ospec_b = pl.BlockSpec((ctx_len, gw), lambda b, t: (blk_b(b, t), 0))
    state = pltpu.VMEM((2 * H, HEAD_DIM, HEAD_DIM), F32)
    outs = pl.pallas_call(
        functools.partial(_scans_kernel, g),
        grid=(nb, nblk + 1),
        in_specs=in_specs,
        out_specs=[ospec_f, ospec_b, ospec_f, ospec_b],
        out_shape=[jax.ShapeDtypeStruct((m, gw), F32)] * 4,
        scratch_shapes=[state, state],
        compiler_params=_cp("parallel", "arbitrary"),
        name="scans",
    )(*args)
    return outs[:2], outs[2:]


def _softmax_pv(parts):
    m = parts[0][0].max(axis=-1, keepdims=True)
    for s, _ in parts[1:]:
        m = jnp.maximum(m, s.max(axis=-1, keepdims=True))
    den, acc = None, None
    for s, v in parts:
        p = jnp.exp(s - m)
        d = jnp.sum(p, axis=-1, keepdims=True)
        a = _dot(p.astype(BF16), v)
        den = d if den is None else den + d
        acc = a if acc is None else acc + a
    return acc / den


def _na_kernel(emit_ctx, rows, *refs):
    q_ref, k_ref, v_ref, qc_ref, kc_ref, vc_ref, bias_ref = refs[:7]
    if emit_ctx:
        yl_ref, yc_ref, kb_ref, vb_ref = refs[7:]
    else:
        yl_ref, kb_ref, vb_ref = refs[7:]
    scale = HEAD_DIM ** -0.5
    win = NA_ROWS * GRID_W
    kb_ref[...] = k_ref[...].astype(BF16)
    vb_ref[...] = v_ref[...].astype(BF16)
    kc = kc_ref[...].astype(BF16)
    vc = vc_ref[...].astype(BF16)

    def body(i, carry):
        pre = []
        for t in range(NA_UNROLL):
            r = i * NA_UNROLL + t
            row0 = jnp.clip(r - NA_ROWS // 2, 0, rows - NA_ROWS)
            qs = pl.ds(pl.multiple_of(r * GRID_W, GRID_W), GRID_W)
            ks = pl.ds(pl.multiple_of(row0 * GRID_W, GRID_W), win)
            q = q_ref[qs, :].astype(BF16)
            pre.append((qs, ks, _dot_nt(q, kb_ref[ks, :]) * scale + bias_ref[0, r - row0], _dot_nt(q, kc) * scale))
        mid = []
        for qs, ks, s_win, s_ctx in pre:
            m = jnp.maximum(s_win.max(axis=-1, keepdims=True), s_ctx.max(axis=-1, keepdims=True))
            p_win, p_ctx = jnp.exp(s_win - m), jnp.exp(s_ctx - m)
            den = jnp.sum(p_win, axis=-1, keepdims=True) + jnp.sum(p_ctx, axis=-1, keepdims=True)
            mid.append((qs, ks, p_win.astype(BF16), p_ctx.astype(BF16), den))
        for qs, ks, p_win, p_ctx, den in mid:
            yl_ref[qs, :] = ((_dot(p_win, vb_ref[ks, :]) + _dot(p_ctx, vc)) / den).astype(yl_ref.dtype)
        return carry

    lax.fori_loop(0, rows // NA_UNROLL, body, 0)
    if emit_ctx:
        s = _dot_nt(qc_ref[...].astype(BF16), kc) * scale
        yc_ref[...] = _softmax_pv([(s, vc)]).astype(yc_ref.dtype)


def _na_bias_kernel(rpb_ref, o_ref):
    n = lax.broadcasted_iota(jnp.int32, (LANE, GRID_W * GRID_W), 1)
    j = lax.broadcasted_iota(jnp.int32, (LANE, GRID_W * GRID_W), 0)
    q, w = n >> 6, n & (GRID_W - 1)
    dc = jnp.clip(w - q, 1 - NA_COLS, NA_COLS - 1) + NA_COLS - 1
    onehot = jnp.where(dc == j, 1.0, 0.0).astype(F32)
    m = _dot(rpb_ref[...], onehot, precision=HIGHEST)
    c0 = jnp.clip(q[0:1] - NA_COLS // 2, 0, GRID_W - NA_COLS)
    ok = (w[0:1] >= c0) & (w[0:1] < c0 + NA_COLS)
    o_ref[...] = jnp.where(ok, m, NEG)


def _na_bias_tables(rpb):
    depth, H, nr, nc = rpb.shape
    assert GRID_W == 64 and depth * H * nr <= LANE and nc <= LANE
    flat = jnp.zeros((LANE, LANE), F32).at[:depth * H * nr, :nc].set(rpb.reshape(-1, nc).astype(F32))
    m = pl.pallas_call(
        _na_bias_kernel,
        out_shape=jax.ShapeDtypeStruct((LANE, GRID_W * GRID_W), F32),
        compiler_params=pltpu.CompilerParams(vmem_limit_bytes=VMEM_LIMIT),
        name="na_bias",
    )(flat)
    m = m[:depth * H * nr].reshape(depth, H, nr, GRID_W, GRID_W)
    tab = jnp.stack([jnp.stack([m[:, :, k - s + NA_ROWS - 1] for k in range(NA_ROWS)], axis=3)
                     for s in range(NA_ROWS)], axis=2)
    return tab.reshape(depth, H, NA_ROWS, GRID_W, NA_ROWS * GRID_W)


def _na(z, bias, nb, seq, ctx_len, emit_ctx):
    rows = seq // GRID_W
    cblk0 = nb * seq // ctx_len
    H = N_HEADS
    lat = lambda cb: pl.BlockSpec((seq, LANE), lambda b, h: (b, cb + h))
    ctx = lambda cb: pl.BlockSpec((ctx_len, LANE), lambda b, h: (cblk0 + b, cb + h))
    win = NA_ROWS * GRID_W
    in_specs = [lat(CB_NAQ), lat(CB_NAK), lat(CB_NAV), ctx(CB_NAQ), ctx(CB_NAK), ctx(CB_NAV),
                pl.BlockSpec((1, NA_ROWS, GRID_W, win), lambda b, h: (h, 0, 0, 0))]
    out_specs = [pl.BlockSpec((seq, LANE), lambda b, h: (b, h))]
    out_shape = [jax.ShapeDtypeStruct((nb * seq, GROUP_WIDTH), BF16)]
    if emit_ctx:
        out_specs.append(pl.BlockSpec((ctx_len, LANE), lambda b, h: (b, h)))
        out_shape.append(jax.ShapeDtypeStruct((nb * ctx_len, GROUP_WIDTH), BF16))
    return pl.pallas_call(
        functools.partial(_na_kernel, emit_ctx, rows),
        grid=(nb, H),
        in_specs=in_specs, out_specs=out_specs, out_shape=out_shape,
        scratch_shapes=[pltpu.VMEM((seq, LANE), BF16), pltpu.VMEM((seq, LANE), BF16)],
        compiler_params=_cp("parallel", "parallel"),
        name="na_attn",
    )(z, z, z, z, z, z, bias)


def _rms(x, w):
    return x * lax.rsqrt(jnp.mean(x * x, axis=-1, keepdims=True) + RMS_EPS) * w


def _mla_prep_kernel(cq_ref, ckv_ref, kra_ref, krb_ref, cc_ref, ss_ref, qnw_ref, kvnw_ref, wuq_ref, wukv_ref,
                     q_ref, kn_ref, kr_ref, v_ref):
    cc, ss = cc_ref[...], ss_ref[...]
    qn = _rms(cq_ref[...], qnw_ref[...]).astype(BF16)
    qa = _dot(qn, wuq_ref[...])
    for h in range(N_HEADS):
        b = 3 * LANE * h
        q_ref[:, 2 * LANE * h:2 * LANE * h + LANE] = qa[:, b:b + LANE].astype(BF16)
        q_ref[:, 2 * LANE * h + LANE:2 * LANE * (h + 1)] = (
            qa[:, b + LANE:b + 2 * LANE] * cc + qa[:, b + 2 * LANE:b + 3 * LANE] * ss).astype(BF16)
    kvn = _rms(ckv_ref[...], kvnw_ref[...]).astype(BF16)
    kv = _dot(kvn, wukv_ref[...])
    kn_ref[...] = kv[:, :GROUP_WIDTH].astype(BF16)
    v_ref[...] = kv[:, GROUP_WIDTH:].astype(BF16)
    kr_ref[...] = (kra_ref[...] * cc + krb_ref[...] * ss).astype(BF16)


def _mla_prep(z, cc, ss, qnw, kvnw, wuq, wukv, n_lat, seq, tm=512):
    m = z.shape[0]
    nlt, spt = n_lat // tm, seq // tm
    tab = lambda i: jnp.where(i < nlt, i % spt, spt)
    H = N_HEADS
    return pl.pallas_call(
        _mla_prep_kernel,
        grid=(m // tm,),
        in_specs=[pl.BlockSpec((tm, MLA_Q_RANK), lambda i: (i, CB_MQ * LANE // MLA_Q_RANK)),
                  pl.BlockSpec((tm, MLA_KV_RANK), lambda i: (i, CB_MKV * LANE // MLA_KV_RANK)),
                  pl.BlockSpec((tm, LANE), lambda i: (i, CB_MKRA)),
                  pl.BlockSpec((tm, LANE), lambda i: (i, CB_MKRB)),
                  pl.BlockSpec((tm, LANE), lambda i: (tab(i), 0)),
                  pl.BlockSpec((tm, LANE), lambda i: (tab(i), 0)),
                  pl.BlockSpec((1, MLA_Q_RANK), lambda i: (0, 0)),
                  pl.BlockSpec((1, MLA_KV_RANK), lambda i: (0, 0)),
                  pl.BlockSpec(wuq.shape, lambda i: (0, 0)),
                  pl.BlockSpec(wukv.shape, lambda i: (0, 0))],
        out_specs=[pl.BlockSpec((tm, 2 * LANE * H), lambda i: (i, 0)),
                   pl.BlockSpec((tm, GROUP_WIDTH), lambda i: (i, 0)),
                   pl.BlockSpec((tm, LANE), lambda i: (i, 0)),
                   pl.BlockSpec((tm, GROUP_WIDTH), lambda i: (i, 0))],
        out_shape=[jax.ShapeDtypeStruct((m, 2 * LANE * H), BF16),
                   jax.ShapeDtypeStruct((m, GROUP_WIDTH), BF16),
                   jax.ShapeDtypeStruct((m, LANE), BF16),
                   jax.ShapeDtypeStruct((m, GROUP_WIDTH), BF16)],
        compiler_params=_cp("parallel"),
        name="mla_prep",
    )(z, z, z, z, cc, ss, qnw.reshape(1, -1), kvnw.reshape(1, -1), wuq, wukv)


def _mla_attn_kernel(with_lat, seq, *refs):
    if with_lat:
        q_ref, knl_ref, krl_ref, vl_ref, knc_ref, krc_ref, vc_ref, y_ref, k_scr = refs
    else:
        q_ref, knc_ref, krc_ref, vc_ref, y_ref, k_scr = refs
    scale = MLA_QK_DIM ** -0.5
    nk = k_scr.shape[0]

    @pl.when(pl.program_id(2) == 0)
    def _():
        if with_lat:
            k_scr[0:seq, 0:LANE] = knl_ref[...]
            k_scr[0:seq, LANE:2 * LANE] = krl_ref[...]
        k_scr[nk - knc_ref.shape[0]:nk, 0:LANE] = knc_ref[...]
        k_scr[nk - knc_ref.shape[0]:nk, LANE:2 * LANE] = krc_ref[...]

    tq = q_ref.shape[0]
    qsub = min(tq, MLA_QSUB)
    nsub = tq // qsub
    vals = ([vl_ref] if with_lat else []) + [vc_ref]

    def qk(s):
        q = q_ref[s * qsub:(s + 1) * qsub, :]
        out = [_dot_nt(q, k_scr[0:seq, :])] if with_lat else []
        return out + [_dot_nt(q, k_scr[nk - knc_ref.shape[0]:nk, :])]

    def softmax(raw):
        m = raw[0].max(axis=-1, keepdims=True)
        for s in raw[1:]:
            m = jnp.maximum(m, s.max(axis=-1, keepdims=True))
        ps = [jnp.exp2((s - m) * (scale * LOG2E)) for s in raw]
        den = sum(jnp.sum(p, axis=-1, keepdims=True) for p in ps)
        return [p.astype(BF16) for p in ps], den

    raw = qk(0)
    for s in range(nsub):
        nxt = qk(s + 1) if s + 1 < nsub else None
        ps, den = softmax(raw)
        acc = sum(_dot(p, v[...]) for p, v in zip(ps, vals))
        y_ref[s * qsub:(s + 1) * qsub, :] = (acc / den).astype(y_ref.dtype)
        raw = nxt


def _mla_attn(q, kn, kr, v, nb, seq, ctx_len, with_lat, tq=2048):
    H = N_HEADS
    cblk0 = nb * seq // ctx_len
    nq = seq if with_lat else ctx_len
    tq = min(tq, nq)
    qblk0 = 0 if with_lat else nb * seq // tq
    ctxs = [pl.BlockSpec((ctx_len, LANE), lambda b, h, i: (cblk0 + b, h)),
            pl.BlockSpec((ctx_len, LANE), lambda b, h, i: (cblk0 + b, 0)),
            pl.BlockSpec((ctx_len, LANE), lambda b, h, i: (cblk0 + b, h))]
    lats = [pl.BlockSpec((seq, LANE), lambda b, h, i: (b, h)),
            pl.BlockSpec((seq, LANE), lambda b, h, i: (b, 0)),
            pl.BlockSpec((seq, LANE), lambda b, h, i: (b, h))]
    in_specs = [pl.BlockSpec((tq, 2 * LANE), lambda b, h, i: (qblk0 + b * (nq // tq) + i, h))]
    args = [q]
    if with_lat:
        in_specs += lats
        args += [kn, kr, v]
    in_specs += ctxs
    args += [kn, kr, v]
    nk = (seq if with_lat else 0) + ctx_len
    return pl.pallas_call(
        functools.partial(_mla_attn_kernel, with_lat, seq),
        grid=(nb, H, nq // tq),
        in_specs=in_specs,
        out_specs=pl.BlockSpec((tq, LANE), lambda b, h, i: (b * (nq // tq) + i, h)),
        out_shape=jax.ShapeDtypeStruct((nb * nq, GROUP_WIDTH), BF16),
        scratch_shapes=[pltpu.VMEM((nk, 2 * LANE), BF16)],
        compiler_params=_cp("parallel", "parallel", "arbitrary"),
        name="mla_attn_lat" if with_lat else "mla_attn_ctx",
    )(*args)


def _prep_w_in(w_in):
    gw = GROUP_WIDTH
    o_na = 4 * gw + 4 * N_HEADS
    o_mla = o_na + 3 * gw
    o_kv = o_mla + MLA_Q_RANK
    o_kr = o_kv + MLA_KV_RANK
    o_hg = o_kr + MLA_ROPE
    depth, d, n_in = w_in.shape
    kr = w_in[..., o_kr:o_hg].astype(BF16)
    k1, k2 = kr[..., 0::2], kr[..., 1::2]
    zpad = jnp.zeros((depth, d, LANE - MLA_ROPE), BF16)
    kra, krb = jnp.concatenate([k1, k2, zpad], axis=-1), jnp.concatenate([k2, k1, zpad], axis=-1)
    moves = ((CB_MQ, o_mla, MLA_Q_RANK), (CB_MKV, o_kv, MLA_KV_RANK), (CB_GAB, 4 * gw, 4 * N_HEADS),
             (CB_GQKV, 0, 4 * gw), (CB_NAQ, o_na, 3 * gw), (CB_HQ, o_hg, 5 * gw))

    def body(w_ref, kra_ref, krb_ref, o_ref):
        for cb, src, width in moves:
            lo = src // LANE * LANE
            hi = min(-(-(src + width) // LANE) * LANE, n_in)
            piece = w_ref[:, lo:hi][:, src - lo:src - lo + width].astype(BF16)
            pad = -width % LANE
            if pad:
                piece = jnp.concatenate([piece, jnp.zeros((piece.shape[0], pad), BF16)], axis=-1)
            o_ref[:, cb * LANE:cb * LANE + width + pad] = piece
        o_ref[:, CB_MKRA * LANE:(CB_MKRA + 1) * LANE] = kra_ref[...]
        o_ref[:, CB_MKRB * LANE:(CB_MKRB + 1) * LANE] = krb_ref[...]

    tr = 256
    return pl.pallas_call(
        body,
        grid=(depth, d // tr),
        in_specs=[pl.BlockSpec((None, tr, n_in), lambda l, i: (l, i, 0)),
                  pl.BlockSpec((None, tr, LANE), lambda l, i: (l, i, 0)),
                  pl.BlockSpec((None, tr, LANE), lambda l, i: (l, i, 0))],
        out_specs=pl.BlockSpec((None, tr, NP_IN), lambda l, i: (l, i, 0)),
        out_shape=jax.ShapeDtypeStruct((depth, d, NP_IN), BF16),
        compiler_params=_cp("parallel", "parallel"),
        name="w_in_layout",
    )(w_in, kra, krb)


def _prep_w_uq(w_uq):
    r = w_uq.shape[0]
    z = jnp.zeros((r, LANE - MLA_ROPE), w_uq.dtype)
    cols = []
    for h in range(N_HEADS):
        wh = w_uq[:, h * MLA_QK_DIM:(h + 1) * MLA_QK_DIM]
        rope = wh[:, MLA_NOPE:]
        r1, r2 = rope[:, 0::2], rope[:, 1::2]
        cols += [wh[:, :MLA_NOPE], r1, r2, z, r2, r1, z]
    return jnp.concatenate(cols, axis=1).astype(BF16)


def _rope_tables(seq, tm):
    n_freq = MLA_ROPE // 4
    freqs = ROPE_BASE ** (-jnp.arange(n_freq, dtype=F32) / n_freq)
    t = jnp.arange(seq)
    ang = jnp.concatenate([(t // GRID_W).astype(F32)[:, None] * freqs,
                           (t % GRID_W).astype(F32)[:, None] * freqs], -1)
    cos, sin = jnp.cos(ang), jnp.sin(ang)
    zp = jnp.zeros((seq, LANE - MLA_ROPE), F32)
    cc = jnp.concatenate([cos, cos, zp], axis=1)
    ss = jnp.concatenate([-sin, sin, zp], axis=1)
    ident = jnp.zeros((tm, LANE), F32).at[:, :MLA_ROPE].set(1.0)
    return jnp.concatenate([cc, ident], axis=0), jnp.concatenate([ss, jnp.zeros((tm, LANE), F32)], axis=0)


def kernel(x, c, ctx, c_ctx, w_ada, b_ada, w_in, gdn_conv_w, gdn_a_log, gdn_dt_bias, gdn_norm_w, na_rpb,
           mla_q_norm_w, mla_kv_norm_w, mla_w_uq, mla_w_uk, mla_w_uv, hgrn_lower_bounds, hgrn_norm_w, w_out,
           ln1_w, ln1_b, w_mlp1, w_mlp2, ln2_w, ln2_b):
    nb, seq, d = x.shape
    ctx_len = ctx.shape[1]
    depth = w_ada.shape[0]
    n_lat, n_ctx = nb * seq, nb * ctx_len
    alpha = (2 * depth) ** 0.25
    tm = 512
    tmm = 1024 if (seq % 1024 == 0 and n_ctx % 1024 == 0) else tm
    assert nb < 8 and seq % tm == 0 and n_ctx % tm == 0 and seq % ctx_len == 0 and ctx_len % CHUNK == 0

    cin = jnp.zeros((8, d), F32).at[:nb].set(c).at[nb].set(c_ctx)
    ada = _ada(cin, w_ada, b_ada)
    p_lb = jax.nn.softmax(hgrn_lower_bounds.astype(F32), axis=0)
    lbs = jnp.cumsum(p_lb, axis=0) - p_lb[0]
    cc, ss = _rope_tables(seq, tm)
    na_bias = _na_bias_tables(na_rpb)
    w_in_b, w_out_b = _prep_w_in(w_in), w_out.astype(BF16)
    w_mlp1_b, w_mlp2_b = w_mlp1.astype(BF16), w_mlp2.astype(BF16)

    xs = (x.reshape(n_lat, d), ctx.reshape(n_ctx, d))
    for l in range(depth):
        emit_ctx = l < depth - 1
        ada_r = ada[l].reshape(8 * 6, 1, d)
        z = _inproj(xs, ada_r, w_in_b, l, n_lat + n_ctx, n_lat, seq, nb, tm=tmm)

        qkv = _gdn_conv(z, gdn_conv_w[l], nb, seq, ctx_len)
        p = _gdn_gates(z, gdn_a_log[l], gdn_dt_bias[l], tm=tm // 2)
        gdn_o, hgrn_o = _scans(*_gdn_prep(qkv, p, nb, seq, ctx_len), z, lbs[l], nb, seq, ctx_len)
        m_out = n_lat + n_ctx if emit_ctx else n_lat
        yb = _na(z, na_bias[l], nb, seq, ctx_len, emit_ctx)
        q, kn, kr, v = _mla_prep(z, cc, ss, mla_q_norm_w[l], mla_kv_norm_w[l], _prep_w_uq(mla_w_uq[l]),
                                 jnp.concatenate([mla_w_uk[l], mla_w_uv[l]], axis=1).astype(BF16),
                                 n_lat, seq, tm=tm)
        ym = [_mla_attn(q, kn, kr, v, nb, seq, ctx_len, True)]
        if emit_ctx:
            ym.append(_mla_attn(q, kn, kr, v, nb, seq, ctx_len, False))
        if len(xs) == 1 and emit_ctx:
            yb, ym = [jnp.concatenate(yb, axis=0)], [jnp.concatenate(ym, axis=0)]
        x_all = _outproj(xs if emit_ctx else xs[:1], gdn_o, yb, ym, hgrn_o, z, gdn_norm_w[l], hgrn_norm_w[l],
                         w_out_b, l, ada_r, ln1_w[l], ln1_b[l], m_out, n_lat, seq, nb, alpha)
        x_all = _mlp(x_all, w_mlp1_b, w_mlp2_b, l, ada_r, ln2_w[l], ln2_b[l], seq, nb, alpha, tm=tm)
        xs = (x_all,)
    return x_all[:n_lat].reshape(nb, seq, d)
```

```python
import functools

import jax
import jax.numpy as jnp
from jax import lax
from jax.experimental import pallas as pl
from jax.experimental.pallas import tpu as pltpu

F32 = jnp.float32
BF16 = jnp.bfloat16
HIGHEST = lax.Precision.HIGHEST

GRID_W = 64
N_HEADS = 4
HEAD_DIM = 128
GROUP_WIDTH = 512
CHUNK = 64
SUB = 8
BIG = 2 * SUB
GDN_CONV = 5
NA_ROWS = 8
NA_COLS = 16
NA_UNROLL = 16
MLA_Q_RANK = 384
MLA_KV_RANK = 256
MLA_NOPE = 128
MLA_ROPE = 64
MLA_QK_DIM = MLA_NOPE + MLA_ROPE
MLA_QSUB = 512
ROPE_BASE = 10000.0
LN_EPS = 1e-5
RMS_EPS = 1e-6
LOG2E = 1.4426950408889634
NEG = -1e30

LANE = 128
CB_MQ, CB_MKRA, CB_MKV, CB_MKRB, CB_GAB = 0, 3, 4, 6, 7
CB_GQKV, CB_GGATE = 8, 20
CB_NAQ, CB_NAK, CB_NAV = 24, 28, 32
CB_HQ, CB_HFF, CB_HFB, CB_HI, CB_HG = 36, 40, 44, 48, 52
CHUNK_SHIFT = 6
PREP_CHUNKS = 4
NP_IN = 56 * LANE

VMEM_LIMIT = 48 << 20


def _cp(*sem):
    return pltpu.CompilerParams(dimension_semantics=sem, vmem_limit_bytes=VMEM_LIMIT)


def _silu(x):
    return x * jax.nn.sigmoid(x)


def _dot(a, b, **kw):
    return jnp.dot(a, b, preferred_element_type=F32, **kw)


def _dot_nt(a, b, **kw):
    return lax.dot_general(a, b, (((1,), (1,)), ((), ())), preferred_element_type=F32, **kw)


def _dot_tn(a, b, **kw):
    return lax.dot_general(a, b, (((0,), (0,)), ((), ())), preferred_element_type=F32, **kw)


def _ada_kernel(c_ref, w_ref, b_ref, o_ref):
    s = _silu(c_ref[...])
    o_ref[0] = _dot(s, w_ref[0], precision=HIGHEST) + b_ref[0]


def _ada(cin, w_ada, b_ada):
    depth, d, n = w_ada.shape
    tn = 1024
    return pl.pallas_call(
        _ada_kernel,
        grid=(depth, n // tn),
        in_specs=[pl.BlockSpec((8, d), lambda l, j: (0, 0)),
                  pl.BlockSpec((1, d, tn), lambda l, j: (l, 0, j)),
                  pl.BlockSpec((1, 1, tn), lambda l, j: (l, 0, j))],
        out_specs=pl.BlockSpec((1, 8, tn), lambda l, j: (l, 0, j)),
        out_shape=jax.ShapeDtypeStruct((depth, 8, n), F32),
        compiler_params=_cp("parallel", "parallel"),
        name="ada",
    )(cin, w_ada, b_ada.reshape(depth, 1, n))


ROW_STEP = 256


def _modulate(x_ref, sh_ref, sc_ref, xm_ref):
    sc1, sh = 1.0 + sc_ref[0], sh_ref[0]

    def body(t, carry):
        sl = pl.ds(pl.multiple_of(t * ROW_STEP, ROW_STEP), ROW_STEP)
        xm_ref[sl, :] = (x_ref[sl, :] * sc1 + sh).astype(BF16)
        return carry

    lax.fori_loop(0, x_ref.shape[0] // ROW_STEP, body, 0)


def _inproj_kernel(n_lat_tiles, *refs):
    xs, (sh_ref, sc_ref, w_ref, o_ref, xm_ref) = refs[:-5], refs[-5:]
    first = pl.program_id(1) == 0
    if len(xs) == 1:
        pl.when(first)(lambda: _modulate(xs[0], sh_ref, sc_ref, xm_ref))
    else:
        is_lat = pl.program_id(0) < n_lat_tiles
        pl.when(first & is_lat)(lambda: _modulate(xs[0], sh_ref, sc_ref, xm_ref))
        pl.when(first & jnp.logical_not(is_lat))(lambda: _modulate(xs[1], sh_ref, sc_ref, xm_ref))
    o_ref[...] = _dot(xm_ref[...], w_ref[...])


def _inproj(xs, ada_r, w, layer, m, n_lat, seq, nb, tm=512, tn=1024):
    _, d, n = w.shape
    nlt = n_lat // tm
    row = lambda i: jnp.minimum((i * tm) // seq, nb)
    if len(xs) == 1:
        x_specs = [pl.BlockSpec((tm, d), lambda i, j: (i, 0))]
    else:
        x_specs = [pl.BlockSpec((tm, d), lambda i, j: (jnp.minimum(i, nlt - 1), 0)),
                   pl.BlockSpec((tm, d), lambda i, j: (jnp.maximum(i - nlt, 0), 0),
                                pipeline_mode=pl.Buffered(1))]
    return pl.pallas_call(
        functools.partial(_inproj_kernel, nlt),
        grid=(m // tm, n // tn),
        in_specs=x_specs + [
                  pl.BlockSpec((1, 1, d), lambda i, j: (row(i) * 6 + 0, 0, 0)),
                  pl.BlockSpec((1, 1, d), lambda i, j: (row(i) * 6 + 1, 0, 0)),
                  pl.BlockSpec((None, d, tn), lambda i, j: (layer, 0, j))],
        out_specs=pl.BlockSpec((tm, tn), lambda i, j: (i, j)),
        out_shape=jax.ShapeDtypeStruct((m, n), F32),
        scratch_shapes=[pltpu.VMEM((tm, d), BF16)],
        compiler_params=_cp("parallel", "arbitrary"),
        name="inproj",
    )(*xs, ada_r, ada_r, w)


def _layernorm(r, w, b):
    mu = jnp.mean(r, axis=-1, keepdims=True)
    rc = r - mu
    var = jnp.mean(rc * rc, axis=-1, keepdims=True)
    return rc * lax.rsqrt(var + LN_EPS) * w + b


def _head_norm_gate(o, nw, gate):
    o = o * lax.rsqrt(jnp.mean(o * o, axis=-1, keepdims=True) + RMS_EPS) * nw
    return o * _silu(gate)


def _scan_mixer_out(of_ref, ob_ref, gate_ref, nw_ref):
    nw = nw_ref[...]
    heads = []
    for h in range(N_HEADS):
        hs = slice(h * LANE, (h + 1) * LANE)
        heads.append(_head_norm_gate(of_ref[:, hs] + ob_ref[:, hs], nw, gate_ref[:, hs]).astype(BF16))
    return jnp.concatenate(heads, axis=-1)


def _outproj_kernel(alpha, n_lat_tiles, n_split, *refs):
    pairs, rest = refs[:n_split], refs[n_split:]
    (gaf_ref, gab_ref, gag_ref, hgf_ref, hgb_ref, hgg_ref, nwa_ref, nwh_ref,
     w_ref, g_ref, lw_ref, lb_ref, o_ref) = rest
    gw = GROUP_WIDTH

    def run(x_ref, yb_ref, ym_ref):
        acc = _dot(_scan_mixer_out(gaf_ref, gab_ref, gag_ref, nwa_ref), w_ref[0:gw, :])
        acc += _dot(yb_ref[...], w_ref[gw:2 * gw, :])
        acc += _dot(ym_ref[...], w_ref[2 * gw:3 * gw, :])
        acc += _dot(_scan_mixer_out(hgf_ref, hgb_ref, hgg_ref, nwh_ref), w_ref[3 * gw:4 * gw, :])
        r = alpha * x_ref[...] + g_ref[0] * acc
        o_ref[...] = _layernorm(r, lw_ref[...], lb_ref[...])

    if n_split == 3:
        run(*pairs)
    else:
        is_lat = pl.program_id(0) < n_lat_tiles
        pl.when(is_lat)(lambda: run(*pairs[0::2]))
        pl.when(jnp.logical_not(is_lat))(lambda: run(*pairs[1::2]))


def _outproj(xs, gdn_o, ybs, yms, hgrn_o, z, nwa, nwh, w, layer, ada_r, lw, lb, m_out, n_lat, seq, nb, alpha,
             tm=256):
    d = w.shape[1]
    gw = GROUP_WIDTH
    nlt = n_lat // tm
    row = lambda i: jnp.minimum((i * tm) // seq, nb)
    in_specs, args = [], []
    dual = len(xs) == 2
    for arrs, width in ((xs, d), (ybs, gw), (yms, gw)):
        if dual:
            in_specs += [pl.BlockSpec((tm, width), lambda i: (jnp.minimum(i, nlt - 1), 0)),
                         pl.BlockSpec((tm, width), lambda i: (jnp.maximum(i - nlt, 0), 0))]
        else:
            in_specs += [pl.BlockSpec((tm, width), lambda i: (i, 0))]
        args += list(arrs)
    n_split = len(args)
    rowspec = pl.BlockSpec((tm, gw), lambda i: (i, 0))
    gate = lambda cb: pl.BlockSpec((tm, gw), lambda i: (i, cb * LANE // gw))
    vec = lambda n: pl.BlockSpec((1, n), lambda i: (0, 0))
    in_specs += [rowspec, rowspec, gate(CB_GGATE), rowspec, rowspec, gate(CB_HG), vec(LANE), vec(LANE),
                 pl.BlockSpec((None, d, d), lambda i: (layer, 0, 0), pipeline_mode=pl.Buffered(1)),
                 pl.BlockSpec((1, 1, d), lambda i: (row(i) * 6 + 2, 0, 0)), vec(d), vec(d)]
    args += [*gdn_o, z, *hgrn_o, z, nwa.reshape(1, LANE), nwh.reshape(1, LANE), w, ada_r,
             lw.reshape(1, d), lb.reshape(1, d)]
    return pl.pallas_call(
        functools.partial(_outproj_kernel, alpha, nlt, n_split),
        grid=(m_out // tm,),
        in_specs=in_specs,
        out_specs=pl.BlockSpec((tm, d), lambda i: (i, 0)),
        out_shape=jax.ShapeDtypeStruct((m_out, d), F32),
        compiler_params=_cp("parallel"),
        name="outproj_ln",
    )(*args)


def _mlp_kernel(alpha, x_ref, sh_ref, sc_ref, g_ref, w1_ref, w2_ref, lw_ref, lb_ref, o_ref, xm_ref, acc_ref):
    k = pl.program_id(1)

    @pl.when(k == 0)
    def _():
        _modulate(x_ref, sh_ref, sc_ref, xm_ref)
        acc_ref[...] = jnp.zeros_like(acc_ref)

    h = jnp.maximum(_dot(xm_ref[...], w1_ref[...]), 0.0)
    acc_ref[...] += _dot((h * h).astype(BF16), w2_ref[...])

    @pl.when(k == pl.num_programs(1) - 1)
    def _():
        g, lw, lb = g_ref[0], lw_ref[...], lb_ref[...]

        def body(t, carry):
            sl = pl.ds(pl.multiple_of(t * ROW_STEP, ROW_STEP), ROW_STEP)
            o_ref[sl, :] = _layernorm(alpha * x_ref[sl, :] + g * acc_ref[sl, :], lw, lb)
            return carry

        lax.fori_loop(0, x_ref.shape[0] // ROW_STEP, body, 0)


def _mlp(x_all, w1, w2, layer, ada_r, lw, lb, seq, nb, alpha, tm=512, th=1024):
    m, d = x_all.shape
    hid = w1.shape[2]
    row = lambda i: jnp.minimum((i * tm) // seq, nb)
    return pl.pallas_call(
        functools.partial(_mlp_kernel, alpha),
        grid=(m // tm, hid // th),
        in_specs=[pl.BlockSpec((tm, d), lambda i, k: (i, 0)),
                  pl.BlockSpec((1, 1, d), lambda i, k: (row(i) * 6 + 3, 0, 0)),
                  pl.BlockSpec((1, 1, d), lambda i, k: (row(i) * 6 + 4, 0, 0)),
                  pl.BlockSpec((1, 1, d), lambda i, k: (row(i) * 6 + 5, 0, 0)),
                  pl.BlockSpec((None, d, th), lambda i, k: (layer, 0, k)),
                  pl.BlockSpec((None, th, d), lambda i, k: (layer, k, 0)),
                  pl.BlockSpec((1, d), lambda i, k: (0, 0)),
                  pl.BlockSpec((1, d), lambda i, k: (0, 0))],
        out_specs=pl.BlockSpec((tm, d), lambda i, k: (i, 0)),
        out_shape=jax.ShapeDtypeStruct((m, d), F32),
        scratch_shapes=[pltpu.VMEM((tm, d), BF16), pltpu.VMEM((tm, d), F32)],
        compiler_params=_cp("parallel", "arbitrary"),
        name="mlp_ln",
    )(x_all, ada_r, ada_r, ada_r, w1, w2, lw.reshape(1, d), lb.reshape(1, d))


def _gdn_conv_kernel(seq, ctx_len, xl_ref, xc_ref, w_ref, o_ref, pad_ref):
    j = pl.program_id(1)
    w = w_ref[...]
    qscale = jnp.where(j < N_HEADS, HEAD_DIM ** -0.5, 1.0).astype(F32)
    p0 = 8 - GDN_CONV // 2
    for x_ref, nrows, o0 in ((xl_ref, seq, 0), (xc_ref, ctx_len, seq)):
        pad_ref[0:8, :] = jnp.zeros((8, LANE), F32)
        pad_ref[nrows + 8:nrows + 16, :] = jnp.zeros((8, LANE), F32)
        pad_ref[8:nrows + 8, :] = x_ref[...]
        rb = min(nrows, 256)
        for r0 in range(0, nrows, rb):
            y = pad_ref[r0 + p0:r0 + p0 + rb, :] * w[0:1, :]
            for i in range(1, GDN_CONV):
                y = y + pad_ref[r0 + p0 + i:r0 + p0 + i + rb, :] * w[i:i + 1, :]
            y = _silu(y)
            nrm = y * lax.rsqrt(jnp.sum(y * y, axis=-1, keepdims=True) + RMS_EPS) * qscale
            o_ref[o0 + r0:o0 + r0 + rb, :] = jnp.where(j < 2 * N_HEADS, nrm, y)


def _gdn_conv(z, conv_w, nb, seq, ctx_len):
    nblk = 3 * N_HEADS
    cblk0 = nb * seq // ctx_len
    return pl.pallas_call(
        functools.partial(_gdn_conv_kernel, seq, ctx_len),
        grid=(nb, nblk),
        in_specs=[pl.BlockSpec((seq, LANE), lambda b, j: (b, CB_GQKV + j)),
                  pl.BlockSpec((ctx_len, LANE), lambda b, j: (cblk0 + b, CB_GQKV + j)),
                  pl.BlockSpec((GDN_CONV, LANE), lambda b, j: (0, j))],
        out_specs=pl.BlockSpec((seq + ctx_len, LANE), lambda b, j: (b, j)),
        out_shape=jax.ShapeDtypeStruct((nb * (seq + ctx_len), nblk * LANE), F32),
        scratch_shapes=[pltpu.VMEM((seq + 16, LANE), F32)],
        compiler_params=_cp("parallel", "parallel"),
        name="gdn_conv",
    )(z, z, conv_w)


def _gdn_gates_kernel(tm, s_ref, alog_ref, dtb_ref, o_ref):
    s = s_ref[...]
    g = -jnp.exp(alog_ref[...]) * (jnp.maximum(s + dtb_ref[...], 0.0)
                                    + jnp.log1p(jnp.exp(-jnp.abs(s + dtb_ref[...]))))
    r = lax.broadcasted_iota(jnp.int32, (tm, tm), 0)
    c = lax.broadcasted_iota(jnp.int32, (tm, tm), 1)
    same = (r >> CHUNK_SHIFT) == (c >> CHUNK_SHIFT)
    lo = jnp.where(same & (c <= r), 1.0, 0.0).astype(F32)
    up = jnp.where(same & (c >= r), 1.0, 0.0).astype(F32)
    cum_f = _dot(lo, g, precision=HIGHEST)
    cum_b = _dot(up, g, precision=HIGHEST)
    col = lax.broadcasted_iota(jnp.int32, s.shape, 1)
    o_ref[...] = jnp.where(col < N_HEADS, cum_f,
                           jnp.where(col < 2 * N_HEADS, cum_b,
                                     jnp.where(col < 4 * N_HEADS, jax.nn.sigmoid(s), 0.0)))


def _gdn_gates(z, a_log, dt_bias, tm=512):
    m = z.shape[0]
    pad = lambda v: jnp.zeros((1, LANE), F32).at[0, :2 * N_HEADS].set(v.reshape(-1).astype(F32))
    return pl.pallas_call(
        functools.partial(_gdn_gates_kernel, tm),
        grid=(m // tm,),
        in_specs=[pl.BlockSpec((tm, LANE), lambda i: (i, CB_GAB)),
                  pl.BlockSpec((1, LANE), lambda i: (0, 0)),
                  pl.BlockSpec((1, LANE), lambda i: (0, 0))],
        out_specs=pl.BlockSpec((tm, LANE), lambda i: (i, 0)),
        out_shape=jax.ShapeDtypeStruct((m, LANE), F32),
        compiler_params=_cp("parallel"),
        name="gdn_gates",
    )(z, pad(a_log), pad(dt_bias))


def _tri_masks(n):
    r = lax.broadcasted_iota(jnp.int32, (n, n), 0)
    c = lax.broadcasted_iota(jnp.int32, (n, n), 1)
    return r, c


def _split(x):
    hi = x.astype(BF16)
    return hi, (x - hi.astype(F32)).astype(BF16)


def _dot3(a, b):
    return _dot(a[0], b[0]) + (_dot(a[0], b[1]) + _dot(a[1], b[0]))


def _gdn_prep_kernel(qkv_ref, p_ref, u_ref, w_ref, qt_ref, kt_ref, att_ref, el_ref):
    C, H = CHUNK, N_HEADS
    lane = lax.broadcasted_iota(jnp.int32, (C, LANE), 1)
    r, c = _tri_masks(C)
    eye = jnp.where(r == c, 1.0, 0.0).astype(F32)
    a_list, rhs_list, where_list = [], [], []
    for n in range(PREP_CHUNKS):
        rs = slice(n * C, (n + 1) * C)
        pblk = p_ref[rs, :]
        tblk = pblk.T
        col = lambda idx, pblk=pblk: jnp.sum(jnp.where(lane == idx, pblk, 0.0), axis=-1, keepdims=True)
        row = lambda idx, tblk=tblk: tblk[idx:idx + 1, :]
        for h in range(H):
            hs = slice(h * LANE, (h + 1) * LANE)
            q = qkv_ref[rs, h * LANE:(h + 1) * LANE]
            k = qkv_ref[rs, (H + h) * LANE:(H + h + 1) * LANE]
            v = qkv_ref[rs, (2 * H + h) * LANE:(2 * H + h + 1) * LANE]
            qbf, kbf = q.astype(BF16), k.astype(BF16)
            for d in range(2):
                idx = d * H + h
                cum_c, cum_r, beta_c = col(idx), row(idx), col(2 * H + idx)
                incl = (c >= r) if d else (c <= r)
                strict = (c > r) if d else (c < r)
                last = cum_r[:, 0:1] if d else cum_r[:, C - 1:C]
                decay = jnp.exp(jnp.where(incl, cum_c - cum_r, NEG))
                kb = k * beta_c
                ec = jnp.exp(cum_c)
                a_list.append(jnp.where(strict, _dot_nt(kb.astype(BF16), kbf) * decay, 0.0))
                rhs_list.append(jnp.concatenate([v * beta_c, kb * ec], axis=-1).astype(BF16))
                where_list.append((d, rs, hs))
                att_ref[d, h, rs, :] = jnp.where(incl, _dot_nt(qbf, kbf) * decay, 0.0).astype(BF16)
                qt_ref[d, rs, hs] = (q * ec).astype(BF16)
                kt_ref[d, rs, hs] = (k * jnp.exp(last - cum_c)).astype(BF16)
                el_ref[n, idx:idx + 1, :] = jnp.broadcast_to(jnp.exp(last), (1, LANE))
    ts = [eye - a for a in a_list]
    ps = [a.astype(BF16) for a in a_list]
    for _ in range(5):
        ps = [_dot(p, p).astype(BF16) for p in ps]
        ts = [t + _dot(t.astype(BF16), p) for t, p in zip(ts, ps)]
    res = [eye - t - _dot3(_split(a), _split(t)) for a, t in zip(a_list, ts)]
    ts = [t + _dot(t.astype(BF16), e.astype(BF16)) for t, e in zip(ts, res)]
    sols = [_dot(t.astype(BF16), rhs) for t, rhs in zip(ts, rhs_list)]
    for sol, (d, rs, hs) in zip(sols, where_list):
        u_ref[d, rs, hs] = sol[:, :HEAD_DIM]
        w_ref[d, rs, hs] = sol[:, HEAD_DIM:].astype(BF16)


def _gdn_prep(qkv, p, nb, seq, ctx_len):
    m = qkv.shape[0]
    H = N_HEADS
    gw = GROUP_WIDTH
    rows = PREP_CHUNKS * CHUNK
    assert seq % rows == 0 and ctx_len % rows == 0
    dspec = pl.BlockSpec((2, rows, gw), lambda i: (0, i, 0))
    nl, nc = seq // rows, ctx_len // rows

    def qkv_blk(i):
        ic = i - nb * nl
        return jnp.where(i < nb * nl, (i // nl) * (nl + nc) + i % nl, (ic // nc) * (nl + nc) + nl + ic % nc)

    return pl.pallas_call(
        _gdn_prep_kernel,
        grid=(m // rows,),
        in_specs=[pl.BlockSpec((rows, 3 * gw), lambda i: (qkv_blk(i), 0)),
                  pl.BlockSpec((rows, LANE), lambda i: (i, 0))],
        out_specs=[dspec, dspec, dspec, dspec,
                   pl.BlockSpec((2, H, rows, CHUNK), lambda i: (0, 0, i, 0)),
                   pl.BlockSpec((PREP_CHUNKS, 2 * H, LANE), lambda i: (i, 0, 0))],
        out_shape=[jax.ShapeDtypeStruct((2, m, gw), F32),
                   jax.ShapeDtypeStruct((2, m, gw), BF16),
                   jax.ShapeDtypeStruct((2, m, gw), BF16),
                   jax.ShapeDtypeStruct((2, m, gw), BF16),
                   jax.ShapeDtypeStruct((2, H, m, CHUNK), BF16),
                   jax.ShapeDtypeStruct((m // CHUNK, 2 * H, LANE), F32)],
        compiler_params=_cp("parallel"),
        name="gdn_prep",
    )(qkv, p)


def _gdn_scan_a(ins, s_ref, g, j):
    H = N_HEADS
    mid = []
    for d in range(2):
        u_ref, w_ref, qt_ref = ins[d:6:2]
        cj = g - 1 - j if d else j
        rs = pl.ds(pl.multiple_of(cj * CHUNK, CHUNK), CHUNK)
        for h in range(H):
            hs = slice(h * LANE, (h + 1) * LANE)
            sb = s_ref[d * H + h].astype(BF16)
            vnb = (u_ref[0, rs, hs] - _dot(w_ref[0, rs, hs], sb)).astype(BF16)
            mid.append((vnb, _dot(qt_ref[0, rs, hs], sb)))
    return mid


def _gdn_scan_b(ins, s_ref, o_refs, g, j, mid):
    H = N_HEADS
    for d in range(2):
        kt_ref, att_ref, el_ref = ins[6 + d::2]
        cj = g - 1 - j if d else j
        rs = pl.ds(pl.multiple_of(cj * CHUNK, CHUNK), CHUNK)
        for h in range(H):
            hs = slice(h * LANE, (h + 1) * LANE)
            idx = d * H + h
            vnb, o_state = mid[idx]
            s_ref[idx] = s_ref[idx] * el_ref[cj, idx:idx + 1, :] + _dot_tn(kt_ref[0, rs, hs], vnb)
            o_refs[d][rs, hs] = o_state + _dot(att_ref[0, h, rs, :], vnb)


def _scans_kernel(g, *refs):
    gdn_ins = refs[:12]
    qf_ref, ff_ref, if_ref, qb_ref, fb_ref, ib_ref, lb_ref = refs[12:19]
    gof_ref, gob_ref, of_ref, ob_ref, gs_ref, s_ref = refs[19:]
    C, H, nsub = CHUNK, N_HEADS, CHUNK // SUB

    @pl.when(pl.program_id(1) == 0)
    def _():
        s_ref[...] = jnp.zeros_like(s_ref)
        gs_ref[...] = jnp.zeros_like(gs_ref)

    r, c = _tri_masks(C)
    tri = (jnp.where(c <= r, 1.0, 0.0).astype(F32), jnp.where(c >= r, 1.0, 0.0).astype(F32))
    nbig = C // BIG
    same_big = (r >> (BIG.bit_length() - 1)) == (c >> (BIG.bit_length() - 1))
    trow = lax.broadcasted_iota(jnp.int32, (C, HEAD_DIM), 0)
    srow = lax.broadcasted_iota(jnp.int32, (SUB, C), 0)
    scol = lax.broadcasted_iota(jnp.int32, (SUB, C), 1)
    chains = [(d, h) for d in range(2) for h in range(H)]
    srcs = ((qf_ref, ff_ref, if_ref, of_ref), (qb_ref, fb_ref, ib_ref, ob_ref))

    def body(j, carry):
        ph1 = []
        for d, h in chains:
            cj = g - 1 - j if d else j
            rs, hs = pl.ds(pl.multiple_of(cj * C, C), C), slice(h * LANE, (h + 1) * LANE)
            lb = lb_ref[:, hs]
            f = lb + (1.0 - lb) * jax.nn.sigmoid(srcs[d][1][rs, hs])
            ph1.append((rs, hs, f, _dot(tri[d], jnp.log(f), precision=HIGHEST)))
        gdn_mid = _gdn_scan_a(gdn_ins, gs_ref, g, j)
        ph2 = []
        for (d, h), (rs, hs, f, cum) in zip(chains, ph1):
            q_ref, _, i_ref, _ = srcs[d]
            q = _silu(q_ref[rs, hs]) * HEAD_DIM ** -0.5
            k = 1.0 - f
            vb = i_ref[rs, hs].astype(BF16)
            last = cum[0:1, :] if d else cum[C - 1:C, :]
            idx = d * H + h
            St = s_ref[idx]
            o_state = _dot_nt((q * jnp.exp(cum)).astype(BF16), St.astype(BF16))
            s_ref[idx] = St * jnp.exp(last) + _dot_tn(vb, (k * jnp.exp(last - cum)).astype(BF16))
            lvl1 = []
            for a2 in range(nbig):
                sa = slice(a2 * BIG, (a2 + 1) * BIG)
                if d and a2 < nbig - 1:
                    cb = cum[(a2 + 1) * BIG:(a2 + 1) * BIG + 1, :]
                    kt = k * jnp.exp(jnp.where(trow >= (a2 + 1) * BIG, cb - cum, NEG))
                elif (not d) and a2 > 0:
                    cb = cum[a2 * BIG - 1:a2 * BIG, :]
                    kt = k * jnp.exp(jnp.where(trow < a2 * BIG, cb - cum, NEG))
                else:
                    lvl1.append(None)
                    continue
                lvl1.append(_dot_nt((q[sa] * jnp.exp(cum[sa] - cb)).astype(BF16), kt.astype(BF16)))
            mid_row = SUB if d else SUB - 1
            cb2 = jnp.concatenate([jnp.broadcast_to(cum[a2 * BIG + mid_row:a2 * BIG + mid_row + 1, :], (BIG, HEAD_DIM))
                                   for a2 in range(nbig)], axis=0)
            early = ((trow & (BIG - 1)) >= SUB) if d else ((trow & (BIG - 1)) < SUB)
            qt2 = q * jnp.exp(jnp.where(early, NEG, cum - cb2))
            kt2 = k * jnp.exp(jnp.where(early, cb2 - cum, NEG))
            sc2 = jnp.where(same_big, _dot_nt(qt2.astype(BF16), kt2.astype(BF16)), 0.0)
            inter = []
            for a in range(nsub):
                blk = sc2[a * SUB:(a + 1) * SUB]
                if lvl1[a // 2] is not None:
                    blk = blk + lvl1[a // 2][(a % 2) * SUB:(a % 2 + 1) * SUB]
                inter.append(blk)
            ph2.append((q, k, cum, vb, o_state, inter))
        _gdn_scan_b(gdn_ins, gs_ref, (gof_ref, gob_ref), g, j, gdn_mid)
        for (d, h), (rs, hs, _, _), (q, k, cum, vb, o_state, inter) in zip(chains, ph1, ph2):
            blocks = []
            for a in range(nsub):
                sa = slice(a * SUB, (a + 1) * SUB)
                qa, ka, ca, sc = q[sa], k[sa], cum[sa], inter[a]
                for t in range(SUB):
                    dec = jnp.exp(ca - ca[t:t + 1, :])
                    st = jnp.sum(qa * ka[t:t + 1, :] * dec, axis=-1, keepdims=True)
                    ok = (srow <= t) if d else (srow >= t)
                    sc = jnp.where((scol == a * SUB + t) & ok, st, sc)
                blocks.append(sc)
            scores = jnp.concatenate(blocks, axis=0)
            srcs[d][3][rs, hs] = o_state + _dot(scores.astype(BF16), vb)
        return carry

    lax.fori_loop(0, g, body, 0)


def _scans(u, w, qt, kt, att, el, z, lbs, nb, seq, ctx_len):
    m = z.shape[0]
    H = N_HEADS
    gw = GROUP_WIDTH
    g = ctx_len // CHUNK
    nblk = seq // ctx_len
    cblk0 = nb * nblk
    blk_f = lambda b, t: jnp.where(t == 0, cblk0 + b, b * nblk + t - 1)
    blk_b = lambda b, t: jnp.where(t == 0, cblk0 + b, b * nblk + nblk - t)
    in_specs, args = [], []
    for arr in (u, w, qt, kt):
        in_specs += [pl.BlockSpec((1, ctx_len, gw), lambda b, t: (0, blk_f(b, t), 0)),
                     pl.BlockSpec((1, ctx_len, gw), lambda b, t: (1, blk_b(b, t), 0))]
        args += [arr, arr]
    in_specs += [pl.BlockSpec((1, H, ctx_len, CHUNK), lambda b, t: (0, 0, blk_f(b, t), 0)),
                 pl.BlockSpec((1, H, ctx_len, CHUNK), lambda b, t: (1, 0, blk_b(b, t), 0)),
                 pl.BlockSpec((g, 2 * H, LANE), lambda b, t: (blk_f(b, t), 0, 0)),
                 pl.BlockSpec((g, 2 * H, LANE), lambda b, t: (blk_b(b, t), 0, 0))]
    args += [att, att, el, el]
    col = lambda cb: cb * LANE // gw
    zspec = lambda blk, cb: pl.BlockSpec((ctx_len, gw), lambda b, t: (blk(b, t), col(cb)))
    in_specs += [zspec(blk_f, CB_HQ), zspec(blk_f, CB_HFF), zspec(blk_f, CB_HI),
                 zspec(blk_b, CB_HQ), zspec(blk_b, CB_HFB), zspec(blk_b, CB_HI),
                 pl.BlockSpec((1, gw), lambda b, t: (0, 0))]
    args += [z, z, z, z, z, z, lbs.reshape(1, gw)]
    ospec_f = pl.BlockSpec((ctx_len, gw), lambda b, t: (blk_f(b, t), 0))
    ospec_b = pl.BlockSpec((ctx_len, gw), lambda b, t: (blk_b(b, t), 0))
    state = pltpu.VMEM((2 * H, HEAD_DIM, HEAD_DIM), F32)
    outs = pl.pallas_call(
        functools.partial(_scans_kernel, g),
        grid=(nb, nblk + 1),
        in_specs=in_specs,
        out_specs=[ospec_f, ospec_b, ospec_f, ospec_b],
        out_shape=[jax.ShapeDtypeStruct((m, gw), F32)] * 4,
        scratch_shapes=[state, state],
        compiler_params=_cp("parallel", "arbitrary"),
        name="scans",
    )(*args)
    return outs[:2], outs[2:]


def _softmax_pv(parts):
    m = parts[0][0].max(axis=-1, keepdims=True)
    for s, _ in parts[1:]:
        m = jnp.maximum(m, s.max(axis=-1, keepdims=True))
    den, acc = None, None
    for s, v in parts:
        p = jnp.exp(s - m)
        d = jnp.sum(p, axis=-1, keepdims=True)
        a = _dot(p.astype(BF16), v)
        den = d if den is None else den + d
        acc = a if acc is None else acc + a
    return acc / den


def _na_kernel(emit_ctx, rows, *refs):
    q_ref, k_ref, v_ref, qc_ref, kc_ref, vc_ref, bias_ref = refs[:7]
    if emit_ctx:
        yl_ref, yc_ref, kb_ref, vb_ref = refs[7:]
    else:
        yl_ref, kb_ref, vb_ref = refs[7:]
    scale = HEAD_DIM ** -0.5
    win = NA_ROWS * GRID_W
    kb_ref[...] = k_ref[...].astype(BF16)
    vb_ref[...] = v_ref[...].astype(BF16)
    kc = kc_ref[...].astype(BF16)
    vc = vc_ref[...].astype(BF16)

    def body(i, carry):
        pre = []
        for t in range(NA_UNROLL):
            r = i * NA_UNROLL + t
            row0 = jnp.clip(r - NA_ROWS // 2, 0, rows - NA_ROWS)
            qs = pl.ds(pl.multiple_of(r * GRID_W, GRID_W), GRID_W)
            ks = pl.ds(pl.multiple_of(row0 * GRID_W, GRID_W), win)
            q = q_ref[qs, :].astype(BF16)
            pre.append((qs, ks, _dot_nt(q, kb_ref[ks, :]) * scale + bias_ref[0, r - row0], _dot_nt(q, kc) * scale))
        mid = []
        for qs, ks, s_win, s_ctx in pre:
            m = jnp.maximum(s_win.max(axis=-1, keepdims=True), s_ctx.max(axis=-1, keepdims=True))
            p_win, p_ctx = jnp.exp(s_win - m), jnp.exp(s_ctx - m)
            den = jnp.sum(p_win, axis=-1, keepdims=True) + jnp.sum(p_ctx, axis=-1, keepdims=True)
            mid.append((qs, ks, p_win.astype(BF16), p_ctx.astype(BF16), den))
        for qs, ks, p_win, p_ctx, den in mid:
            yl_ref[qs, :] = ((_dot(p_win, vb_ref[ks, :]) + _dot(p_ctx, vc)) / den).astype(yl_ref.dtype)
        return carry

    lax.fori_loop(0, rows // NA_UNROLL, body, 0)
    if emit_ctx:
        s = _dot_nt(qc_ref[...].astype(BF16), kc) * scale
        yc_ref[...] = _softmax_pv([(s, vc)]).astype(yc_ref.dtype)


def _na_bias_kernel(rpb_ref, o_ref):
    n = lax.broadcasted_iota(jnp.int32, (LANE, GRID_W * GRID_W), 1)
    j = lax.broadcasted_iota(jnp.int32, (LANE, GRID_W * GRID_W), 0)
    q, w = n >> 6, n & (GRID_W - 1)
    dc = jnp.clip(w - q, 1 - NA_COLS, NA_COLS - 1) + NA_COLS - 1
    onehot = jnp.where(dc == j, 1.0, 0.0).astype(F32)
    m = _dot(rpb_ref[...], onehot, precision=HIGHEST)
    c0 = jnp.clip(q[0:1] - NA_COLS // 2, 0, GRID_W - NA_COLS)
    ok = (w[0:1] >= c0) & (w[0:1] < c0 + NA_COLS)
    o_ref[...] = jnp.where(ok, m, NEG)


def _na_bias_tables(rpb):
    depth, H, nr, nc = rpb.shape
    assert GRID_W == 64 and depth * H * nr <= LANE and nc <= LANE
    flat = jnp.zeros((LANE, LANE), F32).at[:depth * H * nr, :nc].set(rpb.reshape(-1, nc).astype(F32))
    m = pl.pallas_call(
        _na_bias_kernel,
        out_shape=jax.ShapeDtypeStruct((LANE, GRID_W * GRID_W), F32),
        compiler_params=pltpu.CompilerParams(vmem_limit_bytes=VMEM_LIMIT),
        name="na_bias",
    )(flat)
    m = m[:depth * H * nr].reshape(depth, H, nr, GRID_W, GRID_W)
    tab = jnp.stack([jnp.stack([m[:, :, k - s + NA_ROWS - 1] for k in range(NA_ROWS)], axis=3)
                     for s in range(NA_ROWS)], axis=2)
    return tab.reshape(depth, H, NA_ROWS, GRID_W, NA_ROWS * GRID_W)


def _na(z, bias, nb, seq, ctx_len, emit_ctx):
    rows = seq // GRID_W
    assert seq % GRID_W == 0 and rows >= NA_ROWS and rows % NA_UNROLL == 0
    cblk0 = nb * seq // ctx_len
    H = N_HEADS
    lat = lambda cb: pl.BlockSpec((seq, LANE), lambda b, h: (b, cb + h))
    ctx = lambda cb: pl.BlockSpec((ctx_len, LANE), lambda b, h: (cblk0 + b, cb + h))
    win = NA_ROWS * GRID_W
    in_specs = [lat(CB_NAQ), lat(CB_NAK), lat(CB_NAV), ctx(CB_NAQ), ctx(CB_NAK), ctx(CB_NAV),
                pl.BlockSpec((1, NA_ROWS, GRID_W, win), lambda b, h: (h, 0, 0, 0))]
    out_specs = [pl.BlockSpec((seq, LANE), lambda b, h: (b, h))]
    out_shape = [jax.ShapeDtypeStruct((nb * seq, GROUP_WIDTH), BF16)]
    if emit_ctx:
        out_specs.append(pl.BlockSpec((ctx_len, LANE), lambda b, h: (b, h)))
        out_shape.append(jax.ShapeDtypeStruct((nb * ctx_len, GROUP_WIDTH), BF16))
    return pl.pallas_call(
        functools.partial(_na_kernel, emit_ctx, rows),
        grid=(nb, H),
        in_specs=in_specs, out_specs=out_specs, out_shape=out_shape,
        scratch_shapes=[pltpu.VMEM((seq, LANE), BF16), pltpu.VMEM((seq, LANE), BF16)],
        compiler_params=_cp("parallel", "parallel"),
        name="na_attn",
    )(z, z, z, z, z, z, bias)


def _rms(x, w):
    return x * lax.rsqrt(jnp.mean(x * x, axis=-1, keepdims=True) + RMS_EPS) * w


def _mla_prep_kernel(cq_ref, ckv_ref, kra_ref, krb_ref, cc_ref, ss_ref, qnw_ref, kvnw_ref, wuq_ref, wukv_ref,
                     q_ref, kn_ref, kr_ref, v_ref):
    cc, ss = cc_ref[...], ss_ref[...]
    qn = _rms(cq_ref[...], qnw_ref[...]).astype(BF16)
    qa = _dot(qn, wuq_ref[...])
    for h in range(N_HEADS):
        b = 3 * LANE * h
        q_ref[:, 2 * LANE * h:2 * LANE * h + LANE] = qa[:, b:b + LANE].astype(BF16)
        q_ref[:, 2 * LANE * h + LANE:2 * LANE * (h + 1)] = (
            qa[:, b + LANE:b + 2 * LANE] * cc + qa[:, b + 2 * LANE:b + 3 * LANE] * ss).astype(BF16)
    kvn = _rms(ckv_ref[...], kvnw_ref[...]).astype(BF16)
    kv = _dot(kvn, wukv_ref[...])
    kn_ref[...] = kv[:, :GROUP_WIDTH].astype(BF16)
    v_ref[...] = kv[:, GROUP_WIDTH:].astype(BF16)
    kr_ref[...] = (kra_ref[...] * cc + krb_ref[...] * ss).astype(BF16)


def _mla_prep(z, cc, ss, qnw, kvnw, wuq, wukv, n_lat, seq, tm=512):
    m = z.shape[0]
    nlt, spt = n_lat // tm, seq // tm
    tab = lambda i: jnp.where(i < nlt, i % spt, spt)
    H = N_HEADS
    return pl.pallas_call(
        _mla_prep_kernel,
        grid=(m // tm,),
        in_specs=[pl.BlockSpec((tm, MLA_Q_RANK), lambda i: (i, CB_MQ * LANE // MLA_Q_RANK)),
                  pl.BlockSpec((tm, MLA_KV_RANK), lambda i: (i, CB_MKV * LANE // MLA_KV_RANK)),
                  pl.BlockSpec((tm, LANE), lambda i: (i, CB_MKRA)),
                  pl.BlockSpec((tm, LANE), lambda i: (i, CB_MKRB)),
                  pl.BlockSpec((tm, LANE), lambda i: (tab(i), 0)),
                  pl.BlockSpec((tm, LANE), lambda i: (tab(i), 0)),
                  pl.BlockSpec((1, MLA_Q_RANK), lambda i: (0, 0)),
                  pl.BlockSpec((1, MLA_KV_RANK), lambda i: (0, 0)),
                  pl.BlockSpec(wuq.shape, lambda i: (0, 0)),
                  pl.BlockSpec(wukv.shape, lambda i: (0, 0))],
        out_specs=[pl.BlockSpec((tm, 2 * LANE * H), lambda i: (i, 0)),
                   pl.BlockSpec((tm, GROUP_WIDTH), lambda i: (i, 0)),
                   pl.BlockSpec((tm, LANE), lambda i: (i, 0)),
                   pl.BlockSpec((tm, GROUP_WIDTH), lambda i: (i, 0))],
        out_shape=[jax.ShapeDtypeStruct((m, 2 * LANE * H), BF16),
                   jax.ShapeDtypeStruct((m, GROUP_WIDTH), BF16),
                   jax.ShapeDtypeStruct((m, LANE), BF16),
                   jax.ShapeDtypeStruct((m, GROUP_WIDTH), BF16)],
        compiler_params=_cp("parallel"),
        name="mla_prep",
    )(z, z, z, z, cc, ss, qnw.reshape(1, -1), kvnw.reshape(1, -1), wuq, wukv)


def _mla_attn_kernel(with_lat, seq, *refs):
    if with_lat:
        q_ref, knl_ref, krl_ref, vl_ref, knc_ref, krc_ref, vc_ref, y_ref, k_scr = refs
    else:
        q_ref, knc_ref, krc_ref, vc_ref, y_ref, k_scr = refs
    scale = MLA_QK_DIM ** -0.5
    nk = k_scr.shape[0]

    @pl.when(pl.program_id(2) == 0)
    def _():
        if with_lat:
            k_scr[0:seq, 0:LANE] = knl_ref[...]
            k_scr[0:seq, LANE:2 * LANE] = krl_ref[...]
        k_scr[nk - knc_ref.shape[0]:nk, 0:LANE] = knc_ref[...]
        k_scr[nk - knc_ref.shape[0]:nk, LANE:2 * LANE] = krc_ref[...]

    tq = q_ref.shape[0]
    qsub = min(tq, MLA_QSUB)
    nsub = tq // qsub
    vals = ([vl_ref] if with_lat else []) + [vc_ref]

    def qk(s):
        q = q_ref[s * qsub:(s + 1) * qsub, :]
        out = [_dot_nt(q, k_scr[0:seq, :])] if with_lat else []
        return out + [_dot_nt(q, k_scr[nk - knc_ref.shape[0]:nk, :])]

    def softmax(raw):
        m = raw[0].max(axis=-1, keepdims=True)
        for s in raw[1:]:
            m = jnp.maximum(m, s.max(axis=-1, keepdims=True))
        ps = [jnp.exp2((s - m) * (scale * LOG2E)) for s in raw]
        den = sum(jnp.sum(p, axis=-1, keepdims=True) for p in ps)
        return [p.astype(BF16) for p in ps], den

    raw = qk(0)
    for s in range(nsub):
        nxt = qk(s + 1) if s + 1 < nsub else None
        ps, den = softmax(raw)
        acc = sum(_dot(p, v[...]) for p, v in zip(ps, vals))
        y_ref[s * qsub:(s + 1) * qsub, :] = (acc / den).astype(y_ref.dtype)
        raw = nxt


def _mla_attn(q, kn, kr, v, nb, seq, ctx_len, with_lat, tq=4096):
    H = N_HEADS
    cblk0 = nb * seq // ctx_len
    nq = seq if with_lat else ctx_len
    tq = min(tq, nq)
    qblk0 = 0 if with_lat else nb * seq // tq
    ctxs = [pl.BlockSpec((ctx_len, LANE), lambda b, h, i: (cblk0 + b, h)),
            pl.BlockSpec((ctx_len, LANE), lambda b, h, i: (cblk0 + b, 0)),
            pl.BlockSpec((ctx_len, LANE), lambda b, h, i: (cblk0 + b, h))]
    lats = [pl.BlockSpec((seq, LANE), lambda b, h, i: (b, h)),
            pl.BlockSpec((seq, LANE), lambda b, h, i: (b, 0)),
            pl.BlockSpec((seq, LANE), lambda b, h, i: (b, h))]
    in_specs = [pl.BlockSpec((tq, 2 * LANE), lambda b, h, i: (qblk0 + b * (nq // tq) + i, h))]
    args = [q]
    if with_lat:
        in_specs += lats
        args += [kn, kr, v]
    in_specs += ctxs
    args += [kn, kr, v]
    nk = (seq if with_lat else 0) + ctx_len
    return pl.pallas_call(
        functools.partial(_mla_attn_kernel, with_lat, seq),
        grid=(nb, H, nq // tq),
        in_specs=in_specs,
        out_specs=pl.BlockSpec((tq, LANE), lambda b, h, i: (b * (nq // tq) + i, h)),
        out_shape=jax.ShapeDtypeStruct((nb * nq, GROUP_WIDTH), BF16),
        scratch_shapes=[pltpu.VMEM((nk, 2 * LANE), BF16)],
        compiler_params=_cp("parallel", "parallel", "arbitrary"),
        name="mla_attn_lat" if with_lat else "mla_attn_ctx",
    )(*args)


def _prep_w_in(w_in):
    gw = GROUP_WIDTH
    o_na = 4 * gw + 4 * N_HEADS
    o_mla = o_na + 3 * gw
    o_kv = o_mla + MLA_Q_RANK
    o_kr = o_kv + MLA_KV_RANK
    o_hg = o_kr + MLA_ROPE
    depth, d, n_in = w_in.shape
    kr = w_in[..., o_kr:o_hg].astype(BF16)
    k1, k2 = kr[..., 0::2], kr[..., 1::2]
    zpad = jnp.zeros((depth, d, LANE - MLA_ROPE), BF16)
    kra, krb = jnp.concatenate([k1, k2, zpad], axis=-1), jnp.concatenate([k2, k1, zpad], axis=-1)
    moves = ((CB_MQ, o_mla, MLA_Q_RANK), (CB_MKV, o_kv, MLA_KV_RANK), (CB_GAB, 4 * gw, 4 * N_HEADS),
             (CB_GQKV, 0, 4 * gw), (CB_NAQ, o_na, 3 * gw), (CB_HQ, o_hg, 5 * gw))

    def body(w_ref, kra_ref, krb_ref, o_ref):
        for cb, src, width in moves:
            lo = src // LANE * LANE
            hi = min(-(-(src + width) // LANE) * LANE, n_in)
            piece = w_ref[:, lo:hi][:, src - lo:src - lo + width].astype(BF16)
            pad = -width % LANE
            if pad:
                piece = jnp.concatenate([piece, jnp.zeros((piece.shape[0], pad), BF16)], axis=-1)
            o_ref[:, cb * LANE:cb * LANE + width + pad] = piece
        o_ref[:, CB_MKRA * LANE:(CB_MKRA + 1) * LANE] = kra_ref[...]
        o_ref[:, CB_MKRB * LANE:(CB_MKRB + 1) * LANE] = krb_ref[...]

    tr = 256
    return pl.pallas_call(
        body,
        grid=(depth, d // tr),
        in_specs=[pl.BlockSpec((None, tr, n_in), lambda l, i: (l, i, 0)),
                  pl.BlockSpec((None, tr, LANE), lambda l, i: (l, i, 0)),
                  pl.BlockSpec((None, tr, LANE), lambda l, i: (l, i, 0))],
        out_specs=pl.BlockSpec((None, tr, NP_IN), lambda l, i: (l, i, 0)),
        out_shape=jax.ShapeDtypeStruct((depth, d, NP_IN), BF16),
        compiler_params=_cp("parallel", "parallel"),
        name="w_in_layout",
    )(w_in, kra, krb)


def _prep_w_uq(w_uq):
    r = w_uq.shape[0]
    z = jnp.zeros((r, LANE - MLA_ROPE), w_uq.dtype)
    cols = []
    for h in range(N_HEADS):
        wh = w_uq[:, h * MLA_QK_DIM:(h + 1) * MLA_QK_DIM]
        rope = wh[:, MLA_NOPE:]
        r1, r2 = rope[:, 0::2], rope[:, 1::2]
        cols += [wh[:, :MLA_NOPE], r1, r2, z, r2, r1, z]
    return jnp.concatenate(cols, axis=1).astype(BF16)


def _rope_tables(seq, tm):
    n_freq = MLA_ROPE // 4
    freqs = ROPE_BASE ** (-jnp.arange(n_freq, dtype=F32) / n_freq)
    t = jnp.arange(seq)
    ang = jnp.concatenate([(t // GRID_W).astype(F32)[:, None] * freqs,
                           (t % GRID_W).astype(F32)[:, None] * freqs], -1)
    cos, sin = jnp.cos(ang), jnp.sin(ang)
    zp = jnp.zeros((seq, LANE - MLA_ROPE), F32)
    cc = jnp.concatenate([cos, cos, zp], axis=1)
    ss = jnp.concatenate([-sin, sin, zp], axis=1)
    ident = jnp.zeros((tm, LANE), F32).at[:, :MLA_ROPE].set(1.0)
    return jnp.concatenate([cc, ident], axis=0), jnp.concatenate([ss, jnp.zeros((tm, LANE), F32)], axis=0)


def kernel(x, c, ctx, c_ctx, w_ada, b_ada, w_in, gdn_conv_w, gdn_a_log, gdn_dt_bias, gdn_norm_w, na_rpb,
           mla_q_norm_w, mla_kv_norm_w, mla_w_uq, mla_w_uk, mla_w_uv, hgrn_lower_bounds, hgrn_norm_w, w_out,
           ln1_w, ln1_b, w_mlp1, w_mlp2, ln2_w, ln2_b):
    nb, seq, d = x.shape
    ctx_len = ctx.shape[1]
    depth = w_ada.shape[0]
    n_lat, n_ctx = nb * seq, nb * ctx_len
    alpha = (2 * depth) ** 0.25
    tm = 512
    tmm = 1024 if (seq % 1024 == 0 and n_ctx % 1024 == 0) else tm
    assert nb < 8 and seq % tm == 0 and n_ctx % tm == 0 and seq % ctx_len == 0 and ctx_len % CHUNK == 0

    cin = jnp.zeros((8, d), F32).at[:nb].set(c).at[nb].set(c_ctx)
    ada = _ada(cin, w_ada, b_ada)
    p_lb = jax.nn.softmax(hgrn_lower_bounds.astype(F32), axis=0)
    lbs = jnp.cumsum(p_lb, axis=0) - p_lb[0]
    cc, ss = _rope_tables(seq, tm)
    na_bias = _na_bias_tables(na_rpb)
    w_in_b, w_out_b = _prep_w_in(w_in), w_out.astype(BF16)
    w_mlp1_b, w_mlp2_b = w_mlp1.astype(BF16), w_mlp2.astype(BF16)

    xs = (x.reshape(n_lat, d), ctx.reshape(n_ctx, d))
    for l in range(depth):
        emit_ctx = l < depth - 1
        ada_r = ada[l].reshape(8 * 6, 1, d)
        z = _inproj(xs, ada_r, w_in_b, l, n_lat + n_ctx, n_lat, seq, nb, tm=tmm)

        qkv = _gdn_conv(z, gdn_conv_w[l], nb, seq, ctx_len)
        p = _gdn_gates(z, gdn_a_log[l], gdn_dt_bias[l], tm=tm // 2)
        gdn_o, hgrn_o = _scans(*_gdn_prep(qkv, p, nb, seq, ctx_len), z, lbs[l], nb, seq, ctx_len)
        m_out = n_lat + n_ctx if emit_ctx else n_lat
        yb = _na(z, na_bias[l], nb, seq, ctx_len, emit_ctx)
        q, kn, kr, v = _mla_prep(z, cc, ss, mla_q_norm_w[l], mla_kv_norm_w[l], _prep_w_uq(mla_w_uq[l]),
                                 jnp.concatenate([mla_w_uk[l], mla_w_uv[l]], axis=1).astype(BF16),
                                 n_lat, seq, tm=tm)
        ym = [_mla_attn(q, kn, kr, v, nb, seq, ctx_len, True)]
        if emit_ctx:
            ym.append(_mla_attn(q, kn, kr, v, nb, seq, ctx_len, False))
        if len(xs) == 1 and emit_ctx:
            yb, ym = [jnp.concatenate(yb, axis=0)], [jnp.concatenate(ym, axis=0)]
        x_all = _outproj(xs if emit_ctx else xs[:1], gdn_o, yb, ym, hgrn_o, z, gdn_norm_w[l], hgrn_norm_w[l],
                         w_out_b, l, ada_r, ln1_w[l], ln1_b[l], m_out, n_lat, seq, nb, alpha,
                         tm=tm // 2 if len(xs) == 2 else tm)
        x_all = _mlp(x_all, w_mlp1_b, w_mlp2_b, l, ada_r, ln2_w[l], ln2_b[l], seq, nb, alpha, tm=tm)
        xs = (x_all,)
    return x_all[:n_lat].reshape(nb, seq, d)
```

```python
import functools

import jax
import jax.numpy as jnp
from jax import lax
from jax.experimental import pallas as pl
from jax.experimental.pallas import tpu as pltpu

F32 = jnp.float32
BF16 = jnp.bfloat16
HIGHEST = lax.Precision.HIGHEST

GRID_W = 64
N_HEADS = 4
HEAD_DIM = 128
GROUP_WIDTH = 512
CHUNK = 64
SUB = 8
BIG = 2 * SUB
GDN_CONV = 5
NA_ROWS = 8
NA_COLS = 16
NA_UNROLL = 16
MLA_Q_RANK = 384
MLA_KV_RANK = 256
MLA_NOPE = 128
MLA_ROPE = 64
MLA_QK_DIM = MLA_NOPE + MLA_ROPE
MLA_QSUB = 512
ROPE_BASE = 10000.0
LN_EPS = 1e-5
RMS_EPS = 1e-6
LOG2E = 1.4426950408889634
NEG = -1e30

LANE = 128
CB_MQ, CB_MKRA, CB_MKV, CB_MKRB, CB_GAB = 0, 3, 4, 6, 7
CB_GQKV, CB_GGATE = 8, 20
CB_NAQ, CB_NAK, CB_NAV = 24, 28, 32
CB_HQ, CB_HFF, CB_HFB, CB_HI, CB_HG = 36, 40, 44, 48, 52
CHUNK_SHIFT = 6
PREP_CHUNKS = 4
NP_IN = 56 * LANE

VMEM_LIMIT = 48 << 20


def _cp(*sem):
    return pltpu.CompilerParams(dimension_semantics=sem, vmem_limit_bytes=VMEM_LIMIT)


def _silu(x):
    return x * jax.nn.sigmoid(x)


def _dot(a, b, **kw):
    return jnp.dot(a, b, preferred_element_type=F32, **kw)


def _dot_nt(a, b, **kw):
    return lax.dot_general(a, b, (((1,), (1,)), ((), ())), preferred_element_type=F32, **kw)


def _dot_tn(a, b, **kw):
    return lax.dot_general(a, b, (((0,), (0,)), ((), ())), preferred_element_type=F32, **kw)


def _ada_kernel(c_ref, w_ref, b_ref, o_ref):
    s = _silu(c_ref[...])
    o_ref[0] = _dot(s, w_ref[0], precision=HIGHEST) + b_ref[0]


def _ada(cin, w_ada, b_ada):
    depth, d, n = w_ada.shape
    tn = 1024
    return pl.pallas_call(
        _ada_kernel,
        grid=(depth, n // tn),
        in_specs=[pl.BlockSpec((8, d), lambda l, j: (0, 0)),
                  pl.BlockSpec((1, d, tn), lambda l, j: (l, 0, j)),
                  pl.BlockSpec((1, 1, tn), lambda l, j: (l, 0, j))],
        out_specs=pl.BlockSpec((1, 8, tn), lambda l, j: (l, 0, j)),
        out_shape=jax.ShapeDtypeStruct((depth, 8, n), F32),
        compiler_params=_cp("parallel", "parallel"),
        name="ada",
    )(cin, w_ada, b_ada.reshape(depth, 1, n))


ROW_STEP = 256


def _modulate(x_ref, sh_ref, sc_ref, xm_ref):
    sc1, sh = 1.0 + sc_ref[0], sh_ref[0]

    def body(t, carry):
        sl = pl.ds(pl.multiple_of(t * ROW_STEP, ROW_STEP), ROW_STEP)
        xm_ref[sl, :] = (x_ref[sl, :] * sc1 + sh).astype(BF16)
        return carry

    lax.fori_loop(0, x_ref.shape[0] // ROW_STEP, body, 0)


def _inproj_kernel(n_lat_tiles, *refs):
    xs, (sh_ref, sc_ref, w_ref, o_ref, xm_ref) = refs[:-5], refs[-5:]
    first = pl.program_id(1) == 0
    if len(xs) == 1:
        pl.when(first)(lambda: _modulate(xs[0], sh_ref, sc_ref, xm_ref))
    else:
        is_lat = pl.program_id(0) < n_lat_tiles
        pl.when(first & is_lat)(lambda: _modulate(xs[0], sh_ref, sc_ref, xm_ref))
        pl.when(first & jnp.logical_not(is_lat))(lambda: _modulate(xs[1], sh_ref, sc_ref, xm_ref))
    o_ref[...] = _dot(xm_ref[...], w_ref[...])


def _inproj(xs, ada_r, w, layer, m, n_lat, seq, nb, tm=512, tn=1024):
    _, d, n = w.shape
    nlt = n_lat // tm
    row = lambda i: jnp.minimum((i * tm) // seq, nb)
    if len(xs) == 1:
        x_specs = [pl.BlockSpec((tm, d), lambda i, j: (i, 0))]
    else:
        x_specs = [pl.BlockSpec((tm, d), lambda i, j: (jnp.minimum(i, nlt - 1), 0)),
                   pl.BlockSpec((tm, d), lambda i, j: (jnp.maximum(i - nlt, 0), 0),
                                pipeline_mode=pl.Buffered(1))]
    return pl.pallas_call(
        functools.partial(_inproj_kernel, nlt),
        grid=(m // tm, n // tn),
        in_specs=x_specs + [
                  pl.BlockSpec((1, 1, d), lambda i, j: (row(i) * 6 + 0, 0, 0)),
                  pl.BlockSpec((1, 1, d), lambda i, j: (row(i) * 6 + 1, 0, 0)),
                  pl.BlockSpec((None, d, tn), lambda i, j: (layer, 0, j))],
        out_specs=pl.BlockSpec((tm, tn), lambda i, j: (i, j)),
        out_shape=jax.ShapeDtypeStruct((m, n), F32),
        scratch_shapes=[pltpu.VMEM((tm, d), BF16)],
        compiler_params=_cp("parallel", "arbitrary"),
        name="inproj",
    )(*xs, ada_r, ada_r, w)


def _layernorm(r, w, b):
    mu = jnp.mean(r, axis=-1, keepdims=True)
    rc = r - mu
    var = jnp.mean(rc * rc, axis=-1, keepdims=True)
    return rc * lax.rsqrt(var + LN_EPS) * w + b


def _head_norm_gate(o, nw, gate):
    o = o * lax.rsqrt(jnp.mean(o * o, axis=-1, keepdims=True) + RMS_EPS) * nw
    return o * _silu(gate)


def _scan_mixer_out(of_ref, ob_ref, gate_ref, nw_ref):
    nw = nw_ref[...]
    heads = []
    for h in range(N_HEADS):
        hs = slice(h * LANE, (h + 1) * LANE)
        heads.append(_head_norm_gate(of_ref[:, hs] + ob_ref[:, hs], nw, gate_ref[:, hs]).astype(BF16))
    return jnp.concatenate(heads, axis=-1)


def _outproj_kernel(alpha, n_lat_tiles, n_split, *refs):
    pairs, rest = refs[:n_split], refs[n_split:]
    (gaf_ref, gab_ref, gag_ref, hgf_ref, hgb_ref, hgg_ref, nwa_ref, nwh_ref,
     w_ref, g_ref, lw_ref, lb_ref, o_ref) = rest
    gw = GROUP_WIDTH

    def run(x_ref, yb_ref, ym_ref):
        acc = _dot(_scan_mixer_out(gaf_ref, gab_ref, gag_ref, nwa_ref), w_ref[0:gw, :])
        acc += _dot(yb_ref[...], w_ref[gw:2 * gw, :])
        acc += _dot(ym_ref[...], w_ref[2 * gw:3 * gw, :])
        acc += _dot(_scan_mixer_out(hgf_ref, hgb_ref, hgg_ref, nwh_ref), w_ref[3 * gw:4 * gw, :])
        r = alpha * x_ref[...] + g_ref[0] * acc
        o_ref[...] = _layernorm(r, lw_ref[...], lb_ref[...])

    if n_split == 3:
        run(*pairs)
    else:
        is_lat = pl.program_id(0) < n_lat_tiles
        pl.when(is_lat)(lambda: run(*pairs[0::2]))
        pl.when(jnp.logical_not(is_lat))(lambda: run(*pairs[1::2]))


def _outproj(xs, gdn_o, ybs, yms, hgrn_o, z, nwa, nwh, w, layer, ada_r, lw, lb, m_out, n_lat, seq, nb, alpha,
             tm=256):
    d = w.shape[1]
    gw = GROUP_WIDTH
    nlt = n_lat // tm
    row = lambda i: jnp.minimum((i * tm) // seq, nb)
    in_specs, args = [], []
    dual = len(xs) == 2
    for arrs, width in ((xs, d), (ybs, gw), (yms, gw)):
        if dual:
            in_specs += [pl.BlockSpec((tm, width), lambda i: (jnp.minimum(i, nlt - 1), 0)),
                         pl.BlockSpec((tm, width), lambda i: (jnp.maximum(i - nlt, 0), 0))]
        else:
            in_specs += [pl.BlockSpec((tm, width), lambda i: (i, 0))]
        args += list(arrs)
    n_split = len(args)
    rowspec = pl.BlockSpec((tm, gw), lambda i: (i, 0))
    gate = lambda cb: pl.BlockSpec((tm, gw), lambda i: (i, cb * LANE // gw))
    vec = lambda n: pl.BlockSpec((1, n), lambda i: (0, 0))
    in_specs += [rowspec, rowspec, gate(CB_GGATE), rowspec, rowspec, gate(CB_HG), vec(LANE), vec(LANE),
                 pl.BlockSpec((None, d, d), lambda i: (layer, 0, 0), pipeline_mode=pl.Buffered(1)),
                 pl.BlockSpec((1, 1, d), lambda i: (row(i) * 6 + 2, 0, 0)), vec(d), vec(d)]
    args += [*gdn_o, z, *hgrn_o, z, nwa.reshape(1, LANE), nwh.reshape(1, LANE), w, ada_r,
             lw.reshape(1, d), lb.reshape(1, d)]
    return pl.pallas_call(
        functools.partial(_outproj_kernel, alpha, nlt, n_split),
        grid=(m_out // tm,),
        in_specs=in_specs,
        out_specs=pl.BlockSpec((tm, d), lambda i: (i, 0)),
        out_shape=jax.ShapeDtypeStruct((m_out, d), F32),
        compiler_params=_cp("parallel"),
        name="outproj_ln",
    )(*args)


def _mlp_kernel(alpha, x_ref, sh_ref, sc_ref, g_ref, w1_ref, w2_ref, lw_ref, lb_ref, o_ref, xm_ref, acc_ref):
    k = pl.program_id(1)

    @pl.when(k == 0)
    def _():
        _modulate(x_ref, sh_ref, sc_ref, xm_ref)
        acc_ref[...] = jnp.zeros_like(acc_ref)

    h = jnp.maximum(_dot(xm_ref[...], w1_ref[...]), 0.0)
    acc_ref[...] += _dot((h * h).astype(BF16), w2_ref[...])

    @pl.when(k == pl.num_programs(1) - 1)
    def _():
        g, lw, lb = g_ref[0], lw_ref[...], lb_ref[...]

        def body(t, carry):
            sl = pl.ds(pl.multiple_of(t * ROW_STEP, ROW_STEP), ROW_STEP)
            o_ref[sl, :] = _layernorm(alpha * x_ref[sl, :] + g * acc_ref[sl, :], lw, lb)
            return carry

        lax.fori_loop(0, x_ref.shape[0] // ROW_STEP, body, 0)


def _mlp(x_all, w1, w2, layer, ada_r, lw, lb, seq, nb, alpha, tm=512, th=1024):
    m, d = x_all.shape
    hid = w1.shape[2]
    row = lambda i: jnp.minimum((i * tm) // seq, nb)
    return pl.pallas_call(
        functools.partial(_mlp_kernel, alpha),
        grid=(m // tm, hid // th),
        in_specs=[pl.BlockSpec((tm, d), lambda i, k: (i, 0)),
                  pl.BlockSpec((1, 1, d), lambda i, k: (row(i) * 6 + 3, 0, 0)),
                  pl.BlockSpec((1, 1, d), lambda i, k: (row(i) * 6 + 4, 0, 0)),
                  pl.BlockSpec((1, 1, d), lambda i, k: (row(i) * 6 + 5, 0, 0)),
                  pl.BlockSpec((None, d, th), lambda i, k: (layer, 0, k)),
                  pl.BlockSpec((None, th, d), lambda i, k: (layer, k, 0)),
                  pl.BlockSpec((1, d), lambda i, k: (0, 0)),
                  pl.BlockSpec((1, d), lambda i, k: (0, 0))],
        out_specs=pl.BlockSpec((tm, d), lambda i, k: (i, 0)),
        out_shape=jax.ShapeDtypeStruct((m, d), F32),
        scratch_shapes=[pltpu.VMEM((tm, d), BF16), pltpu.VMEM((tm, d), F32)],
        compiler_params=_cp("parallel", "arbitrary"),
        name="mlp_ln",
    )(x_all, ada_r, ada_r, ada_r, w1, w2, lw.reshape(1, d), lb.reshape(1, d))


def _gdn_conv_kernel(seq, ctx_len, xl_ref, xc_ref, w_ref, o_ref, pad_ref):
    j = pl.program_id(1)
    w = w_ref[...]
    qscale = jnp.where(j < N_HEADS, HEAD_DIM ** -0.5, 1.0).astype(F32)
    p0 = 8 - GDN_CONV // 2
    for x_ref, nrows, o0 in ((xl_ref, seq, 0), (xc_ref, ctx_len, seq)):
        pad_ref[0:8, :] = jnp.zeros((8, LANE), F32)
        pad_ref[nrows + 8:nrows + 16, :] = jnp.zeros((8, LANE), F32)
        pad_ref[8:nrows + 8, :] = x_ref[...]
        rb = min(nrows, 256)
        for r0 in range(0, nrows, rb):
            y = pad_ref[r0 + p0:r0 + p0 + rb, :] * w[0:1, :]
            for i in range(1, GDN_CONV):
                y = y + pad_ref[r0 + p0 + i:r0 + p0 + i + rb, :] * w[i:i + 1, :]
            y = _silu(y)
            nrm = y * lax.rsqrt(jnp.sum(y * y, axis=-1, keepdims=True) + RMS_EPS) * qscale
            o_ref[o0 + r0:o0 + r0 + rb, :] = jnp.where(j < 2 * N_HEADS, nrm, y)


def _gdn_conv(z, conv_w, nb, seq, ctx_len):
    nblk = 3 * N_HEADS
    cblk0 = nb * seq // ctx_len
    return pl.pallas_call(
        functools.partial(_gdn_conv_kernel, seq, ctx_len),
        grid=(nb, nblk),
        in_specs=[pl.BlockSpec((seq, LANE), lambda b, j: (b, CB_GQKV + j)),
                  pl.BlockSpec((ctx_len, LANE), lambda b, j: (cblk0 + b, CB_GQKV + j)),
                  pl.BlockSpec((GDN_CONV, LANE), lambda b, j: (0, j))],
        out_specs=pl.BlockSpec((seq + ctx_len, LANE), lambda b, j: (b, j)),
        out_shape=jax.ShapeDtypeStruct((nb * (seq + ctx_len), nblk * LANE), F32),
        scratch_shapes=[pltpu.VMEM((seq + 16, LANE), F32)],
        compiler_params=_cp("parallel", "parallel"),
        name="gdn_conv",
    )(z, z, conv_w)


def _gdn_gates_kernel(tm, s_ref, alog_ref, dtb_ref, o_ref):
    s = s_ref[...]
    g = -jnp.exp(alog_ref[...]) * (jnp.maximum(s + dtb_ref[...], 0.0)
                                    + jnp.log1p(jnp.exp(-jnp.abs(s + dtb_ref[...]))))
    r = lax.broadcasted_iota(jnp.int32, (tm, tm), 0)
    c = lax.broadcasted_iota(jnp.int32, (tm, tm), 1)
    same = (r >> CHUNK_SHIFT) == (c >> CHUNK_SHIFT)
    lo = jnp.where(same & (c <= r), 1.0, 0.0).astype(F32)
    up = jnp.where(same & (c >= r), 1.0, 0.0).astype(F32)
    cum_f = _dot(lo, g, precision=HIGHEST)
    cum_b = _dot(up, g, precision=HIGHEST)
    col = lax.broadcasted_iota(jnp.int32, s.shape, 1)
    o_ref[...] = jnp.where(col < N_HEADS, cum_f,
                           jnp.where(col < 2 * N_HEADS, cum_b,
                                     jnp.where(col < 4 * N_HEADS, jax.nn.sigmoid(s), 0.0)))


def _gdn_gates(z, a_log, dt_bias, tm=512):
    m = z.shape[0]
    pad = lambda v: jnp.zeros((1, LANE), F32).at[0, :2 * N_HEADS].set(v.reshape(-1).astype(F32))
    return pl.pallas_call(
        functools.partial(_gdn_gates_kernel, tm),
        grid=(m // tm,),
        in_specs=[pl.BlockSpec((tm, LANE), lambda i: (i, CB_GAB)),
                  pl.BlockSpec((1, LANE), lambda i: (0, 0)),
                  pl.BlockSpec((1, LANE), lambda i: (0, 0))],
        out_specs=pl.BlockSpec((tm, LANE), lambda i: (i, 0)),
        out_shape=jax.ShapeDtypeStruct((m, LANE), F32),
        compiler_params=_cp("parallel"),
        name="gdn_gates",
    )(z, pad(a_log), pad(dt_bias))


def _tri_masks(n):
    r = lax.broadcasted_iota(jnp.int32, (n, n), 0)
    c = lax.broadcasted_iota(jnp.int32, (n, n), 1)
    return r, c


def _split(x):
    hi = x.astype(BF16)
    return hi, (x - hi.astype(F32)).astype(BF16)


def _dot3(a, b):
    return _dot(a[0], b[0]) + (_dot(a[0], b[1]) + _dot(a[1], b[0]))


def _gdn_prep_kernel(qkv_ref, p_ref, u_ref, w_ref, qt_ref, kt_ref, att_ref, el_ref):
    C, H = CHUNK, N_HEADS
    lane = lax.broadcasted_iota(jnp.int32, (C, LANE), 1)
    r, c = _tri_masks(C)
    eye = jnp.where(r == c, 1.0, 0.0).astype(F32)
    a_list, rhs_list, where_list = [], [], []
    for n in range(PREP_CHUNKS):
        rs = slice(n * C, (n + 1) * C)
        pblk = p_ref[rs, :]
        tblk = pblk.T
        col = lambda idx, pblk=pblk: jnp.sum(jnp.where(lane == idx, pblk, 0.0), axis=-1, keepdims=True)
        row = lambda idx, tblk=tblk: tblk[idx:idx + 1, :]
        for h in range(H):
            hs = slice(h * LANE, (h + 1) * LANE)
            q = qkv_ref[rs, h * LANE:(h + 1) * LANE]
            k = qkv_ref[rs, (H + h) * LANE:(H + h + 1) * LANE]
            v = qkv_ref[rs, (2 * H + h) * LANE:(2 * H + h + 1) * LANE]
            qbf, kbf = q.astype(BF16), k.astype(BF16)
            for d in range(2):
                idx = d * H + h
                cum_c, cum_r, beta_c = col(idx), row(idx), col(2 * H + idx)
                incl = (c >= r) if d else (c <= r)
                strict = (c > r) if d else (c < r)
                last = cum_r[:, 0:1] if d else cum_r[:, C - 1:C]
                decay = jnp.exp(jnp.where(incl, cum_c - cum_r, NEG))
                kb = k * beta_c
                ec = jnp.exp(cum_c)
                a_list.append(jnp.where(strict, _dot_nt(kb.astype(BF16), kbf) * decay, 0.0))
                rhs_list.append(jnp.concatenate([v * beta_c, kb * ec], axis=-1).astype(BF16))
                where_list.append((d, rs, hs))
                att_ref[d, h, rs, :] = jnp.where(incl, _dot_nt(qbf, kbf) * decay, 0.0).astype(BF16)
                qt_ref[d, rs, hs] = (q * ec).astype(BF16)
                kt_ref[d, rs, hs] = (k * jnp.exp(last - cum_c)).astype(BF16)
                el_ref[n, idx:idx + 1, :] = jnp.broadcast_to(jnp.exp(last), (1, LANE))
    ts = [eye - a for a in a_list]
    ps = [a.astype(BF16) for a in a_list]
    for _ in range(5):
        ps = [_dot(p, p).astype(BF16) for p in ps]
        ts = [t + _dot(t.astype(BF16), p) for t, p in zip(ts, ps)]
    res = [eye - t - _dot3(_split(a), _split(t)) for a, t in zip(a_list, ts)]
    ts = [t + _dot(t.astype(BF16), e.astype(BF16)) for t, e in zip(ts, res)]
    sols = [_dot(t.astype(BF16), rhs) for t, rhs in zip(ts, rhs_list)]
    for sol, (d, rs, hs) in zip(sols, where_list):
        u_ref[d, rs, hs] = sol[:, :HEAD_DIM]
        w_ref[d, rs, hs] = sol[:, HEAD_DIM:].astype(BF16)


def _gdn_prep(qkv, p, nb, seq, ctx_len):
    m = qkv.shape[0]
    H = N_HEADS
    gw = GROUP_WIDTH
    rows = PREP_CHUNKS * CHUNK
    assert seq % rows == 0 and ctx_len % rows == 0
    dspec = pl.BlockSpec((2, rows, gw), lambda i: (0, i, 0))
    nl, nc = seq // rows, ctx_len // rows

    def qkv_blk(i):
        ic = i - nb * nl
        return jnp.where(i < nb * nl, (i // nl) * (nl + nc) + i % nl, (ic // nc) * (nl + nc) + nl + ic % nc)

    return pl.pallas_call(
        _gdn_prep_kernel,
        grid=(m // rows,),
        in_specs=[pl.BlockSpec((rows, 3 * gw), lambda i: (qkv_blk(i), 0)),
                  pl.BlockSpec((rows, LANE), lambda i: (i, 0))],
        out_specs=[dspec, dspec, dspec, dspec,
                   pl.BlockSpec((2, H, rows, CHUNK), lambda i: (0, 0, i, 0)),
                   pl.BlockSpec((PREP_CHUNKS, 2 * H, LANE), lambda i: (i, 0, 0))],
        out_shape=[jax.ShapeDtypeStruct((2, m, gw), F32),
                   jax.ShapeDtypeStruct((2, m, gw), BF16),
                   jax.ShapeDtypeStruct((2, m, gw), BF16),
                   jax.ShapeDtypeStruct((2, m, gw), BF16),
                   jax.ShapeDtypeStruct((2, H, m, CHUNK), BF16),
                   jax.ShapeDtypeStruct((m // CHUNK, 2 * H, LANE), F32)],
        compiler_params=_cp("parallel"),
        name="gdn_prep",
    )(qkv, p)


def _gdn_scan_a(ins, s_ref, g, j):
    H = N_HEADS
    mid = []
    for d in range(2):
        u_ref, w_ref, qt_ref = ins[d:6:2]
        cj = g - 1 - j if d else j
        rs = pl.ds(pl.multiple_of(cj * CHUNK, CHUNK), CHUNK)
        for h in range(H):
            hs = slice(h * LANE, (h + 1) * LANE)
            sb = s_ref[d * H + h].astype(BF16)
            vnb = (u_ref[0, rs, hs] - _dot(w_ref[0, rs, hs], sb)).astype(BF16)
            mid.append((vnb, _dot(qt_ref[0, rs, hs], sb)))
    return mid


def _gdn_scan_b(ins, s_ref, o_refs, g, j, mid):
    H = N_HEADS
    for d in range(2):
        kt_ref, att_ref, el_ref = ins[6 + d::2]
        cj = g - 1 - j if d else j
        rs = pl.ds(pl.multiple_of(cj * CHUNK, CHUNK), CHUNK)
        for h in range(H):
            hs = slice(h * LANE, (h + 1) * LANE)
            idx = d * H + h
            vnb, o_state = mid[idx]
            s_ref[idx] = s_ref[idx] * el_ref[cj, idx:idx + 1, :] + _dot_tn(kt_ref[0, rs, hs], vnb)
            o_refs[d][rs, hs] = o_state + _dot(att_ref[0, h, rs, :], vnb)


def _scans_kernel(g, *refs):
    gdn_ins = refs[:12]
    qf_ref, ff_ref, if_ref, qb_ref, fb_ref, ib_ref, lb_ref = refs[12:19]
    gof_ref, gob_ref, of_ref, ob_ref, gs_ref, s_ref = refs[19:]
    C, H, nsub = CHUNK, N_HEADS, CHUNK // SUB

    @pl.when(pl.program_id(1) == 0)
    def _():
        s_ref[...] = jnp.zeros_like(s_ref)
        gs_ref[...] = jnp.zeros_like(gs_ref)

    r, c = _tri_masks(C)
    tri = (jnp.where(c <= r, 1.0, 0.0).astype(F32), jnp.where(c >= r, 1.0, 0.0).astype(F32))
    nbig = C // BIG
    same_big = (r >> (BIG.bit_length() - 1)) == (c >> (BIG.bit_length() - 1))
    trow = lax.broadcasted_iota(jnp.int32, (C, HEAD_DIM), 0)
    srow = lax.broadcasted_iota(jnp.int32, (SUB, C), 0)
    scol = lax.broadcasted_iota(jnp.int32, (SUB, C), 1)
    chains = [(d, h) for d in range(2) for h in range(H)]
    srcs = ((qf_ref, ff_ref, if_ref, of_ref), (qb_ref, fb_ref, ib_ref, ob_ref))

    def body(j, carry):
        ph1 = []
        for d, h in chains:
            cj = g - 1 - j if d else j
            rs, hs = pl.ds(pl.multiple_of(cj * C, C), C), slice(h * LANE, (h + 1) * LANE)
            lb = lb_ref[:, hs]
            f = lb + (1.0 - lb) * jax.nn.sigmoid(srcs[d][1][rs, hs])
            ph1.append((rs, hs, f, _dot(tri[d], jnp.log(f), precision=HIGHEST)))
        gdn_mid = _gdn_scan_a(gdn_ins, gs_ref, g, j)
        ph2 = []
        for (d, h), (rs, hs, f, cum) in zip(chains, ph1):
            q_ref, _, i_ref, _ = srcs[d]
            q = _silu(q_ref[rs, hs]) * HEAD_DIM ** -0.5
            k = 1.0 - f
            vb = i_ref[rs, hs].astype(BF16)
            last = cum[0:1, :] if d else cum[C - 1:C, :]
            idx = d * H + h
            St = s_ref[idx]
            o_state = _dot_nt((q * jnp.exp(cum)).astype(BF16), St.astype(BF16))
            s_ref[idx] = St * jnp.exp(last) + _dot_tn(vb, (k * jnp.exp(last - cum)).astype(BF16))
            lvl1 = []
            for a2 in range(nbig):
                sa = slice(a2 * BIG, (a2 + 1) * BIG)
                if d and a2 < nbig - 1:
                    cb = cum[(a2 + 1) * BIG:(a2 + 1) * BIG + 1, :]
                    kt = k * jnp.exp(jnp.where(trow >= (a2 + 1) * BIG, cb - cum, NEG))
                elif (not d) and a2 > 0:
                    cb = cum[a2 * BIG - 1:a2 * BIG, :]
                    kt = k * jnp.exp(jnp.where(trow < a2 * BIG, cb - cum, NEG))
                else:
                    lvl1.append(None)
                    continue
                lvl1.append(_dot_nt((q[sa] * jnp.exp(cum[sa] - cb)).astype(BF16), kt.astype(BF16)))
            mid_row = SUB if d else SUB - 1
            cb2 = jnp.concatenate([jnp.broadcast_to(cum[a2 * BIG + mid_row:a2 * BIG + mid_row + 1, :], (BIG, HEAD_DIM))
                                   for a2 in range(nbig)], axis=0)
            early = ((trow & (BIG - 1)) >= SUB) if d else ((trow & (BIG - 1)) < SUB)
            qt2 = q * jnp.exp(jnp.where(early, NEG, cum - cb2))
            kt2 = k * jnp.exp(jnp.where(early, cb2 - cum, NEG))
            sc2 = jnp.where(same_big, _dot_nt(qt2.astype(BF16), kt2.astype(BF16)), 0.0)
            inter = []
            for a in range(nsub):
                blk = sc2[a * SUB:(a + 1) * SUB]
                if lvl1[a // 2] is not None:
                    blk = blk + lvl1[a // 2][(a % 2) * SUB:(a % 2 + 1) * SUB]
                inter.append(blk)
            ph2.append((q, k, cum, vb, o_state, inter))
        _gdn_scan_b(gdn_ins, gs_ref, (gof_ref, gob_ref), g, j, gdn_mid)
        for (d, h), (rs, hs, _, _), (q, k, cum, vb, o_state, inter) in zip(chains, ph1, ph2):
            blocks = []
            for a in range(nsub):
                sa = slice(a * SUB, (a + 1) * SUB)
                qa, ka, ca, sc = q[sa], k[sa], cum[sa], inter[a]
                for t in range(SUB):
                    dec = jnp.exp(ca - ca[t:t + 1, :])
                    st = jnp.sum(qa * ka[t:t + 1, :] * dec, axis=-1, keepdims=True)
                    ok = (srow <= t) if d else (srow >= t)
                    sc = jnp.where((scol == a * SUB + t) & ok, st, sc)
                blocks.append(sc)
            scores = jnp.concatenate(blocks, axis=0)
            srcs[d][3][rs, hs] = o_state + _dot(scores.astype(BF16), vb)
        return carry

    lax.fori_loop(0, g, body, 0)


def _scans(u, w, qt, kt, att, el, z, lbs, nb, seq, ctx_len):
    m = z.shape[0]
    H = N_HEADS
    gw = GROUP_WIDTH
    g = ctx_len // CHUNK
    nblk = seq // ctx_len
    cblk0 = nb * nblk
    blk_f = lambda b, t: jnp.where(t == 0, cblk0 + b, b * nblk + t - 1)
    blk_b = lambda b, t: jnp.where(t == 0, cblk0 + b, b * nblk + nblk - t)
    in_specs, args = [], []
    for arr in (u, w, qt, kt):
        in_specs += [pl.BlockSpec((1, ctx_len, gw), lambda b, t: (0, blk_f(b, t), 0)),
                     pl.BlockSpec((1, ctx_len, gw), lambda b, t: (1, blk_b(b, t), 0))]
        args += [arr, arr]
    in_specs += [pl.BlockSpec((1, H, ctx_len, CHUNK), lambda b, t: (0, 0, blk_f(b, t), 0)),
                 pl.BlockSpec((1, H, ctx_len, CHUNK), lambda b, t: (1, 0, blk_b(b, t), 0)),
                 pl.BlockSpec((g, 2 * H, LANE), lambda b, t: (blk_f(b, t), 0, 0)),
                 pl.BlockSpec((g, 2 * H, LANE), lambda b, t: (blk_b(b, t), 0, 0))]
    args += [att, att, el, el]
    col = lambda cb: cb * LANE // gw
    zspec = lambda blk, cb: pl.BlockSpec((ctx_len, gw), lambda b, t: (blk(b, t), col(cb)))
    in_specs += [zspec(blk_f, CB_HQ), zspec(blk_f, CB_HFF), zspec(blk_f, CB_HI),
                 zspec(blk_b, CB_HQ), zspec(blk_b, CB_HFB), zspec(blk_b, CB_HI),
                 pl.BlockSpec((1, gw), lambda b, t: (0, 0))]
    args += [z, z, z, z, z, z, lbs.reshape(1, gw)]
    ospec_f = pl.BlockSpec((ctx_len, gw), lambda b, t: (blk_f(b, t), 0))
    ospec_b = pl.BlockSpec((ctx_len, gw), lambda b, t: (blk_b(b, t), 0))
    state = pltpu.VMEM((2 * H, HEAD_DIM, HEAD_DIM), F32)
    outs = pl.pallas_call(
        functools.partial(_scans_kernel, g),
        grid=(nb, nblk + 1),
        in_specs=in_specs,
        out_specs=[ospec_f, ospec_b, ospec_f, ospec_b],
        out_shape=[jax.ShapeDtypeStruct((m, gw), F32)] * 4,
        scratch_shapes=[state, state],
        compiler_params=_cp("parallel", "arbitrary"),
        name="scans",
    )(*args)
    return outs[:2], outs[2:]


def _softmax_pv(parts):
    m = parts[0][0].max(axis=-1, keepdims=True)
    for s, _ in parts[1:]:
        m = jnp.maximum(m, s.max(axis=-1, keepdims=True))
    den, acc = None, None
    for s, v in parts:
        p = jnp.exp(s - m)
        d = jnp.sum(p, axis=-1, keepdims=True)
        a = _dot(p.astype(BF16), v)
        den = d if den is None else den + d
        acc = a if acc is None else acc + a
    return acc / den


def _na_kernel(emit_ctx, rows, *refs):
    q_ref, k_ref, v_ref, qc_ref, kc_ref, vc_ref, bias_ref = refs[:7]
    if emit_ctx:
        yl_ref, yc_ref, kb_ref, vb_ref = refs[7:]
    else:
        yl_ref, kb_ref, vb_ref = refs[7:]
    scale = HEAD_DIM ** -0.5
    win = NA_ROWS * GRID_W
    kb_ref[...] = k_ref[...].astype(BF16)
    vb_ref[...] = v_ref[...].astype(BF16)
    kc = kc_ref[...].astype(BF16)
    vc = vc_ref[...].astype(BF16)

    def body(i, carry):
        pre = []
        for t in range(NA_UNROLL):
            r = i * NA_UNROLL + t
            row0 = jnp.clip(r - NA_ROWS // 2, 0, rows - NA_ROWS)
            qs = pl.ds(pl.multiple_of(r * GRID_W, GRID_W), GRID_W)
            ks = pl.ds(pl.multiple_of(row0 * GRID_W, GRID_W), win)
            q = q_ref[qs, :].astype(BF16)
            pre.append((qs, ks, _dot_nt(q, kb_ref[ks, :]) * scale + bias_ref[0, r - row0], _dot_nt(q, kc) * scale))
        mid = []
        for qs, ks, s_win, s_ctx in pre:
            m = jnp.maximum(s_win.max(axis=-1, keepdims=True), s_ctx.max(axis=-1, keepdims=True))
            p_win, p_ctx = jnp.exp(s_win - m), jnp.exp(s_ctx - m)
            den = jnp.sum(p_win, axis=-1, keepdims=True) + jnp.sum(p_ctx, axis=-1, keepdims=True)
            mid.append((qs, ks, p_win.astype(BF16), p_ctx.astype(BF16), den))
        for qs, ks, p_win, p_ctx, den in mid:
            yl_ref[qs, :] = ((_dot(p_win, vb_ref[ks, :]) + _dot(p_ctx, vc)) / den).astype(yl_ref.dtype)
        return carry

    lax.fori_loop(0, rows // NA_UNROLL, body, 0)
    if emit_ctx:
        s = _dot_nt(qc_ref[...].astype(BF16), kc) * scale
        yc_ref[...] = _softmax_pv([(s, vc)]).astype(yc_ref.dtype)


def _na_bias_kernel(rpb_ref, o_ref):
    n = lax.broadcasted_iota(jnp.int32, (LANE, GRID_W * GRID_W), 1)
    j = lax.broadcasted_iota(jnp.int32, (LANE, GRID_W * GRID_W), 0)
    q, w = n >> 6, n & (GRID_W - 1)
    dc = jnp.clip(w - q, 1 - NA_COLS, NA_COLS - 1) + NA_COLS - 1
    onehot = jnp.where(dc == j, 1.0, 0.0).astype(F32)
    m = _dot(rpb_ref[...], onehot, precision=HIGHEST)
    c0 = jnp.clip(q[0:1] - NA_COLS // 2, 0, GRID_W - NA_COLS)
    ok = (w[0:1] >= c0) & (w[0:1] < c0 + NA_COLS)
    o_ref[...] = jnp.where(ok, m, NEG)


def _na_bias_tables(rpb):
    depth, H, nr, nc = rpb.shape
    assert GRID_W == 64 and depth * H * nr <= LANE and nc <= LANE
    flat = jnp.zeros((LANE, LANE), F32).at[:depth * H * nr, :nc].set(rpb.reshape(-1, nc).astype(F32))
    m = pl.pallas_call(
        _na_bias_kernel,
        out_shape=jax.ShapeDtypeStruct((LANE, GRID_W * GRID_W), F32),
        compiler_params=pltpu.CompilerParams(vmem_limit_bytes=VMEM_LIMIT),
        name="na_bias",
    )(flat)
    m = m[:depth * H * nr].reshape(depth, H, nr, GRID_W, GRID_W)
    tab = jnp.stack([jnp.stack([m[:, :, k - s + NA_ROWS - 1] for k in range(NA_ROWS)], axis=3)
                     for s in range(NA_ROWS)], axis=2)
    return tab.reshape(depth, H, NA_ROWS, GRID_W, NA_ROWS * GRID_W)


def _na(z, bias, nb, seq, ctx_len, emit_ctx):
    rows = seq // GRID_W
    assert seq % GRID_W == 0 and rows >= NA_ROWS and rows % NA_UNROLL == 0
    cblk0 = nb * seq // ctx_len
    H = N_HEADS
    lat = lambda cb: pl.BlockSpec((seq, LANE), lambda b, h: (b, cb + h))
    ctx = lambda cb: pl.BlockSpec((ctx_len, LANE), lambda b, h: (cblk0 + b, cb + h))
    win = NA_ROWS * GRID_W
    in_specs = [lat(CB_NAQ), lat(CB_NAK), lat(CB_NAV), ctx(CB_NAQ), ctx(CB_NAK), ctx(CB_NAV),
                pl.BlockSpec((1, NA_ROWS, GRID_W, win), lambda b, h: (h, 0, 0, 0))]
    out_specs = [pl.BlockSpec((seq, LANE), lambda b, h: (b, h))]
    out_shape = [jax.ShapeDtypeStruct((nb * seq, GROUP_WIDTH), BF16)]
    if emit_ctx:
        out_specs.append(pl.BlockSpec((ctx_len, LANE), lambda b, h: (b, h)))
        out_shape.append(jax.ShapeDtypeStruct((nb * ctx_len, GROUP_WIDTH), BF16))
    return pl.pallas_call(
        functools.partial(_na_kernel, emit_ctx, rows),
        grid=(nb, H),
        in_specs=in_specs, out_specs=out_specs, out_shape=out_shape,
        scratch_shapes=[pltpu.VMEM((seq, LANE), BF16), pltpu.VMEM((seq, LANE), BF16)],
        compiler_params=_cp("parallel", "parallel"),
        name="na_attn",
    )(z, z, z, z, z, z, bias)


def _rms(x, w):
    return x * lax.rsqrt(jnp.mean(x * x, axis=-1, keepdims=True) + RMS_EPS) * w


def _mla_prep_kernel(cq_ref, ckv_ref, kra_ref, krb_ref, cc_ref, ss_ref, qnw_ref, kvnw_ref, wuq_ref, wukv_ref,
                     q_ref, kn_ref, kr_ref, v_ref):
    cc, ss = cc_ref[...], ss_ref[...]
    qn = _rms(cq_ref[...], qnw_ref[...]).astype(BF16)
    qa = _dot(qn, wuq_ref[...])
    for h in range(N_HEADS):
        b = 3 * LANE * h
        q_ref[:, 2 * LANE * h:2 * LANE * h + LANE] = qa[:, b:b + LANE].astype(BF16)
        q_ref[:, 2 * LANE * h + LANE:2 * LANE * (h + 1)] = (
            qa[:, b + LANE:b + 2 * LANE] * cc + qa[:, b + 2 * LANE:b + 3 * LANE] * ss).astype(BF16)
    kvn = _rms(ckv_ref[...], kvnw_ref[...]).astype(BF16)
    kv = _dot(kvn, wukv_ref[...])
    kn_ref[...] = kv[:, :GROUP_WIDTH].astype(BF16)
    v_ref[...] = kv[:, GROUP_WIDTH:].astype(BF16)
    kr_ref[...] = (kra_ref[...] * cc + krb_ref[...] * ss).astype(BF16)


def _mla_prep(z, cc, ss, qnw, kvnw, wuq, wukv, n_lat, seq, tm=512):
    m = z.shape[0]
    nlt, spt = n_lat // tm, seq // tm
    tab = lambda i: jnp.where(i < nlt, i % spt, spt)
    H = N_HEADS
    return pl.pallas_call(
        _mla_prep_kernel,
        grid=(m // tm,),
        in_specs=[pl.BlockSpec((tm, MLA_Q_RANK), lambda i: (i, CB_MQ * LANE // MLA_Q_RANK)),
                  pl.BlockSpec((tm, MLA_KV_RANK), lambda i: (i, CB_MKV * LANE // MLA_KV_RANK)),
                  pl.BlockSpec((tm, LANE), lambda i: (i, CB_MKRA)),
                  pl.BlockSpec((tm, LANE), lambda i: (i, CB_MKRB)),
                  pl.BlockSpec((tm, LANE), lambda i: (tab(i), 0)),
                  pl.BlockSpec((tm, LANE), lambda i: (tab(i), 0)),
                  pl.BlockSpec((1, MLA_Q_RANK), lambda i: (0, 0)),
                  pl.BlockSpec((1, MLA_KV_RANK), lambda i: (0, 0)),
                  pl.BlockSpec(wuq.shape, lambda i: (0, 0)),
                  pl.BlockSpec(wukv.shape, lambda i: (0, 0))],
        out_specs=[pl.BlockSpec((tm, 2 * LANE * H), lambda i: (i, 0)),
                   pl.BlockSpec((tm, GROUP_WIDTH), lambda i: (i, 0)),
                   pl.BlockSpec((tm, LANE), lambda i: (i, 0)),
                   pl.BlockSpec((tm, GROUP_WIDTH), lambda i: (i, 0))],
        out_shape=[jax.ShapeDtypeStruct((m, 2 * LANE * H), BF16),
                   jax.ShapeDtypeStruct((m, GROUP_WIDTH), BF16),
                   jax.ShapeDtypeStruct((m, LANE), BF16),
                   jax.ShapeDtypeStruct((m, GROUP_WIDTH), BF16)],
        compiler_params=_cp("parallel"),
        name="mla_prep",
    )(z, z, z, z, cc, ss, qnw.reshape(1, -1), kvnw.reshape(1, -1), wuq, wukv)


def _mla_attn_kernel(with_lat, seq, *refs):
    if with_lat:
        q_ref, knl_ref, krl_ref, vl_ref, knc_ref, krc_ref, vc_ref, y_ref, k_scr = refs
    else:
        q_ref, knc_ref, krc_ref, vc_ref, y_ref, k_scr = refs
    scale = MLA_QK_DIM ** -0.5
    nk = k_scr.shape[0]

    @pl.when(pl.program_id(2) == 0)
    def _():
        if with_lat:
            k_scr[0:seq, 0:LANE] = knl_ref[...]
            k_scr[0:seq, LANE:2 * LANE] = krl_ref[...]
        k_scr[nk - knc_ref.shape[0]:nk, 0:LANE] = knc_ref[...]
        k_scr[nk - knc_ref.shape[0]:nk, LANE:2 * LANE] = krc_ref[...]

    tq = q_ref.shape[0]
    qsub = min(tq, MLA_QSUB)
    nsub = tq // qsub
    vals = ([vl_ref] if with_lat else []) + [vc_ref]

    def qk(s):
        q = q_ref[s * qsub:(s + 1) * qsub, :]
        out = [_dot_nt(q, k_scr[0:seq, :])] if with_lat else []
        return out + [_dot_nt(q, k_scr[nk - knc_ref.shape[0]:nk, :])]

    def softmax(raw):
        m = raw[0].max(axis=-1, keepdims=True)
        for s in raw[1:]:
            m = jnp.maximum(m, s.max(axis=-1, keepdims=True))
        ps = [jnp.exp2((s - m) * (scale * LOG2E)) for s in raw]
        den = sum(jnp.sum(p, axis=-1, keepdims=True) for p in ps)
        return [p.astype(BF16) for p in ps], den

    raw = qk(0)
    for s in range(nsub):
        nxt = qk(s + 1) if s + 1 < nsub else None
        ps, den = softmax(raw)
        acc = sum(_dot(p, v[...]) for p, v in zip(ps, vals))
        y_ref[s * qsub:(s + 1) * qsub, :] = (acc / den).astype(y_ref.dtype)
        raw = nxt


def _mla_attn(q, kn, kr, v, nb, seq, ctx_len, with_lat, tq=4096):
    H = N_HEADS
    cblk0 = nb * seq // ctx_len
    nq = seq if with_lat else ctx_len
    tq = min(tq, nq)
    qblk0 = 0 if with_lat else nb * seq // tq
    ctxs = [pl.BlockSpec((ctx_len, LANE), lambda b, h, i: (cblk0 + b, h)),
            pl.BlockSpec((ctx_len, LANE), lambda b, h, i: (cblk0 + b, 0)),
            pl.BlockSpec((ctx_len, LANE), lambda b, h, i: (cblk0 + b, h))]
    lats = [pl.BlockSpec((seq, LANE), lambda b, h, i: (b, h)),
            pl.BlockSpec((seq, LANE), lambda b, h, i: (b, 0)),
            pl.BlockSpec((seq, LANE), lambda b, h, i: (b, h))]
    in_specs = [pl.BlockSpec((tq, 2 * LANE), lambda b, h, i: (qblk0 + b * (nq // tq) + i, h))]
    args = [q]
    if with_lat:
        in_specs += lats
        args += [kn, kr, v]
    in_specs += ctxs
    args += [kn, kr, v]
    nk = (seq if with_lat else 0) + ctx_len
    return pl.pallas_call(
        functools.partial(_mla_attn_kernel, with_lat, seq),
        grid=(nb, H, nq // tq),
        in_specs=in_specs,
        out_specs=pl.BlockSpec((tq, LANE), lambda b, h, i: (b * (nq // tq) + i, h)),
        out_shape=jax.ShapeDtypeStruct((nb * nq, GROUP_WIDTH), BF16),
        scratch_shapes=[pltpu.VMEM((nk, 2 * LANE), BF16)],
        compiler_params=_cp("parallel", "parallel", "arbitrary"),
        name="mla_attn_lat" if with_lat else "mla_attn_ctx",
    )(*args)


def _prep_w_in(w_in):
    gw = GROUP_WIDTH
    o_na = 4 * gw + 4 * N_HEADS
    o_mla = o_na + 3 * gw
    o_kv = o_mla + MLA_Q_RANK
    o_kr = o_kv + MLA_KV_RANK
    o_hg = o_kr + MLA_ROPE
    depth, d, n_in = w_in.shape
    kr = w_in[..., o_kr:o_hg].astype(BF16)
    k1, k2 = kr[..., 0::2], kr[..., 1::2]
    zpad = jnp.zeros((depth, d, LANE - MLA_ROPE), BF16)
    kra, krb = jnp.concatenate([k1, k2, zpad], axis=-1), jnp.concatenate([k2, k1, zpad], axis=-1)
    moves = ((CB_MQ, o_mla, MLA_Q_RANK), (CB_MKV, o_kv, MLA_KV_RANK), (CB_GAB, 4 * gw, 4 * N_HEADS),
             (CB_GQKV, 0, 4 * gw), (CB_NAQ, o_na, 3 * gw), (CB_HQ, o_hg, 5 * gw))

    def body(w_ref, kra_ref, krb_ref, o_ref):
        for cb, src, width in moves:
            lo = src // LANE * LANE
            hi = min(-(-(src + width) // LANE) * LANE, n_in)
            piece = w_ref[:, lo:hi][:, src - lo:src - lo + width].astype(BF16)
            pad = -width % LANE
            if pad:
                piece = jnp.concatenate([piece, jnp.zeros((piece.shape[0], pad), BF16)], axis=-1)
            o_ref[:, cb * LANE:cb * LANE + width + pad] = piece
        o_ref[:, CB_MKRA * LANE:(CB_MKRA + 1) * LANE] = kra_ref[...]
        o_ref[:, CB_MKRB * LANE:(CB_MKRB + 1) * LANE] = krb_ref[...]

    tr = 256
    return pl.pallas_call(
        body,
        grid=(depth, d // tr),
        in_specs=[pl.BlockSpec((None, tr, n_in), lambda l, i: (l, i, 0)),
                  pl.BlockSpec((None, tr, LANE), lambda l, i: (l, i, 0)),
                  pl.BlockSpec((None, tr, LANE), lambda l, i: (l, i, 0))],
        out_specs=pl.BlockSpec((None, tr, NP_IN), lambda l, i: (l, i, 0)),
        out_shape=jax.ShapeDtypeStruct((depth, d, NP_IN), BF16),
        compiler_params=_cp("parallel", "parallel"),
        name="w_in_layout",
    )(w_in, kra, krb)


def _prep_w_uq(w_uq):
    r = w_uq.shape[0]
    z = jnp.zeros((r, LANE - MLA_ROPE), w_uq.dtype)
    cols = []
    for h in range(N_HEADS):
        wh = w_uq[:, h * MLA_QK_DIM:(h + 1) * MLA_QK_DIM]
        rope = wh[:, MLA_NOPE:]
        r1, r2 = rope[:, 0::2], rope[:, 1::2]
        cols += [wh[:, :MLA_NOPE], r1, r2, z, r2, r1, z]
    return jnp.concatenate(cols, axis=1).astype(BF16)


def _rope_tables(seq, tm):
    n_freq = MLA_ROPE // 4
    freqs = ROPE_BASE ** (-jnp.arange(n_freq, dtype=F32) / n_freq)
    t = jnp.arange(seq)
    ang = jnp.concatenate([(t // GRID_W).astype(F32)[:, None] * freqs,
                           (t % GRID_W).astype(F32)[:, None] * freqs], -1)
    cos, sin = jnp.cos(ang), jnp.sin(ang)
    zp = jnp.zeros((seq, LANE - MLA_ROPE), F32)
    cc = jnp.concatenate([cos, cos, zp], axis=1)
    ss = jnp.concatenate([-sin, sin, zp], axis=1)
    ident = jnp.zeros((tm, LANE), F32).at[:, :MLA_ROPE].set(1.0)
    return jnp.concatenate([cc, ident], axis=0), jnp.concatenate([ss, jnp.zeros((tm, LANE), F32)], axis=0)


def kernel(x, c, ctx, c_ctx, w_ada, b_ada, w_in, gdn_conv_w, gdn_a_log, gdn_dt_bias, gdn_norm_w, na_rpb,
           mla_q_norm_w, mla_kv_norm_w, mla_w_uq, mla_w_uk, mla_w_uv, hgrn_lower_bounds, hgrn_norm_w, w_out,
           ln1_w, ln1_b, w_mlp1, w_mlp2, ln2_w, ln2_b):
    nb, seq, d = x.shape
    ctx_len = ctx.shape[1]
    depth = w_ada.shape[0]
    n_lat, n_ctx = nb * seq, nb * ctx_len
    alpha = (2 * depth) ** 0.25
    tm = 512
    tmm = 1024 if (seq % 1024 == 0 and n_ctx % 1024 == 0) else tm
    assert nb < 8 and seq % tm == 0 and n_ctx % tm == 0 and seq % ctx_len == 0 and ctx_len % CHUNK == 0

    cin = jnp.zeros((8, d), F32).at[:nb].set(c).at[nb].set(c_ctx)
    ada = _ada(cin, w_ada, b_ada)
    p_lb = jax.nn.softmax(hgrn_lower_bounds.astype(F32), axis=0)
    lbs = jnp.cumsum(p_lb, axis=0) - p_lb[0]
    cc, ss = _rope_tables(seq, tm)
    na_bias = _na_bias_tables(na_rpb)
    w_in_b, w_out_b = _prep_w_in(w_in.astype(BF16)), w_out.astype(BF16)
    w_mlp1_b, w_mlp2_b = w_mlp1.astype(BF16), w_mlp2.astype(BF16)

    xs = (x.reshape(n_lat, d), ctx.reshape(n_ctx, d))
    for l in range(depth):
        emit_ctx = l < depth - 1
        ada_r = ada[l].reshape(8 * 6, 1, d)
        z = _inproj(xs, ada_r, w_in_b, l, n_lat + n_ctx, n_lat, seq, nb, tm=tmm)

        qkv = _gdn_conv(z, gdn_conv_w[l], nb, seq, ctx_len)
        p = _gdn_gates(z, gdn_a_log[l], gdn_dt_bias[l], tm=tm // 2)
        gdn_o, hgrn_o = _scans(*_gdn_prep(qkv, p, nb, seq, ctx_len), z, lbs[l], nb, seq, ctx_len)
        m_out = n_lat + n_ctx if emit_ctx else n_lat
        yb = _na(z, na_bias[l], nb, seq, ctx_len, emit_ctx)
        q, kn, kr, v = _mla_prep(z, cc, ss, mla_q_norm_w[l], mla_kv_norm_w[l], _prep_w_uq(mla_w_uq[l]),
                                 jnp.concatenate([mla_w_uk[l], mla_w_uv[l]], axis=1).astype(BF16),
                                 n_lat, seq, tm=tm)
        ym = [_mla_attn(q, kn, kr, v, nb, seq, ctx_len, True)]
        if emit_ctx:
            ym.append(_mla_attn(q, kn, kr, v, nb, seq, ctx_len, False))
        if len(xs) == 1 and emit_ctx:
            yb, ym = [jnp.concatenate(yb, axis=0)], [jnp.concatenate(ym, axis=0)]
        x_all = _outproj(xs if emit_ctx else xs[:1], gdn_o, yb, ym, hgrn_o, z, gdn_norm_w[l], hgrn_norm_w[l],
                         w_out_b, l, ada_r, ln1_w[l], ln1_b[l], m_out, n_lat, seq, nb, alpha,
                         tm=tm // 2 if len(xs) == 2 else tm)
        x_all = _mlp(x_all, w_mlp1_b, w_mlp2_b, l, ada_r, ln2_w[l], ln2_b[l], seq, nb, alpha, tm=tm)
        xs = (x_all,)
    return x_all[:n_lat].reshape(nb, seq, d)
```

```python
import functools

import jax
import jax.numpy as jnp
from jax import lax
from jax.experimental import pallas as pl
from jax.experimental.pallas import tpu as pltpu

F32 = jnp.float32
BF16 = jnp.bfloat16
HIGHEST = lax.Precision.HIGHEST

GRID_W = 64
N_HEADS = 4
HEAD_DIM = 128
GROUP_WIDTH = 512
CHUNK = 64
SUB = 8
BIG = 2 * SUB
GDN_CONV = 5
NA_ROWS = 8
NA_COLS = 16
NA_UNROLL = 16
MLA_Q_RANK = 384
MLA_KV_RANK = 256
MLA_NOPE = 128
MLA_ROPE = 64
MLA_QK_DIM = MLA_NOPE + MLA_ROPE
MLA_QSUB = 512
ROPE_BASE = 10000.0
LN_EPS = 1e-5
RMS_EPS = 1e-6
LOG2E = 1.4426950408889634
NEG = -1e30

LANE = 128
CB_MQ, CB_MKRA, CB_MKV, CB_MKRB, CB_GAB = 0, 3, 4, 6, 7
CB_GQKV, CB_GGATE = 8, 20
CB_NAQ, CB_NAK, CB_NAV = 24, 28, 32
CB_HQ, CB_HFF, CB_HFB, CB_HI, CB_HG = 36, 40, 44, 48, 52
CHUNK_SHIFT = 6
PREP_CHUNKS = 4
NP_IN = 56 * LANE

VMEM_LIMIT = 48 << 20
MLP_VMEM_LIMIT = 56 << 20


def _cp(*sem):
    return pltpu.CompilerParams(dimension_semantics=sem, vmem_limit_bytes=VMEM_LIMIT)


def _silu(x):
    return x * jax.nn.sigmoid(x)


def _dot(a, b, **kw):
    return jnp.dot(a, b, preferred_element_type=F32, **kw)


def _dot_nt(a, b, **kw):
    return lax.dot_general(a, b, (((1,), (1,)), ((), ())), preferred_element_type=F32, **kw)


def _dot_tn(a, b, **kw):
    return lax.dot_general(a, b, (((0,), (0,)), ((), ())), preferred_element_type=F32, **kw)


def _ada_kernel(c_ref, w_ref, b_ref, o_ref):
    s = _silu(c_ref[...])
    o_ref[0] = _dot(s, w_ref[0], precision=HIGHEST) + b_ref[0]


def _ada(cin, w_ada, b_ada):
    depth, d, n = w_ada.shape
    tn = 1024
    return pl.pallas_call(
        _ada_kernel,
        grid=(depth, n // tn),
        in_specs=[pl.BlockSpec((8, d), lambda l, j: (0, 0)),
                  pl.BlockSpec((1, d, tn), lambda l, j: (l, 0, j)),
                  pl.BlockSpec((1, 1, tn), lambda l, j: (l, 0, j))],
        out_specs=pl.BlockSpec((1, 8, tn), lambda l, j: (l, 0, j)),
        out_shape=jax.ShapeDtypeStruct((depth, 8, n), F32),
        compiler_params=_cp("parallel", "parallel"),
        name="ada",
    )(cin, w_ada, b_ada.reshape(depth, 1, n))


ROW_STEP = 256


def _modulate(x_ref, sh_ref, sc_ref, xm_ref):
    sc1, sh = 1.0 + sc_ref[0], sh_ref[0]

    def body(t, carry):
        sl = pl.ds(pl.multiple_of(t * ROW_STEP, ROW_STEP), ROW_STEP)
        xm_ref[sl, :] = (x_ref[sl, :] * sc1 + sh).astype(BF16)
        return carry

    lax.fori_loop(0, x_ref.shape[0] // ROW_STEP, body, 0)


def _inproj_kernel(n_lat_tiles, *refs):
    xs, (sh_ref, sc_ref, w_ref, o_ref, xm_ref) = refs[:-5], refs[-5:]
    first = pl.program_id(1) == 0
    if len(xs) == 1:
        pl.when(first)(lambda: _modulate(xs[0], sh_ref, sc_ref, xm_ref))
    else:
        is_lat = pl.program_id(0) < n_lat_tiles
        pl.when(first & is_lat)(lambda: _modulate(xs[0], sh_ref, sc_ref, xm_ref))
        pl.when(first & jnp.logical_not(is_lat))(lambda: _modulate(xs[1], sh_ref, sc_ref, xm_ref))
    o_ref[...] = _dot(xm_ref[...], w_ref[...])


def _inproj(xs, ada_r, w, layer, m, n_lat, seq, nb, tm=512, tn=1024):
    _, d, n = w.shape
    nlt = n_lat // tm
    row = lambda i: jnp.minimum((i * tm) // seq, nb)
    if len(xs) == 1:
        x_specs = [pl.BlockSpec((tm, d), lambda i, j: (i, 0))]
    else:
        x_specs = [pl.BlockSpec((tm, d), lambda i, j: (jnp.minimum(i, nlt - 1), 0)),
                   pl.BlockSpec((tm, d), lambda i, j: (jnp.maximum(i - nlt, 0), 0),
                                pipeline_mode=pl.Buffered(1))]
    return pl.pallas_call(
        functools.partial(_inproj_kernel, nlt),
        grid=(m // tm, n // tn),
        in_specs=x_specs + [
                  pl.BlockSpec((1, 1, d), lambda i, j: (row(i) * 6 + 0, 0, 0)),
                  pl.BlockSpec((1, 1, d), lambda i, j: (row(i) * 6 + 1, 0, 0)),
                  pl.BlockSpec((None, d, tn), lambda i, j: (layer, 0, j))],
        out_specs=pl.BlockSpec((tm, tn), lambda i, j: (i, j)),
        out_shape=jax.ShapeDtypeStruct((m, n), F32),
        scratch_shapes=[pltpu.VMEM((tm, d), BF16)],
        compiler_params=_cp("parallel", "arbitrary"),
        name="inproj",
    )(*xs, ada_r, ada_r, w)


def _layernorm(r, w, b):
    mu = jnp.mean(r, axis=-1, keepdims=True)
    rc = r - mu
    var = jnp.mean(rc * rc, axis=-1, keepdims=True)
    return rc * lax.rsqrt(var + LN_EPS) * w + b


def _head_norm_gate(o, nw, gate):
    o = o * lax.rsqrt(jnp.mean(o * o, axis=-1, keepdims=True) + RMS_EPS) * nw
    return o * _silu(gate)


def _scan_mixer_out(of_ref, ob_ref, gate_ref, nw_ref):
    nw = nw_ref[...]
    heads = []
    for h in range(N_HEADS):
        hs = slice(h * LANE, (h + 1) * LANE)
        heads.append(_head_norm_gate(of_ref[:, hs] + ob_ref[:, hs], nw, gate_ref[:, hs]).astype(BF16))
    return jnp.concatenate(heads, axis=-1)


def _outproj_kernel(alpha, n_lat_tiles, n_split, *refs):
    pairs, rest = refs[:n_split], refs[n_split:]
    (gaf_ref, gab_ref, gag_ref, hgf_ref, hgb_ref, hgg_ref, nwa_ref, nwh_ref,
     w_ref, g_ref, lw_ref, lb_ref, o_ref) = rest
    gw = GROUP_WIDTH

    def run(x_ref, yb_ref, ym_ref):
        acc = _dot(_scan_mixer_out(gaf_ref, gab_ref, gag_ref, nwa_ref), w_ref[0:gw, :])
        acc += _dot(yb_ref[...], w_ref[gw:2 * gw, :])
        acc += _dot(ym_ref[...], w_ref[2 * gw:3 * gw, :])
        acc += _dot(_scan_mixer_out(hgf_ref, hgb_ref, hgg_ref, nwh_ref), w_ref[3 * gw:4 * gw, :])
        r = alpha * x_ref[...] + g_ref[0] * acc
        o_ref[...] = _layernorm(r, lw_ref[...], lb_ref[...])

    if n_split == 3:
        run(*pairs)
    else:
        is_lat = pl.program_id(0) < n_lat_tiles
        pl.when(is_lat)(lambda: run(*pairs[0::2]))
        pl.when(jnp.logical_not(is_lat))(lambda: run(*pairs[1::2]))


def _outproj(xs, gdn_o, ybs, yms, hgrn_o, z, nwa, nwh, w, layer, ada_r, lw, lb, m_out, n_lat, seq, nb, alpha,
             tm=256):
    d = w.shape[1]
    gw = GROUP_WIDTH
    nlt = n_lat // tm
    row = lambda i: jnp.minimum((i * tm) // seq, nb)
    in_specs, args = [], []
    dual = len(xs) == 2
    for arrs, width in ((xs, d), (ybs, gw), (yms, gw)):
        if dual:
            in_specs += [pl.BlockSpec((tm, width), lambda i: (jnp.minimum(i, nlt - 1), 0)),
                         pl.BlockSpec((tm, width), lambda i: (jnp.maximum(i - nlt, 0), 0))]
        else:
            in_specs += [pl.BlockSpec((tm, width), lambda i: (i, 0))]
        args += list(arrs)
    n_split = len(args)
    rowspec = pl.BlockSpec((tm, gw), lambda i: (i, 0))
    gate = lambda cb: pl.BlockSpec((tm, gw), lambda i: (i, cb * LANE // gw))
    vec = lambda n: pl.BlockSpec((1, n), lambda i: (0, 0))
    in_specs += [rowspec, rowspec, gate(CB_GGATE), rowspec, rowspec, gate(CB_HG), vec(LANE), vec(LANE),
                 pl.BlockSpec((None, d, d), lambda i: (layer, 0, 0), pipeline_mode=pl.Buffered(1)),
                 pl.BlockSpec((1, 1, d), lambda i: (row(i) * 6 + 2, 0, 0)), vec(d), vec(d)]
    args += [*gdn_o, z, *hgrn_o, z, nwa.reshape(1, LANE), nwh.reshape(1, LANE), w, ada_r,
             lw.reshape(1, d), lb.reshape(1, d)]
    return pl.pallas_call(
        functools.partial(_outproj_kernel, alpha, nlt, n_split),
        grid=(m_out // tm,),
        in_specs=in_specs,
        out_specs=pl.BlockSpec((tm, d), lambda i: (i, 0)),
        out_shape=jax.ShapeDtypeStruct((m_out, d), F32),
        compiler_params=_cp("parallel"),
        name="outproj_ln",
    )(*args)


MLP_WBUF = 3


def _mlp_kernel(alpha, layer, th, x_ref, sh_ref, sc_ref, g_ref, w1_hbm, w2_hbm, lw_ref, lb_ref, o_ref,
                xm_ref, w1_buf, w2_buf, sem):
    acc_ref = o_ref
    k, nk = pl.program_id(1), pl.num_programs(1)
    s = pl.program_id(0) * nk + k
    total = pl.num_programs(0) * nk
    ahead = MLP_WBUF - 1

    def copies(step):
        tile, slot = lax.rem(step, nk), lax.rem(step, MLP_WBUF)
        cols = pl.ds(pl.multiple_of(tile * th, th), th)
        return (pltpu.make_async_copy(w1_hbm.at[layer, :, cols], w1_buf.at[slot], sem.at[0, slot]),
                pltpu.make_async_copy(w2_hbm.at[layer, cols, :], w2_buf.at[slot], sem.at[1, slot]))

    @pl.when(s == 0)
    def _():
        for step in range(ahead):
            for c in copies(step):
                c.start()

    @pl.when(s + ahead < total)
    def _():
        for c in copies(s + ahead):
            c.start()

    @pl.when(k == 0)
    def _():
        _modulate(x_ref, sh_ref, sc_ref, xm_ref)
        acc_ref[...] = jnp.zeros_like(acc_ref)

    for c in copies(s):
        c.wait()
    slot = lax.rem(s, MLP_WBUF)
    h = jnp.maximum(_dot(xm_ref[...], w1_buf[slot]), 0.0)
    acc_ref[...] += _dot((h * h).astype(BF16), w2_buf[slot])

    @pl.when(k == pl.num_programs(1) - 1)
    def _():
        g, lw, lb = g_ref[0], lw_ref[...], lb_ref[...]

        def body(t, carry):
            sl = pl.ds(pl.multiple_of(t * ROW_STEP, ROW_STEP), ROW_STEP)
            o_ref[sl, :] = _layernorm(alpha * x_ref[sl, :] + g * acc_ref[sl, :], lw, lb)
            return carry

        lax.fori_loop(0, x_ref.shape[0] // ROW_STEP, body, 0)


def _mlp(x_all, w1, w2, layer, ada_r, lw, lb, seq, nb, alpha, tm=512, th=1024):
    m, d = x_all.shape
    hid = w1.shape[2]
    row = lambda i: jnp.minimum((i * tm) // seq, nb)
    return pl.pallas_call(
        functools.partial(_mlp_kernel, alpha, layer, th),
        grid=(m // tm, hid // th),
        in_specs=[pl.BlockSpec((tm, d), lambda i, k: (i, 0)),
                  pl.BlockSpec((1, 1, d), lambda i, k: (row(i) * 6 + 3, 0, 0)),
                  pl.BlockSpec((1, 1, d), lambda i, k: (row(i) * 6 + 4, 0, 0)),
                  pl.BlockSpec((1, 1, d), lambda i, k: (row(i) * 6 + 5, 0, 0)),
                  pl.BlockSpec(memory_space=pl.ANY),
                  pl.BlockSpec(memory_space=pl.ANY),
                  pl.BlockSpec((1, d), lambda i, k: (0, 0)),
                  pl.BlockSpec((1, d), lambda i, k: (0, 0))],
        out_specs=pl.BlockSpec((tm, d), lambda i, k: (i, 0)),
        out_shape=jax.ShapeDtypeStruct((m, d), F32),
        scratch_shapes=[pltpu.VMEM((tm, d), BF16), pltpu.VMEM((MLP_WBUF, d, th), BF16),
                        pltpu.VMEM((MLP_WBUF, th, d), BF16), pltpu.SemaphoreType.DMA((2, MLP_WBUF))],
        compiler_params=pltpu.CompilerParams(dimension_semantics=("arbitrary", "arbitrary"),
                                             vmem_limit_bytes=MLP_VMEM_LIMIT),
        name="mlp_ln",
    )(x_all, ada_r, ada_r, ada_r, w1, w2, lw.reshape(1, d), lb.reshape(1, d))


def _gdn_conv_kernel(seq, ctx_len, xl_ref, xc_ref, w_ref, o_ref, pad_ref):
    j = pl.program_id(1)
    w = w_ref[...]
    qscale = jnp.where(j < N_HEADS, HEAD_DIM ** -0.5, 1.0).astype(F32)
    p0 = 8 - GDN_CONV // 2
    for x_ref, nrows, o0 in ((xl_ref, seq, 0), (xc_ref, ctx_len, seq)):
        pad_ref[0:8, :] = jnp.zeros((8, LANE), F32)
        pad_ref[nrows + 8:nrows + 16, :] = jnp.zeros((8, LANE), F32)
        pad_ref[8:nrows + 8, :] = x_ref[...]
        rb = min(nrows, 256)
        for r0 in range(0, nrows, rb):
            y = pad_ref[r0 + p0:r0 + p0 + rb, :] * w[0:1, :]
            for i in range(1, GDN_CONV):
                y = y + pad_ref[r0 + p0 + i:r0 + p0 + i + rb, :] * w[i:i + 1, :]
            y = _silu(y)
            nrm = y * lax.rsqrt(jnp.sum(y * y, axis=-1, keepdims=True) + RMS_EPS) * qscale
            o_ref[o0 + r0:o0 + r0 + rb, :] = jnp.where(j < 2 * N_HEADS, nrm, y)


def _gdn_conv(z, conv_w, nb, seq, ctx_len):
    nblk = 3 * N_HEADS
    cblk0 = nb * seq // ctx_len
    return pl.pallas_call(
        functools.partial(_gdn_conv_kernel, seq, ctx_len),
        grid=(nb, nblk),
        in_specs=[pl.BlockSpec((seq, LANE), lambda b, j: (b, CB_GQKV + j)),
                  pl.BlockSpec((ctx_len, LANE), lambda b, j: (cblk0 + b, CB_GQKV + j)),
                  pl.BlockSpec((GDN_CONV, LANE), lambda b, j: (0, j))],
        out_specs=pl.BlockSpec((seq + ctx_len, LANE), lambda b, j: (b, j)),
        out_shape=jax.ShapeDtypeStruct((nb * (seq + ctx_len), nblk * LANE), F32),
        scratch_shapes=[pltpu.VMEM((seq + 16, LANE), F32)],
        compiler_params=_cp("parallel", "parallel"),
        name="gdn_conv",
    )(z, z, conv_w)


def _gdn_gates_kernel(tm, s_ref, alog_ref, dtb_ref, o_ref):
    s = s_ref[...]
    g = -jnp.exp(alog_ref[...]) * (jnp.maximum(s + dtb_ref[...], 0.0)
                                    + jnp.log1p(jnp.exp(-jnp.abs(s + dtb_ref[...]))))
    r = lax.broadcasted_iota(jnp.int32, (tm, tm), 0)
    c = lax.broadcasted_iota(jnp.int32, (tm, tm), 1)
    same = (r >> CHUNK_SHIFT) == (c >> CHUNK_SHIFT)
    lo = jnp.where(same & (c <= r), 1.0, 0.0).astype(F32)
    up = jnp.where(same & (c >= r), 1.0, 0.0).astype(F32)
    cum_f = _dot(lo, g, precision=HIGHEST)
    cum_b = _dot(up, g, precision=HIGHEST)
    col = lax.broadcasted_iota(jnp.int32, s.shape, 1)
    o_ref[...] = jnp.where(col < N_HEADS, cum_f,
                           jnp.where(col < 2 * N_HEADS, cum_b,
                                     jnp.where(col < 4 * N_HEADS, jax.nn.sigmoid(s), 0.0)))


def _gdn_gates(z, a_log, dt_bias, tm=512):
    m = z.shape[0]
    pad = lambda v: jnp.zeros((1, LANE), F32).at[0, :2 * N_HEADS].set(v.reshape(-1).astype(F32))
    return pl.pallas_call(
        functools.partial(_gdn_gates_kernel, tm),
        grid=(m // tm,),
        in_specs=[pl.BlockSpec((tm, LANE), lambda i: (i, CB_GAB)),
                  pl.BlockSpec((1, LANE), lambda i: (0, 0)),
                  pl.BlockSpec((1, LANE), lambda i: (0, 0))],
        out_specs=pl.BlockSpec((tm, LANE), lambda i: (i, 0)),
        out_shape=jax.ShapeDtypeStruct((m, LANE), F32),
        compiler_params=_cp("parallel"),
        name="gdn_gates",
    )(z, pad(a_log), pad(dt_bias))


def _tri_masks(n):
    r = lax.broadcasted_iota(jnp.int32, (n, n), 0)
    c = lax.broadcasted_iota(jnp.int32, (n, n), 1)
    return r, c


def _split(x):
    hi = x.astype(BF16)
    return hi, (x - hi.astype(F32)).astype(BF16)


def _dot3(a, b):
    return _dot(a[0], b[0]) + (_dot(a[0], b[1]) + _dot(a[1], b[0]))


def _gdn_prep_kernel(qkv_ref, p_ref, u_ref, w_ref, qt_ref, kt_ref, att_ref, el_ref):
    C, H = CHUNK, N_HEADS
    lane = lax.broadcasted_iota(jnp.int32, (C, LANE), 1)
    r, c = _tri_masks(C)
    eye = jnp.where(r == c, 1.0, 0.0).astype(F32)
    a_list, rhs_list, where_list = [], [], []
    for n in range(PREP_CHUNKS):
        rs = slice(n * C, (n + 1) * C)
        pblk = p_ref[rs, :]
        tblk = pblk.T
        col = lambda idx, pblk=pblk: jnp.sum(jnp.where(lane == idx, pblk, 0.0), axis=-1, keepdims=True)
        row = lambda idx, tblk=tblk: tblk[idx:idx + 1, :]
        for h in range(H):
            hs = slice(h * LANE, (h + 1) * LANE)
            q = qkv_ref[rs, h * LANE:(h + 1) * LANE]
            k = qkv_ref[rs, (H + h) * LANE:(H + h + 1) * LANE]
            v = qkv_ref[rs, (2 * H + h) * LANE:(2 * H + h + 1) * LANE]
            qbf, kbf = q.astype(BF16), k.astype(BF16)
            for d in range(2):
                idx = d * H + h
                cum_c, cum_r, beta_c = col(idx), row(idx), col(2 * H + idx)
                incl = (c >= r) if d else (c <= r)
                strict = (c > r) if d else (c < r)
                last = cum_r[:, 0:1] if d else cum_r[:, C - 1:C]
                decay = jnp.exp(jnp.where(incl, cum_c - cum_r, NEG))
                kb = k * beta_c
                ec = jnp.exp(cum_c)
                a_list.append(jnp.where(strict, _dot_nt(kb.astype(BF16), kbf) * decay, 0.0))
                rhs_list.append(jnp.concatenate([v * beta_c, kb * ec], axis=-1).astype(BF16))
                where_list.append((d, rs, hs))
                att_ref[d, h, rs, :] = jnp.where(incl, _dot_nt(qbf, kbf) * decay, 0.0).astype(BF16)
                qt_ref[d, rs, hs] = (q * ec).astype(BF16)
                kt_ref[d, rs, hs] = (k * jnp.exp(last - cum_c)).astype(BF16)
                el_ref[n, idx:idx + 1, :] = jnp.broadcast_to(jnp.exp(last), (1, LANE))
    ts = [eye - a for a in a_list]
    ps = [a.astype(BF16) for a in a_list]
    for _ in range(5):
        ps = [_dot(p, p).astype(BF16) for p in ps]
        ts = [t + _dot(t.astype(BF16), p) for t, p in zip(ts, ps)]
    res = [eye - t - _dot3(_split(a), _split(t)) for a, t in zip(a_list, ts)]
    ts = [t + _dot(t.astype(BF16), e.astype(BF16)) for t, e in zip(ts, res)]
    sols = [_dot(t.astype(BF16), rhs) for t, rhs in zip(ts, rhs_list)]
    for sol, (d, rs, hs) in zip(sols, where_list):
        u_ref[d, rs, hs] = sol[:, :HEAD_DIM]
        w_ref[d, rs, hs] = sol[:, HEAD_DIM:].astype(BF16)


def _gdn_prep(qkv, p, nb, seq, ctx_len):
    m = qkv.shape[0]
    H = N_HEADS
    gw = GROUP_WIDTH
    rows = PREP_CHUNKS * CHUNK
    assert seq % rows == 0 and ctx_len % rows == 0
    dspec = pl.BlockSpec((2, rows, gw), lambda i: (0, i, 0))
    nl, nc = seq // rows, ctx_len // rows

    def qkv_blk(i):
        ic = i - nb * nl
        return jnp.where(i < nb * nl, (i // nl) * (nl + nc) + i % nl, (ic // nc) * (nl + nc) + nl + ic % nc)

    return pl.pallas_call(
        _gdn_prep_kernel,
        grid=(m // rows,),
        in_specs=[pl.BlockSpec((rows, 3 * gw), lambda i: (qkv_blk(i), 0)),
                  pl.BlockSpec((rows, LANE), lambda i: (i, 0))],
        out_specs=[dspec, dspec, dspec, dspec,
                   pl.BlockSpec((2, H, rows, CHUNK), lambda i: (0, 0, i, 0)),
                   pl.BlockSpec((PREP_CHUNKS, 2 * H, LANE), lambda i: (i, 0, 0))],
        out_shape=[jax.ShapeDtypeStruct((2, m, gw), F32),
                   jax.ShapeDtypeStruct((2, m, gw), BF16),
                   jax.ShapeDtypeStruct((2, m, gw), BF16),
                   jax.ShapeDtypeStruct((2, m, gw), BF16),
                   jax.ShapeDtypeStruct((2, H, m, CHUNK), BF16),
                   jax.ShapeDtypeStruct((m // CHUNK, 2 * H, LANE), F32)],
        compiler_params=_cp("parallel"),
        name="gdn_prep",
    )(qkv, p)


def _gdn_scan_a(ins, s_ref, g, j):
    H = N_HEADS
    mid = []
    for d in range(2):
        u_ref, w_ref, qt_ref = ins[d:6:2]
        cj = g - 1 - j if d else j
        rs = pl.ds(pl.multiple_of(cj * CHUNK, CHUNK), CHUNK)
        for h in range(H):
            hs = slice(h * LANE, (h + 1) * LANE)
            sb = s_ref[d * H + h].astype(BF16)
            vnb = (u_ref[0, rs, hs] - _dot(w_ref[0, rs, hs], sb)).astype(BF16)
            mid.append((vnb, _dot(qt_ref[0, rs, hs], sb)))
    return mid


def _gdn_scan_b(ins, s_ref, o_refs, g, j, mid):
    H = N_HEADS
    for d in range(2):
        kt_ref, att_ref, el_ref = ins[6 + d::2]
        cj = g - 1 - j if d else j
        rs = pl.ds(pl.multiple_of(cj * CHUNK, CHUNK), CHUNK)
        for h in range(H):
            hs = slice(h * LANE, (h + 1) * LANE)
            idx = d * H + h
            vnb, o_state = mid[idx]
            s_ref[idx] = s_ref[idx] * el_ref[cj, idx:idx + 1, :] + _dot_tn(kt_ref[0, rs, hs], vnb)
            o_refs[d][rs, hs] = o_state + _dot(att_ref[0, h, rs, :], vnb)


def _scans_kernel(g, *refs):
    gdn_ins = refs[:12]
    qf_ref, ff_ref, if_ref, qb_ref, fb_ref, ib_ref, lb_ref = refs[12:19]
    gof_ref, gob_ref, of_ref, ob_ref, gs_ref, s_ref = refs[19:]
    C, H, nsub = CHUNK, N_HEADS, CHUNK // SUB

    @pl.when(pl.program_id(1) == 0)
    def _():
        s_ref[...] = jnp.zeros_like(s_ref)
        gs_ref[...] = jnp.zeros_like(gs_ref)

    r, c = _tri_masks(C)
    tri = (jnp.where(c <= r, 1.0, 0.0).astype(F32), jnp.where(c >= r, 1.0, 0.0).astype(F32))
    nbig = C // BIG
    same_big = (r >> (BIG.bit_length() - 1)) == (c >> (BIG.bit_length() - 1))
    trow = lax.broadcasted_iota(jnp.int32, (C, HEAD_DIM), 0)
    srow = lax.broadcasted_iota(jnp.int32, (SUB, C), 0)
    scol = lax.broadcasted_iota(jnp.int32, (SUB, C), 1)
    chains = [(d, h) for d in range(2) for h in range(H)]
    srcs = ((qf_ref, ff_ref, if_ref, of_ref), (qb_ref, fb_ref, ib_ref, ob_ref))

    def body(j, carry):
        ph1 = []
        for d, h in chains:
            cj = g - 1 - j if d else j
            rs, hs = pl.ds(pl.multiple_of(cj * C, C), C), slice(h * LANE, (h + 1) * LANE)
            lb = lb_ref[:, hs]
            f = lb + (1.0 - lb) * jax.nn.sigmoid(srcs[d][1][rs, hs])
            ph1.append((rs, hs, f, _dot(tri[d], jnp.log(f), precision=HIGHEST)))
        gdn_mid = _gdn_scan_a(gdn_ins, gs_ref, g, j)
        ph2 = []
        for (d, h), (rs, hs, f, cum) in zip(chains, ph1):
            q_ref, _, i_ref, _ = srcs[d]
            q = _silu(q_ref[rs, hs]) * HEAD_DIM ** -0.5
            k = 1.0 - f
            vb = i_ref[rs, hs].astype(BF16)
            last = cum[0:1, :] if d else cum[C - 1:C, :]
            idx = d * H + h
            St = s_ref[idx]
            o_state = _dot_nt((q * jnp.exp(cum)).astype(BF16), St.astype(BF16))
            s_ref[idx] = St * jnp.exp(last) + _dot_tn(vb, (k * jnp.exp(last - cum)).astype(BF16))
            lvl1 = []
            for a2 in range(nbig):
                sa = slice(a2 * BIG, (a2 + 1) * BIG)
                if d and a2 < nbig - 1:
                    cb = cum[(a2 + 1) * BIG:(a2 + 1) * BIG + 1, :]
                    kt = k * jnp.exp(jnp.where(trow >= (a2 + 1) * BIG, cb - cum, NEG))
                elif (not d) and a2 > 0:
                    cb = cum[a2 * BIG - 1:a2 * BIG, :]
                    kt = k * jnp.exp(jnp.where(trow < a2 * BIG, cb - cum, NEG))
                else:
                    lvl1.append(None)
                    continue
                lvl1.append(_dot_nt((q[sa] * jnp.exp(cum[sa] - cb)).astype(BF16), kt.astype(BF16)))
            mid_row = SUB if d else SUB - 1
            cb2 = jnp.concatenate([jnp.broadcast_to(cum[a2 * BIG + mid_row:a2 * BIG + mid_row + 1, :], (BIG, HEAD_DIM))
                                   for a2 in range(nbig)], axis=0)
            early = ((trow & (BIG - 1)) >= SUB) if d else ((trow & (BIG - 1)) < SUB)
            qt2 = q * jnp.exp(jnp.where(early, NEG, cum - cb2))
            kt2 = k * jnp.exp(jnp.where(early, cb2 - cum, NEG))
            sc2 = jnp.where(same_big, _dot_nt(qt2.astype(BF16), kt2.astype(BF16)), 0.0)
            inter = []
            for a in range(nsub):
                blk = sc2[a * SUB:(a + 1) * SUB]
                if lvl1[a // 2] is not None:
                    blk = blk + lvl1[a // 2][(a % 2) * SUB:(a % 2 + 1) * SUB]
                inter.append(blk)
            ph2.append((q, k, cum, vb, o_state, inter))
        _gdn_scan_b(gdn_ins, gs_ref, (gof_ref, gob_ref), g, j, gdn_mid)
        for (d, h), (rs, hs, _, _), (q, k, cum, vb, o_state, inter) in zip(chains, ph1, ph2):
            blocks = []
            for a in range(nsub):
                sa = slice(a * SUB, (a + 1) * SUB)
                qa, ka, ca, sc = q[sa], k[sa], cum[sa], inter[a]
                for t in range(SUB):
                    dec = jnp.exp(ca - ca[t:t + 1, :])
                    st = jnp.sum(qa * ka[t:t + 1, :] * dec, axis=-1, keepdims=True)
                    ok = (srow <= t) if d else (srow >= t)
                    sc = jnp.where((scol == a * SUB + t) & ok, st, sc)
                blocks.append(sc)
            scores = jnp.concatenate(blocks, axis=0)
            srcs[d][3][rs, hs] = o_state + _dot(scores.astype(BF16), vb)
        return carry

    lax.fori_loop(0, g, body, 0)


def _scans(u, w, qt, kt, att, el, z, lbs, nb, seq, ctx_len):
    m = z.shape[0]
    H = N_HEADS
    gw = GROUP_WIDTH
    g = ctx_len // CHUNK
    nblk = seq // ctx_len
    cblk0 = nb * nblk
    blk_f = lambda b, t: jnp.where(t == 0, cblk0 + b, b * nblk + t - 1)
    blk_b = lambda b, t: jnp.where(t == 0, cblk0 + b, b * nblk + nblk - t)
    in_specs, args = [], []
    for arr in (u, w, qt, kt):
        in_specs += [pl.BlockSpec((1, ctx_len, gw), lambda b, t: (0, blk_f(b, t), 0)),
                     pl.BlockSpec((1, ctx_len, gw), lambda b, t: (1, blk_b(b, t), 0))]
        args += [arr, arr]
    in_specs += [pl.BlockSpec((1, H, ctx_len, CHUNK), lambda b, t: (0, 0, blk_f(b, t), 0)),
                 pl.BlockSpec((1, H, ctx_len, CHUNK), lambda b, t: (1, 0, blk_b(b, t), 0)),
                 pl.BlockSpec((g, 2 * H, LANE), lambda b, t: (blk_f(b, t), 0, 0)),
                 pl.BlockSpec((g, 2 * H, LANE), lambda b, t: (blk_b(b, t), 0, 0))]
    args += [att, att, el, el]
    col = lambda cb: cb * LANE // gw
    zspec = lambda blk, cb: pl.BlockSpec((ctx_len, gw), lambda b, t: (blk(b, t), col(cb)))
    in_specs += [zspec(blk_f, CB_HQ), zspec(blk_f, CB_HFF), zspec(blk_f, CB_HI),
                 zspec(blk_b, CB_HQ), zspec(blk_b, CB_HFB), zspec(blk_b, CB_HI),
                 pl.BlockSpec((1, gw), lambda b, t: (0, 0))]
    args += [z, z, z, z, z, z, lbs.reshape(1, gw)]
    ospec_f = pl.BlockSpec((ctx_len, gw), lambda b, t: (blk_f(b, t), 0))
    ospec_b = pl.BlockSpec((ctx_len, gw), lambda b, t: (blk_b(b, t), 0))
    state = pltpu.VMEM((2 * H, HEAD_DIM, HEAD_DIM), F32)
    outs = pl.pallas_call(
        functools.partial(_scans_kernel, g),
        grid=(nb, nblk + 1),
        in_specs=in_specs,
        out_specs=[ospec_f, ospec_b, ospec_f, ospec_b],
        out_shape=[jax.ShapeDtypeStruct((m, gw), F32)] * 4,
        scratch_shapes=[state, state],
        compiler_params=_cp("parallel", "arbitrary"),
        name="scans",
    )(*args)
    return outs[:2], outs[2:]


def _softmax_pv(parts):
    m = parts[0][0].max(axis=-1, keepdims=True)
    for s, _ in parts[1:]:
        m = jnp.maximum(m, s.max(axis=-1, keepdims=True))
    den, acc = None, None
    for s, v in parts:
        p = jnp.exp(s - m)
        d = jnp.sum(p, axis=-1, keepdims=True)
        a = _dot(p.astype(BF16), v)
        den = d if den is None else den + d
        acc = a if acc is None else acc + a
    return acc / den


def _na_kernel(emit_ctx, rows, *refs):
    q_ref, k_ref, v_ref, qc_ref, kc_ref, vc_ref, bias_ref = refs[:7]
    if emit_ctx:
        yl_ref, yc_ref, kb_ref, vb_ref = refs[7:]
    else:
        yl_ref, kb_ref, vb_ref = refs[7:]
    scale = HEAD_DIM ** -0.5
    win = NA_ROWS * GRID_W
    kb_ref[...] = k_ref[...].astype(BF16)
    vb_ref[...] = v_ref[...].astype(BF16)
    kc = kc_ref[...].astype(BF16)
    vc = vc_ref[...].astype(BF16)

    def body(i, carry):
        pre = []
        for t in range(NA_UNROLL):
            r = i * NA_UNROLL + t
            row0 = jnp.clip(r - NA_ROWS // 2, 0, rows - NA_ROWS)
            qs = pl.ds(pl.multiple_of(r * GRID_W, GRID_W), GRID_W)
            ks = pl.ds(pl.multiple_of(row0 * GRID_W, GRID_W), win)
            q = q_ref[qs, :].astype(BF16)
            pre.append((qs, ks, _dot_nt(q, kb_ref[ks, :]) * scale + bias_ref[0, r - row0], _dot_nt(q, kc) * scale))
        mid = []
        for qs, ks, s_win, s_ctx in pre:
            m = jnp.maximum(s_win.max(axis=-1, keepdims=True), s_ctx.max(axis=-1, keepdims=True))
            p_win, p_ctx = jnp.exp(s_win - m), jnp.exp(s_ctx - m)
            den = jnp.sum(p_win, axis=-1, keepdims=True) + jnp.sum(p_ctx, axis=-1, keepdims=True)
            mid.append((qs, ks, p_win.astype(BF16), p_ctx.astype(BF16), den))
        for qs, ks, p_win, p_ctx, den in mid:
            yl_ref[qs, :] = ((_dot(p_win, vb_ref[ks, :]) + _dot(p_ctx, vc)) / den).astype(yl_ref.dtype)
        return carry

    lax.fori_loop(0, rows // NA_UNROLL, body, 0)
    if emit_ctx:
        s = _dot_nt(qc_ref[...].astype(BF16), kc) * scale
        yc_ref[...] = _softmax_pv([(s, vc)]).astype(yc_ref.dtype)


def _na_bias_kernel(rpb_ref, o_ref):
    n = lax.broadcasted_iota(jnp.int32, (LANE, GRID_W * GRID_W), 1)
    j = lax.broadcasted_iota(jnp.int32, (LANE, GRID_W * GRID_W), 0)
    q, w = n >> 6, n & (GRID_W - 1)
    dc = jnp.clip(w - q, 1 - NA_COLS, NA_COLS - 1) + NA_COLS - 1
    onehot = jnp.where(dc == j, 1.0, 0.0).astype(F32)
    m = _dot(rpb_ref[...], onehot, precision=HIGHEST)
    c0 = jnp.clip(q[0:1] - NA_COLS // 2, 0, GRID_W - NA_COLS)
    ok = (w[0:1] >= c0) & (w[0:1] < c0 + NA_COLS)
    o_ref[...] = jnp.where(ok, m, NEG)


def _na_bias_tables(rpb):
    depth, H, nr, nc = rpb.shape
    assert GRID_W == 64 and depth * H * nr <= LANE and nc <= LANE
    flat = jnp.zeros((LANE, LANE), F32).at[:depth * H * nr, :nc].set(rpb.reshape(-1, nc).astype(F32))
    m = pl.pallas_call(
        _na_bias_kernel,
        out_shape=jax.ShapeDtypeStruct((LANE, GRID_W * GRID_W), F32),
        compiler_params=pltpu.CompilerParams(vmem_limit_bytes=VMEM_LIMIT),
        name="na_bias",
    )(flat)
    m = m[:depth * H * nr].reshape(depth, H, nr, GRID_W, GRID_W)
    tab = jnp.stack([jnp.stack([m[:, :, k - s + NA_ROWS - 1] for k in range(NA_ROWS)], axis=3)
                     for s in range(NA_ROWS)], axis=2)
    return tab.reshape(depth, H, NA_ROWS, GRID_W, NA_ROWS * GRID_W)


def _na(z, bias, nb, seq, ctx_len, emit_ctx):
    rows = seq // GRID_W
    assert seq % GRID_W == 0 and rows >= NA_ROWS and rows % NA_UNROLL == 0
    cblk0 = nb * seq // ctx_len
    H = N_HEADS
    lat = lambda cb: pl.BlockSpec((seq, LANE), lambda b, h: (b, cb + h))
    ctx = lambda cb: pl.BlockSpec((ctx_len, LANE), lambda b, h: (cblk0 + b, cb + h))
    win = NA_ROWS * GRID_W
    in_specs = [lat(CB_NAQ), lat(CB_NAK), lat(CB_NAV), ctx(CB_NAQ), ctx(CB_NAK), ctx(CB_NAV),
                pl.BlockSpec((1, NA_ROWS, GRID_W, win), lambda b, h: (h, 0, 0, 0))]
    out_specs = [pl.BlockSpec((seq, LANE), lambda b, h: (b, h))]
    out_shape = [jax.ShapeDtypeStruct((nb * seq, GROUP_WIDTH), BF16)]
    if emit_ctx:
        out_specs.append(pl.BlockSpec((ctx_len, LANE), lambda b, h: (b, h)))
        out_shape.append(jax.ShapeDtypeStruct((nb * ctx_len, GROUP_WIDTH), BF16))
    return pl.pallas_call(
        functools.partial(_na_kernel, emit_ctx, rows),
        grid=(nb, H),
        in_specs=in_specs, out_specs=out_specs, out_shape=out_shape,
        scratch_shapes=[pltpu.VMEM((seq, LANE), BF16), pltpu.VMEM((seq, LANE), BF16)],
        compiler_params=_cp("parallel", "parallel"),
        name="na_attn",
    )(z, z, z, z, z, z, bias)


def _rms(x, w):
    return x * lax.rsqrt(jnp.mean(x * x, axis=-1, keepdims=True) + RMS_EPS) * w


def _mla_prep_kernel(cq_ref, ckv_ref, kra_ref, krb_ref, cc_ref, ss_ref, qnw_ref, kvnw_ref, wuq_ref, wukv_ref,
                     q_ref, kn_ref, kr_ref, v_ref):
    cc, ss = cc_ref[...], ss_ref[...]
    qn = _rms(cq_ref[...], qnw_ref[...]).astype(BF16)
    qa = _dot(qn, wuq_ref[...])
    for h in range(N_HEADS):
        b = 3 * LANE * h
        q_ref[:, 2 * LANE * h:2 * LANE * h + LANE] = qa[:, b:b + LANE].astype(BF16)
        q_ref[:, 2 * LANE * h + LANE:2 * LANE * (h + 1)] = (
            qa[:, b + LANE:b + 2 * LANE] * cc + qa[:, b + 2 * LANE:b + 3 * LANE] * ss).astype(BF16)
    kvn = _rms(ckv_ref[...], kvnw_ref[...]).astype(BF16)
    kv = _dot(kvn, wukv_ref[...])
    kn_ref[...] = kv[:, :GROUP_WIDTH].astype(BF16)
    v_ref[...] = kv[:, GROUP_WIDTH:].astype(BF16)
    kr_ref[...] = (kra_ref[...] * cc + krb_ref[...] * ss).astype(BF16)


def _mla_prep(z, cc, ss, qnw, kvnw, wuq, wukv, n_lat, seq, tm=512):
    m = z.shape[0]
    nlt, spt = n_lat // tm, seq // tm
    tab = lambda i: jnp.where(i < nlt, i % spt, spt)
    H = N_HEADS
    return pl.pallas_call(
        _mla_prep_kernel,
        grid=(m // tm,),
        in_specs=[pl.BlockSpec((tm, MLA_Q_RANK), lambda i: (i, CB_MQ * LANE // MLA_Q_RANK)),
                  pl.BlockSpec((tm, MLA_KV_RANK), lambda i: (i, CB_MKV * LANE // MLA_KV_RANK)),
                  pl.BlockSpec((tm, LANE), lambda i: (i, CB_MKRA)),
                  pl.BlockSpec((tm, LANE), lambda i: (i, CB_MKRB)),
                  pl.BlockSpec((tm, LANE), lambda i: (tab(i), 0)),
                  pl.BlockSpec((tm, LANE), lambda i: (tab(i), 0)),
                  pl.BlockSpec((1, MLA_Q_RANK), lambda i: (0, 0)),
                  pl.BlockSpec((1, MLA_KV_RANK), lambda i: (0, 0)),
                  pl.BlockSpec(wuq.shape, lambda i: (0, 0)),
                  pl.BlockSpec(wukv.shape, lambda i: (0, 0))],
        out_specs=[pl.BlockSpec((tm, 2 * LANE * H), lambda i: (i, 0)),
                   pl.BlockSpec((tm, GROUP_WIDTH), lambda i: (i, 0)),
                   pl.BlockSpec((tm, LANE), lambda i: (i, 0)),
                   pl.BlockSpec((tm, GROUP_WIDTH), lambda i: (i, 0))],
        out_shape=[jax.ShapeDtypeStruct((m, 2 * LANE * H), BF16),
                   jax.ShapeDtypeStruct((m, GROUP_WIDTH), BF16),
                   jax.ShapeDtypeStruct((m, LANE), BF16),
                   jax.ShapeDtypeStruct((m, GROUP_WIDTH), BF16)],
        compiler_params=_cp("parallel"),
        name="mla_prep",
    )(z, z, z, z, cc, ss, qnw.reshape(1, -1), kvnw.reshape(1, -1), wuq, wukv)


def _mla_attn_kernel(with_lat, seq, *refs):
    if with_lat:
        q_ref, knl_ref, krl_ref, vl_ref, knc_ref, krc_ref, vc_ref, y_ref, k_scr = refs
    else:
        q_ref, knc_ref, krc_ref, vc_ref, y_ref, k_scr = refs
    scale = MLA_QK_DIM ** -0.5
    nk = k_scr.shape[0]

    @pl.when(pl.program_id(2) == 0)
    def _():
        if with_lat:
            k_scr[0:seq, 0:LANE] = knl_ref[...]
            k_scr[0:seq, LANE:2 * LANE] = krl_ref[...]
        k_scr[nk - knc_ref.shape[0]:nk, 0:LANE] = knc_ref[...]
        k_scr[nk - knc_ref.shape[0]:nk, LANE:2 * LANE] = krc_ref[...]

    tq = q_ref.shape[0]
    qsub = min(tq, MLA_QSUB)
    nsub = tq // qsub
    vals = ([vl_ref] if with_lat else []) + [vc_ref]

    def qk(s):
        q = q_ref[s * qsub:(s + 1) * qsub, :]
        out = [_dot_nt(q, k_scr[0:seq, :])] if with_lat else []
        return out + [_dot_nt(q, k_scr[nk - knc_ref.shape[0]:nk, :])]

    def softmax(raw):
        m = raw[0].max(axis=-1, keepdims=True)
        for s in raw[1:]:
            m = jnp.maximum(m, s.max(axis=-1, keepdims=True))
        ps = [jnp.exp2((s - m) * (scale * LOG2E)) for s in raw]
        den = sum(jnp.sum(p, axis=-1, keepdims=True) for p in ps)
        return [p.astype(BF16) for p in ps], den

    raw = qk(0)
    for s in range(nsub):
        nxt = qk(s + 1) if s + 1 < nsub else None
        ps, den = softmax(raw)
        acc = sum(_dot(p, v[...]) for p, v in zip(ps, vals))
        y_ref[s * qsub:(s + 1) * qsub, :] = (acc / den).astype(y_ref.dtype)
        raw = nxt


def _mla_attn(q, kn, kr, v, nb, seq, ctx_len, with_lat, tq=4096):
    H = N_HEADS
    cblk0 = nb * seq // ctx_len
    nq = seq if with_lat else ctx_len
    tq = min(tq, nq)
    qblk0 = 0 if with_lat else nb * seq // tq
    ctxs = [pl.BlockSpec((ctx_len, LANE), lambda b, h, i: (cblk0 + b, h)),
            pl.BlockSpec((ctx_len, LANE), lambda b, h, i: (cblk0 + b, 0)),
            pl.BlockSpec((ctx_len, LANE), lambda b, h, i: (cblk0 + b, h))]
    lats = [pl.BlockSpec((seq, LANE), lambda b, h, i: (b, h)),
            pl.BlockSpec((seq, LANE), lambda b, h, i: (b, 0)),
            pl.BlockSpec((seq, LANE), lambda b, h, i: (b, h))]
    in_specs = [pl.BlockSpec((tq, 2 * LANE), lambda b, h, i: (qblk0 + b * (nq // tq) + i, h))]
    args = [q]
    if with_lat:
        in_specs += lats
        args += [kn, kr, v]
    in_specs += ctxs
    args += [kn, kr, v]
    nk = (seq if with_lat else 0) + ctx_len
    return pl.pallas_call(
        functools.partial(_mla_attn_kernel, with_lat, seq),
        grid=(nb, H, nq // tq),
        in_specs=in_specs,
        out_specs=pl.BlockSpec((tq, LANE), lambda b, h, i: (b * (nq // tq) + i, h)),
        out_shape=jax.ShapeDtypeStruct((nb * nq, GROUP_WIDTH), BF16),
        scratch_shapes=[pltpu.VMEM((nk, 2 * LANE), BF16)],
        compiler_params=_cp("parallel", "parallel", "arbitrary"),
        name="mla_attn_lat" if with_lat else "mla_attn_ctx",
    )(*args)


def _prep_w_in(w_in):
    gw = GROUP_WIDTH
    o_na = 4 * gw + 4 * N_HEADS
    o_mla = o_na + 3 * gw
    o_kv = o_mla + MLA_Q_RANK
    o_kr = o_kv + MLA_KV_RANK
    o_hg = o_kr + MLA_ROPE
    depth, d, n_in = w_in.shape
    kr = w_in[..., o_kr:o_hg].astype(BF16)
    k1, k2 = kr[..., 0::2], kr[..., 1::2]
    zpad = jnp.zeros((depth, d, LANE - MLA_ROPE), BF16)
    kra, krb = jnp.concatenate([k1, k2, zpad], axis=-1), jnp.concatenate([k2, k1, zpad], axis=-1)
    moves = ((CB_MQ, o_mla, MLA_Q_RANK), (CB_MKV, o_kv, MLA_KV_RANK), (CB_GAB, 4 * gw, 4 * N_HEADS),
             (CB_GQKV, 0, 4 * gw), (CB_NAQ, o_na, 3 * gw), (CB_HQ, o_hg, 5 * gw))

    def body(w_ref, kra_ref, krb_ref, o_ref):
        for cb, src, width in moves:
            lo = src // LANE * LANE
            hi = min(-(-(src + width) // LANE) * LANE, n_in)
            piece = w_ref[:, lo:hi][:, src - lo:src - lo + width].astype(BF16)
            pad = -width % LANE
            if pad:
                piece = jnp.concatenate([piece, jnp.zeros((piece.shape[0], pad), BF16)], axis=-1)
            o_ref[:, cb * LANE:cb * LANE + width + pad] = piece
        o_ref[:, CB_MKRA * LANE:(CB_MKRA + 1) * LANE] = kra_ref[...]
        o_ref[:, CB_MKRB * LANE:(CB_MKRB + 1) * LANE] = krb_ref[...]

    tr = 256
    return pl.pallas_call(
        body,
        grid=(depth, d // tr),
        in_specs=[pl.BlockSpec((None, tr, n_in), lambda l, i: (l, i, 0)),
                  pl.BlockSpec((None, tr, LANE), lambda l, i: (l, i, 0)),
                  pl.BlockSpec((None, tr, LANE), lambda l, i: (l, i, 0))],
        out_specs=pl.BlockSpec((None, tr, NP_IN), lambda l, i: (l, i, 0)),
        out_shape=jax.ShapeDtypeStruct((depth, d, NP_IN), BF16),
        compiler_params=_cp("parallel", "parallel"),
        name="w_in_layout",
    )(w_in, kra, krb)


def _prep_w_uq(w_uq):
    r = w_uq.shape[0]
    z = jnp.zeros((r, LANE - MLA_ROPE), w_uq.dtype)
    cols = []
    for h in range(N_HEADS):
        wh = w_uq[:, h * MLA_QK_DIM:(h + 1) * MLA_QK_DIM]
        rope = wh[:, MLA_NOPE:]
        r1, r2 = rope[:, 0::2], rope[:, 1::2]
        cols += [wh[:, :MLA_NOPE], r1, r2, z, r2, r1, z]
    return jnp.concatenate(cols, axis=1).astype(BF16)


def _rope_tables(seq, tm):
    n_freq = MLA_ROPE // 4
    freqs = ROPE_BASE ** (-jnp.arange(n_freq, dtype=F32) / n_freq)
    t = jnp.arange(seq)
    ang = jnp.concatenate([(t // GRID_W).astype(F32)[:, None] * freqs,
                           (t % GRID_W).astype(F32)[:, None] * freqs], -1)
    cos, sin = jnp.cos(ang), jnp.sin(ang)
    zp = jnp.zeros((seq, LANE - MLA_ROPE), F32)
    cc = jnp.concatenate([cos, cos, zp], axis=1)
    ss = jnp.concatenate([-sin, sin, zp], axis=1)
    ident = jnp.zeros((tm, LANE), F32).at[:, :MLA_ROPE].set(1.0)
    return jnp.concatenate([cc, ident], axis=0), jnp.concatenate([ss, jnp.zeros((tm, LANE), F32)], axis=0)


def kernel(x, c, ctx, c_ctx, w_ada, b_ada, w_in, gdn_conv_w, gdn_a_log, gdn_dt_bias, gdn_norm_w, na_rpb,
           mla_q_norm_w, mla_kv_norm_w, mla_w_uq, mla_w_uk, mla_w_uv, hgrn_lower_bounds, hgrn_norm_w, w_out,
           ln1_w, ln1_b, w_mlp1, w_mlp2, ln2_w, ln2_b):
    nb, seq, d = x.shape
    ctx_len = ctx.shape[1]
    depth = w_ada.shape[0]
    n_lat, n_ctx = nb * seq, nb * ctx_len
    alpha = (2 * depth) ** 0.25
    tm = 512
    tmm = 1024 if (seq % 1024 == 0 and n_ctx % 1024 == 0) else tm
    assert nb < 8 and seq % tm == 0 and n_ctx % tm == 0 and seq % ctx_len == 0 and ctx_len % CHUNK == 0

    cin = jnp.zeros((8, d), F32).at[:nb].set(c).at[nb].set(c_ctx)
    ada = _ada(cin, w_ada, b_ada)
    p_lb = jax.nn.softmax(hgrn_lower_bounds.astype(F32), axis=0)
    lbs = jnp.cumsum(p_lb, axis=0) - p_lb[0]
    cc, ss = _rope_tables(seq, tm)
    na_bias = _na_bias_tables(na_rpb)
    w_in_b, w_out_b = _prep_w_in(w_in.astype(BF16)), w_out.astype(BF16)
    w_mlp1_b, w_mlp2_b = w_mlp1.astype(BF16), w_mlp2.astype(BF16)

    xs = (x.reshape(n_lat, d), ctx.reshape(n_ctx, d))
    for l in range(depth):
        emit_ctx = l < depth - 1
        ada_r = ada[l].reshape(8 * 6, 1, d)
        z = _inproj(xs, ada_r, w_in_b, l, n_lat + n_ctx, n_lat, seq, nb, tm=tmm)

        qkv = _gdn_conv(z, gdn_conv_w[l], nb, seq, ctx_len)
        p = _gdn_gates(z, gdn_a_log[l], gdn_dt_bias[l], tm=tm // 2)
        gdn_o, hgrn_o = _scans(*_gdn_prep(qkv, p, nb, seq, ctx_len), z, lbs[l], nb, seq, ctx_len)
        m_out = n_lat + n_ctx if emit_ctx else n_lat
        yb = _na(z, na_bias[l], nb, seq, ctx_len, emit_ctx)
        q, kn, kr, v = _mla_prep(z, cc, ss, mla_q_norm_w[l], mla_kv_norm_w[l], _prep_w_uq(mla_w_uq[l]),
                                 jnp.concatenate([mla_w_uk[l], mla_w_uv[l]], axis=1).astype(BF16),
                                 n_lat, seq, tm=tm)
        ym = [_mla_attn(q, kn, kr, v, nb, seq, ctx_len, True)]
        if emit_ctx:
            ym.append(_mla_attn(q, kn, kr, v, nb, seq, ctx_len, False))
        if len(xs) == 1 and emit_ctx:
            yb, ym = [jnp.concatenate(yb, axis=0)], [jnp.concatenate(ym, axis=0)]
        x_all = _outproj(xs if emit_ctx else xs[:1], gdn_o, yb, ym, hgrn_o, z, gdn_norm_w[l], hgrn_norm_w[l],
                         w_out_b, l, ada_r, ln1_w[l], ln1_b[l], m_out, n_lat, seq, nb, alpha,
                         tm=tm // 2 if len(xs) == 2 else tm)
        x_all = _mlp(x_all, w_mlp1_b, w_mlp2_b, l, ada_r, ln2_w[l], ln2_b[l], seq, nb, alpha, tm=tm)
        xs = (x_all,)
    return x_all[:n_lat].reshape(nb, seq, d)
```
